```python
import jax, jax.numpy as jnp
from jax import lax
import numpy as np

D_MODEL = 1024
BATCH = 2
SEQ = 8192
DEPTH = 2

N_MIXERS = 2
N_ATTN_LAYERS = (DEPTH + 1) // 2
N_GMLP_LAYERS = DEPTH // 2
HEAD_DIM = 64
N_Q_HEADS = D_MODEL // HEAD_DIM
N_KV_HEADS = 4
GQA_GROUP = N_Q_HEADS // N_KV_HEADS
WINDOW = 128
ROPE_THETA = 10000.0
Q_WIDTH = N_Q_HEADS * HEAD_DIM
KV_WIDTH = N_KV_HEADS * HEAD_DIM
QKV_WIDTH = Q_WIDTH + 2 * KV_WIDTH
CHUNK = 128
GMLP_WIDTH = 2 * D_MODEL
N_SGU_GROUPS = 8
SGU_GROUP_DIM = GMLP_WIDTH // N_SGU_GROUPS
N_EXPERT_GROUPS = 4
EXPERTS_PER_GROUP = 8
TOP_K_INNER = 2
EXPERT_FF = D_MODEL // 4
N_MOD = 6
DEEPNORM_ALPHA = (2.0 * DEPTH) ** 0.25
DEEPNORM_BETA = (8.0 * DEPTH) ** -0.25
LN_EPS = 1e-5

kernel_name = "hybrid_swa_sink_gmlp_hmoe_deepnorm_adaln"


def layer_norm(x, g, b):
    xf = x.astype(jnp.float32)
    mu = jnp.mean(xf, axis=-1, keepdims=True)
    var = jnp.mean(jnp.square(xf - mu), axis=-1, keepdims=True)
    y = (xf - mu) * lax.rsqrt(var + LN_EPS)
    return (y * g.astype(jnp.float32) + b.astype(jnp.float32)).astype(x.dtype)


def rope(x, cos, sin):
    xf = x.astype(jnp.float32)
    x1, x2 = jnp.split(xf, 2, axis=-1)
    return jnp.concatenate([x1 * cos - x2 * sin, x2 * cos + x1 * sin], axis=-1).astype(x.dtype)


def sliding_window_sink_attention(h, cos, sin, w_qkv, b_qkv, sinks, w_o, b_o):
    B, S, _ = h.shape
    nb = S // WINDOW
    qkv = h @ w_qkv + b_qkv
    q, k, v = jnp.split(qkv, [Q_WIDTH, Q_WIDTH + KV_WIDTH], axis=-1)
    q = rope(q.reshape(B, S, N_Q_HEADS, HEAD_DIM), cos, sin)
    k = rope(k.reshape(B, S, N_KV_HEADS, HEAD_DIM), cos, sin)
    v = v.reshape(B, S, N_KV_HEADS, HEAD_DIM)
    qb = q.reshape(B, nb, WINDOW, N_KV_HEADS, GQA_GROUP, HEAD_DIM)

    def band(t):
        tb = t.reshape(B, nb, WINDOW, N_KV_HEADS, HEAD_DIM)
        prev = jnp.pad(tb, ((0, 0), (1, 0), (0, 0), (0, 0), (0, 0)))[:, :-1]
        return jnp.concatenate([prev, tb], axis=2)

    kb, vb = band(k), band(v)
    scores = jnp.einsum('bnqkgd,bnskd->bnkgqs', qb, kb,
                        preferred_element_type=jnp.float32) * (HEAD_DIM ** -0.5)
    qi = jnp.arange(WINDOW)[:, None] + WINDOW
    kj = jnp.arange(2 * WINDOW)[None, :]
    in_band = (kj <= qi) & (kj > qi - WINDOW)
    blk_valid = (jnp.arange(nb)[:, None, None] > 0) | (kj[None] >= WINDOW)
    mask = in_band[None] & blk_valid
    scores = jnp.where(mask[None, :, None, None], scores, -jnp.inf)
    sink = sinks.astype(jnp.float32).reshape(1, 1, N_KV_HEADS, GQA_GROUP, 1, 1)
    m = jnp.maximum(jnp.max(scores, axis=-1, keepdims=True), sink)
    p = jnp.exp(scores - m)
    denom = jnp.sum(p, axis=-1, keepdims=True) + jnp.exp(sink - m)
    probs = (p / denom).astype(v.dtype)
    o = jnp.einsum('bnkgqs,bnskd->bnqkgd', probs, vb)
    return o.reshape(B, S, Q_WIDTH) @ w_o + b_o


def chunked_spatial_gating(h, w_in, b_in, sgu_ln_g, sgu_ln_b, w_s, b_s, w_out, b_out):
    B, S, _ = h.shape
    nc = S // CHUNK
    z = jax.nn.gelu(h @ w_in + b_in, approximate=False)
    u, v = jnp.split(z, 2, axis=-1)
    v = layer_norm(v, sgu_ln_g, sgu_ln_b)
    v = v.reshape(B, nc, CHUNK, N_SGU_GROUPS, SGU_GROUP_DIM)
    causal = jnp.tril(jnp.ones((CHUNK, CHUNK), dtype=bool))
    ws = jnp.where(causal[None], w_s, 0).astype(v.dtype)
    mixed = jnp.einsum('gts,bnsgc->bntgc', ws, v) + b_s.T[None, None, :, :, None]
    out = u * mixed.reshape(B, S, GMLP_WIDTH)
    return out @ w_out + b_out


def hierarchical_moe(h, w_group_router, b_group_router, w_expert_router, b_expert_router,
                     w_gate_up, w_down):
    B, S, D = h.shape
    t = h.reshape(B * S, D)
    group_logits = (t @ w_group_router + b_group_router).astype(jnp.float32)
    group_probs = jax.nn.softmax(group_logits, axis=-1)
    g_p, g_idx = lax.top_k(group_probs, 1)
    group_onehot = jax.nn.one_hot(g_idx[:, 0], N_EXPERT_GROUPS, dtype=jnp.float32)
    expert_logits = (t @ w_expert_router + b_expert_router).astype(jnp.float32)
    expert_logits = expert_logits.reshape(-1, N_EXPERT_GROUPS, EXPERTS_PER_GROUP)
    sel_logits = jnp.sum(expert_logits * group_onehot[:, :, None], axis=1)
    top_vals, top_idx = lax.top_k(sel_logits, TOP_K_INNER)
    top_w = jax.nn.softmax(top_vals, axis=-1) * g_p
    inner = jnp.sum(jax.nn.one_hot(top_idx, EXPERTS_PER_GROUP, dtype=jnp.float32)
                    * top_w[..., None], axis=1)
    combine = (group_onehot[:, :, None] * inner[:, None, :]).astype(h.dtype)
    y = jnp.zeros_like(t)
    for g in range(N_EXPERT_GROUPS):
        gu = jnp.einsum('td,edf->tef', t, w_gate_up[g])
        gate, up = jnp.split(gu, 2, axis=-1)
        act = jax.nn.silu(gate) * up * combine[:, g, :, None]
        y = y + jnp.einsum('tef,efd->td', act, w_down[g])
    return y.reshape(B, S, D)


def setup_inputs(seed: int = 0) -> dict:
    key = jax.random.key(seed)
    ks = list(jax.random.split(key, 32))
    D = D_MODEL

    def nrm(i, shape, scale):
        return jax.random.normal(ks[i], shape, jnp.float32) * scale

    x = nrm(0, (BATCH, SEQ, D), 1.0)
    c = nrm(1, (BATCH, D), 1.0)
    positions = (jnp.arange(SEQ, dtype=jnp.int32)[None, :]
                 + jax.random.randint(ks[2], (BATCH, 1), 0, 4096, dtype=jnp.int32))
    ada_w = nrm(3, (DEPTH, D, N_MOD * D), 0.2 * D ** -0.5)
    ada_b = nrm(4, (DEPTH, N_MOD * D), 0.02)
    post_ln_g = 1.0 + nrm(5, (DEPTH, 2, D), 0.1)
    post_ln_b = nrm(6, (DEPTH, 2, D), 0.02)
    attn_w_qkv = nrm(7, (N_ATTN_LAYERS, D, QKV_WIDTH), D ** -0.5)
    attn_b_qkv = nrm(8, (N_ATTN_LAYERS, QKV_WIDTH), 0.02)
    attn_sinks = nrm(9, (N_ATTN_LAYERS, N_Q_HEADS), 1.0)
    attn_w_o = nrm(10, (N_ATTN_LAYERS, Q_WIDTH, D), Q_WIDTH ** -0.5 * DEEPNORM_BETA)
    attn_b_o = nrm(11, (N_ATTN_LAYERS, D), 0.02)
    gmlp_w_in = nrm(12, (N_GMLP_LAYERS, D, 2 * GMLP_WIDTH), D ** -0.5)
    gmlp_b_in = nrm(13, (N_GMLP_LAYERS, 2 * GMLP_WIDTH), 0.02)
    gmlp_sgu_ln_g = 1.0 + nrm(14, (N_GMLP_LAYERS, GMLP_WIDTH), 0.1)
    gmlp_sgu_ln_b = nrm(15, (N_GMLP_LAYERS, GMLP_WIDTH), 0.02)
    gmlp_w_s = nrm(16, (N_GMLP_LAYERS, N_SGU_GROUPS, CHUNK, CHUNK), CHUNK ** -0.5)
    gmlp_b_s = 1.0 + nrm(17, (N_GMLP_LAYERS, N_SGU_GROUPS, CHUNK), 0.1)
    gmlp_w_out = nrm(18, (N_GMLP_LAYERS, GMLP_WIDTH, D), GMLP_WIDTH ** -0.5 * DEEPNORM_BETA)
    gmlp_b_out = nrm(19, (N_GMLP_LAYERS, D), 0.02)
    moe_w_group_router = nrm(20, (DEPTH, D, N_EXPERT_GROUPS), D ** -0.5)
    moe_b_group_router = nrm(21, (DEPTH, N_EXPERT_GROUPS), 0.01)
    moe_w_expert_router = nrm(22, (DEPTH, D, N_EXPERT_GROUPS * EXPERTS_PER_GROUP), D ** -0.5)
    moe_b_expert_router = nrm(23, (DEPTH, N_EXPERT_GROUPS * EXPERTS_PER_GROUP), 0.01)
    moe_w_gate_up = nrm(24, (DEPTH, N_EXPERT_GROUPS, EXPERTS_PER_GROUP, D, 2 * EXPERT_FF), D ** -0.5)
    moe_w_down = nrm(25, (DEPTH, N_EXPERT_GROUPS, EXPERTS_PER_GROUP, EXPERT_FF, D),
                     EXPERT_FF ** -0.5 * DEEPNORM_BETA)
    return {"x": x, "c": c, "positions": positions, "ada_w": ada_w, "ada_b": ada_b,
            "post_ln_g": post_ln_g, "post_ln_b": post_ln_b,
            "attn_w_qkv": attn_w_qkv, "attn_b_qkv": attn_b_qkv, "attn_sinks": attn_sinks,
            "attn_w_o": attn_w_o, "attn_b_o": attn_b_o,
            "gmlp_w_in": gmlp_w_in, "gmlp_b_in": gmlp_b_in, "gmlp_sgu_ln_g": gmlp_sgu_ln_g,
            "gmlp_sgu_ln_b": gmlp_sgu_ln_b, "gmlp_w_s": gmlp_w_s, "gmlp_b_s": gmlp_b_s,
            "gmlp_w_out": gmlp_w_out, "gmlp_b_out": gmlp_b_out,
            "moe_w_group_router": moe_w_group_router, "moe_b_group_router": moe_b_group_router,
            "moe_w_expert_router": moe_w_expert_router, "moe_b_expert_router": moe_b_expert_router,
            "moe_w_gate_up": moe_w_gate_up, "moe_w_down": moe_w_down}


def reference(x, c, positions, ada_w, ada_b, post_ln_g, post_ln_b,
              attn_w_qkv, attn_b_qkv, attn_sinks, attn_w_o, attn_b_o,
              gmlp_w_in, gmlp_b_in, gmlp_sgu_ln_g, gmlp_sgu_ln_b, gmlp_w_s, gmlp_b_s,
              gmlp_w_out, gmlp_b_out,
              moe_w_group_router, moe_b_group_router, moe_w_expert_router, moe_b_expert_router,
              moe_w_gate_up, moe_w_down):
    inv_freq = ROPE_THETA ** (-jnp.arange(0, HEAD_DIM, 2, dtype=jnp.float32) / HEAD_DIM)
    ang = positions.astype(jnp.float32)[..., None] * inv_freq
    cos = jnp.cos(ang)[:, :, None, :]
    sin = jnp.sin(ang)[:, :, None, :]
    c_act = jax.nn.silu(c)
    for i in range(DEPTH):
        mod = (c_act @ ada_w[i] + ada_b[i])[:, None, :]
        sh_m, sc_m, gt_m, sh_f, sc_f, gt_f = jnp.split(mod, N_MOD, axis=-1)
        h = x * (1 + sc_m) + sh_m
        j = i // N_MIXERS
        if i % N_MIXERS == 0:
            y = sliding_window_sink_attention(h, cos, sin, attn_w_qkv[j], attn_b_qkv[j],
                                              attn_sinks[j], attn_w_o[j], attn_b_o[j])
        else:
            y = chunked_spatial_gating(h, gmlp_w_in[j], gmlp_b_in[j], gmlp_sgu_ln_g[j],
                                       gmlp_sgu_ln_b[j], gmlp_w_s[j], gmlp_b_s[j],
                                       gmlp_w_out[j], gmlp_b_out[j])
        x = layer_norm(DEEPNORM_ALPHA * x + (1 + gt_m) * y, post_ln_g[i, 0], post_ln_b[i, 0])
        h = x * (1 + sc_f) + sh_f
        y = hierarchical_moe(h, moe_w_group_router[i], moe_b_group_router[i],
                             moe_w_expert_router[i], moe_b_expert_router[i],
                             moe_w_gate_up[i], moe_w_down[i])
        x = layer_norm(DEEPNORM_ALPHA * x + (1 + gt_f) * y, post_ln_g[i, 1], post_ln_b[i, 1])
    return x
```

```python
import functools

import jax
import jax.numpy as jnp
from jax import lax
from jax.experimental import pallas as pl
from jax.experimental.pallas import tpu as pltpu

F32 = jnp.float32
BF16 = jnp.bfloat16
I32 = jnp.int32

D_MODEL = 1024
DEPTH = 2
HEAD_DIM = 64
N_Q_HEADS = 16
N_KV_HEADS = 4
GQA_GROUP = N_Q_HEADS // N_KV_HEADS
WINDOW = 128
ROPE_THETA = 10000.0
Q_WIDTH = N_Q_HEADS * HEAD_DIM
KV_WIDTH = N_KV_HEADS * HEAD_DIM
QKV_WIDTH = Q_WIDTH + 2 * KV_WIDTH
CHUNK = 128
GMLP_WIDTH = 2 * D_MODEL
N_SGU_GROUPS = 8
SGU_GROUP_DIM = GMLP_WIDTH // N_SGU_GROUPS
N_EXPERT_GROUPS = 4
EXPERTS_PER_GROUP = 8
N_EXPERTS = N_EXPERT_GROUPS * EXPERTS_PER_GROUP
EXPERT_FF = D_MODEL // 4
N_MOD = 6
DEEPNORM_ALPHA = (2.0 * DEPTH) ** 0.25
LN_EPS = 1e-5

LANES = 128
MOD_ROWS = 8
ROUTER_LANES = 128
EXPERT_COL0 = 8
SORT_CHUNK = 256
EXPERT_TILE = 256
DISPATCH_BLOCK = 1024
VMEM_LIMIT = 56 * 1024 * 1024


def _cparams(n_axes, vmem=VMEM_LIMIT):
    return pltpu.CompilerParams(dimension_semantics=("arbitrary",) * n_axes, vmem_limit_bytes=vmem)


ROW_CHUNKS = D_MODEL // LANES


def _store_rows(ref, val):
    n = val.shape[0]
    for c in range(ROW_CHUNKS):
        ref[pl.ds(c, n, stride=ROW_CHUNKS), :] = val[:, c * LANES:(c + 1) * LANES]


def _load_rows(ref, n):
    return jnp.concatenate([ref[pl.ds(c, n, stride=ROW_CHUNKS), :] for c in range(ROW_CHUNKS)], axis=1)


def _layer_norm(r, g, b):
    mu = jnp.mean(r, axis=-1, keepdims=True)
    d = r - mu
    var = jnp.mean(d * d, axis=-1, keepdims=True)
    return d * lax.rsqrt(var + LN_EPS) * g + b


def _mod_body(c_ref, w_ref, b_ref, o_ref):
    c = c_ref[...]
    ca = c * jax.nn.sigmoid(c)
    o_ref[...] = jnp.dot(ca.astype(BF16), w_ref[...].astype(BF16), preferred_element_type=F32) + b_ref[...]


def _adaln_mod(c_pad, ada_w, ada_b):
    tn = 1536
    n_out = N_MOD * D_MODEL
    return pl.pallas_call(
        _mod_body,
        grid=(DEPTH, n_out // tn),
        in_specs=[
            pl.BlockSpec((MOD_ROWS, D_MODEL), lambda l, j: (0, 0)),
            pl.BlockSpec((None, D_MODEL, tn), lambda l, j: (l, 0, j)),
            pl.BlockSpec((None, 1, tn), lambda l, j: (l, 0, j)),
        ],
        out_specs=pl.BlockSpec((None, MOD_ROWS, tn), lambda l, j: (l, 0, j)),
        out_shape=jax.ShapeDtypeStruct((DEPTH, MOD_ROWS, n_out), F32),
        compiler_params=_cparams(2),
        name="adaln_mod",
    )(c_pad, ada_w, ada_b.reshape(DEPTH, 1, n_out))


def _qkv_body(x_ref, mod_ref, pos_ref, w_ref, b_ref, invf_ref, o_ref, wbf_ref):
    @pl.when(pl.program_id(0) == 0)
    def _():
        wbf_ref[...] = w_ref[...].astype(BF16)

    tm = x_ref.shape[0]
    sh = mod_ref[:, 0:D_MODEL]
    sc = mod_ref[:, D_MODEL:2 * D_MODEL]
    h = x_ref[...] * (1.0 + sc) + sh
    qkv = jnp.dot(h.astype(BF16), wbf_ref[...], preferred_element_type=F32) + b_ref[...]

    ang = invf_ref[...] * pos_ref[...].astype(F32)
    c = jnp.cos(ang)
    s = jnp.sin(ang)
    ct = jnp.concatenate([c, c, c, c], axis=0).T
    st = jnp.concatenate([-s, s, -s, s], axis=0).T
    lane = lax.broadcasted_iota(I32, (tm, LANES), 1)
    first_half = (lane & (HEAD_DIM // 2)) == 0
    n_rope = (Q_WIDTH + KV_WIDTH) // LANES
    for j in range(n_rope):
        blk = qkv[:, j * LANES:(j + 1) * LANES]
        rot = jnp.where(first_half, pltpu.roll(blk, LANES - HEAD_DIM // 2, 1), pltpu.roll(blk, HEAD_DIM // 2, 1))
        r = blk * ct + rot * st
        if j < Q_WIDTH // LANES:
            r = r * (HEAD_DIM ** -0.5)
        o_ref[:, j * LANES:(j + 1) * LANES] = r.astype(BF16)
    o_ref[:, Q_WIDTH + KV_WIDTH:] = qkv[:, Q_WIDTH + KV_WIDTH:].astype(BF16)


def _qkv_rope(x2, mod_rows, layer, positions, w_qkv, b_qkv, seq):
    t = x2.shape[0]
    tm = 512
    steps_per_batch = seq // tm
    inv_freq = ROPE_THETA ** (-jnp.arange(0, HEAD_DIM, 2, dtype=F32) / HEAD_DIM)
    return pl.pallas_call(
        _qkv_body,
        grid=(t // tm,),
        in_specs=[
            pl.BlockSpec((tm, D_MODEL), lambda i: (i, 0)),
            pl.BlockSpec((None, 1, N_MOD * D_MODEL), lambda i: (layer * MOD_ROWS + i // steps_per_batch, 0, 0)),
            pl.BlockSpec((None, 1, tm), lambda i: (i, 0, 0)),
            pl.BlockSpec((D_MODEL, QKV_WIDTH), lambda i: (0, 0)),
            pl.BlockSpec((1, QKV_WIDTH), lambda i: (0, 0)),
            pl.BlockSpec((HEAD_DIM // 2, 1), lambda i: (0, 0)),
        ],
        out_specs=pl.BlockSpec((tm, QKV_WIDTH), lambda i: (i, 0)),
        out_shape=jax.ShapeDtypeStruct((t, QKV_WIDTH), BF16),
        scratch_shapes=[pltpu.VMEM((D_MODEL, QKV_WIDTH), BF16)],
        compiler_params=_cparams(1),
        name="qkv_rope",
    )(x2, mod_rows, positions.reshape(t // tm, 1, tm), w_qkv, b_qkv.reshape(1, QKV_WIDTH),
      inv_freq.reshape(HEAD_DIM // 2, 1))


def _attn_body(sink_ref, q_ref, kvc_ref, kvp_ref, o_ref):
    n = pl.program_id(1)
    q = q_ref[...]
    kvc = kvc_ref[...]
    kvp = kvp_ref[...]
    kb = jnp.concatenate([kvp[:, :KV_WIDTH], kvc[:, :KV_WIDTH]], axis=0)
    vb = jnp.concatenate([kvp[:, KV_WIDTH:], kvc[:, KV_WIDTH:]], axis=0)
    qi = lax.broadcasted_iota(I32, (WINDOW, 2 * WINDOW), 0) + WINDOW
    kj = lax.broadcasted_iota(I32, (WINDOW, 2 * WINDOW), 1)
    mask = (kj <= qi) & (kj > qi - WINDOW) & ((kj >= WINDOW) | (n > 0))
    outs = []
    for h in range(N_Q_HEADS):
        g = h // GQA_GROUP
        qh = q[:, h * HEAD_DIM:(h + 1) * HEAD_DIM]
        kg = kb[:, g * HEAD_DIM:(g + 1) * HEAD_DIM]
        vg = vb[:, g * HEAD_DIM:(g + 1) * HEAD_DIM]
        s = lax.dot_general(qh, kg, (((1,), (1,)), ((), ())), preferred_element_type=F32)
        s = jnp.where(mask, s, -jnp.inf)
        sink = sink_ref[h]
        m = jnp.maximum(jnp.max(s, axis=-1, keepdims=True), sink)
        p = jnp.exp(s - m)
        denom = jnp.sum(p, axis=-1, keepdims=True) + jnp.exp(sink - m)
        o = jnp.dot(p.astype(BF16), vg, preferred_element_type=F32)
        outs.append(o / denom)
    o_ref[...] = jnp.concatenate(outs, axis=1).astype(BF16)


def _attention(qkv, sinks, batch, seq):
    t = qkv.shape[0]
    nb = seq // WINDOW
    kv_col = Q_WIDTH // (2 * KV_WIDTH)
    return pl.pallas_call(
        _attn_body,
        grid=(batch, nb),
        in_specs=[
            pl.BlockSpec(memory_space=pltpu.SMEM),
            pl.BlockSpec((WINDOW, Q_WIDTH), lambda b, n: (b * nb + n, 0)),
            pl.BlockSpec((WINDOW, 2 * KV_WIDTH), lambda b, n: (b * nb + n, kv_col)),
            pl.BlockSpec((WINDOW, 2 * KV_WIDTH), lambda b, n: (b * nb + jnp.maximum(n - 1, 0), kv_col)),
        ],
        out_specs=pl.BlockSpec((WINDOW, Q_WIDTH), lambda b, n: (b * nb + n, 0)),
        out_shape=jax.ShapeDtypeStruct((t, Q_WIDTH), BF16),
        compiler_params=_cparams(2),
        name="swa_attention",
    )(sinks, qkv, qkv, qkv)


def _route(lt):
    tm = lt.shape[1]
    row = lax.broadcasted_iota(I32, (EXPERTS_PER_GROUP, tm), 0)
    neg = -jnp.inf
    gl = jnp.where(row < N_EXPERT_GROUPS, lt[0:EXPERTS_PER_GROUP], neg)
    gm = jnp.max(gl, axis=0, keepdims=True)
    g_p = 1.0 / jnp.sum(jnp.exp(gl - gm), axis=0, keepdims=True)
    g_idx = jnp.min(jnp.where(gl == gm, row, EXPERTS_PER_GROUP), axis=0, keepdims=True)
    sel = lt[EXPERT_COL0 + (N_EXPERT_GROUPS - 1) * EXPERTS_PER_GROUP:EXPERT_COL0 + N_EXPERTS]
    for g in range(N_EXPERT_GROUPS - 2, -1, -1):
        lo = EXPERT_COL0 + g * EXPERTS_PER_GROUP
        sel = jnp.where(g_idx == g, lt[lo:lo + EXPERTS_PER_GROUP], sel)
    v1 = jnp.max(sel, axis=0, keepdims=True)
    i1 = jnp.min(jnp.where(sel == v1, row, EXPERTS_PER_GROUP), axis=0, keepdims=True)
    sel2 = jnp.where(row == i1, neg, sel)
    v2 = jnp.max(sel2, axis=0, keepdims=True)
    i2 = jnp.min(jnp.where(sel2 == v2, row, EXPERTS_PER_GROUP), axis=0, keepdims=True)
    e2 = jnp.exp(v2 - v1)
    w1 = g_p / (1.0 + e2)
    w2 = g_p * e2 / (1.0 + e2)
    base = g_idx * EXPERTS_PER_GROUP
    return base + i1, base + i2, w1, w2


def _proj_body(o_ref, x_ref, mod_ref, w_ref, b_ref, lng_ref, lnb_ref, wr_ref, br_ref,
               x1_ref, h2_ref, ids_ref, wts_ref, wbf_ref):
    @pl.when(pl.program_id(0) == 0)
    def _():
        wbf_ref[...] = w_ref[...].astype(BF16)

    d = D_MODEL
    y = jnp.dot(o_ref[...], wbf_ref[...], preferred_element_type=F32) + b_ref[...]
    gt_m = mod_ref[:, 2 * d:3 * d]
    x1 = _layer_norm(DEEPNORM_ALPHA * x_ref[...] + (1.0 + gt_m) * y, lng_ref[...], lnb_ref[...])
    x1_ref[...] = x1
    sh_f = mod_ref[:, 3 * d:4 * d]
    sc_f = mod_ref[:, 4 * d:5 * d]
    h2 = x1 * (1.0 + sc_f) + sh_f
    _store_rows(h2_ref, h2)
    logits = jnp.dot(h2, wr_ref[...], precision=lax.Precision.HIGHEST, preferred_element_type=F32) + br_ref[...]
    ea, eb, wa, wb = _route(logits.T)
    ids_ref[0:1, :] = ea
    ids_ref[1:2, :] = eb
    wts_ref[0:1, :] = wa
    wts_ref[1:2, :] = wb


def _proj_ln_router(o, x2, mod_rows, layer, w, b, ln_g, ln_b, w_router, b_router, seq):
    t, k = o.shape
    tm = 256
    steps_per_batch = seq // tm
    d = D_MODEL
    return pl.pallas_call(
        _proj_body,
        grid=(t // tm,),
        in_specs=[
            pl.BlockSpec((tm, k), lambda i: (i, 0)),
            pl.BlockSpec((tm, d), lambda i: (i, 0)),
            pl.BlockSpec((None, 1, N_MOD * d), lambda i: (layer * MOD_ROWS + i // steps_per_batch, 0, 0)),
            pl.BlockSpec((k, d), lambda i: (0, 0)),
            pl.BlockSpec((1, d), lambda i: (0, 0)),
            pl.BlockSpec((1, d), lambda i: (0, 0)),
            pl.BlockSpec((1, d), lambda i: (0, 0)),
            pl.BlockSpec((d, ROUTER_LANES), lambda i: (0, 0)),
            pl.BlockSpec((1, ROUTER_LANES), lambda i: (0, 0)),
        ],
        out_specs=[
            pl.BlockSpec((tm, d), lambda i: (i, 0)),
            pl.BlockSpec((tm * ROW_CHUNKS, LANES), lambda i: (i, 0)),
            pl.BlockSpec((2, tm), lambda i: (0, i)),
            pl.BlockSpec((2, tm), lambda i: (0, i)),
        ],
        out_shape=[
            jax.ShapeDtypeStruct((t, d), F32),
            jax.ShapeDtypeStruct((t * ROW_CHUNKS, LANES), F32),
            jax.ShapeDtypeStruct((2, t), I32),
            jax.ShapeDtypeStruct((2, t), F32),
        ],
        scratch_shapes=[pltpu.VMEM((k, d), BF16)],
        compiler_params=_cparams(1),
        name="proj_ln_router",
    )(o, x2, mod_rows, w, b.reshape(1, d), ln_g.reshape(1, d), ln_b.reshape(1, d), w_router, b_router)


def _sort_body(ids_ref, pos_ref, te_ref, nused_ref, rank_ref):
    n_rows = ids_ref.shape[0]
    c = SORT_CHUNK
    erow = lax.broadcasted_iota(I32, (N_EXPERTS, c), 0)
    tri = (lax.broadcasted_iota(I32, (c, c), 0) <= lax.broadcasted_iota(I32, (c, c), 1)).astype(BF16)

    def rank_step(r, carry):
        onehot = erow == ids_ref[pl.ds(r, 1), :]
        pref = jnp.dot(onehot.astype(BF16), tri, preferred_element_type=F32)
        rank = jnp.sum(jnp.where(onehot, pref + carry, 0.0), axis=0, keepdims=True) - 1.0
        rank_ref[pl.ds(r, 1), :] = rank
        return carry + pref[:, c - 1:c]

    counts = lax.fori_loop(0, n_rows, rank_step, jnp.zeros((N_EXPERTS, 1), F32))
    n_tile = jnp.floor((counts + (EXPERT_TILE - 1)) * (1.0 / EXPERT_TILE))
    low = (lax.broadcasted_iota(I32, (N_EXPERTS, N_EXPERTS), 1)
           <= lax.broadcasted_iota(I32, (N_EXPERTS, N_EXPERTS), 0)).astype(BF16)
    cum = jnp.dot(low, jnp.broadcast_to(n_tile, (N_EXPERTS, LANES)).astype(BF16),
                  preferred_element_type=F32)[:, 0:1]
    row_off = (cum - n_tile) * EXPERT_TILE

    def pos_step(r, _):
        onehot = erow == ids_ref[pl.ds(r, 1), :]
        off = jnp.sum(jnp.where(onehot, row_off, 0.0), axis=0, keepdims=True)
        pos_ref[pl.ds(r, 1), :] = (off + rank_ref[pl.ds(r, 1), :]).astype(I32)
        return 0

    lax.fori_loop(0, n_rows, pos_step, 0)
    total = jnp.max(cum, axis=0, keepdims=True)
    n_lanes = te_ref.shape[1]
    tile = jnp.minimum(lax.broadcasted_iota(I32, (N_EXPERTS, n_lanes), 1).astype(F32), total - 1.0)
    te_ref[...] = jnp.sum(jnp.where(cum <= tile, 1.0, 0.0), axis=0, keepdims=True).astype(I32)
    nused_ref[...] = jnp.broadcast_to(total, nused_ref.shape).astype(I32)


def _expert_sort(ids, n_tiles):
    n_assign = ids.shape[0] * ids.shape[1]
    n_rows = n_assign // SORT_CHUNK
    te_lanes = -(-n_tiles // LANES) * LANES
    pos, te, nused = pl.pallas_call(
        _sort_body,
        grid=(1,),
        in_specs=[pl.BlockSpec((n_rows, SORT_CHUNK), lambda i: (0, 0))],
        out_specs=[
            pl.BlockSpec((n_rows, SORT_CHUNK), lambda i: (0, 0)),
            pl.BlockSpec((1, te_lanes), lambda i: (0, 0)),
            pl.BlockSpec((1, LANES), lambda i: (0, 0)),
        ],
        out_shape=[
            jax.ShapeDtypeStruct((n_rows, SORT_CHUNK), I32),
            jax.ShapeDtypeStruct((1, te_lanes), I32),
            jax.ShapeDtypeStruct((1, LANES), I32),
        ],
        scratch_shapes=[pltpu.VMEM((n_rows, SORT_CHUNK), F32)],
        compiler_params=_cparams(1),
        name="expert_sort",
    )(ids.reshape(n_rows, SORT_CHUNK))
    return pos, te[0, :n_tiles], nused[0, :1]


def _row_copy(src, src_row, dst, dst_row, sem):
    s0 = pl.multiple_of(src_row * ROW_CHUNKS, ROW_CHUNKS)
    d0 = pl.multiple_of(dst_row * ROW_CHUNKS, ROW_CHUNKS)
    return pltpu.make_async_copy(src.at[pl.ds(s0, ROW_CHUNKS)], dst.at[pl.ds(d0, ROW_CHUNKS)], sem)


def _dispatch_body(pos_ref, h_ref, xs_in_ref, xs_ref, sem, *, n_tokens):
    del xs_in_ref
    nblk = pos_ref.shape[1]
    t0 = lax.rem(pl.program_id(0) * nblk, n_tokens)

    def issue(j, _):
        _row_copy(h_ref, t0 + j, xs_ref, pos_ref[0, j], sem).start()
        return 0

    lax.fori_loop(0, nblk, issue, 0)

    def drain(j, _):
        _row_copy(h_ref, 0, xs_ref, 0, sem).wait()
        return 0

    lax.fori_loop(0, nblk, drain, 0)


def _dispatch(pos, h2, n_rows):
    t = h2.shape[0] // ROW_CHUNKS
    n_assign = pos.shape[0] * pos.shape[1]
    nblk = DISPATCH_BLOCK
    zeros = jnp.zeros((n_rows * ROW_CHUNKS, LANES), h2.dtype)
    return pl.pallas_call(
        functools.partial(_dispatch_body, n_tokens=t),
        grid=(n_assign // nblk,),
        in_specs=[
            pl.BlockSpec((None, 1, nblk), lambda i: (i, 0, 0), memory_space=pltpu.SMEM),
            pl.BlockSpec(memory_space=pl.ANY),
            pl.BlockSpec(memory_space=pl.ANY),
        ],
        out_specs=pl.BlockSpec(memory_space=pl.ANY),
        out_shape=jax.ShapeDtypeStruct((n_rows * ROW_CHUNKS, LANES), h2.dtype),
        scratch_shapes=[pltpu.SemaphoreType.DMA],
        input_output_aliases={2: 0},
        compiler_params=_cparams(1),
        name="moe_dispatch",
    )(pos.reshape(n_assign // nblk, 1, nblk), h2, zeros)


def _expert_body(te_ref, nused_ref, xs_ref, wgu_ref, wd_ref, ys_ref, wgu_bf, wd_bf):
    i = pl.program_id(0)
    used = i < nused_ref[0]
    prev = te_ref[jnp.maximum(i - 1, 0)]

    @pl.when(used & ((i == 0) | (te_ref[i] != prev)))
    def _():
        wgu_bf[...] = wgu_ref[...].astype(BF16)
        wd_bf[...] = wd_ref[...].astype(BF16)

    @pl.when(used)
    def _():
        xs = _load_rows(xs_ref, EXPERT_TILE).astype(BF16)
        gu = jnp.dot(xs, wgu_bf[...], preferred_element_type=F32)
        gate = gu[:, :EXPERT_FF]
        up = gu[:, EXPERT_FF:]
        act = gate * jax.nn.sigmoid(gate) * up
        _store_rows(ys_ref, jnp.dot(act.astype(BF16), wd_bf[...], preferred_element_type=F32))

    @pl.when(jnp.logical_not(used))
    def _():
        ys_ref[...] = jnp.zeros(ys_ref.shape, ys_ref.dtype)


def _expert_mlp(tile_expert, n_used, xs, w_gate_up, w_down):
    d = D_MODEL
    n_tiles = xs.shape[0] // (EXPERT_TILE * ROW_CHUNKS)
    f2 = 2 * EXPERT_FF
    tile_rows = EXPERT_TILE * ROW_CHUNKS
    grid_spec = pltpu.PrefetchScalarGridSpec(
        num_scalar_prefetch=2,
        grid=(n_tiles,),
        in_specs=[
            pl.BlockSpec((tile_rows, LANES), lambda i, te, nu: (jnp.minimum(i, nu[0] - 1), 0)),
            pl.BlockSpec((None, d, f2), lambda i, te, nu: (te[i], 0, 0)),
            pl.BlockSpec((None, EXPERT_FF, d), lambda i, te, nu: (te[i], 0, 0)),
        ],
        out_specs=pl.BlockSpec((tile_rows, LANES), lambda i, te, nu: (i, 0)),
        scratch_shapes=[pltpu.VMEM((d, f2), BF16), pltpu.VMEM((EXPERT_FF, d), BF16)],
    )
    return pl.pallas_call(
        _expert_body,
        grid_spec=grid_spec,
        out_shape=jax.ShapeDtypeStruct(xs.shape, F32),
        compiler_params=_cparams(1),
        name="expert_mlp",
    )(tile_expert, n_used, xs, w_gate_up.reshape(N_EXPERTS, d, f2), w_down.reshape(N_EXPERTS, EXPERT_FF, d))


def _combine_body(pos_ref, wts_ref, x_ref, mod_ref, lng_ref, lnb_ref, ys_ref, o_ref, buf, sem):
    tm = x_ref.shape[0]
    d = D_MODEL

    def issue(j, _):
        for k in range(2):
            _row_copy(ys_ref, pos_ref[k, j], buf.at[k], j, sem).start()
        return 0

    lax.fori_loop(0, tm, issue, 0)

    def drain(j, _):
        for k in range(2):
            _row_copy(ys_ref, 0, buf.at[k], 0, sem).wait()
        return 0

    lax.fori_loop(0, tm, drain, 0)

    cols = []
    for k in range(2):
        wt = jnp.broadcast_to(wts_ref[k:k + 1, :], (LANES, tm)).T
        cols.append(jnp.concatenate([wt] * (d // LANES), axis=1))
    y = cols[0] * _load_rows(buf.at[0], tm) + cols[1] * _load_rows(buf.at[1], tm)
    gt_f = mod_ref[:, 5 * d:6 * d]
    o_ref[...] = _layer_norm(DEEPNORM_ALPHA * x_ref[...] + (1.0 + gt_f) * y, lng_ref[...], lnb_ref[...])


def _combine_ln(pos2, wts, x1, mod_rows, layer, ln_g, ln_b, ys, seq):
    t, d = x1.shape
    tm = 256
    steps_per_batch = seq // tm
    return pl.pallas_call(
        _combine_body,
        grid=(t // tm,),
        in_specs=[
            pl.BlockSpec((2, tm), lambda i: (0, i), memory_space=pltpu.SMEM),
            pl.BlockSpec((2, tm), lambda i: (0, i)),
            pl.BlockSpec((tm, d), lambda i: (i, 0)),
            pl.BlockSpec((None, 1, N_MOD * d), lambda i: (layer * MOD_ROWS + i // steps_per_batch, 0, 0)),
            pl.BlockSpec((1, d), lambda i: (0, 0)),
            pl.BlockSpec((1, d), lambda i: (0, 0)),
            pl.BlockSpec(memory_space=pl.ANY),
        ],
        out_specs=pl.BlockSpec((tm, d), lambda i: (i, 0)),
        out_shape=jax.ShapeDtypeStruct((t, d), F32),
        scratch_shapes=[pltpu.VMEM((2, tm * ROW_CHUNKS, LANES), ys.dtype), pltpu.SemaphoreType.DMA],
        compiler_params=_cparams(1),
        name="moe_combine_ln",
    )(pos2, wts, x1, mod_rows, ln_g.reshape(1, d), ln_b.reshape(1, d), ys)


def _gmlp_body(x_ref, mod_ref, w_ref, b_ref, g_ref, beta_ref, ws_ref, bs_ref, o_ref, ws_bf):
    @pl.when(pl.program_id(0) == 0)
    def _():
        tri = lax.broadcasted_iota(I32, (CHUNK, CHUNK), 0) >= lax.broadcasted_iota(I32, (CHUNK, CHUNK), 1)
        for g in range(N_SGU_GROUPS):
            ws_bf[g] = jnp.where(tri, ws_ref[g], 0.0).astype(BF16)

    tm = x_ref.shape[0]
    d = D_MODEL
    sh = mod_ref[:, 0:d]
    sc = mod_ref[:, d:2 * d]
    h = x_ref[...] * (1.0 + sc) + sh
    z = jnp.dot(h.astype(BF16), w_ref[...], preferred_element_type=F32) + b_ref[...]
    z = 0.5 * z * (1.0 + lax.erf(z * (2.0 ** -0.5)))
    u = z[:, :GMLP_WIDTH]
    v = _layer_norm(z[:, GMLP_WIDTH:], g_ref[...], beta_ref[...]).astype(BF16)
    for ci in range(tm // CHUNK):
        rows = slice(ci * CHUNK, (ci + 1) * CHUNK)
        for g in range(N_SGU_GROUPS):
            lanes = slice(g * SGU_GROUP_DIM, (g + 1) * SGU_GROUP_DIM)
            mixed = jnp.dot(ws_bf[g], v[rows, lanes], preferred_element_type=F32) + bs_ref[:, g:g + 1]
            o_ref[rows, lanes] = (u[rows, lanes] * mixed).astype(BF16)


def _gmlp_gate(x2, mod_rows, layer, w_in_bf, b_in, sgu_g, sgu_b, w_s, b_s, seq):
    t, d = x2.shape
    tm = 256
    steps_per_batch = seq // tm
    gw = GMLP_WIDTH
    return pl.pallas_call(
        _gmlp_body,
        grid=(t // tm,),
        in_specs=[
            pl.BlockSpec((tm, d), lambda i: (i, 0)),
            pl.BlockSpec((None, 1, N_MOD * d), lambda i: (layer * MOD_ROWS + i // steps_per_batch, 0, 0)),
            pl.BlockSpec((d, 2 * gw), lambda i: (0, 0)),
            pl.BlockSpec((1, 2 * gw), lambda i: (0, 0)),
            pl.BlockSpec((1, gw), lambda i: (0, 0)),
            pl.BlockSpec((1, gw), lambda i: (0, 0)),
            pl.BlockSpec((N_SGU_GROUPS, CHUNK, CHUNK), lambda i: (0, 0, 0)),
            pl.BlockSpec((CHUNK, N_SGU_GROUPS), lambda i: (0, 0)),
        ],
        out_specs=pl.BlockSpec((tm, gw), lambda i: (i, 0)),
        out_shape=jax.ShapeDtypeStruct((t, gw), BF16),
        scratch_shapes=[pltpu.VMEM((N_SGU_GROUPS, CHUNK, CHUNK), BF16)],
        compiler_params=_cparams(1),
        name="gmlp_gate",
    )(x2, mod_rows, w_in_bf, b_in.reshape(1, 2 * gw), sgu_g.reshape(1, gw), sgu_b.reshape(1, gw), w_s, b_s.T)


def _router_params(w_group, b_group, w_expert, b_expert):
    d = w_group.shape[0]
    w = jnp.zeros((d, ROUTER_LANES), F32)
    w = w.at[:, :N_EXPERT_GROUPS].set(w_group).at[:, EXPERT_COL0:EXPERT_COL0 + N_EXPERTS].set(w_expert)
    b = jnp.zeros((1, ROUTER_LANES), F32)
    b = b.at[0, :N_EXPERT_GROUPS].set(b_group).at[0, EXPERT_COL0:EXPERT_COL0 + N_EXPERTS].set(b_expert)
    return w, b


def _moe_block(x1, h2, ids, wts, mod_rows, layer, ln_g, ln_b, w_gate_up, w_down, seq):
    t = x1.shape[0]
    n_rows = 2 * t + N_EXPERTS * EXPERT_TILE
    n_tiles = n_rows // EXPERT_TILE
    pos, tile_expert, n_used = _expert_sort(ids, n_tiles)
    xs = _dispatch(pos, h2, n_rows)
    ys = _expert_mlp(tile_expert, n_used, xs, w_gate_up, w_down)
    return _combine_ln(pos.reshape(2, t), wts, x1, mod_rows, layer, ln_g, ln_b, ys, seq)


def kernel(x, c, positions, ada_w, ada_b, post_ln_g, post_ln_b, attn_w_qkv, attn_b_qkv, attn_sinks, attn_w_o, attn_b_o, gmlp_w_in, gmlp_b_in, gmlp_sgu_ln_g, gmlp_sgu_ln_b, gmlp_w_s, gmlp_b_s, gmlp_w_out, gmlp_b_out, moe_w_group_router, moe_b_group_router, moe_w_expert_router, moe_b_expert_router, moe_w_gate_up, moe_w_down):
    batch, seq, d = x.shape
    t = batch * seq
    x2 = x.reshape(t, d)
    c_pad = jnp.zeros((MOD_ROWS, d), F32).at[:batch].set(c)
    mod_rows = _adaln_mod(c_pad, ada_w, ada_b).reshape(DEPTH * MOD_ROWS, 1, N_MOD * d)

    for layer in range(DEPTH):
        j = layer // 2
        if layer % 2 == 0:
            qkv = _qkv_rope(x2, mod_rows, layer, positions, attn_w_qkv[j], attn_b_qkv[j], seq)
            mix = _attention(qkv, attn_sinks[j], batch, seq)
            w_out, b_out = attn_w_o[j], attn_b_o[j]
        else:
            mix = _gmlp_gate(x2, mod_rows, layer, gmlp_w_in[j].astype(BF16), gmlp_b_in[j], gmlp_sgu_ln_g[j],
                             gmlp_sgu_ln_b[j], gmlp_w_s[j], gmlp_b_s[j], seq)
            w_out, b_out = gmlp_w_out[j], gmlp_b_out[j]
        w_router, b_router = _router_params(moe_w_group_router[layer], moe_b_group_router[layer],
                                            moe_w_expert_router[layer], moe_b_expert_router[layer])
        x1, h2, ids, wts = _proj_ln_router(mix, x2, mod_rows, layer, w_out, b_out, post_ln_g[layer, 0],
                                           post_ln_b[layer, 0], w_router, b_router, seq)
        x2 = _moe_block(x1, h2, ids, wts, mod_rows, layer, post_ln_g[layer, 1], post_ln_b[layer, 1],
                        moe_w_gate_up[layer], moe_w_down[layer], seq)
    return x2.reshape(batch, seq, d)
```

```python
import functools

import jax
import jax.numpy as jnp
from jax import lax
from jax.experimental import pallas as pl
from jax.experimental.pallas import tpu as pltpu

F32 = jnp.float32
BF16 = jnp.bfloat16
I32 = jnp.int32

D_MODEL = 1024
DEPTH = 2
HEAD_DIM = 64
N_Q_HEADS = 16
N_KV_HEADS = 4
GQA_GROUP = N_Q_HEADS // N_KV_HEADS
WINDOW = 128
ROPE_THETA = 10000.0
Q_WIDTH = N_Q_HEADS * HEAD_DIM
KV_WIDTH = N_KV_HEADS * HEAD_DIM
QKV_WIDTH = Q_WIDTH + 2 * KV_WIDTH
CHUNK = 128
GMLP_WIDTH = 2 * D_MODEL
N_SGU_GROUPS = 8
SGU_GROUP_DIM = GMLP_WIDTH // N_SGU_GROUPS
N_EXPERT_GROUPS = 4
EXPERTS_PER_GROUP = 8
N_EXPERTS = N_EXPERT_GROUPS * EXPERTS_PER_GROUP
EXPERT_FF = D_MODEL // 4
N_MOD = 6
DEEPNORM_ALPHA = (2.0 * DEPTH) ** 0.25
LN_EPS = 1e-5

LANES = 128
MOD_ROWS = 8
ROUTER_LANES = 128
EXPERT_COL0 = 8
SORT_CHUNK = 256
EXPERT_TILE = 256
DISPATCH_BLOCK = 1024
VMEM_LIMIT = 56 * 1024 * 1024


def _cparams(n_axes, vmem=VMEM_LIMIT):
    return pltpu.CompilerParams(dimension_semantics=("arbitrary",) * n_axes, vmem_limit_bytes=vmem)


ROW_CHUNKS = D_MODEL // LANES


def _store_rows(ref, val):
    n = val.shape[0]
    for c in range(ROW_CHUNKS):
        ref[pl.ds(c, n, stride=ROW_CHUNKS), :] = val[:, c * LANES:(c + 1) * LANES]


def _load_rows(ref, n):
    return jnp.concatenate([ref[pl.ds(c, n, stride=ROW_CHUNKS), :] for c in range(ROW_CHUNKS)], axis=1)


def _layer_norm(r, g, b):
    mu = jnp.mean(r, axis=-1, keepdims=True)
    d = r - mu
    var = jnp.mean(d * d, axis=-1, keepdims=True)
    return d * lax.rsqrt(var + LN_EPS) * g + b


def _mod_body(c_ref, w_ref, b_ref, o_ref):
    c = c_ref[...]
    ca = c * jax.nn.sigmoid(c)
    o_ref[...] = jnp.dot(ca.astype(BF16), w_ref[...].astype(BF16), preferred_element_type=F32) + b_ref[...]


def _adaln_mod(c_pad, ada_w, ada_b):
    tn = 1536
    n_out = N_MOD * D_MODEL
    return pl.pallas_call(
        _mod_body,
        grid=(DEPTH, n_out // tn),
        in_specs=[
            pl.BlockSpec((MOD_ROWS, D_MODEL), lambda l, j: (0, 0)),
            pl.BlockSpec((None, D_MODEL, tn), lambda l, j: (l, 0, j)),
            pl.BlockSpec((None, 1, tn), lambda l, j: (l, 0, j)),
        ],
        out_specs=pl.BlockSpec((None, MOD_ROWS, tn), lambda l, j: (l, 0, j)),
        out_shape=jax.ShapeDtypeStruct((DEPTH, MOD_ROWS, n_out), F32),
        compiler_params=_cparams(2),
        name="adaln_mod",
    )(c_pad, ada_w, ada_b.reshape(DEPTH, 1, n_out))


def _qkv_body(x_ref, mod_ref, pos_ref, w_ref, b_ref, invf_ref, o_ref, wbf_ref):
    @pl.when(pl.program_id(0) == 0)
    def _():
        wbf_ref[...] = w_ref[...].astype(BF16)

    tm = x_ref.shape[0]
    sh = mod_ref[:, 0:D_MODEL]
    sc = mod_ref[:, D_MODEL:2 * D_MODEL]
    h = x_ref[...] * (1.0 + sc) + sh
    qkv = jnp.dot(h.astype(BF16), wbf_ref[...], preferred_element_type=F32) + b_ref[...]

    ang = invf_ref[...] * pos_ref[...].astype(F32)
    c = jnp.cos(ang)
    s = jnp.sin(ang)
    ct = jnp.concatenate([c, c, c, c], axis=0).T
    st = jnp.concatenate([-s, s, -s, s], axis=0).T
    lane = lax.broadcasted_iota(I32, (tm, LANES), 1)
    first_half = (lane & (HEAD_DIM // 2)) == 0
    n_rope = (Q_WIDTH + KV_WIDTH) // LANES
    for j in range(n_rope):
        blk = qkv[:, j * LANES:(j + 1) * LANES]
        rot = jnp.where(first_half, pltpu.roll(blk, LANES - HEAD_DIM // 2, 1), pltpu.roll(blk, HEAD_DIM // 2, 1))
        r = blk * ct + rot * st
        if j < Q_WIDTH // LANES:
            r = r * (HEAD_DIM ** -0.5)
        o_ref[:, j * LANES:(j + 1) * LANES] = r.astype(BF16)
    o_ref[:, Q_WIDTH + KV_WIDTH:] = qkv[:, Q_WIDTH + KV_WIDTH:].astype(BF16)


def _qkv_rope(x2, mod_rows, layer, positions, w_qkv, b_qkv, seq):
    t = x2.shape[0]
    tm = 512
    steps_per_batch = seq // tm
    inv_freq = ROPE_THETA ** (-jnp.arange(0, HEAD_DIM, 2, dtype=F32) / HEAD_DIM)
    return pl.pallas_call(
        _qkv_body,
        grid=(t // tm,),
        in_specs=[
            pl.BlockSpec((tm, D_MODEL), lambda i: (i, 0)),
            pl.BlockSpec((None, 1, N_MOD * D_MODEL), lambda i: (layer * MOD_ROWS + i // steps_per_batch, 0, 0)),
            pl.BlockSpec((None, 1, tm), lambda i: (i, 0, 0)),
            pl.BlockSpec((D_MODEL, QKV_WIDTH), lambda i: (0, 0)),
            pl.BlockSpec((1, QKV_WIDTH), lambda i: (0, 0)),
            pl.BlockSpec((HEAD_DIM // 2, 1), lambda i: (0, 0)),
        ],
        out_specs=pl.BlockSpec((tm, QKV_WIDTH), lambda i: (i, 0)),
        out_shape=jax.ShapeDtypeStruct((t, QKV_WIDTH), BF16),
        scratch_shapes=[pltpu.VMEM((D_MODEL, QKV_WIDTH), BF16)],
        compiler_params=_cparams(1),
        name="qkv_rope",
    )(x2, mod_rows, positions.reshape(t // tm, 1, tm), w_qkv, b_qkv.reshape(1, QKV_WIDTH),
      inv_freq.reshape(HEAD_DIM // 2, 1))


def _attn_body(sink_ref, q_ref, kvc_ref, kvp_ref, o_ref):
    n = pl.program_id(1)
    q = q_ref[...]
    kvc = kvc_ref[...]
    kvp = kvp_ref[...]
    kb = jnp.concatenate([kvp[:, :KV_WIDTH], kvc[:, :KV_WIDTH]], axis=0)
    vb = jnp.concatenate([kvp[:, KV_WIDTH:], kvc[:, KV_WIDTH:]], axis=0)
    qi = lax.broadcasted_iota(I32, (WINDOW, 2 * WINDOW), 0) + WINDOW
    kj = lax.broadcasted_iota(I32, (WINDOW, 2 * WINDOW), 1)
    mask = (kj <= qi) & (kj > qi - WINDOW) & ((kj >= WINDOW) | (n > 0))
    outs = []
    for h in range(N_Q_HEADS):
        g = h // GQA_GROUP
        qh = q[:, h * HEAD_DIM:(h + 1) * HEAD_DIM]
        kg = kb[:, g * HEAD_DIM:(g + 1) * HEAD_DIM]
        vg = vb[:, g * HEAD_DIM:(g + 1) * HEAD_DIM]
        s = lax.dot_general(qh, kg, (((1,), (1,)), ((), ())), preferred_element_type=F32)
        s = jnp.where(mask, s, -jnp.inf)
        sink = sink_ref[h]
        m = jnp.maximum(jnp.max(s, axis=-1, keepdims=True), sink)
        p = jnp.exp(s - m)
        denom = jnp.sum(p, axis=-1, keepdims=True) + jnp.exp(sink - m)
        o = jnp.dot(p.astype(BF16), vg, preferred_element_type=F32)
        outs.append(o / denom)
    o_ref[...] = jnp.concatenate(outs, axis=1).astype(BF16)


def _attention(qkv, sinks, batch, seq):
    t = qkv.shape[0]
    nb = seq // WINDOW
    kv_col = Q_WIDTH // (2 * KV_WIDTH)
    return pl.pallas_call(
        _attn_body,
        grid=(batch, nb),
        in_specs=[
            pl.BlockSpec(memory_space=pltpu.SMEM),
            pl.BlockSpec((WINDOW, Q_WIDTH), lambda b, n: (b * nb + n, 0)),
            pl.BlockSpec((WINDOW, 2 * KV_WIDTH), lambda b, n: (b * nb + n, kv_col)),
            pl.BlockSpec((WINDOW, 2 * KV_WIDTH), lambda b, n: (b * nb + jnp.maximum(n - 1, 0), kv_col)),
        ],
        out_specs=pl.BlockSpec((WINDOW, Q_WIDTH), lambda b, n: (b * nb + n, 0)),
        out_shape=jax.ShapeDtypeStruct((t, Q_WIDTH), BF16),
        compiler_params=_cparams(2),
        name="swa_attention",
    )(sinks, qkv, qkv, qkv)


def _route(lt):
    tm = lt.shape[1]
    row = lax.broadcasted_iota(I32, (EXPERTS_PER_GROUP, tm), 0)
    neg = -jnp.inf
    gl = jnp.where(row < N_EXPERT_GROUPS, lt[0:EXPERTS_PER_GROUP], neg)
    gm = jnp.max(gl, axis=0, keepdims=True)
    g_p = 1.0 / jnp.sum(jnp.exp(gl - gm), axis=0, keepdims=True)
    g_idx = jnp.min(jnp.where(gl == gm, row, EXPERTS_PER_GROUP), axis=0, keepdims=True)
    sel = lt[EXPERT_COL0 + (N_EXPERT_GROUPS - 1) * EXPERTS_PER_GROUP:EXPERT_COL0 + N_EXPERTS]
    for g in range(N_EXPERT_GROUPS - 2, -1, -1):
        lo = EXPERT_COL0 + g * EXPERTS_PER_GROUP
        sel = jnp.where(g_idx == g, lt[lo:lo + EXPERTS_PER_GROUP], sel)
    v1 = jnp.max(sel, axis=0, keepdims=True)
    i1 = jnp.min(jnp.where(sel == v1, row, EXPERTS_PER_GROUP), axis=0, keepdims=True)
    sel2 = jnp.where(row == i1, neg, sel)
    v2 = jnp.max(sel2, axis=0, keepdims=True)
    i2 = jnp.min(jnp.where(sel2 == v2, row, EXPERTS_PER_GROUP), axis=0, keepdims=True)
    e2 = jnp.exp(v2 - v1)
    w1 = g_p / (1.0 + e2)
    w2 = g_p * e2 / (1.0 + e2)
    base = g_idx * EXPERTS_PER_GROUP
    return base + i1, base + i2, w1, w2


def _proj_body(o_ref, x_ref, mod_ref, w_ref, b_ref, lng_ref, lnb_ref, wr_ref, br_ref,
               x1_ref, h2_ref, ids_ref, wts_ref, wbf_ref):
    @pl.when(pl.program_id(0) == 0)
    def _():
        wbf_ref[...] = w_ref[...].astype(BF16)

    d = D_MODEL
    y = jnp.dot(o_ref[...], wbf_ref[...], preferred_element_type=F32) + b_ref[...]
    gt_m = mod_ref[:, 2 * d:3 * d]
    x1 = _layer_norm(DEEPNORM_ALPHA * x_ref[...] + (1.0 + gt_m) * y, lng_ref[...], lnb_ref[...])
    x1_ref[...] = x1
    sh_f = mod_ref[:, 3 * d:4 * d]
    sc_f = mod_ref[:, 4 * d:5 * d]
    h2 = x1 * (1.0 + sc_f) + sh_f
    _store_rows(h2_ref, h2)
    logits = jnp.dot(h2, wr_ref[...], precision=lax.Precision.HIGHEST, preferred_element_type=F32) + br_ref[...]
    ea, eb, wa, wb = _route(logits.T)
    ids_ref[0:1, :] = ea
    ids_ref[1:2, :] = eb
    wts_ref[0:1, :] = wa
    wts_ref[1:2, :] = wb


def _proj_ln_router(o, x2, mod_rows, layer, w, b, ln_g, ln_b, w_router, b_router, seq):
    t, k = o.shape
    tm = 256
    steps_per_batch = seq // tm
    d = D_MODEL
    return pl.pallas_call(
        _proj_body,
        grid=(t // tm,),
        in_specs=[
            pl.BlockSpec((tm, k), lambda i: (i, 0)),
            pl.BlockSpec((tm, d), lambda i: (i, 0)),
            pl.BlockSpec((None, 1, N_MOD * d), lambda i: (layer * MOD_ROWS + i // steps_per_batch, 0, 0)),
            pl.BlockSpec((k, d), lambda i: (0, 0)),
            pl.BlockSpec((1, d), lambda i: (0, 0)),
            pl.BlockSpec((1, d), lambda i: (0, 0)),
            pl.BlockSpec((1, d), lambda i: (0, 0)),
            pl.BlockSpec((d, ROUTER_LANES), lambda i: (0, 0)),
            pl.BlockSpec((1, ROUTER_LANES), lambda i: (0, 0)),
        ],
        out_specs=[
            pl.BlockSpec((tm, d), lambda i: (i, 0)),
            pl.BlockSpec((tm * ROW_CHUNKS, LANES), lambda i: (i, 0)),
            pl.BlockSpec((2, tm), lambda i: (0, i)),
            pl.BlockSpec((2, tm), lambda i: (0, i)),
        ],
        out_shape=[
            jax.ShapeDtypeStruct((t, d), F32),
            jax.ShapeDtypeStruct((t * ROW_CHUNKS, LANES), F32),
            jax.ShapeDtypeStruct((2, t), I32),
            jax.ShapeDtypeStruct((2, t), F32),
        ],
        scratch_shapes=[pltpu.VMEM((k, d), BF16)],
        compiler_params=_cparams(1),
        name="proj_ln_router",
    )(o, x2, mod_rows, w, b.reshape(1, d), ln_g.reshape(1, d), ln_b.reshape(1, d), w_router, b_router)


def _sort_body(ids_ref, pos_ref, te_ref, nused_ref, cum_ref, rank_ref):
    n_rows = ids_ref.shape[0]
    c = SORT_CHUNK
    erow = lax.broadcasted_iota(I32, (N_EXPERTS, c), 0)
    tri = (lax.broadcasted_iota(I32, (c, c), 0) <= lax.broadcasted_iota(I32, (c, c), 1)).astype(BF16)

    def rank_step(r, carry):
        onehot = erow == ids_ref[pl.ds(r, 1), :]
        pref = jnp.dot(onehot.astype(BF16), tri, preferred_element_type=F32)
        rank = jnp.sum(jnp.where(onehot, pref + carry, 0.0), axis=0, keepdims=True) - 1.0
        rank_ref[pl.ds(r, 1), :] = rank
        return carry + pref[:, c - 1:c]

    counts = lax.fori_loop(0, n_rows, rank_step, jnp.zeros((N_EXPERTS, 1), F32))
    n_tile = jnp.floor((counts + (EXPERT_TILE - 1)) * (1.0 / EXPERT_TILE))
    low = (lax.broadcasted_iota(I32, (N_EXPERTS, N_EXPERTS), 1)
           <= lax.broadcasted_iota(I32, (N_EXPERTS, N_EXPERTS), 0)).astype(BF16)
    cum = jnp.dot(low, jnp.broadcast_to(n_tile, (N_EXPERTS, LANES)).astype(BF16),
                  preferred_element_type=F32)[:, 0:1]
    row_off = (cum - n_tile) * EXPERT_TILE

    def pos_step(r, _):
        onehot = erow == ids_ref[pl.ds(r, 1), :]
        off = jnp.sum(jnp.where(onehot, row_off, 0.0), axis=0, keepdims=True)
        pos_ref[pl.ds(r, 1), :] = (off + rank_ref[pl.ds(r, 1), :]).astype(I32)
        return 0

    lax.fori_loop(0, n_rows, pos_step, 0)
    total = jnp.max(cum, axis=0, keepdims=True)
    n_lanes = te_ref.shape[1]
    tile = jnp.minimum(lax.broadcasted_iota(I32, (N_EXPERTS, n_lanes), 1).astype(F32), total - 1.0)
    te_ref[...] = jnp.sum(jnp.where(cum <= tile, 1.0, 0.0), axis=0, keepdims=True).astype(I32)
    nused_ref[...] = jnp.broadcast_to(total, nused_ref.shape).astype(I32)
    le = (lax.broadcasted_iota(I32, (N_EXPERTS, LANES), 0) <= lax.broadcasted_iota(I32, (N_EXPERTS, LANES), 1))
    cum_ref[...] = jnp.sum(jnp.where(le, n_tile, 0.0), axis=0, keepdims=True).astype(I32)


def _expert_sort(ids, n_tiles):
    n_assign = ids.shape[0] * ids.shape[1]
    n_rows = n_assign // SORT_CHUNK
    te_lanes = -(-n_tiles // LANES) * LANES
    pos, te, nused, cum = pl.pallas_call(
        _sort_body,
        grid=(1,),
        in_specs=[pl.BlockSpec((n_rows, SORT_CHUNK), lambda i: (0, 0))],
        out_specs=[
            pl.BlockSpec((n_rows, SORT_CHUNK), lambda i: (0, 0)),
            pl.BlockSpec((1, te_lanes), lambda i: (0, 0)),
            pl.BlockSpec((1, LANES), lambda i: (0, 0)),
            pl.BlockSpec((1, LANES), lambda i: (0, 0)),
        ],
        out_shape=[
            jax.ShapeDtypeStruct((n_rows, SORT_CHUNK), I32),
            jax.ShapeDtypeStruct((1, te_lanes), I32),
            jax.ShapeDtypeStruct((1, LANES), I32),
            jax.ShapeDtypeStruct((1, LANES), I32),
        ],
        scratch_shapes=[pltpu.VMEM((n_rows, SORT_CHUNK), F32)],
        compiler_params=_cparams(1),
        name="expert_sort",
    )(ids.reshape(n_rows, SORT_CHUNK))
    return pos, te[0, :n_tiles], nused[0, :1], cum[0, :N_EXPERTS]


def _row_copy(src, src_row, dst, dst_row, sem):
    s0 = pl.multiple_of(src_row * ROW_CHUNKS, ROW_CHUNKS)
    d0 = pl.multiple_of(dst_row * ROW_CHUNKS, ROW_CHUNKS)
    return pltpu.make_async_copy(src.at[pl.ds(s0, ROW_CHUNKS)], dst.at[pl.ds(d0, ROW_CHUNKS)], sem)


def _dispatch_body(cum_ref, nused_ref, pos_ref, h_ref, xs_ref, zbuf, sem, zsem, *, n_tiles):
    tm = pos_ref.shape[1]
    tile_rows = EXPERT_TILE * ROW_CHUNKS

    def zero_tile(tile):
        r0 = pl.multiple_of(tile * tile_rows, tile_rows)
        return pltpu.make_async_copy(zbuf, xs_ref.at[pl.ds(r0, tile_rows)], zsem)

    @pl.when(pl.program_id(0) == 0)
    def _():
        zbuf[...] = jnp.zeros(zbuf.shape, zbuf.dtype)
        n_used = nused_ref[0]

        def zero_last(e, _):
            zero_tile(jnp.maximum(cum_ref[e] - 1, 0)).start()
            return 0

        def zero_tail(tl, _):
            zero_tile(tl).start()
            return 0

        def zero_wait(_, c):
            zero_tile(0).wait()
            return c

        lax.fori_loop(0, N_EXPERTS, zero_last, 0)
        lax.fori_loop(n_used, n_tiles, zero_tail, 0)
        lax.fori_loop(0, N_EXPERTS + n_tiles - n_used, zero_wait, 0)

    def issue(j, _):
        for k in range(2):
            _row_copy(h_ref, j, xs_ref, pos_ref[k, j], sem).start()
        return 0

    lax.fori_loop(0, tm, issue, 0)

    def drain(j, _):
        for k in range(2):
            _row_copy(h_ref, 0, xs_ref, 0, sem).wait()
        return 0

    lax.fori_loop(0, tm, drain, 0)


def _dispatch(pos2, cum, n_used, h2, n_rows):
    t = h2.shape[0] // ROW_CHUNKS
    tm = 256
    grid_spec = pltpu.PrefetchScalarGridSpec(
        num_scalar_prefetch=2,
        grid=(t // tm,),
        in_specs=[
            pl.BlockSpec((2, tm), lambda i, cu, nu: (0, i), memory_space=pltpu.SMEM),
            pl.BlockSpec((tm * ROW_CHUNKS, LANES), lambda i, cu, nu: (i, 0)),
        ],
        out_specs=pl.BlockSpec(memory_space=pl.ANY),
        scratch_shapes=[pltpu.VMEM((EXPERT_TILE * ROW_CHUNKS, LANES), h2.dtype),
                        pltpu.SemaphoreType.DMA, pltpu.SemaphoreType.DMA],
    )
    return pl.pallas_call(
        functools.partial(_dispatch_body, n_tiles=n_rows // EXPERT_TILE),
        grid_spec=grid_spec,
        out_shape=jax.ShapeDtypeStruct((n_rows * ROW_CHUNKS, LANES), h2.dtype),
        compiler_params=_cparams(1),
        name="moe_dispatch",
    )(cum, n_used, pos2, h2)


def _expert_body(te_ref, nused_ref, xs_ref, wgu_ref, wd_ref, ys_ref, wgu_bf, wd_bf):
    i = pl.program_id(0)
    used = i < nused_ref[0]
    prev = te_ref[jnp.maximum(i - 1, 0)]

    @pl.when(used & ((i == 0) | (te_ref[i] != prev)))
    def _():
        wgu_bf[...] = wgu_ref[...].astype(BF16)
        wd_bf[...] = wd_ref[...].astype(BF16)

    @pl.when(used)
    def _():
        xs = _load_rows(xs_ref, EXPERT_TILE).astype(BF16)
        gu = jnp.dot(xs, wgu_bf[...], preferred_element_type=F32)
        gate = gu[:, :EXPERT_FF]
        up = gu[:, EXPERT_FF:]
        act = gate * jax.nn.sigmoid(gate) * up
        _store_rows(ys_ref, jnp.dot(act.astype(BF16), wd_bf[...], preferred_element_type=F32))

    @pl.when(jnp.logical_not(used))
    def _():
        ys_ref[...] = jnp.zeros(ys_ref.shape, ys_ref.dtype)


def _expert_mlp(tile_expert, n_used, xs, w_gate_up, w_down):
    d = D_MODEL
    n_tiles = xs.shape[0] // (EXPERT_TILE * ROW_CHUNKS)
    f2 = 2 * EXPERT_FF
    tile_rows = EXPERT_TILE * ROW_CHUNKS
    grid_spec = pltpu.PrefetchScalarGridSpec(
        num_scalar_prefetch=2,
        grid=(n_tiles,),
        in_specs=[
            pl.BlockSpec((tile_rows, LANES), lambda i, te, nu: (jnp.minimum(i, nu[0] - 1), 0)),
            pl.BlockSpec((None, d, f2), lambda i, te, nu: (te[i], 0, 0)),
            pl.BlockSpec((None, EXPERT_FF, d), lambda i, te, nu: (te[i], 0, 0)),
        ],
        out_specs=pl.BlockSpec((tile_rows, LANES), lambda i, te, nu: (i, 0)),
        scratch_shapes=[pltpu.VMEM((d, f2), BF16), pltpu.VMEM((EXPERT_FF, d), BF16)],
    )
    return pl.pallas_call(
        _expert_body,
        grid_spec=grid_spec,
        out_shape=jax.ShapeDtypeStruct(xs.shape, F32),
        compiler_params=_cparams(1),
        name="expert_mlp",
    )(tile_expert, n_used, xs, w_gate_up.reshape(N_EXPERTS, d, f2), w_down.reshape(N_EXPERTS, EXPERT_FF, d))


def _combine_body(pos_ref, wts_ref, x_ref, mod_ref, lng_ref, lnb_ref, ys_ref, o_ref, buf, sem):
    tm = x_ref.shape[0]
    d = D_MODEL

    def issue(j, _):
        for k in range(2):
            _row_copy(ys_ref, pos_ref[k, j], buf.at[k], j, sem).start()
        return 0

    lax.fori_loop(0, tm, issue, 0)

    def drain(j, _):
        for k in range(2):
            _row_copy(ys_ref, 0, buf.at[k], 0, sem).wait()
        return 0

    lax.fori_loop(0, tm, drain, 0)

    cols = []
    for k in range(2):
        wt = jnp.broadcast_to(wts_ref[k:k + 1, :], (LANES, tm)).T
        cols.append(jnp.concatenate([wt] * (d // LANES), axis=1))
    y = cols[0] * _load_rows(buf.at[0], tm) + cols[1] * _load_rows(buf.at[1], tm)
    gt_f = mod_ref[:, 5 * d:6 * d]
    o_ref[...] = _layer_norm(DEEPNORM_ALPHA * x_ref[...] + (1.0 + gt_f) * y, lng_ref[...], lnb_ref[...])


def _combine_ln(pos2, wts, x1, mod_rows, layer, ln_g, ln_b, ys, seq):
    t, d = x1.shape
    tm = 256
    steps_per_batch = seq // tm
    return pl.pallas_call(
        _combine_body,
        grid=(t // tm,),
        in_specs=[
            pl.BlockSpec((2, tm), lambda i: (0, i), memory_space=pltpu.SMEM),
            pl.BlockSpec((2, tm), lambda i: (0, i)),
            pl.BlockSpec((tm, d), lambda i: (i, 0)),
            pl.BlockSpec((None, 1, N_MOD * d), lambda i: (layer * MOD_ROWS + i // steps_per_batch, 0, 0)),
            pl.BlockSpec((1, d), lambda i: (0, 0)),
            pl.BlockSpec((1, d), lambda i: (0, 0)),
            pl.BlockSpec(memory_space=pl.ANY),
        ],
        out_specs=pl.BlockSpec((tm, d), lambda i: (i, 0)),
        out_shape=jax.ShapeDtypeStruct((t, d), F32),
        scratch_shapes=[pltpu.VMEM((2, tm * ROW_CHUNKS, LANES), ys.dtype), pltpu.SemaphoreType.DMA],
        compiler_params=_cparams(1),
        name="moe_combine_ln",
    )(pos2, wts, x1, mod_rows, ln_g.reshape(1, d), ln_b.reshape(1, d), ys)


def _gmlp_body(x_ref, mod_ref, w_ref, b_ref, g_ref, beta_ref, ws_ref, bs_ref, o_ref, ws_bf):
    @pl.when(pl.program_id(0) == 0)
    def _():
        tri = lax.broadcasted_iota(I32, (CHUNK, CHUNK), 0) >= lax.broadcasted_iota(I32, (CHUNK, CHUNK), 1)
        for g in range(N_SGU_GROUPS):
            ws_bf[g] = jnp.where(tri, ws_ref[g], 0.0).astype(BF16)

    tm = x_ref.shape[0]
    d = D_MODEL
    sh = mod_ref[:, 0:d]
    sc = mod_ref[:, d:2 * d]
    h = x_ref[...] * (1.0 + sc) + sh
    z = jnp.dot(h.astype(BF16), w_ref[...], preferred_element_type=F32) + b_ref[...]
    z = 0.5 * z * (1.0 + lax.erf(z * (2.0 ** -0.5)))
    u = z[:, :GMLP_WIDTH]
    v = _layer_norm(z[:, GMLP_WIDTH:], g_ref[...], beta_ref[...]).astype(BF16)
    for ci in range(tm // CHUNK):
        rows = slice(ci * CHUNK, (ci + 1) * CHUNK)
        for g in range(N_SGU_GROUPS):
            lanes = slice(g * SGU_GROUP_DIM, (g + 1) * SGU_GROUP_DIM)
            mixed = jnp.dot(ws_bf[g], v[rows, lanes], preferred_element_type=F32) + bs_ref[:, g:g + 1]
            o_ref[rows, lanes] = (u[rows, lanes] * mixed).astype(BF16)


def _gmlp_gate(x2, mod_rows, layer, w_in_bf, b_in, sgu_g, sgu_b, w_s, b_s, seq):
    t, d = x2.shape
    tm = 256
    steps_per_batch = seq // tm
    gw = GMLP_WIDTH
    return pl.pallas_call(
        _gmlp_body,
        grid=(t // tm,),
        in_specs=[
            pl.BlockSpec((tm, d), lambda i: (i, 0)),
            pl.BlockSpec((None, 1, N_MOD * d), lambda i: (layer * MOD_ROWS + i // steps_per_batch, 0, 0)),
            pl.BlockSpec((d, 2 * gw), lambda i: (0, 0)),
            pl.BlockSpec((1, 2 * gw), lambda i: (0, 0)),
            pl.BlockSpec((1, gw), lambda i: (0, 0)),
            pl.BlockSpec((1, gw), lambda i: (0, 0)),
            pl.BlockSpec((N_SGU_GROUPS, CHUNK, CHUNK), lambda i: (0, 0, 0)),
            pl.BlockSpec((CHUNK, N_SGU_GROUPS), lambda i: (0, 0)),
        ],
        out_specs=pl.BlockSpec((tm, gw), lambda i: (i, 0)),
        out_shape=jax.ShapeDtypeStruct((t, gw), BF16),
        scratch_shapes=[pltpu.VMEM((N_SGU_GROUPS, CHUNK, CHUNK), BF16)],
        compiler_params=_cparams(1),
        name="gmlp_gate",
    )(x2, mod_rows, w_in_bf, b_in.reshape(1, 2 * gw), sgu_g.reshape(1, gw), sgu_b.reshape(1, gw), w_s, b_s.T)


def _router_params(w_group, b_group, w_expert, b_expert):
    d = w_group.shape[0]
    w = jnp.zeros((d, ROUTER_LANES), F32)
    w = w.at[:, :N_EXPERT_GROUPS].set(w_group).at[:, EXPERT_COL0:EXPERT_COL0 + N_EXPERTS].set(w_expert)
    b = jnp.zeros((1, ROUTER_LANES), F32)
    b = b.at[0, :N_EXPERT_GROUPS].set(b_group).at[0, EXPERT_COL0:EXPERT_COL0 + N_EXPERTS].set(b_expert)
    return w, b


def _moe_block(x1, h2, ids, wts, mod_rows, layer, ln_g, ln_b, w_gate_up, w_down, seq):
    t = x1.shape[0]
    n_rows = 2 * t + N_EXPERTS * EXPERT_TILE
    n_tiles = n_rows // EXPERT_TILE
    pos, tile_expert, n_used, cum = _expert_sort(ids, n_tiles)
    pos2 = pos.reshape(2, t)
    xs = _dispatch(pos2, cum, n_used, h2, n_rows)
    ys = _expert_mlp(tile_expert, n_used, xs, w_gate_up, w_down)
    return _combine_ln(pos2, wts, x1, mod_rows, layer, ln_g, ln_b, ys, seq)


def kernel(x, c, positions, ada_w, ada_b, post_ln_g, post_ln_b, attn_w_qkv, attn_b_qkv, attn_sinks, attn_w_o, attn_b_o, gmlp_w_in, gmlp_b_in, gmlp_sgu_ln_g, gmlp_sgu_ln_b, gmlp_w_s, gmlp_b_s, gmlp_w_out, gmlp_b_out, moe_w_group_router, moe_b_group_router, moe_w_expert_router, moe_b_expert_router, moe_w_gate_up, moe_w_down):
    batch, seq, d = x.shape
    t = batch * seq
    x2 = x.reshape(t, d)
    c_pad = jnp.zeros((MOD_ROWS, d), F32).at[:batch].set(c)
    mod_rows = _adaln_mod(c_pad, ada_w, ada_b).reshape(DEPTH * MOD_ROWS, 1, N_MOD * d)

    for layer in range(DEPTH):
        j = layer // 2
        if layer % 2 == 0:
            qkv = _qkv_rope(x2, mod_rows, layer, positions, attn_w_qkv[j], attn_b_qkv[j], seq)
            mix = _attention(qkv, attn_sinks[j], batch, seq)
            w_out, b_out = attn_w_o[j], attn_b_o[j]
        else:
            mix = _gmlp_gate(x2, mod_rows, layer, gmlp_w_in[j].astype(BF16), gmlp_b_in[j], gmlp_sgu_ln_g[j],
                             gmlp_sgu_ln_b[j], gmlp_w_s[j], gmlp_b_s[j], seq)
            w_out, b_out = gmlp_w_out[j], gmlp_b_out[j]
        w_router, b_router = _router_params(moe_w_group_router[layer], moe_b_group_router[layer],
                                            moe_w_expert_router[layer], moe_b_expert_router[layer])
        x1, h2, ids, wts = _proj_ln_router(mix, x2, mod_rows, layer, w_out, b_out, post_ln_g[layer, 0],
                                           post_ln_b[layer, 0], w_router, b_router, seq)
        x2 = _moe_block(x1, h2, ids, wts, mod_rows, layer, post_ln_g[layer, 1], post_ln_b[layer, 1],
                        moe_w_gate_up[layer], moe_w_down[layer], seq)
    return x2.reshape(batch, seq, d)
```

```python
import functools

import jax
import jax.numpy as jnp
from jax import lax
from jax.experimental import pallas as pl
from jax.experimental.pallas import tpu as pltpu
from jax.experimental.pallas import tpu_sc as plsc

F32 = jnp.float32
BF16 = jnp.bfloat16
I32 = jnp.int32

D_MODEL = 1024
DEPTH = 2
HEAD_DIM = 64
N_Q_HEADS = 16
N_KV_HEADS = 4
GQA_GROUP = N_Q_HEADS // N_KV_HEADS
WINDOW = 128
ROPE_THETA = 10000.0
Q_WIDTH = N_Q_HEADS * HEAD_DIM
KV_WIDTH = N_KV_HEADS * HEAD_DIM
QKV_WIDTH = Q_WIDTH + 2 * KV_WIDTH
CHUNK = 128
GMLP_WIDTH = 2 * D_MODEL
N_SGU_GROUPS = 8
SGU_GROUP_DIM = GMLP_WIDTH // N_SGU_GROUPS
N_EXPERT_GROUPS = 4
EXPERTS_PER_GROUP = 8
N_EXPERTS = N_EXPERT_GROUPS * EXPERTS_PER_GROUP
EXPERT_FF = D_MODEL // 4
N_MOD = 6
DEEPNORM_ALPHA = (2.0 * DEPTH) ** 0.25
LN_EPS = 1e-5

LANES = 128
MOD_ROWS = 8
ROUTER_LANES = 128
EXPERT_COL0 = 8
SORT_CHUNK = 256
EXPERT_TILE = 256
VMEM_LIMIT = 56 * 1024 * 1024
SC_CORES = 2
SC_WORKERS = 32
SC_CHUNK = 32


def _cparams(n_axes, vmem=VMEM_LIMIT):
    return pltpu.CompilerParams(dimension_semantics=("arbitrary",) * n_axes, vmem_limit_bytes=vmem)


ROW_CHUNKS = D_MODEL // LANES


def _store_rows(ref, val):
    n = val.shape[0]
    for c in range(ROW_CHUNKS):
        ref[pl.ds(c, n, stride=ROW_CHUNKS), :] = val[:, c * LANES:(c + 1) * LANES]


def _load_rows(ref, n):
    return jnp.concatenate([ref[pl.ds(c, n, stride=ROW_CHUNKS), :] for c in range(ROW_CHUNKS)], axis=1)


def _layer_norm(r, g, b):
    mu = jnp.mean(r, axis=-1, keepdims=True)
    d = r - mu
    var = jnp.mean(d * d, axis=-1, keepdims=True)
    return d * lax.rsqrt(var + LN_EPS) * g + b


def _mod_body(c_ref, w_ref, b_ref, o_ref):
    c = c_ref[...]
    ca = c * jax.nn.sigmoid(c)
    o_ref[...] = jnp.dot(ca.astype(BF16), w_ref[...].astype(BF16), preferred_element_type=F32) + b_ref[...]


def _adaln_mod(c_pad, ada_w, ada_b):
    tn = 1536
    n_out = N_MOD * D_MODEL
    return pl.pallas_call(
        _mod_body,
        grid=(DEPTH, n_out // tn),
        in_specs=[
            pl.BlockSpec((MOD_ROWS, D_MODEL), lambda l, j: (0, 0)),
            pl.BlockSpec((None, D_MODEL, tn), lambda l, j: (l, 0, j)),
            pl.BlockSpec((None, 1, tn), lambda l, j: (l, 0, j)),
        ],
        out_specs=pl.BlockSpec((None, MOD_ROWS, tn), lambda l, j: (l, 0, j)),
        out_shape=jax.ShapeDtypeStruct((DEPTH, MOD_ROWS, n_out), F32),
        compiler_params=_cparams(2),
        name="adaln_mod",
    )(c_pad, ada_w, ada_b.reshape(DEPTH, 1, n_out))


def _qkv_body(x_ref, mod_ref, pos_ref, w_ref, b_ref, invf_ref, o_ref, wbf_ref):
    @pl.when(pl.program_id(0) == 0)
    def _():
        wbf_ref[...] = w_ref[...].astype(BF16)

    tm = x_ref.shape[0]
    sh = mod_ref[:, 0:D_MODEL]
    sc = mod_ref[:, D_MODEL:2 * D_MODEL]
    h = x_ref[...] * (1.0 + sc) + sh
    qkv = jnp.dot(h.astype(BF16), wbf_ref[...], preferred_element_type=F32) + b_ref[...]

    ang = invf_ref[...] * pos_ref[...].astype(F32)
    c = jnp.cos(ang)
    s = jnp.sin(ang)
    ct = jnp.concatenate([c, c, c, c], axis=0).T
    st = jnp.concatenate([-s, s, -s, s], axis=0).T
    lane = lax.broadcasted_iota(I32, (tm, LANES), 1)
    first_half = (lane & (HEAD_DIM // 2)) == 0
    n_rope = (Q_WIDTH + KV_WIDTH) // LANES
    for j in range(n_rope):
        blk = qkv[:, j * LANES:(j + 1) * LANES]
        rot = jnp.where(first_half, pltpu.roll(blk, LANES - HEAD_DIM // 2, 1), pltpu.roll(blk, HEAD_DIM // 2, 1))
        r = blk * ct + rot * st
        if j < Q_WIDTH // LANES:
            r = r * (HEAD_DIM ** -0.5)
        o_ref[:, j * LANES:(j + 1) * LANES] = r.astype(BF16)
    o_ref[:, Q_WIDTH + KV_WIDTH:] = qkv[:, Q_WIDTH + KV_WIDTH:].astype(BF16)


def _qkv_rope(x2, mod_rows, layer, positions, w_qkv, b_qkv, seq):
    t = x2.shape[0]
    tm = 512
    steps_per_batch = seq // tm
    inv_freq = ROPE_THETA ** (-jnp.arange(0, HEAD_DIM, 2, dtype=F32) / HEAD_DIM)
    return pl.pallas_call(
        _qkv_body,
        grid=(t // tm,),
        in_specs=[
            pl.BlockSpec((tm, D_MODEL), lambda i: (i, 0)),
            pl.BlockSpec((None, 1, N_MOD * D_MODEL), lambda i: (layer * MOD_ROWS + i // steps_per_batch, 0, 0)),
            pl.BlockSpec((None, 1, tm), lambda i: (i, 0, 0)),
            pl.BlockSpec((D_MODEL, QKV_WIDTH), lambda i: (0, 0)),
            pl.BlockSpec((1, QKV_WIDTH), lambda i: (0, 0)),
            pl.BlockSpec((HEAD_DIM // 2, 1), lambda i: (0, 0)),
        ],
        out_specs=pl.BlockSpec((tm, QKV_WIDTH), lambda i: (i, 0)),
        out_shape=jax.ShapeDtypeStruct((t, QKV_WIDTH), BF16),
        scratch_shapes=[pltpu.VMEM((D_MODEL, QKV_WIDTH), BF16)],
        compiler_params=_cparams(1),
        name="qkv_rope",
    )(x2, mod_rows, positions.reshape(t // tm, 1, tm), w_qkv, b_qkv.reshape(1, QKV_WIDTH),
      inv_freq.reshape(HEAD_DIM // 2, 1))


def _attn_body(sink_ref, q_ref, kvc_ref, kvp_ref, o_ref):
    n = pl.program_id(1)
    q = q_ref[...]
    kvc = kvc_ref[...]
    kvp = kvp_ref[...]
    kb = jnp.concatenate([kvp[:, :KV_WIDTH], kvc[:, :KV_WIDTH]], axis=0)
    vb = jnp.concatenate([kvp[:, KV_WIDTH:], kvc[:, KV_WIDTH:]], axis=0)
    qi = lax.broadcasted_iota(I32, (WINDOW, 2 * WINDOW), 0) + WINDOW
    kj = lax.broadcasted_iota(I32, (WINDOW, 2 * WINDOW), 1)
    mask = (kj <= qi) & (kj > qi - WINDOW) & ((kj >= WINDOW) | (n > 0))
    outs = []
    for h in range(N_Q_HEADS):
        g = h // GQA_GROUP
        qh = q[:, h * HEAD_DIM:(h + 1) * HEAD_DIM]
        kg = kb[:, g * HEAD_DIM:(g + 1) * HEAD_DIM]
        vg = vb[:, g * HEAD_DIM:(g + 1) * HEAD_DIM]
        s = lax.dot_general(qh, kg, (((1,), (1,)), ((), ())), preferred_element_type=F32)
        s = jnp.where(mask, s, -jnp.inf)
        sink = sink_ref[h]
        m = jnp.maximum(jnp.max(s, axis=-1, keepdims=True), sink)
        p = jnp.exp(s - m)
        denom = jnp.sum(p, axis=-1, keepdims=True) + jnp.exp(sink - m)
        o = jnp.dot(p.astype(BF16), vg, preferred_element_type=F32)
        outs.append(o / denom)
    o_ref[...] = jnp.concatenate(outs, axis=1).astype(BF16)


def _attention(qkv, sinks, batch, seq):
    t = qkv.shape[0]
    nb = seq // WINDOW
    kv_col = Q_WIDTH // (2 * KV_WIDTH)
    return pl.pallas_call(
        _attn_body,
        grid=(batch, nb),
        in_specs=[
            pl.BlockSpec(memory_space=pltpu.SMEM),
            pl.BlockSpec((WINDOW, Q_WIDTH), lambda b, n: (b * nb + n, 0)),
            pl.BlockSpec((WINDOW, 2 * KV_WIDTH), lambda b, n: (b * nb + n, kv_col)),
            pl.BlockSpec((WINDOW, 2 * KV_WIDTH), lambda b, n: (b * nb + jnp.maximum(n - 1, 0), kv_col)),
        ],
        out_specs=pl.BlockSpec((WINDOW, Q_WIDTH), lambda b, n: (b * nb + n, 0)),
        out_shape=jax.ShapeDtypeStruct((t, Q_WIDTH), BF16),
        compiler_params=_cparams(2),
        name="swa_attention",
    )(sinks, qkv, qkv, qkv)


def _route(lt):
    tm = lt.shape[1]
    row = lax.broadcasted_iota(I32, (EXPERTS_PER_GROUP, tm), 0)
    neg = -jnp.inf
    gl = jnp.where(row < N_EXPERT_GROUPS, lt[0:EXPERTS_PER_GROUP], neg)
    gm = jnp.max(gl, axis=0, keepdims=True)
    g_p = 1.0 / jnp.sum(jnp.exp(gl - gm), axis=0, keepdims=True)
    g_idx = jnp.min(jnp.where(gl == gm, row, EXPERTS_PER_GROUP), axis=0, keepdims=True)
    sel = lt[EXPERT_COL0 + (N_EXPERT_GROUPS - 1) * EXPERTS_PER_GROUP:EXPERT_COL0 + N_EXPERTS]
    for g in range(N_EXPERT_GROUPS - 2, -1, -1):
        lo = EXPERT_COL0 + g * EXPERTS_PER_GROUP
        sel = jnp.where(g_idx == g, lt[lo:lo + EXPERTS_PER_GROUP], sel)
    v1 = jnp.max(sel, axis=0, keepdims=True)
    i1 = jnp.min(jnp.where(sel == v1, row, EXPERTS_PER_GROUP), axis=0, keepdims=True)
    sel2 = jnp.where(row == i1, neg, sel)
    v2 = jnp.max(sel2, axis=0, keepdims=True)
    i2 = jnp.min(jnp.where(sel2 == v2, row, EXPERTS_PER_GROUP), axis=0, keepdims=True)
    e2 = jnp.exp(v2 - v1)
    w1 = g_p / (1.0 + e2)
    w2 = g_p * e2 / (1.0 + e2)
    base = g_idx * EXPERTS_PER_GROUP
    return base + i1, base + i2, w1, w2


def _proj_body(o_ref, x_ref, mod_ref, w_ref, b_ref, lng_ref, lnb_ref, wr_ref, br_ref,
               x1_ref, h2_ref, ids_ref, wts_ref, wbf_ref):
    @pl.when(pl.program_id(0) == 0)
    def _():
        wbf_ref[...] = w_ref[...].astype(BF16)

    d = D_MODEL
    y = jnp.dot(o_ref[...], wbf_ref[...], preferred_element_type=F32) + b_ref[...]
    gt_m = mod_ref[:, 2 * d:3 * d]
    x1 = _layer_norm(DEEPNORM_ALPHA * x_ref[...] + (1.0 + gt_m) * y, lng_ref[...], lnb_ref[...])
    x1_ref[...] = x1
    sh_f = mod_ref[:, 3 * d:4 * d]
    sc_f = mod_ref[:, 4 * d:5 * d]
    h2 = x1 * (1.0 + sc_f) + sh_f
    _store_rows(h2_ref, h2)
    logits = jnp.dot(h2.astype(BF16), wr_ref[...].astype(BF16), preferred_element_type=F32) + br_ref[...]
    ea, eb, wa, wb = _route(logits.T)
    ids_ref[0:1, :] = ea
    ids_ref[1:2, :] = eb
    wts_ref[0:1, :] = wa
    wts_ref[1:2, :] = wb


def _proj_ln_router(o, x2, mod_rows, layer, w, b, ln_g, ln_b, w_router, b_router, seq):
    t, k = o.shape
    tm = 256
    steps_per_batch = seq // tm
    d = D_MODEL
    return pl.pallas_call(
        _proj_body,
        grid=(t // tm,),
        in_specs=[
            pl.BlockSpec((tm, k), lambda i: (i, 0)),
            pl.BlockSpec((tm, d), lambda i: (i, 0)),
            pl.BlockSpec((None, 1, N_MOD * d), lambda i: (layer * MOD_ROWS + i // steps_per_batch, 0, 0)),
            pl.BlockSpec((k, d), lambda i: (0, 0)),
            pl.BlockSpec((1, d), lambda i: (0, 0)),
            pl.BlockSpec((1, d), lambda i: (0, 0)),
            pl.BlockSpec((1, d), lambda i: (0, 0)),
            pl.BlockSpec((d, ROUTER_LANES), lambda i: (0, 0)),
            pl.BlockSpec((1, ROUTER_LANES), lambda i: (0, 0)),
        ],
        out_specs=[
            pl.BlockSpec((tm, d), lambda i: (i, 0)),
            pl.BlockSpec((tm * ROW_CHUNKS, LANES), lambda i: (i, 0)),
            pl.BlockSpec((2, tm), lambda i: (0, i)),
            pl.BlockSpec((2, tm), lambda i: (0, i)),
        ],
        out_shape=[
            jax.ShapeDtypeStruct((t, d), F32),
            jax.ShapeDtypeStruct((t * ROW_CHUNKS, LANES), F32),
            jax.ShapeDtypeStruct((2, t), I32),
            jax.ShapeDtypeStruct((2, t), F32),
        ],
        scratch_shapes=[pltpu.VMEM((k, d), BF16)],
        compiler_params=_cparams(1),
        name="proj_ln_router",
    )(o, x2, mod_rows, w, b.reshape(1, d), ln_g.reshape(1, d), ln_b.reshape(1, d), w_router, b_router)


def _sort_body(ids_ref, pos_ref, te_ref, nused_ref, cum_ref, rank_ref):
    n_rows = ids_ref.shape[0]
    c = SORT_CHUNK
    erow = lax.broadcasted_iota(I32, (N_EXPERTS, c), 0)
    tri = (lax.broadcasted_iota(I32, (c, c), 0) <= lax.broadcasted_iota(I32, (c, c), 1)).astype(BF16)

    def rank_step(r, carry):
        onehot = erow == ids_ref[pl.ds(r, 1), :]
        pref = jnp.dot(onehot.astype(BF16), tri, preferred_element_type=F32)
        rank = jnp.sum(jnp.where(onehot, pref + carry, 0.0), axis=0, keepdims=True) - 1.0
        rank_ref[pl.ds(r, 1), :] = rank
        return carry + pref[:, c - 1:c]

    counts = lax.fori_loop(0, n_rows, rank_step, jnp.zeros((N_EXPERTS, 1), F32))
    n_tile = jnp.floor((counts + (EXPERT_TILE - 1)) * (1.0 / EXPERT_TILE))
    low = (lax.broadcasted_iota(I32, (N_EXPERTS, N_EXPERTS), 1)
           <= lax.broadcasted_iota(I32, (N_EXPERTS, N_EXPERTS), 0)).astype(BF16)
    cum = jnp.dot(low, jnp.broadcast_to(n_tile, (N_EXPERTS, LANES)).astype(BF16),
                  preferred_element_type=F32)[:, 0:1]
    row_off = (cum - n_tile) * EXPERT_TILE

    def pos_step(r, _):
        onehot = erow == ids_ref[pl.ds(r, 1), :]
        off = jnp.sum(jnp.where(onehot, row_off, 0.0), axis=0, keepdims=True)
        pos_ref[pl.ds(r, 1), :] = (off + rank_ref[pl.ds(r, 1), :]).astype(I32)
        return 0

    lax.fori_loop(0, n_rows, pos_step, 0)
    total = jnp.max(cum, axis=0, keepdims=True)
    n_lanes = te_ref.shape[1]
    tile = jnp.minimum(lax.broadcasted_iota(I32, (N_EXPERTS, n_lanes), 1).astype(F32), total - 1.0)
    te_ref[...] = jnp.sum(jnp.where(cum <= tile, 1.0, 0.0), axis=0, keepdims=True).astype(I32)
    nused_ref[...] = jnp.broadcast_to(total, nused_ref.shape).astype(I32)
    le = (lax.broadcasted_iota(I32, (N_EXPERTS, LANES), 0) <= lax.broadcasted_iota(I32, (N_EXPERTS, LANES), 1))
    cum_ref[...] = jnp.sum(jnp.where(le, n_tile, 0.0), axis=0, keepdims=True).astype(I32)


def _expert_sort(ids, n_tiles):
    n_assign = ids.shape[0] * ids.shape[1]
    n_rows = n_assign // SORT_CHUNK
    te_lanes = -(-n_tiles // LANES) * LANES
    pos, te, nused, cum = pl.pallas_call(
        _sort_body,
        grid=(1,),
        in_specs=[pl.BlockSpec((n_rows, SORT_CHUNK), lambda i: (0, 0))],
        out_specs=[
            pl.BlockSpec((n_rows, SORT_CHUNK), lambda i: (0, 0)),
            pl.BlockSpec((1, te_lanes), lambda i: (0, 0)),
            pl.BlockSpec((1, LANES), lambda i: (0, 0)),
            pl.BlockSpec((1, LANES), lambda i: (0, 0)),
        ],
        out_shape=[
            jax.ShapeDtypeStruct((n_rows, SORT_CHUNK), I32),
            jax.ShapeDtypeStruct((1, te_lanes), I32),
            jax.ShapeDtypeStruct((1, LANES), I32),
            jax.ShapeDtypeStruct((1, LANES), I32),
        ],
        scratch_shapes=[pltpu.VMEM((n_rows, SORT_CHUNK), F32)],
        compiler_params=_cparams(1),
        name="expert_sort",
    )(ids.reshape(n_rows, SORT_CHUNK))
    return pos, te[0, :n_tiles], nused[0, :1], cum[0, :N_EXPERTS]


def _row_copy(src, src_row, dst, dst_row, sem):
    s0 = pl.multiple_of(src_row * ROW_CHUNKS, ROW_CHUNKS)
    d0 = pl.multiple_of(dst_row * ROW_CHUNKS, ROW_CHUNKS)
    return pltpu.make_async_copy(src.at[pl.ds(s0, ROW_CHUNKS)], dst.at[pl.ds(d0, ROW_CHUNKS)], sem)


def _dispatch_body(cum_ref, nused_ref, pos_ref, h_ref, xs_ref, zbuf, sem, zsem, *, n_tiles):
    tm = pos_ref.shape[1]
    tile_rows = EXPERT_TILE * ROW_CHUNKS

    def zero_tile(tile):
        r0 = pl.multiple_of(tile * tile_rows, tile_rows)
        return pltpu.make_async_copy(zbuf, xs_ref.at[pl.ds(r0, tile_rows)], zsem)

    @pl.when(pl.program_id(0) == 0)
    def _():
        zbuf[...] = jnp.zeros(zbuf.shape, zbuf.dtype)
        n_used = nused_ref[0]

        def zero_last(e, _):
            zero_tile(jnp.maximum(cum_ref[e] - 1, 0)).start()
            return 0

        def zero_tail(tl, _):
            zero_tile(tl).start()
            return 0

        def zero_wait(_, c):
            zero_tile(0).wait()
            return c

        lax.fori_loop(0, N_EXPERTS, zero_last, 0)
        lax.fori_loop(n_used, n_tiles, zero_tail, 0)
        lax.fori_loop(0, N_EXPERTS + n_tiles - n_used, zero_wait, 0)

    def issue(j, _):
        for k in range(2):
            _row_copy(h_ref, j, xs_ref, pos_ref[k, j], sem).start()
        return 0

    lax.fori_loop(0, tm, issue, 0)

    def drain(j, _):
        for k in range(2):
            _row_copy(h_ref, 0, xs_ref, 0, sem).wait()
        return 0

    lax.fori_loop(0, tm, drain, 0)


def _dispatch(pos2, cum, n_used, h2, n_rows):
    t = h2.shape[0] // ROW_CHUNKS
    tm = 256
    grid_spec = pltpu.PrefetchScalarGridSpec(
        num_scalar_prefetch=2,
        grid=(t // tm,),
        in_specs=[
            pl.BlockSpec((2, tm), lambda i, cu, nu: (0, i), memory_space=pltpu.SMEM),
            pl.BlockSpec((tm * ROW_CHUNKS, LANES), lambda i, cu, nu: (i, 0)),
        ],
        out_specs=pl.BlockSpec(memory_space=pl.ANY),
        scratch_shapes=[pltpu.VMEM((EXPERT_TILE * ROW_CHUNKS, LANES), h2.dtype),
                        pltpu.SemaphoreType.DMA, pltpu.SemaphoreType.DMA],
    )
    return pl.pallas_call(
        functools.partial(_dispatch_body, n_tiles=n_rows // EXPERT_TILE),
        grid_spec=grid_spec,
        out_shape=jax.ShapeDtypeStruct((n_rows * ROW_CHUNKS, LANES), h2.dtype),
        compiler_params=_cparams(1),
        name="moe_dispatch",
    )(cum, n_used, pos2, h2)


def _expert_body(te_ref, nused_ref, xs_ref, wgu_ref, wd_ref, ys_ref, wgu_bf, wd_bf):
    i = pl.program_id(0)
    used = i < nused_ref[0]
    prev = te_ref[jnp.maximum(i - 1, 0)]

    @pl.when(used & ((i == 0) | (te_ref[i] != prev)))
    def _():
        wgu_bf[...] = wgu_ref[...].astype(BF16)
        wd_bf[...] = wd_ref[...].astype(BF16)

    @pl.when(used)
    def _():
        xs = _load_rows(xs_ref, EXPERT_TILE).astype(BF16)
        gu = jnp.dot(xs, wgu_bf[...], preferred_element_type=F32)
        gate = gu[:, :EXPERT_FF]
        up = gu[:, EXPERT_FF:]
        act = gate * jax.nn.sigmoid(gate) * up
        _store_rows(ys_ref, jnp.dot(act.astype(BF16), wd_bf[...], preferred_element_type=F32))

    @pl.when(jnp.logical_not(used))
    def _():
        ys_ref[...] = jnp.zeros(ys_ref.shape, ys_ref.dtype)


def _expert_mlp(tile_expert, n_used, xs, w_gate_up, w_down, layer):
    d = D_MODEL
    n_tiles = xs.shape[0] // (EXPERT_TILE * ROW_CHUNKS)
    f2 = 2 * EXPERT_FF
    tile_rows = EXPERT_TILE * ROW_CHUNKS
    e0 = layer * N_EXPERTS
    grid_spec = pltpu.PrefetchScalarGridSpec(
        num_scalar_prefetch=2,
        grid=(n_tiles,),
        in_specs=[
            pl.BlockSpec((tile_rows, LANES), lambda i, te, nu: (jnp.minimum(i, nu[0] - 1), 0)),
            pl.BlockSpec((None, d, f2), lambda i, te, nu: (e0 + te[i], 0, 0)),
            pl.BlockSpec((None, EXPERT_FF, d), lambda i, te, nu: (e0 + te[i], 0, 0)),
        ],
        out_specs=pl.BlockSpec((tile_rows, LANES), lambda i, te, nu: (i, 0)),
        scratch_shapes=[pltpu.VMEM((d, f2), BF16), pltpu.VMEM((EXPERT_FF, d), BF16)],
    )
    return pl.pallas_call(
        _expert_body,
        grid_spec=grid_spec,
        out_shape=jax.ShapeDtypeStruct(xs.shape, F32),
        compiler_params=_cparams(1),
        name="expert_mlp",
    )(tile_expert, n_used, xs, w_gate_up.reshape(DEPTH * N_EXPERTS, d, f2),
      w_down.reshape(DEPTH * N_EXPERTS, EXPERT_FF, d))


def _sc_gather_rows(table, idx):
    n = idx.shape[0]
    table3 = table.reshape(-1, ROW_CHUNKS, LANES)
    per_worker = n // SC_WORKERS
    n_chunks = per_worker // SC_CHUNK
    mesh = plsc.VectorSubcoreMesh(core_axis_name="c", subcore_axis_name="s", num_cores=SC_CORES,
                                  num_subcores=SC_WORKERS // SC_CORES)

    @functools.partial(
        pl.kernel, mesh=mesh,
        out_type=jax.ShapeDtypeStruct((n, ROW_CHUNKS, LANES), table.dtype),
        scratch_types=[
            pltpu.VMEM((n_chunks, SC_CHUNK), I32),
            pltpu.VMEM((SC_CHUNK, ROW_CHUNKS, LANES), table.dtype),
            pltpu.SemaphoreType.DMA,
        ],
        name="sc_gather_rows",
    )
    def gather(table_hbm, idx_hbm, out_hbm, idx_v, rows_v, sem):
        wid = lax.axis_index("s") * SC_CORES + lax.axis_index("c")
        pltpu.sync_copy(idx_hbm.at[wid], idx_v)

        @pl.loop(0, n_chunks)
        def _(j):
            pltpu.async_copy(table_hbm.at[idx_v.at[j]], rows_v, sem).wait()
            pltpu.sync_copy(rows_v, out_hbm.at[pl.ds(wid * per_worker + j * SC_CHUNK, SC_CHUNK)])

    out = gather(table3, idx.reshape(SC_WORKERS, n_chunks, SC_CHUNK))
    return out.reshape(n * ROW_CHUNKS, LANES)


def _combine_body(wts_ref, x_ref, mod_ref, lng_ref, lnb_ref, ya_ref, yb_ref, o_ref):
    tm = x_ref.shape[0]
    d = D_MODEL
    cols = []
    for k in range(2):
        wt = jnp.broadcast_to(wts_ref[k:k + 1, :], (LANES, tm)).T
        cols.append(jnp.concatenate([wt] * (d // LANES), axis=1))
    y = cols[0] * _load_rows(ya_ref, tm) + cols[1] * _load_rows(yb_ref, tm)
    gt_f = mod_ref[:, 5 * d:6 * d]
    o_ref[...] = _layer_norm(DEEPNORM_ALPHA * x_ref[...] + (1.0 + gt_f) * y, lng_ref[...], lnb_ref[...])


def _combine_ln(pos2, wts, x1, mod_rows, layer, ln_g, ln_b, ys, seq):
    t, d = x1.shape
    tm = 256
    steps_per_batch = seq // tm
    n_steps = t // tm
    yg = _sc_gather_rows(ys, pos2.reshape(2 * t))
    return pl.pallas_call(
        _combine_body,
        grid=(n_steps,),
        in_specs=[
            pl.BlockSpec((2, tm), lambda i: (0, i)),
            pl.BlockSpec((tm, d), lambda i: (i, 0)),
            pl.BlockSpec((None, 1, N_MOD * d), lambda i: (layer * MOD_ROWS + i // steps_per_batch, 0, 0)),
            pl.BlockSpec((1, d), lambda i: (0, 0)),
            pl.BlockSpec((1, d), lambda i: (0, 0)),
            pl.BlockSpec((tm * ROW_CHUNKS, LANES), lambda i: (i, 0)),
            pl.BlockSpec((tm * ROW_CHUNKS, LANES), lambda i: (n_steps + i, 0)),
        ],
        out_specs=pl.BlockSpec((tm, d), lambda i: (i, 0)),
        out_shape=jax.ShapeDtypeStruct((t, d), F32),
        compiler_params=_cparams(1),
        name="moe_combine_ln",
    )(wts, x1, mod_rows, ln_g.reshape(1, d), ln_b.reshape(1, d), yg, yg)


def _gmlp_body(x_ref, mod_ref, w_ref, b_ref, g_ref, beta_ref, ws_ref, bs_ref, o_ref, ws_bf):
    @pl.when(pl.program_id(0) == 0)
    def _():
        tri = lax.broadcasted_iota(I32, (CHUNK, CHUNK), 0) >= lax.broadcasted_iota(I32, (CHUNK, CHUNK), 1)
        for g in range(N_SGU_GROUPS):
            ws_bf[g] = jnp.where(tri, ws_ref[g], 0.0).astype(BF16)

    tm = x_ref.shape[0]
    d = D_MODEL
    sh = mod_ref[:, 0:d]
    sc = mod_ref[:, d:2 * d]
    h = x_ref[...] * (1.0 + sc) + sh
    z = jnp.dot(h.astype(BF16), w_ref[...], preferred_element_type=F32) + b_ref[...]
    z = 0.5 * z * (1.0 + lax.erf(z * (2.0 ** -0.5)))
    u = z[:, :GMLP_WIDTH]
    v = _layer_norm(z[:, GMLP_WIDTH:], g_ref[...], beta_ref[...]).astype(BF16)
    for ci in range(tm // CHUNK):
        rows = slice(ci * CHUNK, (ci + 1) * CHUNK)
        for g in range(N_SGU_GROUPS):
            lanes = slice(g * SGU_GROUP_DIM, (g + 1) * SGU_GROUP_DIM)
            mixed = jnp.dot(ws_bf[g], v[rows, lanes], preferred_element_type=F32) + bs_ref[:, g:g + 1]
            o_ref[rows, lanes] = (u[rows, lanes] * mixed).astype(BF16)


def _gmlp_gate(x2, mod_rows, layer, w_in_bf, b_in, sgu_g, sgu_b, w_s, b_s, seq):
    t, d = x2.shape
    tm = 256
    steps_per_batch = seq // tm
    gw = GMLP_WIDTH
    return pl.pallas_call(
        _gmlp_body,
        grid=(t // tm,),
        in_specs=[
            pl.BlockSpec((tm, d), lambda i: (i, 0)),
            pl.BlockSpec((None, 1, N_MOD * d), lambda i: (layer * MOD_ROWS + i // steps_per_batch, 0, 0)),
            pl.BlockSpec((d, 2 * gw), lambda i: (0, 0)),
            pl.BlockSpec((1, 2 * gw), lambda i: (0, 0)),
            pl.BlockSpec((1, gw), lambda i: (0, 0)),
            pl.BlockSpec((1, gw), lambda i: (0, 0)),
            pl.BlockSpec((N_SGU_GROUPS, CHUNK, CHUNK), lambda i: (0, 0, 0)),
            pl.BlockSpec((CHUNK, N_SGU_GROUPS), lambda i: (0, 0)),
        ],
        out_specs=pl.BlockSpec((tm, gw), lambda i: (i, 0)),
        out_shape=jax.ShapeDtypeStruct((t, gw), BF16),
        scratch_shapes=[pltpu.VMEM((N_SGU_GROUPS, CHUNK, CHUNK), BF16)],
        compiler_params=_cparams(1),
        name="gmlp_gate",
    )(x2, mod_rows, w_in_bf, b_in.reshape(1, 2 * gw), sgu_g.reshape(1, gw), sgu_b.reshape(1, gw), w_s, b_s.T)


def _router_params(w_group, b_group, w_expert, b_expert):
    d = w_group.shape[0]
    w = jnp.zeros((d, ROUTER_LANES), F32)
    w = w.at[:, :N_EXPERT_GROUPS].set(w_group).at[:, EXPERT_COL0:EXPERT_COL0 + N_EXPERTS].set(w_expert)
    b = jnp.zeros((1, ROUTER_LANES), F32)
    b = b.at[0, :N_EXPERT_GROUPS].set(b_group).at[0, EXPERT_COL0:EXPERT_COL0 + N_EXPERTS].set(b_expert)
    return w, b


def _moe_block(x1, h2, ids, wts, mod_rows, layer, ln_g, ln_b, w_gate_up, w_down, seq):
    t = x1.shape[0]
    n_rows = 2 * t + N_EXPERTS * EXPERT_TILE
    n_tiles = n_rows // EXPERT_TILE
    pos, tile_expert, n_used, cum = _expert_sort(ids, n_tiles)
    pos2 = pos.reshape(2, t)
    xs = _dispatch(pos2, cum, n_used, h2, n_rows)
    ys = _expert_mlp(tile_expert, n_used, xs, w_gate_up, w_down, layer)
    return _combine_ln(pos2, wts, x1, mod_rows, layer, ln_g, ln_b, ys, seq)


def kernel(x, c, positions, ada_w, ada_b, post_ln_g, post_ln_b, attn_w_qkv, attn_b_qkv, attn_sinks, attn_w_o, attn_b_o, gmlp_w_in, gmlp_b_in, gmlp_sgu_ln_g, gmlp_sgu_ln_b, gmlp_w_s, gmlp_b_s, gmlp_w_out, gmlp_b_out, moe_w_group_router, moe_b_group_router, moe_w_expert_router, moe_b_expert_router, moe_w_gate_up, moe_w_down):
    batch, seq, d = x.shape
    t = batch * seq
    x2 = x.reshape(t, d)
    c_pad = jnp.zeros((MOD_ROWS, d), F32).at[:batch].set(c)
    mod_rows = _adaln_mod(c_pad, ada_w, ada_b).reshape(DEPTH * MOD_ROWS, 1, N_MOD * d)

    for layer in range(DEPTH):
        j = layer // 2
        if layer % 2 == 0:
            qkv = _qkv_rope(x2, mod_rows, layer, positions, attn_w_qkv[j], attn_b_qkv[j], seq)
            mix = _attention(qkv, attn_sinks[j], batch, seq)
            w_out, b_out = attn_w_o[j], attn_b_o[j]
        else:
            mix = _gmlp_gate(x2, mod_rows, layer, gmlp_w_in[j].astype(BF16), gmlp_b_in[j], gmlp_sgu_ln_g[j],
                             gmlp_sgu_ln_b[j], gmlp_w_s[j], gmlp_b_s[j], seq)
            w_out, b_out = gmlp_w_out[j], gmlp_b_out[j]
        w_router, b_router = _router_params(moe_w_group_router[layer], moe_b_group_router[layer],
                                            moe_w_expert_router[layer], moe_b_expert_router[layer])
        x1, h2, ids, wts = _proj_ln_router(mix, x2, mod_rows, layer, w_out, b_out, post_ln_g[layer, 0],
                                           post_ln_b[layer, 0], w_router, b_router, seq)
        x2 = _moe_block(x1, h2, ids, wts, mod_rows, layer, post_ln_g[layer, 1], post_ln_b[layer, 1],
                        moe_w_gate_up, moe_w_down, seq)
    return x2.reshape(batch, seq, d)
```

```python
import functools

import jax
import jax.numpy as jnp
from jax import lax
from jax.experimental import pallas as pl
from jax.experimental.pallas import tpu as pltpu
from jax.experimental.pallas import tpu_sc as plsc

F32 = jnp.float32
BF16 = jnp.bfloat16
I32 = jnp.int32

D_MODEL = 1024
DEPTH = 2
HEAD_DIM = 64
N_Q_HEADS = 16
N_KV_HEADS = 4
GQA_GROUP = N_Q_HEADS // N_KV_HEADS
WINDOW = 128
ROPE_THETA = 10000.0
Q_WIDTH = N_Q_HEADS * HEAD_DIM
KV_WIDTH = N_KV_HEADS * HEAD_DIM
QKV_WIDTH = Q_WIDTH + 2 * KV_WIDTH
CHUNK = 128
GMLP_WIDTH = 2 * D_MODEL
N_SGU_GROUPS = 8
SGU_GROUP_DIM = GMLP_WIDTH // N_SGU_GROUPS
N_EXPERT_GROUPS = 4
EXPERTS_PER_GROUP = 8
N_EXPERTS = N_EXPERT_GROUPS * EXPERTS_PER_GROUP
EXPERT_FF = D_MODEL // 4
N_MOD = 6
DEEPNORM_ALPHA = (2.0 * DEPTH) ** 0.25
LN_EPS = 1e-5

LANES = 128
MOD_ROWS = 8
ROUTER_LANES = 128
EXPERT_COL0 = 8
SORT_CHUNK = 256
EXPERT_TILE = 256
VMEM_LIMIT = 56 * 1024 * 1024
SC_CORES = 2
SC_WORKERS = 32
SC_CHUNK = 32


def _cparams(n_axes, vmem=VMEM_LIMIT):
    return pltpu.CompilerParams(dimension_semantics=("arbitrary",) * n_axes, vmem_limit_bytes=vmem)


ROW_CHUNKS = D_MODEL // LANES


def _store_rows(ref, val):
    n = val.shape[0]
    for c in range(ROW_CHUNKS):
        ref[pl.ds(c, n, stride=ROW_CHUNKS), :] = val[:, c * LANES:(c + 1) * LANES]


def _load_rows(ref, n):
    return jnp.concatenate([ref[pl.ds(c, n, stride=ROW_CHUNKS), :] for c in range(ROW_CHUNKS)], axis=1)


def _layer_norm(r, g, b):
    mu = jnp.mean(r, axis=-1, keepdims=True)
    d = r - mu
    var = jnp.mean(d * d, axis=-1, keepdims=True)
    return d * lax.rsqrt(var + LN_EPS) * g + b


def _mod_body(c_ref, w_ref, b_ref, o_ref):
    c = c_ref[...]
    ca = c * jax.nn.sigmoid(c)
    o_ref[...] = jnp.dot(ca.astype(BF16), w_ref[...].astype(BF16), preferred_element_type=F32) + b_ref[...]


def _adaln_mod(c_pad, ada_w, ada_b):
    tn = 1536
    n_out = N_MOD * D_MODEL
    return pl.pallas_call(
        _mod_body,
        grid=(DEPTH, n_out // tn),
        in_specs=[
            pl.BlockSpec((MOD_ROWS, D_MODEL), lambda l, j: (0, 0)),
            pl.BlockSpec((None, D_MODEL, tn), lambda l, j: (l, 0, j)),
            pl.BlockSpec((None, 1, tn), lambda l, j: (l, 0, j)),
        ],
        out_specs=pl.BlockSpec((None, MOD_ROWS, tn), lambda l, j: (l, 0, j)),
        out_shape=jax.ShapeDtypeStruct((DEPTH, MOD_ROWS, n_out), F32),
        compiler_params=_cparams(2),
        name="adaln_mod",
    )(c_pad, ada_w, ada_b.reshape(DEPTH, 1, n_out))


def _qkv_body(x_ref, mod_ref, pos_ref, w_ref, b_ref, invf_ref, o_ref, wbf_ref):
    @pl.when(pl.program_id(0) == 0)
    def _():
        wbf_ref[...] = w_ref[...].astype(BF16)

    tm = x_ref.shape[0]
    sh = mod_ref[:, 0:D_MODEL]
    sc = mod_ref[:, D_MODEL:2 * D_MODEL]
    h = x_ref[...] * (1.0 + sc) + sh
    qkv = jnp.dot(h.astype(BF16), wbf_ref[...], preferred_element_type=F32) + b_ref[...]

    ang = invf_ref[...] * pos_ref[...].astype(F32)
    c = jnp.cos(ang)
    s = jnp.sin(ang)
    ct = jnp.concatenate([c, c, c, c], axis=0).T
    st = jnp.concatenate([-s, s, -s, s], axis=0).T
    lane = lax.broadcasted_iota(I32, (tm, LANES), 1)
    first_half = (lane & (HEAD_DIM // 2)) == 0
    n_rope = (Q_WIDTH + KV_WIDTH) // LANES
    for j in range(n_rope):
        blk = qkv[:, j * LANES:(j + 1) * LANES]
        rot = jnp.where(first_half, pltpu.roll(blk, LANES - HEAD_DIM // 2, 1), pltpu.roll(blk, HEAD_DIM // 2, 1))
        r = blk * ct + rot * st
        if j < Q_WIDTH // LANES:
            r = r * (HEAD_DIM ** -0.5)
        o_ref[:, j * LANES:(j + 1) * LANES] = r.astype(BF16)
    o_ref[:, Q_WIDTH + KV_WIDTH:] = qkv[:, Q_WIDTH + KV_WIDTH:].astype(BF16)


def _qkv_rope(x2, mod_rows, layer, positions, w_qkv, b_qkv, seq):
    t = x2.shape[0]
    tm = 512
    steps_per_batch = seq // tm
    inv_freq = ROPE_THETA ** (-jnp.arange(0, HEAD_DIM, 2, dtype=F32) / HEAD_DIM)
    return pl.pallas_call(
        _qkv_body,
        grid=(t // tm,),
        in_specs=[
            pl.BlockSpec((tm, D_MODEL), lambda i: (i, 0)),
            pl.BlockSpec((None, 1, N_MOD * D_MODEL), lambda i: (layer * MOD_ROWS + i // steps_per_batch, 0, 0)),
            pl.BlockSpec((None, 1, tm), lambda i: (i, 0, 0)),
            pl.BlockSpec((D_MODEL, QKV_WIDTH), lambda i: (0, 0)),
            pl.BlockSpec((1, QKV_WIDTH), lambda i: (0, 0)),
            pl.BlockSpec((HEAD_DIM // 2, 1), lambda i: (0, 0)),
        ],
        out_specs=pl.BlockSpec((tm, QKV_WIDTH), lambda i: (i, 0)),
        out_shape=jax.ShapeDtypeStruct((t, QKV_WIDTH), BF16),
        scratch_shapes=[pltpu.VMEM((D_MODEL, QKV_WIDTH), BF16)],
        compiler_params=_cparams(1),
        name="qkv_rope",
    )(x2, mod_rows, positions.reshape(t // tm, 1, tm), w_qkv, b_qkv.reshape(1, QKV_WIDTH),
      inv_freq.reshape(HEAD_DIM // 2, 1))


def _attn_body(sink_ref, q_ref, kvc_ref, kvp_ref, o_ref):
    n = pl.program_id(1)
    q = q_ref[...]
    kvc = kvc_ref[...]
    kvp = kvp_ref[...]
    kb = jnp.concatenate([kvp[:, :KV_WIDTH], kvc[:, :KV_WIDTH]], axis=0)
    vb = jnp.concatenate([kvp[:, KV_WIDTH:], kvc[:, KV_WIDTH:]], axis=0)
    qi = lax.broadcasted_iota(I32, (WINDOW, 2 * WINDOW), 0) + WINDOW
    kj = lax.broadcasted_iota(I32, (WINDOW, 2 * WINDOW), 1)
    mask = (kj <= qi) & (kj > qi - WINDOW) & ((kj >= WINDOW) | (n > 0))
    outs = []
    for h in range(N_Q_HEADS):
        g = h // GQA_GROUP
        qh = q[:, h * HEAD_DIM:(h + 1) * HEAD_DIM]
        kg = kb[:, g * HEAD_DIM:(g + 1) * HEAD_DIM]
        vg = vb[:, g * HEAD_DIM:(g + 1) * HEAD_DIM]
        s = lax.dot_general(qh, kg, (((1,), (1,)), ((), ())), preferred_element_type=F32)
        s = jnp.where(mask, s, -jnp.inf)
        sink = sink_ref[h]
        m = jnp.maximum(jnp.max(s, axis=-1, keepdims=True), sink)
        p = jnp.exp(s - m)
        denom = jnp.sum(p, axis=-1, keepdims=True) + jnp.exp(sink - m)
        o = jnp.dot(p.astype(BF16), vg, preferred_element_type=F32)
        outs.append(o / denom)
    o_ref[...] = jnp.concatenate(outs, axis=1).astype(BF16)


def _attention(qkv, sinks, batch, seq):
    t = qkv.shape[0]
    nb = seq // WINDOW
    kv_col = Q_WIDTH // (2 * KV_WIDTH)
    return pl.pallas_call(
        _attn_body,
        grid=(batch, nb),
        in_specs=[
            pl.BlockSpec(memory_space=pltpu.SMEM),
            pl.BlockSpec((WINDOW, Q_WIDTH), lambda b, n: (b * nb + n, 0)),
            pl.BlockSpec((WINDOW, 2 * KV_WIDTH), lambda b, n: (b * nb + n, kv_col)),
            pl.BlockSpec((WINDOW, 2 * KV_WIDTH), lambda b, n: (b * nb + jnp.maximum(n - 1, 0), kv_col)),
        ],
        out_specs=pl.BlockSpec((WINDOW, Q_WIDTH), lambda b, n: (b * nb + n, 0)),
        out_shape=jax.ShapeDtypeStruct((t, Q_WIDTH), BF16),
        compiler_params=_cparams(2),
        name="swa_attention",
    )(sinks, qkv, qkv, qkv)


def _route(lt):
    tm = lt.shape[1]
    row = lax.broadcasted_iota(I32, (EXPERTS_PER_GROUP, tm), 0)
    neg = -jnp.inf
    gl = jnp.where(row < N_EXPERT_GROUPS, lt[0:EXPERTS_PER_GROUP], neg)
    gm = jnp.max(gl, axis=0, keepdims=True)
    g_p = 1.0 / jnp.sum(jnp.exp(gl - gm), axis=0, keepdims=True)
    g_idx = jnp.min(jnp.where(gl == gm, row, EXPERTS_PER_GROUP), axis=0, keepdims=True)
    sel = lt[EXPERT_COL0 + (N_EXPERT_GROUPS - 1) * EXPERTS_PER_GROUP:EXPERT_COL0 + N_EXPERTS]
    for g in range(N_EXPERT_GROUPS - 2, -1, -1):
        lo = EXPERT_COL0 + g * EXPERTS_PER_GROUP
        sel = jnp.where(g_idx == g, lt[lo:lo + EXPERTS_PER_GROUP], sel)
    v1 = jnp.max(sel, axis=0, keepdims=True)
    i1 = jnp.min(jnp.where(sel == v1, row, EXPERTS_PER_GROUP), axis=0, keepdims=True)
    sel2 = jnp.where(row == i1, neg, sel)
    v2 = jnp.max(sel2, axis=0, keepdims=True)
    i2 = jnp.min(jnp.where(sel2 == v2, row, EXPERTS_PER_GROUP), axis=0, keepdims=True)
    e2 = jnp.exp(v2 - v1)
    w1 = g_p / (1.0 + e2)
    w2 = g_p * e2 / (1.0 + e2)
    base = g_idx * EXPERTS_PER_GROUP
    return base + i1, base + i2, w1, w2


def _proj_body(o_ref, x_ref, mod_ref, w_ref, b_ref, lng_ref, lnb_ref, wr_ref, br_ref,
               x1_ref, h2_ref, ids_ref, wts_ref, wbf_ref):
    @pl.when(pl.program_id(0) == 0)
    def _():
        wbf_ref[...] = w_ref[...].astype(BF16)

    d = D_MODEL
    y = jnp.dot(o_ref[...], wbf_ref[...], preferred_element_type=F32) + b_ref[...]
    gt_m = mod_ref[:, 2 * d:3 * d]
    x1 = _layer_norm(DEEPNORM_ALPHA * x_ref[...] + (1.0 + gt_m) * y, lng_ref[...], lnb_ref[...])
    x1_ref[...] = x1
    sh_f = mod_ref[:, 3 * d:4 * d]
    sc_f = mod_ref[:, 4 * d:5 * d]
    h2 = x1 * (1.0 + sc_f) + sh_f
    _store_rows(h2_ref, h2)
    logits = jnp.dot(h2.astype(BF16), wr_ref[...].astype(BF16), preferred_element_type=F32) + br_ref[...]
    ea, eb, wa, wb = _route(logits.T)
    ids_ref[0:1, :] = ea
    ids_ref[1:2, :] = eb
    wts_ref[0:1, :] = wa
    wts_ref[1:2, :] = wb


def _proj_ln_router(o, x2, mod_rows, layer, w, b, ln_g, ln_b, w_router, b_router, seq):
    t, k = o.shape
    tm = 256
    steps_per_batch = seq // tm
    d = D_MODEL
    return pl.pallas_call(
        _proj_body,
        grid=(t // tm,),
        in_specs=[
            pl.BlockSpec((tm, k), lambda i: (i, 0)),
            pl.BlockSpec((tm, d), lambda i: (i, 0)),
            pl.BlockSpec((None, 1, N_MOD * d), lambda i: (layer * MOD_ROWS + i // steps_per_batch, 0, 0)),
            pl.BlockSpec((k, d), lambda i: (0, 0)),
            pl.BlockSpec((1, d), lambda i: (0, 0)),
            pl.BlockSpec((1, d), lambda i: (0, 0)),
            pl.BlockSpec((1, d), lambda i: (0, 0)),
            pl.BlockSpec((d, ROUTER_LANES), lambda i: (0, 0)),
            pl.BlockSpec((1, ROUTER_LANES), lambda i: (0, 0)),
        ],
        out_specs=[
            pl.BlockSpec((tm, d), lambda i: (i, 0)),
            pl.BlockSpec((tm * ROW_CHUNKS, LANES), lambda i: (i, 0)),
            pl.BlockSpec((2, tm), lambda i: (0, i)),
            pl.BlockSpec((2, tm), lambda i: (0, i)),
        ],
        out_shape=[
            jax.ShapeDtypeStruct((t, d), F32),
            jax.ShapeDtypeStruct((t * ROW_CHUNKS, LANES), F32),
            jax.ShapeDtypeStruct((2, t), I32),
            jax.ShapeDtypeStruct((2, t), F32),
        ],
        scratch_shapes=[pltpu.VMEM((k, d), BF16)],
        compiler_params=_cparams(1),
        name="proj_ln_router",
    )(o, x2, mod_rows, w, b.reshape(1, d), ln_g.reshape(1, d), ln_b.reshape(1, d), w_router, b_router)


def _sort_body(ids_ref, pos_ref, te_ref, nused_ref, tv_ref, rank_ref):
    n_rows = ids_ref.shape[0]
    c = SORT_CHUNK
    erow = lax.broadcasted_iota(I32, (N_EXPERTS, c), 0)
    tri = (lax.broadcasted_iota(I32, (c, c), 0) <= lax.broadcasted_iota(I32, (c, c), 1)).astype(BF16)

    def rank_step(r, carry):
        onehot = erow == ids_ref[pl.ds(r, 1), :]
        pref = jnp.dot(onehot.astype(BF16), tri, preferred_element_type=F32)
        rank = jnp.sum(jnp.where(onehot, pref + carry, 0.0), axis=0, keepdims=True) - 1.0
        rank_ref[pl.ds(r, 1), :] = rank
        return carry + pref[:, c - 1:c]

    counts = lax.fori_loop(0, n_rows, rank_step, jnp.zeros((N_EXPERTS, 1), F32))
    n_tile = jnp.floor((counts + (EXPERT_TILE - 1)) * (1.0 / EXPERT_TILE))
    low = (lax.broadcasted_iota(I32, (N_EXPERTS, N_EXPERTS), 1)
           <= lax.broadcasted_iota(I32, (N_EXPERTS, N_EXPERTS), 0)).astype(BF16)
    cum = jnp.dot(low, jnp.broadcast_to(n_tile, (N_EXPERTS, LANES)).astype(BF16),
                  preferred_element_type=F32)[:, 0:1]
    row_off = (cum - n_tile) * EXPERT_TILE

    def pos_step(r, _):
        onehot = erow == ids_ref[pl.ds(r, 1), :]
        off = jnp.sum(jnp.where(onehot, row_off, 0.0), axis=0, keepdims=True)
        pos_ref[pl.ds(r, 1), :] = (off + rank_ref[pl.ds(r, 1), :]).astype(I32)
        return 0

    lax.fori_loop(0, n_rows, pos_step, 0)
    total = jnp.max(cum, axis=0, keepdims=True)
    n_lanes = te_ref.shape[1]
    tile = jnp.minimum(lax.broadcasted_iota(I32, (N_EXPERTS, n_lanes), 1).astype(F32), total - 1.0)
    te_ref[...] = jnp.sum(jnp.where(cum <= tile, 1.0, 0.0), axis=0, keepdims=True).astype(I32)
    nused_ref[...] = jnp.broadcast_to(total, nused_ref.shape).astype(I32)
    tile_f = lax.broadcasted_iota(I32, (N_EXPERTS, n_lanes), 1).astype(F32)
    first = cum - n_tile
    rows_left = jnp.clip(counts - (tile_f - first) * EXPERT_TILE, 0.0, float(EXPERT_TILE))
    owns = (first <= tile_f) & (tile_f < cum)
    tv_ref[...] = jnp.sum(jnp.where(owns, rows_left, 0.0), axis=0, keepdims=True).astype(I32)


def _expert_sort(ids, n_tiles):
    n_assign = ids.shape[0] * ids.shape[1]
    n_rows = n_assign // SORT_CHUNK
    te_lanes = -(-n_tiles // LANES) * LANES
    pos, te, nused, tv = pl.pallas_call(
        _sort_body,
        grid=(1,),
        in_specs=[pl.BlockSpec((n_rows, SORT_CHUNK), lambda i: (0, 0))],
        out_specs=[
            pl.BlockSpec((n_rows, SORT_CHUNK), lambda i: (0, 0)),
            pl.BlockSpec((1, te_lanes), lambda i: (0, 0)),
            pl.BlockSpec((1, LANES), lambda i: (0, 0)),
            pl.BlockSpec((1, te_lanes), lambda i: (0, 0)),
        ],
        out_shape=[
            jax.ShapeDtypeStruct((n_rows, SORT_CHUNK), I32),
            jax.ShapeDtypeStruct((1, te_lanes), I32),
            jax.ShapeDtypeStruct((1, LANES), I32),
            jax.ShapeDtypeStruct((1, te_lanes), I32),
        ],
        scratch_shapes=[pltpu.VMEM((n_rows, SORT_CHUNK), F32)],
        compiler_params=_cparams(1),
        name="expert_sort",
    )(ids.reshape(n_rows, SORT_CHUNK))
    return pos, te[0, :n_tiles], nused[0, :1], tv[0, :n_tiles]


def _sc_mesh():
    return plsc.VectorSubcoreMesh(core_axis_name="c", subcore_axis_name="s", num_cores=SC_CORES,
                                  num_subcores=SC_WORKERS // SC_CORES)


def _sc_scatter_rows(src, pos2, n_rows):
    t = pos2.shape[1]
    src3 = src.reshape(t, ROW_CHUNKS, LANES)
    per_worker = t // SC_WORKERS
    n_chunks = per_worker // SC_CHUNK
    idx = pos2.reshape(2, SC_WORKERS, n_chunks, SC_CHUNK).transpose(1, 0, 2, 3)
    idx = idx.reshape(SC_WORKERS, 2 * n_chunks, SC_CHUNK)

    @functools.partial(
        pl.kernel, mesh=_sc_mesh(),
        out_type=jax.ShapeDtypeStruct((n_rows, ROW_CHUNKS, LANES), src.dtype),
        scratch_types=[
            pltpu.VMEM((2 * n_chunks, SC_CHUNK), I32),
            pltpu.VMEM((SC_CHUNK, ROW_CHUNKS, LANES), src.dtype),
            pltpu.SemaphoreType.DMA,
        ],
        name="sc_scatter_rows",
    )
    def scatter(src_hbm, idx_hbm, out_hbm, idx_v, rows_v, sem):
        wid = lax.axis_index("s") * SC_CORES + lax.axis_index("c")
        pltpu.sync_copy(idx_hbm.at[wid], idx_v)

        @pl.loop(0, n_chunks)
        def _(j):
            pltpu.sync_copy(src_hbm.at[pl.ds(wid * per_worker + j * SC_CHUNK, SC_CHUNK)], rows_v)
            for k in range(2):
                pltpu.async_copy(rows_v, out_hbm.at[idx_v.at[k * n_chunks + j]], sem).wait()

    return scatter(src3, idx).reshape(n_rows * ROW_CHUNKS, LANES)


def _sc_gather_rows(table, idx):
    n = idx.shape[0]
    table3 = table.reshape(-1, ROW_CHUNKS, LANES)
    per_worker = n // SC_WORKERS
    n_chunks = per_worker // SC_CHUNK

    @functools.partial(
        pl.kernel, mesh=_sc_mesh(),
        out_type=jax.ShapeDtypeStruct((n, ROW_CHUNKS, LANES), table.dtype),
        scratch_types=[
            pltpu.VMEM((n_chunks, SC_CHUNK), I32),
            pltpu.VMEM((SC_CHUNK, ROW_CHUNKS, LANES), table.dtype),
            pltpu.SemaphoreType.DMA,
        ],
        name="sc_gather_rows",
    )
    def gather(table_hbm, idx_hbm, out_hbm, idx_v, rows_v, sem):
        wid = lax.axis_index("s") * SC_CORES + lax.axis_index("c")
        pltpu.sync_copy(idx_hbm.at[wid], idx_v)

        @pl.loop(0, n_chunks)
        def _(j):
            pltpu.async_copy(table_hbm.at[idx_v.at[j]], rows_v, sem).wait()
            pltpu.sync_copy(rows_v, out_hbm.at[pl.ds(wid * per_worker + j * SC_CHUNK, SC_CHUNK)])

    out = gather(table3, idx.reshape(SC_WORKERS, n_chunks, SC_CHUNK))
    return out.reshape(n * ROW_CHUNKS, LANES)


def _expert_body(te_ref, nused_ref, tv_ref, xs_ref, wgu_ref, wd_ref, ys_ref, wgu_bf, wd_bf):
    i = pl.program_id(0)
    used = i < nused_ref[0]
    prev = te_ref[jnp.maximum(i - 1, 0)]

    @pl.when(used & ((i == 0) | (te_ref[i] != prev)))
    def _():
        wgu_bf[...] = wgu_ref[...].astype(BF16)
        wd_bf[...] = wd_ref[...].astype(BF16)

    @pl.when(used)
    def _():
        live = lax.broadcasted_iota(I32, (EXPERT_TILE, 1), 0) < tv_ref[i]
        xs = jnp.where(live, _load_rows(xs_ref, EXPERT_TILE), 0.0).astype(BF16)
        gu = jnp.dot(xs, wgu_bf[...], preferred_element_type=F32)
        gate = gu[:, :EXPERT_FF]
        up = gu[:, EXPERT_FF:]
        act = gate * jax.nn.sigmoid(gate) * up
        _store_rows(ys_ref, jnp.dot(act.astype(BF16), wd_bf[...], preferred_element_type=F32))

    @pl.when(jnp.logical_not(used))
    def _():
        ys_ref[...] = jnp.zeros(ys_ref.shape, ys_ref.dtype)


def _expert_mlp(tile_expert, n_used, tile_valid, xs, w_gate_up, w_down, layer):
    d = D_MODEL
    n_tiles = xs.shape[0] // (EXPERT_TILE * ROW_CHUNKS)
    f2 = 2 * EXPERT_FF
    tile_rows = EXPERT_TILE * ROW_CHUNKS
    e0 = layer * N_EXPERTS
    grid_spec = pltpu.PrefetchScalarGridSpec(
        num_scalar_prefetch=3,
        grid=(n_tiles,),
        in_specs=[
            pl.BlockSpec((tile_rows, LANES), lambda i, te, nu, tv: (jnp.minimum(i, nu[0] - 1), 0)),
            pl.BlockSpec((None, d, f2), lambda i, te, nu, tv: (e0 + te[i], 0, 0)),
            pl.BlockSpec((None, EXPERT_FF, d), lambda i, te, nu, tv: (e0 + te[i], 0, 0)),
        ],
        out_specs=pl.BlockSpec((tile_rows, LANES), lambda i, te, nu, tv: (i, 0)),
        scratch_shapes=[pltpu.VMEM((d, f2), BF16), pltpu.VMEM((EXPERT_FF, d), BF16)],
    )
    return pl.pallas_call(
        _expert_body,
        grid_spec=grid_spec,
        out_shape=jax.ShapeDtypeStruct(xs.shape, F32),
        compiler_params=_cparams(1),
        name="expert_mlp",
    )(tile_expert, n_used, tile_valid, xs, w_gate_up.reshape(DEPTH * N_EXPERTS, d, f2),
      w_down.reshape(DEPTH * N_EXPERTS, EXPERT_FF, d))


def _combine_body(wts_ref, x_ref, mod_ref, lng_ref, lnb_ref, ya_ref, yb_ref, o_ref):
    tm = x_ref.shape[0]
    d = D_MODEL
    cols = []
    for k in range(2):
        wt = jnp.broadcast_to(wts_ref[k:k + 1, :], (LANES, tm)).T
        cols.append(jnp.concatenate([wt] * (d // LANES), axis=1))
    y = cols[0] * _load_rows(ya_ref, tm) + cols[1] * _load_rows(yb_ref, tm)
    gt_f = mod_ref[:, 5 * d:6 * d]
    o_ref[...] = _layer_norm(DEEPNORM_ALPHA * x_ref[...] + (1.0 + gt_f) * y, lng_ref[...], lnb_ref[...])


def _combine_ln(pos2, wts, x1, mod_rows, layer, ln_g, ln_b, ys, seq):
    t, d = x1.shape
    tm = 256
    steps_per_batch = seq // tm
    n_steps = t // tm
    yg = _sc_gather_rows(ys, pos2.reshape(2 * t))
    return pl.pallas_call(
        _combine_body,
        grid=(n_steps,),
        in_specs=[
            pl.BlockSpec((2, tm), lambda i: (0, i)),
            pl.BlockSpec((tm, d), lambda i: (i, 0)),
            pl.BlockSpec((None, 1, N_MOD * d), lambda i: (layer * MOD_ROWS + i // steps_per_batch, 0, 0)),
            pl.BlockSpec((1, d), lambda i: (0, 0)),
            pl.BlockSpec((1, d), lambda i: (0, 0)),
            pl.BlockSpec((tm * ROW_CHUNKS, LANES), lambda i: (i, 0)),
            pl.BlockSpec((tm * ROW_CHUNKS, LANES), lambda i: (n_steps + i, 0)),
        ],
        out_specs=pl.BlockSpec((tm, d), lambda i: (i, 0)),
        out_shape=jax.ShapeDtypeStruct((t, d), F32),
        compiler_params=_cparams(1),
        name="moe_combine_ln",
    )(wts, x1, mod_rows, ln_g.reshape(1, d), ln_b.reshape(1, d), yg, yg)


def _gmlp_body(x_ref, mod_ref, w_ref, b_ref, g_ref, beta_ref, ws_ref, bs_ref, o_ref, ws_bf):
    @pl.when(pl.program_id(0) == 0)
    def _():
        tri = lax.broadcasted_iota(I32, (CHUNK, CHUNK), 0) >= lax.broadcasted_iota(I32, (CHUNK, CHUNK), 1)
        for g in range(N_SGU_GROUPS):
            ws_bf[g] = jnp.where(tri, ws_ref[g], 0.0).astype(BF16)

    tm = x_ref.shape[0]
    d = D_MODEL
    sh = mod_ref[:, 0:d]
    sc = mod_ref[:, d:2 * d]
    h = x_ref[...] * (1.0 + sc) + sh
    z = jnp.dot(h.astype(BF16), w_ref[...], preferred_element_type=F32) + b_ref[...]
    z = 0.5 * z * (1.0 + lax.erf(z * (2.0 ** -0.5)))
    u = z[:, :GMLP_WIDTH]
    v = _layer_norm(z[:, GMLP_WIDTH:], g_ref[...], beta_ref[...]).astype(BF16)
    for ci in range(tm // CHUNK):
        rows = slice(ci * CHUNK, (ci + 1) * CHUNK)
        for g in range(N_SGU_GROUPS):
            lanes = slice(g * SGU_GROUP_DIM, (g + 1) * SGU_GROUP_DIM)
            mixed = jnp.dot(ws_bf[g], v[rows, lanes], preferred_element_type=F32) + bs_ref[:, g:g + 1]
            o_ref[rows, lanes] = (u[rows, lanes] * mixed).astype(BF16)


def _gmlp_gate(x2, mod_rows, layer, w_in_bf, b_in, sgu_g, sgu_b, w_s, b_s, seq):
    t, d = x2.shape
    tm = 256
    steps_per_batch = seq // tm
    gw = GMLP_WIDTH
    return pl.pallas_call(
        _gmlp_body,
        grid=(t // tm,),
        in_specs=[
            pl.BlockSpec((tm, d), lambda i: (i, 0)),
            pl.BlockSpec((None, 1, N_MOD * d), lambda i: (layer * MOD_ROWS + i // steps_per_batch, 0, 0)),
            pl.BlockSpec((d, 2 * gw), lambda i: (0, 0)),
            pl.BlockSpec((1, 2 * gw), lambda i: (0, 0)),
            pl.BlockSpec((1, gw), lambda i: (0, 0)),
            pl.BlockSpec((1, gw), lambda i: (0, 0)),
            pl.BlockSpec((N_SGU_GROUPS, CHUNK, CHUNK), lambda i: (0, 0, 0)),
            pl.BlockSpec((CHUNK, N_SGU_GROUPS), lambda i: (0, 0)),
        ],
        out_specs=pl.BlockSpec((tm, gw), lambda i: (i, 0)),
        out_shape=jax.ShapeDtypeStruct((t, gw), BF16),
        scratch_shapes=[pltpu.VMEM((N_SGU_GROUPS, CHUNK, CHUNK), BF16)],
        compiler_params=_cparams(1),
        name="gmlp_gate",
    )(x2, mod_rows, w_in_bf, b_in.reshape(1, 2 * gw), sgu_g.reshape(1, gw), sgu_b.reshape(1, gw), w_s, b_s.T)


def _router_params(w_group, b_group, w_expert, b_expert):
    d = w_group.shape[0]
    w = jnp.zeros((d, ROUTER_LANES), F32)
    w = w.at[:, :N_EXPERT_GROUPS].set(w_group).at[:, EXPERT_COL0:EXPERT_COL0 + N_EXPERTS].set(w_expert)
    b = jnp.zeros((1, ROUTER_LANES), F32)
    b = b.at[0, :N_EXPERT_GROUPS].set(b_group).at[0, EXPERT_COL0:EXPERT_COL0 + N_EXPERTS].set(b_expert)
    return w, b


def _moe_block(x1, h2, ids, wts, mod_rows, layer, ln_g, ln_b, w_gate_up, w_down, seq):
    t = x1.shape[0]
    n_rows = 2 * t + N_EXPERTS * EXPERT_TILE
    n_tiles = n_rows // EXPERT_TILE
    pos, tile_expert, n_used, tile_valid = _expert_sort(ids, n_tiles)
    pos2 = pos.reshape(2, t)
    xs = _sc_scatter_rows(h2, pos2, n_rows)
    ys = _expert_mlp(tile_expert, n_used, tile_valid, xs, w_gate_up, w_down, layer)
    return _combine_ln(pos2, wts, x1, mod_rows, layer, ln_g, ln_b, ys, seq)


def kernel(x, c, positions, ada_w, ada_b, post_ln_g, post_ln_b, attn_w_qkv, attn_b_qkv, attn_sinks, attn_w_o, attn_b_o, gmlp_w_in, gmlp_b_in, gmlp_sgu_ln_g, gmlp_sgu_ln_b, gmlp_w_s, gmlp_b_s, gmlp_w_out, gmlp_b_out, moe_w_group_router, moe_b_group_router, moe_w_expert_router, moe_b_expert_router, moe_w_gate_up, moe_w_down):
    batch, seq, d = x.shape
    t = batch * seq
    x2 = x.reshape(t, d)
    c_pad = jnp.zeros((MOD_ROWS, d), F32).at[:batch].set(c)
    mod_rows = _adaln_mod(c_pad, ada_w, ada_b).reshape(DEPTH * MOD_ROWS, 1, N_MOD * d)

    for layer in range(DEPTH):
        j = layer // 2
        if layer % 2 == 0:
            qkv = _qkv_rope(x2, mod_rows, layer, positions, attn_w_qkv[j], attn_b_qkv[j], seq)
            mix = _attention(qkv, attn_sinks[j], batch, seq)
            w_out, b_out = attn_w_o[j], attn_b_o[j]
        else:
            mix = _gmlp_gate(x2, mod_rows, layer, gmlp_w_in[j].astype(BF16), gmlp_b_in[j], gmlp_sgu_ln_g[j],
                             gmlp_sgu_ln_b[j], gmlp_w_s[j], gmlp_b_s[j], seq)
            w_out, b_out = gmlp_w_out[j], gmlp_b_out[j]
        w_router, b_router = _router_params(moe_w_group_router[layer], moe_b_group_router[layer],
                                            moe_w_expert_router[layer], moe_b_expert_router[layer])
        x1, h2, ids, wts = _proj_ln_router(mix, x2, mod_rows, layer, w_out, b_out, post_ln_g[layer, 0],
                                           post_ln_b[layer, 0], w_router, b_router, seq)
        x2 = _moe_block(x1, h2, ids, wts, mod_rows, layer, post_ln_g[layer, 1], post_ln_b[layer, 1],
                        moe_w_gate_up, moe_w_down, seq)
    return x2.reshape(batch, seq, d)
```

```python
import functools

import jax
import jax.numpy as jnp
from jax import lax
from jax.experimental import pallas as pl
from jax.experimental.pallas import tpu as pltpu
from jax.experimental.pallas import tpu_sc as plsc

F32 = jnp.float32
BF16 = jnp.bfloat16
I32 = jnp.int32

D_MODEL = 1024
DEPTH = 2
HEAD_DIM = 64
N_Q_HEADS = 16
N_KV_HEADS = 4
GQA_GROUP = N_Q_HEADS // N_KV_HEADS
WINDOW = 128
ROPE_THETA = 10000.0
Q_WIDTH = N_Q_HEADS * HEAD_DIM
KV_WIDTH = N_KV_HEADS * HEAD_DIM
QKV_WIDTH = Q_WIDTH + 2 * KV_WIDTH
CHUNK = 128
GMLP_WIDTH = 2 * D_MODEL
N_SGU_GROUPS = 8
SGU_GROUP_DIM = GMLP_WIDTH // N_SGU_GROUPS
N_EXPERT_GROUPS = 4
EXPERTS_PER_GROUP = 8
N_EXPERTS = N_EXPERT_GROUPS * EXPERTS_PER_GROUP
EXPERT_FF = D_MODEL // 4
N_MOD = 6
DEEPNORM_ALPHA = (2.0 * DEPTH) ** 0.25
LN_EPS = 1e-5

LANES = 128
MOD_ROWS = 8
ROUTER_LANES = 128
EXPERT_COL0 = 8
SORT_CHUNK = 256
EXPERT_TILE = 256
VMEM_LIMIT = 56 * 1024 * 1024
SC_CORES = 2
SC_WORKERS = 32
SC_CHUNK = 64


def _cparams(n_axes, vmem=VMEM_LIMIT):
    return pltpu.CompilerParams(dimension_semantics=("arbitrary",) * n_axes, vmem_limit_bytes=vmem)


U32 = jnp.uint32
ROW_CHUNKS = D_MODEL // 2 // LANES


def _store_rows(ref, val):
    n = val.shape[0]
    half = D_MODEL // 2
    lo = lax.bitcast_convert_type(val[:, :half].astype(BF16).astype(F32), U32)
    hi = lax.bitcast_convert_type(val[:, half:].astype(BF16).astype(F32), U32)
    words = (lo >> 16) | hi
    for c in range(ROW_CHUNKS):
        ref[pl.ds(c, n, stride=ROW_CHUNKS), :] = words[:, c * LANES:(c + 1) * LANES]


def _load_rows(ref, n):
    words = jnp.concatenate([ref[pl.ds(c, n, stride=ROW_CHUNKS), :] for c in range(ROW_CHUNKS)], axis=1)
    lo = lax.bitcast_convert_type(words << 16, F32)
    hi = lax.bitcast_convert_type(words & jnp.uint32(0xFFFF0000), F32)
    return jnp.concatenate([lo, hi], axis=1)


def _layer_norm(r, g, b):
    mu = jnp.mean(r, axis=-1, keepdims=True)
    d = r - mu
    var = jnp.mean(d * d, axis=-1, keepdims=True)
    return d * lax.rsqrt(var + LN_EPS) * g + b


def _mod_body(c_ref, w_ref, b_ref, o_ref):
    c = c_ref[...]
    ca = c * jax.nn.sigmoid(c)
    o_ref[...] = jnp.dot(ca.astype(BF16), w_ref[...].astype(BF16), preferred_element_type=F32) + b_ref[...]


def _adaln_mod(c_pad, ada_w, ada_b):
    tn = 1536
    n_out = N_MOD * D_MODEL
    return pl.pallas_call(
        _mod_body,
        grid=(DEPTH, n_out // tn),
        in_specs=[
            pl.BlockSpec((MOD_ROWS, D_MODEL), lambda l, j: (0, 0)),
            pl.BlockSpec((None, D_MODEL, tn), lambda l, j: (l, 0, j)),
            pl.BlockSpec((None, 1, tn), lambda l, j: (l, 0, j)),
        ],
        out_specs=pl.BlockSpec((None, MOD_ROWS, tn), lambda l, j: (l, 0, j)),
        out_shape=jax.ShapeDtypeStruct((DEPTH, MOD_ROWS, n_out), F32),
        compiler_params=_cparams(2),
        name="adaln_mod",
    )(c_pad, ada_w, ada_b.reshape(DEPTH, 1, n_out))


def _qkv_body(x_ref, mod_ref, pos_ref, w_ref, b_ref, invf_ref, o_ref, wbf_ref):
    @pl.when(pl.program_id(0) == 0)
    def _():
        wbf_ref[...] = w_ref[...].astype(BF16)

    tm = x_ref.shape[0]
    sh = mod_ref[:, 0:D_MODEL]
    sc = mod_ref[:, D_MODEL:2 * D_MODEL]
    h = x_ref[...] * (1.0 + sc) + sh
    qkv = jnp.dot(h.astype(BF16), wbf_ref[...], preferred_element_type=F32) + b_ref[...]

    ang = invf_ref[...] * pos_ref[...].astype(F32)
    c = jnp.cos(ang)
    s = jnp.sin(ang)
    ct = jnp.concatenate([c, c, c, c], axis=0).T
    st = jnp.concatenate([-s, s, -s, s], axis=0).T
    lane = lax.broadcasted_iota(I32, (tm, LANES), 1)
    first_half = (lane & (HEAD_DIM // 2)) == 0
    n_rope = (Q_WIDTH + KV_WIDTH) // LANES
    for j in range(n_rope):
        blk = qkv[:, j * LANES:(j + 1) * LANES]
        rot = jnp.where(first_half, pltpu.roll(blk, LANES - HEAD_DIM // 2, 1), pltpu.roll(blk, HEAD_DIM // 2, 1))
        r = blk * ct + rot * st
        if j < Q_WIDTH // LANES:
            r = r * (HEAD_DIM ** -0.5)
        o_ref[:, j * LANES:(j + 1) * LANES] = r.astype(BF16)
    o_ref[:, Q_WIDTH + KV_WIDTH:] = qkv[:, Q_WIDTH + KV_WIDTH:].astype(BF16)


def _qkv_rope(x2, mod_rows, layer, positions, w_qkv, b_qkv, seq):
    t = x2.shape[0]
    tm = 512
    steps_per_batch = seq // tm
    inv_freq = ROPE_THETA ** (-jnp.arange(0, HEAD_DIM, 2, dtype=F32) / HEAD_DIM)
    return pl.pallas_call(
        _qkv_body,
        grid=(t // tm,),
        in_specs=[
            pl.BlockSpec((tm, D_MODEL), lambda i: (i, 0)),
            pl.BlockSpec((None, 1, N_MOD * D_MODEL), lambda i: (layer * MOD_ROWS + i // steps_per_batch, 0, 0)),
            pl.BlockSpec((None, 1, tm), lambda i: (i, 0, 0)),
            pl.BlockSpec((D_MODEL, QKV_WIDTH), lambda i: (0, 0)),
            pl.BlockSpec((1, QKV_WIDTH), lambda i: (0, 0)),
            pl.BlockSpec((HEAD_DIM // 2, 1), lambda i: (0, 0)),
        ],
        out_specs=pl.BlockSpec((tm, QKV_WIDTH), lambda i: (i, 0)),
        out_shape=jax.ShapeDtypeStruct((t, QKV_WIDTH), BF16),
        scratch_shapes=[pltpu.VMEM((D_MODEL, QKV_WIDTH), BF16)],
        compiler_params=_cparams(1),
        name="qkv_rope",
    )(x2, mod_rows, positions.reshape(t // tm, 1, tm), w_qkv, b_qkv.reshape(1, QKV_WIDTH),
      inv_freq.reshape(HEAD_DIM // 2, 1))


def _attn_body(sink_ref, q_ref, kvc_ref, kvp_ref, o_ref):
    n = pl.program_id(1)
    q = q_ref[...]
    kvc = kvc_ref[...]
    kvp = kvp_ref[...]
    kb = jnp.concatenate([kvp[:, :KV_WIDTH], kvc[:, :KV_WIDTH]], axis=0)
    vb = jnp.concatenate([kvp[:, KV_WIDTH:], kvc[:, KV_WIDTH:]], axis=0)
    qi = lax.broadcasted_iota(I32, (WINDOW, 2 * WINDOW), 0) + WINDOW
    kj = lax.broadcasted_iota(I32, (WINDOW, 2 * WINDOW), 1)
    mask = (kj <= qi) & (kj > qi - WINDOW) & ((kj >= WINDOW) | (n > 0))
    outs = []
    for h in range(N_Q_HEADS):
        g = h // GQA_GROUP
        qh = q[:, h * HEAD_DIM:(h + 1) * HEAD_DIM]
        kg = kb[:, g * HEAD_DIM:(g + 1) * HEAD_DIM]
        vg = vb[:, g * HEAD_DIM:(g + 1) * HEAD_DIM]
        s = lax.dot_general(qh, kg, (((1,), (1,)), ((), ())), preferred_element_type=F32)
        s = jnp.where(mask, s, -jnp.inf)
        sink = sink_ref[h]
        m = jnp.maximum(jnp.max(s, axis=-1, keepdims=True), sink)
        p = jnp.exp(s - m)
        denom = jnp.sum(p, axis=-1, keepdims=True) + jnp.exp(sink - m)
        o = jnp.dot(p.astype(BF16), vg, preferred_element_type=F32)
        outs.append(o / denom)
    o_ref[...] = jnp.concatenate(outs, axis=1).astype(BF16)


def _attention(qkv, sinks, batch, seq):
    t = qkv.shape[0]
    nb = seq // WINDOW
    kv_col = Q_WIDTH // (2 * KV_WIDTH)
    return pl.pallas_call(
        _attn_body,
        grid=(batch, nb),
        in_specs=[
            pl.BlockSpec(memory_space=pltpu.SMEM),
            pl.BlockSpec((WINDOW, Q_WIDTH), lambda b, n: (b * nb + n, 0)),
            pl.BlockSpec((WINDOW, 2 * KV_WIDTH), lambda b, n: (b * nb + n, kv_col)),
            pl.BlockSpec((WINDOW, 2 * KV_WIDTH), lambda b, n: (b * nb + jnp.maximum(n - 1, 0), kv_col)),
        ],
        out_specs=pl.BlockSpec((WINDOW, Q_WIDTH), lambda b, n: (b * nb + n, 0)),
        out_shape=jax.ShapeDtypeStruct((t, Q_WIDTH), BF16),
        compiler_params=_cparams(2),
        name="swa_attention",
    )(sinks, qkv, qkv, qkv)


def _route(lt):
    tm = lt.shape[1]
    row = lax.broadcasted_iota(I32, (EXPERTS_PER_GROUP, tm), 0)
    neg = -jnp.inf
    gl = jnp.where(row < N_EXPERT_GROUPS, lt[0:EXPERTS_PER_GROUP], neg)
    gm = jnp.max(gl, axis=0, keepdims=True)
    g_p = 1.0 / jnp.sum(jnp.exp(gl - gm), axis=0, keepdims=True)
    g_idx = jnp.min(jnp.where(gl == gm, row, EXPERTS_PER_GROUP), axis=0, keepdims=True)
    sel = lt[EXPERT_COL0 + (N_EXPERT_GROUPS - 1) * EXPERTS_PER_GROUP:EXPERT_COL0 + N_EXPERTS]
    for g in range(N_EXPERT_GROUPS - 2, -1, -1):
        lo = EXPERT_COL0 + g * EXPERTS_PER_GROUP
        sel = jnp.where(g_idx == g, lt[lo:lo + EXPERTS_PER_GROUP], sel)
    v1 = jnp.max(sel, axis=0, keepdims=True)
    i1 = jnp.min(jnp.where(sel == v1, row, EXPERTS_PER_GROUP), axis=0, keepdims=True)
    sel2 = jnp.where(row == i1, neg, sel)
    v2 = jnp.max(sel2, axis=0, keepdims=True)
    i2 = jnp.min(jnp.where(sel2 == v2, row, EXPERTS_PER_GROUP), axis=0, keepdims=True)
    e2 = jnp.exp(v2 - v1)
    w1 = g_p / (1.0 + e2)
    w2 = g_p * e2 / (1.0 + e2)
    base = g_idx * EXPERTS_PER_GROUP
    return base + i1, base + i2, w1, w2


def _proj_body(o_ref, x_ref, mod_ref, w_ref, b_ref, lng_ref, lnb_ref, wr_ref, br_ref,
               x1_ref, h2_ref, ids_ref, wts_ref, wbf_ref):
    @pl.when(pl.program_id(0) == 0)
    def _():
        wbf_ref[...] = w_ref[...].astype(BF16)

    d = D_MODEL
    y = jnp.dot(o_ref[...], wbf_ref[...], preferred_element_type=F32) + b_ref[...]
    gt_m = mod_ref[:, 2 * d:3 * d]
    x1 = _layer_norm(DEEPNORM_ALPHA * x_ref[...] + (1.0 + gt_m) * y, lng_ref[...], lnb_ref[...])
    x1_ref[...] = x1
    sh_f = mod_ref[:, 3 * d:4 * d]
    sc_f = mod_ref[:, 4 * d:5 * d]
    h2 = x1 * (1.0 + sc_f) + sh_f
    _store_rows(h2_ref, h2)
    logits = jnp.dot(h2.astype(BF16), wr_ref[...].astype(BF16), preferred_element_type=F32) + br_ref[...]
    ea, eb, wa, wb = _route(logits.T)
    ids_ref[0:1, :] = ea
    ids_ref[1:2, :] = eb
    wts_ref[0:1, :] = wa
    wts_ref[1:2, :] = wb


def _proj_ln_router(o, x2, mod_rows, layer, w, b, ln_g, ln_b, w_router, b_router, seq):
    t, k = o.shape
    tm = 512
    steps_per_batch = seq // tm
    d = D_MODEL
    return pl.pallas_call(
        _proj_body,
        grid=(t // tm,),
        in_specs=[
            pl.BlockSpec((tm, k), lambda i: (i, 0)),
            pl.BlockSpec((tm, d), lambda i: (i, 0)),
            pl.BlockSpec((None, 1, N_MOD * d), lambda i: (layer * MOD_ROWS + i // steps_per_batch, 0, 0)),
            pl.BlockSpec((k, d), lambda i: (0, 0)),
            pl.BlockSpec((1, d), lambda i: (0, 0)),
            pl.BlockSpec((1, d), lambda i: (0, 0)),
            pl.BlockSpec((1, d), lambda i: (0, 0)),
            pl.BlockSpec((d, ROUTER_LANES), lambda i: (0, 0)),
            pl.BlockSpec((1, ROUTER_LANES), lambda i: (0, 0)),
        ],
        out_specs=[
            pl.BlockSpec((tm, d), lambda i: (i, 0)),
            pl.BlockSpec((tm * ROW_CHUNKS, LANES), lambda i: (i, 0)),
            pl.BlockSpec((2, tm), lambda i: (0, i)),
            pl.BlockSpec((2, tm), lambda i: (0, i)),
        ],
        out_shape=[
            jax.ShapeDtypeStruct((t, d), F32),
            jax.ShapeDtypeStruct((t * ROW_CHUNKS, LANES), U32),
            jax.ShapeDtypeStruct((2, t), I32),
            jax.ShapeDtypeStruct((2, t), F32),
        ],
        scratch_shapes=[pltpu.VMEM((k, d), BF16)],
        compiler_params=_cparams(1),
        name="proj_ln_router",
    )(o, x2, mod_rows, w, b.reshape(1, d), ln_g.reshape(1, d), ln_b.reshape(1, d), w_router, b_router)


def _sort_body(ids_ref, pos_ref, te_ref, nused_ref, tv_ref, rank_ref):
    n_rows = ids_ref.shape[0]
    c = SORT_CHUNK
    erow = lax.broadcasted_iota(I32, (N_EXPERTS, c), 0)
    tri = (lax.broadcasted_iota(I32, (c, c), 0) <= lax.broadcasted_iota(I32, (c, c), 1)).astype(BF16)

    def rank_step(r, carry):
        onehot = erow == ids_ref[pl.ds(r, 1), :]
        pref = jnp.dot(onehot.astype(BF16), tri, preferred_element_type=F32)
        rank = jnp.sum(jnp.where(onehot, pref + carry, 0.0), axis=0, keepdims=True) - 1.0
        rank_ref[pl.ds(r, 1), :] = rank
        return carry + pref[:, c - 1:c]

    counts = lax.fori_loop(0, n_rows, rank_step, jnp.zeros((N_EXPERTS, 1), F32))
    n_tile = jnp.floor((counts + (EXPERT_TILE - 1)) * (1.0 / EXPERT_TILE))
    low = (lax.broadcasted_iota(I32, (N_EXPERTS, N_EXPERTS), 1)
           <= lax.broadcasted_iota(I32, (N_EXPERTS, N_EXPERTS), 0)).astype(BF16)
    cum = jnp.dot(low, jnp.broadcast_to(n_tile, (N_EXPERTS, LANES)).astype(BF16),
                  preferred_element_type=F32)[:, 0:1]
    row_off = (cum - n_tile) * EXPERT_TILE

    def pos_step(r, _):
        onehot = erow == ids_ref[pl.ds(r, 1), :]
        off = jnp.sum(jnp.where(onehot, row_off, 0.0), axis=0, keepdims=True)
        pos_ref[pl.ds(r, 1), :] = (off + rank_ref[pl.ds(r, 1), :]).astype(I32)
        return 0

    lax.fori_loop(0, n_rows, pos_step, 0)
    total = jnp.max(cum, axis=0, keepdims=True)
    n_lanes = te_ref.shape[1]
    tile = jnp.minimum(lax.broadcasted_iota(I32, (N_EXPERTS, n_lanes), 1).astype(F32), total - 1.0)
    te_ref[...] = jnp.sum(jnp.where(cum <= tile, 1.0, 0.0), axis=0, keepdims=True).astype(I32)
    nused_ref[...] = jnp.broadcast_to(total, nused_ref.shape).astype(I32)
    tile_f = lax.broadcasted_iota(I32, (N_EXPERTS, n_lanes), 1).astype(F32)
    first = cum - n_tile
    rows_left = jnp.clip(counts - (tile_f - first) * EXPERT_TILE, 0.0, float(EXPERT_TILE))
    owns = (first <= tile_f) & (tile_f < cum)
    tv_ref[...] = jnp.sum(jnp.where(owns, rows_left, 0.0), axis=0, keepdims=True).astype(I32)


def _expert_sort(ids, n_tiles):
    n_assign = ids.shape[0] * ids.shape[1]
    n_rows = n_assign // SORT_CHUNK
    te_lanes = -(-n_tiles // LANES) * LANES
    pos, te, nused, tv = pl.pallas_call(
        _sort_body,
        grid=(1,),
        in_specs=[pl.BlockSpec((n_rows, SORT_CHUNK), lambda i: (0, 0))],
        out_specs=[
            pl.BlockSpec((n_rows, SORT_CHUNK), lambda i: (0, 0)),
            pl.BlockSpec((1, te_lanes), lambda i: (0, 0)),
            pl.BlockSpec((1, LANES), lambda i: (0, 0)),
            pl.BlockSpec((1, te_lanes), lambda i: (0, 0)),
        ],
        out_shape=[
            jax.ShapeDtypeStruct((n_rows, SORT_CHUNK), I32),
            jax.ShapeDtypeStruct((1, te_lanes), I32),
            jax.ShapeDtypeStruct((1, LANES), I32),
            jax.ShapeDtypeStruct((1, te_lanes), I32),
        ],
        scratch_shapes=[pltpu.VMEM((n_rows, SORT_CHUNK), F32)],
        compiler_params=_cparams(1),
        name="expert_sort",
    )(ids.reshape(n_rows, SORT_CHUNK))
    return pos, te[0, :n_tiles], nused[0, :1], tv[0, :n_tiles]


def _sc_mesh():
    return plsc.VectorSubcoreMesh(core_axis_name="c", subcore_axis_name="s", num_cores=SC_CORES,
                                  num_subcores=SC_WORKERS // SC_CORES)


def _sc_scatter_rows(src, pos2, n_rows):
    t = pos2.shape[1]
    src3 = src.reshape(t, ROW_CHUNKS, LANES)
    per_worker = t // SC_WORKERS
    n_chunks = per_worker // SC_CHUNK
    idx = pos2.reshape(2, SC_WORKERS, n_chunks, SC_CHUNK).transpose(1, 0, 2, 3)
    idx = idx.reshape(SC_WORKERS, 2 * n_chunks, SC_CHUNK)

    @functools.partial(
        pl.kernel, mesh=_sc_mesh(),
        out_type=jax.ShapeDtypeStruct((n_rows, ROW_CHUNKS, LANES), src.dtype),
        scratch_types=[
            pltpu.VMEM((2 * n_chunks, SC_CHUNK), I32),
            pltpu.VMEM((SC_CHUNK, ROW_CHUNKS, LANES), src.dtype),
            pltpu.SemaphoreType.DMA,
        ],
        name="sc_scatter_rows",
    )
    def scatter(src_hbm, idx_hbm, out_hbm, idx_v, rows_v, sem):
        wid = lax.axis_index("s") * SC_CORES + lax.axis_index("c")
        pltpu.sync_copy(idx_hbm.at[wid], idx_v)

        @pl.loop(0, n_chunks)
        def _(j):
            pltpu.sync_copy(src_hbm.at[pl.ds(wid * per_worker + j * SC_CHUNK, SC_CHUNK)], rows_v)
            for k in range(2):
                pltpu.async_copy(rows_v, out_hbm.at[idx_v.at[k * n_chunks + j]], sem).wait()

    return scatter(src3, idx).reshape(n_rows * ROW_CHUNKS, LANES)


def _sc_gather_rows(table, idx):
    n = idx.shape[0]
    table3 = table.reshape(-1, ROW_CHUNKS, LANES)
    per_worker = n // SC_WORKERS
    n_chunks = per_worker // SC_CHUNK

    @functools.partial(
        pl.kernel, mesh=_sc_mesh(),
        out_type=jax.ShapeDtypeStruct((n, ROW_CHUNKS, LANES), table.dtype),
        scratch_types=[
            pltpu.VMEM((n_chunks, SC_CHUNK), I32),
            pltpu.VMEM((SC_CHUNK, ROW_CHUNKS, LANES), table.dtype),
            pltpu.SemaphoreType.DMA,
        ],
        name="sc_gather_rows",
    )
    def gather(table_hbm, idx_hbm, out_hbm, idx_v, rows_v, sem):
        wid = lax.axis_index("s") * SC_CORES + lax.axis_index("c")
        pltpu.sync_copy(idx_hbm.at[wid], idx_v)

        @pl.loop(0, n_chunks)
        def _(j):
            pltpu.async_copy(table_hbm.at[idx_v.at[j]], rows_v, sem).wait()
            pltpu.sync_copy(rows_v, out_hbm.at[pl.ds(wid * per_worker + j * SC_CHUNK, SC_CHUNK)])

    out = gather(table3, idx.reshape(SC_WORKERS, n_chunks, SC_CHUNK))
    return out.reshape(n * ROW_CHUNKS, LANES)


def _expert_body(te_ref, nused_ref, tv_ref, xs_ref, wgu_ref, wd_ref, ys_ref, wgu_bf, wd_bf):
    i = pl.program_id(0)
    used = i < nused_ref[0]
    prev = te_ref[jnp.maximum(i - 1, 0)]

    @pl.when(used & ((i == 0) | (te_ref[i] != prev)))
    def _():
        wgu_bf[...] = wgu_ref[...].astype(BF16)
        wd_bf[...] = wd_ref[...].astype(BF16)

    @pl.when(used)
    def _():
        live = lax.broadcasted_iota(I32, (EXPERT_TILE, 1), 0) < tv_ref[i]
        xs = jnp.where(live, _load_rows(xs_ref, EXPERT_TILE), 0.0).astype(BF16)
        gu = jnp.dot(xs, wgu_bf[...], preferred_element_type=F32)
        gate = gu[:, :EXPERT_FF]
        up = gu[:, EXPERT_FF:]
        act = gate * jax.nn.sigmoid(gate) * up
        _store_rows(ys_ref, jnp.dot(act.astype(BF16), wd_bf[...], preferred_element_type=F32))

    @pl.when(jnp.logical_not(used))
    def _():
        ys_ref[...] = jnp.zeros(ys_ref.shape, ys_ref.dtype)


def _expert_mlp(tile_expert, n_used, tile_valid, xs, w_gate_up, w_down, layer):
    d = D_MODEL
    n_tiles = xs.shape[0] // (EXPERT_TILE * ROW_CHUNKS)
    f2 = 2 * EXPERT_FF
    tile_rows = EXPERT_TILE * ROW_CHUNKS
    e0 = layer * N_EXPERTS
    grid_spec = pltpu.PrefetchScalarGridSpec(
        num_scalar_prefetch=3,
        grid=(n_tiles,),
        in_specs=[
            pl.BlockSpec((tile_rows, LANES), lambda i, te, nu, tv: (jnp.minimum(i, nu[0] - 1), 0)),
            pl.BlockSpec((None, d, f2), lambda i, te, nu, tv: (e0 + te[i], 0, 0)),
            pl.BlockSpec((None, EXPERT_FF, d), lambda i, te, nu, tv: (e0 + te[i], 0, 0)),
        ],
        out_specs=pl.BlockSpec((tile_rows, LANES), lambda i, te, nu, tv: (i, 0)),
        scratch_shapes=[pltpu.VMEM((d, f2), BF16), pltpu.VMEM((EXPERT_FF, d), BF16)],
    )
    return pl.pallas_call(
        _expert_body,
        grid_spec=grid_spec,
        out_shape=jax.ShapeDtypeStruct(xs.shape, U32),
        compiler_params=_cparams(1),
        name="expert_mlp",
    )(tile_expert, n_used, tile_valid, xs, w_gate_up.reshape(DEPTH * N_EXPERTS, d, f2),
      w_down.reshape(DEPTH * N_EXPERTS, EXPERT_FF, d))


def _combine_body(wts_ref, x_ref, mod_ref, lng_ref, lnb_ref, ya_ref, yb_ref, o_ref):
    tm = x_ref.shape[0]
    d = D_MODEL
    cols = []
    for k in range(2):
        wt = jnp.broadcast_to(wts_ref[k:k + 1, :], (LANES, tm)).T
        cols.append(jnp.concatenate([wt] * (d // LANES), axis=1))
    y = cols[0] * _load_rows(ya_ref, tm) + cols[1] * _load_rows(yb_ref, tm)
    gt_f = mod_ref[:, 5 * d:6 * d]
    o_ref[...] = _layer_norm(DEEPNORM_ALPHA * x_ref[...] + (1.0 + gt_f) * y, lng_ref[...], lnb_ref[...])


def _combine_ln(pos2, wts, x1, mod_rows, layer, ln_g, ln_b, ys, seq):
    t, d = x1.shape
    tm = 512
    steps_per_batch = seq // tm
    n_steps = t // tm
    yg = _sc_gather_rows(ys, pos2.reshape(2 * t))
    return pl.pallas_call(
        _combine_body,
        grid=(n_steps,),
        in_specs=[
            pl.BlockSpec((2, tm), lambda i: (0, i)),
            pl.BlockSpec((tm, d), lambda i: (i, 0)),
            pl.BlockSpec((None, 1, N_MOD * d), lambda i: (layer * MOD_ROWS + i // steps_per_batch, 0, 0)),
            pl.BlockSpec((1, d), lambda i: (0, 0)),
            pl.BlockSpec((1, d), lambda i: (0, 0)),
            pl.BlockSpec((tm * ROW_CHUNKS, LANES), lambda i: (i, 0)),
            pl.BlockSpec((tm * ROW_CHUNKS, LANES), lambda i: (n_steps + i, 0)),
        ],
        out_specs=pl.BlockSpec((tm, d), lambda i: (i, 0)),
        out_shape=jax.ShapeDtypeStruct((t, d), F32),
        compiler_params=_cparams(1),
        name="moe_combine_ln",
    )(wts, x1, mod_rows, ln_g.reshape(1, d), ln_b.reshape(1, d), yg, yg)


def _gmlp_body(x_ref, mod_ref, w_ref, b_ref, g_ref, beta_ref, ws_ref, bs_ref, o_ref, ws_bf):
    @pl.when(pl.program_id(0) == 0)
    def _():
        tri = lax.broadcasted_iota(I32, (CHUNK, CHUNK), 0) >= lax.broadcasted_iota(I32, (CHUNK, CHUNK), 1)
        for g in range(N_SGU_GROUPS):
            ws_bf[g] = jnp.where(tri, ws_ref[g], 0.0).astype(BF16)

    tm = x_ref.shape[0]
    d = D_MODEL
    sh = mod_ref[:, 0:d]
    sc = mod_ref[:, d:2 * d]
    h = x_ref[...] * (1.0 + sc) + sh
    z = jnp.dot(h.astype(BF16), w_ref[...], preferred_element_type=F32) + b_ref[...]
    z = 0.5 * z * (1.0 + lax.erf(z * (2.0 ** -0.5)))
    u = z[:, :GMLP_WIDTH]
    v = _layer_norm(z[:, GMLP_WIDTH:], g_ref[...], beta_ref[...]).astype(BF16)
    for ci in range(tm // CHUNK):
        rows = slice(ci * CHUNK, (ci + 1) * CHUNK)
        for g in range(N_SGU_GROUPS):
            lanes = slice(g * SGU_GROUP_DIM, (g + 1) * SGU_GROUP_DIM)
            mixed = jnp.dot(ws_bf[g], v[rows, lanes], preferred_element_type=F32) + bs_ref[:, g:g + 1]
            o_ref[rows, lanes] = (u[rows, lanes] * mixed).astype(BF16)


def _gmlp_gate(x2, mod_rows, layer, w_in_bf, b_in, sgu_g, sgu_b, w_s, b_s, seq):
    t, d = x2.shape
    tm = 256
    steps_per_batch = seq // tm
    gw = GMLP_WIDTH
    return pl.pallas_call(
        _gmlp_body,
        grid=(t // tm,),
        in_specs=[
            pl.BlockSpec((tm, d), lambda i: (i, 0)),
            pl.BlockSpec((None, 1, N_MOD * d), lambda i: (layer * MOD_ROWS + i // steps_per_batch, 0, 0)),
            pl.BlockSpec((d, 2 * gw), lambda i: (0, 0)),
            pl.BlockSpec((1, 2 * gw), lambda i: (0, 0)),
            pl.BlockSpec((1, gw), lambda i: (0, 0)),
            pl.BlockSpec((1, gw), lambda i: (0, 0)),
            pl.BlockSpec((N_SGU_GROUPS, CHUNK, CHUNK), lambda i: (0, 0, 0)),
            pl.BlockSpec((CHUNK, N_SGU_GROUPS), lambda i: (0, 0)),
        ],
        out_specs=pl.BlockSpec((tm, gw), lambda i: (i, 0)),
        out_shape=jax.ShapeDtypeStruct((t, gw), BF16),
        scratch_shapes=[pltpu.VMEM((N_SGU_GROUPS, CHUNK, CHUNK), BF16)],
        compiler_params=_cparams(1),
        name="gmlp_gate",
    )(x2, mod_rows, w_in_bf, b_in.reshape(1, 2 * gw), sgu_g.reshape(1, gw), sgu_b.reshape(1, gw), w_s, b_s.T)


def _router_params(w_group, b_group, w_expert, b_expert):
    d = w_group.shape[0]
    w = jnp.zeros((d, ROUTER_LANES), F32)
    w = w.at[:, :N_EXPERT_GROUPS].set(w_group).at[:, EXPERT_COL0:EXPERT_COL0 + N_EXPERTS].set(w_expert)
    b = jnp.zeros((1, ROUTER_LANES), F32)
    b = b.at[0, :N_EXPERT_GROUPS].set(b_group).at[0, EXPERT_COL0:EXPERT_COL0 + N_EXPERTS].set(b_expert)
    return w, b


def _moe_block(x1, h2, ids, wts, mod_rows, layer, ln_g, ln_b, w_gate_up, w_down, seq):
    t = x1.shape[0]
    n_rows = 2 * t + N_EXPERTS * EXPERT_TILE
    n_tiles = n_rows // EXPERT_TILE
    pos, tile_expert, n_used, tile_valid = _expert_sort(ids, n_tiles)
    pos2 = pos.reshape(2, t)
    xs = _sc_scatter_rows(h2, pos2, n_rows)
    ys = _expert_mlp(tile_expert, n_used, tile_valid, xs, w_gate_up, w_down, layer)
    return _combine_ln(pos2, wts, x1, mod_rows, layer, ln_g, ln_b, ys, seq)


def kernel(x, c, positions, ada_w, ada_b, post_ln_g, post_ln_b, attn_w_qkv, attn_b_qkv, attn_sinks, attn_w_o, attn_b_o, gmlp_w_in, gmlp_b_in, gmlp_sgu_ln_g, gmlp_sgu_ln_b, gmlp_w_s, gmlp_b_s, gmlp_w_out, gmlp_b_out, moe_w_group_router, moe_b_group_router, moe_w_expert_router, moe_b_expert_router, moe_w_gate_up, moe_w_down):
    batch, seq, d = x.shape
    t = batch * seq
    x2 = x.reshape(t, d)
    c_pad = jnp.zeros((MOD_ROWS, d), F32).at[:batch].set(c)
    mod_rows = _adaln_mod(c_pad, ada_w, ada_b).reshape(DEPTH * MOD_ROWS, 1, N_MOD * d)

    for layer in range(DEPTH):
        j = layer // 2
        if layer % 2 == 0:
            qkv = _qkv_rope(x2, mod_rows, layer, positions, attn_w_qkv[j], attn_b_qkv[j], seq)
            mix = _attention(qkv, attn_sinks[j], batch, seq)
            w_out, b_out = attn_w_o[j], attn_b_o[j]
        else:
            mix = _gmlp_gate(x2, mod_rows, layer, gmlp_w_in[j].astype(BF16), gmlp_b_in[j], gmlp_sgu_ln_g[j],
                             gmlp_sgu_ln_b[j], gmlp_w_s[j], gmlp_b_s[j], seq)
            w_out, b_out = gmlp_w_out[j], gmlp_b_out[j]
        w_router, b_router = _router_params(moe_w_group_router[layer], moe_b_group_router[layer],
                                            moe_w_expert_router[layer], moe_b_expert_router[layer])
        x1, h2, ids, wts = _proj_ln_router(mix, x2, mod_rows, layer, w_out, b_out, post_ln_g[layer, 0],
                                           post_ln_b[layer, 0], w_router, b_router, seq)
        x2 = _moe_block(x1, h2, ids, wts, mod_rows, layer, post_ln_g[layer, 1], post_ln_b[layer, 1],
                        moe_w_gate_up, moe_w_down, seq)
    return x2.reshape(batch, seq, d)
```

```python
import functools

import jax
import jax.numpy as jnp
from jax import lax
from jax.experimental import pallas as pl
from jax.experimental.pallas import tpu as pltpu
from jax.experimental.pallas import tpu_sc as plsc

F32 = jnp.float32
BF16 = jnp.bfloat16
I32 = jnp.int32

D_MODEL = 1024
DEPTH = 2
HEAD_DIM = 64
N_Q_HEADS = 16
N_KV_HEADS = 4
GQA_GROUP = N_Q_HEADS // N_KV_HEADS
WINDOW = 128
ROPE_THETA = 10000.0
Q_WIDTH = N_Q_HEADS * HEAD_DIM
KV_WIDTH = N_KV_HEADS * HEAD_DIM
QKV_WIDTH = Q_WIDTH + 2 * KV_WIDTH
CHUNK = 128
GMLP_WIDTH = 2 * D_MODEL
N_SGU_GROUPS = 8
SGU_GROUP_DIM = GMLP_WIDTH // N_SGU_GROUPS
N_EXPERT_GROUPS = 4
EXPERTS_PER_GROUP = 8
N_EXPERTS = N_EXPERT_GROUPS * EXPERTS_PER_GROUP
EXPERT_FF = D_MODEL // 4
N_MOD = 6
DEEPNORM_ALPHA = (2.0 * DEPTH) ** 0.25
LN_EPS = 1e-5

LANES = 128
MOD_ROWS = 8
ROUTER_LANES = 128
EXPERT_COL0 = 8
SORT_CHUNK = 256
EXPERT_TILE = 512
VMEM_LIMIT = 56 * 1024 * 1024
SC_CORES = 2
SC_WORKERS = 32
SC_CHUNK = 64


def _cparams(n_axes, vmem=VMEM_LIMIT):
    return pltpu.CompilerParams(dimension_semantics=("arbitrary",) * n_axes, vmem_limit_bytes=vmem)


U32 = jnp.uint32
ROW_CHUNKS = D_MODEL // 2 // LANES


def _store_rows(ref, val):
    n = val.shape[0]
    half = D_MODEL // 2
    lo = lax.bitcast_convert_type(val[:, :half].astype(BF16).astype(F32), U32)
    hi = lax.bitcast_convert_type(val[:, half:].astype(BF16).astype(F32), U32)
    words = (lo >> 16) | hi
    for c in range(ROW_CHUNKS):
        ref[pl.ds(c, n, stride=ROW_CHUNKS), :] = words[:, c * LANES:(c + 1) * LANES]


def _load_rows(ref, n):
    words = jnp.concatenate([ref[pl.ds(c, n, stride=ROW_CHUNKS), :] for c in range(ROW_CHUNKS)], axis=1)
    lo = lax.bitcast_convert_type(words << 16, F32)
    hi = lax.bitcast_convert_type(words & jnp.uint32(0xFFFF0000), F32)
    return jnp.concatenate([lo, hi], axis=1)


def _layer_norm(r, g, b):
    mu = jnp.mean(r, axis=-1, keepdims=True)
    d = r - mu
    var = jnp.mean(d * d, axis=-1, keepdims=True)
    return d * lax.rsqrt(var + LN_EPS) * g + b


def _mod_body(c_ref, w_ref, b_ref, o_ref):
    c = c_ref[...]
    ca = c * jax.nn.sigmoid(c)
    o_ref[...] = jnp.dot(ca.astype(BF16), w_ref[...].astype(BF16), preferred_element_type=F32) + b_ref[...]


def _adaln_mod(c_pad, ada_w, ada_b):
    tn = 1536
    n_out = N_MOD * D_MODEL
    return pl.pallas_call(
        _mod_body,
        grid=(DEPTH, n_out // tn),
        in_specs=[
            pl.BlockSpec((MOD_ROWS, D_MODEL), lambda l, j: (0, 0)),
            pl.BlockSpec((None, D_MODEL, tn), lambda l, j: (l, 0, j)),
            pl.BlockSpec((None, 1, tn), lambda l, j: (l, 0, j)),
        ],
        out_specs=pl.BlockSpec((None, MOD_ROWS, tn), lambda l, j: (l, 0, j)),
        out_shape=jax.ShapeDtypeStruct((DEPTH, MOD_ROWS, n_out), F32),
        compiler_params=_cparams(2),
        name="adaln_mod",
    )(c_pad, ada_w, ada_b.reshape(DEPTH, 1, n_out))


def _qkv_body(x_ref, mod_ref, pos_ref, w_ref, b_ref, invf_ref, o_ref, wbf_ref):
    @pl.when(pl.program_id(0) == 0)
    def _():
        wbf_ref[...] = w_ref[...].astype(BF16)

    tm = x_ref.shape[0]
    sh = mod_ref[:, 0:D_MODEL]
    sc = mod_ref[:, D_MODEL:2 * D_MODEL]
    h = x_ref[...] * (1.0 + sc) + sh
    qkv = jnp.dot(h.astype(BF16), wbf_ref[...], preferred_element_type=F32) + b_ref[...]

    ang = invf_ref[...] * pos_ref[...].astype(F32)
    c = jnp.cos(ang)
    s = jnp.sin(ang)
    ct = jnp.concatenate([c, c, c, c], axis=0).T
    st = jnp.concatenate([-s, s, -s, s], axis=0).T
    lane = lax.broadcasted_iota(I32, (tm, LANES), 1)
    first_half = (lane & (HEAD_DIM // 2)) == 0
    n_rope = (Q_WIDTH + KV_WIDTH) // LANES
    for j in range(n_rope):
        blk = qkv[:, j * LANES:(j + 1) * LANES]
        rot = jnp.where(first_half, pltpu.roll(blk, LANES - HEAD_DIM // 2, 1), pltpu.roll(blk, HEAD_DIM // 2, 1))
        r = blk * ct + rot * st
        if j < Q_WIDTH // LANES:
            r = r * (HEAD_DIM ** -0.5)
        o_ref[:, j * LANES:(j + 1) * LANES] = r.astype(BF16)
    o_ref[:, Q_WIDTH + KV_WIDTH:] = qkv[:, Q_WIDTH + KV_WIDTH:].astype(BF16)


def _qkv_rope(x2, mod_rows, layer, positions, w_qkv, b_qkv, seq):
    t = x2.shape[0]
    tm = 512
    steps_per_batch = seq // tm
    inv_freq = ROPE_THETA ** (-jnp.arange(0, HEAD_DIM, 2, dtype=F32) / HEAD_DIM)
    return pl.pallas_call(
        _qkv_body,
        grid=(t // tm,),
        in_specs=[
            pl.BlockSpec((tm, D_MODEL), lambda i: (i, 0)),
            pl.BlockSpec((None, 1, N_MOD * D_MODEL), lambda i: (layer * MOD_ROWS + i // steps_per_batch, 0, 0)),
            pl.BlockSpec((None, 1, tm), lambda i: (i, 0, 0)),
            pl.BlockSpec((D_MODEL, QKV_WIDTH), lambda i: (0, 0)),
            pl.BlockSpec((1, QKV_WIDTH), lambda i: (0, 0)),
            pl.BlockSpec((HEAD_DIM // 2, 1), lambda i: (0, 0)),
        ],
        out_specs=pl.BlockSpec((tm, QKV_WIDTH), lambda i: (i, 0)),
        out_shape=jax.ShapeDtypeStruct((t, QKV_WIDTH), BF16),
        scratch_shapes=[pltpu.VMEM((D_MODEL, QKV_WIDTH), BF16)],
        compiler_params=_cparams(1),
        name="qkv_rope",
    )(x2, mod_rows, positions.reshape(t // tm, 1, tm), w_qkv, b_qkv.reshape(1, QKV_WIDTH),
      inv_freq.reshape(HEAD_DIM // 2, 1))


BF16_ROWS = 16


def _attn_prepare(kv, kab_ref, vx_ref, slot):
    kv = kv.astype(F32)
    low = lax.broadcasted_iota(I32, (WINDOW, LANES), 1) < HEAD_DIM
    ones = jnp.ones((WINDOW, LANES), F32)
    for g in range(N_KV_HEADS):
        for part, ref in ((0, None), (KV_WIDTH, vx_ref)):
            tile = kv[:, part + (g // 2) * LANES:part + (g // 2 + 1) * LANES]
            other = pltpu.roll(tile, HEAD_DIM, 1)
            in_low, in_high = (tile, other) if g % 2 == 0 else (other, tile)
            if ref is None:
                kab_ref[slot, 2 * g] = jnp.where(low, in_low, 0.0).astype(BF16)
                kab_ref[slot, 2 * g + 1] = jnp.where(low, 0.0, in_high).astype(BF16)
            else:
                both = jnp.where(low, in_low, in_high)
                vx_ref[slot, g] = jnp.concatenate([both, ones], axis=1).astype(BF16)


def _attn_block(sink_ref, q, kab_ref, vx_ref, s_ref, p_ref, prev, cur, first_block):
    for g in range(N_KV_HEADS):
        q_pair = jnp.concatenate([q[:, (2 * g) * LANES:(2 * g + 1) * LANES],
                                  q[:, (2 * g + 1) * LANES:(2 * g + 2) * LANES]], axis=0)
        for a in range(2):
            kband = jnp.concatenate([kab_ref[prev, 2 * g + a], kab_ref[cur, 2 * g + a]], axis=0)
            s = lax.dot_general(q_pair, kband, (((1,), (1,)), ((), ())), preferred_element_type=F32)
            s_ref[GQA_GROUP * g + a] = s[:WINDOW]
            s_ref[GQA_GROUP * g + 2 + a] = s[WINDOW:]
    qi = lax.broadcasted_iota(I32, (WINDOW, 2 * WINDOW), 0) + WINDOW
    kj = lax.broadcasted_iota(I32, (WINDOW, 2 * WINDOW), 1)
    mask = (kj <= qi) & (kj > qi - WINDOW) & ((kj >= WINDOW) | jnp.logical_not(first_block))
    key0 = lax.broadcasted_iota(I32, (1, 2 * WINDOW), 1) == 0
    for h in range(N_Q_HEADS):
        s = jnp.where(mask, s_ref[h], jnp.where(key0, sink_ref[h], -jnp.inf))
        m = jnp.max(s, axis=-1, keepdims=True)
        p_ref[h] = jnp.exp(s - m).astype(BF16)
    low = lax.broadcasted_iota(I32, (WINDOW, LANES), 1) < HEAD_DIM
    sink_row = ((lax.broadcasted_iota(I32, (BF16_ROWS, 2 * LANES), 0) == 0)
                & (lax.broadcasted_iota(I32, (BF16_ROWS, 2 * LANES), 1) < LANES))
    out_tiles = []
    for g in range(N_KV_HEADS):
        v_prev = vx_ref[prev, g]
        v_head = jnp.where(sink_row, 0.0, v_prev[:BF16_ROWS].astype(F32)).astype(BF16)
        vband = jnp.concatenate([v_head, v_prev[BF16_ROWS:], vx_ref[cur, g]], axis=0)
        p4 = p_ref[GQA_GROUP * g:GQA_GROUP * (g + 1)].reshape(GQA_GROUP * WINDOW, 2 * WINDOW)
        o4 = jnp.dot(p4, vband, preferred_element_type=F32)
        heads = []
        for j in range(GQA_GROUP):
            blk = o4[j * WINDOW:(j + 1) * WINDOW]
            heads.append(blk[:, :LANES] / blk[:, LANES:])
        out_tiles.append(jnp.where(low, heads[0], heads[1]))
        out_tiles.append(jnp.where(low, heads[2], heads[3]))
    return jnp.concatenate(out_tiles, axis=1).astype(BF16)


def _attn_body(sink_ref, q_ref, kv_ref, o_ref, kab_ref, vx_ref, s_ref, p_ref):
    n = pl.program_id(1)

    @pl.when(n == 0)
    def _():
        kab_ref[1] = jnp.zeros(kab_ref.shape[1:], kab_ref.dtype)
        half = (N_KV_HEADS, WINDOW, LANES)
        vx_ref[1] = jnp.concatenate([jnp.zeros(half, BF16), jnp.ones(half, BF16)], axis=-1)

    scratch = (kab_ref, vx_ref, s_ref, p_ref)
    _attn_prepare(kv_ref[0:WINDOW, :], kab_ref, vx_ref, 0)
    o_ref[0:WINDOW, :] = _attn_block(sink_ref, q_ref[0:WINDOW, :], *scratch, 1, 0, n == 0)
    _attn_prepare(kv_ref[WINDOW:2 * WINDOW, :], kab_ref, vx_ref, 1)
    o_ref[WINDOW:2 * WINDOW, :] = _attn_block(sink_ref, q_ref[WINDOW:2 * WINDOW, :], *scratch, 0, 1, False)


def _attention(qkv, sinks, batch, seq):
    t = qkv.shape[0]
    tq = 2 * WINDOW
    steps = seq // tq
    kv_col = Q_WIDTH // (2 * KV_WIDTH)
    return pl.pallas_call(
        _attn_body,
        grid=(batch, steps),
        in_specs=[
            pl.BlockSpec(memory_space=pltpu.SMEM),
            pl.BlockSpec((tq, Q_WIDTH), lambda b, n: (b * steps + n, 0)),
            pl.BlockSpec((tq, 2 * KV_WIDTH), lambda b, n: (b * steps + n, kv_col)),
        ],
        out_specs=pl.BlockSpec((tq, Q_WIDTH), lambda b, n: (b * steps + n, 0)),
        out_shape=jax.ShapeDtypeStruct((t, Q_WIDTH), BF16),
        scratch_shapes=[
            pltpu.VMEM((2, 2 * N_KV_HEADS, WINDOW, LANES), BF16),
            pltpu.VMEM((2, N_KV_HEADS, WINDOW, 2 * LANES), BF16),
            pltpu.VMEM((N_Q_HEADS, WINDOW, 2 * WINDOW), F32),
            pltpu.VMEM((N_Q_HEADS, WINDOW, 2 * WINDOW), BF16),
        ],
        compiler_params=_cparams(2),
        name="swa_attention",
    )(sinks, qkv, qkv)


def _route(lt):
    tm = lt.shape[1]
    row = lax.broadcasted_iota(I32, (EXPERTS_PER_GROUP, tm), 0)
    neg = -jnp.inf
    gl = jnp.where(row < N_EXPERT_GROUPS, lt[0:EXPERTS_PER_GROUP], neg)
    gm = jnp.max(gl, axis=0, keepdims=True)
    g_p = 1.0 / jnp.sum(jnp.exp(gl - gm), axis=0, keepdims=True)
    g_idx = jnp.min(jnp.where(gl == gm, row, EXPERTS_PER_GROUP), axis=0, keepdims=True)
    sel = lt[EXPERT_COL0 + (N_EXPERT_GROUPS - 1) * EXPERTS_PER_GROUP:EXPERT_COL0 + N_EXPERTS]
    for g in range(N_EXPERT_GROUPS - 2, -1, -1):
        lo = EXPERT_COL0 + g * EXPERTS_PER_GROUP
        sel = jnp.where(g_idx == g, lt[lo:lo + EXPERTS_PER_GROUP], sel)
    v1 = jnp.max(sel, axis=0, keepdims=True)
    i1 = jnp.min(jnp.where(sel == v1, row, EXPERTS_PER_GROUP), axis=0, keepdims=True)
    sel2 = jnp.where(row == i1, neg, sel)
    v2 = jnp.max(sel2, axis=0, keepdims=True)
    i2 = jnp.min(jnp.where(sel2 == v2, row, EXPERTS_PER_GROUP), axis=0, keepdims=True)
    e2 = jnp.exp(v2 - v1)
    w1 = g_p / (1.0 + e2)
    w2 = g_p * e2 / (1.0 + e2)
    base = g_idx * EXPERTS_PER_GROUP
    return base + i1, base + i2, w1, w2


def _proj_body(o_ref, x_ref, mod_ref, w_ref, b_ref, lng_ref, lnb_ref, wr_ref, br_ref,
               x1_ref, h2_ref, ids_ref, wts_ref, wbf_ref):
    @pl.when(pl.program_id(0) == 0)
    def _():
        wbf_ref[...] = w_ref[...].astype(BF16)

    d = D_MODEL
    y = jnp.dot(o_ref[...], wbf_ref[...], preferred_element_type=F32) + b_ref[...]
    gt_m = mod_ref[:, 2 * d:3 * d]
    x1 = _layer_norm(DEEPNORM_ALPHA * x_ref[...] + (1.0 + gt_m) * y, lng_ref[...], lnb_ref[...])
    x1_ref[...] = x1
    sh_f = mod_ref[:, 3 * d:4 * d]
    sc_f = mod_ref[:, 4 * d:5 * d]
    h2 = x1 * (1.0 + sc_f) + sh_f
    _store_rows(h2_ref, h2)
    logits = jnp.dot(h2.astype(BF16), wr_ref[...].astype(BF16), preferred_element_type=F32) + br_ref[...]
    ea, eb, wa, wb = _route(logits.T)
    ids_ref[0:1, :] = ea
    ids_ref[1:2, :] = eb
    wts_ref[0:1, :] = wa
    wts_ref[1:2, :] = wb


def _proj_ln_router(o, x2, mod_rows, layer, w, b, ln_g, ln_b, w_router, b_router, seq):
    t, k = o.shape
    tm = 512
    steps_per_batch = seq // tm
    d = D_MODEL
    return pl.pallas_call(
        _proj_body,
        grid=(t // tm,),
        in_specs=[
            pl.BlockSpec((tm, k), lambda i: (i, 0)),
            pl.BlockSpec((tm, d), lambda i: (i, 0)),
            pl.BlockSpec((None, 1, N_MOD * d), lambda i: (layer * MOD_ROWS + i // steps_per_batch, 0, 0)),
            pl.BlockSpec((k, d), lambda i: (0, 0)),
            pl.BlockSpec((1, d), lambda i: (0, 0)),
            pl.BlockSpec((1, d), lambda i: (0, 0)),
            pl.BlockSpec((1, d), lambda i: (0, 0)),
            pl.BlockSpec((d, ROUTER_LANES), lambda i: (0, 0)),
            pl.BlockSpec((1, ROUTER_LANES), lambda i: (0, 0)),
        ],
        out_specs=[
            pl.BlockSpec((tm, d), lambda i: (i, 0)),
            pl.BlockSpec((tm * ROW_CHUNKS, LANES), lambda i: (i, 0)),
            pl.BlockSpec((2, tm), lambda i: (0, i)),
            pl.BlockSpec((2, tm), lambda i: (0, i)),
        ],
        out_shape=[
            jax.ShapeDtypeStruct((t, d), F32),
            jax.ShapeDtypeStruct((t * ROW_CHUNKS, LANES), U32),
            jax.ShapeDtypeStruct((2, t), I32),
            jax.ShapeDtypeStruct((2, t), F32),
        ],
        scratch_shapes=[pltpu.VMEM((k, d), BF16)],
        compiler_params=_cparams(1),
        name="proj_ln_router",
    )(o, x2, mod_rows, w, b.reshape(1, d), ln_g.reshape(1, d), ln_b.reshape(1, d), w_router, b_router)


def _sort_body(ids_ref, pos_ref, te_ref, nused_ref, tv_ref, rank_ref):
    n_rows = ids_ref.shape[0]
    c = SORT_CHUNK
    erow = lax.broadcasted_iota(I32, (N_EXPERTS, c), 0)
    tri = (lax.broadcasted_iota(I32, (c, c), 0) <= lax.broadcasted_iota(I32, (c, c), 1)).astype(BF16)

    def rank_step(r, carry):
        onehot = erow == ids_ref[pl.ds(r, 1), :]
        pref = jnp.dot(onehot.astype(BF16), tri, preferred_element_type=F32)
        rank = jnp.sum(jnp.where(onehot, pref + carry, 0.0), axis=0, keepdims=True) - 1.0
        rank_ref[pl.ds(r, 1), :] = rank
        return carry + pref[:, c - 1:c]

    counts = lax.fori_loop(0, n_rows, rank_step, jnp.zeros((N_EXPERTS, 1), F32))
    n_tile = jnp.floor((counts + (EXPERT_TILE - 1)) * (1.0 / EXPERT_TILE))
    low = (lax.broadcasted_iota(I32, (N_EXPERTS, N_EXPERTS), 1)
           <= lax.broadcasted_iota(I32, (N_EXPERTS, N_EXPERTS), 0)).astype(BF16)
    cum = jnp.dot(low, jnp.broadcast_to(n_tile, (N_EXPERTS, LANES)).astype(BF16),
                  preferred_element_type=F32)[:, 0:1]
    row_off = (cum - n_tile) * EXPERT_TILE

    def pos_step(r, _):
        onehot = erow == ids_ref[pl.ds(r, 1), :]
        off = jnp.sum(jnp.where(onehot, row_off, 0.0), axis=0, keepdims=True)
        pos_ref[pl.ds(r, 1), :] = (off + rank_ref[pl.ds(r, 1), :]).astype(I32)
        return 0

    lax.fori_loop(0, n_rows, pos_step, 0)
    total = jnp.max(cum, axis=0, keepdims=True)
    n_lanes = te_ref.shape[1]
    tile = jnp.minimum(lax.broadcasted_iota(I32, (N_EXPERTS, n_lanes), 1).astype(F32), total - 1.0)
    te_ref[...] = jnp.sum(jnp.where(cum <= tile, 1.0, 0.0), axis=0, keepdims=True).astype(I32)
    nused_ref[...] = jnp.broadcast_to(total, nused_ref.shape).astype(I32)
    tile_f = lax.broadcasted_iota(I32, (N_EXPERTS, n_lanes), 1).astype(F32)
    first = cum - n_tile
    rows_left = jnp.clip(counts - (tile_f - first) * EXPERT_TILE, 0.0, float(EXPERT_TILE))
    owns = (first <= tile_f) & (tile_f < cum)
    tv_ref[...] = jnp.sum(jnp.where(owns, rows_left, 0.0), axis=0, keepdims=True).astype(I32)


def _expert_sort(ids, n_tiles):
    n_assign = ids.shape[0] * ids.shape[1]
    n_rows = n_assign // SORT_CHUNK
    te_lanes = -(-n_tiles // LANES) * LANES
    pos, te, nused, tv = pl.pallas_call(
        _sort_body,
        grid=(1,),
        in_specs=[pl.BlockSpec((n_rows, SORT_CHUNK), lambda i: (0, 0))],
        out_specs=[
            pl.BlockSpec((n_rows, SORT_CHUNK), lambda i: (0, 0)),
            pl.BlockSpec((1, te_lanes), lambda i: (0, 0)),
            pl.BlockSpec((1, LANES), lambda i: (0, 0)),
            pl.BlockSpec((1, te_lanes), lambda i: (0, 0)),
        ],
        out_shape=[
            jax.ShapeDtypeStruct((n_rows, SORT_CHUNK), I32),
            jax.ShapeDtypeStruct((1, te_lanes), I32),
            jax.ShapeDtypeStruct((1, LANES), I32),
            jax.ShapeDtypeStruct((1, te_lanes), I32),
        ],
        scratch_shapes=[pltpu.VMEM((n_rows, SORT_CHUNK), F32)],
        compiler_params=_cparams(1),
        name="expert_sort",
    )(ids.reshape(n_rows, SORT_CHUNK))
    return pos, te[0, :n_tiles], nused[0, :1], tv[0, :n_tiles]


def _sc_mesh():
    return plsc.VectorSubcoreMesh(core_axis_name="c", subcore_axis_name="s", num_cores=SC_CORES,
                                  num_subcores=SC_WORKERS // SC_CORES)


def _sc_scatter_rows(src, pos2, n_rows):
    t = pos2.shape[1]
    src3 = src.reshape(t, ROW_CHUNKS, LANES)
    per_worker = t // SC_WORKERS
    n_chunks = per_worker // SC_CHUNK
    idx = pos2.reshape(2, SC_WORKERS, n_chunks, SC_CHUNK).transpose(1, 0, 2, 3)
    idx = idx.reshape(SC_WORKERS, 2 * n_chunks, SC_CHUNK)

    @functools.partial(
        pl.kernel, mesh=_sc_mesh(),
        out_type=jax.ShapeDtypeStruct((n_rows, ROW_CHUNKS, LANES), src.dtype),
        scratch_types=[
            pltpu.VMEM((2 * n_chunks, SC_CHUNK), I32),
            pltpu.VMEM((SC_CHUNK, ROW_CHUNKS, LANES), src.dtype),
            pltpu.SemaphoreType.DMA,
        ],
        name="sc_scatter_rows",
    )
    def scatter(src_hbm, idx_hbm, out_hbm, idx_v, rows_v, sem):
        wid = lax.axis_index("s") * SC_CORES + lax.axis_index("c")
        pltpu.sync_copy(idx_hbm.at[wid], idx_v)

        @pl.loop(0, n_chunks)
        def _(j):
            pltpu.sync_copy(src_hbm.at[pl.ds(wid * per_worker + j * SC_CHUNK, SC_CHUNK)], rows_v)
            for k in range(2):
                pltpu.async_copy(rows_v, out_hbm.at[idx_v.at[k * n_chunks + j]], sem).wait()

    return scatter(src3, idx).reshape(n_rows * ROW_CHUNKS, LANES)


def _sc_gather_rows(table, idx):
    n = idx.shape[0]
    table3 = table.reshape(-1, ROW_CHUNKS, LANES)
    per_worker = n // SC_WORKERS
    n_chunks = per_worker // SC_CHUNK

    @functools.partial(
        pl.kernel, mesh=_sc_mesh(),
        out_type=jax.ShapeDtypeStruct((n, ROW_CHUNKS, LANES), table.dtype),
        scratch_types=[
            pltpu.VMEM((n_chunks, SC_CHUNK), I32),
            pltpu.VMEM((SC_CHUNK, ROW_CHUNKS, LANES), table.dtype),
            pltpu.SemaphoreType.DMA,
        ],
        name="sc_gather_rows",
    )
    def gather(table_hbm, idx_hbm, out_hbm, idx_v, rows_v, sem):
        wid = lax.axis_index("s") * SC_CORES + lax.axis_index("c")
        pltpu.sync_copy(idx_hbm.at[wid], idx_v)

        @pl.loop(0, n_chunks)
        def _(j):
            pltpu.async_copy(table_hbm.at[idx_v.at[j]], rows_v, sem).wait()
            pltpu.sync_copy(rows_v, out_hbm.at[pl.ds(wid * per_worker + j * SC_CHUNK, SC_CHUNK)])

    out = gather(table3, idx.reshape(SC_WORKERS, n_chunks, SC_CHUNK))
    return out.reshape(n * ROW_CHUNKS, LANES)


def _expert_body(te_ref, nused_ref, tv_ref, xs_ref, wgu_ref, wd_ref, ys_ref, wgu_bf, wd_bf):
    i = pl.program_id(0)
    used = i < nused_ref[0]
    prev = te_ref[jnp.maximum(i - 1, 0)]

    @pl.when(used & ((i == 0) | (te_ref[i] != prev)))
    def _():
        wgu_bf[...] = wgu_ref[...].astype(BF16)
        wd_bf[...] = wd_ref[...].astype(BF16)

    @pl.when(used)
    def _():
        live = lax.broadcasted_iota(I32, (EXPERT_TILE, 1), 0) < tv_ref[i]
        xs = jnp.where(live, _load_rows(xs_ref, EXPERT_TILE), 0.0).astype(BF16)
        gu = jnp.dot(xs, wgu_bf[...], preferred_element_type=F32)
        gate = gu[:, :EXPERT_FF]
        up = gu[:, EXPERT_FF:]
        act = gate * jax.nn.sigmoid(gate) * up
        _store_rows(ys_ref, jnp.dot(act.astype(BF16), wd_bf[...], preferred_element_type=F32))

    @pl.when(jnp.logical_not(used))
    def _():
        ys_ref[...] = jnp.zeros(ys_ref.shape, ys_ref.dtype)


def _expert_mlp(tile_expert, n_used, tile_valid, xs, w_gate_up, w_down, layer):
    d = D_MODEL
    n_tiles = xs.shape[0] // (EXPERT_TILE * ROW_CHUNKS)
    f2 = 2 * EXPERT_FF
    tile_rows = EXPERT_TILE * ROW_CHUNKS
    e0 = layer * N_EXPERTS
    grid_spec = pltpu.PrefetchScalarGridSpec(
        num_scalar_prefetch=3,
        grid=(n_tiles,),
        in_specs=[
            pl.BlockSpec((tile_rows, LANES), lambda i, te, nu, tv: (jnp.minimum(i, nu[0] - 1), 0)),
            pl.BlockSpec((None, d, f2), lambda i, te, nu, tv: (e0 + te[i], 0, 0)),
            pl.BlockSpec((None, EXPERT_FF, d), lambda i, te, nu, tv: (e0 + te[i], 0, 0)),
        ],
        out_specs=pl.BlockSpec((tile_rows, LANES), lambda i, te, nu, tv: (i, 0)),
        scratch_shapes=[pltpu.VMEM((d, f2), BF16), pltpu.VMEM((EXPERT_FF, d), BF16)],
    )
    return pl.pallas_call(
        _expert_body,
        grid_spec=grid_spec,
        out_shape=jax.ShapeDtypeStruct(xs.shape, U32),
        compiler_params=_cparams(1),
        name="expert_mlp",
    )(tile_expert, n_used, tile_valid, xs, w_gate_up.reshape(DEPTH * N_EXPERTS, d, f2),
      w_down.reshape(DEPTH * N_EXPERTS, EXPERT_FF, d))


def _combine_body(wts_ref, x_ref, mod_ref, lng_ref, lnb_ref, ya_ref, yb_ref, o_ref):
    tm = x_ref.shape[0]
    d = D_MODEL
    cols = []
    for k in range(2):
        wt = jnp.broadcast_to(wts_ref[k:k + 1, :], (LANES, tm)).T
        cols.append(jnp.concatenate([wt] * (d // LANES), axis=1))
    y = cols[0] * _load_rows(ya_ref, tm) + cols[1] * _load_rows(yb_ref, tm)
    gt_f = mod_ref[:, 5 * d:6 * d]
    o_ref[...] = _layer_norm(DEEPNORM_ALPHA * x_ref[...] + (1.0 + gt_f) * y, lng_ref[...], lnb_ref[...])


def _combine_ln(pos2, wts, x1, mod_rows, layer, ln_g, ln_b, ys, seq):
    t, d = x1.shape
    tm = 512
    steps_per_batch = seq // tm
    n_steps = t // tm
    yg = _sc_gather_rows(ys, pos2.reshape(2 * t))
    return pl.pallas_call(
        _combine_body,
        grid=(n_steps,),
        in_specs=[
            pl.BlockSpec((2, tm), lambda i: (0, i)),
            pl.BlockSpec((tm, d), lambda i: (i, 0)),
            pl.BlockSpec((None, 1, N_MOD * d), lambda i: (layer * MOD_ROWS + i // steps_per_batch, 0, 0)),
            pl.BlockSpec((1, d), lambda i: (0, 0)),
            pl.BlockSpec((1, d), lambda i: (0, 0)),
            pl.BlockSpec((tm * ROW_CHUNKS, LANES), lambda i: (i, 0)),
            pl.BlockSpec((tm * ROW_CHUNKS, LANES), lambda i: (n_steps + i, 0)),
        ],
        out_specs=pl.BlockSpec((tm, d), lambda i: (i, 0)),
        out_shape=jax.ShapeDtypeStruct((t, d), F32),
        compiler_params=_cparams(1),
        name="moe_combine_ln",
    )(wts, x1, mod_rows, ln_g.reshape(1, d), ln_b.reshape(1, d), yg, yg)


def _gmlp_body(x_ref, mod_ref, w_ref, b_ref, g_ref, beta_ref, ws_ref, bs_ref, o_ref, ws_bf):
    @pl.when(pl.program_id(0) == 0)
    def _():
        tri = lax.broadcasted_iota(I32, (CHUNK, CHUNK), 0) >= lax.broadcasted_iota(I32, (CHUNK, CHUNK), 1)
        for g in range(N_SGU_GROUPS):
            ws_bf[g] = jnp.where(tri, ws_ref[g], 0.0).astype(BF16)

    tm = x_ref.shape[0]
    d = D_MODEL
    sh = mod_ref[:, 0:d]
    sc = mod_ref[:, d:2 * d]
    h = x_ref[...] * (1.0 + sc) + sh
    z = jnp.dot(h.astype(BF16), w_ref[...], preferred_element_type=F32) + b_ref[...]
    z = 0.5 * z * (1.0 + lax.erf(z * (2.0 ** -0.5)))
    u = z[:, :GMLP_WIDTH]
    v = _layer_norm(z[:, GMLP_WIDTH:], g_ref[...], beta_ref[...]).astype(BF16)
    for ci in range(tm // CHUNK):
        rows = slice(ci * CHUNK, (ci + 1) * CHUNK)
        for g in range(N_SGU_GROUPS):
            lanes = slice(g * SGU_GROUP_DIM, (g + 1) * SGU_GROUP_DIM)
            mixed = jnp.dot(ws_bf[g], v[rows, lanes], preferred_element_type=F32) + bs_ref[:, g:g + 1]
            o_ref[rows, lanes] = (u[rows, lanes] * mixed).astype(BF16)


def _gmlp_gate(x2, mod_rows, layer, w_in_bf, b_in, sgu_g, sgu_b, w_s, b_s, seq):
    t, d = x2.shape
    tm = 512
    steps_per_batch = seq // tm
    gw = GMLP_WIDTH
    return pl.pallas_call(
        _gmlp_body,
        grid=(t // tm,),
        in_specs=[
            pl.BlockSpec((tm, d), lambda i: (i, 0)),
            pl.BlockSpec((None, 1, N_MOD * d), lambda i: (layer * MOD_ROWS + i // steps_per_batch, 0, 0)),
            pl.BlockSpec((d, 2 * gw), lambda i: (0, 0)),
            pl.BlockSpec((1, 2 * gw), lambda i: (0, 0)),
            pl.BlockSpec((1, gw), lambda i: (0, 0)),
            pl.BlockSpec((1, gw), lambda i: (0, 0)),
            pl.BlockSpec((N_SGU_GROUPS, CHUNK, CHUNK), lambda i: (0, 0, 0)),
            pl.BlockSpec((CHUNK, N_SGU_GROUPS), lambda i: (0, 0)),
        ],
        out_specs=pl.BlockSpec((tm, gw), lambda i: (i, 0)),
        out_shape=jax.ShapeDtypeStruct((t, gw), BF16),
        scratch_shapes=[pltpu.VMEM((N_SGU_GROUPS, CHUNK, CHUNK), BF16)],
        compiler_params=_cparams(1),
        name="gmlp_gate",
    )(x2, mod_rows, w_in_bf, b_in.reshape(1, 2 * gw), sgu_g.reshape(1, gw), sgu_b.reshape(1, gw), w_s, b_s.T)


def _router_params(w_group, b_group, w_expert, b_expert):
    d = w_group.shape[0]
    w = jnp.zeros((d, ROUTER_LANES), F32)
    w = w.at[:, :N_EXPERT_GROUPS].set(w_group).at[:, EXPERT_COL0:EXPERT_COL0 + N_EXPERTS].set(w_expert)
    b = jnp.zeros((1, ROUTER_LANES), F32)
    b = b.at[0, :N_EXPERT_GROUPS].set(b_group).at[0, EXPERT_COL0:EXPERT_COL0 + N_EXPERTS].set(b_expert)
    return w, b


def _moe_block(x1, h2, ids, wts, mod_rows, layer, ln_g, ln_b, w_gate_up, w_down, seq):
    t = x1.shape[0]
    n_rows = 2 * t + N_EXPERTS * EXPERT_TILE
    n_tiles = n_rows // EXPERT_TILE
    pos, tile_expert, n_used, tile_valid = _expert_sort(ids, n_tiles)
    pos2 = pos.reshape(2, t)
    xs = _sc_scatter_rows(h2, pos2, n_rows)
    ys = _expert_mlp(tile_expert, n_used, tile_valid, xs, w_gate_up, w_down, layer)
    return _combine_ln(pos2, wts, x1, mod_rows, layer, ln_g, ln_b, ys, seq)


def kernel(x, c, positions, ada_w, ada_b, post_ln_g, post_ln_b, attn_w_qkv, attn_b_qkv, attn_sinks, attn_w_o, attn_b_o, gmlp_w_in, gmlp_b_in, gmlp_sgu_ln_g, gmlp_sgu_ln_b, gmlp_w_s, gmlp_b_s, gmlp_w_out, gmlp_b_out, moe_w_group_router, moe_b_group_router, moe_w_expert_router, moe_b_expert_router, moe_w_gate_up, moe_w_down):
    batch, seq, d = x.shape
    t = batch * seq
    x2 = x.reshape(t, d)
    c_pad = jnp.zeros((MOD_ROWS, d), F32).at[:batch].set(c)
    mod_rows = _adaln_mod(c_pad, ada_w, ada_b).reshape(DEPTH * MOD_ROWS, 1, N_MOD * d)

    for layer in range(DEPTH):
        j = layer // 2
        if layer % 2 == 0:
            qkv = _qkv_rope(x2, mod_rows, layer, positions, attn_w_qkv[j], attn_b_qkv[j], seq)
            mix = _attention(qkv, attn_sinks[j], batch, seq)
            w_out, b_out = attn_w_o[j], attn_b_o[j]
        else:
            mix = _gmlp_gate(x2, mod_rows, layer, gmlp_w_in[j].astype(BF16), gmlp_b_in[j], gmlp_sgu_ln_g[j],
                             gmlp_sgu_ln_b[j], gmlp_w_s[j], gmlp_b_s[j], seq)
            w_out, b_out = gmlp_w_out[j], gmlp_b_out[j]
        w_router, b_router = _router_params(moe_w_group_router[layer], moe_b_group_router[layer],
                                            moe_w_expert_router[layer], moe_b_expert_router[layer])
        x1, h2, ids, wts = _proj_ln_router(mix, x2, mod_rows, layer, w_out, b_out, post_ln_g[layer, 0],
                                           post_ln_b[layer, 0], w_router, b_router, seq)
        x2 = _moe_block(x1, h2, ids, wts, mod_rows, layer, post_ln_g[layer, 1], post_ln_b[layer, 1],
                        moe_w_gate_up, moe_w_down, seq)
    return x2.reshape(batch, seq, d)
```

```python
import functools

import jax
import jax.numpy as jnp
from jax import lax
from jax.experimental import pallas as pl
from jax.experimental.pallas import tpu as pltpu
from jax.experimental.pallas import tpu_sc as plsc

F32 = jnp.float32
BF16 = jnp.bfloat16
I32 = jnp.int32

D_MODEL = 1024
DEPTH = 2
HEAD_DIM = 64
N_Q_HEADS = 16
N_KV_HEADS = 4
GQA_GROUP = N_Q_HEADS // N_KV_HEADS
WINDOW = 128
ROPE_THETA = 10000.0
Q_WIDTH = N_Q_HEADS * HEAD_DIM
KV_WIDTH = N_KV_HEADS * HEAD_DIM
QKV_WIDTH = Q_WIDTH + 2 * KV_WIDTH
CHUNK = 128
GMLP_WIDTH = 2 * D_MODEL
N_SGU_GROUPS = 8
SGU_GROUP_DIM = GMLP_WIDTH // N_SGU_GROUPS
N_EXPERT_GROUPS = 4
EXPERTS_PER_GROUP = 8
N_EXPERTS = N_EXPERT_GROUPS * EXPERTS_PER_GROUP
EXPERT_FF = D_MODEL // 4
N_MOD = 6
DEEPNORM_ALPHA = (2.0 * DEPTH) ** 0.25
LN_EPS = 1e-5

LANES = 128
MOD_ROWS = 8
ROUTER_LANES = 128
EXPERT_COL0 = 8
SORT_CHUNK = 256
EXPERT_TILE = 512
SUB_ROWS = 256
VMEM_LIMIT = 56 * 1024 * 1024
SC_CORES = 2
SC_WORKERS = 32
SC_CHUNK = 64


def _cparams(n_axes, vmem=VMEM_LIMIT):
    return pltpu.CompilerParams(dimension_semantics=("arbitrary",) * n_axes, vmem_limit_bytes=vmem)


U32 = jnp.uint32
ROW_CHUNKS = D_MODEL // 2 // LANES


def _store_rows(ref, val, token0=0):
    n = val.shape[0]
    half = D_MODEL // 2
    lo = lax.bitcast_convert_type(val[:, :half].astype(BF16).astype(F32), U32)
    hi = lax.bitcast_convert_type(val[:, half:].astype(BF16).astype(F32), U32)
    words = (lo >> 16) | hi
    for c in range(ROW_CHUNKS):
        ref[pl.ds(token0 * ROW_CHUNKS + c, n, stride=ROW_CHUNKS), :] = words[:, c * LANES:(c + 1) * LANES]


def _load_rows(ref, n):
    words = jnp.concatenate([ref[pl.ds(c, n, stride=ROW_CHUNKS), :] for c in range(ROW_CHUNKS)], axis=1)
    lo = lax.bitcast_convert_type(words << 16, F32)
    hi = lax.bitcast_convert_type(words & jnp.uint32(0xFFFF0000), F32)
    return jnp.concatenate([lo, hi], axis=1)


def _layer_norm(r, g, b):
    mu = jnp.mean(r, axis=-1, keepdims=True)
    d = r - mu
    var = jnp.mean(d * d, axis=-1, keepdims=True)
    return d * lax.rsqrt(var + LN_EPS) * g + b


def _mod_body(c_ref, w_ref, b_ref, o_ref):
    c = c_ref[...]
    ca = c * jax.nn.sigmoid(c)
    o_ref[...] = jnp.dot(ca.astype(BF16), w_ref[...].astype(BF16), preferred_element_type=F32) + b_ref[...]


def _adaln_mod(c_pad, ada_w, ada_b):
    tn = 1536
    n_out = N_MOD * D_MODEL
    return pl.pallas_call(
        _mod_body,
        grid=(DEPTH, n_out // tn),
        in_specs=[
            pl.BlockSpec((MOD_ROWS, D_MODEL), lambda l, j: (0, 0)),
            pl.BlockSpec((None, D_MODEL, tn), lambda l, j: (l, 0, j)),
            pl.BlockSpec((None, 1, tn), lambda l, j: (l, 0, j)),
        ],
        out_specs=pl.BlockSpec((None, MOD_ROWS, tn), lambda l, j: (l, 0, j)),
        out_shape=jax.ShapeDtypeStruct((DEPTH, MOD_ROWS, n_out), F32),
        compiler_params=_cparams(2),
        name="adaln_mod",
    )(c_pad, ada_w, ada_b.reshape(DEPTH, 1, n_out))


def _qkv_body(x_ref, mod_ref, pos_ref, w_ref, b_ref, invf_ref, o_ref, wbf_ref):
    @pl.when(pl.program_id(0) == 0)
    def _():
        wbf_ref[...] = w_ref[...].astype(BF16)

    tm = x_ref.shape[0]
    sh = mod_ref[:, 0:D_MODEL]
    sc = mod_ref[:, D_MODEL:2 * D_MODEL]
    h = x_ref[...] * (1.0 + sc) + sh
    qkv = jnp.dot(h.astype(BF16), wbf_ref[...], preferred_element_type=F32) + b_ref[...]

    ang = invf_ref[...] * pos_ref[...].astype(F32)
    c = jnp.cos(ang)
    s = jnp.sin(ang)
    ct = jnp.concatenate([c, c, c, c], axis=0).T
    st = jnp.concatenate([-s, s, -s, s], axis=0).T
    lane = lax.broadcasted_iota(I32, (tm, LANES), 1)
    first_half = (lane & (HEAD_DIM // 2)) == 0
    n_rope = (Q_WIDTH + KV_WIDTH) // LANES
    for j in range(n_rope):
        blk = qkv[:, j * LANES:(j + 1) * LANES]
        rot = jnp.where(first_half, pltpu.roll(blk, LANES - HEAD_DIM // 2, 1), pltpu.roll(blk, HEAD_DIM // 2, 1))
        r = blk * ct + rot * st
        if j < Q_WIDTH // LANES:
            r = r * (HEAD_DIM ** -0.5)
        o_ref[:, j * LANES:(j + 1) * LANES] = r.astype(BF16)
    o_ref[:, Q_WIDTH + KV_WIDTH:] = qkv[:, Q_WIDTH + KV_WIDTH:].astype(BF16)


def _qkv_rope(x2, mod_rows, layer, positions, w_qkv, b_qkv, seq):
    t = x2.shape[0]
    tm = 512
    steps_per_batch = seq // tm
    inv_freq = ROPE_THETA ** (-jnp.arange(0, HEAD_DIM, 2, dtype=F32) / HEAD_DIM)
    return pl.pallas_call(
        _qkv_body,
        grid=(t // tm,),
        in_specs=[
            pl.BlockSpec((tm, D_MODEL), lambda i: (i, 0)),
            pl.BlockSpec((None, 1, N_MOD * D_MODEL), lambda i: (layer * MOD_ROWS + i // steps_per_batch, 0, 0)),
            pl.BlockSpec((None, 1, tm), lambda i: (i, 0, 0)),
            pl.BlockSpec((D_MODEL, QKV_WIDTH), lambda i: (0, 0)),
            pl.BlockSpec((1, QKV_WIDTH), lambda i: (0, 0)),
            pl.BlockSpec((HEAD_DIM // 2, 1), lambda i: (0, 0)),
        ],
        out_specs=pl.BlockSpec((tm, QKV_WIDTH), lambda i: (i, 0)),
        out_shape=jax.ShapeDtypeStruct((t, QKV_WIDTH), BF16),
        scratch_shapes=[pltpu.VMEM((D_MODEL, QKV_WIDTH), BF16)],
        compiler_params=_cparams(1),
        name="qkv_rope",
    )(x2, mod_rows, positions.reshape(t // tm, 1, tm), w_qkv, b_qkv.reshape(1, QKV_WIDTH),
      inv_freq.reshape(HEAD_DIM // 2, 1))


BF16_ROWS = 16


def _attn_prepare(kv, kab_ref, vx_ref, slot):
    kv = kv.astype(F32)
    low = lax.broadcasted_iota(I32, (WINDOW, LANES), 1) < HEAD_DIM
    ones = jnp.ones((WINDOW, LANES), F32)
    for g in range(N_KV_HEADS):
        for part, ref in ((0, None), (KV_WIDTH, vx_ref)):
            tile = kv[:, part + (g // 2) * LANES:part + (g // 2 + 1) * LANES]
            other = pltpu.roll(tile, HEAD_DIM, 1)
            in_low, in_high = (tile, other) if g % 2 == 0 else (other, tile)
            if ref is None:
                kab_ref[slot, 2 * g] = jnp.where(low, in_low, 0.0).astype(BF16)
                kab_ref[slot, 2 * g + 1] = jnp.where(low, 0.0, in_high).astype(BF16)
            else:
                both = jnp.where(low, in_low, in_high)
                vx_ref[slot, g] = jnp.concatenate([both, ones], axis=1).astype(BF16)


def _attn_block(sink_ref, q, kab_ref, vx_ref, s_ref, p_ref, prev, cur, first_block):
    for g in range(N_KV_HEADS):
        q_pair = jnp.concatenate([q[:, (2 * g) * LANES:(2 * g + 1) * LANES],
                                  q[:, (2 * g + 1) * LANES:(2 * g + 2) * LANES]], axis=0)
        for a in range(2):
            kband = jnp.concatenate([kab_ref[prev, 2 * g + a], kab_ref[cur, 2 * g + a]], axis=0)
            s = lax.dot_general(q_pair, kband, (((1,), (1,)), ((), ())), preferred_element_type=F32)
            s_ref[GQA_GROUP * g + a] = s[:WINDOW]
            s_ref[GQA_GROUP * g + 2 + a] = s[WINDOW:]
    qi = lax.broadcasted_iota(I32, (WINDOW, 2 * WINDOW), 0) + WINDOW
    kj = lax.broadcasted_iota(I32, (WINDOW, 2 * WINDOW), 1)
    mask = (kj <= qi) & (kj > qi - WINDOW) & ((kj >= WINDOW) | jnp.logical_not(first_block))
    key0 = lax.broadcasted_iota(I32, (1, 2 * WINDOW), 1) == 0
    for h in range(N_Q_HEADS):
        s = jnp.where(mask, s_ref[h], jnp.where(key0, sink_ref[h], -jnp.inf))
        m = jnp.max(s, axis=-1, keepdims=True)
        p_ref[h] = jnp.exp(s - m).astype(BF16)
    low = lax.broadcasted_iota(I32, (WINDOW, LANES), 1) < HEAD_DIM
    sink_row = ((lax.broadcasted_iota(I32, (BF16_ROWS, 2 * LANES), 0) == 0)
                & (lax.broadcasted_iota(I32, (BF16_ROWS, 2 * LANES), 1) < LANES))
    out_tiles = []
    for g in range(N_KV_HEADS):
        v_prev = vx_ref[prev, g]
        v_head = jnp.where(sink_row, 0.0, v_prev[:BF16_ROWS].astype(F32)).astype(BF16)
        vband = jnp.concatenate([v_head, v_prev[BF16_ROWS:], vx_ref[cur, g]], axis=0)
        p4 = p_ref[GQA_GROUP * g:GQA_GROUP * (g + 1)].reshape(GQA_GROUP * WINDOW, 2 * WINDOW)
        o4 = jnp.dot(p4, vband, preferred_element_type=F32)
        heads = []
        for j in range(GQA_GROUP):
            blk = o4[j * WINDOW:(j + 1) * WINDOW]
            heads.append(blk[:, :LANES] / blk[:, LANES:])
        out_tiles.append(jnp.where(low, heads[0], heads[1]))
        out_tiles.append(jnp.where(low, heads[2], heads[3]))
    return jnp.concatenate(out_tiles, axis=1).astype(BF16)


def _attn_body(sink_ref, q_ref, kv_ref, o_ref, kab_ref, vx_ref, s_ref, p_ref):
    n = pl.program_id(1)

    @pl.when(n == 0)
    def _():
        kab_ref[1] = jnp.zeros(kab_ref.shape[1:], kab_ref.dtype)
        half = (N_KV_HEADS, WINDOW, LANES)
        vx_ref[1] = jnp.concatenate([jnp.zeros(half, BF16), jnp.ones(half, BF16)], axis=-1)

    scratch = (kab_ref, vx_ref, s_ref, p_ref)
    _attn_prepare(kv_ref[0:WINDOW, :], kab_ref, vx_ref, 0)
    o_ref[0:WINDOW, :] = _attn_block(sink_ref, q_ref[0:WINDOW, :], *scratch, 1, 0, n == 0)
    _attn_prepare(kv_ref[WINDOW:2 * WINDOW, :], kab_ref, vx_ref, 1)
    o_ref[WINDOW:2 * WINDOW, :] = _attn_block(sink_ref, q_ref[WINDOW:2 * WINDOW, :], *scratch, 0, 1, False)


def _attention(qkv, sinks, batch, seq):
    t = qkv.shape[0]
    tq = 2 * WINDOW
    steps = seq // tq
    kv_col = Q_WIDTH // (2 * KV_WIDTH)
    return pl.pallas_call(
        _attn_body,
        grid=(batch, steps),
        in_specs=[
            pl.BlockSpec(memory_space=pltpu.SMEM),
            pl.BlockSpec((tq, Q_WIDTH), lambda b, n: (b * steps + n, 0)),
            pl.BlockSpec((tq, 2 * KV_WIDTH), lambda b, n: (b * steps + n, kv_col)),
        ],
        out_specs=pl.BlockSpec((tq, Q_WIDTH), lambda b, n: (b * steps + n, 0)),
        out_shape=jax.ShapeDtypeStruct((t, Q_WIDTH), BF16),
        scratch_shapes=[
            pltpu.VMEM((2, 2 * N_KV_HEADS, WINDOW, LANES), BF16),
            pltpu.VMEM((2, N_KV_HEADS, WINDOW, 2 * LANES), BF16),
            pltpu.VMEM((N_Q_HEADS, WINDOW, 2 * WINDOW), F32),
            pltpu.VMEM((N_Q_HEADS, WINDOW, 2 * WINDOW), BF16),
        ],
        compiler_params=_cparams(2),
        name="swa_attention",
    )(sinks, qkv, qkv)


def _route(lt):
    tm = lt.shape[1]
    row = lax.broadcasted_iota(I32, (EXPERTS_PER_GROUP, tm), 0)
    neg = -jnp.inf
    gl = jnp.where(row < N_EXPERT_GROUPS, lt[0:EXPERTS_PER_GROUP], neg)
    gm = jnp.max(gl, axis=0, keepdims=True)
    g_p = 1.0 / jnp.sum(jnp.exp(gl - gm), axis=0, keepdims=True)
    g_idx = jnp.min(jnp.where(gl == gm, row, EXPERTS_PER_GROUP), axis=0, keepdims=True)
    sel = lt[EXPERT_COL0 + (N_EXPERT_GROUPS - 1) * EXPERTS_PER_GROUP:EXPERT_COL0 + N_EXPERTS]
    for g in range(N_EXPERT_GROUPS - 2, -1, -1):
        lo = EXPERT_COL0 + g * EXPERTS_PER_GROUP
        sel = jnp.where(g_idx == g, lt[lo:lo + EXPERTS_PER_GROUP], sel)
    v1 = jnp.max(sel, axis=0, keepdims=True)
    i1 = jnp.min(jnp.where(sel == v1, row, EXPERTS_PER_GROUP), axis=0, keepdims=True)
    sel2 = jnp.where(row == i1, neg, sel)
    v2 = jnp.max(sel2, axis=0, keepdims=True)
    i2 = jnp.min(jnp.where(sel2 == v2, row, EXPERTS_PER_GROUP), axis=0, keepdims=True)
    e2 = jnp.exp(v2 - v1)
    w1 = g_p / (1.0 + e2)
    w2 = g_p * e2 / (1.0 + e2)
    base = g_idx * EXPERTS_PER_GROUP
    return base + i1, base + i2, w1, w2


def _proj_body(o_ref, x_ref, mod_ref, w_ref, b_ref, lng_ref, lnb_ref, wr_ref, br_ref,
               x1_ref, h2_ref, ids_ref, wts_ref, wbf_ref):
    @pl.when(pl.program_id(0) == 0)
    def _():
        wbf_ref[...] = w_ref[...].astype(BF16)

    d = D_MODEL
    gt_m = mod_ref[:, 2 * d:3 * d]
    sh_f = mod_ref[:, 3 * d:4 * d]
    sc_f = mod_ref[:, 4 * d:5 * d]
    wr = wr_ref[...].astype(BF16)
    for r in range(x_ref.shape[0] // SUB_ROWS):
        rows = slice(r * SUB_ROWS, (r + 1) * SUB_ROWS)
        y = jnp.dot(o_ref[rows, :], wbf_ref[...], preferred_element_type=F32) + b_ref[...]
        x1 = _layer_norm(DEEPNORM_ALPHA * x_ref[rows, :] + (1.0 + gt_m) * y, lng_ref[...], lnb_ref[...])
        x1_ref[rows, :] = x1
        h2 = x1 * (1.0 + sc_f) + sh_f
        _store_rows(h2_ref, h2, r * SUB_ROWS)
        logits = jnp.dot(h2.astype(BF16), wr, preferred_element_type=F32) + br_ref[...]
        ea, eb, wa, wb = _route(logits.T)
        ids_ref[0:1, rows] = ea
        ids_ref[1:2, rows] = eb
        wts_ref[0:1, rows] = wa
        wts_ref[1:2, rows] = wb


def _proj_ln_router(o, x2, mod_rows, layer, w, b, ln_g, ln_b, w_router, b_router, seq):
    t, k = o.shape
    tm = 512
    steps_per_batch = seq // tm
    d = D_MODEL
    return pl.pallas_call(
        _proj_body,
        grid=(t // tm,),
        in_specs=[
            pl.BlockSpec((tm, k), lambda i: (i, 0)),
            pl.BlockSpec((tm, d), lambda i: (i, 0)),
            pl.BlockSpec((None, 1, N_MOD * d), lambda i: (layer * MOD_ROWS + i // steps_per_batch, 0, 0)),
            pl.BlockSpec((k, d), lambda i: (0, 0)),
            pl.BlockSpec((1, d), lambda i: (0, 0)),
            pl.BlockSpec((1, d), lambda i: (0, 0)),
            pl.BlockSpec((1, d), lambda i: (0, 0)),
            pl.BlockSpec((d, ROUTER_LANES), lambda i: (0, 0)),
            pl.BlockSpec((1, ROUTER_LANES), lambda i: (0, 0)),
        ],
        out_specs=[
            pl.BlockSpec((tm, d), lambda i: (i, 0)),
            pl.BlockSpec((tm * ROW_CHUNKS, LANES), lambda i: (i, 0)),
            pl.BlockSpec((2, tm), lambda i: (0, i)),
            pl.BlockSpec((2, tm), lambda i: (0, i)),
        ],
        out_shape=[
            jax.ShapeDtypeStruct((t, d), F32),
            jax.ShapeDtypeStruct((t * ROW_CHUNKS, LANES), U32),
            jax.ShapeDtypeStruct((2, t), I32),
            jax.ShapeDtypeStruct((2, t), F32),
        ],
        scratch_shapes=[pltpu.VMEM((k, d), BF16)],
        compiler_params=_cparams(1),
        name="proj_ln_router",
    )(o, x2, mod_rows, w, b.reshape(1, d), ln_g.reshape(1, d), ln_b.reshape(1, d), w_router, b_router)


def _sort_body(ids_ref, pos_ref, te_ref, nused_ref, tv_ref, rank_ref):
    n_rows = ids_ref.shape[0]
    c = SORT_CHUNK
    erow = lax.broadcasted_iota(I32, (N_EXPERTS, c), 0)
    tri = (lax.broadcasted_iota(I32, (c, c), 0) <= lax.broadcasted_iota(I32, (c, c), 1)).astype(BF16)

    def rank_step(r, carry):
        onehot = erow == ids_ref[pl.ds(r, 1), :]
        pref = jnp.dot(onehot.astype(BF16), tri, preferred_element_type=F32)
        rank = jnp.sum(jnp.where(onehot, pref + carry, 0.0), axis=0, keepdims=True) - 1.0
        rank_ref[pl.ds(r, 1), :] = rank
        return carry + pref[:, c - 1:c]

    counts = lax.fori_loop(0, n_rows, rank_step, jnp.zeros((N_EXPERTS, 1), F32), unroll=8)
    n_tile = jnp.floor((counts + (EXPERT_TILE - 1)) * (1.0 / EXPERT_TILE))
    low = (lax.broadcasted_iota(I32, (N_EXPERTS, N_EXPERTS), 1)
           <= lax.broadcasted_iota(I32, (N_EXPERTS, N_EXPERTS), 0)).astype(BF16)
    cum = jnp.dot(low, jnp.broadcast_to(n_tile, (N_EXPERTS, LANES)).astype(BF16),
                  preferred_element_type=F32)[:, 0:1]
    row_off = (cum - n_tile) * EXPERT_TILE

    def pos_step(r, _):
        onehot = erow == ids_ref[pl.ds(r, 1), :]
        off = jnp.sum(jnp.where(onehot, row_off, 0.0), axis=0, keepdims=True)
        pos_ref[pl.ds(r, 1), :] = (off + rank_ref[pl.ds(r, 1), :]).astype(I32)
        return 0

    lax.fori_loop(0, n_rows, pos_step, 0, unroll=8)
    total = jnp.max(cum, axis=0, keepdims=True)
    n_lanes = te_ref.shape[1]
    tile = jnp.minimum(lax.broadcasted_iota(I32, (N_EXPERTS, n_lanes), 1).astype(F32), total - 1.0)
    te_ref[...] = jnp.sum(jnp.where(cum <= tile, 1.0, 0.0), axis=0, keepdims=True).astype(I32)
    nused_ref[...] = jnp.broadcast_to(total, nused_ref.shape).astype(I32)
    tile_f = lax.broadcasted_iota(I32, (N_EXPERTS, n_lanes), 1).astype(F32)
    first = cum - n_tile
    rows_left = jnp.clip(counts - (tile_f - first) * EXPERT_TILE, 0.0, float(EXPERT_TILE))
    owns = (first <= tile_f) & (tile_f < cum)
    tv_ref[...] = jnp.sum(jnp.where(owns, rows_left, 0.0), axis=0, keepdims=True).astype(I32)


def _expert_sort(ids, n_tiles):
    n_assign = ids.shape[0] * ids.shape[1]
    n_rows = n_assign // SORT_CHUNK
    te_lanes = -(-n_tiles // LANES) * LANES
    pos, te, nused, tv = pl.pallas_call(
        _sort_body,
        grid=(1,),
        in_specs=[pl.BlockSpec((n_rows, SORT_CHUNK), lambda i: (0, 0))],
        out_specs=[
            pl.BlockSpec((n_rows, SORT_CHUNK), lambda i: (0, 0)),
            pl.BlockSpec((1, te_lanes), lambda i: (0, 0)),
            pl.BlockSpec((1, LANES), lambda i: (0, 0)),
            pl.BlockSpec((1, te_lanes), lambda i: (0, 0)),
        ],
        out_shape=[
            jax.ShapeDtypeStruct((n_rows, SORT_CHUNK), I32),
            jax.ShapeDtypeStruct((1, te_lanes), I32),
            jax.ShapeDtypeStruct((1, LANES), I32),
            jax.ShapeDtypeStruct((1, te_lanes), I32),
        ],
        scratch_shapes=[pltpu.VMEM((n_rows, SORT_CHUNK), F32)],
        compiler_params=_cparams(1),
        name="expert_sort",
    )(ids.reshape(n_rows, SORT_CHUNK))
    return pos, te[0, :n_tiles], nused[0, :1], tv[0, :n_tiles]


def _sc_mesh():
    return plsc.VectorSubcoreMesh(core_axis_name="c", subcore_axis_name="s", num_cores=SC_CORES,
                                  num_subcores=SC_WORKERS // SC_CORES)


def _sc_scatter_rows(src, pos2, n_rows):
    t = pos2.shape[1]
    src3 = src.reshape(t, ROW_CHUNKS, LANES)
    per_worker = t // SC_WORKERS
    n_chunks = per_worker // SC_CHUNK
    idx = pos2.reshape(2, SC_WORKERS, n_chunks, SC_CHUNK).transpose(1, 0, 2, 3)
    idx = idx.reshape(SC_WORKERS, 2 * n_chunks, SC_CHUNK)

    @functools.partial(
        pl.kernel, mesh=_sc_mesh(),
        out_type=jax.ShapeDtypeStruct((n_rows, ROW_CHUNKS, LANES), src.dtype),
        scratch_types=[
            pltpu.VMEM((2 * n_chunks, SC_CHUNK), I32),
            pltpu.VMEM((2, SC_CHUNK, ROW_CHUNKS, LANES), src.dtype),
            pltpu.SemaphoreType.DMA((2,)),
            pltpu.SemaphoreType.DMA((2,)),
        ],
        name="sc_scatter_rows",
    )
    def scatter(src_hbm, idx_hbm, out_hbm, idx_v, rows_v, rsem, wsem):
        wid = lax.axis_index("s") * SC_CORES + lax.axis_index("c")
        pltpu.sync_copy(idx_hbm.at[wid], idx_v)

        def read(j):
            b = j % 2
            return pltpu.async_copy(src_hbm.at[pl.ds(wid * per_worker + j * SC_CHUNK, SC_CHUNK)], rows_v.at[b],
                                    rsem.at[b])

        def write(j):
            b = j % 2
            return [pltpu.async_copy(rows_v.at[b], out_hbm.at[idx_v.at[k * n_chunks + j]], wsem.at[b])
                    for k in range(2)]

        reads = {0: read(0)}
        writes = {}
        for j in range(n_chunks):
            reads.pop(j).wait()
            if j + 1 < n_chunks:
                for cp in writes.pop(j - 1, []):
                    cp.wait()
                reads[j + 1] = read(j + 1)
            writes[j] = write(j)
        for cps in writes.values():
            for cp in cps:
                cp.wait()

    return scatter(src3, idx).reshape(n_rows * ROW_CHUNKS, LANES)


def _sc_gather_rows(table, idx):
    n = idx.shape[0]
    table3 = table.reshape(-1, ROW_CHUNKS, LANES)
    per_worker = n // SC_WORKERS
    n_chunks = per_worker // SC_CHUNK

    @functools.partial(
        pl.kernel, mesh=_sc_mesh(),
        out_type=jax.ShapeDtypeStruct((n, ROW_CHUNKS, LANES), table.dtype),
        scratch_types=[
            pltpu.VMEM((n_chunks, SC_CHUNK), I32),
            pltpu.VMEM((2, SC_CHUNK, ROW_CHUNKS, LANES), table.dtype),
            pltpu.SemaphoreType.DMA((2,)),
            pltpu.SemaphoreType.DMA((2,)),
        ],
        name="sc_gather_rows",
    )
    def gather(table_hbm, idx_hbm, out_hbm, idx_v, rows_v, rsem, wsem):
        wid = lax.axis_index("s") * SC_CORES + lax.axis_index("c")
        pltpu.sync_copy(idx_hbm.at[wid], idx_v)

        def read(j):
            b = j % 2
            return pltpu.async_copy(table_hbm.at[idx_v.at[j]], rows_v.at[b], rsem.at[b])

        def write(j):
            b = j % 2
            return pltpu.async_copy(rows_v.at[b], out_hbm.at[pl.ds(wid * per_worker + j * SC_CHUNK, SC_CHUNK)],
                                    wsem.at[b])

        reads = {0: read(0)}
        writes = {}
        for j in range(n_chunks):
            reads.pop(j).wait()
            if j + 1 < n_chunks:
                if j - 1 in writes:
                    writes.pop(j - 1).wait()
                reads[j + 1] = read(j + 1)
            writes[j] = write(j)
        for cp in writes.values():
            cp.wait()

    out = gather(table3, idx.reshape(SC_WORKERS, n_chunks, SC_CHUNK))
    return out.reshape(n * ROW_CHUNKS, LANES)


def _expert_body(te_ref, nused_ref, tv_ref, xs_ref, wgu_ref, wd_ref, ys_ref, wgu_bf, wd_bf):
    i = pl.program_id(0)
    used = i < nused_ref[0]
    prev = te_ref[jnp.maximum(i - 1, 0)]

    @pl.when(used & ((i == 0) | (te_ref[i] != prev)))
    def _():
        wgu_bf[...] = wgu_ref[...].astype(BF16)
        wd_bf[...] = wd_ref[...].astype(BF16)

    @pl.when(used)
    def _():
        live = lax.broadcasted_iota(I32, (EXPERT_TILE, 1), 0) < tv_ref[i]
        xs = jnp.where(live, _load_rows(xs_ref, EXPERT_TILE), 0.0).astype(BF16)
        gu = jnp.dot(xs, wgu_bf[...], preferred_element_type=F32)
        gate = gu[:, :EXPERT_FF]
        up = gu[:, EXPERT_FF:]
        act = gate * jax.nn.sigmoid(gate) * up
        _store_rows(ys_ref, jnp.dot(act.astype(BF16), wd_bf[...], preferred_element_type=F32))

    @pl.when(jnp.logical_not(used))
    def _():
        ys_ref[...] = jnp.zeros(ys_ref.shape, ys_ref.dtype)


def _expert_mlp(tile_expert, n_used, tile_valid, xs, w_gate_up, w_down, layer):
    d = D_MODEL
    n_tiles = xs.shape[0] // (EXPERT_TILE * ROW_CHUNKS)
    f2 = 2 * EXPERT_FF
    tile_rows = EXPERT_TILE * ROW_CHUNKS
    e0 = layer * N_EXPERTS
    grid_spec = pltpu.PrefetchScalarGridSpec(
        num_scalar_prefetch=3,
        grid=(n_tiles,),
        in_specs=[
            pl.BlockSpec((tile_rows, LANES), lambda i, te, nu, tv: (jnp.minimum(i, nu[0] - 1), 0)),
            pl.BlockSpec((None, d, f2), lambda i, te, nu, tv: (e0 + te[i], 0, 0)),
            pl.BlockSpec((None, EXPERT_FF, d), lambda i, te, nu, tv: (e0 + te[i], 0, 0)),
        ],
        out_specs=pl.BlockSpec((tile_rows, LANES), lambda i, te, nu, tv: (i, 0)),
        scratch_shapes=[pltpu.VMEM((d, f2), BF16), pltpu.VMEM((EXPERT_FF, d), BF16)],
    )
    return pl.pallas_call(
        _expert_body,
        grid_spec=grid_spec,
        out_shape=jax.ShapeDtypeStruct(xs.shape, U32),
        compiler_params=_cparams(1),
        name="expert_mlp",
    )(tile_expert, n_used, tile_valid, xs, w_gate_up.reshape(DEPTH * N_EXPERTS, d, f2),
      w_down.reshape(DEPTH * N_EXPERTS, EXPERT_FF, d))


def _combine_body(wts_ref, x_ref, mod_ref, lng_ref, lnb_ref, ya_ref, yb_ref, o_ref):
    tm = x_ref.shape[0]
    d = D_MODEL
    cols = []
    for k in range(2):
        wt = jnp.broadcast_to(wts_ref[k:k + 1, :], (LANES, tm)).T
        cols.append(jnp.concatenate([wt] * (d // LANES), axis=1))
    y = cols[0] * _load_rows(ya_ref, tm) + cols[1] * _load_rows(yb_ref, tm)
    gt_f = mod_ref[:, 5 * d:6 * d]
    o_ref[...] = _layer_norm(DEEPNORM_ALPHA * x_ref[...] + (1.0 + gt_f) * y, lng_ref[...], lnb_ref[...])


def _combine_ln(pos2, wts, x1, mod_rows, layer, ln_g, ln_b, ys, seq):
    t, d = x1.shape
    tm = 512
    steps_per_batch = seq // tm
    n_steps = t // tm
    yg = _sc_gather_rows(ys, pos2.reshape(2 * t))
    return pl.pallas_call(
        _combine_body,
        grid=(n_steps,),
        in_specs=[
            pl.BlockSpec((2, tm), lambda i: (0, i)),
            pl.BlockSpec((tm, d), lambda i: (i, 0)),
            pl.BlockSpec((None, 1, N_MOD * d), lambda i: (layer * MOD_ROWS + i // steps_per_batch, 0, 0)),
            pl.BlockSpec((1, d), lambda i: (0, 0)),
            pl.BlockSpec((1, d), lambda i: (0, 0)),
            pl.BlockSpec((tm * ROW_CHUNKS, LANES), lambda i: (i, 0)),
            pl.BlockSpec((tm * ROW_CHUNKS, LANES), lambda i: (n_steps + i, 0)),
        ],
        out_specs=pl.BlockSpec((tm, d), lambda i: (i, 0)),
        out_shape=jax.ShapeDtypeStruct((t, d), F32),
        compiler_params=_cparams(1),
        name="moe_combine_ln",
    )(wts, x1, mod_rows, ln_g.reshape(1, d), ln_b.reshape(1, d), yg, yg)


def _gmlp_body(x_ref, mod_ref, w_ref, b_ref, g_ref, beta_ref, ws_ref, bs_ref, o_ref, ws_bf):
    @pl.when(pl.program_id(0) == 0)
    def _():
        tri = lax.broadcasted_iota(I32, (CHUNK, CHUNK), 0) >= lax.broadcasted_iota(I32, (CHUNK, CHUNK), 1)
        for g in range(N_SGU_GROUPS):
            ws_bf[g] = jnp.where(tri, ws_ref[g], 0.0).astype(BF16)

    tm = x_ref.shape[0]
    d = D_MODEL
    sh = mod_ref[:, 0:d]
    sc = mod_ref[:, d:2 * d]
    h = x_ref[...] * (1.0 + sc) + sh
    z = jnp.dot(h.astype(BF16), w_ref[...], preferred_element_type=F32) + b_ref[...]
    z = 0.5 * z * (1.0 + lax.erf(z * (2.0 ** -0.5)))
    u = z[:, :GMLP_WIDTH]
    v = _layer_norm(z[:, GMLP_WIDTH:], g_ref[...], beta_ref[...]).astype(BF16)
    for ci in range(tm // CHUNK):
        rows = slice(ci * CHUNK, (ci + 1) * CHUNK)
        for g in range(N_SGU_GROUPS):
            lanes = slice(g * SGU_GROUP_DIM, (g + 1) * SGU_GROUP_DIM)
            mixed = jnp.dot(ws_bf[g], v[rows, lanes], preferred_element_type=F32) + bs_ref[:, g:g + 1]
            o_ref[rows, lanes] = (u[rows, lanes] * mixed).astype(BF16)


def _gmlp_gate(x2, mod_rows, layer, w_in_bf, b_in, sgu_g, sgu_b, w_s, b_s, seq):
    t, d = x2.shape
    tm = 512
    steps_per_batch = seq // tm
    gw = GMLP_WIDTH
    return pl.pallas_call(
        _gmlp_body,
        grid=(t // tm,),
        in_specs=[
            pl.BlockSpec((tm, d), lambda i: (i, 0)),
            pl.BlockSpec((None, 1, N_MOD * d), lambda i: (layer * MOD_ROWS + i // steps_per_batch, 0, 0)),
            pl.BlockSpec((d, 2 * gw), lambda i: (0, 0)),
            pl.BlockSpec((1, 2 * gw), lambda i: (0, 0)),
            pl.BlockSpec((1, gw), lambda i: (0, 0)),
            pl.BlockSpec((1, gw), lambda i: (0, 0)),
            pl.BlockSpec((N_SGU_GROUPS, CHUNK, CHUNK), lambda i: (0, 0, 0)),
            pl.BlockSpec((CHUNK, N_SGU_GROUPS), lambda i: (0, 0)),
        ],
        out_specs=pl.BlockSpec((tm, gw), lambda i: (i, 0)),
        out_shape=jax.ShapeDtypeStruct((t, gw), BF16),
        scratch_shapes=[pltpu.VMEM((N_SGU_GROUPS, CHUNK, CHUNK), BF16)],
        compiler_params=_cparams(1),
        name="gmlp_gate",
    )(x2, mod_rows, w_in_bf, b_in.reshape(1, 2 * gw), sgu_g.reshape(1, gw), sgu_b.reshape(1, gw), w_s, b_s.T)


def _router_params(w_group, b_group, w_expert, b_expert):
    d = w_group.shape[0]
    w = jnp.zeros((d, ROUTER_LANES), F32)
    w = w.at[:, :N_EXPERT_GROUPS].set(w_group).at[:, EXPERT_COL0:EXPERT_COL0 + N_EXPERTS].set(w_expert)
    b = jnp.zeros((1, ROUTER_LANES), F32)
    b = b.at[0, :N_EXPERT_GROUPS].set(b_group).at[0, EXPERT_COL0:EXPERT_COL0 + N_EXPERTS].set(b_expert)
    return w, b


def _moe_block(x1, h2, ids, wts, mod_rows, layer, ln_g, ln_b, w_gate_up, w_down, seq):
    t = x1.shape[0]
    n_rows = 2 * t + N_EXPERTS * EXPERT_TILE
    n_tiles = n_rows // EXPERT_TILE
    pos, tile_expert, n_used, tile_valid = _expert_sort(ids, n_tiles)
    pos2 = pos.reshape(2, t)
    xs = _sc_scatter_rows(h2, pos2, n_rows)
    ys = _expert_mlp(tile_expert, n_used, tile_valid, xs, w_gate_up, w_down, layer)
    return _combine_ln(pos2, wts, x1, mod_rows, layer, ln_g, ln_b, ys, seq)


def kernel(x, c, positions, ada_w, ada_b, post_ln_g, post_ln_b, attn_w_qkv, attn_b_qkv, attn_sinks, attn_w_o, attn_b_o, gmlp_w_in, gmlp_b_in, gmlp_sgu_ln_g, gmlp_sgu_ln_b, gmlp_w_s, gmlp_b_s, gmlp_w_out, gmlp_b_out, moe_w_group_router, moe_b_group_router, moe_w_expert_router, moe_b_expert_router, moe_w_gate_up, moe_w_down):
    batch, seq, d = x.shape
    t = batch * seq
    x2 = x.reshape(t, d)
    c_pad = jnp.zeros((MOD_ROWS, d), F32).at[:batch].set(c)
    mod_rows = _adaln_mod(c_pad, ada_w, ada_b).reshape(DEPTH * MOD_ROWS, 1, N_MOD * d)

    for layer in range(DEPTH):
        j = layer // 2
        if layer % 2 == 0:
            qkv = _qkv_rope(x2, mod_rows, layer, positions, attn_w_qkv[j], attn_b_qkv[j], seq)
            mix = _attention(qkv, attn_sinks[j], batch, seq)
            w_out, b_out = attn_w_o[j], attn_b_o[j]
        else:
            mix = _gmlp_gate(x2, mod_rows, layer, gmlp_w_in[j].astype(BF16), gmlp_b_in[j], gmlp_sgu_ln_g[j],
                             gmlp_sgu_ln_b[j], gmlp_w_s[j], gmlp_b_s[j], seq)
            w_out, b_out = gmlp_w_out[j], gmlp_b_out[j]
        w_router, b_router = _router_params(moe_w_group_router[layer], moe_b_group_router[layer],
                                            moe_w_expert_router[layer], moe_b_expert_router[layer])
        x1, h2, ids, wts = _proj_ln_router(mix, x2, mod_rows, layer, w_out, b_out, post_ln_g[layer, 0],
                                           post_ln_b[layer, 0], w_router, b_router, seq)
        x2 = _moe_block(x1, h2, ids, wts, mod_rows, layer, post_ln_g[layer, 1], post_ln_b[layer, 1],
                        moe_w_gate_up, moe_w_down, seq)
    return x2.reshape(batch, seq, d)
```

```python
import functools

import jax
import jax.numpy as jnp
from jax import lax
from jax.experimental import pallas as pl
from jax.experimental.pallas import tpu as pltpu
from jax.experimental.pallas import tpu_sc as plsc

F32 = jnp.float32
BF16 = jnp.bfloat16
I32 = jnp.int32

D_MODEL = 1024
DEPTH = 2
HEAD_DIM = 64
N_Q_HEADS = 16
N_KV_HEADS = 4
GQA_GROUP = N_Q_HEADS // N_KV_HEADS
WINDOW = 128
ROPE_THETA = 10000.0
Q_WIDTH = N_Q_HEADS * HEAD_DIM
KV_WIDTH = N_KV_HEADS * HEAD_DIM
QKV_WIDTH = Q_WIDTH + 2 * KV_WIDTH
CHUNK = 128
GMLP_WIDTH = 2 * D_MODEL
N_SGU_GROUPS = 8
SGU_GROUP_DIM = GMLP_WIDTH // N_SGU_GROUPS
N_EXPERT_GROUPS = 4
EXPERTS_PER_GROUP = 8
N_EXPERTS = N_EXPERT_GROUPS * EXPERTS_PER_GROUP
EXPERT_FF = D_MODEL // 4
N_MOD = 6
DEEPNORM_ALPHA = (2.0 * DEPTH) ** 0.25
LN_EPS = 1e-5

LANES = 128
MOD_ROWS = 8
ROUTER_LANES = 128
EXPERT_COL0 = 8
SORT_CHUNK = 256
EXPERT_TILE = 512
SUB_ROWS = 256
VMEM_LIMIT = 56 * 1024 * 1024
SC_CORES = 2
SC_WORKERS = 32
SC_CHUNK = 64


def _cparams(n_axes, vmem=VMEM_LIMIT):
    return pltpu.CompilerParams(dimension_semantics=("arbitrary",) * n_axes, vmem_limit_bytes=vmem)


U32 = jnp.uint32
ROW_CHUNKS = D_MODEL // 2 // LANES


def _store_rows(ref, val, token0=0):
    n = val.shape[0]
    half = D_MODEL // 2
    lo = lax.bitcast_convert_type(val[:, :half].astype(BF16).astype(F32), U32)
    hi = lax.bitcast_convert_type(val[:, half:].astype(BF16).astype(F32), U32)
    words = (lo >> 16) | hi
    for c in range(ROW_CHUNKS):
        ref[pl.ds(token0 * ROW_CHUNKS + c, n, stride=ROW_CHUNKS), :] = words[:, c * LANES:(c + 1) * LANES]


def _load_rows(ref, n):
    words = jnp.concatenate([ref[pl.ds(c, n, stride=ROW_CHUNKS), :] for c in range(ROW_CHUNKS)], axis=1)
    lo = lax.bitcast_convert_type(words << 16, F32)
    hi = lax.bitcast_convert_type(words & jnp.uint32(0xFFFF0000), F32)
    return jnp.concatenate([lo, hi], axis=1)


def _layer_norm(r, g, b):
    mu = jnp.mean(r, axis=-1, keepdims=True)
    d = r - mu
    var = jnp.mean(d * d, axis=-1, keepdims=True)
    return d * lax.rsqrt(var + LN_EPS) * g + b


def _mod_body(c_ref, w_ref, b_ref, o_ref):
    c = c_ref[...]
    ca = c * jax.nn.sigmoid(c)
    o_ref[...] = jnp.dot(ca.astype(BF16), w_ref[...].astype(BF16), preferred_element_type=F32) + b_ref[...]


def _adaln_mod(c_pad, ada_w, ada_b):
    tn = 1536
    n_out = N_MOD * D_MODEL
    return pl.pallas_call(
        _mod_body,
        grid=(DEPTH, n_out // tn),
        in_specs=[
            pl.BlockSpec((MOD_ROWS, D_MODEL), lambda l, j: (0, 0)),
            pl.BlockSpec((None, D_MODEL, tn), lambda l, j: (l, 0, j)),
            pl.BlockSpec((None, 1, tn), lambda l, j: (l, 0, j)),
        ],
        out_specs=pl.BlockSpec((None, MOD_ROWS, tn), lambda l, j: (l, 0, j)),
        out_shape=jax.ShapeDtypeStruct((DEPTH, MOD_ROWS, n_out), F32),
        compiler_params=_cparams(2),
        name="adaln_mod",
    )(c_pad, ada_w, ada_b.reshape(DEPTH, 1, n_out))


def _qkv_body(x_ref, mod_ref, pos_ref, w_ref, b_ref, invf_ref, o_ref, wbf_ref):
    @pl.when(pl.program_id(0) == 0)
    def _():
        wbf_ref[...] = w_ref[...].astype(BF16)

    tm = x_ref.shape[0]
    sh = mod_ref[:, 0:D_MODEL]
    sc = mod_ref[:, D_MODEL:2 * D_MODEL]
    h = x_ref[...] * (1.0 + sc) + sh
    qkv = jnp.dot(h.astype(BF16), wbf_ref[...], preferred_element_type=F32) + b_ref[...]

    ang = invf_ref[...] * pos_ref[...].astype(F32)
    c = jnp.cos(ang)
    s = jnp.sin(ang)
    ct = jnp.concatenate([c, c, c, c], axis=0).T
    st = jnp.concatenate([-s, s, -s, s], axis=0).T
    lane = lax.broadcasted_iota(I32, (tm, LANES), 1)
    first_half = (lane & (HEAD_DIM // 2)) == 0
    n_rope = (Q_WIDTH + KV_WIDTH) // LANES
    for j in range(n_rope):
        blk = qkv[:, j * LANES:(j + 1) * LANES]
        rot = jnp.where(first_half, pltpu.roll(blk, LANES - HEAD_DIM // 2, 1), pltpu.roll(blk, HEAD_DIM // 2, 1))
        r = blk * ct + rot * st
        if j < Q_WIDTH // LANES:
            r = r * (HEAD_DIM ** -0.5)
        o_ref[:, j * LANES:(j + 1) * LANES] = r.astype(BF16)
    o_ref[:, Q_WIDTH + KV_WIDTH:] = qkv[:, Q_WIDTH + KV_WIDTH:].astype(BF16)


def _qkv_rope(x2, mod_rows, layer, positions, w_qkv, b_qkv, seq):
    t = x2.shape[0]
    tm = 512
    steps_per_batch = seq // tm
    inv_freq = ROPE_THETA ** (-jnp.arange(0, HEAD_DIM, 2, dtype=F32) / HEAD_DIM)
    return pl.pallas_call(
        _qkv_body,
        grid=(t // tm,),
        in_specs=[
            pl.BlockSpec((tm, D_MODEL), lambda i: (i, 0)),
            pl.BlockSpec((None, 1, N_MOD * D_MODEL), lambda i: (layer * MOD_ROWS + i // steps_per_batch, 0, 0)),
            pl.BlockSpec((None, 1, tm), lambda i: (i, 0, 0)),
            pl.BlockSpec((D_MODEL, QKV_WIDTH), lambda i: (0, 0)),
            pl.BlockSpec((1, QKV_WIDTH), lambda i: (0, 0)),
            pl.BlockSpec((HEAD_DIM // 2, 1), lambda i: (0, 0)),
        ],
        out_specs=pl.BlockSpec((tm, QKV_WIDTH), lambda i: (i, 0)),
        out_shape=jax.ShapeDtypeStruct((t, QKV_WIDTH), BF16),
        scratch_shapes=[pltpu.VMEM((D_MODEL, QKV_WIDTH), BF16)],
        compiler_params=_cparams(1),
        name="qkv_rope",
    )(x2, mod_rows, positions.reshape(t // tm, 1, tm), w_qkv, b_qkv.reshape(1, QKV_WIDTH),
      inv_freq.reshape(HEAD_DIM // 2, 1))


BF16_ROWS = 16


def _attn_prepare(kv, kab_ref, vx_ref, slot):
    kv = kv.astype(F32)
    low = lax.broadcasted_iota(I32, (WINDOW, LANES), 1) < HEAD_DIM
    ones = jnp.ones((WINDOW, LANES), F32)
    for g in range(N_KV_HEADS):
        for part, ref in ((0, None), (KV_WIDTH, vx_ref)):
            tile = kv[:, part + (g // 2) * LANES:part + (g // 2 + 1) * LANES]
            other = pltpu.roll(tile, HEAD_DIM, 1)
            in_low, in_high = (tile, other) if g % 2 == 0 else (other, tile)
            if ref is None:
                kab_ref[slot, 2 * g] = jnp.where(low, in_low, 0.0).astype(BF16)
                kab_ref[slot, 2 * g + 1] = jnp.where(low, 0.0, in_high).astype(BF16)
            else:
                both = jnp.where(low, in_low, in_high)
                vx_ref[slot, g] = jnp.concatenate([both, ones], axis=1).astype(BF16)


def _attn_block(sink_ref, q, kab_ref, vx_ref, s_ref, p_ref, prev, cur, first_block):
    for g in range(N_KV_HEADS):
        q_pair = jnp.concatenate([q[:, (2 * g) * LANES:(2 * g + 1) * LANES],
                                  q[:, (2 * g + 1) * LANES:(2 * g + 2) * LANES]], axis=0)
        for a in range(2):
            kband = jnp.concatenate([kab_ref[prev, 2 * g + a], kab_ref[cur, 2 * g + a]], axis=0)
            s = lax.dot_general(q_pair, kband, (((1,), (1,)), ((), ())), preferred_element_type=F32)
            s_ref[GQA_GROUP * g + a] = s[:WINDOW]
            s_ref[GQA_GROUP * g + 2 + a] = s[WINDOW:]
    qi = lax.broadcasted_iota(I32, (WINDOW, 2 * WINDOW), 0) + WINDOW
    kj = lax.broadcasted_iota(I32, (WINDOW, 2 * WINDOW), 1)
    mask = (kj <= qi) & (kj > qi - WINDOW) & ((kj >= WINDOW) | jnp.logical_not(first_block))
    key0 = lax.broadcasted_iota(I32, (1, 2 * WINDOW), 1) == 0
    for h in range(N_Q_HEADS):
        s = jnp.where(mask, s_ref[h], jnp.where(key0, sink_ref[h], -jnp.inf))
        m = jnp.max(s, axis=-1, keepdims=True)
        p_ref[h] = jnp.exp(s - m).astype(BF16)
    low = lax.broadcasted_iota(I32, (WINDOW, LANES), 1) < HEAD_DIM
    sink_row = ((lax.broadcasted_iota(I32, (BF16_ROWS, 2 * LANES), 0) == 0)
                & (lax.broadcasted_iota(I32, (BF16_ROWS, 2 * LANES), 1) < LANES))
    out_tiles = []
    for g in range(N_KV_HEADS):
        v_prev = vx_ref[prev, g]
        v_head = jnp.where(sink_row, 0.0, v_prev[:BF16_ROWS].astype(F32)).astype(BF16)
        vband = jnp.concatenate([v_head, v_prev[BF16_ROWS:], vx_ref[cur, g]], axis=0)
        p4 = p_ref[GQA_GROUP * g:GQA_GROUP * (g + 1)].reshape(GQA_GROUP * WINDOW, 2 * WINDOW)
        o4 = jnp.dot(p4, vband, preferred_element_type=F32)
        heads = []
        for j in range(GQA_GROUP):
            blk = o4[j * WINDOW:(j + 1) * WINDOW]
            heads.append(blk[:, :LANES] / blk[:, LANES:])
        out_tiles.append(jnp.where(low, heads[0], heads[1]))
        out_tiles.append(jnp.where(low, heads[2], heads[3]))
    return jnp.concatenate(out_tiles, axis=1).astype(BF16)


def _attn_body(sink_ref, q_ref, kv_ref, o_ref, kab_ref, vx_ref, s_ref, p_ref):
    n = pl.program_id(1)

    @pl.when(n == 0)
    def _():
        kab_ref[1] = jnp.zeros(kab_ref.shape[1:], kab_ref.dtype)
        half = (N_KV_HEADS, WINDOW, LANES)
        vx_ref[1] = jnp.concatenate([jnp.zeros(half, BF16), jnp.ones(half, BF16)], axis=-1)

    scratch = (kab_ref, vx_ref, s_ref, p_ref)
    _attn_prepare(kv_ref[0:WINDOW, :], kab_ref, vx_ref, 0)
    o_ref[0:WINDOW, :] = _attn_block(sink_ref, q_ref[0:WINDOW, :], *scratch, 1, 0, n == 0)
    _attn_prepare(kv_ref[WINDOW:2 * WINDOW, :], kab_ref, vx_ref, 1)
    o_ref[WINDOW:2 * WINDOW, :] = _attn_block(sink_ref, q_ref[WINDOW:2 * WINDOW, :], *scratch, 0, 1, False)


def _attention(qkv, sinks, batch, seq):
    t = qkv.shape[0]
    tq = 2 * WINDOW
    steps = seq // tq
    kv_col = Q_WIDTH // (2 * KV_WIDTH)
    return pl.pallas_call(
        _attn_body,
        grid=(batch, steps),
        in_specs=[
            pl.BlockSpec(memory_space=pltpu.SMEM),
            pl.BlockSpec((tq, Q_WIDTH), lambda b, n: (b * steps + n, 0)),
            pl.BlockSpec((tq, 2 * KV_WIDTH), lambda b, n: (b * steps + n, kv_col)),
        ],
        out_specs=pl.BlockSpec((tq, Q_WIDTH), lambda b, n: (b * steps + n, 0)),
        out_shape=jax.ShapeDtypeStruct((t, Q_WIDTH), BF16),
        scratch_shapes=[
            pltpu.VMEM((2, 2 * N_KV_HEADS, WINDOW, LANES), BF16),
            pltpu.VMEM((2, N_KV_HEADS, WINDOW, 2 * LANES), BF16),
            pltpu.VMEM((N_Q_HEADS, WINDOW, 2 * WINDOW), F32),
            pltpu.VMEM((N_Q_HEADS, WINDOW, 2 * WINDOW), BF16),
        ],
        compiler_params=_cparams(2),
        name="swa_attention",
    )(sinks, qkv, qkv)


def _route(lt):
    tm = lt.shape[1]
    row = lax.broadcasted_iota(I32, (EXPERTS_PER_GROUP, tm), 0)
    neg = -jnp.inf
    gl = jnp.where(row < N_EXPERT_GROUPS, lt[0:EXPERTS_PER_GROUP], neg)
    gm = jnp.max(gl, axis=0, keepdims=True)
    g_p = 1.0 / jnp.sum(jnp.exp(gl - gm), axis=0, keepdims=True)
    g_idx = jnp.min(jnp.where(gl == gm, row, EXPERTS_PER_GROUP), axis=0, keepdims=True)
    sel = lt[EXPERT_COL0 + (N_EXPERT_GROUPS - 1) * EXPERTS_PER_GROUP:EXPERT_COL0 + N_EXPERTS]
    for g in range(N_EXPERT_GROUPS - 2, -1, -1):
        lo = EXPERT_COL0 + g * EXPERTS_PER_GROUP
        sel = jnp.where(g_idx == g, lt[lo:lo + EXPERTS_PER_GROUP], sel)
    v1 = jnp.max(sel, axis=0, keepdims=True)
    i1 = jnp.min(jnp.where(sel == v1, row, EXPERTS_PER_GROUP), axis=0, keepdims=True)
    sel2 = jnp.where(row == i1, neg, sel)
    v2 = jnp.max(sel2, axis=0, keepdims=True)
    i2 = jnp.min(jnp.where(sel2 == v2, row, EXPERTS_PER_GROUP), axis=0, keepdims=True)
    e2 = jnp.exp(v2 - v1)
    w1 = g_p / (1.0 + e2)
    w2 = g_p * e2 / (1.0 + e2)
    base = g_idx * EXPERTS_PER_GROUP
    return base + i1, base + i2, w1, w2


def _proj_body(o_ref, x_ref, mod_ref, w_ref, b_ref, lng_ref, lnb_ref, wr_ref, br_ref,
               x1_ref, h2_ref, ids_ref, wts_ref, wbf_ref):
    @pl.when(pl.program_id(0) == 0)
    def _():
        wbf_ref[...] = w_ref[...].astype(BF16)

    d = D_MODEL
    gt_m = mod_ref[:, 2 * d:3 * d]
    sh_f = mod_ref[:, 3 * d:4 * d]
    sc_f = mod_ref[:, 4 * d:5 * d]
    wr = wr_ref[...].astype(BF16)
    for r in range(x_ref.shape[0] // SUB_ROWS):
        rows = slice(r * SUB_ROWS, (r + 1) * SUB_ROWS)
        y = jnp.dot(o_ref[rows, :], wbf_ref[...], preferred_element_type=F32) + b_ref[...]
        x1 = _layer_norm(DEEPNORM_ALPHA * x_ref[rows, :] + (1.0 + gt_m) * y, lng_ref[...], lnb_ref[...])
        x1_ref[rows, :] = x1
        h2 = x1 * (1.0 + sc_f) + sh_f
        _store_rows(h2_ref, h2, r * SUB_ROWS)
        logits = jnp.dot(h2.astype(BF16), wr, preferred_element_type=F32) + br_ref[...]
        ea, eb, wa, wb = _route(logits.T)
        ids_ref[0:1, rows] = ea
        ids_ref[1:2, rows] = eb
        wts_ref[0:1, rows] = wa
        wts_ref[1:2, rows] = wb


def _proj_ln_router(o, x2, mod_rows, layer, w, b, ln_g, ln_b, w_router, b_router, seq):
    t, k = o.shape
    tm = 512
    steps_per_batch = seq // tm
    d = D_MODEL
    return pl.pallas_call(
        _proj_body,
        grid=(t // tm,),
        in_specs=[
            pl.BlockSpec((tm, k), lambda i: (i, 0)),
            pl.BlockSpec((tm, d), lambda i: (i, 0)),
            pl.BlockSpec((None, 1, N_MOD * d), lambda i: (layer * MOD_ROWS + i // steps_per_batch, 0, 0)),
            pl.BlockSpec((k, d), lambda i: (0, 0)),
            pl.BlockSpec((1, d), lambda i: (0, 0)),
            pl.BlockSpec((1, d), lambda i: (0, 0)),
            pl.BlockSpec((1, d), lambda i: (0, 0)),
            pl.BlockSpec((d, ROUTER_LANES), lambda i: (0, 0)),
            pl.BlockSpec((1, ROUTER_LANES), lambda i: (0, 0)),
        ],
        out_specs=[
            pl.BlockSpec((tm, d), lambda i: (i, 0)),
            pl.BlockSpec((tm * ROW_CHUNKS, LANES), lambda i: (i, 0)),
            pl.BlockSpec((2, tm), lambda i: (0, i)),
            pl.BlockSpec((2, tm), lambda i: (0, i)),
        ],
        out_shape=[
            jax.ShapeDtypeStruct((t, d), F32),
            jax.ShapeDtypeStruct((t * ROW_CHUNKS, LANES), U32),
            jax.ShapeDtypeStruct((2, t), I32),
            jax.ShapeDtypeStruct((2, t), F32),
        ],
        scratch_shapes=[pltpu.VMEM((k, d), BF16)],
        compiler_params=_cparams(1),
        name="proj_ln_router",
    )(o, x2, mod_rows, w, b.reshape(1, d), ln_g.reshape(1, d), ln_b.reshape(1, d), w_router, b_router)


def _sort_body(ids_ref, pos_ref, te_ref, nused_ref, tv_ref, rank_ref):
    n_rows = ids_ref.shape[0]
    c = SORT_CHUNK
    erow = lax.broadcasted_iota(I32, (N_EXPERTS, c), 0)
    tri = (lax.broadcasted_iota(I32, (c, c), 0) <= lax.broadcasted_iota(I32, (c, c), 1)).astype(BF16)

    def rank_step(r, carry):
        onehot = erow == ids_ref[pl.ds(r, 1), :]
        pref = jnp.dot(onehot.astype(BF16), tri, preferred_element_type=F32)
        rank = jnp.sum(jnp.where(onehot, pref + carry, 0.0), axis=0, keepdims=True) - 1.0
        rank_ref[pl.ds(r, 1), :] = rank
        return carry + pref[:, c - 1:c]

    counts = lax.fori_loop(0, n_rows, rank_step, jnp.zeros((N_EXPERTS, 1), F32), unroll=8)
    n_tile = jnp.floor((counts + (EXPERT_TILE - 1)) * (1.0 / EXPERT_TILE))
    low = (lax.broadcasted_iota(I32, (N_EXPERTS, N_EXPERTS), 1)
           <= lax.broadcasted_iota(I32, (N_EXPERTS, N_EXPERTS), 0)).astype(BF16)
    cum = jnp.dot(low, jnp.broadcast_to(n_tile, (N_EXPERTS, LANES)).astype(BF16),
                  preferred_element_type=F32)[:, 0:1]
    row_off = (cum - n_tile) * EXPERT_TILE

    def pos_step(r, _):
        onehot = erow == ids_ref[pl.ds(r, 1), :]
        off = jnp.sum(jnp.where(onehot, row_off, 0.0), axis=0, keepdims=True)
        pos_ref[pl.ds(r, 1), :] = (off + rank_ref[pl.ds(r, 1), :]).astype(I32)
        return 0

    lax.fori_loop(0, n_rows, pos_step, 0, unroll=8)
    total = jnp.max(cum, axis=0, keepdims=True)
    n_lanes = te_ref.shape[1]
    tile = jnp.minimum(lax.broadcasted_iota(I32, (N_EXPERTS, n_lanes), 1).astype(F32), total - 1.0)
    te_ref[...] = jnp.sum(jnp.where(cum <= tile, 1.0, 0.0), axis=0, keepdims=True).astype(I32)
    nused_ref[...] = jnp.broadcast_to(total, nused_ref.shape).astype(I32)
    tile_f = lax.broadcasted_iota(I32, (N_EXPERTS, n_lanes), 1).astype(F32)
    first = cum - n_tile
    rows_left = jnp.clip(counts - (tile_f - first) * EXPERT_TILE, 0.0, float(EXPERT_TILE))
    owns = (first <= tile_f) & (tile_f < cum)
    tv_ref[...] = jnp.sum(jnp.where(owns, rows_left, 0.0), axis=0, keepdims=True).astype(I32)


def _expert_sort(ids, n_tiles):
    n_assign = ids.shape[0] * ids.shape[1]
    n_rows = n_assign // SORT_CHUNK
    te_lanes = -(-n_tiles // LANES) * LANES
    pos, te, nused, tv = pl.pallas_call(
        _sort_body,
        grid=(1,),
        in_specs=[pl.BlockSpec((n_rows, SORT_CHUNK), lambda i: (0, 0))],
        out_specs=[
            pl.BlockSpec((n_rows, SORT_CHUNK), lambda i: (0, 0)),
            pl.BlockSpec((1, te_lanes), lambda i: (0, 0)),
            pl.BlockSpec((1, LANES), lambda i: (0, 0)),
            pl.BlockSpec((1, te_lanes), lambda i: (0, 0)),
        ],
        out_shape=[
            jax.ShapeDtypeStruct((n_rows, SORT_CHUNK), I32),
            jax.ShapeDtypeStruct((1, te_lanes), I32),
            jax.ShapeDtypeStruct((1, LANES), I32),
            jax.ShapeDtypeStruct((1, te_lanes), I32),
        ],
        scratch_shapes=[pltpu.VMEM((n_rows, SORT_CHUNK), F32)],
        compiler_params=_cparams(1),
        name="expert_sort",
    )(ids.reshape(n_rows, SORT_CHUNK))
    return pos, te[0, :n_tiles], nused[0, :1], tv[0, :n_tiles]


def _sc_mesh():
    return plsc.VectorSubcoreMesh(core_axis_name="c", subcore_axis_name="s", num_cores=SC_CORES,
                                  num_subcores=SC_WORKERS // SC_CORES)


def _sc_scatter_rows(src, pos2, n_rows):
    t = pos2.shape[1]
    src3 = src.reshape(t, ROW_CHUNKS, LANES)
    per_worker = t // SC_WORKERS
    n_chunks = per_worker // SC_CHUNK
    idx = pos2.reshape(2, SC_WORKERS, n_chunks, SC_CHUNK).transpose(1, 0, 2, 3)
    idx = idx.reshape(SC_WORKERS, 2 * n_chunks, SC_CHUNK)

    @functools.partial(
        pl.kernel, mesh=_sc_mesh(),
        out_type=jax.ShapeDtypeStruct((n_rows, ROW_CHUNKS, LANES), src.dtype),
        scratch_types=[
            pltpu.VMEM((2 * n_chunks, SC_CHUNK), I32),
            pltpu.VMEM((2, SC_CHUNK, ROW_CHUNKS, LANES), src.dtype),
            pltpu.SemaphoreType.DMA((2,)),
            pltpu.SemaphoreType.DMA((2,)),
        ],
        name="sc_scatter_rows",
    )
    def scatter(src_hbm, idx_hbm, out_hbm, idx_v, rows_v, rsem, wsem):
        wid = lax.axis_index("s") * SC_CORES + lax.axis_index("c")
        pltpu.sync_copy(idx_hbm.at[wid], idx_v)

        def read(j):
            b = j % 2
            return pltpu.async_copy(src_hbm.at[pl.ds(wid * per_worker + j * SC_CHUNK, SC_CHUNK)], rows_v.at[b],
                                    rsem.at[b])

        def write(j):
            b = j % 2
            return [pltpu.async_copy(rows_v.at[b], out_hbm.at[idx_v.at[k * n_chunks + j]], wsem.at[b])
                    for k in range(2)]

        reads = {0: read(0)}
        writes = {}
        for j in range(n_chunks):
            reads.pop(j).wait()
            if j + 1 < n_chunks:
                for cp in writes.pop(j - 1, []):
                    cp.wait()
                reads[j + 1] = read(j + 1)
            writes[j] = write(j)
        for cps in writes.values():
            for cp in cps:
                cp.wait()

    return scatter(src3, idx).reshape(n_rows * ROW_CHUNKS, LANES)


def _sc_gather_rows(table, idx):
    n = idx.shape[0]
    table3 = table.reshape(-1, ROW_CHUNKS, LANES)
    per_worker = n // SC_WORKERS
    n_chunks = per_worker // SC_CHUNK

    @functools.partial(
        pl.kernel, mesh=_sc_mesh(),
        out_type=jax.ShapeDtypeStruct((n, ROW_CHUNKS, LANES), table.dtype),
        scratch_types=[
            pltpu.VMEM((n_chunks, SC_CHUNK), I32),
            pltpu.VMEM((2, SC_CHUNK, ROW_CHUNKS, LANES), table.dtype),
            pltpu.SemaphoreType.DMA((2,)),
            pltpu.SemaphoreType.DMA((2,)),
        ],
        name="sc_gather_rows",
    )
    def gather(table_hbm, idx_hbm, out_hbm, idx_v, rows_v, rsem, wsem):
        wid = lax.axis_index("s") * SC_CORES + lax.axis_index("c")
        pltpu.sync_copy(idx_hbm.at[wid], idx_v)

        def read(j):
            b = j % 2
            return pltpu.async_copy(table_hbm.at[idx_v.at[j]], rows_v.at[b], rsem.at[b])

        def write(j):
            b = j % 2
            return pltpu.async_copy(rows_v.at[b], out_hbm.at[pl.ds(wid * per_worker + j * SC_CHUNK, SC_CHUNK)],
                                    wsem.at[b])

        reads = {0: read(0)}
        writes = {}
        for j in range(n_chunks):
            reads.pop(j).wait()
            if j + 1 < n_chunks:
                if j - 1 in writes:
                    writes.pop(j - 1).wait()
                reads[j + 1] = read(j + 1)
            writes[j] = write(j)
        for cp in writes.values():
            cp.wait()

    out = gather(table3, idx.reshape(SC_WORKERS, n_chunks, SC_CHUNK))
    return out.reshape(n * ROW_CHUNKS, LANES)


def _expert_body(te_ref, nused_ref, tv_ref, xs_ref, wgu_ref, wd_ref, ys_ref, wgu_bf, wd_bf):
    i = pl.program_id(0)
    used = i < nused_ref[0]
    prev = te_ref[jnp.maximum(i - 1, 0)]

    @pl.when(used & ((i == 0) | (te_ref[i] != prev)))
    def _():
        wgu_bf[...] = wgu_ref[...].astype(BF16)
        wd_bf[...] = wd_ref[...].astype(BF16)

    @pl.when(used)
    def _():
        live = lax.broadcasted_iota(I32, (EXPERT_TILE, 1), 0) < tv_ref[i]
        xs = jnp.where(live, _load_rows(xs_ref, EXPERT_TILE), 0.0).astype(BF16)
        gu = jnp.dot(xs, wgu_bf[...], preferred_element_type=F32)
        gate = gu[:, :EXPERT_FF]
        up = gu[:, EXPERT_FF:]
        act = gate * jax.nn.sigmoid(gate) * up
        _store_rows(ys_ref, jnp.dot(act.astype(BF16), wd_bf[...], preferred_element_type=F32))

    @pl.when(jnp.logical_not(used))
    def _():
        ys_ref[...] = jnp.zeros(ys_ref.shape, ys_ref.dtype)


def _expert_mlp(tile_expert, n_used, tile_valid, xs, w_gate_up, w_down, layer):
    d = D_MODEL
    n_tiles = xs.shape[0] // (EXPERT_TILE * ROW_CHUNKS)
    f2 = 2 * EXPERT_FF
    tile_rows = EXPERT_TILE * ROW_CHUNKS
    e0 = layer * N_EXPERTS
    grid_spec = pltpu.PrefetchScalarGridSpec(
        num_scalar_prefetch=3,
        grid=(n_tiles,),
        in_specs=[
            pl.BlockSpec((tile_rows, LANES), lambda i, te, nu, tv: (jnp.minimum(i, nu[0] - 1), 0)),
            pl.BlockSpec((None, d, f2), lambda i, te, nu, tv: (e0 + te[i], 0, 0)),
            pl.BlockSpec((None, EXPERT_FF, d), lambda i, te, nu, tv: (e0 + te[i], 0, 0)),
        ],
        out_specs=pl.BlockSpec((tile_rows, LANES), lambda i, te, nu, tv: (i, 0)),
        scratch_shapes=[pltpu.VMEM((d, f2), BF16), pltpu.VMEM((EXPERT_FF, d), BF16)],
    )
    return pl.pallas_call(
        _expert_body,
        grid_spec=grid_spec,
        out_shape=jax.ShapeDtypeStruct(xs.shape, U32),
        compiler_params=_cparams(1),
        name="expert_mlp",
    )(tile_expert, n_used, tile_valid, xs, w_gate_up.reshape(DEPTH * N_EXPERTS, d, f2),
      w_down.reshape(DEPTH * N_EXPERTS, EXPERT_FF, d))


def _moe_combine(wts_ref, x_ref, mod_ref, lng_ref, lnb_ref, ya_ref, yb_ref):
    tm = x_ref.shape[0]
    d = D_MODEL
    cols = []
    for k in range(2):
        wt = jnp.broadcast_to(wts_ref[k:k + 1, :], (LANES, tm)).T
        cols.append(jnp.concatenate([wt] * (d // LANES), axis=1))
    y = cols[0] * _load_rows(ya_ref, tm) + cols[1] * _load_rows(yb_ref, tm)
    gt_f = mod_ref[:, 5 * d:6 * d]
    return _layer_norm(DEEPNORM_ALPHA * x_ref[...] + (1.0 + gt_f) * y, lng_ref[...], lnb_ref[...])


def _combine_specs(tm, n_steps, steps_per_batch, layer):
    d = D_MODEL
    return [
        pl.BlockSpec((2, tm), lambda i: (0, i)),
        pl.BlockSpec((tm, d), lambda i: (i, 0)),
        pl.BlockSpec((None, 1, N_MOD * d), lambda i: (layer * MOD_ROWS + i // steps_per_batch, 0, 0)),
        pl.BlockSpec((1, d), lambda i: (0, 0)),
        pl.BlockSpec((1, d), lambda i: (0, 0)),
        pl.BlockSpec((tm * ROW_CHUNKS, LANES), lambda i: (i, 0)),
        pl.BlockSpec((tm * ROW_CHUNKS, LANES), lambda i: (n_steps + i, 0)),
    ]


def _combine_body(wts_ref, x_ref, mod_ref, lng_ref, lnb_ref, ya_ref, yb_ref, o_ref):
    o_ref[...] = _moe_combine(wts_ref, x_ref, mod_ref, lng_ref, lnb_ref, ya_ref, yb_ref)


def _combine_ln(moe, mod_rows, layer, ln_g, ln_b, seq):
    wts, x1, yg = moe
    t, d = x1.shape
    tm = 512
    n_steps = t // tm
    return pl.pallas_call(
        _combine_body,
        grid=(n_steps,),
        in_specs=_combine_specs(tm, n_steps, seq // tm, layer),
        out_specs=pl.BlockSpec((tm, d), lambda i: (i, 0)),
        out_shape=jax.ShapeDtypeStruct((t, d), F32),
        compiler_params=_cparams(1),
        name="moe_combine_ln",
    )(wts, x1, mod_rows, ln_g.reshape(1, d), ln_b.reshape(1, d), yg, yg)


def _gmlp_body(wts_ref, x1_ref, modp_ref, lng_ref, lnb_ref, ya_ref, yb_ref,
               mod_ref, w_ref, b_ref, g_ref, beta_ref, ws_ref, bs_ref, x2_ref, o_ref, ws_bf):
    @pl.when(pl.program_id(0) == 0)
    def _():
        tri = lax.broadcasted_iota(I32, (CHUNK, CHUNK), 0) >= lax.broadcasted_iota(I32, (CHUNK, CHUNK), 1)
        for g in range(N_SGU_GROUPS):
            ws_bf[g] = jnp.where(tri, ws_ref[g], 0.0).astype(BF16)

    tm = x1_ref.shape[0]
    d = D_MODEL
    x2 = _moe_combine(wts_ref, x1_ref, modp_ref, lng_ref, lnb_ref, ya_ref, yb_ref)
    x2_ref[...] = x2
    sh = mod_ref[:, 0:d]
    sc = mod_ref[:, d:2 * d]
    h = x2 * (1.0 + sc) + sh
    z = jnp.dot(h.astype(BF16), w_ref[...], preferred_element_type=F32) + b_ref[...]
    z = 0.5 * z * (1.0 + lax.erf(z * (2.0 ** -0.5)))
    u = z[:, :GMLP_WIDTH]
    v = _layer_norm(z[:, GMLP_WIDTH:], g_ref[...], beta_ref[...]).astype(BF16)
    for ci in range(tm // CHUNK):
        rows = slice(ci * CHUNK, (ci + 1) * CHUNK)
        for g in range(N_SGU_GROUPS):
            lanes = slice(g * SGU_GROUP_DIM, (g + 1) * SGU_GROUP_DIM)
            mixed = jnp.dot(ws_bf[g], v[rows, lanes], preferred_element_type=F32) + bs_ref[:, g:g + 1]
            o_ref[rows, lanes] = (u[rows, lanes] * mixed).astype(BF16)


def _gmlp_gate(moe, ln_g, ln_b, mod_rows, layer, w_in_bf, b_in, sgu_g, sgu_b, w_s, b_s, seq):
    wts, x1, yg = moe
    t, d = x1.shape
    tm = 512
    steps_per_batch = seq // tm
    n_steps = t // tm
    gw = GMLP_WIDTH
    return pl.pallas_call(
        _gmlp_body,
        grid=(n_steps,),
        in_specs=_combine_specs(tm, n_steps, steps_per_batch, layer - 1) + [
            pl.BlockSpec((None, 1, N_MOD * d), lambda i: (layer * MOD_ROWS + i // steps_per_batch, 0, 0)),
            pl.BlockSpec((d, 2 * gw), lambda i: (0, 0)),
            pl.BlockSpec((1, 2 * gw), lambda i: (0, 0)),
            pl.BlockSpec((1, gw), lambda i: (0, 0)),
            pl.BlockSpec((1, gw), lambda i: (0, 0)),
            pl.BlockSpec((N_SGU_GROUPS, CHUNK, CHUNK), lambda i: (0, 0, 0)),
            pl.BlockSpec((CHUNK, N_SGU_GROUPS), lambda i: (0, 0)),
        ],
        out_specs=[pl.BlockSpec((tm, d), lambda i: (i, 0)), pl.BlockSpec((tm, gw), lambda i: (i, 0))],
        out_shape=[jax.ShapeDtypeStruct((t, d), F32), jax.ShapeDtypeStruct((t, gw), BF16)],
        scratch_shapes=[pltpu.VMEM((N_SGU_GROUPS, CHUNK, CHUNK), BF16)],
        compiler_params=_cparams(1),
        name="combine_gmlp_gate",
    )(wts, x1, mod_rows, ln_g.reshape(1, d), ln_b.reshape(1, d), yg, yg,
      mod_rows, w_in_bf, b_in.reshape(1, 2 * gw), sgu_g.reshape(1, gw), sgu_b.reshape(1, gw), w_s, b_s.T)


def _router_params(w_group, b_group, w_expert, b_expert):
    d = w_group.shape[0]
    w = jnp.zeros((d, ROUTER_LANES), F32)
    w = w.at[:, :N_EXPERT_GROUPS].set(w_group).at[:, EXPERT_COL0:EXPERT_COL0 + N_EXPERTS].set(w_expert)
    b = jnp.zeros((1, ROUTER_LANES), F32)
    b = b.at[0, :N_EXPERT_GROUPS].set(b_group).at[0, EXPERT_COL0:EXPERT_COL0 + N_EXPERTS].set(b_expert)
    return w, b


def _moe_experts(x1, h2, ids, wts, layer, w_gate_up, w_down):
    t = x1.shape[0]
    n_rows = 2 * t + N_EXPERTS * EXPERT_TILE
    n_tiles = n_rows // EXPERT_TILE
    pos, tile_expert, n_used, tile_valid = _expert_sort(ids, n_tiles)
    pos2 = pos.reshape(2, t)
    xs = _sc_scatter_rows(h2, pos2, n_rows)
    ys = _expert_mlp(tile_expert, n_used, tile_valid, xs, w_gate_up, w_down, layer)
    yg = _sc_gather_rows(ys, pos2.reshape(2 * t))
    return wts, x1, yg


def kernel(x, c, positions, ada_w, ada_b, post_ln_g, post_ln_b, attn_w_qkv, attn_b_qkv, attn_sinks, attn_w_o, attn_b_o, gmlp_w_in, gmlp_b_in, gmlp_sgu_ln_g, gmlp_sgu_ln_b, gmlp_w_s, gmlp_b_s, gmlp_w_out, gmlp_b_out, moe_w_group_router, moe_b_group_router, moe_w_expert_router, moe_b_expert_router, moe_w_gate_up, moe_w_down):
    batch, seq, d = x.shape
    t = batch * seq
    x2 = x.reshape(t, d)
    c_pad = jnp.zeros((MOD_ROWS, d), F32).at[:batch].set(c)
    mod_rows = _adaln_mod(c_pad, ada_w, ada_b).reshape(DEPTH * MOD_ROWS, 1, N_MOD * d)

    moe = None
    for layer in range(DEPTH):
        j = layer // 2
        if layer % 2 == 0:
            if moe is not None:
                x2 = _combine_ln(moe, mod_rows, layer - 1, post_ln_g[layer - 1, 1], post_ln_b[layer - 1, 1], seq)
            qkv = _qkv_rope(x2, mod_rows, layer, positions, attn_w_qkv[j], attn_b_qkv[j], seq)
            mix = _attention(qkv, attn_sinks[j], batch, seq)
            w_out, b_out = attn_w_o[j], attn_b_o[j]
        else:
            x2, mix = _gmlp_gate(moe, post_ln_g[layer - 1, 1], post_ln_b[layer - 1, 1], mod_rows, layer,
                                 gmlp_w_in[j].astype(BF16), gmlp_b_in[j], gmlp_sgu_ln_g[j], gmlp_sgu_ln_b[j],
                                 gmlp_w_s[j], gmlp_b_s[j], seq)
            w_out, b_out = gmlp_w_out[j], gmlp_b_out[j]
        w_router, b_router = _router_params(moe_w_group_router[layer], moe_b_group_router[layer],
                                            moe_w_expert_router[layer], moe_b_expert_router[layer])
        x1, h2, ids, wts = _proj_ln_router(mix, x2, mod_rows, layer, w_out, b_out, post_ln_g[layer, 0],
                                           post_ln_b[layer, 0], w_router, b_router, seq)
        moe = _moe_experts(x1, h2, ids, wts, layer, moe_w_gate_up, moe_w_down)
    x2 = _combine_ln(moe, mod_rows, DEPTH - 1, post_ln_g[DEPTH - 1, 1], post_ln_b[DEPTH - 1, 1], seq)
    return x2.reshape(batch, seq, d)
```

```python
import functools

import jax
import jax.numpy as jnp
from jax import lax
from jax.experimental import pallas as pl
from jax.experimental.pallas import tpu as pltpu
from jax.experimental.pallas import tpu_sc as plsc

F32 = jnp.float32
BF16 = jnp.bfloat16
I32 = jnp.int32

D_MODEL = 1024
DEPTH = 2
HEAD_DIM = 64
N_Q_HEADS = 16
N_KV_HEADS = 4
GQA_GROUP = N_Q_HEADS // N_KV_HEADS
WINDOW = 128
ROPE_THETA = 10000.0
Q_WIDTH = N_Q_HEADS * HEAD_DIM
KV_WIDTH = N_KV_HEADS * HEAD_DIM
QKV_WIDTH = Q_WIDTH + 2 * KV_WIDTH
CHUNK = 128
GMLP_WIDTH = 2 * D_MODEL
N_SGU_GROUPS = 8
SGU_GROUP_DIM = GMLP_WIDTH // N_SGU_GROUPS
N_EXPERT_GROUPS = 4
EXPERTS_PER_GROUP = 8
N_EXPERTS = N_EXPERT_GROUPS * EXPERTS_PER_GROUP
EXPERT_FF = D_MODEL // 4
N_MOD = 6
DEEPNORM_ALPHA = (2.0 * DEPTH) ** 0.25
LN_EPS = 1e-5

LANES = 128
MOD_ROWS = 8
ROUTER_LANES = 128
EXPERT_COL0 = 8
SORT_CHUNK = 256
EXPERT_TILE = 512
SUB_ROWS = 256
VMEM_LIMIT = 56 * 1024 * 1024
SC_CORES = 2
SC_WORKERS = 32
SC_CHUNK = 64


def _cparams(n_axes, vmem=VMEM_LIMIT):
    return pltpu.CompilerParams(dimension_semantics=("arbitrary",) * n_axes, vmem_limit_bytes=vmem)


U32 = jnp.uint32
ROW_CHUNKS = D_MODEL // 2 // LANES


def _store_rows(ref, val, token0=0):
    n = val.shape[0]
    half = D_MODEL // 2
    lo = lax.bitcast_convert_type(val[:, :half].astype(BF16).astype(F32), U32)
    hi = lax.bitcast_convert_type(val[:, half:].astype(BF16).astype(F32), U32)
    words = (lo >> 16) | hi
    for c in range(ROW_CHUNKS):
        ref[pl.ds(token0 * ROW_CHUNKS + c, n, stride=ROW_CHUNKS), :] = words[:, c * LANES:(c + 1) * LANES]


def _load_rows(ref, n):
    words = jnp.concatenate([ref[pl.ds(c, n, stride=ROW_CHUNKS), :] for c in range(ROW_CHUNKS)], axis=1)
    lo = lax.bitcast_convert_type(words << 16, F32)
    hi = lax.bitcast_convert_type(words & jnp.uint32(0xFFFF0000), F32)
    return jnp.concatenate([lo, hi], axis=1)


def _layer_norm(r, g, b):
    mu = jnp.mean(r, axis=-1, keepdims=True)
    d = r - mu
    var = jnp.mean(d * d, axis=-1, keepdims=True)
    return d * lax.rsqrt(var + LN_EPS) * g + b


def _mod_body(c_ref, w_ref, b_ref, o_ref):
    c = c_ref[...]
    ca = c * jax.nn.sigmoid(c)
    o_ref[...] = jnp.dot(ca.astype(BF16), w_ref[...].astype(BF16), preferred_element_type=F32) + b_ref[...]


def _adaln_mod(c_pad, ada_w, ada_b):
    tn = 1536
    n_out = N_MOD * D_MODEL
    return pl.pallas_call(
        _mod_body,
        grid=(DEPTH, n_out // tn),
        in_specs=[
            pl.BlockSpec((MOD_ROWS, D_MODEL), lambda l, j: (0, 0)),
            pl.BlockSpec((None, D_MODEL, tn), lambda l, j: (l, 0, j)),
            pl.BlockSpec((None, 1, tn), lambda l, j: (l, 0, j)),
        ],
        out_specs=pl.BlockSpec((None, MOD_ROWS, tn), lambda l, j: (l, 0, j)),
        out_shape=jax.ShapeDtypeStruct((DEPTH, MOD_ROWS, n_out), F32),
        compiler_params=_cparams(2),
        name="adaln_mod",
    )(c_pad, ada_w, ada_b.reshape(DEPTH, 1, n_out))


def _qkv_body(x_ref, mod_ref, pos_ref, w_ref, b_ref, invf_ref, o_ref, wbf_ref):
    @pl.when(pl.program_id(0) == 0)
    def _():
        wbf_ref[...] = w_ref[...].astype(BF16)

    tm = x_ref.shape[0]
    sh = mod_ref[:, 0:D_MODEL]
    sc = mod_ref[:, D_MODEL:2 * D_MODEL]
    h = x_ref[...] * (1.0 + sc) + sh
    qkv = jnp.dot(h.astype(BF16), wbf_ref[...], preferred_element_type=F32) + b_ref[...]

    ang = invf_ref[...] * pos_ref[...].astype(F32)
    c = jnp.cos(ang)
    s = jnp.sin(ang)
    ct = jnp.concatenate([c, c, c, c], axis=0).T
    st = jnp.concatenate([-s, s, -s, s], axis=0).T
    lane = lax.broadcasted_iota(I32, (tm, LANES), 1)
    first_half = (lane & (HEAD_DIM // 2)) == 0
    n_rope = (Q_WIDTH + KV_WIDTH) // LANES
    for j in range(n_rope):
        blk = qkv[:, j * LANES:(j + 1) * LANES]
        rot = jnp.where(first_half, pltpu.roll(blk, LANES - HEAD_DIM // 2, 1), pltpu.roll(blk, HEAD_DIM // 2, 1))
        r = blk * ct + rot * st
        if j < Q_WIDTH // LANES:
            r = r * (HEAD_DIM ** -0.5)
        o_ref[:, j * LANES:(j + 1) * LANES] = r.astype(BF16)
    o_ref[:, Q_WIDTH + KV_WIDTH:] = qkv[:, Q_WIDTH + KV_WIDTH:].astype(BF16)


def _qkv_rope(x2, mod_rows, layer, positions, w_qkv, b_qkv, seq):
    t = x2.shape[0]
    tm = 512
    steps_per_batch = seq // tm
    inv_freq = ROPE_THETA ** (-jnp.arange(0, HEAD_DIM, 2, dtype=F32) / HEAD_DIM)
    return pl.pallas_call(
        _qkv_body,
        grid=(t // tm,),
        in_specs=[
            pl.BlockSpec((tm, D_MODEL), lambda i: (i, 0)),
            pl.BlockSpec((None, 1, N_MOD * D_MODEL), lambda i: (layer * MOD_ROWS + i // steps_per_batch, 0, 0)),
            pl.BlockSpec((None, 1, tm), lambda i: (i, 0, 0)),
            pl.BlockSpec((D_MODEL, QKV_WIDTH), lambda i: (0, 0)),
            pl.BlockSpec((1, QKV_WIDTH), lambda i: (0, 0)),
            pl.BlockSpec((HEAD_DIM // 2, 1), lambda i: (0, 0)),
        ],
        out_specs=pl.BlockSpec((tm, QKV_WIDTH), lambda i: (i, 0)),
        out_shape=jax.ShapeDtypeStruct((t, QKV_WIDTH), BF16),
        scratch_shapes=[pltpu.VMEM((D_MODEL, QKV_WIDTH), BF16)],
        compiler_params=_cparams(1),
        name="qkv_rope",
    )(x2, mod_rows, positions.reshape(t // tm, 1, tm), w_qkv, b_qkv.reshape(1, QKV_WIDTH),
      inv_freq.reshape(HEAD_DIM // 2, 1))


BF16_ROWS = 16


def _attn_prepare(kv, kab_ref, vx_ref, slot):
    kv = kv.astype(F32)
    low = lax.broadcasted_iota(I32, (WINDOW, LANES), 1) < HEAD_DIM
    ones = jnp.ones((WINDOW, LANES), F32)
    for g in range(N_KV_HEADS):
        for part, ref in ((0, None), (KV_WIDTH, vx_ref)):
            tile = kv[:, part + (g // 2) * LANES:part + (g // 2 + 1) * LANES]
            other = pltpu.roll(tile, HEAD_DIM, 1)
            in_low, in_high = (tile, other) if g % 2 == 0 else (other, tile)
            if ref is None:
                kab_ref[slot, 2 * g] = jnp.where(low, in_low, 0.0).astype(BF16)
                kab_ref[slot, 2 * g + 1] = jnp.where(low, 0.0, in_high).astype(BF16)
            else:
                both = jnp.where(low, in_low, in_high)
                vx_ref[slot, g] = jnp.concatenate([both, ones], axis=1).astype(BF16)


def _attn_block(sink_ref, q, kab_ref, vx_ref, s_ref, p_ref, prev, cur, first_block):
    for g in range(N_KV_HEADS):
        q_pair = jnp.concatenate([q[:, (2 * g) * LANES:(2 * g + 1) * LANES],
                                  q[:, (2 * g + 1) * LANES:(2 * g + 2) * LANES]], axis=0)
        for a in range(2):
            kband = jnp.concatenate([kab_ref[prev, 2 * g + a], kab_ref[cur, 2 * g + a]], axis=0)
            s = lax.dot_general(q_pair, kband, (((1,), (1,)), ((), ())), preferred_element_type=F32)
            s_ref[GQA_GROUP * g + a] = s[:WINDOW]
            s_ref[GQA_GROUP * g + 2 + a] = s[WINDOW:]
    qi = lax.broadcasted_iota(I32, (WINDOW, 2 * WINDOW), 0) + WINDOW
    kj = lax.broadcasted_iota(I32, (WINDOW, 2 * WINDOW), 1)
    mask = (kj <= qi) & (kj > qi - WINDOW) & ((kj >= WINDOW) | jnp.logical_not(first_block))
    key0 = lax.broadcasted_iota(I32, (1, 2 * WINDOW), 1) == 0
    for h in range(N_Q_HEADS):
        s = jnp.where(mask, s_ref[h], jnp.where(key0, sink_ref[h], -jnp.inf))
        m = jnp.max(s, axis=-1, keepdims=True)
        p_ref[h] = jnp.exp(s - m).astype(BF16)
    low = lax.broadcasted_iota(I32, (WINDOW, LANES), 1) < HEAD_DIM
    sink_row = ((lax.broadcasted_iota(I32, (BF16_ROWS, 2 * LANES), 0) == 0)
                & (lax.broadcasted_iota(I32, (BF16_ROWS, 2 * LANES), 1) < LANES))
    out_tiles = []
    for g in range(N_KV_HEADS):
        v_prev = vx_ref[prev, g]
        v_head = jnp.where(sink_row, 0.0, v_prev[:BF16_ROWS].astype(F32)).astype(BF16)
        vband = jnp.concatenate([v_head, v_prev[BF16_ROWS:], vx_ref[cur, g]], axis=0)
        p4 = p_ref[GQA_GROUP * g:GQA_GROUP * (g + 1)].reshape(GQA_GROUP * WINDOW, 2 * WINDOW)
        o4 = jnp.dot(p4, vband, preferred_element_type=F32)
        heads = []
        for j in range(GQA_GROUP):
            blk = o4[j * WINDOW:(j + 1) * WINDOW]
            heads.append(blk[:, :LANES] / blk[:, LANES:])
        out_tiles.append(jnp.where(low, heads[0], heads[1]))
        out_tiles.append(jnp.where(low, heads[2], heads[3]))
    return jnp.concatenate(out_tiles, axis=1).astype(BF16)


def _attn_body(sink_ref, q_ref, kv_ref, o_ref, kab_ref, vx_ref, s_ref, p_ref):
    n = pl.program_id(1)

    @pl.when(n == 0)
    def _():
        kab_ref[1] = jnp.zeros(kab_ref.shape[1:], kab_ref.dtype)
        half = (N_KV_HEADS, WINDOW, LANES)
        vx_ref[1] = jnp.concatenate([jnp.zeros(half, BF16), jnp.ones(half, BF16)], axis=-1)

    scratch = (kab_ref, vx_ref, s_ref, p_ref)
    _attn_prepare(kv_ref[0:WINDOW, :], kab_ref, vx_ref, 0)
    o_ref[0:WINDOW, :] = _attn_block(sink_ref, q_ref[0:WINDOW, :], *scratch, 1, 0, n == 0)
    _attn_prepare(kv_ref[WINDOW:2 * WINDOW, :], kab_ref, vx_ref, 1)
    o_ref[WINDOW:2 * WINDOW, :] = _attn_block(sink_ref, q_ref[WINDOW:2 * WINDOW, :], *scratch, 0, 1, False)


def _attention(qkv, sinks, batch, seq):
    t = qkv.shape[0]
    tq = 2 * WINDOW
    steps = seq // tq
    kv_col = Q_WIDTH // (2 * KV_WIDTH)
    return pl.pallas_call(
        _attn_body,
        grid=(batch, steps),
        in_specs=[
            pl.BlockSpec(memory_space=pltpu.SMEM),
            pl.BlockSpec((tq, Q_WIDTH), lambda b, n: (b * steps + n, 0)),
            pl.BlockSpec((tq, 2 * KV_WIDTH), lambda b, n: (b * steps + n, kv_col)),
        ],
        out_specs=pl.BlockSpec((tq, Q_WIDTH), lambda b, n: (b * steps + n, 0)),
        out_shape=jax.ShapeDtypeStruct((t, Q_WIDTH), BF16),
        scratch_shapes=[
            pltpu.VMEM((2, 2 * N_KV_HEADS, WINDOW, LANES), BF16),
            pltpu.VMEM((2, N_KV_HEADS, WINDOW, 2 * LANES), BF16),
            pltpu.VMEM((N_Q_HEADS, WINDOW, 2 * WINDOW), F32),
            pltpu.VMEM((N_Q_HEADS, WINDOW, 2 * WINDOW), BF16),
        ],
        compiler_params=_cparams(2),
        name="swa_attention",
    )(sinks, qkv, qkv)


def _route(lt):
    tm = lt.shape[1]
    row = lax.broadcasted_iota(I32, (EXPERTS_PER_GROUP, tm), 0)
    neg = -jnp.inf
    gl = jnp.where(row < N_EXPERT_GROUPS, lt[0:EXPERTS_PER_GROUP], neg)
    gm = jnp.max(gl, axis=0, keepdims=True)
    g_p = 1.0 / jnp.sum(jnp.exp(gl - gm), axis=0, keepdims=True)
    g_idx = jnp.min(jnp.where(gl == gm, row, EXPERTS_PER_GROUP), axis=0, keepdims=True)
    sel = lt[EXPERT_COL0 + (N_EXPERT_GROUPS - 1) * EXPERTS_PER_GROUP:EXPERT_COL0 + N_EXPERTS]
    for g in range(N_EXPERT_GROUPS - 2, -1, -1):
        lo = EXPERT_COL0 + g * EXPERTS_PER_GROUP
        sel = jnp.where(g_idx == g, lt[lo:lo + EXPERTS_PER_GROUP], sel)
    v1 = jnp.max(sel, axis=0, keepdims=True)
    i1 = jnp.min(jnp.where(sel == v1, row, EXPERTS_PER_GROUP), axis=0, keepdims=True)
    sel2 = jnp.where(row == i1, neg, sel)
    v2 = jnp.max(sel2, axis=0, keepdims=True)
    i2 = jnp.min(jnp.where(sel2 == v2, row, EXPERTS_PER_GROUP), axis=0, keepdims=True)
    e2 = jnp.exp(v2 - v1)
    w1 = g_p / (1.0 + e2)
    w2 = g_p * e2 / (1.0 + e2)
    base = g_idx * EXPERTS_PER_GROUP
    return base + i1, base + i2, w1, w2


def _proj_body(o_ref, x_ref, mod_ref, w_ref, b_ref, lng_ref, lnb_ref, wr_ref, br_ref,
               x1_ref, h2_ref, ids_ref, wts_ref, wbf_ref):
    @pl.when(pl.program_id(0) == 0)
    def _():
        wbf_ref[...] = w_ref[...].astype(BF16)

    d = D_MODEL
    gt_m = mod_ref[:, 2 * d:3 * d]
    sh_f = mod_ref[:, 3 * d:4 * d]
    sc_f = mod_ref[:, 4 * d:5 * d]
    wr = wr_ref[...].astype(BF16)
    for r in range(x_ref.shape[0] // SUB_ROWS):
        rows = slice(r * SUB_ROWS, (r + 1) * SUB_ROWS)
        y = jnp.dot(o_ref[rows, :], wbf_ref[...], preferred_element_type=F32) + b_ref[...]
        x1 = _layer_norm(DEEPNORM_ALPHA * x_ref[rows, :] + (1.0 + gt_m) * y, lng_ref[...], lnb_ref[...])
        x1_ref[rows, :] = x1
        h2 = x1 * (1.0 + sc_f) + sh_f
        _store_rows(h2_ref, h2, r * SUB_ROWS)
        logits = jnp.dot(h2.astype(BF16), wr, preferred_element_type=F32) + br_ref[...]
        ea, eb, wa, wb = _route(logits.T)
        ids_ref[0:1, rows] = ea
        ids_ref[1:2, rows] = eb
        wts_ref[0:1, rows] = wa
        wts_ref[1:2, rows] = wb


def _proj_ln_router(o, x2, mod_rows, layer, w, b, ln_g, ln_b, w_router, b_router, seq):
    t, k = o.shape
    tm = 512
    steps_per_batch = seq // tm
    d = D_MODEL
    return pl.pallas_call(
        _proj_body,
        grid=(t // tm,),
        in_specs=[
            pl.BlockSpec((tm, k), lambda i: (i, 0)),
            pl.BlockSpec((tm, d), lambda i: (i, 0)),
            pl.BlockSpec((None, 1, N_MOD * d), lambda i: (layer * MOD_ROWS + i // steps_per_batch, 0, 0)),
            pl.BlockSpec((k, d), lambda i: (0, 0)),
            pl.BlockSpec((1, d), lambda i: (0, 0)),
            pl.BlockSpec((1, d), lambda i: (0, 0)),
            pl.BlockSpec((1, d), lambda i: (0, 0)),
            pl.BlockSpec((d, ROUTER_LANES), lambda i: (0, 0)),
            pl.BlockSpec((1, ROUTER_LANES), lambda i: (0, 0)),
        ],
        out_specs=[
            pl.BlockSpec((tm, d), lambda i: (i, 0)),
            pl.BlockSpec((tm * ROW_CHUNKS, LANES), lambda i: (i, 0)),
            pl.BlockSpec((2, tm), lambda i: (0, i)),
            pl.BlockSpec((2, tm), lambda i: (0, i)),
        ],
        out_shape=[
            jax.ShapeDtypeStruct((t, d), F32),
            jax.ShapeDtypeStruct((t * ROW_CHUNKS, LANES), U32),
            jax.ShapeDtypeStruct((2, t), I32),
            jax.ShapeDtypeStruct((2, t), F32),
        ],
        scratch_shapes=[pltpu.VMEM((k, d), BF16)],
        compiler_params=_cparams(1),
        name="proj_ln_router",
    )(o, x2, mod_rows, w, b.reshape(1, d), ln_g.reshape(1, d), ln_b.reshape(1, d), w_router, b_router)


def _sort_body(ids_ref, pos_ref, te_ref, nused_ref, tv_ref, rank_ref):
    n_rows = ids_ref.shape[0]
    c = SORT_CHUNK
    erow = lax.broadcasted_iota(I32, (N_EXPERTS, c), 0)
    tri = (lax.broadcasted_iota(I32, (c, c), 0) <= lax.broadcasted_iota(I32, (c, c), 1)).astype(BF16)

    def rank_step(r, carry):
        onehot = erow == ids_ref[pl.ds(r, 1), :]
        pref = jnp.dot(onehot.astype(BF16), tri, preferred_element_type=F32)
        rank = jnp.sum(jnp.where(onehot, pref + carry, 0.0), axis=0, keepdims=True) - 1.0
        rank_ref[pl.ds(r, 1), :] = rank
        return carry + pref[:, c - 1:c]

    counts = lax.fori_loop(0, n_rows, rank_step, jnp.zeros((N_EXPERTS, 1), F32), unroll=8)
    n_tile = jnp.floor((counts + (EXPERT_TILE - 1)) * (1.0 / EXPERT_TILE))
    low = (lax.broadcasted_iota(I32, (N_EXPERTS, N_EXPERTS), 1)
           <= lax.broadcasted_iota(I32, (N_EXPERTS, N_EXPERTS), 0)).astype(BF16)
    cum = jnp.dot(low, jnp.broadcast_to(n_tile, (N_EXPERTS, LANES)).astype(BF16),
                  preferred_element_type=F32)[:, 0:1]
    row_off = (cum - n_tile) * EXPERT_TILE

    def pos_step(r, _):
        onehot = erow == ids_ref[pl.ds(r, 1), :]
        off = jnp.sum(jnp.where(onehot, row_off, 0.0), axis=0, keepdims=True)
        pos_ref[pl.ds(r, 1), :] = (off + rank_ref[pl.ds(r, 1), :]).astype(I32)
        return 0

    lax.fori_loop(0, n_rows, pos_step, 0, unroll=8)
    total = jnp.max(cum, axis=0, keepdims=True)
    n_lanes = te_ref.shape[1]
    tile = jnp.minimum(lax.broadcasted_iota(I32, (N_EXPERTS, n_lanes), 1).astype(F32), total - 1.0)
    te_ref[...] = jnp.sum(jnp.where(cum <= tile, 1.0, 0.0), axis=0, keepdims=True).astype(I32)
    nused_ref[...] = jnp.broadcast_to(total, nused_ref.shape).astype(I32)
    tile_f = lax.broadcasted_iota(I32, (N_EXPERTS, n_lanes), 1).astype(F32)
    first = cum - n_tile
    rows_left = jnp.clip(counts - (tile_f - first) * EXPERT_TILE, 0.0, float(EXPERT_TILE))
    owns = (first <= tile_f) & (tile_f < cum)
    tv_ref[...] = jnp.sum(jnp.where(owns, rows_left, 0.0), axis=0, keepdims=True).astype(I32)


def _expert_sort(ids, n_tiles):
    n_assign = ids.shape[0] * ids.shape[1]
    n_rows = n_assign // SORT_CHUNK
    te_lanes = -(-n_tiles // LANES) * LANES
    pos, te, nused, tv = pl.pallas_call(
        _sort_body,
        grid=(1,),
        in_specs=[pl.BlockSpec((n_rows, SORT_CHUNK), lambda i: (0, 0))],
        out_specs=[
            pl.BlockSpec((n_rows, SORT_CHUNK), lambda i: (0, 0)),
            pl.BlockSpec((1, te_lanes), lambda i: (0, 0)),
            pl.BlockSpec((1, LANES), lambda i: (0, 0)),
            pl.BlockSpec((1, te_lanes), lambda i: (0, 0)),
        ],
        out_shape=[
            jax.ShapeDtypeStruct((n_rows, SORT_CHUNK), I32),
            jax.ShapeDtypeStruct((1, te_lanes), I32),
            jax.ShapeDtypeStruct((1, LANES), I32),
            jax.ShapeDtypeStruct((1, te_lanes), I32),
        ],
        scratch_shapes=[pltpu.VMEM((n_rows, SORT_CHUNK), F32)],
        compiler_params=_cparams(1),
        name="expert_sort",
    )(ids.reshape(n_rows, SORT_CHUNK))
    return pos, te[0, :n_tiles], nused[0, :1], tv[0, :n_tiles]


def _sc_mesh():
    return plsc.VectorSubcoreMesh(core_axis_name="c", subcore_axis_name="s", num_cores=SC_CORES,
                                  num_subcores=SC_WORKERS // SC_CORES)


def _sc_scatter_rows(src, pos2, n_rows):
    t = pos2.shape[1]
    src3 = src.reshape(t, ROW_CHUNKS, LANES)
    per_worker = t // SC_WORKERS
    n_chunks = per_worker // SC_CHUNK
    idx = pos2.reshape(2, SC_WORKERS, n_chunks, SC_CHUNK).transpose(1, 0, 2, 3)
    idx = idx.reshape(SC_WORKERS, 2 * n_chunks, SC_CHUNK)

    @functools.partial(
        pl.kernel, mesh=_sc_mesh(),
        out_type=jax.ShapeDtypeStruct((n_rows, ROW_CHUNKS, LANES), src.dtype),
        scratch_types=[
            pltpu.VMEM((2 * n_chunks, SC_CHUNK), I32),
            pltpu.VMEM((2, SC_CHUNK, ROW_CHUNKS, LANES), src.dtype),
            pltpu.SemaphoreType.DMA((2,)),
            pltpu.SemaphoreType.DMA((2,)),
        ],
        name="sc_scatter_rows",
    )
    def scatter(src_hbm, idx_hbm, out_hbm, idx_v, rows_v, rsem, wsem):
        wid = lax.axis_index("s") * SC_CORES + lax.axis_index("c")
        pltpu.sync_copy(idx_hbm.at[wid], idx_v)

        def read(j):
            b = j % 2
            return pltpu.async_copy(src_hbm.at[pl.ds(wid * per_worker + j * SC_CHUNK, SC_CHUNK)], rows_v.at[b],
                                    rsem.at[b])

        def write(j):
            b = j % 2
            return [pltpu.async_copy(rows_v.at[b], out_hbm.at[idx_v.at[k * n_chunks + j]], wsem.at[b])
                    for k in range(2)]

        reads = {0: read(0)}
        writes = {}
        for j in range(n_chunks):
            reads.pop(j).wait()
            if j + 1 < n_chunks:
                for cp in writes.pop(j - 1, []):
                    cp.wait()
                reads[j + 1] = read(j + 1)
            writes[j] = write(j)
        for cps in writes.values():
            for cp in cps:
                cp.wait()

    return scatter(src3, idx).reshape(n_rows * ROW_CHUNKS, LANES)


def _sc_gather_rows(table, idx):
    n = idx.shape[0]
    table3 = table.reshape(-1, ROW_CHUNKS, LANES)
    per_worker = n // SC_WORKERS
    n_chunks = per_worker // SC_CHUNK

    @functools.partial(
        pl.kernel, mesh=_sc_mesh(),
        out_type=jax.ShapeDtypeStruct((n, ROW_CHUNKS, LANES), table.dtype),
        scratch_types=[
            pltpu.VMEM((n_chunks, SC_CHUNK), I32),
            pltpu.VMEM((2, SC_CHUNK, ROW_CHUNKS, LANES), table.dtype),
            pltpu.SemaphoreType.DMA((2,)),
            pltpu.SemaphoreType.DMA((2,)),
        ],
        name="sc_gather_rows",
    )
    def gather(table_hbm, idx_hbm, out_hbm, idx_v, rows_v, rsem, wsem):
        wid = lax.axis_index("s") * SC_CORES + lax.axis_index("c")
        pltpu.sync_copy(idx_hbm.at[wid], idx_v)

        def read(j):
            b = j % 2
            return pltpu.async_copy(table_hbm.at[idx_v.at[j]], rows_v.at[b], rsem.at[b])

        def write(j):
            b = j % 2
            return pltpu.async_copy(rows_v.at[b], out_hbm.at[pl.ds(wid * per_worker + j * SC_CHUNK, SC_CHUNK)],
                                    wsem.at[b])

        reads = {0: read(0)}
        writes = {}
        for j in range(n_chunks):
            reads.pop(j).wait()
            if j + 1 < n_chunks:
                if j - 1 in writes:
                    writes.pop(j - 1).wait()
                reads[j + 1] = read(j + 1)
            writes[j] = write(j)
        for cp in writes.values():
            cp.wait()

    out = gather(table3, idx.reshape(SC_WORKERS, n_chunks, SC_CHUNK))
    return out.reshape(n * ROW_CHUNKS, LANES)


XS_SLOTS = 3


def _expert_body(te_ref, nused_ref, tv_ref, xs_hbm, wgu_ref, wd_ref, ys_ref, wgu_bf, wd_bf, xbuf, xsem):
    i = pl.program_id(0)
    n_used = nused_ref[0]
    used = i < n_used
    prev = te_ref[jnp.maximum(i - 1, 0)]
    tile_rows = EXPERT_TILE * ROW_CHUNKS

    def fetch(tile):
        slot = lax.rem(tile, XS_SLOTS)
        r0 = pl.multiple_of(tile * tile_rows, tile_rows)
        return pltpu.make_async_copy(xs_hbm.at[pl.ds(r0, tile_rows)], xbuf.at[slot], xsem.at[slot])

    @pl.when((i == 0) & (n_used > 0))
    def _():
        fetch(0).start()

    @pl.when((i == 0) & (n_used > 1))
    def _():
        fetch(1).start()

    @pl.when(i + 2 < n_used)
    def _():
        fetch(i + 2).start()

    @pl.when(used & ((i == 0) | (te_ref[i] != prev)))
    def _():
        wgu_bf[...] = wgu_ref[...].astype(BF16)
        wd_bf[...] = wd_ref[...].astype(BF16)

    @pl.when(used)
    def _():
        fetch(i).wait()
        live = lax.broadcasted_iota(I32, (EXPERT_TILE, 1), 0) < tv_ref[i]
        xs = jnp.where(live, _load_rows(xbuf.at[lax.rem(i, XS_SLOTS)], EXPERT_TILE), 0.0).astype(BF16)
        gu = jnp.dot(xs, wgu_bf[...], preferred_element_type=F32)
        gate = gu[:, :EXPERT_FF]
        up = gu[:, EXPERT_FF:]
        act = gate * jax.nn.sigmoid(gate) * up
        _store_rows(ys_ref, jnp.dot(act.astype(BF16), wd_bf[...], preferred_element_type=F32))

    @pl.when(jnp.logical_not(used))
    def _():
        ys_ref[...] = jnp.zeros(ys_ref.shape, ys_ref.dtype)


def _expert_mlp(tile_expert, n_used, tile_valid, xs, w_gate_up, w_down, layer):
    d = D_MODEL
    n_tiles = xs.shape[0] // (EXPERT_TILE * ROW_CHUNKS)
    f2 = 2 * EXPERT_FF
    tile_rows = EXPERT_TILE * ROW_CHUNKS
    e0 = layer * N_EXPERTS
    grid_spec = pltpu.PrefetchScalarGridSpec(
        num_scalar_prefetch=3,
        grid=(n_tiles,),
        in_specs=[
            pl.BlockSpec(memory_space=pl.ANY),
            pl.BlockSpec((None, d, f2), lambda i, te, nu, tv: (e0 + te[i], 0, 0)),
            pl.BlockSpec((None, EXPERT_FF, d), lambda i, te, nu, tv: (e0 + te[i], 0, 0)),
        ],
        out_specs=pl.BlockSpec((tile_rows, LANES), lambda i, te, nu, tv: (i, 0)),
        scratch_shapes=[pltpu.VMEM((d, f2), BF16), pltpu.VMEM((EXPERT_FF, d), BF16),
                        pltpu.VMEM((XS_SLOTS, tile_rows, LANES), U32), pltpu.SemaphoreType.DMA((XS_SLOTS,))],
    )
    return pl.pallas_call(
        _expert_body,
        grid_spec=grid_spec,
        out_shape=jax.ShapeDtypeStruct(xs.shape, U32),
        compiler_params=_cparams(1),
        name="expert_mlp",
    )(tile_expert, n_used, tile_valid, xs, w_gate_up.reshape(DEPTH * N_EXPERTS, d, f2),
      w_down.reshape(DEPTH * N_EXPERTS, EXPERT_FF, d))


def _moe_combine(wts_ref, x_ref, mod_ref, lng_ref, lnb_ref, ya_ref, yb_ref):
    tm = x_ref.shape[0]
    d = D_MODEL
    cols = []
    for k in range(2):
        wt = jnp.broadcast_to(wts_ref[k:k + 1, :], (LANES, tm)).T
        cols.append(jnp.concatenate([wt] * (d // LANES), axis=1))
    y = cols[0] * _load_rows(ya_ref, tm) + cols[1] * _load_rows(yb_ref, tm)
    gt_f = mod_ref[:, 5 * d:6 * d]
    return _layer_norm(DEEPNORM_ALPHA * x_ref[...] + (1.0 + gt_f) * y, lng_ref[...], lnb_ref[...])


def _combine_specs(tm, n_steps, steps_per_batch, layer):
    d = D_MODEL
    return [
        pl.BlockSpec((2, tm), lambda i: (0, i)),
        pl.BlockSpec((tm, d), lambda i: (i, 0)),
        pl.BlockSpec((None, 1, N_MOD * d), lambda i: (layer * MOD_ROWS + i // steps_per_batch, 0, 0)),
        pl.BlockSpec((1, d), lambda i: (0, 0)),
        pl.BlockSpec((1, d), lambda i: (0, 0)),
        pl.BlockSpec((tm * ROW_CHUNKS, LANES), lambda i: (i, 0)),
        pl.BlockSpec((tm * ROW_CHUNKS, LANES), lambda i: (n_steps + i, 0)),
    ]


def _combine_body(wts_ref, x_ref, mod_ref, lng_ref, lnb_ref, ya_ref, yb_ref, o_ref):
    o_ref[...] = _moe_combine(wts_ref, x_ref, mod_ref, lng_ref, lnb_ref, ya_ref, yb_ref)


def _combine_ln(moe, mod_rows, layer, ln_g, ln_b, seq):
    wts, x1, yg = moe
    t, d = x1.shape
    tm = 512
    n_steps = t // tm
    return pl.pallas_call(
        _combine_body,
        grid=(n_steps,),
        in_specs=_combine_specs(tm, n_steps, seq // tm, layer),
        out_specs=pl.BlockSpec((tm, d), lambda i: (i, 0)),
        out_shape=jax.ShapeDtypeStruct((t, d), F32),
        compiler_params=_cparams(1),
        name="moe_combine_ln",
    )(wts, x1, mod_rows, ln_g.reshape(1, d), ln_b.reshape(1, d), yg, yg)


def _gmlp_body(wts_ref, x1_ref, modp_ref, lng_ref, lnb_ref, ya_ref, yb_ref,
               mod_ref, w_ref, b_ref, g_ref, beta_ref, ws_ref, bs_ref, x2_ref, o_ref, ws_bf):
    @pl.when(pl.program_id(0) == 0)
    def _():
        tri = lax.broadcasted_iota(I32, (CHUNK, CHUNK), 0) >= lax.broadcasted_iota(I32, (CHUNK, CHUNK), 1)
        for g in range(N_SGU_GROUPS):
            ws_bf[g] = jnp.where(tri, ws_ref[g], 0.0).astype(BF16)

    tm = x1_ref.shape[0]
    d = D_MODEL
    x2 = _moe_combine(wts_ref, x1_ref, modp_ref, lng_ref, lnb_ref, ya_ref, yb_ref)
    x2_ref[...] = x2
    sh = mod_ref[:, 0:d]
    sc = mod_ref[:, d:2 * d]
    h = x2 * (1.0 + sc) + sh
    z = jnp.dot(h.astype(BF16), w_ref[...], preferred_element_type=F32) + b_ref[...]
    z = 0.5 * z * (1.0 + lax.erf(z * (2.0 ** -0.5)))
    u = z[:, :GMLP_WIDTH]
    v = _layer_norm(z[:, GMLP_WIDTH:], g_ref[...], beta_ref[...]).astype(BF16)
    for ci in range(tm // CHUNK):
        rows = slice(ci * CHUNK, (ci + 1) * CHUNK)
        for g in range(N_SGU_GROUPS):
            lanes = slice(g * SGU_GROUP_DIM, (g + 1) * SGU_GROUP_DIM)
            mixed = jnp.dot(ws_bf[g], v[rows, lanes], preferred_element_type=F32) + bs_ref[:, g:g + 1]
            o_ref[rows, lanes] = (u[rows, lanes] * mixed).astype(BF16)


def _gmlp_gate(moe, ln_g, ln_b, mod_rows, layer, w_in_bf, b_in, sgu_g, sgu_b, w_s, b_s, seq):
    wts, x1, yg = moe
    t, d = x1.shape
    tm = 512
    steps_per_batch = seq // tm
    n_steps = t // tm
    gw = GMLP_WIDTH
    return pl.pallas_call(
        _gmlp_body,
        grid=(n_steps,),
        in_specs=_combine_specs(tm, n_steps, steps_per_batch, layer - 1) + [
            pl.BlockSpec((None, 1, N_MOD * d), lambda i: (layer * MOD_ROWS + i // steps_per_batch, 0, 0)),
            pl.BlockSpec((d, 2 * gw), lambda i: (0, 0)),
            pl.BlockSpec((1, 2 * gw), lambda i: (0, 0)),
            pl.BlockSpec((1, gw), lambda i: (0, 0)),
            pl.BlockSpec((1, gw), lambda i: (0, 0)),
            pl.BlockSpec((N_SGU_GROUPS, CHUNK, CHUNK), lambda i: (0, 0, 0)),
            pl.BlockSpec((CHUNK, N_SGU_GROUPS), lambda i: (0, 0)),
        ],
        out_specs=[pl.BlockSpec((tm, d), lambda i: (i, 0)), pl.BlockSpec((tm, gw), lambda i: (i, 0))],
        out_shape=[jax.ShapeDtypeStruct((t, d), F32), jax.ShapeDtypeStruct((t, gw), BF16)],
        scratch_shapes=[pltpu.VMEM((N_SGU_GROUPS, CHUNK, CHUNK), BF16)],
        compiler_params=_cparams(1),
        name="combine_gmlp_gate",
    )(wts, x1, mod_rows, ln_g.reshape(1, d), ln_b.reshape(1, d), yg, yg,
      mod_rows, w_in_bf, b_in.reshape(1, 2 * gw), sgu_g.reshape(1, gw), sgu_b.reshape(1, gw), w_s, b_s.T)


def _router_params(w_group, b_group, w_expert, b_expert):
    d = w_group.shape[0]
    w = jnp.zeros((d, ROUTER_LANES), F32)
    w = w.at[:, :N_EXPERT_GROUPS].set(w_group).at[:, EXPERT_COL0:EXPERT_COL0 + N_EXPERTS].set(w_expert)
    b = jnp.zeros((1, ROUTER_LANES), F32)
    b = b.at[0, :N_EXPERT_GROUPS].set(b_group).at[0, EXPERT_COL0:EXPERT_COL0 + N_EXPERTS].set(b_expert)
    return w, b


def _moe_experts(x1, h2, ids, wts, layer, w_gate_up, w_down):
    t = x1.shape[0]
    n_rows = 2 * t + N_EXPERTS * EXPERT_TILE
    n_tiles = n_rows // EXPERT_TILE
    pos, tile_expert, n_used, tile_valid = _expert_sort(ids, n_tiles)
    pos2 = pos.reshape(2, t)
    xs = _sc_scatter_rows(h2, pos2, n_rows)
    ys = _expert_mlp(tile_expert, n_used, tile_valid, xs, w_gate_up, w_down, layer)
    yg = _sc_gather_rows(ys, pos2.reshape(2 * t))
    return wts, x1, yg


def kernel(x, c, positions, ada_w, ada_b, post_ln_g, post_ln_b, attn_w_qkv, attn_b_qkv, attn_sinks, attn_w_o, attn_b_o, gmlp_w_in, gmlp_b_in, gmlp_sgu_ln_g, gmlp_sgu_ln_b, gmlp_w_s, gmlp_b_s, gmlp_w_out, gmlp_b_out, moe_w_group_router, moe_b_group_router, moe_w_expert_router, moe_b_expert_router, moe_w_gate_up, moe_w_down):
    batch, seq, d = x.shape
    t = batch * seq
    x2 = x.reshape(t, d)
    c_pad = jnp.zeros((MOD_ROWS, d), F32).at[:batch].set(c)
    mod_rows = _adaln_mod(c_pad, ada_w, ada_b).reshape(DEPTH * MOD_ROWS, 1, N_MOD * d)

    moe = None
    for layer in range(DEPTH):
        j = layer // 2
        if layer % 2 == 0:
            if moe is not None:
                x2 = _combine_ln(moe, mod_rows, layer - 1, post_ln_g[layer - 1, 1], post_ln_b[layer - 1, 1], seq)
            qkv = _qkv_rope(x2, mod_rows, layer, positions, attn_w_qkv[j], attn_b_qkv[j], seq)
            mix = _attention(qkv, attn_sinks[j], batch, seq)
            w_out, b_out = attn_w_o[j], attn_b_o[j]
        else:
            x2, mix = _gmlp_gate(moe, post_ln_g[layer - 1, 1], post_ln_b[layer - 1, 1], mod_rows, layer,
                                 gmlp_w_in[j].astype(BF16), gmlp_b_in[j], gmlp_sgu_ln_g[j], gmlp_sgu_ln_b[j],
                                 gmlp_w_s[j], gmlp_b_s[j], seq)
            w_out, b_out = gmlp_w_out[j], gmlp_b_out[j]
        w_router, b_router = _router_params(moe_w_group_router[layer], moe_b_group_router[layer],
                                            moe_w_expert_router[layer], moe_b_expert_router[layer])
        x1, h2, ids, wts = _proj_ln_router(mix, x2, mod_rows, layer, w_out, b_out, post_ln_g[layer, 0],
                                           post_ln_b[layer, 0], w_router, b_router, seq)
        moe = _moe_experts(x1, h2, ids, wts, layer, moe_w_gate_up, moe_w_down)
    x2 = _combine_ln(moe, mod_rows, DEPTH - 1, post_ln_g[DEPTH - 1, 1], post_ln_b[DEPTH - 1, 1], seq)
    return x2.reshape(batch, seq, d)
```

```python
import functools

import jax
import jax.numpy as jnp
from jax import lax
from jax.experimental import pallas as pl
from jax.experimental.pallas import tpu as pltpu
from jax.experimental.pallas import tpu_sc as plsc

F32 = jnp.float32
BF16 = jnp.bfloat16
I32 = jnp.int32

D_MODEL = 1024
DEPTH = 2
HEAD_DIM = 64
N_Q_HEADS = 16
N_KV_HEADS = 4
GQA_GROUP = N_Q_HEADS // N_KV_HEADS
WINDOW = 128
ROPE_THETA = 10000.0
Q_WIDTH = N_Q_HEADS * HEAD_DIM
KV_WIDTH = N_KV_HEADS * HEAD_DIM
QKV_WIDTH = Q_WIDTH + 2 * KV_WIDTH
CHUNK = 128
GMLP_WIDTH = 2 * D_MODEL
N_SGU_GROUPS = 8
SGU_GROUP_DIM = GMLP_WIDTH // N_SGU_GROUPS
N_EXPERT_GROUPS = 4
EXPERTS_PER_GROUP = 8
N_EXPERTS = N_EXPERT_GROUPS * EXPERTS_PER_GROUP
EXPERT_FF = D_MODEL // 4
N_MOD = 6
DEEPNORM_ALPHA = (2.0 * DEPTH) ** 0.25
LN_EPS = 1e-5

LANES = 128
MOD_ROWS = 8
ROUTER_LANES = 128
EXPERT_COL0 = 8
SORT_CHUNK = 256
EXPERT_TILE = 512
SUB_ROWS = 256
VMEM_LIMIT = 56 * 1024 * 1024
SC_CORES = 2
SC_WORKERS = 32
SC_CHUNK = 64


def _cparams(n_axes, vmem=VMEM_LIMIT):
    return pltpu.CompilerParams(dimension_semantics=("arbitrary",) * n_axes, vmem_limit_bytes=vmem)


U32 = jnp.uint32
ROW_CHUNKS = D_MODEL // 2 // LANES


def _store_rows(ref, val, token0=0):
    n = val.shape[0]
    half = D_MODEL // 2
    lo = lax.bitcast_convert_type(val[:, :half].astype(BF16).astype(F32), U32)
    hi = lax.bitcast_convert_type(val[:, half:].astype(BF16).astype(F32), U32)
    words = (lo >> 16) | hi
    for c in range(ROW_CHUNKS):
        ref[pl.ds(token0 * ROW_CHUNKS + c, n, stride=ROW_CHUNKS), :] = words[:, c * LANES:(c + 1) * LANES]


def _load_rows(ref, n, token0=0):
    words = jnp.concatenate([ref[pl.ds(token0 * ROW_CHUNKS + c, n, stride=ROW_CHUNKS), :]
                             for c in range(ROW_CHUNKS)], axis=1)
    lo = lax.bitcast_convert_type(words << 16, F32)
    hi = lax.bitcast_convert_type(words & jnp.uint32(0xFFFF0000), F32)
    return jnp.concatenate([lo, hi], axis=1)


def _layer_norm(r, g, b):
    mu = jnp.mean(r, axis=-1, keepdims=True)
    d = r - mu
    var = jnp.mean(d * d, axis=-1, keepdims=True)
    return d * lax.rsqrt(var + LN_EPS) * g + b


def _mod_body(c_ref, w_ref, b_ref, o_ref):
    c = c_ref[...]
    ca = c * jax.nn.sigmoid(c)
    o_ref[...] = jnp.dot(ca.astype(BF16), w_ref[...].astype(BF16), preferred_element_type=F32) + b_ref[...]


def _adaln_mod(c_pad, ada_w, ada_b):
    tn = 1536
    n_out = N_MOD * D_MODEL
    return pl.pallas_call(
        _mod_body,
        grid=(DEPTH, n_out // tn),
        in_specs=[
            pl.BlockSpec((MOD_ROWS, D_MODEL), lambda l, j: (0, 0)),
            pl.BlockSpec((None, D_MODEL, tn), lambda l, j: (l, 0, j)),
            pl.BlockSpec((None, 1, tn), lambda l, j: (l, 0, j)),
        ],
        out_specs=pl.BlockSpec((None, MOD_ROWS, tn), lambda l, j: (l, 0, j)),
        out_shape=jax.ShapeDtypeStruct((DEPTH, MOD_ROWS, n_out), F32),
        compiler_params=_cparams(2),
        name="adaln_mod",
    )(c_pad, ada_w, ada_b.reshape(DEPTH, 1, n_out))


def _qkv_body(x_ref, mod_ref, pos_ref, w_ref, b_ref, invf_ref, o_ref, wbf_ref):
    @pl.when(pl.program_id(0) == 0)
    def _():
        wbf_ref[...] = w_ref[...].astype(BF16)

    tm = x_ref.shape[0]
    sh = mod_ref[:, 0:D_MODEL]
    sc = mod_ref[:, D_MODEL:2 * D_MODEL]
    h = x_ref[...] * (1.0 + sc) + sh
    qkv = jnp.dot(h.astype(BF16), wbf_ref[...], preferred_element_type=F32) + b_ref[...]

    ang = invf_ref[...] * pos_ref[...].astype(F32)
    c = jnp.cos(ang)
    s = jnp.sin(ang)
    ct = jnp.concatenate([c, c, c, c], axis=0).T
    st = jnp.concatenate([-s, s, -s, s], axis=0).T
    lane = lax.broadcasted_iota(I32, (tm, LANES), 1)
    first_half = (lane & (HEAD_DIM // 2)) == 0
    n_rope = (Q_WIDTH + KV_WIDTH) // LANES
    for j in range(n_rope):
        blk = qkv[:, j * LANES:(j + 1) * LANES]
        rot = jnp.where(first_half, pltpu.roll(blk, LANES - HEAD_DIM // 2, 1), pltpu.roll(blk, HEAD_DIM // 2, 1))
        r = blk * ct + rot * st
        if j < Q_WIDTH // LANES:
            r = r * (HEAD_DIM ** -0.5)
        o_ref[:, j * LANES:(j + 1) * LANES] = r.astype(BF16)
    o_ref[:, Q_WIDTH + KV_WIDTH:] = qkv[:, Q_WIDTH + KV_WIDTH:].astype(BF16)


def _qkv_rope(x2, mod_rows, layer, positions, w_qkv, b_qkv, seq):
    t = x2.shape[0]
    tm = 1024
    steps_per_batch = seq // tm
    inv_freq = ROPE_THETA ** (-jnp.arange(0, HEAD_DIM, 2, dtype=F32) / HEAD_DIM)
    return pl.pallas_call(
        _qkv_body,
        grid=(t // tm,),
        in_specs=[
            pl.BlockSpec((tm, D_MODEL), lambda i: (i, 0)),
            pl.BlockSpec((None, 1, N_MOD * D_MODEL), lambda i: (layer * MOD_ROWS + i // steps_per_batch, 0, 0)),
            pl.BlockSpec((None, 1, tm), lambda i: (i, 0, 0)),
            pl.BlockSpec((D_MODEL, QKV_WIDTH), lambda i: (0, 0)),
            pl.BlockSpec((1, QKV_WIDTH), lambda i: (0, 0)),
            pl.BlockSpec((HEAD_DIM // 2, 1), lambda i: (0, 0)),
        ],
        out_specs=pl.BlockSpec((tm, QKV_WIDTH), lambda i: (i, 0)),
        out_shape=jax.ShapeDtypeStruct((t, QKV_WIDTH), BF16),
        scratch_shapes=[pltpu.VMEM((D_MODEL, QKV_WIDTH), BF16)],
        compiler_params=_cparams(1),
        name="qkv_rope",
    )(x2, mod_rows, positions.reshape(t // tm, 1, tm), w_qkv, b_qkv.reshape(1, QKV_WIDTH),
      inv_freq.reshape(HEAD_DIM // 2, 1))


BF16_ROWS = 16


def _attn_prepare(kv, kab_ref, vx_ref, slot):
    kv = kv.astype(F32)
    low = lax.broadcasted_iota(I32, (WINDOW, LANES), 1) < HEAD_DIM
    ones = jnp.ones((WINDOW, LANES), F32)
    for g in range(N_KV_HEADS):
        for part, ref in ((0, None), (KV_WIDTH, vx_ref)):
            tile = kv[:, part + (g // 2) * LANES:part + (g // 2 + 1) * LANES]
            other = pltpu.roll(tile, HEAD_DIM, 1)
            in_low, in_high = (tile, other) if g % 2 == 0 else (other, tile)
            if ref is None:
                kab_ref[slot, 2 * g] = jnp.where(low, in_low, 0.0).astype(BF16)
                kab_ref[slot, 2 * g + 1] = jnp.where(low, 0.0, in_high).astype(BF16)
            else:
                both = jnp.where(low, in_low, in_high)
                vx_ref[slot, g] = jnp.concatenate([both, ones], axis=1).astype(BF16)


def _attn_block(sink_ref, q, kab_ref, vx_ref, s_ref, p_ref, prev, cur, first_block):
    for g in range(N_KV_HEADS):
        q_pair = jnp.concatenate([q[:, (2 * g) * LANES:(2 * g + 1) * LANES],
                                  q[:, (2 * g + 1) * LANES:(2 * g + 2) * LANES]], axis=0)
        for a in range(2):
            kband = jnp.concatenate([kab_ref[prev, 2 * g + a], kab_ref[cur, 2 * g + a]], axis=0)
            s = lax.dot_general(q_pair, kband, (((1,), (1,)), ((), ())), preferred_element_type=F32)
            s_ref[GQA_GROUP * g + a] = s[:WINDOW]
            s_ref[GQA_GROUP * g + 2 + a] = s[WINDOW:]
    qi = lax.broadcasted_iota(I32, (WINDOW, 2 * WINDOW), 0) + WINDOW
    kj = lax.broadcasted_iota(I32, (WINDOW, 2 * WINDOW), 1)
    mask = (kj <= qi) & (kj > qi - WINDOW) & ((kj >= WINDOW) | jnp.logical_not(first_block))
    key0 = lax.broadcasted_iota(I32, (1, 2 * WINDOW), 1) == 0
    for h in range(N_Q_HEADS):
        s = jnp.where(mask, s_ref[h], jnp.where(key0, sink_ref[h], -jnp.inf))
        m = jnp.max(s, axis=-1, keepdims=True)
        p_ref[h] = jnp.exp(s - m).astype(BF16)
    low = lax.broadcasted_iota(I32, (WINDOW, LANES), 1) < HEAD_DIM
    sink_row = ((lax.broadcasted_iota(I32, (BF16_ROWS, 2 * LANES), 0) == 0)
                & (lax.broadcasted_iota(I32, (BF16_ROWS, 2 * LANES), 1) < LANES))
    out_tiles = []
    for g in range(N_KV_HEADS):
        v_prev = vx_ref[prev, g]
        v_head = jnp.where(sink_row, 0.0, v_prev[:BF16_ROWS].astype(F32)).astype(BF16)
        vband = jnp.concatenate([v_head, v_prev[BF16_ROWS:], vx_ref[cur, g]], axis=0)
        p4 = p_ref[GQA_GROUP * g:GQA_GROUP * (g + 1)].reshape(GQA_GROUP * WINDOW, 2 * WINDOW)
        o4 = jnp.dot(p4, vband, preferred_element_type=F32)
        heads = []
        for j in range(GQA_GROUP):
            blk = o4[j * WINDOW:(j + 1) * WINDOW]
            heads.append(blk[:, :LANES] / blk[:, LANES:])
        out_tiles.append(jnp.where(low, heads[0], heads[1]))
        out_tiles.append(jnp.where(low, heads[2], heads[3]))
    return jnp.concatenate(out_tiles, axis=1).astype(BF16)


def _attn_body(sink_ref, q_ref, kv_ref, o_ref, kab_ref, vx_ref, s_ref, p_ref):
    n = pl.program_id(1)

    @pl.when(n == 0)
    def _():
        kab_ref[1] = jnp.zeros(kab_ref.shape[1:], kab_ref.dtype)
        half = (N_KV_HEADS, WINDOW, LANES)
        vx_ref[1] = jnp.concatenate([jnp.zeros(half, BF16), jnp.ones(half, BF16)], axis=-1)

    scratch = (kab_ref, vx_ref, s_ref, p_ref)
    for blk in range(q_ref.shape[0] // WINDOW):
        rows = slice(blk * WINDOW, (blk + 1) * WINDOW)
        cur = blk % 2
        _attn_prepare(kv_ref[rows, :], kab_ref, vx_ref, cur)
        o_ref[rows, :] = _attn_block(sink_ref, q_ref[rows, :], *scratch, 1 - cur, cur,
                                     (n == 0) if blk == 0 else False)


def _attention(qkv, sinks, batch, seq):
    t = qkv.shape[0]
    tq = 4 * WINDOW
    steps = seq // tq
    kv_col = Q_WIDTH // (2 * KV_WIDTH)
    return pl.pallas_call(
        _attn_body,
        grid=(batch, steps),
        in_specs=[
            pl.BlockSpec(memory_space=pltpu.SMEM),
            pl.BlockSpec((tq, Q_WIDTH), lambda b, n: (b * steps + n, 0)),
            pl.BlockSpec((tq, 2 * KV_WIDTH), lambda b, n: (b * steps + n, kv_col)),
        ],
        out_specs=pl.BlockSpec((tq, Q_WIDTH), lambda b, n: (b * steps + n, 0)),
        out_shape=jax.ShapeDtypeStruct((t, Q_WIDTH), BF16),
        scratch_shapes=[
            pltpu.VMEM((2, 2 * N_KV_HEADS, WINDOW, LANES), BF16),
            pltpu.VMEM((2, N_KV_HEADS, WINDOW, 2 * LANES), BF16),
            pltpu.VMEM((N_Q_HEADS, WINDOW, 2 * WINDOW), F32),
            pltpu.VMEM((N_Q_HEADS, WINDOW, 2 * WINDOW), BF16),
        ],
        compiler_params=_cparams(2),
        name="swa_attention",
    )(sinks, qkv, qkv)


def _route(lt):
    tm = lt.shape[1]
    row = lax.broadcasted_iota(I32, (EXPERTS_PER_GROUP, tm), 0)
    neg = -jnp.inf
    gl = jnp.where(row < N_EXPERT_GROUPS, lt[0:EXPERTS_PER_GROUP], neg)
    gm = jnp.max(gl, axis=0, keepdims=True)
    g_p = 1.0 / jnp.sum(jnp.exp(gl - gm), axis=0, keepdims=True)
    g_idx = jnp.min(jnp.where(gl == gm, row, EXPERTS_PER_GROUP), axis=0, keepdims=True)
    sel = lt[EXPERT_COL0 + (N_EXPERT_GROUPS - 1) * EXPERTS_PER_GROUP:EXPERT_COL0 + N_EXPERTS]
    for g in range(N_EXPERT_GROUPS - 2, -1, -1):
        lo = EXPERT_COL0 + g * EXPERTS_PER_GROUP
        sel = jnp.where(g_idx == g, lt[lo:lo + EXPERTS_PER_GROUP], sel)
    v1 = jnp.max(sel, axis=0, keepdims=True)
    i1 = jnp.min(jnp.where(sel == v1, row, EXPERTS_PER_GROUP), axis=0, keepdims=True)
    sel2 = jnp.where(row == i1, neg, sel)
    v2 = jnp.max(sel2, axis=0, keepdims=True)
    i2 = jnp.min(jnp.where(sel2 == v2, row, EXPERTS_PER_GROUP), axis=0, keepdims=True)
    e2 = jnp.exp(v2 - v1)
    w1 = g_p / (1.0 + e2)
    w2 = g_p * e2 / (1.0 + e2)
    base = g_idx * EXPERTS_PER_GROUP
    return base + i1, base + i2, w1, w2


def _proj_body(o_ref, x_ref, mod_ref, w_ref, b_ref, lng_ref, lnb_ref, wr_ref, br_ref,
               x1_ref, h2_ref, ids_ref, wts_ref, wbf_ref):
    @pl.when(pl.program_id(0) == 0)
    def _():
        wbf_ref[...] = w_ref[...].astype(BF16)

    d = D_MODEL
    gt_m = mod_ref[:, 2 * d:3 * d]
    sh_f = mod_ref[:, 3 * d:4 * d]
    sc_f = mod_ref[:, 4 * d:5 * d]
    wr = wr_ref[...].astype(BF16)
    for r in range(x_ref.shape[0] // SUB_ROWS):
        rows = slice(r * SUB_ROWS, (r + 1) * SUB_ROWS)
        y = jnp.dot(o_ref[rows, :], wbf_ref[...], preferred_element_type=F32) + b_ref[...]
        x1 = _layer_norm(DEEPNORM_ALPHA * x_ref[rows, :] + (1.0 + gt_m) * y, lng_ref[...], lnb_ref[...])
        x1_ref[rows, :] = x1
        h2 = x1 * (1.0 + sc_f) + sh_f
        _store_rows(h2_ref, h2, r * SUB_ROWS)
        logits = jnp.dot(h2.astype(BF16), wr, preferred_element_type=F32) + br_ref[...]
        ea, eb, wa, wb = _route(logits.T)
        ids_ref[0:1, rows] = ea
        ids_ref[1:2, rows] = eb
        wts_ref[0:1, rows] = wa
        wts_ref[1:2, rows] = wb


def _proj_ln_router(o, x2, mod_rows, layer, w, b, ln_g, ln_b, w_router, b_router, seq):
    t, k = o.shape
    tm = 1024
    steps_per_batch = seq // tm
    d = D_MODEL
    return pl.pallas_call(
        _proj_body,
        grid=(t // tm,),
        in_specs=[
            pl.BlockSpec((tm, k), lambda i: (i, 0)),
            pl.BlockSpec((tm, d), lambda i: (i, 0)),
            pl.BlockSpec((None, 1, N_MOD * d), lambda i: (layer * MOD_ROWS + i // steps_per_batch, 0, 0)),
            pl.BlockSpec((k, d), lambda i: (0, 0), pipeline_mode=pl.Buffered(1)),
            pl.BlockSpec((1, d), lambda i: (0, 0)),
            pl.BlockSpec((1, d), lambda i: (0, 0)),
            pl.BlockSpec((1, d), lambda i: (0, 0)),
            pl.BlockSpec((d, ROUTER_LANES), lambda i: (0, 0)),
            pl.BlockSpec((1, ROUTER_LANES), lambda i: (0, 0)),
        ],
        out_specs=[
            pl.BlockSpec((tm, d), lambda i: (i, 0)),
            pl.BlockSpec((tm * ROW_CHUNKS, LANES), lambda i: (i, 0)),
            pl.BlockSpec((2, tm), lambda i: (0, i)),
            pl.BlockSpec((2, tm), lambda i: (0, i)),
        ],
        out_shape=[
            jax.ShapeDtypeStruct((t, d), F32),
            jax.ShapeDtypeStruct((t * ROW_CHUNKS, LANES), U32),
            jax.ShapeDtypeStruct((2, t), I32),
            jax.ShapeDtypeStruct((2, t), F32),
        ],
        scratch_shapes=[pltpu.VMEM((k, d), BF16)],
        compiler_params=_cparams(1),
        name="proj_ln_router",
    )(o, x2, mod_rows, w, b.reshape(1, d), ln_g.reshape(1, d), ln_b.reshape(1, d), w_router, b_router)


def _sort_body(ids_ref, pos_ref, te_ref, nused_ref, tv_ref, rank_ref):
    n_rows = ids_ref.shape[0]
    c = SORT_CHUNK
    erow = lax.broadcasted_iota(I32, (N_EXPERTS, c), 0)
    tri = (lax.broadcasted_iota(I32, (c, c), 0) <= lax.broadcasted_iota(I32, (c, c), 1)).astype(BF16)

    def rank_step(r, carry):
        onehot = erow == ids_ref[pl.ds(r, 1), :]
        pref = jnp.dot(onehot.astype(BF16), tri, preferred_element_type=F32)
        rank = jnp.sum(jnp.where(onehot, pref + carry, 0.0), axis=0, keepdims=True) - 1.0
        rank_ref[pl.ds(r, 1), :] = rank
        return carry + pref[:, c - 1:c]

    counts = lax.fori_loop(0, n_rows, rank_step, jnp.zeros((N_EXPERTS, 1), F32), unroll=8)
    n_tile = jnp.floor((counts + (EXPERT_TILE - 1)) * (1.0 / EXPERT_TILE))
    low = (lax.broadcasted_iota(I32, (N_EXPERTS, N_EXPERTS), 1)
           <= lax.broadcasted_iota(I32, (N_EXPERTS, N_EXPERTS), 0)).astype(BF16)
    cum = jnp.dot(low, jnp.broadcast_to(n_tile, (N_EXPERTS, LANES)).astype(BF16),
                  preferred_element_type=F32)[:, 0:1]
    row_off = (cum - n_tile) * EXPERT_TILE

    def pos_step(r, _):
        onehot = erow == ids_ref[pl.ds(r, 1), :]
        off = jnp.sum(jnp.where(onehot, row_off, 0.0), axis=0, keepdims=True)
        pos_ref[pl.ds(r, 1), :] = (off + rank_ref[pl.ds(r, 1), :]).astype(I32)
        return 0

    lax.fori_loop(0, n_rows, pos_step, 0, unroll=8)
    total = jnp.max(cum, axis=0, keepdims=True)
    n_lanes = te_ref.shape[1]
    tile = jnp.minimum(lax.broadcasted_iota(I32, (N_EXPERTS, n_lanes), 1).astype(F32), total - 1.0)
    te_ref[...] = jnp.sum(jnp.where(cum <= tile, 1.0, 0.0), axis=0, keepdims=True).astype(I32)
    nused_ref[...] = jnp.broadcast_to(total, nused_ref.shape).astype(I32)
    tile_f = lax.broadcasted_iota(I32, (N_EXPERTS, n_lanes), 1).astype(F32)
    first = cum - n_tile
    rows_left = jnp.clip(counts - (tile_f - first) * EXPERT_TILE, 0.0, float(EXPERT_TILE))
    owns = (first <= tile_f) & (tile_f < cum)
    tv_ref[...] = jnp.sum(jnp.where(owns, rows_left, 0.0), axis=0, keepdims=True).astype(I32)


def _expert_sort(ids, n_tiles):
    n_assign = ids.shape[0] * ids.shape[1]
    n_rows = n_assign // SORT_CHUNK
    te_lanes = -(-n_tiles // LANES) * LANES
    pos, te, nused, tv = pl.pallas_call(
        _sort_body,
        grid=(1,),
        in_specs=[pl.BlockSpec((n_rows, SORT_CHUNK), lambda i: (0, 0))],
        out_specs=[
            pl.BlockSpec((n_rows, SORT_CHUNK), lambda i: (0, 0)),
            pl.BlockSpec((1, te_lanes), lambda i: (0, 0)),
            pl.BlockSpec((1, LANES), lambda i: (0, 0)),
            pl.BlockSpec((1, te_lanes), lambda i: (0, 0)),
        ],
        out_shape=[
            jax.ShapeDtypeStruct((n_rows, SORT_CHUNK), I32),
            jax.ShapeDtypeStruct((1, te_lanes), I32),
            jax.ShapeDtypeStruct((1, LANES), I32),
            jax.ShapeDtypeStruct((1, te_lanes), I32),
        ],
        scratch_shapes=[pltpu.VMEM((n_rows, SORT_CHUNK), F32)],
        compiler_params=_cparams(1),
        name="expert_sort",
    )(ids.reshape(n_rows, SORT_CHUNK))
    return pos, te[0, :n_tiles], nused[0, :1], tv[0, :n_tiles]


def _sc_mesh():
    return plsc.VectorSubcoreMesh(core_axis_name="c", subcore_axis_name="s", num_cores=SC_CORES,
                                  num_subcores=SC_WORKERS // SC_CORES)


def _sc_scatter_rows(src, pos2, n_rows):
    t = pos2.shape[1]
    src3 = src.reshape(t, ROW_CHUNKS, LANES)
    per_worker = t // SC_WORKERS
    n_chunks = per_worker // SC_CHUNK
    idx = pos2.reshape(2, SC_WORKERS, n_chunks, SC_CHUNK).transpose(1, 0, 2, 3)
    idx = idx.reshape(SC_WORKERS, 2 * n_chunks, SC_CHUNK)

    @functools.partial(
        pl.kernel, mesh=_sc_mesh(),
        out_type=jax.ShapeDtypeStruct((n_rows, ROW_CHUNKS, LANES), src.dtype),
        scratch_types=[
            pltpu.VMEM((2 * n_chunks, SC_CHUNK), I32),
            pltpu.VMEM((2, SC_CHUNK, ROW_CHUNKS, LANES), src.dtype),
            pltpu.SemaphoreType.DMA((2,)),
            pltpu.SemaphoreType.DMA((2,)),
        ],
        name="sc_scatter_rows",
    )
    def scatter(src_hbm, idx_hbm, out_hbm, idx_v, rows_v, rsem, wsem):
        wid = lax.axis_index("s") * SC_CORES + lax.axis_index("c")
        pltpu.sync_copy(idx_hbm.at[wid], idx_v)

        def read(j):
            b = j % 2
            return pltpu.async_copy(src_hbm.at[pl.ds(wid * per_worker + j * SC_CHUNK, SC_CHUNK)], rows_v.at[b],
                                    rsem.at[b])

        def write(j):
            b = j % 2
            return [pltpu.async_copy(rows_v.at[b], out_hbm.at[idx_v.at[k * n_chunks + j]], wsem.at[b])
                    for k in range(2)]

        reads = {0: read(0)}
        writes = {}
        for j in range(n_chunks):
            reads.pop(j).wait()
            if j + 1 < n_chunks:
                for cp in writes.pop(j - 1, []):
                    cp.wait()
                reads[j + 1] = read(j + 1)
            writes[j] = write(j)
        for cps in writes.values():
            for cp in cps:
                cp.wait()

    return scatter(src3, idx).reshape(n_rows * ROW_CHUNKS, LANES)


def _sc_gather_rows(table, idx):
    n = idx.shape[0]
    table3 = table.reshape(-1, ROW_CHUNKS, LANES)
    per_worker = n // SC_WORKERS
    n_chunks = per_worker // SC_CHUNK

    @functools.partial(
        pl.kernel, mesh=_sc_mesh(),
        out_type=jax.ShapeDtypeStruct((n, ROW_CHUNKS, LANES), table.dtype),
        scratch_types=[
            pltpu.VMEM((n_chunks, SC_CHUNK), I32),
            pltpu.VMEM((2, SC_CHUNK, ROW_CHUNKS, LANES), table.dtype),
            pltpu.SemaphoreType.DMA((2,)),
            pltpu.SemaphoreType.DMA((2,)),
        ],
        name="sc_gather_rows",
    )
    def gather(table_hbm, idx_hbm, out_hbm, idx_v, rows_v, rsem, wsem):
        wid = lax.axis_index("s") * SC_CORES + lax.axis_index("c")
        pltpu.sync_copy(idx_hbm.at[wid], idx_v)

        def read(j):
            b = j % 2
            return pltpu.async_copy(table_hbm.at[idx_v.at[j]], rows_v.at[b], rsem.at[b])

        def write(j):
            b = j % 2
            return pltpu.async_copy(rows_v.at[b], out_hbm.at[pl.ds(wid * per_worker + j * SC_CHUNK, SC_CHUNK)],
                                    wsem.at[b])

        reads = {0: read(0)}
        writes = {}
        for j in range(n_chunks):
            reads.pop(j).wait()
            if j + 1 < n_chunks:
                if j - 1 in writes:
                    writes.pop(j - 1).wait()
                reads[j + 1] = read(j + 1)
            writes[j] = write(j)
        for cp in writes.values():
            cp.wait()

    out = gather(table3, idx.reshape(SC_WORKERS, n_chunks, SC_CHUNK))
    return out.reshape(n * ROW_CHUNKS, LANES)


XS_SLOTS = 3


def _expert_body(te_ref, nused_ref, tv_ref, xs_hbm, wgu_ref, wd_ref, ys_ref, wgu_bf, wd_bf, xbuf, xsem):
    i = pl.program_id(0)
    n_used = nused_ref[0]
    used = i < n_used
    prev = te_ref[jnp.maximum(i - 1, 0)]
    tile_rows = EXPERT_TILE * ROW_CHUNKS

    def fetch(tile):
        slot = lax.rem(tile, XS_SLOTS)
        r0 = pl.multiple_of(tile * tile_rows, tile_rows)
        return pltpu.make_async_copy(xs_hbm.at[pl.ds(r0, tile_rows)], xbuf.at[slot], xsem.at[slot])

    @pl.when((i == 0) & (n_used > 0))
    def _():
        fetch(0).start()

    @pl.when((i == 0) & (n_used > 1))
    def _():
        fetch(1).start()

    @pl.when(i + 2 < n_used)
    def _():
        fetch(i + 2).start()

    @pl.when(used & ((i == 0) | (te_ref[i] != prev)))
    def _():
        wgu_bf[...] = wgu_ref[...].astype(BF16)
        wd_bf[...] = wd_ref[...].astype(BF16)

    @pl.when(used)
    def _():
        fetch(i).wait()
        live = lax.broadcasted_iota(I32, (EXPERT_TILE, 1), 0) < tv_ref[i]
        xs = jnp.where(live, _load_rows(xbuf.at[lax.rem(i, XS_SLOTS)], EXPERT_TILE), 0.0).astype(BF16)
        gu = jnp.dot(xs, wgu_bf[...], preferred_element_type=F32)
        gate = gu[:, :EXPERT_FF]
        up = gu[:, EXPERT_FF:]
        act = gate * jax.nn.sigmoid(gate) * up
        _store_rows(ys_ref, jnp.dot(act.astype(BF16), wd_bf[...], preferred_element_type=F32))

    @pl.when(jnp.logical_not(used))
    def _():
        ys_ref[...] = jnp.zeros(ys_ref.shape, ys_ref.dtype)


def _expert_mlp(tile_expert, n_used, tile_valid, xs, w_gate_up, w_down, layer):
    d = D_MODEL
    n_tiles = xs.shape[0] // (EXPERT_TILE * ROW_CHUNKS)
    f2 = 2 * EXPERT_FF
    tile_rows = EXPERT_TILE * ROW_CHUNKS
    e0 = layer * N_EXPERTS
    grid_spec = pltpu.PrefetchScalarGridSpec(
        num_scalar_prefetch=3,
        grid=(n_tiles,),
        in_specs=[
            pl.BlockSpec(memory_space=pl.ANY),
            pl.BlockSpec((None, d, f2), lambda i, te, nu, tv: (e0 + te[i], 0, 0)),
            pl.BlockSpec((None, EXPERT_FF, d), lambda i, te, nu, tv: (e0 + te[i], 0, 0)),
        ],
        out_specs=pl.BlockSpec((tile_rows, LANES), lambda i, te, nu, tv: (i, 0)),
        scratch_shapes=[pltpu.VMEM((d, f2), BF16), pltpu.VMEM((EXPERT_FF, d), BF16),
                        pltpu.VMEM((XS_SLOTS, tile_rows, LANES), U32), pltpu.SemaphoreType.DMA((XS_SLOTS,))],
    )
    return pl.pallas_call(
        _expert_body,
        grid_spec=grid_spec,
        out_shape=jax.ShapeDtypeStruct(xs.shape, U32),
        compiler_params=_cparams(1),
        name="expert_mlp",
    )(tile_expert, n_used, tile_valid, xs, w_gate_up.reshape(DEPTH * N_EXPERTS, d, f2),
      w_down.reshape(DEPTH * N_EXPERTS, EXPERT_FF, d))


def _moe_combine(wts_ref, x_ref, mod_ref, lng_ref, lnb_ref, ya_ref, yb_ref, token0, n):
    d = D_MODEL
    rows = slice(token0, token0 + n)
    cols = []
    for k in range(2):
        wt = jnp.broadcast_to(wts_ref[k:k + 1, rows], (LANES, n)).T
        cols.append(jnp.concatenate([wt] * (d // LANES), axis=1))
    y = cols[0] * _load_rows(ya_ref, n, token0) + cols[1] * _load_rows(yb_ref, n, token0)
    gt_f = mod_ref[:, 5 * d:6 * d]
    return _layer_norm(DEEPNORM_ALPHA * x_ref[rows, :] + (1.0 + gt_f) * y, lng_ref[...], lnb_ref[...])


def _combine_specs(tm, n_steps, steps_per_batch, layer):
    d = D_MODEL
    return [
        pl.BlockSpec((2, tm), lambda i: (0, i)),
        pl.BlockSpec((tm, d), lambda i: (i, 0)),
        pl.BlockSpec((None, 1, N_MOD * d), lambda i: (layer * MOD_ROWS + i // steps_per_batch, 0, 0)),
        pl.BlockSpec((1, d), lambda i: (0, 0)),
        pl.BlockSpec((1, d), lambda i: (0, 0)),
        pl.BlockSpec((tm * ROW_CHUNKS, LANES), lambda i: (i, 0)),
        pl.BlockSpec((tm * ROW_CHUNKS, LANES), lambda i: (n_steps + i, 0)),
    ]


def _combine_body(wts_ref, x_ref, mod_ref, lng_ref, lnb_ref, ya_ref, yb_ref, o_ref):
    for r in range(x_ref.shape[0] // SUB_ROWS):
        o_ref[r * SUB_ROWS:(r + 1) * SUB_ROWS, :] = _moe_combine(wts_ref, x_ref, mod_ref, lng_ref, lnb_ref, ya_ref,
                                                                 yb_ref, r * SUB_ROWS, SUB_ROWS)


def _combine_ln(moe, mod_rows, layer, ln_g, ln_b, seq):
    wts, x1, yg = moe
    t, d = x1.shape
    tm = 1024
    n_steps = t // tm
    return pl.pallas_call(
        _combine_body,
        grid=(n_steps,),
        in_specs=_combine_specs(tm, n_steps, seq // tm, layer),
        out_specs=pl.BlockSpec((tm, d), lambda i: (i, 0)),
        out_shape=jax.ShapeDtypeStruct((t, d), F32),
        compiler_params=_cparams(1),
        name="moe_combine_ln",
    )(wts, x1, mod_rows, ln_g.reshape(1, d), ln_b.reshape(1, d), yg, yg)


def _gmlp_body(wts_ref, x1_ref, modp_ref, lng_ref, lnb_ref, ya_ref, yb_ref,
               mod_ref, w_ref, b_ref, g_ref, beta_ref, ws_ref, bs_ref, x2_ref, o_ref, ws_bf):
    @pl.when(pl.program_id(0) == 0)
    def _():
        tri = lax.broadcasted_iota(I32, (CHUNK, CHUNK), 0) >= lax.broadcasted_iota(I32, (CHUNK, CHUNK), 1)
        for g in range(N_SGU_GROUPS):
            ws_bf[g] = jnp.where(tri, ws_ref[g], 0.0).astype(BF16)

    d = D_MODEL
    sh = mod_ref[:, 0:d]
    sc = mod_ref[:, d:2 * d]
    sub_rows = x1_ref.shape[0]
    for r in range(x1_ref.shape[0] // sub_rows):
        r0 = r * sub_rows
        x2 = _moe_combine(wts_ref, x1_ref, modp_ref, lng_ref, lnb_ref, ya_ref, yb_ref, r0, sub_rows)
        x2_ref[r0:r0 + sub_rows, :] = x2
        h = x2 * (1.0 + sc) + sh
        z = jnp.dot(h.astype(BF16), w_ref[...], preferred_element_type=F32) + b_ref[...]
        z = 0.5 * z * (1.0 + lax.erf(z * (2.0 ** -0.5)))
        u = z[:, :GMLP_WIDTH]
        v = _layer_norm(z[:, GMLP_WIDTH:], g_ref[...], beta_ref[...]).astype(BF16)
        for ci in range(sub_rows // CHUNK):
            rows = slice(ci * CHUNK, (ci + 1) * CHUNK)
            out_rows = slice(r0 + ci * CHUNK, r0 + (ci + 1) * CHUNK)
            for g in range(N_SGU_GROUPS):
                lanes = slice(g * SGU_GROUP_DIM, (g + 1) * SGU_GROUP_DIM)
                mixed = jnp.dot(ws_bf[g], v[rows, lanes], preferred_element_type=F32) + bs_ref[:, g:g + 1]
                o_ref[out_rows, lanes] = (u[rows, lanes] * mixed).astype(BF16)


def _gmlp_gate(moe, ln_g, ln_b, mod_rows, layer, w_in_bf, b_in, sgu_g, sgu_b, w_s, b_s, seq):
    wts, x1, yg = moe
    t, d = x1.shape
    tm = 512
    steps_per_batch = seq // tm
    n_steps = t // tm
    gw = GMLP_WIDTH
    return pl.pallas_call(
        _gmlp_body,
        grid=(n_steps,),
        in_specs=_combine_specs(tm, n_steps, steps_per_batch, layer - 1) + [
            pl.BlockSpec((None, 1, N_MOD * d), lambda i: (layer * MOD_ROWS + i // steps_per_batch, 0, 0)),
            pl.BlockSpec((d, 2 * gw), lambda i: (0, 0)),
            pl.BlockSpec((1, 2 * gw), lambda i: (0, 0)),
            pl.BlockSpec((1, gw), lambda i: (0, 0)),
            pl.BlockSpec((1, gw), lambda i: (0, 0)),
            pl.BlockSpec((N_SGU_GROUPS, CHUNK, CHUNK), lambda i: (0, 0, 0)),
            pl.BlockSpec((CHUNK, N_SGU_GROUPS), lambda i: (0, 0)),
        ],
        out_specs=[pl.BlockSpec((tm, d), lambda i: (i, 0)), pl.BlockSpec((tm, gw), lambda i: (i, 0))],
        out_shape=[jax.ShapeDtypeStruct((t, d), F32), jax.ShapeDtypeStruct((t, gw), BF16)],
        scratch_shapes=[pltpu.VMEM((N_SGU_GROUPS, CHUNK, CHUNK), BF16)],
        compiler_params=_cparams(1),
        name="combine_gmlp_gate",
    )(wts, x1, mod_rows, ln_g.reshape(1, d), ln_b.reshape(1, d), yg, yg,
      mod_rows, w_in_bf, b_in.reshape(1, 2 * gw), sgu_g.reshape(1, gw), sgu_b.reshape(1, gw), w_s, b_s.T)


def _router_params(w_group, b_group, w_expert, b_expert):
    d = w_group.shape[0]
    w = jnp.zeros((d, ROUTER_LANES), F32)
    w = w.at[:, :N_EXPERT_GROUPS].set(w_group).at[:, EXPERT_COL0:EXPERT_COL0 + N_EXPERTS].set(w_expert)
    b = jnp.zeros((1, ROUTER_LANES), F32)
    b = b.at[0, :N_EXPERT_GROUPS].set(b_group).at[0, EXPERT_COL0:EXPERT_COL0 + N_EXPERTS].set(b_expert)
    return w, b


def _moe_experts(x1, h2, ids, wts, layer, w_gate_up, w_down):
    t = x1.shape[0]
    n_rows = 2 * t + N_EXPERTS * EXPERT_TILE
    n_tiles = n_rows // EXPERT_TILE
    pos, tile_expert, n_used, tile_valid = _expert_sort(ids, n_tiles)
    pos2 = pos.reshape(2, t)
    xs = _sc_scatter_rows(h2, pos2, n_rows)
    ys = _expert_mlp(tile_expert, n_used, tile_valid, xs, w_gate_up, w_down, layer)
    yg = _sc_gather_rows(ys, pos2.reshape(2 * t))
    return wts, x1, yg


def kernel(x, c, positions, ada_w, ada_b, post_ln_g, post_ln_b, attn_w_qkv, attn_b_qkv, attn_sinks, attn_w_o, attn_b_o, gmlp_w_in, gmlp_b_in, gmlp_sgu_ln_g, gmlp_sgu_ln_b, gmlp_w_s, gmlp_b_s, gmlp_w_out, gmlp_b_out, moe_w_group_router, moe_b_group_router, moe_w_expert_router, moe_b_expert_router, moe_w_gate_up, moe_w_down):
    batch, seq, d = x.shape
    t = batch * seq
    x2 = x.reshape(t, d)
    c_pad = jnp.zeros((MOD_ROWS, d), F32).at[:batch].set(c)
    mod_rows = _adaln_mod(c_pad, ada_w, ada_b).reshape(DEPTH * MOD_ROWS, 1, N_MOD * d)

    moe = None
    for layer in range(DEPTH):
        j = layer // 2
        if layer % 2 == 0:
            if moe is not None:
                x2 = _combine_ln(moe, mod_rows, layer - 1, post_ln_g[layer - 1, 1], post_ln_b[layer - 1, 1], seq)
            qkv = _qkv_rope(x2, mod_rows, layer, positions, attn_w_qkv[j], attn_b_qkv[j], seq)
            mix = _attention(qkv, attn_sinks[j], batch, seq)
            w_out, b_out = attn_w_o[j], attn_b_o[j]
        else:
            x2, mix = _gmlp_gate(moe, post_ln_g[layer - 1, 1], post_ln_b[layer - 1, 1], mod_rows, layer,
                                 gmlp_w_in[j].astype(BF16), gmlp_b_in[j], gmlp_sgu_ln_g[j], gmlp_sgu_ln_b[j],
                                 gmlp_w_s[j], gmlp_b_s[j], seq)
            w_out, b_out = gmlp_w_out[j], gmlp_b_out[j]
        w_router, b_router = _router_params(moe_w_group_router[layer], moe_b_group_router[layer],
                                            moe_w_expert_router[layer], moe_b_expert_router[layer])
        x1, h2, ids, wts = _proj_ln_router(mix, x2, mod_rows, layer, w_out, b_out, post_ln_g[layer, 0],
                                           post_ln_b[layer, 0], w_router, b_router, seq)
        moe = _moe_experts(x1, h2, ids, wts, layer, moe_w_gate_up, moe_w_down)
    x2 = _combine_ln(moe, mod_rows, DEPTH - 1, post_ln_g[DEPTH - 1, 1], post_ln_b[DEPTH - 1, 1], seq)
    return x2.reshape(batch, seq, d)
```

```python
import functools

import jax
import jax.numpy as jnp
from jax import lax
from jax.experimental import pallas as pl
from jax.experimental.pallas import tpu as pltpu
from jax.experimental.pallas import tpu_sc as plsc

F32 = jnp.float32
BF16 = jnp.bfloat16
I32 = jnp.int32

D_MODEL = 1024
DEPTH = 2
HEAD_DIM = 64
N_Q_HEADS = 16
N_KV_HEADS = 4
GQA_GROUP = N_Q_HEADS // N_KV_HEADS
WINDOW = 128
ROPE_THETA = 10000.0
Q_WIDTH = N_Q_HEADS * HEAD_DIM
KV_WIDTH = N_KV_HEADS * HEAD_DIM
QKV_WIDTH = Q_WIDTH + 2 * KV_WIDTH
CHUNK = 128
GMLP_WIDTH = 2 * D_MODEL
N_SGU_GROUPS = 8
SGU_GROUP_DIM = GMLP_WIDTH // N_SGU_GROUPS
N_EXPERT_GROUPS = 4
EXPERTS_PER_GROUP = 8
N_EXPERTS = N_EXPERT_GROUPS * EXPERTS_PER_GROUP
EXPERT_FF = D_MODEL // 4
N_MOD = 6
DEEPNORM_ALPHA = (2.0 * DEPTH) ** 0.25
LN_EPS = 1e-5

LANES = 128
MOD_ROWS = 8
ROUTER_LANES = 128
EXPERT_COL0 = 8
SORT_CHUNK = 256
EXPERT_TILE = 512
SUB_ROWS = 256
VMEM_LIMIT = 56 * 1024 * 1024
SC_CORES = 2
SC_WORKERS = 32
SC_CHUNK = 64


def _cparams(n_axes, vmem=VMEM_LIMIT):
    return pltpu.CompilerParams(dimension_semantics=("arbitrary",) * n_axes, vmem_limit_bytes=vmem)


U32 = jnp.uint32
ROW_CHUNKS = D_MODEL // 2 // LANES


def _store_rows(ref, val, token0=0):
    n = val.shape[0]
    half = D_MODEL // 2
    lo = lax.bitcast_convert_type(val[:, :half].astype(BF16).astype(F32), U32)
    hi = lax.bitcast_convert_type(val[:, half:].astype(BF16).astype(F32), U32)
    words = (lo >> 16) | hi
    for c in range(ROW_CHUNKS):
        ref[pl.ds(token0 * ROW_CHUNKS + c, n, stride=ROW_CHUNKS), :] = words[:, c * LANES:(c + 1) * LANES]


def _load_rows(ref, n, token0=0):
    words = jnp.concatenate([ref[pl.ds(token0 * ROW_CHUNKS + c, n, stride=ROW_CHUNKS), :]
                             for c in range(ROW_CHUNKS)], axis=1)
    lo = lax.bitcast_convert_type(words << 16, F32)
    hi = lax.bitcast_convert_type(words & jnp.uint32(0xFFFF0000), F32)
    return jnp.concatenate([lo, hi], axis=1)


def _layer_norm(r, g, b):
    mu = jnp.mean(r, axis=-1, keepdims=True)
    d = r - mu
    var = jnp.mean(d * d, axis=-1, keepdims=True)
    return d * lax.rsqrt(var + LN_EPS) * g + b


def _mod_body(c_ref, w_ref, b_ref, o_ref):
    c = c_ref[...]
    ca = c * jax.nn.sigmoid(c)
    o_ref[...] = jnp.dot(ca.astype(BF16), w_ref[...].astype(BF16), preferred_element_type=F32) + b_ref[...]


def _adaln_mod(c_pad, ada_w, ada_b):
    tn = 1536
    n_out = N_MOD * D_MODEL
    return pl.pallas_call(
        _mod_body,
        grid=(DEPTH, n_out // tn),
        in_specs=[
            pl.BlockSpec((MOD_ROWS, D_MODEL), lambda l, j: (0, 0)),
            pl.BlockSpec((None, D_MODEL, tn), lambda l, j: (l, 0, j)),
            pl.BlockSpec((None, 1, tn), lambda l, j: (l, 0, j)),
        ],
        out_specs=pl.BlockSpec((None, MOD_ROWS, tn), lambda l, j: (l, 0, j)),
        out_shape=jax.ShapeDtypeStruct((DEPTH, MOD_ROWS, n_out), F32),
        compiler_params=_cparams(2),
        name="adaln_mod",
    )(c_pad, ada_w, ada_b.reshape(DEPTH, 1, n_out))


def _qkv_body(x_ref, mod_ref, pos_ref, w_ref, b_ref, invf_ref, o_ref, wbf_ref):
    @pl.when(pl.program_id(0) == 0)
    def _():
        wbf_ref[...] = w_ref[...].astype(BF16)

    tm = x_ref.shape[0]
    sh = mod_ref[:, 0:D_MODEL]
    sc = mod_ref[:, D_MODEL:2 * D_MODEL]
    h = x_ref[...] * (1.0 + sc) + sh
    qkv = jnp.dot(h.astype(BF16), wbf_ref[...], preferred_element_type=F32) + b_ref[...]

    ang = invf_ref[...] * pos_ref[...].astype(F32)
    c = jnp.cos(ang)
    s = jnp.sin(ang)
    ct = jnp.concatenate([c, c, c, c], axis=0).T
    st = jnp.concatenate([-s, s, -s, s], axis=0).T
    lane = lax.broadcasted_iota(I32, (tm, LANES), 1)
    first_half = (lane & (HEAD_DIM // 2)) == 0
    n_rope = (Q_WIDTH + KV_WIDTH) // LANES
    for j in range(n_rope):
        blk = qkv[:, j * LANES:(j + 1) * LANES]
        rot = jnp.where(first_half, pltpu.roll(blk, LANES - HEAD_DIM // 2, 1), pltpu.roll(blk, HEAD_DIM // 2, 1))
        r = blk * ct + rot * st
        if j < Q_WIDTH // LANES:
            r = r * (HEAD_DIM ** -0.5)
        o_ref[:, j * LANES:(j + 1) * LANES] = r.astype(BF16)
    o_ref[:, Q_WIDTH + KV_WIDTH:] = qkv[:, Q_WIDTH + KV_WIDTH:].astype(BF16)


def _qkv_rope(x2, mod_rows, layer, positions, w_qkv, b_qkv, seq):
    t = x2.shape[0]
    tm = 1024
    steps_per_batch = seq // tm
    inv_freq = ROPE_THETA ** (-jnp.arange(0, HEAD_DIM, 2, dtype=F32) / HEAD_DIM)
    return pl.pallas_call(
        _qkv_body,
        grid=(t // tm,),
        in_specs=[
            pl.BlockSpec((tm, D_MODEL), lambda i: (i, 0)),
            pl.BlockSpec((None, 1, N_MOD * D_MODEL), lambda i: (layer * MOD_ROWS + i // steps_per_batch, 0, 0)),
            pl.BlockSpec((None, 1, tm), lambda i: (i, 0, 0)),
            pl.BlockSpec((D_MODEL, QKV_WIDTH), lambda i: (0, 0)),
            pl.BlockSpec((1, QKV_WIDTH), lambda i: (0, 0)),
            pl.BlockSpec((HEAD_DIM // 2, 1), lambda i: (0, 0)),
        ],
        out_specs=pl.BlockSpec((tm, QKV_WIDTH), lambda i: (i, 0)),
        out_shape=jax.ShapeDtypeStruct((t, QKV_WIDTH), BF16),
        scratch_shapes=[pltpu.VMEM((D_MODEL, QKV_WIDTH), BF16)],
        compiler_params=_cparams(1),
        name="qkv_rope",
    )(x2, mod_rows, positions.reshape(t // tm, 1, tm), w_qkv, b_qkv.reshape(1, QKV_WIDTH),
      inv_freq.reshape(HEAD_DIM // 2, 1))


BF16_ROWS = 16


def _attn_prepare(kv, kab_ref, vx_ref, slot):
    kv = kv.astype(F32)
    low = lax.broadcasted_iota(I32, (WINDOW, LANES), 1) < HEAD_DIM
    ones = jnp.ones((WINDOW, LANES), F32)
    for g in range(N_KV_HEADS):
        for part, ref in ((0, None), (KV_WIDTH, vx_ref)):
            tile = kv[:, part + (g // 2) * LANES:part + (g // 2 + 1) * LANES]
            other = pltpu.roll(tile, HEAD_DIM, 1)
            in_low, in_high = (tile, other) if g % 2 == 0 else (other, tile)
            if ref is None:
                kab_ref[slot, 2 * g] = jnp.where(low, in_low, 0.0).astype(BF16)
                kab_ref[slot, 2 * g + 1] = jnp.where(low, 0.0, in_high).astype(BF16)
            else:
                both = jnp.where(low, in_low, in_high)
                vx_ref[slot, g] = jnp.concatenate([both, ones], axis=1).astype(BF16)


def _attn_block(sink_ref, q, kab_ref, vx_ref, s_ref, p_ref, prev, cur, first_block):
    for g in range(N_KV_HEADS):
        q_pair = jnp.concatenate([q[:, (2 * g) * LANES:(2 * g + 1) * LANES],
                                  q[:, (2 * g + 1) * LANES:(2 * g + 2) * LANES]], axis=0)
        for a in range(2):
            kband = jnp.concatenate([kab_ref[prev, 2 * g + a], kab_ref[cur, 2 * g + a]], axis=0)
            s = lax.dot_general(q_pair, kband, (((1,), (1,)), ((), ())), preferred_element_type=F32)
            s_ref[GQA_GROUP * g + a] = s[:WINDOW]
            s_ref[GQA_GROUP * g + 2 + a] = s[WINDOW:]
    qi = lax.broadcasted_iota(I32, (WINDOW, 2 * WINDOW), 0) + WINDOW
    kj = lax.broadcasted_iota(I32, (WINDOW, 2 * WINDOW), 1)
    mask = (kj <= qi) & (kj > qi - WINDOW) & ((kj >= WINDOW) | jnp.logical_not(first_block))
    key0 = lax.broadcasted_iota(I32, (1, 2 * WINDOW), 1) == 0
    for h in range(N_Q_HEADS):
        s = jnp.where(mask, s_ref[h], jnp.where(key0, sink_ref[h], -jnp.inf))
        m = jnp.max(s, axis=-1, keepdims=True)
        p_ref[h] = jnp.exp(s - m).astype(BF16)
    low = lax.broadcasted_iota(I32, (WINDOW, LANES), 1) < HEAD_DIM
    sink_row = ((lax.broadcasted_iota(I32, (BF16_ROWS, 2 * LANES), 0) == 0)
                & (lax.broadcasted_iota(I32, (BF16_ROWS, 2 * LANES), 1) < LANES))
    out_tiles = []
    for g in range(N_KV_HEADS):
        v_prev = vx_ref[prev, g]
        v_head = jnp.where(sink_row, 0.0, v_prev[:BF16_ROWS].astype(F32)).astype(BF16)
        vband = jnp.concatenate([v_head, v_prev[BF16_ROWS:], vx_ref[cur, g]], axis=0)
        p4 = p_ref[GQA_GROUP * g:GQA_GROUP * (g + 1)].reshape(GQA_GROUP * WINDOW, 2 * WINDOW)
        o4 = jnp.dot(p4, vband, preferred_element_type=F32)
        heads = []
        for j in range(GQA_GROUP):
            blk = o4[j * WINDOW:(j + 1) * WINDOW]
            heads.append(blk[:, :LANES] / blk[:, LANES:])
        out_tiles.append(jnp.where(low, heads[0], heads[1]))
        out_tiles.append(jnp.where(low, heads[2], heads[3]))
    return jnp.concatenate(out_tiles, axis=1).astype(BF16)


def _attn_body(sink_ref, q_ref, kv_ref, o_ref, kab_ref, vx_ref, s_ref, p_ref):
    n = pl.program_id(1)

    @pl.when(n == 0)
    def _():
        kab_ref[1] = jnp.zeros(kab_ref.shape[1:], kab_ref.dtype)
        half = (N_KV_HEADS, WINDOW, LANES)
        vx_ref[1] = jnp.concatenate([jnp.zeros(half, BF16), jnp.ones(half, BF16)], axis=-1)

    scratch = (kab_ref, vx_ref, s_ref, p_ref)
    for blk in range(q_ref.shape[0] // WINDOW):
        rows = slice(blk * WINDOW, (blk + 1) * WINDOW)
        cur = blk % 2
        _attn_prepare(kv_ref[rows, :], kab_ref, vx_ref, cur)
        o_ref[rows, :] = _attn_block(sink_ref, q_ref[rows, :], *scratch, 1 - cur, cur,
                                     (n == 0) if blk == 0 else False)


def _attention(qkv, sinks, batch, seq):
    t = qkv.shape[0]
    tq = 4 * WINDOW
    steps = seq // tq
    kv_col = Q_WIDTH // (2 * KV_WIDTH)
    return pl.pallas_call(
        _attn_body,
        grid=(batch, steps),
        in_specs=[
            pl.BlockSpec(memory_space=pltpu.SMEM),
            pl.BlockSpec((tq, Q_WIDTH), lambda b, n: (b * steps + n, 0)),
            pl.BlockSpec((tq, 2 * KV_WIDTH), lambda b, n: (b * steps + n, kv_col)),
        ],
        out_specs=pl.BlockSpec((tq, Q_WIDTH), lambda b, n: (b * steps + n, 0)),
        out_shape=jax.ShapeDtypeStruct((t, Q_WIDTH), BF16),
        scratch_shapes=[
            pltpu.VMEM((2, 2 * N_KV_HEADS, WINDOW, LANES), BF16),
            pltpu.VMEM((2, N_KV_HEADS, WINDOW, 2 * LANES), BF16),
            pltpu.VMEM((N_Q_HEADS, WINDOW, 2 * WINDOW), F32),
            pltpu.VMEM((N_Q_HEADS, WINDOW, 2 * WINDOW), BF16),
        ],
        compiler_params=_cparams(2),
        name="swa_attention",
    )(sinks, qkv, qkv)


def _route(lt):
    tm = lt.shape[1]
    row = lax.broadcasted_iota(I32, (EXPERTS_PER_GROUP, tm), 0)
    neg = -jnp.inf
    gl = jnp.where(row < N_EXPERT_GROUPS, lt[0:EXPERTS_PER_GROUP], neg)
    gm = jnp.max(gl, axis=0, keepdims=True)
    g_p = 1.0 / jnp.sum(jnp.exp(gl - gm), axis=0, keepdims=True)
    g_idx = jnp.min(jnp.where(gl == gm, row, EXPERTS_PER_GROUP), axis=0, keepdims=True)
    sel = lt[EXPERT_COL0 + (N_EXPERT_GROUPS - 1) * EXPERTS_PER_GROUP:EXPERT_COL0 + N_EXPERTS]
    for g in range(N_EXPERT_GROUPS - 2, -1, -1):
        lo = EXPERT_COL0 + g * EXPERTS_PER_GROUP
        sel = jnp.where(g_idx == g, lt[lo:lo + EXPERTS_PER_GROUP], sel)
    v1 = jnp.max(sel, axis=0, keepdims=True)
    i1 = jnp.min(jnp.where(sel == v1, row, EXPERTS_PER_GROUP), axis=0, keepdims=True)
    sel2 = jnp.where(row == i1, neg, sel)
    v2 = jnp.max(sel2, axis=0, keepdims=True)
    i2 = jnp.min(jnp.where(sel2 == v2, row, EXPERTS_PER_GROUP), axis=0, keepdims=True)
    e2 = jnp.exp(v2 - v1)
    w1 = g_p / (1.0 + e2)
    w2 = g_p * e2 / (1.0 + e2)
    base = g_idx * EXPERTS_PER_GROUP
    return base + i1, base + i2, w1, w2


def _proj_body(o_ref, x_ref, mod_ref, w_ref, b_ref, lng_ref, lnb_ref, wr_ref, br_ref,
               x1_ref, h2_ref, ids_ref, wts_ref, wbf_ref):
    @pl.when(pl.program_id(0) == 0)
    def _():
        wbf_ref[...] = w_ref[...].astype(BF16)

    d = D_MODEL
    gt_m = mod_ref[:, 2 * d:3 * d]
    sh_f = mod_ref[:, 3 * d:4 * d]
    sc_f = mod_ref[:, 4 * d:5 * d]
    wr = wr_ref[...].astype(BF16)
    for r in range(x_ref.shape[0] // SUB_ROWS):
        rows = slice(r * SUB_ROWS, (r + 1) * SUB_ROWS)
        y = jnp.dot(o_ref[rows, :], wbf_ref[...], preferred_element_type=F32) + b_ref[...]
        x1 = _layer_norm(DEEPNORM_ALPHA * x_ref[rows, :] + (1.0 + gt_m) * y, lng_ref[...], lnb_ref[...])
        x1_ref[rows, :] = x1
        h2 = x1 * (1.0 + sc_f) + sh_f
        _store_rows(h2_ref, h2, r * SUB_ROWS)
        logits = jnp.dot(h2.astype(BF16), wr, preferred_element_type=F32) + br_ref[...]
        ea, eb, wa, wb = _route(logits.T)
        ids_ref[0:1, rows] = ea
        ids_ref[1:2, rows] = eb
        wts_ref[0:1, rows] = wa
        wts_ref[1:2, rows] = wb


def _proj_ln_router(o, x2, mod_rows, layer, w, b, ln_g, ln_b, w_router, b_router, seq):
    t, k = o.shape
    tm = 1024
    steps_per_batch = seq // tm
    d = D_MODEL
    return pl.pallas_call(
        _proj_body,
        grid=(t // tm,),
        in_specs=[
            pl.BlockSpec((tm, k), lambda i: (i, 0)),
            pl.BlockSpec((tm, d), lambda i: (i, 0)),
            pl.BlockSpec((None, 1, N_MOD * d), lambda i: (layer * MOD_ROWS + i // steps_per_batch, 0, 0)),
            pl.BlockSpec((k, d), lambda i: (0, 0), pipeline_mode=pl.Buffered(1)),
            pl.BlockSpec((1, d), lambda i: (0, 0)),
            pl.BlockSpec((1, d), lambda i: (0, 0)),
            pl.BlockSpec((1, d), lambda i: (0, 0)),
            pl.BlockSpec((d, ROUTER_LANES), lambda i: (0, 0)),
            pl.BlockSpec((1, ROUTER_LANES), lambda i: (0, 0)),
        ],
        out_specs=[
            pl.BlockSpec((tm, d), lambda i: (i, 0)),
            pl.BlockSpec((tm * ROW_CHUNKS, LANES), lambda i: (i, 0)),
            pl.BlockSpec((2, tm), lambda i: (0, i)),
            pl.BlockSpec((2, tm), lambda i: (0, i)),
        ],
        out_shape=[
            jax.ShapeDtypeStruct((t, d), F32),
            jax.ShapeDtypeStruct((t * ROW_CHUNKS, LANES), U32),
            jax.ShapeDtypeStruct((2, t), I32),
            jax.ShapeDtypeStruct((2, t), F32),
        ],
        scratch_shapes=[pltpu.VMEM((k, d), BF16)],
        compiler_params=_cparams(1),
        name="proj_ln_router",
    )(o, x2, mod_rows, w, b.reshape(1, d), ln_g.reshape(1, d), ln_b.reshape(1, d), w_router, b_router)


def _sort_body(ids_ref, pos_ref, te_ref, nused_ref, tv_ref, rank_ref):
    n_rows = ids_ref.shape[0]
    c = SORT_CHUNK
    erow = lax.broadcasted_iota(I32, (N_EXPERTS, c), 0)
    tri = (lax.broadcasted_iota(I32, (c, c), 0) <= lax.broadcasted_iota(I32, (c, c), 1)).astype(BF16)

    def rank_step(r, carry):
        onehot = erow == ids_ref[pl.ds(r, 1), :]
        pref = jnp.dot(onehot.astype(BF16), tri, preferred_element_type=F32)
        rank = jnp.sum(jnp.where(onehot, pref + carry, 0.0), axis=0, keepdims=True) - 1.0
        rank_ref[pl.ds(r, 1), :] = rank
        return carry + pref[:, c - 1:c]

    counts = lax.fori_loop(0, n_rows, rank_step, jnp.zeros((N_EXPERTS, 1), F32), unroll=8)
    n_tile = jnp.floor((counts + (EXPERT_TILE - 1)) * (1.0 / EXPERT_TILE))
    low = (lax.broadcasted_iota(I32, (N_EXPERTS, N_EXPERTS), 1)
           <= lax.broadcasted_iota(I32, (N_EXPERTS, N_EXPERTS), 0)).astype(BF16)
    cum = jnp.dot(low, jnp.broadcast_to(n_tile, (N_EXPERTS, LANES)).astype(BF16),
                  preferred_element_type=F32)[:, 0:1]
    row_off = (cum - n_tile) * EXPERT_TILE

    def pos_step(r, _):
        onehot = erow == ids_ref[pl.ds(r, 1), :]
        off = jnp.sum(jnp.where(onehot, row_off, 0.0), axis=0, keepdims=True)
        pos_ref[pl.ds(r, 1), :] = (off + rank_ref[pl.ds(r, 1), :]).astype(I32)
        return 0

    lax.fori_loop(0, n_rows, pos_step, 0, unroll=8)
    total = jnp.max(cum, axis=0, keepdims=True)
    n_lanes = te_ref.shape[1]
    tile = jnp.minimum(lax.broadcasted_iota(I32, (N_EXPERTS, n_lanes), 1).astype(F32), total - 1.0)
    te_ref[...] = jnp.sum(jnp.where(cum <= tile, 1.0, 0.0), axis=0, keepdims=True).astype(I32)
    nused_ref[...] = jnp.broadcast_to(total, nused_ref.shape).astype(I32)
    tile_f = lax.broadcasted_iota(I32, (N_EXPERTS, n_lanes), 1).astype(F32)
    first = cum - n_tile
    rows_left = jnp.clip(counts - (tile_f - first) * EXPERT_TILE, 0.0, float(EXPERT_TILE))
    owns = (first <= tile_f) & (tile_f < cum)
    tv_ref[...] = jnp.sum(jnp.where(owns, rows_left, 0.0), axis=0, keepdims=True).astype(I32)


def _expert_sort(ids, n_tiles):
    n_assign = ids.shape[0] * ids.shape[1]
    n_rows = n_assign // SORT_CHUNK
    te_lanes = -(-n_tiles // LANES) * LANES
    pos, te, nused, tv = pl.pallas_call(
        _sort_body,
        grid=(1,),
        in_specs=[pl.BlockSpec((n_rows, SORT_CHUNK), lambda i: (0, 0))],
        out_specs=[
            pl.BlockSpec((n_rows, SORT_CHUNK), lambda i: (0, 0)),
            pl.BlockSpec((1, te_lanes), lambda i: (0, 0)),
            pl.BlockSpec((1, LANES), lambda i: (0, 0)),
            pl.BlockSpec((1, te_lanes), lambda i: (0, 0)),
        ],
        out_shape=[
            jax.ShapeDtypeStruct((n_rows, SORT_CHUNK), I32),
            jax.ShapeDtypeStruct((1, te_lanes), I32),
            jax.ShapeDtypeStruct((1, LANES), I32),
            jax.ShapeDtypeStruct((1, te_lanes), I32),
        ],
        scratch_shapes=[pltpu.VMEM((n_rows, SORT_CHUNK), F32)],
        compiler_params=_cparams(1),
        name="expert_sort",
    )(ids.reshape(n_rows, SORT_CHUNK))
    return pos, te[0, :n_tiles], nused[0, :1], tv[0, :n_tiles]


def _sc_mesh():
    return plsc.VectorSubcoreMesh(core_axis_name="c", subcore_axis_name="s", num_cores=SC_CORES,
                                  num_subcores=SC_WORKERS // SC_CORES)


def _sc_scatter_rows(src, pos2, n_rows):
    t = pos2.shape[1]
    src3 = src.reshape(t, ROW_CHUNKS, LANES)
    per_worker = t // SC_WORKERS
    n_chunks = per_worker // SC_CHUNK
    idx = pos2.reshape(2, SC_WORKERS, n_chunks, SC_CHUNK).transpose(1, 0, 2, 3)
    idx = idx.reshape(SC_WORKERS, 2 * n_chunks, SC_CHUNK)

    @functools.partial(
        pl.kernel, mesh=_sc_mesh(),
        out_type=jax.ShapeDtypeStruct((n_rows, ROW_CHUNKS, LANES), src.dtype),
        scratch_types=[
            pltpu.VMEM((2 * n_chunks, SC_CHUNK), I32),
            pltpu.VMEM((2, SC_CHUNK, ROW_CHUNKS, LANES), src.dtype),
            pltpu.SemaphoreType.DMA((2,)),
            pltpu.SemaphoreType.DMA((2,)),
        ],
        name="sc_scatter_rows",
    )
    def scatter(src_hbm, idx_hbm, out_hbm, idx_v, rows_v, rsem, wsem):
        wid = lax.axis_index("s") * SC_CORES + lax.axis_index("c")
        pltpu.sync_copy(idx_hbm.at[wid], idx_v)

        def read(j):
            b = j % 2
            return pltpu.async_copy(src_hbm.at[pl.ds(wid * per_worker + j * SC_CHUNK, SC_CHUNK)], rows_v.at[b],
                                    rsem.at[b])

        def write(j):
            b = j % 2
            return [pltpu.async_copy(rows_v.at[b], out_hbm.at[idx_v.at[k * n_chunks + j]], wsem.at[b])
                    for k in range(2)]

        reads = {0: read(0)}
        writes = {}
        for j in range(n_chunks):
            reads.pop(j).wait()
            if j + 1 < n_chunks:
                for cp in writes.pop(j - 1, []):
                    cp.wait()
                reads[j + 1] = read(j + 1)
            writes[j] = write(j)
        for cps in writes.values():
            for cp in cps:
                cp.wait()

    return scatter(src3, idx).reshape(n_rows * ROW_CHUNKS, LANES)


def _sc_gather_rows(table, idx):
    n = idx.shape[0]
    table3 = table.reshape(-1, ROW_CHUNKS, LANES)
    per_worker = n // SC_WORKERS
    n_chunks = per_worker // SC_CHUNK

    @functools.partial(
        pl.kernel, mesh=_sc_mesh(),
        out_type=jax.ShapeDtypeStruct((n, ROW_CHUNKS, LANES), table.dtype),
        scratch_types=[
            pltpu.VMEM((n_chunks, SC_CHUNK), I32),
            pltpu.VMEM((2, SC_CHUNK, ROW_CHUNKS, LANES), table.dtype),
            pltpu.SemaphoreType.DMA((2,)),
            pltpu.SemaphoreType.DMA((2,)),
        ],
        name="sc_gather_rows",
    )
    def gather(table_hbm, idx_hbm, out_hbm, idx_v, rows_v, rsem, wsem):
        wid = lax.axis_index("s") * SC_CORES + lax.axis_index("c")
        pltpu.sync_copy(idx_hbm.at[wid], idx_v)

        def read(j):
            b = j % 2
            return pltpu.async_copy(table_hbm.at[idx_v.at[j]], rows_v.at[b], rsem.at[b])

        def write(j):
            b = j % 2
            return pltpu.async_copy(rows_v.at[b], out_hbm.at[pl.ds(wid * per_worker + j * SC_CHUNK, SC_CHUNK)],
                                    wsem.at[b])

        reads = {0: read(0)}
        writes = {}
        for j in range(n_chunks):
            reads.pop(j).wait()
            if j + 1 < n_chunks:
                if j - 1 in writes:
                    writes.pop(j - 1).wait()
                reads[j + 1] = read(j + 1)
            writes[j] = write(j)
        for cp in writes.values():
            cp.wait()

    out = gather(table3, idx.reshape(SC_WORKERS, n_chunks, SC_CHUNK))
    return out.reshape(n * ROW_CHUNKS, LANES)


XS_SLOTS = 3


def _expert_body(te_ref, nused_ref, tv_ref, xs_hbm, wgu_ref, wd_ref, ys_ref, wgu_bf, wd_bf, xbuf, xsem):
    i = pl.program_id(0)
    n_used = nused_ref[0]
    used = i < n_used
    prev = te_ref[jnp.maximum(i - 1, 0)]
    tile_rows = EXPERT_TILE * ROW_CHUNKS

    def fetch(tile):
        slot = lax.rem(tile, XS_SLOTS)
        r0 = pl.multiple_of(tile * tile_rows, tile_rows)
        return pltpu.make_async_copy(xs_hbm.at[pl.ds(r0, tile_rows)], xbuf.at[slot], xsem.at[slot])

    @pl.when((i == 0) & (n_used > 0))
    def _():
        fetch(0).start()

    @pl.when((i == 0) & (n_used > 1))
    def _():
        fetch(1).start()

    @pl.when(i + 2 < n_used)
    def _():
        fetch(i + 2).start()

    @pl.when(used & ((i == 0) | (te_ref[i] != prev)))
    def _():
        wgu_bf[...] = wgu_ref[...].astype(BF16)
        wd_bf[...] = wd_ref[...].astype(BF16)

    @pl.when(used)
    def _():
        fetch(i).wait()
        live = lax.broadcasted_iota(I32, (EXPERT_TILE, 1), 0) < tv_ref[i]
        xs = jnp.where(live, _load_rows(xbuf.at[lax.rem(i, XS_SLOTS)], EXPERT_TILE), 0.0).astype(BF16)
        gu = jnp.dot(xs, wgu_bf[...], preferred_element_type=F32)
        gate = gu[:, :EXPERT_FF]
        up = gu[:, EXPERT_FF:]
        act = gate * jax.nn.sigmoid(gate) * up
        _store_rows(ys_ref, jnp.dot(act.astype(BF16), wd_bf[...], preferred_element_type=F32))

    @pl.when(jnp.logical_not(used))
    def _():
        ys_ref[...] = jnp.zeros(ys_ref.shape, ys_ref.dtype)


def _expert_mlp(tile_expert, n_used, tile_valid, xs, w_gate_up, w_down, layer):
    d = D_MODEL
    n_tiles = xs.shape[0] // (EXPERT_TILE * ROW_CHUNKS)
    f2 = 2 * EXPERT_FF
    tile_rows = EXPERT_TILE * ROW_CHUNKS
    e0 = layer * N_EXPERTS
    grid_spec = pltpu.PrefetchScalarGridSpec(
        num_scalar_prefetch=3,
        grid=(n_tiles,),
        in_specs=[
            pl.BlockSpec(memory_space=pl.ANY),
            pl.BlockSpec((None, d, f2), lambda i, te, nu, tv: (e0 + te[i], 0, 0)),
            pl.BlockSpec((None, EXPERT_FF, d), lambda i, te, nu, tv: (e0 + te[i], 0, 0)),
        ],
        out_specs=pl.BlockSpec((tile_rows, LANES), lambda i, te, nu, tv: (i, 0)),
        scratch_shapes=[pltpu.VMEM((d, f2), BF16), pltpu.VMEM((EXPERT_FF, d), BF16),
                        pltpu.VMEM((XS_SLOTS, tile_rows, LANES), U32), pltpu.SemaphoreType.DMA((XS_SLOTS,))],
    )
    return pl.pallas_call(
        _expert_body,
        grid_spec=grid_spec,
        out_shape=jax.ShapeDtypeStruct(xs.shape, U32),
        compiler_params=_cparams(1),
        name="expert_mlp",
    )(tile_expert, n_used, tile_valid, xs, w_gate_up.reshape(DEPTH * N_EXPERTS, d, f2),
      w_down.reshape(DEPTH * N_EXPERTS, EXPERT_FF, d))


def _moe_combine(wts_ref, x_ref, mod_ref, lng_ref, lnb_ref, ya_ref, yb_ref, token0, n):
    d = D_MODEL
    rows = slice(token0, token0 + n)
    cols = []
    for k in range(2):
        wt = jnp.broadcast_to(wts_ref[k:k + 1, rows], (LANES, n)).T
        cols.append(jnp.concatenate([wt] * (d // LANES), axis=1))
    y = cols[0] * _load_rows(ya_ref, n, token0) + cols[1] * _load_rows(yb_ref, n, token0)
    gt_f = mod_ref[:, 5 * d:6 * d]
    return _layer_norm(DEEPNORM_ALPHA * x_ref[rows, :] + (1.0 + gt_f) * y, lng_ref[...], lnb_ref[...])


def _combine_specs(tm, n_steps, steps_per_batch, layer):
    d = D_MODEL
    return [
        pl.BlockSpec((2, tm), lambda i: (0, i)),
        pl.BlockSpec((tm, d), lambda i: (i, 0)),
        pl.BlockSpec((None, 1, N_MOD * d), lambda i: (layer * MOD_ROWS + i // steps_per_batch, 0, 0)),
        pl.BlockSpec((1, d), lambda i: (0, 0)),
        pl.BlockSpec((1, d), lambda i: (0, 0)),
        pl.BlockSpec((tm * ROW_CHUNKS, LANES), lambda i: (i, 0)),
        pl.BlockSpec((tm * ROW_CHUNKS, LANES), lambda i: (n_steps + i, 0)),
    ]


def _combine_body(wts_ref, x_ref, mod_ref, lng_ref, lnb_ref, ya_ref, yb_ref, o_ref):
    for r in range(x_ref.shape[0] // SUB_ROWS):
        o_ref[r * SUB_ROWS:(r + 1) * SUB_ROWS, :] = _moe_combine(wts_ref, x_ref, mod_ref, lng_ref, lnb_ref, ya_ref,
                                                                 yb_ref, r * SUB_ROWS, SUB_ROWS)


def _combine_ln(moe, mod_rows, layer, ln_g, ln_b, seq):
    wts, x1, yg = moe
    t, d = x1.shape
    tm = 1024
    n_steps = t // tm
    return pl.pallas_call(
        _combine_body,
        grid=(n_steps,),
        in_specs=_combine_specs(tm, n_steps, seq // tm, layer),
        out_specs=pl.BlockSpec((tm, d), lambda i: (i, 0)),
        out_shape=jax.ShapeDtypeStruct((t, d), F32),
        compiler_params=_cparams(1),
        name="moe_combine_ln",
    )(wts, x1, mod_rows, ln_g.reshape(1, d), ln_b.reshape(1, d), yg, yg)


def _gmlp_body(wts_ref, x1_ref, modp_ref, lng_ref, lnb_ref, ya_ref, yb_ref,
               mod_ref, w_ref, b_ref, g_ref, beta_ref, ws_ref, bs_ref, x2_ref, o_ref, ws_bf, w_bf):
    @pl.when(pl.program_id(0) == 0)
    def _():
        tri = lax.broadcasted_iota(I32, (CHUNK, CHUNK), 0) >= lax.broadcasted_iota(I32, (CHUNK, CHUNK), 1)
        for g in range(N_SGU_GROUPS):
            ws_bf[g] = jnp.where(tri, ws_ref[g], 0.0).astype(BF16)
        w_bf[...] = w_ref[...].astype(BF16)

    d = D_MODEL
    sh = mod_ref[:, 0:d]
    sc = mod_ref[:, d:2 * d]
    sub_rows = x1_ref.shape[0]
    for r in range(x1_ref.shape[0] // sub_rows):
        r0 = r * sub_rows
        x2 = _moe_combine(wts_ref, x1_ref, modp_ref, lng_ref, lnb_ref, ya_ref, yb_ref, r0, sub_rows)
        x2_ref[r0:r0 + sub_rows, :] = x2
        h = x2 * (1.0 + sc) + sh
        z = jnp.dot(h.astype(BF16), w_bf[...], preferred_element_type=F32) + b_ref[...]
        z = 0.5 * z * (1.0 + lax.erf(z * (2.0 ** -0.5)))
        u = z[:, :GMLP_WIDTH]
        v = _layer_norm(z[:, GMLP_WIDTH:], g_ref[...], beta_ref[...]).astype(BF16)
        for ci in range(sub_rows // CHUNK):
            rows = slice(ci * CHUNK, (ci + 1) * CHUNK)
            out_rows = slice(r0 + ci * CHUNK, r0 + (ci + 1) * CHUNK)
            for g in range(N_SGU_GROUPS):
                lanes = slice(g * SGU_GROUP_DIM, (g + 1) * SGU_GROUP_DIM)
                mixed = jnp.dot(ws_bf[g], v[rows, lanes], preferred_element_type=F32) + bs_ref[:, g:g + 1]
                o_ref[out_rows, lanes] = (u[rows, lanes] * mixed).astype(BF16)


def _gmlp_gate(moe, ln_g, ln_b, mod_rows, layer, w_in, b_in, sgu_g, sgu_b, w_s, b_s, seq):
    wts, x1, yg = moe
    t, d = x1.shape
    tm = 512
    steps_per_batch = seq // tm
    n_steps = t // tm
    gw = GMLP_WIDTH
    return pl.pallas_call(
        _gmlp_body,
        grid=(n_steps,),
        in_specs=_combine_specs(tm, n_steps, steps_per_batch, layer - 1) + [
            pl.BlockSpec((None, 1, N_MOD * d), lambda i: (layer * MOD_ROWS + i // steps_per_batch, 0, 0)),
            pl.BlockSpec((d, 2 * gw), lambda i: (0, 0)),
            pl.BlockSpec((1, 2 * gw), lambda i: (0, 0)),
            pl.BlockSpec((1, gw), lambda i: (0, 0)),
            pl.BlockSpec((1, gw), lambda i: (0, 0)),
            pl.BlockSpec((N_SGU_GROUPS, CHUNK, CHUNK), lambda i: (0, 0, 0)),
            pl.BlockSpec((CHUNK, N_SGU_GROUPS), lambda i: (0, 0)),
        ],
        out_specs=[pl.BlockSpec((tm, d), lambda i: (i, 0)), pl.BlockSpec((tm, gw), lambda i: (i, 0))],
        out_shape=[jax.ShapeDtypeStruct((t, d), F32), jax.ShapeDtypeStruct((t, gw), BF16)],
        scratch_shapes=[pltpu.VMEM((N_SGU_GROUPS, CHUNK, CHUNK), BF16), pltpu.VMEM((d, 2 * gw), BF16)],
        compiler_params=_cparams(1),
        name="combine_gmlp_gate",
    )(wts, x1, mod_rows, ln_g.reshape(1, d), ln_b.reshape(1, d), yg, yg,
      mod_rows, w_in, b_in.reshape(1, 2 * gw), sgu_g.reshape(1, gw), sgu_b.reshape(1, gw), w_s, b_s.T)


def _router_params(w_group, b_group, w_expert, b_expert):
    def lanes(group_part, expert_part):
        rows = group_part.shape[0]
        gap = jnp.zeros((rows, EXPERT_COL0 - N_EXPERT_GROUPS), F32)
        tail = jnp.zeros((rows, ROUTER_LANES - EXPERT_COL0 - N_EXPERTS), F32)
        return jnp.concatenate([group_part, gap, expert_part, tail], axis=1)

    return lanes(w_group, w_expert), lanes(b_group[None, :], b_expert[None, :])


def _moe_experts(x1, h2, ids, wts, layer, w_gate_up, w_down):
    t = x1.shape[0]
    n_rows = 2 * t + N_EXPERTS * EXPERT_TILE
    n_tiles = n_rows // EXPERT_TILE
    pos, tile_expert, n_used, tile_valid = _expert_sort(ids, n_tiles)
    pos2 = pos.reshape(2, t)
    xs = _sc_scatter_rows(h2, pos2, n_rows)
    ys = _expert_mlp(tile_expert, n_used, tile_valid, xs, w_gate_up, w_down, layer)
    yg = _sc_gather_rows(ys, pos2.reshape(2 * t))
    return wts, x1, yg


def kernel(x, c, positions, ada_w, ada_b, post_ln_g, post_ln_b, attn_w_qkv, attn_b_qkv, attn_sinks, attn_w_o, attn_b_o, gmlp_w_in, gmlp_b_in, gmlp_sgu_ln_g, gmlp_sgu_ln_b, gmlp_w_s, gmlp_b_s, gmlp_w_out, gmlp_b_out, moe_w_group_router, moe_b_group_router, moe_w_expert_router, moe_b_expert_router, moe_w_gate_up, moe_w_down):
    batch, seq, d = x.shape
    t = batch * seq
    x2 = x.reshape(t, d)
    c_pad = jnp.pad(c, ((0, MOD_ROWS - batch), (0, 0)))
    mod_rows = _adaln_mod(c_pad, ada_w, ada_b).reshape(DEPTH * MOD_ROWS, 1, N_MOD * d)

    moe = None
    for layer in range(DEPTH):
        j = layer // 2
        if layer % 2 == 0:
            if moe is not None:
                x2 = _combine_ln(moe, mod_rows, layer - 1, post_ln_g[layer - 1, 1], post_ln_b[layer - 1, 1], seq)
            qkv = _qkv_rope(x2, mod_rows, layer, positions, attn_w_qkv[j], attn_b_qkv[j], seq)
            mix = _attention(qkv, attn_sinks[j], batch, seq)
            w_out, b_out = attn_w_o[j], attn_b_o[j]
        else:
            x2, mix = _gmlp_gate(moe, post_ln_g[layer - 1, 1], post_ln_b[layer - 1, 1], mod_rows, layer,
                                 gmlp_w_in[j], gmlp_b_in[j], gmlp_sgu_ln_g[j], gmlp_sgu_ln_b[j],
                                 gmlp_w_s[j], gmlp_b_s[j], seq)
            w_out, b_out = gmlp_w_out[j], gmlp_b_out[j]
        w_router, b_router = _router_params(moe_w_group_router[layer], moe_b_group_router[layer],
                                            moe_w_expert_router[layer], moe_b_expert_router[layer])
        x1, h2, ids, wts = _proj_ln_router(mix, x2, mod_rows, layer, w_out, b_out, post_ln_g[layer, 0],
                                           post_ln_b[layer, 0], w_router, b_router, seq)
        moe = _moe_experts(x1, h2, ids, wts, layer, moe_w_gate_up, moe_w_down)
    x2 = _combine_ln(moe, mod_rows, DEPTH - 1, post_ln_g[DEPTH - 1, 1], post_ln_b[DEPTH - 1, 1], seq)
    return x2.reshape(batch, seq, d)
```

```python
import functools

import jax
import jax.numpy as jnp
from jax import lax
from jax.experimental import pallas as pl
from jax.experimental.pallas import tpu as pltpu
from jax.experimental.pallas import tpu_sc as plsc

F32 = jnp.float32
BF16 = jnp.bfloat16
I32 = jnp.int32

D_MODEL = 1024
DEPTH = 2
HEAD_DIM = 64
N_Q_HEADS = 16
N_KV_HEADS = 4
GQA_GROUP = N_Q_HEADS // N_KV_HEADS
WINDOW = 128
ROPE_THETA = 10000.0
Q_WIDTH = N_Q_HEADS * HEAD_DIM
KV_WIDTH = N_KV_HEADS * HEAD_DIM
QKV_WIDTH = Q_WIDTH + 2 * KV_WIDTH
CHUNK = 128
GMLP_WIDTH = 2 * D_MODEL
N_SGU_GROUPS = 8
SGU_GROUP_DIM = GMLP_WIDTH // N_SGU_GROUPS
N_EXPERT_GROUPS = 4
EXPERTS_PER_GROUP = 8
N_EXPERTS = N_EXPERT_GROUPS * EXPERTS_PER_GROUP
EXPERT_FF = D_MODEL // 4
N_MOD = 6
DEEPNORM_ALPHA = (2.0 * DEPTH) ** 0.25
LN_EPS = 1e-5

LANES = 128
MOD_ROWS = 8
ROUTER_LANES = 128
EXPERT_COL0 = 8
SORT_CHUNK = 256
EXPERT_TILE = 512
SUB_ROWS = 256
VMEM_LIMIT = 56 * 1024 * 1024
SC_CORES = 2
SC_WORKERS = 32
SC_CHUNK = 64
SC_SLOTS = 3


def _cparams(n_axes, vmem=VMEM_LIMIT):
    return pltpu.CompilerParams(dimension_semantics=("arbitrary",) * n_axes, vmem_limit_bytes=vmem)


U32 = jnp.uint32
ROW_CHUNKS = D_MODEL // 2 // LANES


def _store_rows(ref, val, token0=0):
    n = val.shape[0]
    half = D_MODEL // 2
    lo = lax.bitcast_convert_type(val[:, :half].astype(BF16).astype(F32), U32)
    hi = lax.bitcast_convert_type(val[:, half:].astype(BF16).astype(F32), U32)
    words = (lo >> 16) | hi
    for c in range(ROW_CHUNKS):
        ref[pl.ds(token0 * ROW_CHUNKS + c, n, stride=ROW_CHUNKS), :] = words[:, c * LANES:(c + 1) * LANES]


def _load_rows(ref, n, token0=0):
    words = jnp.concatenate([ref[pl.ds(token0 * ROW_CHUNKS + c, n, stride=ROW_CHUNKS), :]
                             for c in range(ROW_CHUNKS)], axis=1)
    lo = lax.bitcast_convert_type(words << 16, F32)
    hi = lax.bitcast_convert_type(words & jnp.uint32(0xFFFF0000), F32)
    return jnp.concatenate([lo, hi], axis=1)


def _layer_norm(r, g, b):
    mu = jnp.mean(r, axis=-1, keepdims=True)
    d = r - mu
    var = jnp.mean(d * d, axis=-1, keepdims=True)
    return d * lax.rsqrt(var + LN_EPS) * g + b


def _mod_body(c_ref, w_ref, b_ref, o_ref):
    c = c_ref[...]
    ca = c * jax.nn.sigmoid(c)
    o_ref[...] = jnp.dot(ca.astype(BF16), w_ref[...].astype(BF16), preferred_element_type=F32) + b_ref[...]


def _adaln_mod(c_pad, ada_w, ada_b):
    tn = 1536
    n_out = N_MOD * D_MODEL
    return pl.pallas_call(
        _mod_body,
        grid=(DEPTH, n_out // tn),
        in_specs=[
            pl.BlockSpec((MOD_ROWS, D_MODEL), lambda l, j: (0, 0)),
            pl.BlockSpec((None, D_MODEL, tn), lambda l, j: (l, 0, j)),
            pl.BlockSpec((None, 1, tn), lambda l, j: (l, 0, j)),
        ],
        out_specs=pl.BlockSpec((None, MOD_ROWS, tn), lambda l, j: (l, 0, j)),
        out_shape=jax.ShapeDtypeStruct((DEPTH, MOD_ROWS, n_out), F32),
        compiler_params=_cparams(2),
        name="adaln_mod",
    )(c_pad, ada_w, ada_b.reshape(DEPTH, 1, n_out))


def _qkv_body(x_ref, mod_ref, pos_ref, w_ref, b_ref, invf_ref, o_ref, wbf_ref):
    @pl.when(pl.program_id(0) == 0)
    def _():
        wbf_ref[...] = w_ref[...].astype(BF16)

    tm = x_ref.shape[0]
    sh = mod_ref[:, 0:D_MODEL]
    sc = mod_ref[:, D_MODEL:2 * D_MODEL]
    h = x_ref[...] * (1.0 + sc) + sh
    qkv = jnp.dot(h.astype(BF16), wbf_ref[...], preferred_element_type=F32) + b_ref[...]

    ang = invf_ref[...] * pos_ref[...].astype(F32)
    c = jnp.cos(ang)
    s = jnp.sin(ang)
    ct = jnp.concatenate([c, c, c, c], axis=0).T
    st = jnp.concatenate([-s, s, -s, s], axis=0).T
    lane = lax.broadcasted_iota(I32, (tm, LANES), 1)
    first_half = (lane & (HEAD_DIM // 2)) == 0
    n_rope = (Q_WIDTH + KV_WIDTH) // LANES
    for j in range(n_rope):
        blk = qkv[:, j * LANES:(j + 1) * LANES]
        rot = jnp.where(first_half, pltpu.roll(blk, LANES - HEAD_DIM // 2, 1), pltpu.roll(blk, HEAD_DIM // 2, 1))
        r = blk * ct + rot * st
        if j < Q_WIDTH // LANES:
            r = r * (HEAD_DIM ** -0.5)
        o_ref[:, j * LANES:(j + 1) * LANES] = r.astype(BF16)
    o_ref[:, Q_WIDTH + KV_WIDTH:] = qkv[:, Q_WIDTH + KV_WIDTH:].astype(BF16)


def _qkv_rope(x2, mod_rows, layer, positions, w_qkv, b_qkv, seq):
    t = x2.shape[0]
    tm = 1024
    steps_per_batch = seq // tm
    inv_freq = ROPE_THETA ** (-jnp.arange(0, HEAD_DIM, 2, dtype=F32) / HEAD_DIM)
    return pl.pallas_call(
        _qkv_body,
        grid=(t // tm,),
        in_specs=[
            pl.BlockSpec((tm, D_MODEL), lambda i: (i, 0)),
            pl.BlockSpec((None, 1, N_MOD * D_MODEL), lambda i: (layer * MOD_ROWS + i // steps_per_batch, 0, 0)),
            pl.BlockSpec((None, 1, tm), lambda i: (i, 0, 0)),
            pl.BlockSpec((D_MODEL, QKV_WIDTH), lambda i: (0, 0)),
            pl.BlockSpec((1, QKV_WIDTH), lambda i: (0, 0)),
            pl.BlockSpec((HEAD_DIM // 2, 1), lambda i: (0, 0)),
        ],
        out_specs=pl.BlockSpec((tm, QKV_WIDTH), lambda i: (i, 0)),
        out_shape=jax.ShapeDtypeStruct((t, QKV_WIDTH), BF16),
        scratch_shapes=[pltpu.VMEM((D_MODEL, QKV_WIDTH), BF16)],
        compiler_params=_cparams(1),
        name="qkv_rope",
    )(x2, mod_rows, positions.reshape(t // tm, 1, tm), w_qkv, b_qkv.reshape(1, QKV_WIDTH),
      inv_freq.reshape(HEAD_DIM // 2, 1))


BF16_ROWS = 16


def _attn_prepare(kv, kab_ref, vx_ref, slot):
    kv = kv.astype(F32)
    low = lax.broadcasted_iota(I32, (WINDOW, LANES), 1) < HEAD_DIM
    ones = jnp.ones((WINDOW, LANES), F32)
    for g in range(N_KV_HEADS):
        for part, ref in ((0, None), (KV_WIDTH, vx_ref)):
            tile = kv[:, part + (g // 2) * LANES:part + (g // 2 + 1) * LANES]
            other = pltpu.roll(tile, HEAD_DIM, 1)
            in_low, in_high = (tile, other) if g % 2 == 0 else (other, tile)
            if ref is None:
                kab_ref[slot, 2 * g] = jnp.where(low, in_low, 0.0).astype(BF16)
                kab_ref[slot, 2 * g + 1] = jnp.where(low, 0.0, in_high).astype(BF16)
            else:
                both = jnp.where(low, in_low, in_high)
                vx_ref[slot, g] = jnp.concatenate([both, ones], axis=1).astype(BF16)


def _attn_block(sink_ref, q, kab_ref, vx_ref, s_ref, p_ref, prev, cur, first_block):
    for g in range(N_KV_HEADS):
        q_pair = jnp.concatenate([q[:, (2 * g) * LANES:(2 * g + 1) * LANES],
                                  q[:, (2 * g + 1) * LANES:(2 * g + 2) * LANES]], axis=0)
        for a in range(2):
            kband = jnp.concatenate([kab_ref[prev, 2 * g + a], kab_ref[cur, 2 * g + a]], axis=0)
            s = lax.dot_general(q_pair, kband, (((1,), (1,)), ((), ())), preferred_element_type=F32)
            s_ref[GQA_GROUP * g + a] = s[:WINDOW]
            s_ref[GQA_GROUP * g + 2 + a] = s[WINDOW:]
    qi = lax.broadcasted_iota(I32, (WINDOW, 2 * WINDOW), 0) + WINDOW
    kj = lax.broadcasted_iota(I32, (WINDOW, 2 * WINDOW), 1)
    mask = (kj <= qi) & (kj > qi - WINDOW) & ((kj >= WINDOW) | jnp.logical_not(first_block))
    key0 = lax.broadcasted_iota(I32, (1, 2 * WINDOW), 1) == 0
    for h in range(N_Q_HEADS):
        s = jnp.where(mask, s_ref[h], jnp.where(key0, sink_ref[h], -jnp.inf))
        m = jnp.max(s, axis=-1, keepdims=True)
        p_ref[h] = jnp.exp(s - m).astype(BF16)
    low = lax.broadcasted_iota(I32, (WINDOW, LANES), 1) < HEAD_DIM
    sink_row = ((lax.broadcasted_iota(I32, (BF16_ROWS, 2 * LANES), 0) == 0)
                & (lax.broadcasted_iota(I32, (BF16_ROWS, 2 * LANES), 1) < LANES))
    out_tiles = []
    for g in range(N_KV_HEADS):
        v_prev = vx_ref[prev, g]
        v_head = jnp.where(sink_row, 0.0, v_prev[:BF16_ROWS].astype(F32)).astype(BF16)
        vband = jnp.concatenate([v_head, v_prev[BF16_ROWS:], vx_ref[cur, g]], axis=0)
        p4 = p_ref[GQA_GROUP * g:GQA_GROUP * (g + 1)].reshape(GQA_GROUP * WINDOW, 2 * WINDOW)
        o4 = jnp.dot(p4, vband, preferred_element_type=F32)
        heads = []
        for j in range(GQA_GROUP):
            blk = o4[j * WINDOW:(j + 1) * WINDOW]
            heads.append(blk[:, :LANES] / blk[:, LANES:])
        out_tiles.append(jnp.where(low, heads[0], heads[1]))
        out_tiles.append(jnp.where(low, heads[2], heads[3]))
    return jnp.concatenate(out_tiles, axis=1).astype(BF16)


def _attn_body(sink_ref, q_ref, kv_ref, o_ref, kab_ref, vx_ref, s_ref, p_ref):
    n = pl.program_id(1)

    @pl.when(n == 0)
    def _():
        kab_ref[1] = jnp.zeros(kab_ref.shape[1:], kab_ref.dtype)
        half = (N_KV_HEADS, WINDOW, LANES)
        vx_ref[1] = jnp.concatenate([jnp.zeros(half, BF16), jnp.ones(half, BF16)], axis=-1)

    scratch = (kab_ref, vx_ref, s_ref, p_ref)
    for blk in range(q_ref.shape[0] // WINDOW):
        rows = slice(blk * WINDOW, (blk + 1) * WINDOW)
        cur = blk % 2
        _attn_prepare(kv_ref[rows, :], kab_ref, vx_ref, cur)
        o_ref[rows, :] = _attn_block(sink_ref, q_ref[rows, :], *scratch, 1 - cur, cur,
                                     (n == 0) if blk == 0 else False)


def _attention(qkv, sinks, batch, seq):
    t = qkv.shape[0]
    tq = 4 * WINDOW
    steps = seq // tq
    kv_col = Q_WIDTH // (2 * KV_WIDTH)
    return pl.pallas_call(
        _attn_body,
        grid=(batch, steps),
        in_specs=[
            pl.BlockSpec(memory_space=pltpu.SMEM),
            pl.BlockSpec((tq, Q_WIDTH), lambda b, n: (b * steps + n, 0)),
            pl.BlockSpec((tq, 2 * KV_WIDTH), lambda b, n: (b * steps + n, kv_col)),
        ],
        out_specs=pl.BlockSpec((tq, Q_WIDTH), lambda b, n: (b * steps + n, 0)),
        out_shape=jax.ShapeDtypeStruct((t, Q_WIDTH), BF16),
        scratch_shapes=[
            pltpu.VMEM((2, 2 * N_KV_HEADS, WINDOW, LANES), BF16),
            pltpu.VMEM((2, N_KV_HEADS, WINDOW, 2 * LANES), BF16),
            pltpu.VMEM((N_Q_HEADS, WINDOW, 2 * WINDOW), F32),
            pltpu.VMEM((N_Q_HEADS, WINDOW, 2 * WINDOW), BF16),
        ],
        compiler_params=_cparams(2),
        name="swa_attention",
    )(sinks, qkv, qkv)


def _route(lt):
    tm = lt.shape[1]
    row = lax.broadcasted_iota(I32, (EXPERTS_PER_GROUP, tm), 0)
    neg = -jnp.inf
    gl = jnp.where(row < N_EXPERT_GROUPS, lt[0:EXPERTS_PER_GROUP], neg)
    gm = jnp.max(gl, axis=0, keepdims=True)
    g_p = 1.0 / jnp.sum(jnp.exp(gl - gm), axis=0, keepdims=True)
    g_idx = jnp.min(jnp.where(gl == gm, row, EXPERTS_PER_GROUP), axis=0, keepdims=True)
    sel = lt[EXPERT_COL0 + (N_EXPERT_GROUPS - 1) * EXPERTS_PER_GROUP:EXPERT_COL0 + N_EXPERTS]
    for g in range(N_EXPERT_GROUPS - 2, -1, -1):
        lo = EXPERT_COL0 + g * EXPERTS_PER_GROUP
        sel = jnp.where(g_idx == g, lt[lo:lo + EXPERTS_PER_GROUP], sel)
    v1 = jnp.max(sel, axis=0, keepdims=True)
    i1 = jnp.min(jnp.where(sel == v1, row, EXPERTS_PER_GROUP), axis=0, keepdims=True)
    sel2 = jnp.where(row == i1, neg, sel)
    v2 = jnp.max(sel2, axis=0, keepdims=True)
    i2 = jnp.min(jnp.where(sel2 == v2, row, EXPERTS_PER_GROUP), axis=0, keepdims=True)
    e2 = jnp.exp(v2 - v1)
    w1 = g_p / (1.0 + e2)
    w2 = g_p * e2 / (1.0 + e2)
    base = g_idx * EXPERTS_PER_GROUP
    return base + i1, base + i2, w1, w2


def _proj_body(o_ref, x_ref, mod_ref, w_ref, b_ref, lng_ref, lnb_ref, wr_ref, br_ref,
               x1_ref, h2_ref, ids_ref, wts_ref, wbf_ref):
    @pl.when(pl.program_id(0) == 0)
    def _():
        wbf_ref[...] = w_ref[...].astype(BF16)

    d = D_MODEL
    gt_m = mod_ref[:, 2 * d:3 * d]
    sh_f = mod_ref[:, 3 * d:4 * d]
    sc_f = mod_ref[:, 4 * d:5 * d]
    wr = wr_ref[...].astype(BF16)
    for r in range(x_ref.shape[0] // SUB_ROWS):
        rows = slice(r * SUB_ROWS, (r + 1) * SUB_ROWS)
        y = jnp.dot(o_ref[rows, :], wbf_ref[...], preferred_element_type=F32) + b_ref[...]
        x1 = _layer_norm(DEEPNORM_ALPHA * x_ref[rows, :] + (1.0 + gt_m) * y, lng_ref[...], lnb_ref[...])
        x1_ref[rows, :] = x1
        h2 = x1 * (1.0 + sc_f) + sh_f
        _store_rows(h2_ref, h2, r * SUB_ROWS)
        logits = jnp.dot(h2.astype(BF16), wr, preferred_element_type=F32) + br_ref[...]
        ea, eb, wa, wb = _route(logits.T)
        ids_ref[0:1, rows] = ea
        ids_ref[1:2, rows] = eb
        wts_ref[0:1, rows] = wa
        wts_ref[1:2, rows] = wb


def _proj_ln_router(o, x2, mod_rows, layer, w, b, ln_g, ln_b, w_router, b_router, seq):
    t, k = o.shape
    tm = 1024
    steps_per_batch = seq // tm
    d = D_MODEL
    return pl.pallas_call(
        _proj_body,
        grid=(t // tm,),
        in_specs=[
            pl.BlockSpec((tm, k), lambda i: (i, 0)),
            pl.BlockSpec((tm, d), lambda i: (i, 0)),
            pl.BlockSpec((None, 1, N_MOD * d), lambda i: (layer * MOD_ROWS + i // steps_per_batch, 0, 0)),
            pl.BlockSpec((k, d), lambda i: (0, 0), pipeline_mode=pl.Buffered(1)),
            pl.BlockSpec((1, d), lambda i: (0, 0)),
            pl.BlockSpec((1, d), lambda i: (0, 0)),
            pl.BlockSpec((1, d), lambda i: (0, 0)),
            pl.BlockSpec((d, ROUTER_LANES), lambda i: (0, 0)),
            pl.BlockSpec((1, ROUTER_LANES), lambda i: (0, 0)),
        ],
        out_specs=[
            pl.BlockSpec((tm, d), lambda i: (i, 0)),
            pl.BlockSpec((tm * ROW_CHUNKS, LANES), lambda i: (i, 0)),
            pl.BlockSpec((2, tm), lambda i: (0, i)),
            pl.BlockSpec((2, tm), lambda i: (0, i)),
        ],
        out_shape=[
            jax.ShapeDtypeStruct((t, d), F32),
            jax.ShapeDtypeStruct((t * ROW_CHUNKS, LANES), U32),
            jax.ShapeDtypeStruct((2, t), I32),
            jax.ShapeDtypeStruct((2, t), F32),
        ],
        scratch_shapes=[pltpu.VMEM((k, d), BF16)],
        compiler_params=_cparams(1),
        name="proj_ln_router",
    )(o, x2, mod_rows, w, b.reshape(1, d), ln_g.reshape(1, d), ln_b.reshape(1, d), w_router, b_router)


def _sort_body(ids_ref, pos_ref, te_ref, nused_ref, tv_ref, rank_ref):
    n_rows = ids_ref.shape[0]
    c = SORT_CHUNK
    erow = lax.broadcasted_iota(I32, (N_EXPERTS, c), 0)
    tri = (lax.broadcasted_iota(I32, (c, c), 0) <= lax.broadcasted_iota(I32, (c, c), 1)).astype(BF16)

    def rank_step(r, carry):
        onehot = erow == ids_ref[pl.ds(r, 1), :]
        pref = jnp.dot(onehot.astype(BF16), tri, preferred_element_type=F32)
        rank = jnp.sum(jnp.where(onehot, pref + carry, 0.0), axis=0, keepdims=True) - 1.0
        rank_ref[pl.ds(r, 1), :] = rank
        return carry + pref[:, c - 1:c]

    counts = lax.fori_loop(0, n_rows, rank_step, jnp.zeros((N_EXPERTS, 1), F32), unroll=8)
    n_tile = jnp.floor((counts + (EXPERT_TILE - 1)) * (1.0 / EXPERT_TILE))
    low = (lax.broadcasted_iota(I32, (N_EXPERTS, N_EXPERTS), 1)
           <= lax.broadcasted_iota(I32, (N_EXPERTS, N_EXPERTS), 0)).astype(BF16)
    cum = jnp.dot(low, jnp.broadcast_to(n_tile, (N_EXPERTS, LANES)).astype(BF16),
                  preferred_element_type=F32)[:, 0:1]
    row_off = (cum - n_tile) * EXPERT_TILE

    def pos_step(r, _):
        onehot = erow == ids_ref[pl.ds(r, 1), :]
        off = jnp.sum(jnp.where(onehot, row_off, 0.0), axis=0, keepdims=True)
        pos_ref[pl.ds(r, 1), :] = (off + rank_ref[pl.ds(r, 1), :]).astype(I32)
        return 0

    lax.fori_loop(0, n_rows, pos_step, 0, unroll=8)
    total = jnp.max(cum, axis=0, keepdims=True)
    n_lanes = te_ref.shape[1]
    tile = jnp.minimum(lax.broadcasted_iota(I32, (N_EXPERTS, n_lanes), 1).astype(F32), total - 1.0)
    te_ref[...] = jnp.sum(jnp.where(cum <= tile, 1.0, 0.0), axis=0, keepdims=True).astype(I32)
    nused_ref[...] = jnp.broadcast_to(total, nused_ref.shape).astype(I32)
    tile_f = lax.broadcasted_iota(I32, (N_EXPERTS, n_lanes), 1).astype(F32)
    first = cum - n_tile
    rows_left = jnp.clip(counts - (tile_f - first) * EXPERT_TILE, 0.0, float(EXPERT_TILE))
    owns = (first <= tile_f) & (tile_f < cum)
    tv_ref[...] = jnp.sum(jnp.where(owns, rows_left, 0.0), axis=0, keepdims=True).astype(I32)


def _expert_sort(ids, n_tiles):
    n_assign = ids.shape[0] * ids.shape[1]
    n_rows = n_assign // SORT_CHUNK
    te_lanes = -(-n_tiles // LANES) * LANES
    pos, te, nused, tv = pl.pallas_call(
        _sort_body,
        grid=(1,),
        in_specs=[pl.BlockSpec((n_rows, SORT_CHUNK), lambda i: (0, 0))],
        out_specs=[
            pl.BlockSpec((n_rows, SORT_CHUNK), lambda i: (0, 0)),
            pl.BlockSpec((1, te_lanes), lambda i: (0, 0)),
            pl.BlockSpec((1, LANES), lambda i: (0, 0)),
            pl.BlockSpec((1, te_lanes), lambda i: (0, 0)),
        ],
        out_shape=[
            jax.ShapeDtypeStruct((n_rows, SORT_CHUNK), I32),
            jax.ShapeDtypeStruct((1, te_lanes), I32),
            jax.ShapeDtypeStruct((1, LANES), I32),
            jax.ShapeDtypeStruct((1, te_lanes), I32),
        ],
        scratch_shapes=[pltpu.VMEM((n_rows, SORT_CHUNK), F32)],
        compiler_params=_cparams(1),
        name="expert_sort",
    )(ids.reshape(n_rows, SORT_CHUNK))
    return pos, te[0, :n_tiles], nused[0, :1], tv[0, :n_tiles]


def _sc_mesh():
    return plsc.VectorSubcoreMesh(core_axis_name="c", subcore_axis_name="s", num_cores=SC_CORES,
                                  num_subcores=SC_WORKERS // SC_CORES)


def _sc_ring(n_chunks, read, write):
    reads, writes = {}, {}
    for j in range(min(SC_SLOTS - 1, n_chunks)):
        reads[j] = read(j)
    for j in range(n_chunks):
        for cp in reads.pop(j):
            cp.wait()
        nxt = j + SC_SLOTS - 1
        if nxt < n_chunks:
            for cp in writes.pop(nxt - SC_SLOTS, []):
                cp.wait()
            reads[nxt] = read(nxt)
        writes[j] = write(j)
    for cps in writes.values():
        for cp in cps:
            cp.wait()


def _sc_scatter_rows(src, pos2, n_rows):
    t = pos2.shape[1]
    src3 = src.reshape(t, ROW_CHUNKS, LANES)
    per_worker = t // SC_WORKERS
    n_chunks = per_worker // SC_CHUNK
    idx = pos2.reshape(2, SC_WORKERS, n_chunks, SC_CHUNK).transpose(1, 0, 2, 3)
    idx = idx.reshape(SC_WORKERS, 2 * n_chunks, SC_CHUNK)

    @functools.partial(
        pl.kernel, mesh=_sc_mesh(),
        out_type=jax.ShapeDtypeStruct((n_rows, ROW_CHUNKS, LANES), src.dtype),
        scratch_types=[
            pltpu.VMEM((2 * n_chunks, SC_CHUNK), I32),
            pltpu.VMEM((SC_SLOTS, SC_CHUNK, ROW_CHUNKS, LANES), src.dtype),
            pltpu.SemaphoreType.DMA((SC_SLOTS,)),
            pltpu.SemaphoreType.DMA((SC_SLOTS,)),
        ],
        name="sc_scatter_rows",
    )
    def scatter(src_hbm, idx_hbm, out_hbm, idx_v, rows_v, rsem, wsem):
        wid = lax.axis_index("s") * SC_CORES + lax.axis_index("c")
        pltpu.sync_copy(idx_hbm.at[wid], idx_v)

        def read(j):
            b = j % SC_SLOTS
            return [pltpu.async_copy(src_hbm.at[pl.ds(wid * per_worker + j * SC_CHUNK, SC_CHUNK)], rows_v.at[b],
                                     rsem.at[b])]

        def write(j):
            b = j % SC_SLOTS
            return [pltpu.async_copy(rows_v.at[b], out_hbm.at[idx_v.at[k * n_chunks + j]], wsem.at[b])
                    for k in range(2)]

        _sc_ring(n_chunks, read, write)

    return scatter(src3, idx).reshape(n_rows * ROW_CHUNKS, LANES)


def _sc_gather_rows(table, idx):
    n = idx.shape[0]
    table3 = table.reshape(-1, ROW_CHUNKS, LANES)
    per_worker = n // SC_WORKERS
    n_chunks = per_worker // SC_CHUNK

    @functools.partial(
        pl.kernel, mesh=_sc_mesh(),
        out_type=jax.ShapeDtypeStruct((n, ROW_CHUNKS, LANES), table.dtype),
        scratch_types=[
            pltpu.VMEM((n_chunks, SC_CHUNK), I32),
            pltpu.VMEM((SC_SLOTS, SC_CHUNK, ROW_CHUNKS, LANES), table.dtype),
            pltpu.SemaphoreType.DMA((SC_SLOTS,)),
            pltpu.SemaphoreType.DMA((SC_SLOTS,)),
        ],
        name="sc_gather_rows",
    )
    def gather(table_hbm, idx_hbm, out_hbm, idx_v, rows_v, rsem, wsem):
        wid = lax.axis_index("s") * SC_CORES + lax.axis_index("c")
        pltpu.sync_copy(idx_hbm.at[wid], idx_v)

        def read(j):
            b = j % SC_SLOTS
            return [pltpu.async_copy(table_hbm.at[idx_v.at[j]], rows_v.at[b], rsem.at[b])]

        def write(j):
            b = j % SC_SLOTS
            return [pltpu.async_copy(rows_v.at[b], out_hbm.at[pl.ds(wid * per_worker + j * SC_CHUNK, SC_CHUNK)],
                                     wsem.at[b])]

        _sc_ring(n_chunks, read, write)

    out = gather(table3, idx.reshape(SC_WORKERS, n_chunks, SC_CHUNK))
    return out.reshape(n * ROW_CHUNKS, LANES)


XS_SLOTS = 3


def _expert_body(te_ref, nused_ref, tv_ref, xs_hbm, wgu_ref, wd_ref, ys_ref, wgu_bf, wd_bf, xbuf, xsem):
    i = pl.program_id(0)
    n_used = nused_ref[0]
    used = i < n_used
    prev = te_ref[jnp.maximum(i - 1, 0)]
    tile_rows = EXPERT_TILE * ROW_CHUNKS

    def fetch(tile):
        slot = lax.rem(tile, XS_SLOTS)
        r0 = pl.multiple_of(tile * tile_rows, tile_rows)
        return pltpu.make_async_copy(xs_hbm.at[pl.ds(r0, tile_rows)], xbuf.at[slot], xsem.at[slot])

    @pl.when((i == 0) & (n_used > 0))
    def _():
        fetch(0).start()

    @pl.when((i == 0) & (n_used > 1))
    def _():
        fetch(1).start()

    @pl.when(i + 2 < n_used)
    def _():
        fetch(i + 2).start()

    @pl.when(used & ((i == 0) | (te_ref[i] != prev)))
    def _():
        wgu_bf[...] = wgu_ref[...].astype(BF16)
        wd_bf[...] = wd_ref[...].astype(BF16)

    @pl.when(used)
    def _():
        fetch(i).wait()
        live = lax.broadcasted_iota(I32, (EXPERT_TILE, 1), 0) < tv_ref[i]
        xs = jnp.where(live, _load_rows(xbuf.at[lax.rem(i, XS_SLOTS)], EXPERT_TILE), 0.0).astype(BF16)
        gu = jnp.dot(xs, wgu_bf[...], preferred_element_type=F32)
        gate = gu[:, :EXPERT_FF]
        up = gu[:, EXPERT_FF:]
        act = gate * jax.nn.sigmoid(gate) * up
        _store_rows(ys_ref, jnp.dot(act.astype(BF16), wd_bf[...], preferred_element_type=F32))

    @pl.when(jnp.logical_not(used))
    def _():
        ys_ref[...] = jnp.zeros(ys_ref.shape, ys_ref.dtype)


def _expert_mlp(tile_expert, n_used, tile_valid, xs, w_gate_up, w_down, layer):
    d = D_MODEL
    n_tiles = xs.shape[0] // (EXPERT_TILE * ROW_CHUNKS)
    f2 = 2 * EXPERT_FF
    tile_rows = EXPERT_TILE * ROW_CHUNKS
    e0 = layer * N_EXPERTS
    grid_spec = pltpu.PrefetchScalarGridSpec(
        num_scalar_prefetch=3,
        grid=(n_tiles,),
        in_specs=[
            pl.BlockSpec(memory_space=pl.ANY),
            pl.BlockSpec((None, d, f2), lambda i, te, nu, tv: (e0 + te[i], 0, 0)),
            pl.BlockSpec((None, EXPERT_FF, d), lambda i, te, nu, tv: (e0 + te[i], 0, 0)),
        ],
        out_specs=pl.BlockSpec((tile_rows, LANES), lambda i, te, nu, tv: (i, 0)),
        scratch_shapes=[pltpu.VMEM((d, f2), BF16), pltpu.VMEM((EXPERT_FF, d), BF16),
                        pltpu.VMEM((XS_SLOTS, tile_rows, LANES), U32), pltpu.SemaphoreType.DMA((XS_SLOTS,))],
    )
    return pl.pallas_call(
        _expert_body,
        grid_spec=grid_spec,
        out_shape=jax.ShapeDtypeStruct(xs.shape, U32),
        compiler_params=_cparams(1),
        name="expert_mlp",
    )(tile_expert, n_used, tile_valid, xs, w_gate_up.reshape(DEPTH * N_EXPERTS, d, f2),
      w_down.reshape(DEPTH * N_EXPERTS, EXPERT_FF, d))


def _moe_combine(wts_ref, x_ref, mod_ref, lng_ref, lnb_ref, ya_ref, yb_ref, token0, n):
    d = D_MODEL
    rows = slice(token0, token0 + n)
    cols = []
    for k in range(2):
        wt = jnp.broadcast_to(wts_ref[k:k + 1, rows], (LANES, n)).T
        cols.append(jnp.concatenate([wt] * (d // LANES), axis=1))
    y = cols[0] * _load_rows(ya_ref, n, token0) + cols[1] * _load_rows(yb_ref, n, token0)
    gt_f = mod_ref[:, 5 * d:6 * d]
    return _layer_norm(DEEPNORM_ALPHA * x_ref[rows, :] + (1.0 + gt_f) * y, lng_ref[...], lnb_ref[...])


def _combine_specs(tm, n_steps, steps_per_batch, layer):
    d = D_MODEL
    return [
        pl.BlockSpec((2, tm), lambda i: (0, i)),
        pl.BlockSpec((tm, d), lambda i: (i, 0)),
        pl.BlockSpec((None, 1, N_MOD * d), lambda i: (layer * MOD_ROWS + i // steps_per_batch, 0, 0)),
        pl.BlockSpec((1, d), lambda i: (0, 0)),
        pl.BlockSpec((1, d), lambda i: (0, 0)),
        pl.BlockSpec((tm * ROW_CHUNKS, LANES), lambda i: (i, 0)),
        pl.BlockSpec((tm * ROW_CHUNKS, LANES), lambda i: (n_steps + i, 0)),
    ]


def _combine_body(wts_ref, x_ref, mod_ref, lng_ref, lnb_ref, ya_ref, yb_ref, o_ref):
    for r in range(x_ref.shape[0] // SUB_ROWS):
        o_ref[r * SUB_ROWS:(r + 1) * SUB_ROWS, :] = _moe_combine(wts_ref, x_ref, mod_ref, lng_ref, lnb_ref, ya_ref,
                                                                 yb_ref, r * SUB_ROWS, SUB_ROWS)


def _combine_ln(moe, mod_rows, layer, ln_g, ln_b, seq):
    wts, x1, yg = moe
    t, d = x1.shape
    tm = 1024
    n_steps = t // tm
    return pl.pallas_call(
        _combine_body,
        grid=(n_steps,),
        in_specs=_combine_specs(tm, n_steps, seq // tm, layer),
        out_specs=pl.BlockSpec((tm, d), lambda i: (i, 0)),
        out_shape=jax.ShapeDtypeStruct((t, d), F32),
        compiler_params=_cparams(1),
        name="moe_combine_ln",
    )(wts, x1, mod_rows, ln_g.reshape(1, d), ln_b.reshape(1, d), yg, yg)


def _gmlp_body(wts_ref, x1_ref, modp_ref, lng_ref, lnb_ref, ya_ref, yb_ref,
               mod_ref, w_ref, b_ref, g_ref, beta_ref, ws_ref, bs_ref, x2_ref, o_ref, ws_bf, w_bf):
    @pl.when(pl.program_id(0) == 0)
    def _():
        tri = lax.broadcasted_iota(I32, (CHUNK, CHUNK), 0) >= lax.broadcasted_iota(I32, (CHUNK, CHUNK), 1)
        for g in range(N_SGU_GROUPS):
            ws_bf[g] = jnp.where(tri, ws_ref[g], 0.0).astype(BF16)
        w_bf[...] = w_ref[...].astype(BF16)

    d = D_MODEL
    sh = mod_ref[:, 0:d]
    sc = mod_ref[:, d:2 * d]
    sub_rows = x1_ref.shape[0]
    for r in range(x1_ref.shape[0] // sub_rows):
        r0 = r * sub_rows
        x2 = _moe_combine(wts_ref, x1_ref, modp_ref, lng_ref, lnb_ref, ya_ref, yb_ref, r0, sub_rows)
        x2_ref[r0:r0 + sub_rows, :] = x2
        h = x2 * (1.0 + sc) + sh
        z = jnp.dot(h.astype(BF16), w_bf[...], preferred_element_type=F32) + b_ref[...]
        z = 0.5 * z * (1.0 + lax.erf(z * (2.0 ** -0.5)))
        u = z[:, :GMLP_WIDTH]
        v = _layer_norm(z[:, GMLP_WIDTH:], g_ref[...], beta_ref[...]).astype(BF16)
        for ci in range(sub_rows // CHUNK):
            rows = slice(ci * CHUNK, (ci + 1) * CHUNK)
            out_rows = slice(r0 + ci * CHUNK, r0 + (ci + 1) * CHUNK)
            for g in range(N_SGU_GROUPS):
                lanes = slice(g * SGU_GROUP_DIM, (g + 1) * SGU_GROUP_DIM)
                mixed = jnp.dot(ws_bf[g], v[rows, lanes], preferred_element_type=F32) + bs_ref[:, g:g + 1]
                o_ref[out_rows, lanes] = (u[rows, lanes] * mixed).astype(BF16)


def _gmlp_gate(moe, ln_g, ln_b, mod_rows, layer, w_in, b_in, sgu_g, sgu_b, w_s, b_s, seq):
    wts, x1, yg = moe
    t, d = x1.shape
    tm = 512
    steps_per_batch = seq // tm
    n_steps = t // tm
    gw = GMLP_WIDTH
    return pl.pallas_call(
        _gmlp_body,
        grid=(n_steps,),
        in_specs=_combine_specs(tm, n_steps, steps_per_batch, layer - 1) + [
            pl.BlockSpec((None, 1, N_MOD * d), lambda i: (layer * MOD_ROWS + i // steps_per_batch, 0, 0)),
            pl.BlockSpec((d, 2 * gw), lambda i: (0, 0)),
            pl.BlockSpec((1, 2 * gw), lambda i: (0, 0)),
            pl.BlockSpec((1, gw), lambda i: (0, 0)),
            pl.BlockSpec((1, gw), lambda i: (0, 0)),
            pl.BlockSpec((N_SGU_GROUPS, CHUNK, CHUNK), lambda i: (0, 0, 0)),
            pl.BlockSpec((CHUNK, N_SGU_GROUPS), lambda i: (0, 0)),
        ],
        out_specs=[pl.BlockSpec((tm, d), lambda i: (i, 0)), pl.BlockSpec((tm, gw), lambda i: (i, 0))],
        out_shape=[jax.ShapeDtypeStruct((t, d), F32), jax.ShapeDtypeStruct((t, gw), BF16)],
        scratch_shapes=[pltpu.VMEM((N_SGU_GROUPS, CHUNK, CHUNK), BF16), pltpu.VMEM((d, 2 * gw), BF16)],
        compiler_params=_cparams(1),
        name="combine_gmlp_gate",
    )(wts, x1, mod_rows, ln_g.reshape(1, d), ln_b.reshape(1, d), yg, yg,
      mod_rows, w_in, b_in.reshape(1, 2 * gw), sgu_g.reshape(1, gw), sgu_b.reshape(1, gw), w_s, b_s.T)


def _router_params(w_group, b_group, w_expert, b_expert):
    def lanes(group_part, expert_part):
        rows = group_part.shape[0]
        gap = jnp.zeros((rows, EXPERT_COL0 - N_EXPERT_GROUPS), F32)
        tail = jnp.zeros((rows, ROUTER_LANES - EXPERT_COL0 - N_EXPERTS), F32)
        return jnp.concatenate([group_part, gap, expert_part, tail], axis=1)

    return lanes(w_group, w_expert), lanes(b_group[None, :], b_expert[None, :])


def _moe_experts(x1, h2, ids, wts, layer, w_gate_up, w_down):
    t = x1.shape[0]
    n_rows = 2 * t + N_EXPERTS * EXPERT_TILE
    n_tiles = n_rows // EXPERT_TILE
    pos, tile_expert, n_used, tile_valid = _expert_sort(ids, n_tiles)
    pos2 = pos.reshape(2, t)
    xs = _sc_scatter_rows(h2, pos2, n_rows)
    ys = _expert_mlp(tile_expert, n_used, tile_valid, xs, w_gate_up, w_down, layer)
    yg = _sc_gather_rows(ys, pos2.reshape(2 * t))
    return wts, x1, yg


def kernel(x, c, positions, ada_w, ada_b, post_ln_g, post_ln_b, attn_w_qkv, attn_b_qkv, attn_sinks, attn_w_o, attn_b_o, gmlp_w_in, gmlp_b_in, gmlp_sgu_ln_g, gmlp_sgu_ln_b, gmlp_w_s, gmlp_b_s, gmlp_w_out, gmlp_b_out, moe_w_group_router, moe_b_group_router, moe_w_expert_router, moe_b_expert_router, moe_w_gate_up, moe_w_down):
    batch, seq, d = x.shape
    t = batch * seq
    x2 = x.reshape(t, d)
    c_pad = jnp.pad(c, ((0, MOD_ROWS - batch), (0, 0)))
    mod_rows = _adaln_mod(c_pad, ada_w, ada_b).reshape(DEPTH * MOD_ROWS, 1, N_MOD * d)

    moe = None
    for layer in range(DEPTH):
        j = layer // 2
        if layer % 2 == 0:
            if moe is not None:
                x2 = _combine_ln(moe, mod_rows, layer - 1, post_ln_g[layer - 1, 1], post_ln_b[layer - 1, 1], seq)
            qkv = _qkv_rope(x2, mod_rows, layer, positions, attn_w_qkv[j], attn_b_qkv[j], seq)
            mix = _attention(qkv, attn_sinks[j], batch, seq)
            w_out, b_out = attn_w_o[j], attn_b_o[j]
        else:
            x2, mix = _gmlp_gate(moe, post_ln_g[layer - 1, 1], post_ln_b[layer - 1, 1], mod_rows, layer,
                                 gmlp_w_in[j], gmlp_b_in[j], gmlp_sgu_ln_g[j], gmlp_sgu_ln_b[j],
                                 gmlp_w_s[j], gmlp_b_s[j], seq)
            w_out, b_out = gmlp_w_out[j], gmlp_b_out[j]
        w_router, b_router = _router_params(moe_w_group_router[layer], moe_b_group_router[layer],
                                            moe_w_expert_router[layer], moe_b_expert_router[layer])
        x1, h2, ids, wts = _proj_ln_router(mix, x2, mod_rows, layer, w_out, b_out, post_ln_g[layer, 0],
                                           post_ln_b[layer, 0], w_router, b_router, seq)
        moe = _moe_experts(x1, h2, ids, wts, layer, moe_w_gate_up, moe_w_down)
    x2 = _combine_ln(moe, mod_rows, DEPTH - 1, post_ln_g[DEPTH - 1, 1], post_ln_b[DEPTH - 1, 1], seq)
    return x2.reshape(batch, seq, d)
```

```python
import functools

import jax
import jax.numpy as jnp
from jax import lax
from jax.experimental import pallas as pl
from jax.experimental.pallas import tpu as pltpu
from jax.experimental.pallas import tpu_sc as plsc

F32 = jnp.float32
BF16 = jnp.bfloat16
I32 = jnp.int32

D_MODEL = 1024
DEPTH = 2
HEAD_DIM = 64
N_Q_HEADS = 16
N_KV_HEADS = 4
GQA_GROUP = N_Q_HEADS // N_KV_HEADS
WINDOW = 128
ROPE_THETA = 10000.0
Q_WIDTH = N_Q_HEADS * HEAD_DIM
KV_WIDTH = N_KV_HEADS * HEAD_DIM
QKV_WIDTH = Q_WIDTH + 2 * KV_WIDTH
CHUNK = 128
GMLP_WIDTH = 2 * D_MODEL
N_SGU_GROUPS = 8
SGU_GROUP_DIM = GMLP_WIDTH // N_SGU_GROUPS
N_EXPERT_GROUPS = 4
EXPERTS_PER_GROUP = 8
N_EXPERTS = N_EXPERT_GROUPS * EXPERTS_PER_GROUP
EXPERT_FF = D_MODEL // 4
N_MOD = 6
DEEPNORM_ALPHA = (2.0 * DEPTH) ** 0.25
LN_EPS = 1e-5

LANES = 128
MOD_ROWS = 8
ROUTER_LANES = 128
EXPERT_COL0 = 8
SORT_CHUNK = 256
EXPERT_TILE = 512
SUB_ROWS = 256
VMEM_LIMIT = 56 * 1024 * 1024
SC_CORES = 2
SC_WORKERS = 32
SC_CHUNK = 64
SC_SLOTS = 3


def _cparams(n_axes, vmem=VMEM_LIMIT):
    return pltpu.CompilerParams(dimension_semantics=("arbitrary",) * n_axes, vmem_limit_bytes=vmem)


U32 = jnp.uint32
ROW_CHUNKS = D_MODEL // 2 // LANES


def _store_rows(ref, val, token0=0):
    n = val.shape[0]
    half = D_MODEL // 2
    words = pltpu.pack_elementwise([val[:, :half], val[:, half:]], packed_dtype=BF16)
    for c in range(ROW_CHUNKS):
        ref[pl.ds(token0 * ROW_CHUNKS + c, n, stride=ROW_CHUNKS), :] = words[:, c * LANES:(c + 1) * LANES]


def _load_rows(ref, n, token0=0):
    words = jnp.concatenate([ref[pl.ds(token0 * ROW_CHUNKS + c, n, stride=ROW_CHUNKS), :]
                             for c in range(ROW_CHUNKS)], axis=1)
    lo = lax.bitcast_convert_type(words << 16, F32)
    hi = lax.bitcast_convert_type(words & jnp.uint32(0xFFFF0000), F32)
    return jnp.concatenate([lo, hi], axis=1)


def _layer_norm(r, g, b):
    mu = jnp.mean(r, axis=-1, keepdims=True)
    d = r - mu
    var = jnp.mean(d * d, axis=-1, keepdims=True)
    return d * lax.rsqrt(var + LN_EPS) * g + b


def _mod_body(c_ref, w_ref, b_ref, o_ref):
    c = c_ref[...]
    ca = c * jax.nn.sigmoid(c)
    o_ref[...] = jnp.dot(ca.astype(BF16), w_ref[...].astype(BF16), preferred_element_type=F32) + b_ref[...]


def _adaln_mod(c_pad, ada_w, ada_b):
    tn = 1536
    n_out = N_MOD * D_MODEL
    return pl.pallas_call(
        _mod_body,
        grid=(DEPTH, n_out // tn),
        in_specs=[
            pl.BlockSpec((MOD_ROWS, D_MODEL), lambda l, j: (0, 0)),
            pl.BlockSpec((None, D_MODEL, tn), lambda l, j: (l, 0, j)),
            pl.BlockSpec((None, 1, tn), lambda l, j: (l, 0, j)),
        ],
        out_specs=pl.BlockSpec((None, MOD_ROWS, tn), lambda l, j: (l, 0, j)),
        out_shape=jax.ShapeDtypeStruct((DEPTH, MOD_ROWS, n_out), F32),
        compiler_params=_cparams(2),
        name="adaln_mod",
    )(c_pad, ada_w, ada_b.reshape(DEPTH, 1, n_out))


def _qkv_body(x_ref, mod_ref, pos_ref, w_ref, b_ref, invf_ref, o_ref, wbf_ref):
    @pl.when(pl.program_id(0) == 0)
    def _():
        wbf_ref[...] = w_ref[...].astype(BF16)

    tm = x_ref.shape[0]
    sh = mod_ref[:, 0:D_MODEL]
    sc = mod_ref[:, D_MODEL:2 * D_MODEL]
    h = x_ref[...] * (1.0 + sc) + sh
    qkv = jnp.dot(h.astype(BF16), wbf_ref[...], preferred_element_type=F32) + b_ref[...]

    ang = invf_ref[...] * pos_ref[...].astype(F32)
    c = jnp.cos(ang)
    s = jnp.sin(ang)
    ct = jnp.concatenate([c, c, c, c], axis=0).T
    st = jnp.concatenate([-s, s, -s, s], axis=0).T
    lane = lax.broadcasted_iota(I32, (tm, LANES), 1)
    first_half = (lane & (HEAD_DIM // 2)) == 0
    n_rope = (Q_WIDTH + KV_WIDTH) // LANES
    for j in range(n_rope):
        blk = qkv[:, j * LANES:(j + 1) * LANES]
        rot = jnp.where(first_half, pltpu.roll(blk, LANES - HEAD_DIM // 2, 1), pltpu.roll(blk, HEAD_DIM // 2, 1))
        r = blk * ct + rot * st
        if j < Q_WIDTH // LANES:
            r = r * (HEAD_DIM ** -0.5)
        o_ref[:, j * LANES:(j + 1) * LANES] = r.astype(BF16)
    o_ref[:, Q_WIDTH + KV_WIDTH:] = qkv[:, Q_WIDTH + KV_WIDTH:].astype(BF16)


def _qkv_rope(x2, mod_rows, layer, positions, w_qkv, b_qkv, seq):
    t = x2.shape[0]
    tm = 1024
    steps_per_batch = seq // tm
    inv_freq = ROPE_THETA ** (-jnp.arange(0, HEAD_DIM, 2, dtype=F32) / HEAD_DIM)
    return pl.pallas_call(
        _qkv_body,
        grid=(t // tm,),
        in_specs=[
            pl.BlockSpec((tm, D_MODEL), lambda i: (i, 0)),
            pl.BlockSpec((None, 1, N_MOD * D_MODEL), lambda i: (layer * MOD_ROWS + i // steps_per_batch, 0, 0)),
            pl.BlockSpec((None, 1, tm), lambda i: (i, 0, 0)),
            pl.BlockSpec((D_MODEL, QKV_WIDTH), lambda i: (0, 0)),
            pl.BlockSpec((1, QKV_WIDTH), lambda i: (0, 0)),
            pl.BlockSpec((HEAD_DIM // 2, 1), lambda i: (0, 0)),
        ],
        out_specs=pl.BlockSpec((tm, QKV_WIDTH), lambda i: (i, 0)),
        out_shape=jax.ShapeDtypeStruct((t, QKV_WIDTH), BF16),
        scratch_shapes=[pltpu.VMEM((D_MODEL, QKV_WIDTH), BF16)],
        compiler_params=_cparams(1),
        name="qkv_rope",
    )(x2, mod_rows, positions.reshape(t // tm, 1, tm), w_qkv, b_qkv.reshape(1, QKV_WIDTH),
      inv_freq.reshape(HEAD_DIM // 2, 1))


BF16_ROWS = 16


def _attn_prepare(kv, kab_ref, vx_ref, slot):
    kv = kv.astype(F32)
    low = lax.broadcasted_iota(I32, (WINDOW, LANES), 1) < HEAD_DIM
    ones = jnp.ones((WINDOW, LANES), F32)
    for g in range(N_KV_HEADS):
        for part, ref in ((0, None), (KV_WIDTH, vx_ref)):
            tile = kv[:, part + (g // 2) * LANES:part + (g // 2 + 1) * LANES]
            other = pltpu.roll(tile, HEAD_DIM, 1)
            in_low, in_high = (tile, other) if g % 2 == 0 else (other, tile)
            if ref is None:
                kab_ref[slot, 2 * g] = jnp.where(low, in_low, 0.0).astype(BF16)
                kab_ref[slot, 2 * g + 1] = jnp.where(low, 0.0, in_high).astype(BF16)
            else:
                both = jnp.where(low, in_low, in_high)
                vx_ref[slot, g] = jnp.concatenate([both, ones], axis=1).astype(BF16)


def _attn_block(sink_ref, q, kab_ref, vx_ref, s_ref, p_ref, prev, cur, first_block):
    for g in range(N_KV_HEADS):
        q_pair = jnp.concatenate([q[:, (2 * g) * LANES:(2 * g + 1) * LANES],
                                  q[:, (2 * g + 1) * LANES:(2 * g + 2) * LANES]], axis=0)
        for a in range(2):
            kband = jnp.concatenate([kab_ref[prev, 2 * g + a], kab_ref[cur, 2 * g + a]], axis=0)
            s = lax.dot_general(q_pair, kband, (((1,), (1,)), ((), ())), preferred_element_type=F32)
            s_ref[GQA_GROUP * g + a] = s[:WINDOW]
            s_ref[GQA_GROUP * g + 2 + a] = s[WINDOW:]
    qi = lax.broadcasted_iota(I32, (WINDOW, 2 * WINDOW), 0) + WINDOW
    kj = lax.broadcasted_iota(I32, (WINDOW, 2 * WINDOW), 1)
    mask = (kj <= qi) & (kj > qi - WINDOW) & ((kj >= WINDOW) | jnp.logical_not(first_block))
    key0 = lax.broadcasted_iota(I32, (1, 2 * WINDOW), 1) == 0
    for h in range(N_Q_HEADS):
        s = jnp.where(mask, s_ref[h], jnp.where(key0, sink_ref[h], -jnp.inf))
        m = jnp.max(s, axis=-1, keepdims=True)
        p_ref[h] = jnp.exp(s - m).astype(BF16)
    low = lax.broadcasted_iota(I32, (WINDOW, LANES), 1) < HEAD_DIM
    sink_row = ((lax.broadcasted_iota(I32, (BF16_ROWS, 2 * LANES), 0) == 0)
                & (lax.broadcasted_iota(I32, (BF16_ROWS, 2 * LANES), 1) < LANES))
    out_tiles = []
    for g in range(N_KV_HEADS):
        v_prev = vx_ref[prev, g]
        v_head = jnp.where(sink_row, 0.0, v_prev[:BF16_ROWS].astype(F32)).astype(BF16)
        vband = jnp.concatenate([v_head, v_prev[BF16_ROWS:], vx_ref[cur, g]], axis=0)
        p4 = p_ref[GQA_GROUP * g:GQA_GROUP * (g + 1)].reshape(GQA_GROUP * WINDOW, 2 * WINDOW)
        o4 = jnp.dot(p4, vband, preferred_element_type=F32)
        heads = []
        for j in range(GQA_GROUP):
            blk = o4[j * WINDOW:(j + 1) * WINDOW]
            heads.append(blk[:, :LANES] / blk[:, LANES:])
        out_tiles.append(jnp.where(low, heads[0], heads[1]))
        out_tiles.append(jnp.where(low, heads[2], heads[3]))
    return jnp.concatenate(out_tiles, axis=1).astype(BF16)


def _attn_body(sink_ref, q_ref, kv_ref, o_ref, kab_ref, vx_ref, s_ref, p_ref):
    n = pl.program_id(1)

    @pl.when(n == 0)
    def _():
        kab_ref[1] = jnp.zeros(kab_ref.shape[1:], kab_ref.dtype)
        half = (N_KV_HEADS, WINDOW, LANES)
        vx_ref[1] = jnp.concatenate([jnp.zeros(half, BF16), jnp.ones(half, BF16)], axis=-1)

    scratch = (kab_ref, vx_ref, s_ref, p_ref)
    for blk in range(q_ref.shape[0] // WINDOW):
        rows = slice(blk * WINDOW, (blk + 1) * WINDOW)
        cur = blk % 2
        _attn_prepare(kv_ref[rows, :], kab_ref, vx_ref, cur)
        o_ref[rows, :] = _attn_block(sink_ref, q_ref[rows, :], *scratch, 1 - cur, cur,
                                     (n == 0) if blk == 0 else False)


def _attention(qkv, sinks, batch, seq):
    t = qkv.shape[0]
    tq = 4 * WINDOW
    steps = seq // tq
    kv_col = Q_WIDTH // (2 * KV_WIDTH)
    return pl.pallas_call(
        _attn_body,
        grid=(batch, steps),
        in_specs=[
            pl.BlockSpec(memory_space=pltpu.SMEM),
            pl.BlockSpec((tq, Q_WIDTH), lambda b, n: (b * steps + n, 0)),
            pl.BlockSpec((tq, 2 * KV_WIDTH), lambda b, n: (b * steps + n, kv_col)),
        ],
        out_specs=pl.BlockSpec((tq, Q_WIDTH), lambda b, n: (b * steps + n, 0)),
        out_shape=jax.ShapeDtypeStruct((t, Q_WIDTH), BF16),
        scratch_shapes=[
            pltpu.VMEM((2, 2 * N_KV_HEADS, WINDOW, LANES), BF16),
            pltpu.VMEM((2, N_KV_HEADS, WINDOW, 2 * LANES), BF16),
            pltpu.VMEM((N_Q_HEADS, WINDOW, 2 * WINDOW), F32),
            pltpu.VMEM((N_Q_HEADS, WINDOW, 2 * WINDOW), BF16),
        ],
        compiler_params=_cparams(2),
        name="swa_attention",
    )(sinks, qkv, qkv)


def _route(lt):
    tm = lt.shape[1]
    row = lax.broadcasted_iota(I32, (EXPERTS_PER_GROUP, tm), 0)
    neg = -jnp.inf
    gl = jnp.where(row < N_EXPERT_GROUPS, lt[0:EXPERTS_PER_GROUP], neg)
    gm = jnp.max(gl, axis=0, keepdims=True)
    g_p = 1.0 / jnp.sum(jnp.exp(gl - gm), axis=0, keepdims=True)
    g_idx = jnp.min(jnp.where(gl == gm, row, EXPERTS_PER_GROUP), axis=0, keepdims=True)
    sel = lt[EXPERT_COL0 + (N_EXPERT_GROUPS - 1) * EXPERTS_PER_GROUP:EXPERT_COL0 + N_EXPERTS]
    for g in range(N_EXPERT_GROUPS - 2, -1, -1):
        lo = EXPERT_COL0 + g * EXPERTS_PER_GROUP
        sel = jnp.where(g_idx == g, lt[lo:lo + EXPERTS_PER_GROUP], sel)
    v1 = jnp.max(sel, axis=0, keepdims=True)
    i1 = jnp.min(jnp.where(sel == v1, row, EXPERTS_PER_GROUP), axis=0, keepdims=True)
    sel2 = jnp.where(row == i1, neg, sel)
    v2 = jnp.max(sel2, axis=0, keepdims=True)
    i2 = jnp.min(jnp.where(sel2 == v2, row, EXPERTS_PER_GROUP), axis=0, keepdims=True)
    e2 = jnp.exp(v2 - v1)
    w1 = g_p / (1.0 + e2)
    w2 = g_p * e2 / (1.0 + e2)
    base = g_idx * EXPERTS_PER_GROUP
    return base + i1, base + i2, w1, w2


def _proj_body(o_ref, x_ref, mod_ref, w_ref, b_ref, lng_ref, lnb_ref, wr_ref, br_ref,
               x1_ref, h2_ref, ids_ref, wts_ref, wbf_ref):
    @pl.when(pl.program_id(0) == 0)
    def _():
        wbf_ref[...] = w_ref[...].astype(BF16)

    d = D_MODEL
    gt_m = mod_ref[:, 2 * d:3 * d]
    sh_f = mod_ref[:, 3 * d:4 * d]
    sc_f = mod_ref[:, 4 * d:5 * d]
    wr = wr_ref[...].astype(BF16)
    for r in range(x_ref.shape[0] // SUB_ROWS):
        rows = slice(r * SUB_ROWS, (r + 1) * SUB_ROWS)
        y = jnp.dot(o_ref[rows, :], wbf_ref[...], preferred_element_type=F32) + b_ref[...]
        x1 = _layer_norm(DEEPNORM_ALPHA * x_ref[rows, :] + (1.0 + gt_m) * y, lng_ref[...], lnb_ref[...])
        x1_ref[rows, :] = x1
        h2 = x1 * (1.0 + sc_f) + sh_f
        _store_rows(h2_ref, h2, r * SUB_ROWS)
        logits = jnp.dot(h2.astype(BF16), wr, preferred_element_type=F32) + br_ref[...]
        ea, eb, wa, wb = _route(logits.T)
        ids_ref[0:1, rows] = ea
        ids_ref[1:2, rows] = eb
        wts_ref[0:1, rows] = wa
        wts_ref[1:2, rows] = wb


def _proj_ln_router(o, x2, mod_rows, layer, w, b, ln_g, ln_b, w_router, b_router, seq):
    t, k = o.shape
    tm = 1024
    steps_per_batch = seq // tm
    d = D_MODEL
    return pl.pallas_call(
        _proj_body,
        grid=(t // tm,),
        in_specs=[
            pl.BlockSpec((tm, k), lambda i: (i, 0)),
            pl.BlockSpec((tm, d), lambda i: (i, 0)),
            pl.BlockSpec((None, 1, N_MOD * d), lambda i: (layer * MOD_ROWS + i // steps_per_batch, 0, 0)),
            pl.BlockSpec((k, d), lambda i: (0, 0), pipeline_mode=pl.Buffered(1)),
            pl.BlockSpec((1, d), lambda i: (0, 0)),
            pl.BlockSpec((1, d), lambda i: (0, 0)),
            pl.BlockSpec((1, d), lambda i: (0, 0)),
            pl.BlockSpec((d, ROUTER_LANES), lambda i: (0, 0)),
            pl.BlockSpec((1, ROUTER_LANES), lambda i: (0, 0)),
        ],
        out_specs=[
            pl.BlockSpec((tm, d), lambda i: (i, 0)),
            pl.BlockSpec((tm * ROW_CHUNKS, LANES), lambda i: (i, 0)),
            pl.BlockSpec((2, tm), lambda i: (0, i)),
            pl.BlockSpec((2, tm), lambda i: (0, i)),
        ],
        out_shape=[
            jax.ShapeDtypeStruct((t, d), F32),
            jax.ShapeDtypeStruct((t * ROW_CHUNKS, LANES), U32),
            jax.ShapeDtypeStruct((2, t), I32),
            jax.ShapeDtypeStruct((2, t), F32),
        ],
        scratch_shapes=[pltpu.VMEM((k, d), BF16)],
        compiler_params=_cparams(1),
        name="proj_ln_router",
    )(o, x2, mod_rows, w, b.reshape(1, d), ln_g.reshape(1, d), ln_b.reshape(1, d), w_router, b_router)


def _sort_body(ids_ref, pos_ref, te_ref, nused_ref, tv_ref, rank_ref):
    n_rows = ids_ref.shape[0]
    c = SORT_CHUNK
    erow = lax.broadcasted_iota(I32, (N_EXPERTS, c), 0)
    tri = (lax.broadcasted_iota(I32, (c, c), 0) <= lax.broadcasted_iota(I32, (c, c), 1)).astype(BF16)

    def rank_step(r, carry):
        onehot = erow == ids_ref[pl.ds(r, 1), :]
        pref = jnp.dot(onehot.astype(BF16), tri, preferred_element_type=F32)
        rank = jnp.sum(jnp.where(onehot, pref + carry, 0.0), axis=0, keepdims=True) - 1.0
        rank_ref[pl.ds(r, 1), :] = rank
        return carry + pref[:, c - 1:c]

    counts = lax.fori_loop(0, n_rows, rank_step, jnp.zeros((N_EXPERTS, 1), F32), unroll=8)
    n_tile = jnp.floor((counts + (EXPERT_TILE - 1)) * (1.0 / EXPERT_TILE))
    low = (lax.broadcasted_iota(I32, (N_EXPERTS, N_EXPERTS), 1)
           <= lax.broadcasted_iota(I32, (N_EXPERTS, N_EXPERTS), 0)).astype(BF16)
    cum = jnp.dot(low, jnp.broadcast_to(n_tile, (N_EXPERTS, LANES)).astype(BF16),
                  preferred_element_type=F32)[:, 0:1]
    row_off = (cum - n_tile) * EXPERT_TILE

    def pos_step(r, _):
        onehot = erow == ids_ref[pl.ds(r, 1), :]
        off = jnp.sum(jnp.where(onehot, row_off, 0.0), axis=0, keepdims=True)
        pos_ref[pl.ds(r, 1), :] = (off + rank_ref[pl.ds(r, 1), :]).astype(I32)
        return 0

    lax.fori_loop(0, n_rows, pos_step, 0, unroll=8)
    total = jnp.max(cum, axis=0, keepdims=True)
    n_lanes = te_ref.shape[1]
    tile = jnp.minimum(lax.broadcasted_iota(I32, (N_EXPERTS, n_lanes), 1).astype(F32), total - 1.0)
    te_ref[...] = jnp.sum(jnp.where(cum <= tile, 1.0, 0.0), axis=0, keepdims=True).astype(I32)
    nused_ref[...] = jnp.broadcast_to(total, nused_ref.shape).astype(I32)
    tile_f = lax.broadcasted_iota(I32, (N_EXPERTS, n_lanes), 1).astype(F32)
    first = cum - n_tile
    rows_left = jnp.clip(counts - (tile_f - first) * EXPERT_TILE, 0.0, float(EXPERT_TILE))
    owns = (first <= tile_f) & (tile_f < cum)
    tv_ref[...] = jnp.sum(jnp.where(owns, rows_left, 0.0), axis=0, keepdims=True).astype(I32)


def _expert_sort(ids, n_tiles):
    n_assign = ids.shape[0] * ids.shape[1]
    n_rows = n_assign // SORT_CHUNK
    te_lanes = -(-n_tiles // LANES) * LANES
    pos, te, nused, tv = pl.pallas_call(
        _sort_body,
        grid=(1,),
        in_specs=[pl.BlockSpec((n_rows, SORT_CHUNK), lambda i: (0, 0))],
        out_specs=[
            pl.BlockSpec((n_rows, SORT_CHUNK), lambda i: (0, 0)),
            pl.BlockSpec((1, te_lanes), lambda i: (0, 0)),
            pl.BlockSpec((1, LANES), lambda i: (0, 0)),
            pl.BlockSpec((1, te_lanes), lambda i: (0, 0)),
        ],
        out_shape=[
            jax.ShapeDtypeStruct((n_rows, SORT_CHUNK), I32),
            jax.ShapeDtypeStruct((1, te_lanes), I32),
            jax.ShapeDtypeStruct((1, LANES), I32),
            jax.ShapeDtypeStruct((1, te_lanes), I32),
        ],
        scratch_shapes=[pltpu.VMEM((n_rows, SORT_CHUNK), F32)],
        compiler_params=_cparams(1),
        name="expert_sort",
    )(ids.reshape(n_rows, SORT_CHUNK))
    return pos, te[0, :n_tiles], nused[0, :1], tv[0, :n_tiles]


def _sc_mesh():
    return plsc.VectorSubcoreMesh(core_axis_name="c", subcore_axis_name="s", num_cores=SC_CORES,
                                  num_subcores=SC_WORKERS // SC_CORES)


def _sc_ring(n_chunks, read, write):
    reads, writes = {}, {}
    for j in range(min(SC_SLOTS - 1, n_chunks)):
        reads[j] = read(j)
    for j in range(n_chunks):
        for cp in reads.pop(j):
            cp.wait()
        nxt = j + SC_SLOTS - 1
        if nxt < n_chunks:
            for cp in writes.pop(nxt - SC_SLOTS, []):
                cp.wait()
            reads[nxt] = read(nxt)
        writes[j] = write(j)
    for cps in writes.values():
        for cp in cps:
            cp.wait()


def _sc_scatter_rows(src, pos2, n_rows):
    t = pos2.shape[1]
    src3 = src.reshape(t, ROW_CHUNKS, LANES)
    per_worker = t // SC_WORKERS
    n_chunks = per_worker // SC_CHUNK
    idx = pos2.reshape(2, SC_WORKERS, n_chunks, SC_CHUNK).transpose(1, 0, 2, 3)
    idx = idx.reshape(SC_WORKERS, 2 * n_chunks, SC_CHUNK)

    @functools.partial(
        pl.kernel, mesh=_sc_mesh(),
        out_type=jax.ShapeDtypeStruct((n_rows, ROW_CHUNKS, LANES), src.dtype),
        scratch_types=[
            pltpu.VMEM((2 * n_chunks, SC_CHUNK), I32),
            pltpu.VMEM((SC_SLOTS, SC_CHUNK, ROW_CHUNKS, LANES), src.dtype),
            pltpu.SemaphoreType.DMA((SC_SLOTS,)),
            pltpu.SemaphoreType.DMA((SC_SLOTS,)),
        ],
        name="sc_scatter_rows",
    )
    def scatter(src_hbm, idx_hbm, out_hbm, idx_v, rows_v, rsem, wsem):
        wid = lax.axis_index("s") * SC_CORES + lax.axis_index("c")
        pltpu.sync_copy(idx_hbm.at[wid], idx_v)

        def read(j):
            b = j % SC_SLOTS
            return [pltpu.async_copy(src_hbm.at[pl.ds(wid * per_worker + j * SC_CHUNK, SC_CHUNK)], rows_v.at[b],
                                     rsem.at[b])]

        def write(j):
            b = j % SC_SLOTS
            return [pltpu.async_copy(rows_v.at[b], out_hbm.at[idx_v.at[k * n_chunks + j]], wsem.at[b])
                    for k in range(2)]

        _sc_ring(n_chunks, read, write)

    return scatter(src3, idx).reshape(n_rows * ROW_CHUNKS, LANES)


def _sc_gather_rows(table, idx):
    n = idx.shape[0]
    table3 = table.reshape(-1, ROW_CHUNKS, LANES)
    per_worker = n // SC_WORKERS
    n_chunks = per_worker // SC_CHUNK

    @functools.partial(
        pl.kernel, mesh=_sc_mesh(),
        out_type=jax.ShapeDtypeStruct((n, ROW_CHUNKS, LANES), table.dtype),
        scratch_types=[
            pltpu.VMEM((n_chunks, SC_CHUNK), I32),
            pltpu.VMEM((SC_SLOTS, SC_CHUNK, ROW_CHUNKS, LANES), table.dtype),
            pltpu.SemaphoreType.DMA((SC_SLOTS,)),
            pltpu.SemaphoreType.DMA((SC_SLOTS,)),
        ],
        name="sc_gather_rows",
    )
    def gather(table_hbm, idx_hbm, out_hbm, idx_v, rows_v, rsem, wsem):
        wid = lax.axis_index("s") * SC_CORES + lax.axis_index("c")
        pltpu.sync_copy(idx_hbm.at[wid], idx_v)

        def read(j):
            b = j % SC_SLOTS
            return [pltpu.async_copy(table_hbm.at[idx_v.at[j]], rows_v.at[b], rsem.at[b])]

        def write(j):
            b = j % SC_SLOTS
            return [pltpu.async_copy(rows_v.at[b], out_hbm.at[pl.ds(wid * per_worker + j * SC_CHUNK, SC_CHUNK)],
                                     wsem.at[b])]

        _sc_ring(n_chunks, read, write)

    out = gather(table3, idx.reshape(SC_WORKERS, n_chunks, SC_CHUNK))
    return out.reshape(n * ROW_CHUNKS, LANES)


XS_SLOTS = 3


def _expert_body(te_ref, nused_ref, tv_ref, xs_hbm, wgu_ref, wd_ref, ys_ref, wgu_bf, wd_bf, xbuf, xsem):
    i = pl.program_id(0)
    n_used = nused_ref[0]
    used = i < n_used
    prev = te_ref[jnp.maximum(i - 1, 0)]
    tile_rows = EXPERT_TILE * ROW_CHUNKS

    def fetch(tile):
        slot = lax.rem(tile, XS_SLOTS)
        r0 = pl.multiple_of(tile * tile_rows, tile_rows)
        return pltpu.make_async_copy(xs_hbm.at[pl.ds(r0, tile_rows)], xbuf.at[slot], xsem.at[slot])

    @pl.when((i == 0) & (n_used > 0))
    def _():
        fetch(0).start()

    @pl.when((i == 0) & (n_used > 1))
    def _():
        fetch(1).start()

    @pl.when(i + 2 < n_used)
    def _():
        fetch(i + 2).start()

    @pl.when(used & ((i == 0) | (te_ref[i] != prev)))
    def _():
        wgu_bf[...] = wgu_ref[...].astype(BF16)
        wd_bf[...] = wd_ref[...].astype(BF16)

    @pl.when(used)
    def _():
        fetch(i).wait()
        live = lax.broadcasted_iota(I32, (EXPERT_TILE, 1), 0) < tv_ref[i]
        xs = jnp.where(live, _load_rows(xbuf.at[lax.rem(i, XS_SLOTS)], EXPERT_TILE), 0.0).astype(BF16)
        gu = jnp.dot(xs, wgu_bf[...], preferred_element_type=F32)
        gate = gu[:, :EXPERT_FF]
        up = gu[:, EXPERT_FF:]
        act = gate * jax.nn.sigmoid(gate) * up
        _store_rows(ys_ref, jnp.dot(act.astype(BF16), wd_bf[...], preferred_element_type=F32))

    @pl.when(jnp.logical_not(used))
    def _():
        ys_ref[...] = jnp.zeros(ys_ref.shape, ys_ref.dtype)


def _expert_mlp(tile_expert, n_used, tile_valid, xs, w_gate_up, w_down, layer):
    d = D_MODEL
    n_tiles = xs.shape[0] // (EXPERT_TILE * ROW_CHUNKS)
    f2 = 2 * EXPERT_FF
    tile_rows = EXPERT_TILE * ROW_CHUNKS
    e0 = layer * N_EXPERTS
    grid_spec = pltpu.PrefetchScalarGridSpec(
        num_scalar_prefetch=3,
        grid=(n_tiles,),
        in_specs=[
            pl.BlockSpec(memory_space=pl.ANY),
            pl.BlockSpec((None, d, f2), lambda i, te, nu, tv: (e0 + te[i], 0, 0)),
            pl.BlockSpec((None, EXPERT_FF, d), lambda i, te, nu, tv: (e0 + te[i], 0, 0)),
        ],
        out_specs=pl.BlockSpec((tile_rows, LANES), lambda i, te, nu, tv: (i, 0)),
        scratch_shapes=[pltpu.VMEM((d, f2), BF16), pltpu.VMEM((EXPERT_FF, d), BF16),
                        pltpu.VMEM((XS_SLOTS, tile_rows, LANES), U32), pltpu.SemaphoreType.DMA((XS_SLOTS,))],
    )
    return pl.pallas_call(
        _expert_body,
        grid_spec=grid_spec,
        out_shape=jax.ShapeDtypeStruct(xs.shape, U32),
        compiler_params=_cparams(1),
        name="expert_mlp",
    )(tile_expert, n_used, tile_valid, xs, w_gate_up.reshape(DEPTH * N_EXPERTS, d, f2),
      w_down.reshape(DEPTH * N_EXPERTS, EXPERT_FF, d))


def _moe_combine(wts_ref, x_ref, mod_ref, lng_ref, lnb_ref, ya_ref, yb_ref, token0, n):
    d = D_MODEL
    rows = slice(token0, token0 + n)
    cols = []
    for k in range(2):
        wt = jnp.broadcast_to(wts_ref[k:k + 1, rows], (LANES, n)).T
        cols.append(jnp.concatenate([wt] * (d // LANES), axis=1))
    y = cols[0] * _load_rows(ya_ref, n, token0) + cols[1] * _load_rows(yb_ref, n, token0)
    gt_f = mod_ref[:, 5 * d:6 * d]
    return _layer_norm(DEEPNORM_ALPHA * x_ref[rows, :] + (1.0 + gt_f) * y, lng_ref[...], lnb_ref[...])


def _combine_specs(tm, n_steps, steps_per_batch, layer, tile0=0):
    d = D_MODEL
    return [
        pl.BlockSpec((2, tm), lambda i: (0, tile0 + i)),
        pl.BlockSpec((tm, d), lambda i: (tile0 + i, 0)),
        pl.BlockSpec((None, 1, N_MOD * d), lambda i: (layer * MOD_ROWS + (tile0 + i) // steps_per_batch, 0, 0)),
        pl.BlockSpec((1, d), lambda i: (0, 0)),
        pl.BlockSpec((1, d), lambda i: (0, 0)),
        pl.BlockSpec((tm * ROW_CHUNKS, LANES), lambda i: (i, 0)),
        pl.BlockSpec((tm * ROW_CHUNKS, LANES), lambda i: (n_steps + i, 0)),
    ]


def _gather_part(ys, pos2, part, n_parts):
    n = pos2.shape[1] // n_parts
    idx = jnp.concatenate([pos2[0, part * n:(part + 1) * n], pos2[1, part * n:(part + 1) * n]])
    return _sc_gather_rows(ys, idx)


def _combine_body(wts_ref, x_ref, mod_ref, lng_ref, lnb_ref, ya_ref, yb_ref, *rest):
    o_ref = rest[-1]
    for r in range(x_ref.shape[0] // SUB_ROWS):
        o_ref[r * SUB_ROWS:(r + 1) * SUB_ROWS, :] = _moe_combine(wts_ref, x_ref, mod_ref, lng_ref, lnb_ref, ya_ref,
                                                                 yb_ref, r * SUB_ROWS, SUB_ROWS)


COMBINE_PARTS = 2


def _combine_ln(moe, mod_rows, layer, ln_g, ln_b, seq):
    wts, x1, ys, pos2 = moe
    t, d = x1.shape
    tm = 1024
    n_steps = t // tm // COMBINE_PARTS
    parts = [_gather_part(ys, pos2, p, COMBINE_PARTS) for p in range(COMBINE_PARTS)]
    out = None
    for p, yg in enumerate(parts):
        tile0 = p * n_steps
        first = out is None
        out = pl.pallas_call(
            _combine_body,
            grid=(n_steps,),
            in_specs=_combine_specs(tm, n_steps, seq // tm, layer, tile0)
            + ([] if first else [pl.BlockSpec(memory_space=pl.ANY)]),
            out_specs=pl.BlockSpec((tm, d), lambda i, tile0=tile0: (tile0 + i, 0)),
            out_shape=jax.ShapeDtypeStruct((t, d), F32),
            input_output_aliases={} if first else {7: 0},
            compiler_params=_cparams(1),
            name="moe_combine_ln",
        )(wts, x1, mod_rows, ln_g.reshape(1, d), ln_b.reshape(1, d), yg, yg, *([] if first else [out]))
    return out


def _gmlp_body(wts_ref, x1_ref, modp_ref, lng_ref, lnb_ref, ya_ref, yb_ref,
               mod_ref, w_ref, b_ref, g_ref, beta_ref, ws_ref, bs_ref, x2_ref, o_ref, ws_bf, w_bf):
    @pl.when(pl.program_id(0) == 0)
    def _():
        tri = lax.broadcasted_iota(I32, (CHUNK, CHUNK), 0) >= lax.broadcasted_iota(I32, (CHUNK, CHUNK), 1)
        for g in range(N_SGU_GROUPS):
            ws_bf[g] = jnp.where(tri, ws_ref[g], 0.0).astype(BF16)
        w_bf[...] = w_ref[...].astype(BF16)

    d = D_MODEL
    sh = mod_ref[:, 0:d]
    sc = mod_ref[:, d:2 * d]
    sub_rows = x1_ref.shape[0]
    for r in range(x1_ref.shape[0] // sub_rows):
        r0 = r * sub_rows
        x2 = _moe_combine(wts_ref, x1_ref, modp_ref, lng_ref, lnb_ref, ya_ref, yb_ref, r0, sub_rows)
        x2_ref[r0:r0 + sub_rows, :] = x2
        h = x2 * (1.0 + sc) + sh
        z = jnp.dot(h.astype(BF16), w_bf[...], preferred_element_type=F32) + b_ref[...]
        z = 0.5 * z * (1.0 + lax.erf(z * (2.0 ** -0.5)))
        u = z[:, :GMLP_WIDTH]
        v = _layer_norm(z[:, GMLP_WIDTH:], g_ref[...], beta_ref[...]).astype(BF16)
        for ci in range(sub_rows // CHUNK):
            rows = slice(ci * CHUNK, (ci + 1) * CHUNK)
            out_rows = slice(r0 + ci * CHUNK, r0 + (ci + 1) * CHUNK)
            for g in range(N_SGU_GROUPS):
                lanes = slice(g * SGU_GROUP_DIM, (g + 1) * SGU_GROUP_DIM)
                mixed = jnp.dot(ws_bf[g], v[rows, lanes], preferred_element_type=F32) + bs_ref[:, g:g + 1]
                o_ref[out_rows, lanes] = (u[rows, lanes] * mixed).astype(BF16)


def _gmlp_gate(moe, ln_g, ln_b, mod_rows, layer, w_in, b_in, sgu_g, sgu_b, w_s, b_s, seq):
    wts, x1, ys, pos2 = moe
    yg = _gather_part(ys, pos2, 0, 1)
    t, d = x1.shape
    tm = 512
    steps_per_batch = seq // tm
    n_steps = t // tm
    gw = GMLP_WIDTH
    return pl.pallas_call(
        _gmlp_body,
        grid=(n_steps,),
        in_specs=_combine_specs(tm, n_steps, steps_per_batch, layer - 1) + [
            pl.BlockSpec((None, 1, N_MOD * d), lambda i: (layer * MOD_ROWS + i // steps_per_batch, 0, 0)),
            pl.BlockSpec((d, 2 * gw), lambda i: (0, 0)),
            pl.BlockSpec((1, 2 * gw), lambda i: (0, 0)),
            pl.BlockSpec((1, gw), lambda i: (0, 0)),
            pl.BlockSpec((1, gw), lambda i: (0, 0)),
            pl.BlockSpec((N_SGU_GROUPS, CHUNK, CHUNK), lambda i: (0, 0, 0)),
            pl.BlockSpec((CHUNK, N_SGU_GROUPS), lambda i: (0, 0)),
        ],
        out_specs=[pl.BlockSpec((tm, d), lambda i: (i, 0)), pl.BlockSpec((tm, gw), lambda i: (i, 0))],
        out_shape=[jax.ShapeDtypeStruct((t, d), F32), jax.ShapeDtypeStruct((t, gw), BF16)],
        scratch_shapes=[pltpu.VMEM((N_SGU_GROUPS, CHUNK, CHUNK), BF16), pltpu.VMEM((d, 2 * gw), BF16)],
        compiler_params=_cparams(1),
        name="combine_gmlp_gate",
    )(wts, x1, mod_rows, ln_g.reshape(1, d), ln_b.reshape(1, d), yg, yg,
      mod_rows, w_in, b_in.reshape(1, 2 * gw), sgu_g.reshape(1, gw), sgu_b.reshape(1, gw), w_s, b_s.T)


def _router_params(w_group, b_group, w_expert, b_expert):
    def lanes(group_part, expert_part):
        rows = group_part.shape[0]
        gap = jnp.zeros((rows, EXPERT_COL0 - N_EXPERT_GROUPS), F32)
        tail = jnp.zeros((rows, ROUTER_LANES - EXPERT_COL0 - N_EXPERTS), F32)
        return jnp.concatenate([group_part, gap, expert_part, tail], axis=1)

    return lanes(w_group, w_expert), lanes(b_group[None, :], b_expert[None, :])


def _moe_experts(x1, h2, ids, wts, layer, w_gate_up, w_down):
    t = x1.shape[0]
    n_rows = 2 * t + N_EXPERTS * EXPERT_TILE
    n_tiles = n_rows // EXPERT_TILE
    pos, tile_expert, n_used, tile_valid = _expert_sort(ids, n_tiles)
    pos2 = pos.reshape(2, t)
    xs = _sc_scatter_rows(h2, pos2, n_rows)
    ys = _expert_mlp(tile_expert, n_used, tile_valid, xs, w_gate_up, w_down, layer)
    return wts, x1, ys, pos2


def kernel(x, c, positions, ada_w, ada_b, post_ln_g, post_ln_b, attn_w_qkv, attn_b_qkv, attn_sinks, attn_w_o, attn_b_o, gmlp_w_in, gmlp_b_in, gmlp_sgu_ln_g, gmlp_sgu_ln_b, gmlp_w_s, gmlp_b_s, gmlp_w_out, gmlp_b_out, moe_w_group_router, moe_b_group_router, moe_w_expert_router, moe_b_expert_router, moe_w_gate_up, moe_w_down):
    batch, seq, d = x.shape
    t = batch * seq
    x2 = x.reshape(t, d)
    c_pad = jnp.pad(c, ((0, MOD_ROWS - batch), (0, 0)))
    mod_rows = _adaln_mod(c_pad, ada_w, ada_b).reshape(DEPTH * MOD_ROWS, 1, N_MOD * d)

    moe = None
    for layer in range(DEPTH):
        j = layer // 2
        if layer % 2 == 0:
            if moe is not None:
                x2 = _combine_ln(moe, mod_rows, layer - 1, post_ln_g[layer - 1, 1], post_ln_b[layer - 1, 1], seq)
            qkv = _qkv_rope(x2, mod_rows, layer, positions, attn_w_qkv[j], attn_b_qkv[j], seq)
            mix = _attention(qkv, attn_sinks[j], batch, seq)
            w_out, b_out = attn_w_o[j], attn_b_o[j]
        else:
            x2, mix = _gmlp_gate(moe, post_ln_g[layer - 1, 1], post_ln_b[layer - 1, 1], mod_rows, layer,
                                 gmlp_w_in[j], gmlp_b_in[j], gmlp_sgu_ln_g[j], gmlp_sgu_ln_b[j],
                                 gmlp_w_s[j], gmlp_b_s[j], seq)
            w_out, b_out = gmlp_w_out[j], gmlp_b_out[j]
        w_router, b_router = _router_params(moe_w_group_router[layer], moe_b_group_router[layer],
                                            moe_w_expert_router[layer], moe_b_expert_router[layer])
        x1, h2, ids, wts = _proj_ln_router(mix, x2, mod_rows, layer, w_out, b_out, post_ln_g[layer, 0],
                                           post_ln_b[layer, 0], w_router, b_router, seq)
        moe = _moe_experts(x1, h2, ids, wts, layer, moe_w_gate_up, moe_w_down)
    x2 = _combine_ln(moe, mod_rows, DEPTH - 1, post_ln_g[DEPTH - 1, 1], post_ln_b[DEPTH - 1, 1], seq)
    return x2.reshape(batch, seq, d)
```

```python
import functools

import jax
import jax.numpy as jnp
from jax import lax
from jax.experimental import pallas as pl
from jax.experimental.pallas import tpu as pltpu
from jax.experimental.pallas import tpu_sc as plsc

F32 = jnp.float32
BF16 = jnp.bfloat16
I32 = jnp.int32

D_MODEL = 1024
DEPTH = 2
HEAD_DIM = 64
N_Q_HEADS = 16
N_KV_HEADS = 4
GQA_GROUP = N_Q_HEADS // N_KV_HEADS
WINDOW = 128
ROPE_THETA = 10000.0
Q_WIDTH = N_Q_HEADS * HEAD_DIM
KV_WIDTH = N_KV_HEADS * HEAD_DIM
QKV_WIDTH = Q_WIDTH + 2 * KV_WIDTH
CHUNK = 128
GMLP_WIDTH = 2 * D_MODEL
N_SGU_GROUPS = 8
SGU_GROUP_DIM = GMLP_WIDTH // N_SGU_GROUPS
N_EXPERT_GROUPS = 4
EXPERTS_PER_GROUP = 8
N_EXPERTS = N_EXPERT_GROUPS * EXPERTS_PER_GROUP
EXPERT_FF = D_MODEL // 4
N_MOD = 6
DEEPNORM_ALPHA = (2.0 * DEPTH) ** 0.25
LN_EPS = 1e-5

LANES = 128
MOD_ROWS = 8
ROUTER_LANES = 128
EXPERT_COL0 = 8
SORT_CHUNK = 256
EXPERT_TILE = 512
SUB_ROWS = 256
VMEM_LIMIT = 56 * 1024 * 1024
SC_CORES = 2
SC_WORKERS = 32
SC_CHUNK = 64
SC_SLOTS = 3


def _cparams(n_axes, vmem=VMEM_LIMIT):
    return pltpu.CompilerParams(dimension_semantics=("arbitrary",) * n_axes, vmem_limit_bytes=vmem)


U32 = jnp.uint32
ROW_CHUNKS = D_MODEL // 2 // LANES


def _store_rows(ref, val, token0=0):
    n = val.shape[0]
    half = D_MODEL // 2
    words = pltpu.pack_elementwise([val[:, :half], val[:, half:]], packed_dtype=BF16)
    for c in range(ROW_CHUNKS):
        ref[pl.ds(token0 * ROW_CHUNKS + c, n, stride=ROW_CHUNKS), :] = words[:, c * LANES:(c + 1) * LANES]


def _load_rows(ref, n, token0=0):
    words = jnp.concatenate([ref[pl.ds(token0 * ROW_CHUNKS + c, n, stride=ROW_CHUNKS), :]
                             for c in range(ROW_CHUNKS)], axis=1)
    lo = lax.bitcast_convert_type(words << 16, F32)
    hi = lax.bitcast_convert_type(words & jnp.uint32(0xFFFF0000), F32)
    return jnp.concatenate([lo, hi], axis=1)


def _layer_norm(r, g, b):
    mu = jnp.mean(r, axis=-1, keepdims=True)
    d = r - mu
    var = jnp.mean(d * d, axis=-1, keepdims=True)
    return d * lax.rsqrt(var + LN_EPS) * g + b


def _mod_body(c_ref, w_ref, b_ref, o_ref):
    c = c_ref[...]
    ca = c * jax.nn.sigmoid(c)
    o_ref[...] = jnp.dot(ca.astype(BF16), w_ref[...].astype(BF16), preferred_element_type=F32) + b_ref[...]


def _adaln_mod(c_pad, ada_w, ada_b):
    tn = 1536
    n_out = N_MOD * D_MODEL
    return pl.pallas_call(
        _mod_body,
        grid=(DEPTH, n_out // tn),
        in_specs=[
            pl.BlockSpec((MOD_ROWS, D_MODEL), lambda l, j: (0, 0)),
            pl.BlockSpec((None, D_MODEL, tn), lambda l, j: (l, 0, j)),
            pl.BlockSpec((None, 1, tn), lambda l, j: (l, 0, j)),
        ],
        out_specs=pl.BlockSpec((None, MOD_ROWS, tn), lambda l, j: (l, 0, j)),
        out_shape=jax.ShapeDtypeStruct((DEPTH, MOD_ROWS, n_out), F32),
        compiler_params=_cparams(2),
        name="adaln_mod",
    )(c_pad, ada_w, ada_b.reshape(DEPTH, 1, n_out))


def _qkv_body(x_ref, mod_ref, pos_ref, w_ref, b_ref, invf_ref, o_ref, wbf_ref):
    @pl.when(pl.program_id(0) == 0)
    def _():
        wbf_ref[...] = w_ref[...].astype(BF16)

    tm = x_ref.shape[0]
    sh = mod_ref[:, 0:D_MODEL]
    sc = mod_ref[:, D_MODEL:2 * D_MODEL]
    h = x_ref[...] * (1.0 + sc) + sh
    qkv = jnp.dot(h.astype(BF16), wbf_ref[...], preferred_element_type=F32) + b_ref[...]

    ang = invf_ref[...] * pos_ref[...].astype(F32)
    c = jnp.cos(ang)
    s = jnp.sin(ang)
    ct = jnp.concatenate([c, c, c, c], axis=0).T
    st = jnp.concatenate([-s, s, -s, s], axis=0).T
    lane = lax.broadcasted_iota(I32, (tm, LANES), 1)
    first_half = (lane & (HEAD_DIM // 2)) == 0
    n_rope = (Q_WIDTH + KV_WIDTH) // LANES
    for j in range(n_rope):
        blk = qkv[:, j * LANES:(j + 1) * LANES]
        rot = jnp.where(first_half, pltpu.roll(blk, LANES - HEAD_DIM // 2, 1), pltpu.roll(blk, HEAD_DIM // 2, 1))
        r = blk * ct + rot * st
        if j < Q_WIDTH // LANES:
            r = r * (HEAD_DIM ** -0.5)
        o_ref[:, j * LANES:(j + 1) * LANES] = r.astype(BF16)
    o_ref[:, Q_WIDTH + KV_WIDTH:] = qkv[:, Q_WIDTH + KV_WIDTH:].astype(BF16)


def _qkv_rope(x2, mod_rows, layer, positions, w_qkv, b_qkv, seq):
    t = x2.shape[0]
    tm = 1024
    steps_per_batch = seq // tm
    inv_freq = ROPE_THETA ** (-jnp.arange(0, HEAD_DIM, 2, dtype=F32) / HEAD_DIM)
    return pl.pallas_call(
        _qkv_body,
        grid=(t // tm,),
        in_specs=[
            pl.BlockSpec((tm, D_MODEL), lambda i: (i, 0)),
            pl.BlockSpec((None, 1, N_MOD * D_MODEL), lambda i: (layer * MOD_ROWS + i // steps_per_batch, 0, 0)),
            pl.BlockSpec((None, 1, tm), lambda i: (i, 0, 0)),
            pl.BlockSpec((D_MODEL, QKV_WIDTH), lambda i: (0, 0)),
            pl.BlockSpec((1, QKV_WIDTH), lambda i: (0, 0)),
            pl.BlockSpec((HEAD_DIM // 2, 1), lambda i: (0, 0)),
        ],
        out_specs=pl.BlockSpec((tm, QKV_WIDTH), lambda i: (i, 0)),
        out_shape=jax.ShapeDtypeStruct((t, QKV_WIDTH), BF16),
        scratch_shapes=[pltpu.VMEM((D_MODEL, QKV_WIDTH), BF16)],
        compiler_params=_cparams(1),
        name="qkv_rope",
    )(x2, mod_rows, positions.reshape(t // tm, 1, tm), w_qkv, b_qkv.reshape(1, QKV_WIDTH),
      inv_freq.reshape(HEAD_DIM // 2, 1))


BF16_ROWS = 16


def _attn_prepare(kv, kab_ref, vx_ref, slot):
    kv = kv.astype(F32)
    low = lax.broadcasted_iota(I32, (WINDOW, LANES), 1) < HEAD_DIM
    ones = jnp.ones((WINDOW, LANES), F32)
    for g in range(N_KV_HEADS):
        for part, ref in ((0, None), (KV_WIDTH, vx_ref)):
            tile = kv[:, part + (g // 2) * LANES:part + (g // 2 + 1) * LANES]
            other = pltpu.roll(tile, HEAD_DIM, 1)
            in_low, in_high = (tile, other) if g % 2 == 0 else (other, tile)
            if ref is None:
                kab_ref[slot, 2 * g] = jnp.where(low, in_low, 0.0).astype(BF16)
                kab_ref[slot, 2 * g + 1] = jnp.where(low, 0.0, in_high).astype(BF16)
            else:
                both = jnp.where(low, in_low, in_high)
                vx_ref[slot, g] = jnp.concatenate([both, ones], axis=1).astype(BF16)


def _attn_block(sink_ref, q, kab_ref, vx_ref, s_ref, p_ref, prev, cur, first_block):
    for g in range(N_KV_HEADS):
        q_pair = jnp.concatenate([q[:, (2 * g) * LANES:(2 * g + 1) * LANES],
                                  q[:, (2 * g + 1) * LANES:(2 * g + 2) * LANES]], axis=0)
        for a in range(2):
            kband = jnp.concatenate([kab_ref[prev, 2 * g + a], kab_ref[cur, 2 * g + a]], axis=0)
            s = lax.dot_general(q_pair, kband, (((1,), (1,)), ((), ())), preferred_element_type=F32)
            s_ref[GQA_GROUP * g + a] = s[:WINDOW]
            s_ref[GQA_GROUP * g + 2 + a] = s[WINDOW:]
    qi = lax.broadcasted_iota(I32, (WINDOW, 2 * WINDOW), 0) + WINDOW
    kj = lax.broadcasted_iota(I32, (WINDOW, 2 * WINDOW), 1)
    mask = (kj <= qi) & (kj > qi - WINDOW) & ((kj >= WINDOW) | jnp.logical_not(first_block))
    key0 = lax.broadcasted_iota(I32, (1, 2 * WINDOW), 1) == 0
    for h in range(N_Q_HEADS):
        s = jnp.where(mask, s_ref[h], jnp.where(key0, sink_ref[h], -jnp.inf))
        m = jnp.max(s, axis=-1, keepdims=True)
        p_ref[h] = jnp.exp(s - m).astype(BF16)
    low = lax.broadcasted_iota(I32, (WINDOW, LANES), 1) < HEAD_DIM
    sink_row = ((lax.broadcasted_iota(I32, (BF16_ROWS, 2 * LANES), 0) == 0)
                & (lax.broadcasted_iota(I32, (BF16_ROWS, 2 * LANES), 1) < LANES))
    out_tiles = []
    for g in range(N_KV_HEADS):
        v_prev = vx_ref[prev, g]
        v_head = jnp.where(sink_row, 0.0, v_prev[:BF16_ROWS].astype(F32)).astype(BF16)
        vband = jnp.concatenate([v_head, v_prev[BF16_ROWS:], vx_ref[cur, g]], axis=0)
        p4 = p_ref[GQA_GROUP * g:GQA_GROUP * (g + 1)].reshape(GQA_GROUP * WINDOW, 2 * WINDOW)
        o4 = jnp.dot(p4, vband, preferred_element_type=F32)
        heads = []
        for j in range(GQA_GROUP):
            blk = o4[j * WINDOW:(j + 1) * WINDOW]
            heads.append(blk[:, :LANES] / blk[:, LANES:])
        out_tiles.append(jnp.where(low, heads[0], heads[1]))
        out_tiles.append(jnp.where(low, heads[2], heads[3]))
    return jnp.concatenate(out_tiles, axis=1).astype(BF16)


def _attn_body(sink_ref, q_ref, kv_ref, o_ref, kab_ref, vx_ref, s_ref, p_ref):
    n = pl.program_id(1)

    @pl.when(n == 0)
    def _():
        kab_ref[1] = jnp.zeros(kab_ref.shape[1:], kab_ref.dtype)
        half = (N_KV_HEADS, WINDOW, LANES)
        vx_ref[1] = jnp.concatenate([jnp.zeros(half, BF16), jnp.ones(half, BF16)], axis=-1)

    scratch = (kab_ref, vx_ref, s_ref, p_ref)
    for blk in range(q_ref.shape[0] // WINDOW):
        rows = slice(blk * WINDOW, (blk + 1) * WINDOW)
        cur = blk % 2
        _attn_prepare(kv_ref[rows, :], kab_ref, vx_ref, cur)
        o_ref[rows, :] = _attn_block(sink_ref, q_ref[rows, :], *scratch, 1 - cur, cur,
                                     (n == 0) if blk == 0 else False)


def _attention(qkv, sinks, batch, seq):
    t = qkv.shape[0]
    tq = 4 * WINDOW
    steps = seq // tq
    kv_col = Q_WIDTH // (2 * KV_WIDTH)
    return pl.pallas_call(
        _attn_body,
        grid=(batch, steps),
        in_specs=[
            pl.BlockSpec(memory_space=pltpu.SMEM),
            pl.BlockSpec((tq, Q_WIDTH), lambda b, n: (b * steps + n, 0)),
            pl.BlockSpec((tq, 2 * KV_WIDTH), lambda b, n: (b * steps + n, kv_col)),
        ],
        out_specs=pl.BlockSpec((tq, Q_WIDTH), lambda b, n: (b * steps + n, 0)),
        out_shape=jax.ShapeDtypeStruct((t, Q_WIDTH), BF16),
        scratch_shapes=[
            pltpu.VMEM((2, 2 * N_KV_HEADS, WINDOW, LANES), BF16),
            pltpu.VMEM((2, N_KV_HEADS, WINDOW, 2 * LANES), BF16),
            pltpu.VMEM((N_Q_HEADS, WINDOW, 2 * WINDOW), F32),
            pltpu.VMEM((N_Q_HEADS, WINDOW, 2 * WINDOW), BF16),
        ],
        compiler_params=_cparams(2),
        name="swa_attention",
    )(sinks, qkv, qkv)


def _route(lt):
    tm = lt.shape[1]
    row = lax.broadcasted_iota(I32, (EXPERTS_PER_GROUP, tm), 0)
    neg = -jnp.inf
    gl = jnp.where(row < N_EXPERT_GROUPS, lt[0:EXPERTS_PER_GROUP], neg)
    gm = jnp.max(gl, axis=0, keepdims=True)
    g_p = 1.0 / jnp.sum(jnp.exp(gl - gm), axis=0, keepdims=True)
    g_idx = jnp.min(jnp.where(gl == gm, row, EXPERTS_PER_GROUP), axis=0, keepdims=True)
    sel = lt[EXPERT_COL0 + (N_EXPERT_GROUPS - 1) * EXPERTS_PER_GROUP:EXPERT_COL0 + N_EXPERTS]
    for g in range(N_EXPERT_GROUPS - 2, -1, -1):
        lo = EXPERT_COL0 + g * EXPERTS_PER_GROUP
        sel = jnp.where(g_idx == g, lt[lo:lo + EXPERTS_PER_GROUP], sel)
    v1 = jnp.max(sel, axis=0, keepdims=True)
    i1 = jnp.min(jnp.where(sel == v1, row, EXPERTS_PER_GROUP), axis=0, keepdims=True)
    sel2 = jnp.where(row == i1, neg, sel)
    v2 = jnp.max(sel2, axis=0, keepdims=True)
    i2 = jnp.min(jnp.where(sel2 == v2, row, EXPERTS_PER_GROUP), axis=0, keepdims=True)
    e2 = jnp.exp(v2 - v1)
    w1 = g_p / (1.0 + e2)
    w2 = g_p * e2 / (1.0 + e2)
    base = g_idx * EXPERTS_PER_GROUP
    return base + i1, base + i2, w1, w2


def _proj_body(o_ref, x_ref, mod_ref, w_ref, b_ref, lng_ref, lnb_ref, wr_ref, br_ref,
               x1_ref, h2_ref, ids_ref, wts_ref, wbf_ref):
    @pl.when(pl.program_id(0) == 0)
    def _():
        wbf_ref[...] = w_ref[...].astype(BF16)

    d = D_MODEL
    gt_m = mod_ref[:, 2 * d:3 * d]
    sh_f = mod_ref[:, 3 * d:4 * d]
    sc_f = mod_ref[:, 4 * d:5 * d]
    wr = wr_ref[...].astype(BF16)
    for r in range(x_ref.shape[0] // SUB_ROWS):
        rows = slice(r * SUB_ROWS, (r + 1) * SUB_ROWS)
        y = jnp.dot(o_ref[rows, :], wbf_ref[...], preferred_element_type=F32) + b_ref[...]
        x1 = _layer_norm(DEEPNORM_ALPHA * x_ref[rows, :] + (1.0 + gt_m) * y, lng_ref[...], lnb_ref[...])
        x1_ref[rows, :] = x1
        h2 = x1 * (1.0 + sc_f) + sh_f
        _store_rows(h2_ref, h2, r * SUB_ROWS)
        logits = jnp.dot(h2.astype(BF16), wr, preferred_element_type=F32) + br_ref[...]
        ea, eb, wa, wb = _route(logits.T)
        ids_ref[0:1, rows] = ea
        ids_ref[1:2, rows] = eb
        wts_ref[0:1, rows] = wa
        wts_ref[1:2, rows] = wb


def _proj_ln_router(o, x2, mod_rows, layer, w, b, ln_g, ln_b, w_router, b_router, seq):
    t, k = o.shape
    tm = 1024
    steps_per_batch = seq // tm
    d = D_MODEL
    return pl.pallas_call(
        _proj_body,
        grid=(t // tm,),
        in_specs=[
            pl.BlockSpec((tm, k), lambda i: (i, 0)),
            pl.BlockSpec((tm, d), lambda i: (i, 0)),
            pl.BlockSpec((None, 1, N_MOD * d), lambda i: (layer * MOD_ROWS + i // steps_per_batch, 0, 0)),
            pl.BlockSpec((k, d), lambda i: (0, 0), pipeline_mode=pl.Buffered(1)),
            pl.BlockSpec((1, d), lambda i: (0, 0)),
            pl.BlockSpec((1, d), lambda i: (0, 0)),
            pl.BlockSpec((1, d), lambda i: (0, 0)),
            pl.BlockSpec((d, ROUTER_LANES), lambda i: (0, 0)),
            pl.BlockSpec((1, ROUTER_LANES), lambda i: (0, 0)),
        ],
        out_specs=[
            pl.BlockSpec((tm, d), lambda i: (i, 0)),
            pl.BlockSpec((tm * ROW_CHUNKS, LANES), lambda i: (i, 0)),
            pl.BlockSpec((2, tm), lambda i: (0, i)),
            pl.BlockSpec((2, tm), lambda i: (0, i)),
        ],
        out_shape=[
            jax.ShapeDtypeStruct((t, d), F32),
            jax.ShapeDtypeStruct((t * ROW_CHUNKS, LANES), U32),
            jax.ShapeDtypeStruct((2, t), I32),
            jax.ShapeDtypeStruct((2, t), F32),
        ],
        scratch_shapes=[pltpu.VMEM((k, d), BF16)],
        compiler_params=_cparams(1),
        name="proj_ln_router",
    )(o, x2, mod_rows, w, b.reshape(1, d), ln_g.reshape(1, d), ln_b.reshape(1, d), w_router, b_router)


def _sort_body(ids_ref, pos_ref, te_ref, nused_ref, tv_ref, rank_ref):
    n_rows = ids_ref.shape[0]
    c = SORT_CHUNK
    erow = lax.broadcasted_iota(I32, (N_EXPERTS, c), 0)
    tri = (lax.broadcasted_iota(I32, (c, c), 0) <= lax.broadcasted_iota(I32, (c, c), 1)).astype(BF16)

    def rank_step(r, carry):
        onehot = erow == ids_ref[pl.ds(r, 1), :]
        pref = jnp.dot(onehot.astype(BF16), tri, preferred_element_type=F32)
        rank = jnp.sum(jnp.where(onehot, pref + carry, 0.0), axis=0, keepdims=True) - 1.0
        rank_ref[pl.ds(r, 1), :] = rank
        return carry + pref[:, c - 1:c]

    counts = lax.fori_loop(0, n_rows, rank_step, jnp.zeros((N_EXPERTS, 1), F32), unroll=8)
    n_tile = jnp.floor((counts + (EXPERT_TILE - 1)) * (1.0 / EXPERT_TILE))
    low = (lax.broadcasted_iota(I32, (N_EXPERTS, N_EXPERTS), 1)
           <= lax.broadcasted_iota(I32, (N_EXPERTS, N_EXPERTS), 0)).astype(BF16)
    cum = jnp.dot(low, jnp.broadcast_to(n_tile, (N_EXPERTS, LANES)).astype(BF16),
                  preferred_element_type=F32)[:, 0:1]
    row_off = (cum - n_tile) * EXPERT_TILE

    def pos_step(r, _):
        onehot = erow == ids_ref[pl.ds(r, 1), :]
        off = jnp.sum(jnp.where(onehot, row_off, 0.0), axis=0, keepdims=True)
        pos_ref[pl.ds(r, 1), :] = (off + rank_ref[pl.ds(r, 1), :]).astype(I32)
        return 0

    lax.fori_loop(0, n_rows, pos_step, 0, unroll=8)
    total = jnp.max(cum, axis=0, keepdims=True)
    n_lanes = te_ref.shape[1]
    tile = jnp.minimum(lax.broadcasted_iota(I32, (N_EXPERTS, n_lanes), 1).astype(F32), total - 1.0)
    te_ref[...] = jnp.sum(jnp.where(cum <= tile, 1.0, 0.0), axis=0, keepdims=True).astype(I32)
    nused_ref[...] = jnp.broadcast_to(total, nused_ref.shape).astype(I32)
    tile_f = lax.broadcasted_iota(I32, (N_EXPERTS, n_lanes), 1).astype(F32)
    first = cum - n_tile
    rows_left = jnp.clip(counts - (tile_f - first) * EXPERT_TILE, 0.0, float(EXPERT_TILE))
    owns = (first <= tile_f) & (tile_f < cum)
    tv_ref[...] = jnp.sum(jnp.where(owns, rows_left, 0.0), axis=0, keepdims=True).astype(I32)


def _expert_sort(ids, n_tiles):
    n_assign = ids.shape[0] * ids.shape[1]
    n_rows = n_assign // SORT_CHUNK
    te_lanes = -(-n_tiles // LANES) * LANES
    pos, te, nused, tv = pl.pallas_call(
        _sort_body,
        grid=(1,),
        in_specs=[pl.BlockSpec((n_rows, SORT_CHUNK), lambda i: (0, 0))],
        out_specs=[
            pl.BlockSpec((n_rows, SORT_CHUNK), lambda i: (0, 0)),
            pl.BlockSpec((1, te_lanes), lambda i: (0, 0)),
            pl.BlockSpec((1, LANES), lambda i: (0, 0)),
            pl.BlockSpec((1, te_lanes), lambda i: (0, 0)),
        ],
        out_shape=[
            jax.ShapeDtypeStruct((n_rows, SORT_CHUNK), I32),
            jax.ShapeDtypeStruct((1, te_lanes), I32),
            jax.ShapeDtypeStruct((1, LANES), I32),
            jax.ShapeDtypeStruct((1, te_lanes), I32),
        ],
        scratch_shapes=[pltpu.VMEM((n_rows, SORT_CHUNK), F32)],
        compiler_params=_cparams(1),
        name="expert_sort",
    )(ids.reshape(n_rows, SORT_CHUNK))
    return pos, te[0, :n_tiles], nused[0, :1], tv[0, :n_tiles]


def _sc_mesh():
    return plsc.VectorSubcoreMesh(core_axis_name="c", subcore_axis_name="s", num_cores=SC_CORES,
                                  num_subcores=SC_WORKERS // SC_CORES)


def _sc_ring(n_chunks, read, write):
    reads, writes = {}, {}
    for j in range(min(SC_SLOTS - 1, n_chunks)):
        reads[j] = read(j)
    for j in range(n_chunks):
        for cp in reads.pop(j):
            cp.wait()
        nxt = j + SC_SLOTS - 1
        if nxt < n_chunks:
            for cp in writes.pop(nxt - SC_SLOTS, []):
                cp.wait()
            reads[nxt] = read(nxt)
        writes[j] = write(j)
    for cps in writes.values():
        for cp in cps:
            cp.wait()


def _sc_scatter_rows(src, pos2, n_rows):
    t = pos2.shape[1]
    src3 = src.reshape(t, ROW_CHUNKS, LANES)
    per_worker = t // SC_WORKERS
    n_chunks = per_worker // SC_CHUNK
    idx = pos2.reshape(2, SC_WORKERS, n_chunks, SC_CHUNK).transpose(1, 0, 2, 3)
    idx = idx.reshape(SC_WORKERS, 2 * n_chunks, SC_CHUNK)

    @functools.partial(
        pl.kernel, mesh=_sc_mesh(),
        out_type=jax.ShapeDtypeStruct((n_rows, ROW_CHUNKS, LANES), src.dtype),
        scratch_types=[
            pltpu.VMEM((2 * n_chunks, SC_CHUNK), I32),
            pltpu.VMEM((SC_SLOTS, SC_CHUNK, ROW_CHUNKS, LANES), src.dtype),
            pltpu.SemaphoreType.DMA((SC_SLOTS,)),
            pltpu.SemaphoreType.DMA((SC_SLOTS,)),
        ],
        name="sc_scatter_rows",
    )
    def scatter(src_hbm, idx_hbm, out_hbm, idx_v, rows_v, rsem, wsem):
        wid = lax.axis_index("s") * SC_CORES + lax.axis_index("c")
        pltpu.sync_copy(idx_hbm.at[wid], idx_v)

        def read(j):
            b = j % SC_SLOTS
            return [pltpu.async_copy(src_hbm.at[pl.ds(wid * per_worker + j * SC_CHUNK, SC_CHUNK)], rows_v.at[b],
                                     rsem.at[b])]

        def write(j):
            b = j % SC_SLOTS
            return [pltpu.async_copy(rows_v.at[b], out_hbm.at[idx_v.at[k * n_chunks + j]], wsem.at[b])
                    for k in range(2)]

        _sc_ring(n_chunks, read, write)

    return scatter(src3, idx).reshape(n_rows * ROW_CHUNKS, LANES)


def _sc_gather_rows(table, idx):
    n = idx.shape[0]
    table3 = table.reshape(-1, ROW_CHUNKS, LANES)
    per_worker = n // SC_WORKERS
    n_chunks = per_worker // SC_CHUNK

    @functools.partial(
        pl.kernel, mesh=_sc_mesh(),
        out_type=jax.ShapeDtypeStruct((n, ROW_CHUNKS, LANES), table.dtype),
        scratch_types=[
            pltpu.VMEM((n_chunks, SC_CHUNK), I32),
            pltpu.VMEM((SC_SLOTS, SC_CHUNK, ROW_CHUNKS, LANES), table.dtype),
            pltpu.SemaphoreType.DMA((SC_SLOTS,)),
            pltpu.SemaphoreType.DMA((SC_SLOTS,)),
        ],
        name="sc_gather_rows",
    )
    def gather(table_hbm, idx_hbm, out_hbm, idx_v, rows_v, rsem, wsem):
        wid = lax.axis_index("s") * SC_CORES + lax.axis_index("c")
        pltpu.sync_copy(idx_hbm.at[wid], idx_v)

        def read(j):
            b = j % SC_SLOTS
            return [pltpu.async_copy(table_hbm.at[idx_v.at[j]], rows_v.at[b], rsem.at[b])]

        def write(j):
            b = j % SC_SLOTS
            return [pltpu.async_copy(rows_v.at[b], out_hbm.at[pl.ds(wid * per_worker + j * SC_CHUNK, SC_CHUNK)],
                                     wsem.at[b])]

        _sc_ring(n_chunks, read, write)

    out = gather(table3, idx.reshape(SC_WORKERS, n_chunks, SC_CHUNK))
    return out.reshape(n * ROW_CHUNKS, LANES)


XS_SLOTS = 3


def _expert_body(te_ref, nused_ref, tv_ref, xs_hbm, wgu_ref, wd_ref, ys_ref, wgu_bf, wd_bf, xbuf, xsem):
    i = pl.program_id(0)
    n_used = nused_ref[0]
    used = i < n_used
    prev = te_ref[jnp.maximum(i - 1, 0)]
    tile_rows = EXPERT_TILE * ROW_CHUNKS

    def fetch(tile):
        slot = lax.rem(tile, XS_SLOTS)
        r0 = pl.multiple_of(tile * tile_rows, tile_rows)
        return pltpu.make_async_copy(xs_hbm.at[pl.ds(r0, tile_rows)], xbuf.at[slot], xsem.at[slot])

    @pl.when((i == 0) & (n_used > 0))
    def _():
        fetch(0).start()

    @pl.when((i == 0) & (n_used > 1))
    def _():
        fetch(1).start()

    @pl.when(i + 2 < n_used)
    def _():
        fetch(i + 2).start()

    @pl.when(used & ((i == 0) | (te_ref[i] != prev)))
    def _():
        wgu_bf[...] = wgu_ref[...].astype(BF16)
        wd_bf[...] = wd_ref[...].astype(BF16)

    @pl.when(used)
    def _():
        fetch(i).wait()
        live = lax.broadcasted_iota(I32, (EXPERT_TILE, 1), 0) < tv_ref[i]
        xs = jnp.where(live, _load_rows(xbuf.at[lax.rem(i, XS_SLOTS)], EXPERT_TILE), 0.0).astype(BF16)
        gu = jnp.dot(xs, wgu_bf[...], preferred_element_type=F32)
        gate = gu[:, :EXPERT_FF]
        up = gu[:, EXPERT_FF:]
        act = gate * jax.nn.sigmoid(gate) * up
        _store_rows(ys_ref, jnp.dot(act.astype(BF16), wd_bf[...], preferred_element_type=F32))

    @pl.when(jnp.logical_not(used))
    def _():
        ys_ref[...] = jnp.zeros(ys_ref.shape, ys_ref.dtype)


def _expert_mlp(tile_expert, n_used, tile_valid, xs, w_gate_up, w_down, layer):
    d = D_MODEL
    n_tiles = xs.shape[0] // (EXPERT_TILE * ROW_CHUNKS)
    f2 = 2 * EXPERT_FF
    tile_rows = EXPERT_TILE * ROW_CHUNKS
    e0 = layer * N_EXPERTS
    grid_spec = pltpu.PrefetchScalarGridSpec(
        num_scalar_prefetch=3,
        grid=(n_tiles,),
        in_specs=[
            pl.BlockSpec(memory_space=pl.ANY),
            pl.BlockSpec((None, d, f2), lambda i, te, nu, tv: (e0 + te[i], 0, 0)),
            pl.BlockSpec((None, EXPERT_FF, d), lambda i, te, nu, tv: (e0 + te[i], 0, 0)),
        ],
        out_specs=pl.BlockSpec((tile_rows, LANES), lambda i, te, nu, tv: (i, 0)),
        scratch_shapes=[pltpu.VMEM((d, f2), BF16), pltpu.VMEM((EXPERT_FF, d), BF16),
                        pltpu.VMEM((XS_SLOTS, tile_rows, LANES), U32), pltpu.SemaphoreType.DMA((XS_SLOTS,))],
    )
    return pl.pallas_call(
        _expert_body,
        grid_spec=grid_spec,
        out_shape=jax.ShapeDtypeStruct(xs.shape, U32),
        compiler_params=_cparams(1),
        name="expert_mlp",
    )(tile_expert, n_used, tile_valid, xs, w_gate_up.reshape(DEPTH * N_EXPERTS, d, f2),
      w_down.reshape(DEPTH * N_EXPERTS, EXPERT_FF, d))


def _moe_combine(wts_ref, x_ref, mod_ref, lng_ref, lnb_ref, ya_ref, yb_ref, token0, n):
    d = D_MODEL
    rows = slice(token0, token0 + n)
    cols = []
    for k in range(2):
        wt = jnp.broadcast_to(wts_ref[k:k + 1, rows], (LANES, n)).T
        cols.append(jnp.concatenate([wt] * (d // LANES), axis=1))
    y = cols[0] * _load_rows(ya_ref, n, token0) + cols[1] * _load_rows(yb_ref, n, token0)
    gt_f = mod_ref[:, 5 * d:6 * d]
    return _layer_norm(DEEPNORM_ALPHA * x_ref[rows, :] + (1.0 + gt_f) * y, lng_ref[...], lnb_ref[...])


def _combine_specs(tm, n_steps, steps_per_batch, layer):
    d = D_MODEL
    return [
        pl.BlockSpec((2, tm), lambda i: (0, i)),
        pl.BlockSpec((tm, d), lambda i: (i, 0)),
        pl.BlockSpec((None, 1, N_MOD * d), lambda i: (layer * MOD_ROWS + i // steps_per_batch, 0, 0)),
        pl.BlockSpec((1, d), lambda i: (0, 0)),
        pl.BlockSpec((1, d), lambda i: (0, 0)),
        pl.BlockSpec((tm * ROW_CHUNKS, LANES), lambda i: (i, 0)),
        pl.BlockSpec((tm * ROW_CHUNKS, LANES), lambda i: (n_steps + i, 0)),
    ]


def _gather_rows_of_tokens(ys, pos2):
    return _sc_gather_rows(ys, pos2.reshape(-1))


def _combine_body(wts_ref, x_ref, mod_ref, lng_ref, lnb_ref, ya_ref, yb_ref, o_ref):
    for r in range(x_ref.shape[0] // SUB_ROWS):
        o_ref[r * SUB_ROWS:(r + 1) * SUB_ROWS, :] = _moe_combine(wts_ref, x_ref, mod_ref, lng_ref, lnb_ref, ya_ref,
                                                                 yb_ref, r * SUB_ROWS, SUB_ROWS)


def _combine_ln(moe, mod_rows, layer, ln_g, ln_b, seq):
    wts, x1, ys, pos2 = moe
    yg = _gather_rows_of_tokens(ys, pos2)
    t, d = x1.shape
    tm = 1024
    n_steps = t // tm
    return pl.pallas_call(
        _combine_body,
        grid=(n_steps,),
        in_specs=_combine_specs(tm, n_steps, seq // tm, layer),
        out_specs=pl.BlockSpec((tm, d), lambda i: (i, 0)),
        out_shape=jax.ShapeDtypeStruct((t, d), F32),
        compiler_params=_cparams(1),
        name="moe_combine_ln",
    )(wts, x1, mod_rows, ln_g.reshape(1, d), ln_b.reshape(1, d), yg, yg)


def _gmlp_body(wts_ref, x1_ref, modp_ref, lng_ref, lnb_ref, ya_ref, yb_ref,
               mod_ref, w_ref, b_ref, g_ref, beta_ref, ws_ref, bs_ref, x2_ref, o_ref, ws_bf, w_bf):
    @pl.when(pl.program_id(0) == 0)
    def _():
        tri = lax.broadcasted_iota(I32, (CHUNK, CHUNK), 0) >= lax.broadcasted_iota(I32, (CHUNK, CHUNK), 1)
        for g in range(N_SGU_GROUPS):
            ws_bf[g] = jnp.where(tri, ws_ref[g], 0.0).astype(BF16)
        w_bf[...] = w_ref[...].astype(BF16)

    d = D_MODEL
    sh = mod_ref[:, 0:d]
    sc = mod_ref[:, d:2 * d]
    sub_rows = x1_ref.shape[0]
    for r in range(x1_ref.shape[0] // sub_rows):
        r0 = r * sub_rows
        x2 = _moe_combine(wts_ref, x1_ref, modp_ref, lng_ref, lnb_ref, ya_ref, yb_ref, r0, sub_rows)
        x2_ref[r0:r0 + sub_rows, :] = x2
        h = x2 * (1.0 + sc) + sh
        z = jnp.dot(h.astype(BF16), w_bf[...], preferred_element_type=F32) + b_ref[...]
        z = 0.5 * z * (1.0 + lax.erf(z * (2.0 ** -0.5)))
        u = z[:, :GMLP_WIDTH]
        v = _layer_norm(z[:, GMLP_WIDTH:], g_ref[...], beta_ref[...]).astype(BF16)
        for ci in range(sub_rows // CHUNK):
            rows = slice(ci * CHUNK, (ci + 1) * CHUNK)
            out_rows = slice(r0 + ci * CHUNK, r0 + (ci + 1) * CHUNK)
            for g in range(N_SGU_GROUPS):
                lanes = slice(g * SGU_GROUP_DIM, (g + 1) * SGU_GROUP_DIM)
                mixed = jnp.dot(ws_bf[g], v[rows, lanes], preferred_element_type=F32) + bs_ref[:, g:g + 1]
                o_ref[out_rows, lanes] = (u[rows, lanes] * mixed).astype(BF16)


def _gmlp_gate(moe, ln_g, ln_b, mod_rows, layer, w_in, b_in, sgu_g, sgu_b, w_s, b_s, seq):
    wts, x1, ys, pos2 = moe
    yg = _gather_rows_of_tokens(ys, pos2)
    t, d = x1.shape
    tm = 512
    steps_per_batch = seq // tm
    n_steps = t // tm
    gw = GMLP_WIDTH
    return pl.pallas_call(
        _gmlp_body,
        grid=(n_steps,),
        in_specs=_combine_specs(tm, n_steps, steps_per_batch, layer - 1) + [
            pl.BlockSpec((None, 1, N_MOD * d), lambda i: (layer * MOD_ROWS + i // steps_per_batch, 0, 0)),
            pl.BlockSpec((d, 2 * gw), lambda i: (0, 0)),
            pl.BlockSpec((1, 2 * gw), lambda i: (0, 0)),
            pl.BlockSpec((1, gw), lambda i: (0, 0)),
            pl.BlockSpec((1, gw), lambda i: (0, 0)),
            pl.BlockSpec((N_SGU_GROUPS, CHUNK, CHUNK), lambda i: (0, 0, 0)),
            pl.BlockSpec((CHUNK, N_SGU_GROUPS), lambda i: (0, 0)),
        ],
        out_specs=[pl.BlockSpec((tm, d), lambda i: (i, 0)), pl.BlockSpec((tm, gw), lambda i: (i, 0))],
        out_shape=[jax.ShapeDtypeStruct((t, d), F32), jax.ShapeDtypeStruct((t, gw), BF16)],
        scratch_shapes=[pltpu.VMEM((N_SGU_GROUPS, CHUNK, CHUNK), BF16), pltpu.VMEM((d, 2 * gw), BF16)],
        compiler_params=_cparams(1),
        name="combine_gmlp_gate",
    )(wts, x1, mod_rows, ln_g.reshape(1, d), ln_b.reshape(1, d), yg, yg,
      mod_rows, w_in, b_in.reshape(1, 2 * gw), sgu_g.reshape(1, gw), sgu_b.reshape(1, gw), w_s, b_s.T)


def _router_params(w_group, b_group, w_expert, b_expert):
    def lanes(group_part, expert_part):
        rows = group_part.shape[0]
        gap = jnp.zeros((rows, EXPERT_COL0 - N_EXPERT_GROUPS), F32)
        tail = jnp.zeros((rows, ROUTER_LANES - EXPERT_COL0 - N_EXPERTS), F32)
        return jnp.concatenate([group_part, gap, expert_part, tail], axis=1)

    return lanes(w_group, w_expert), lanes(b_group[None, :], b_expert[None, :])


def _moe_experts(x1, h2, ids, wts, layer, w_gate_up, w_down):
    t = x1.shape[0]
    n_rows = 2 * t + N_EXPERTS * EXPERT_TILE
    n_tiles = n_rows // EXPERT_TILE
    pos, tile_expert, n_used, tile_valid = _expert_sort(ids, n_tiles)
    pos2 = pos.reshape(2, t)
    xs = _sc_scatter_rows(h2, pos2, n_rows)
    ys = _expert_mlp(tile_expert, n_used, tile_valid, xs, w_gate_up, w_down, layer)
    return wts, x1, ys, pos2


def kernel(x, c, positions, ada_w, ada_b, post_ln_g, post_ln_b, attn_w_qkv, attn_b_qkv, attn_sinks, attn_w_o, attn_b_o, gmlp_w_in, gmlp_b_in, gmlp_sgu_ln_g, gmlp_sgu_ln_b, gmlp_w_s, gmlp_b_s, gmlp_w_out, gmlp_b_out, moe_w_group_router, moe_b_group_router, moe_w_expert_router, moe_b_expert_router, moe_w_gate_up, moe_w_down):
    batch, seq, d = x.shape
    t = batch * seq
    x2 = x.reshape(t, d)
    c_pad = jnp.pad(c, ((0, MOD_ROWS - batch), (0, 0)))
    mod_rows = _adaln_mod(c_pad, ada_w, ada_b).reshape(DEPTH * MOD_ROWS, 1, N_MOD * d)

    moe = None
    for layer in range(DEPTH):
        j = layer // 2
        if layer % 2 == 0:
            if moe is not None:
                x2 = _combine_ln(moe, mod_rows, layer - 1, post_ln_g[layer - 1, 1], post_ln_b[layer - 1, 1], seq)
            qkv = _qkv_rope(x2, mod_rows, layer, positions, attn_w_qkv[j], attn_b_qkv[j], seq)
            mix = _attention(qkv, attn_sinks[j], batch, seq)
            w_out, b_out = attn_w_o[j], attn_b_o[j]
        else:
            x2, mix = _gmlp_gate(moe, post_ln_g[layer - 1, 1], post_ln_b[layer - 1, 1], mod_rows, layer,
                                 gmlp_w_in[j], gmlp_b_in[j], gmlp_sgu_ln_g[j], gmlp_sgu_ln_b[j],
                                 gmlp_w_s[j], gmlp_b_s[j], seq)
            w_out, b_out = gmlp_w_out[j], gmlp_b_out[j]
        w_router, b_router = _router_params(moe_w_group_router[layer], moe_b_group_router[layer],
                                            moe_w_expert_router[layer], moe_b_expert_router[layer])
        x1, h2, ids, wts = _proj_ln_router(mix, x2, mod_rows, layer, w_out, b_out, post_ln_g[layer, 0],
                                           post_ln_b[layer, 0], w_router, b_router, seq)
        moe = _moe_experts(x1, h2, ids, wts, layer, moe_w_gate_up, moe_w_down)
    x2 = _combine_ln(moe, mod_rows, DEPTH - 1, post_ln_g[DEPTH - 1, 1], post_ln_b[DEPTH - 1, 1], seq)
    return x2.reshape(batch, seq, d)
```

```python
import functools

import jax
import jax.numpy as jnp
from jax import lax
from jax.experimental import pallas as pl
from jax.experimental.pallas import tpu as pltpu
from jax.experimental.pallas import tpu_sc as plsc

F32 = jnp.float32
BF16 = jnp.bfloat16
I32 = jnp.int32

D_MODEL = 1024
DEPTH = 2
HEAD_DIM = 64
N_Q_HEADS = 16
N_KV_HEADS = 4
GQA_GROUP = N_Q_HEADS // N_KV_HEADS
WINDOW = 128
ROPE_THETA = 10000.0
Q_WIDTH = N_Q_HEADS * HEAD_DIM
KV_WIDTH = N_KV_HEADS * HEAD_DIM
QKV_WIDTH = Q_WIDTH + 2 * KV_WIDTH
CHUNK = 128
GMLP_WIDTH = 2 * D_MODEL
N_SGU_GROUPS = 8
SGU_GROUP_DIM = GMLP_WIDTH // N_SGU_GROUPS
N_EXPERT_GROUPS = 4
EXPERTS_PER_GROUP = 8
N_EXPERTS = N_EXPERT_GROUPS * EXPERTS_PER_GROUP
EXPERT_FF = D_MODEL // 4
N_MOD = 6
DEEPNORM_ALPHA = (2.0 * DEPTH) ** 0.25
LN_EPS = 1e-5

LANES = 128
MOD_ROWS = 8
ROUTER_LANES = 128
EXPERT_COL0 = 8
SORT_CHUNK = 256
EXPERT_TILE = 512
SUB_ROWS = 256
VMEM_LIMIT = 56 * 1024 * 1024
SC_CORES = 2
SC_WORKERS = 32
SC_CHUNK = 32
SC_SLOTS = 6


def _cparams(n_axes, vmem=VMEM_LIMIT):
    return pltpu.CompilerParams(dimension_semantics=("arbitrary",) * n_axes, vmem_limit_bytes=vmem)


U32 = jnp.uint32
ROW_CHUNKS = D_MODEL // 2 // LANES


def _store_rows(ref, val, token0=0):
    n = val.shape[0]
    half = D_MODEL // 2
    words = pltpu.pack_elementwise([val[:, :half], val[:, half:]], packed_dtype=BF16)
    for c in range(ROW_CHUNKS):
        ref[pl.ds(token0 * ROW_CHUNKS + c, n, stride=ROW_CHUNKS), :] = words[:, c * LANES:(c + 1) * LANES]


def _load_rows(ref, n, token0=0):
    words = jnp.concatenate([ref[pl.ds(token0 * ROW_CHUNKS + c, n, stride=ROW_CHUNKS), :]
                             for c in range(ROW_CHUNKS)], axis=1)
    lo = lax.bitcast_convert_type(words << 16, F32)
    hi = lax.bitcast_convert_type(words & jnp.uint32(0xFFFF0000), F32)
    return jnp.concatenate([lo, hi], axis=1)


def _layer_norm(r, g, b):
    mu = jnp.mean(r, axis=-1, keepdims=True)
    d = r - mu
    var = jnp.mean(d * d, axis=-1, keepdims=True)
    return d * lax.rsqrt(var + LN_EPS) * g + b


def _mod_body(c_ref, w_ref, b_ref, o_ref):
    c = c_ref[...]
    ca = c * jax.nn.sigmoid(c)
    o_ref[...] = jnp.dot(ca.astype(BF16), w_ref[...].astype(BF16), preferred_element_type=F32) + b_ref[...]


def _adaln_mod(c_pad, ada_w, ada_b):
    tn = 1536
    n_out = N_MOD * D_MODEL
    return pl.pallas_call(
        _mod_body,
        grid=(DEPTH, n_out // tn),
        in_specs=[
            pl.BlockSpec((MOD_ROWS, D_MODEL), lambda l, j: (0, 0)),
            pl.BlockSpec((None, D_MODEL, tn), lambda l, j: (l, 0, j)),
            pl.BlockSpec((None, 1, tn), lambda l, j: (l, 0, j)),
        ],
        out_specs=pl.BlockSpec((None, MOD_ROWS, tn), lambda l, j: (l, 0, j)),
        out_shape=jax.ShapeDtypeStruct((DEPTH, MOD_ROWS, n_out), F32),
        compiler_params=_cparams(2),
        name="adaln_mod",
    )(c_pad, ada_w, ada_b.reshape(DEPTH, 1, n_out))


def _qkv_body(x_ref, mod_ref, pos_ref, w_ref, b_ref, invf_ref, o_ref, wbf_ref):
    @pl.when(pl.program_id(0) == 0)
    def _():
        wbf_ref[...] = w_ref[...].astype(BF16)

    tm = x_ref.shape[0]
    sh = mod_ref[:, 0:D_MODEL]
    sc = mod_ref[:, D_MODEL:2 * D_MODEL]
    h = x_ref[...] * (1.0 + sc) + sh
    qkv = jnp.dot(h.astype(BF16), wbf_ref[...], preferred_element_type=F32) + b_ref[...]

    ang = invf_ref[...] * pos_ref[...].astype(F32)
    c = jnp.cos(ang)
    s = jnp.sin(ang)
    ct = jnp.concatenate([c, c, c, c], axis=0).T
    st = jnp.concatenate([-s, s, -s, s], axis=0).T
    lane = lax.broadcasted_iota(I32, (tm, LANES), 1)
    first_half = (lane & (HEAD_DIM // 2)) == 0
    n_rope = (Q_WIDTH + KV_WIDTH) // LANES
    for j in range(n_rope):
        blk = qkv[:, j * LANES:(j + 1) * LANES]
        rot = jnp.where(first_half, pltpu.roll(blk, LANES - HEAD_DIM // 2, 1), pltpu.roll(blk, HEAD_DIM // 2, 1))
        r = blk * ct + rot * st
        if j < Q_WIDTH // LANES:
            r = r * (HEAD_DIM ** -0.5)
        o_ref[:, j * LANES:(j + 1) * LANES] = r.astype(BF16)
    o_ref[:, Q_WIDTH + KV_WIDTH:] = qkv[:, Q_WIDTH + KV_WIDTH:].astype(BF16)


def _qkv_rope(x2, mod_rows, layer, positions, w_qkv, b_qkv, seq):
    t = x2.shape[0]
    tm = 1024
    steps_per_batch = seq // tm
    inv_freq = ROPE_THETA ** (-jnp.arange(0, HEAD_DIM, 2, dtype=F32) / HEAD_DIM)
    return pl.pallas_call(
        _qkv_body,
        grid=(t // tm,),
        in_specs=[
            pl.BlockSpec((tm, D_MODEL), lambda i: (i, 0)),
            pl.BlockSpec((None, 1, N_MOD * D_MODEL), lambda i: (layer * MOD_ROWS + i // steps_per_batch, 0, 0)),
            pl.BlockSpec((None, 1, tm), lambda i: (i, 0, 0)),
            pl.BlockSpec((D_MODEL, QKV_WIDTH), lambda i: (0, 0)),
            pl.BlockSpec((1, QKV_WIDTH), lambda i: (0, 0)),
            pl.BlockSpec((HEAD_DIM // 2, 1), lambda i: (0, 0)),
        ],
        out_specs=pl.BlockSpec((tm, QKV_WIDTH), lambda i: (i, 0)),
        out_shape=jax.ShapeDtypeStruct((t, QKV_WIDTH), BF16),
        scratch_shapes=[pltpu.VMEM((D_MODEL, QKV_WIDTH), BF16)],
        compiler_params=_cparams(1),
        name="qkv_rope",
    )(x2, mod_rows, positions.reshape(t // tm, 1, tm), w_qkv, b_qkv.reshape(1, QKV_WIDTH),
      inv_freq.reshape(HEAD_DIM // 2, 1))


BF16_ROWS = 16


def _attn_prepare(kv, kab_ref, vx_ref, slot):
    kv = kv.astype(F32)
    low = lax.broadcasted_iota(I32, (WINDOW, LANES), 1) < HEAD_DIM
    ones = jnp.ones((WINDOW, LANES), F32)
    for g in range(N_KV_HEADS):
        for part, ref in ((0, None), (KV_WIDTH, vx_ref)):
            tile = kv[:, part + (g // 2) * LANES:part + (g // 2 + 1) * LANES]
            other = pltpu.roll(tile, HEAD_DIM, 1)
            in_low, in_high = (tile, other) if g % 2 == 0 else (other, tile)
            if ref is None:
                kab_ref[slot, 2 * g] = jnp.where(low, in_low, 0.0).astype(BF16)
                kab_ref[slot, 2 * g + 1] = jnp.where(low, 0.0, in_high).astype(BF16)
            else:
                both = jnp.where(low, in_low, in_high)
                vx_ref[slot, g] = jnp.concatenate([both, ones], axis=1).astype(BF16)


def _attn_block(sink_ref, q, kab_ref, vx_ref, s_ref, p_ref, prev, cur, first_block):
    for g in range(N_KV_HEADS):
        q_pair = jnp.concatenate([q[:, (2 * g) * LANES:(2 * g + 1) * LANES],
                                  q[:, (2 * g + 1) * LANES:(2 * g + 2) * LANES]], axis=0)
        for a in range(2):
            kband = jnp.concatenate([kab_ref[prev, 2 * g + a], kab_ref[cur, 2 * g + a]], axis=0)
            s = lax.dot_general(q_pair, kband, (((1,), (1,)), ((), ())), preferred_element_type=F32)
            s_ref[GQA_GROUP * g + a] = s[:WINDOW]
            s_ref[GQA_GROUP * g + 2 + a] = s[WINDOW:]
    qi = lax.broadcasted_iota(I32, (WINDOW, 2 * WINDOW), 0) + WINDOW
    kj = lax.broadcasted_iota(I32, (WINDOW, 2 * WINDOW), 1)
    mask = (kj <= qi) & (kj > qi - WINDOW) & ((kj >= WINDOW) | jnp.logical_not(first_block))
    key0 = lax.broadcasted_iota(I32, (1, 2 * WINDOW), 1) == 0
    for h in range(N_Q_HEADS):
        s = jnp.where(mask, s_ref[h], jnp.where(key0, sink_ref[h], -jnp.inf))
        m = jnp.max(s, axis=-1, keepdims=True)
        p_ref[h] = jnp.exp(s - m).astype(BF16)
    low = lax.broadcasted_iota(I32, (WINDOW, LANES), 1) < HEAD_DIM
    sink_row = ((lax.broadcasted_iota(I32, (BF16_ROWS, 2 * LANES), 0) == 0)
                & (lax.broadcasted_iota(I32, (BF16_ROWS, 2 * LANES), 1) < LANES))
    out_tiles = []
    for g in range(N_KV_HEADS):
        v_prev = vx_ref[prev, g]
        v_head = jnp.where(sink_row, 0.0, v_prev[:BF16_ROWS].astype(F32)).astype(BF16)
        vband = jnp.concatenate([v_head, v_prev[BF16_ROWS:], vx_ref[cur, g]], axis=0)
        p4 = p_ref[GQA_GROUP * g:GQA_GROUP * (g + 1)].reshape(GQA_GROUP * WINDOW, 2 * WINDOW)
        o4 = jnp.dot(p4, vband, preferred_element_type=F32)
        heads = []
        for j in range(GQA_GROUP):
            blk = o4[j * WINDOW:(j + 1) * WINDOW]
            heads.append(blk[:, :LANES] / blk[:, LANES:])
        out_tiles.append(jnp.where(low, heads[0], heads[1]))
        out_tiles.append(jnp.where(low, heads[2], heads[3]))
    return jnp.concatenate(out_tiles, axis=1).astype(BF16)


def _attn_body(sink_ref, q_ref, kv_ref, o_ref, kab_ref, vx_ref, s_ref, p_ref):
    n = pl.program_id(1)

    @pl.when(n == 0)
    def _():
        kab_ref[1] = jnp.zeros(kab_ref.shape[1:], kab_ref.dtype)
        half = (N_KV_HEADS, WINDOW, LANES)
        vx_ref[1] = jnp.concatenate([jnp.zeros(half, BF16), jnp.ones(half, BF16)], axis=-1)

    scratch = (kab_ref, vx_ref, s_ref, p_ref)
    for blk in range(q_ref.shape[0] // WINDOW):
        rows = slice(blk * WINDOW, (blk + 1) * WINDOW)
        cur = blk % 2
        _attn_prepare(kv_ref[rows, :], kab_ref, vx_ref, cur)
        o_ref[rows, :] = _attn_block(sink_ref, q_ref[rows, :], *scratch, 1 - cur, cur,
                                     (n == 0) if blk == 0 else False)


def _attention(qkv, sinks, batch, seq):
    t = qkv.shape[0]
    tq = 4 * WINDOW
    steps = seq // tq
    kv_col = Q_WIDTH // (2 * KV_WIDTH)
    return pl.pallas_call(
        _attn_body,
        grid=(batch, steps),
        in_specs=[
            pl.BlockSpec(memory_space=pltpu.SMEM),
            pl.BlockSpec((tq, Q_WIDTH), lambda b, n: (b * steps + n, 0)),
            pl.BlockSpec((tq, 2 * KV_WIDTH), lambda b, n: (b * steps + n, kv_col)),
        ],
        out_specs=pl.BlockSpec((tq, Q_WIDTH), lambda b, n: (b * steps + n, 0)),
        out_shape=jax.ShapeDtypeStruct((t, Q_WIDTH), BF16),
        scratch_shapes=[
            pltpu.VMEM((2, 2 * N_KV_HEADS, WINDOW, LANES), BF16),
            pltpu.VMEM((2, N_KV_HEADS, WINDOW, 2 * LANES), BF16),
            pltpu.VMEM((N_Q_HEADS, WINDOW, 2 * WINDOW), F32),
            pltpu.VMEM((N_Q_HEADS, WINDOW, 2 * WINDOW), BF16),
        ],
        compiler_params=_cparams(2),
        name="swa_attention",
    )(sinks, qkv, qkv)


def _route(lt):
    tm = lt.shape[1]
    row = lax.broadcasted_iota(I32, (EXPERTS_PER_GROUP, tm), 0)
    neg = -jnp.inf
    gl = jnp.where(row < N_EXPERT_GROUPS, lt[0:EXPERTS_PER_GROUP], neg)
    gm = jnp.max(gl, axis=0, keepdims=True)
    g_p = 1.0 / jnp.sum(jnp.exp(gl - gm), axis=0, keepdims=True)
    g_idx = jnp.min(jnp.where(gl == gm, row, EXPERTS_PER_GROUP), axis=0, keepdims=True)
    sel = lt[EXPERT_COL0 + (N_EXPERT_GROUPS - 1) * EXPERTS_PER_GROUP:EXPERT_COL0 + N_EXPERTS]
    for g in range(N_EXPERT_GROUPS - 2, -1, -1):
        lo = EXPERT_COL0 + g * EXPERTS_PER_GROUP
        sel = jnp.where(g_idx == g, lt[lo:lo + EXPERTS_PER_GROUP], sel)
    v1 = jnp.max(sel, axis=0, keepdims=True)
    i1 = jnp.min(jnp.where(sel == v1, row, EXPERTS_PER_GROUP), axis=0, keepdims=True)
    sel2 = jnp.where(row == i1, neg, sel)
    v2 = jnp.max(sel2, axis=0, keepdims=True)
    i2 = jnp.min(jnp.where(sel2 == v2, row, EXPERTS_PER_GROUP), axis=0, keepdims=True)
    e2 = jnp.exp(v2 - v1)
    w1 = g_p / (1.0 + e2)
    w2 = g_p * e2 / (1.0 + e2)
    base = g_idx * EXPERTS_PER_GROUP
    return base + i1, base + i2, w1, w2


def _proj_body(o_ref, x_ref, mod_ref, w_ref, b_ref, lng_ref, lnb_ref, wr_ref, br_ref,
               x1_ref, h2_ref, ids_ref, wts_ref, wbf_ref):
    @pl.when(pl.program_id(0) == 0)
    def _():
        wbf_ref[...] = w_ref[...].astype(BF16)

    d = D_MODEL
    gt_m = mod_ref[:, 2 * d:3 * d]
    sh_f = mod_ref[:, 3 * d:4 * d]
    sc_f = mod_ref[:, 4 * d:5 * d]
    wr = wr_ref[...].astype(BF16)
    for r in range(x_ref.shape[0] // SUB_ROWS):
        rows = slice(r * SUB_ROWS, (r + 1) * SUB_ROWS)
        y = jnp.dot(o_ref[rows, :], wbf_ref[...], preferred_element_type=F32) + b_ref[...]
        x1 = _layer_norm(DEEPNORM_ALPHA * x_ref[rows, :] + (1.0 + gt_m) * y, lng_ref[...], lnb_ref[...])
        x1_ref[rows, :] = x1
        h2 = x1 * (1.0 + sc_f) + sh_f
        _store_rows(h2_ref, h2, r * SUB_ROWS)
        logits = jnp.dot(h2.astype(BF16), wr, preferred_element_type=F32) + br_ref[...]
        ea, eb, wa, wb = _route(logits.T)
        ids_ref[0:1, rows] = ea
        ids_ref[1:2, rows] = eb
        wts_ref[0:1, rows] = wa
        wts_ref[1:2, rows] = wb


def _proj_ln_router(o, x2, mod_rows, layer, w, b, ln_g, ln_b, w_router, b_router, seq):
    t, k = o.shape
    tm = 1024
    steps_per_batch = seq // tm
    d = D_MODEL
    return pl.pallas_call(
        _proj_body,
        grid=(t // tm,),
        in_specs=[
            pl.BlockSpec((tm, k), lambda i: (i, 0)),
            pl.BlockSpec((tm, d), lambda i: (i, 0)),
            pl.BlockSpec((None, 1, N_MOD * d), lambda i: (layer * MOD_ROWS + i // steps_per_batch, 0, 0)),
            pl.BlockSpec((k, d), lambda i: (0, 0), pipeline_mode=pl.Buffered(1)),
            pl.BlockSpec((1, d), lambda i: (0, 0)),
            pl.BlockSpec((1, d), lambda i: (0, 0)),
            pl.BlockSpec((1, d), lambda i: (0, 0)),
            pl.BlockSpec((d, ROUTER_LANES), lambda i: (0, 0)),
            pl.BlockSpec((1, ROUTER_LANES), lambda i: (0, 0)),
        ],
        out_specs=[
            pl.BlockSpec((tm, d), lambda i: (i, 0)),
            pl.BlockSpec((tm * ROW_CHUNKS, LANES), lambda i: (i, 0)),
            pl.BlockSpec((2, tm), lambda i: (0, i)),
            pl.BlockSpec((2, tm), lambda i: (0, i)),
        ],
        out_shape=[
            jax.ShapeDtypeStruct((t, d), F32),
            jax.ShapeDtypeStruct((t * ROW_CHUNKS, LANES), U32),
            jax.ShapeDtypeStruct((2, t), I32),
            jax.ShapeDtypeStruct((2, t), F32),
        ],
        scratch_shapes=[pltpu.VMEM((k, d), BF16)],
        compiler_params=_cparams(1),
        name="proj_ln_router",
    )(o, x2, mod_rows, w, b.reshape(1, d), ln_g.reshape(1, d), ln_b.reshape(1, d), w_router, b_router)


def _sort_body(ids_ref, pos_ref, te_ref, nused_ref, tv_ref, rank_ref):
    n_rows = ids_ref.shape[0]
    c = SORT_CHUNK
    erow = lax.broadcasted_iota(I32, (N_EXPERTS, c), 0)
    tri = (lax.broadcasted_iota(I32, (c, c), 0) <= lax.broadcasted_iota(I32, (c, c), 1)).astype(BF16)

    def rank_step(r, carry):
        onehot = erow == ids_ref[pl.ds(r, 1), :]
        pref = jnp.dot(onehot.astype(BF16), tri, preferred_element_type=F32)
        rank = jnp.sum(jnp.where(onehot, pref + carry, 0.0), axis=0, keepdims=True) - 1.0
        rank_ref[pl.ds(r, 1), :] = rank
        return carry + pref[:, c - 1:c]

    counts = lax.fori_loop(0, n_rows, rank_step, jnp.zeros((N_EXPERTS, 1), F32), unroll=8)
    n_tile = jnp.floor((counts + (EXPERT_TILE - 1)) * (1.0 / EXPERT_TILE))
    low = (lax.broadcasted_iota(I32, (N_EXPERTS, N_EXPERTS), 1)
           <= lax.broadcasted_iota(I32, (N_EXPERTS, N_EXPERTS), 0)).astype(BF16)
    cum = jnp.dot(low, jnp.broadcast_to(n_tile, (N_EXPERTS, LANES)).astype(BF16),
                  preferred_element_type=F32)[:, 0:1]
    row_off = (cum - n_tile) * EXPERT_TILE

    def pos_step(r, _):
        onehot = erow == ids_ref[pl.ds(r, 1), :]
        off = jnp.sum(jnp.where(onehot, row_off, 0.0), axis=0, keepdims=True)
        pos_ref[pl.ds(r, 1), :] = (off + rank_ref[pl.ds(r, 1), :]).astype(I32)
        return 0

    lax.fori_loop(0, n_rows, pos_step, 0, unroll=8)
    total = jnp.max(cum, axis=0, keepdims=True)
    n_lanes = te_ref.shape[1]
    tile = jnp.minimum(lax.broadcasted_iota(I32, (N_EXPERTS, n_lanes), 1).astype(F32), total - 1.0)
    te_ref[...] = jnp.sum(jnp.where(cum <= tile, 1.0, 0.0), axis=0, keepdims=True).astype(I32)
    nused_ref[...] = jnp.broadcast_to(total, nused_ref.shape).astype(I32)
    tile_f = lax.broadcasted_iota(I32, (N_EXPERTS, n_lanes), 1).astype(F32)
    first = cum - n_tile
    rows_left = jnp.clip(counts - (tile_f - first) * EXPERT_TILE, 0.0, float(EXPERT_TILE))
    owns = (first <= tile_f) & (tile_f < cum)
    tv_ref[...] = jnp.sum(jnp.where(owns, rows_left, 0.0), axis=0, keepdims=True).astype(I32)


def _expert_sort(ids, n_tiles):
    n_assign = ids.shape[0] * ids.shape[1]
    n_rows = n_assign // SORT_CHUNK
    te_lanes = -(-n_tiles // LANES) * LANES
    pos, te, nused, tv = pl.pallas_call(
        _sort_body,
        grid=(1,),
        in_specs=[pl.BlockSpec((n_rows, SORT_CHUNK), lambda i: (0, 0))],
        out_specs=[
            pl.BlockSpec((n_rows, SORT_CHUNK), lambda i: (0, 0)),
            pl.BlockSpec((1, te_lanes), lambda i: (0, 0)),
            pl.BlockSpec((1, LANES), lambda i: (0, 0)),
            pl.BlockSpec((1, te_lanes), lambda i: (0, 0)),
        ],
        out_shape=[
            jax.ShapeDtypeStruct((n_rows, SORT_CHUNK), I32),
            jax.ShapeDtypeStruct((1, te_lanes), I32),
            jax.ShapeDtypeStruct((1, LANES), I32),
            jax.ShapeDtypeStruct((1, te_lanes), I32),
        ],
        scratch_shapes=[pltpu.VMEM((n_rows, SORT_CHUNK), F32)],
        compiler_params=_cparams(1),
        name="expert_sort",
    )(ids.reshape(n_rows, SORT_CHUNK))
    return pos, te[0, :n_tiles], nused[0, :1], tv[0, :n_tiles]


def _sc_mesh():
    return plsc.VectorSubcoreMesh(core_axis_name="c", subcore_axis_name="s", num_cores=SC_CORES,
                                  num_subcores=SC_WORKERS // SC_CORES)


def _sc_ring(n_chunks, read, write):
    reads, writes = {}, {}
    for j in range(min(SC_SLOTS - 1, n_chunks)):
        reads[j] = read(j)
    for j in range(n_chunks):
        for cp in reads.pop(j):
            cp.wait()
        nxt = j + SC_SLOTS - 1
        if nxt < n_chunks:
            for cp in writes.pop(nxt - SC_SLOTS, []):
                cp.wait()
            reads[nxt] = read(nxt)
        writes[j] = write(j)
    for cps in writes.values():
        for cp in cps:
            cp.wait()


def _sc_scatter_rows(src, pos2, n_rows):
    t = pos2.shape[1]
    src3 = src.reshape(t, ROW_CHUNKS, LANES)
    per_worker = t // SC_WORKERS
    n_chunks = per_worker // SC_CHUNK
    idx = pos2.reshape(2, SC_WORKERS, n_chunks, SC_CHUNK).transpose(1, 0, 2, 3)
    idx = idx.reshape(SC_WORKERS, 2 * n_chunks, SC_CHUNK)

    @functools.partial(
        pl.kernel, mesh=_sc_mesh(),
        out_type=jax.ShapeDtypeStruct((n_rows, ROW_CHUNKS, LANES), src.dtype),
        scratch_types=[
            pltpu.VMEM((2 * n_chunks, SC_CHUNK), I32),
            pltpu.VMEM((SC_SLOTS, SC_CHUNK, ROW_CHUNKS, LANES), src.dtype),
            pltpu.SemaphoreType.DMA((SC_SLOTS,)),
            pltpu.SemaphoreType.DMA((SC_SLOTS,)),
        ],
        name="sc_scatter_rows",
    )
    def scatter(src_hbm, idx_hbm, out_hbm, idx_v, rows_v, rsem, wsem):
        wid = lax.axis_index("s") * SC_CORES + lax.axis_index("c")
        pltpu.sync_copy(idx_hbm.at[wid], idx_v)

        def read(j):
            b = j % SC_SLOTS
            return [pltpu.async_copy(src_hbm.at[pl.ds(wid * per_worker + j * SC_CHUNK, SC_CHUNK)], rows_v.at[b],
                                     rsem.at[b])]

        def write(j):
            b = j % SC_SLOTS
            return [pltpu.async_copy(rows_v.at[b], out_hbm.at[idx_v.at[k * n_chunks + j]], wsem.at[b])
                    for k in range(2)]

        _sc_ring(n_chunks, read, write)

    return scatter(src3, idx).reshape(n_rows * ROW_CHUNKS, LANES)


def _sc_gather_rows(table, idx):
    n = idx.shape[0]
    table3 = table.reshape(-1, ROW_CHUNKS, LANES)
    per_worker = n // SC_WORKERS
    n_chunks = per_worker // SC_CHUNK

    @functools.partial(
        pl.kernel, mesh=_sc_mesh(),
        out_type=jax.ShapeDtypeStruct((n, ROW_CHUNKS, LANES), table.dtype),
        scratch_types=[
            pltpu.VMEM((n_chunks, SC_CHUNK), I32),
            pltpu.VMEM((SC_SLOTS, SC_CHUNK, ROW_CHUNKS, LANES), table.dtype),
            pltpu.SemaphoreType.DMA((SC_SLOTS,)),
            pltpu.SemaphoreType.DMA((SC_SLOTS,)),
        ],
        name="sc_gather_rows",
    )
    def gather(table_hbm, idx_hbm, out_hbm, idx_v, rows_v, rsem, wsem):
        wid = lax.axis_index("s") * SC_CORES + lax.axis_index("c")
        pltpu.sync_copy(idx_hbm.at[wid], idx_v)

        def read(j):
            b = j % SC_SLOTS
            return [pltpu.async_copy(table_hbm.at[idx_v.at[j]], rows_v.at[b], rsem.at[b])]

        def write(j):
            b = j % SC_SLOTS
            return [pltpu.async_copy(rows_v.at[b], out_hbm.at[pl.ds(wid * per_worker + j * SC_CHUNK, SC_CHUNK)],
                                     wsem.at[b])]

        _sc_ring(n_chunks, read, write)

    out = gather(table3, idx.reshape(SC_WORKERS, n_chunks, SC_CHUNK))
    return out.reshape(n * ROW_CHUNKS, LANES)


XS_SLOTS = 3


def _expert_body(te_ref, nused_ref, tv_ref, xs_hbm, wgu_ref, wd_ref, ys_ref, wgu_bf, wd_bf, xbuf, xsem):
    i = pl.program_id(0)
    n_used = nused_ref[0]
    used = i < n_used
    prev = te_ref[jnp.maximum(i - 1, 0)]
    tile_rows = EXPERT_TILE * ROW_CHUNKS

    def fetch(tile):
        slot = lax.rem(tile, XS_SLOTS)
        r0 = pl.multiple_of(tile * tile_rows, tile_rows)
        return pltpu.make_async_copy(xs_hbm.at[pl.ds(r0, tile_rows)], xbuf.at[slot], xsem.at[slot])

    @pl.when((i == 0) & (n_used > 0))
    def _():
        fetch(0).start()

    @pl.when((i == 0) & (n_used > 1))
    def _():
        fetch(1).start()

    @pl.when(i + 2 < n_used)
    def _():
        fetch(i + 2).start()

    @pl.when(used & ((i == 0) | (te_ref[i] != prev)))
    def _():
        wgu_bf[...] = wgu_ref[...].astype(BF16)
        wd_bf[...] = wd_ref[...].astype(BF16)

    @pl.when(used)
    def _():
        fetch(i).wait()

    def expert_rows(n):
        live = lax.broadcasted_iota(I32, (n, 1), 0) < tv_ref[i]
        xs = jnp.where(live, _load_rows(xbuf.at[lax.rem(i, XS_SLOTS)], n), 0.0).astype(BF16)
        gu = jnp.dot(xs, wgu_bf[...], preferred_element_type=F32)
        gate = gu[:, :EXPERT_FF]
        up = gu[:, EXPERT_FF:]
        act = gate * jax.nn.sigmoid(gate) * up
        _store_rows(ys_ref, jnp.dot(act.astype(BF16), wd_bf[...], preferred_element_type=F32))

    half = EXPERT_TILE // 2
    half_full = tv_ref[i] <= half

    @pl.when(used & jnp.logical_not(half_full))
    def _():
        expert_rows(EXPERT_TILE)

    @pl.when(used & half_full)
    def _():
        expert_rows(half)
        ys_ref[half * ROW_CHUNKS:, :] = jnp.zeros((half * ROW_CHUNKS, LANES), ys_ref.dtype)

    @pl.when(jnp.logical_not(used))
    def _():
        ys_ref[...] = jnp.zeros(ys_ref.shape, ys_ref.dtype)


def _expert_mlp(tile_expert, n_used, tile_valid, xs, w_gate_up, w_down, layer):
    d = D_MODEL
    n_tiles = xs.shape[0] // (EXPERT_TILE * ROW_CHUNKS)
    f2 = 2 * EXPERT_FF
    tile_rows = EXPERT_TILE * ROW_CHUNKS
    e0 = layer * N_EXPERTS
    grid_spec = pltpu.PrefetchScalarGridSpec(
        num_scalar_prefetch=3,
        grid=(n_tiles,),
        in_specs=[
            pl.BlockSpec(memory_space=pl.ANY),
            pl.BlockSpec((None, d, f2), lambda i, te, nu, tv: (e0 + te[i], 0, 0)),
            pl.BlockSpec((None, EXPERT_FF, d), lambda i, te, nu, tv: (e0 + te[i], 0, 0)),
        ],
        out_specs=pl.BlockSpec((tile_rows, LANES), lambda i, te, nu, tv: (i, 0)),
        scratch_shapes=[pltpu.VMEM((d, f2), BF16), pltpu.VMEM((EXPERT_FF, d), BF16),
                        pltpu.VMEM((XS_SLOTS, tile_rows, LANES), U32), pltpu.SemaphoreType.DMA((XS_SLOTS,))],
    )
    return pl.pallas_call(
        _expert_body,
        grid_spec=grid_spec,
        out_shape=jax.ShapeDtypeStruct(xs.shape, U32),
        compiler_params=_cparams(1),
        name="expert_mlp",
    )(tile_expert, n_used, tile_valid, xs, w_gate_up.reshape(DEPTH * N_EXPERTS, d, f2),
      w_down.reshape(DEPTH * N_EXPERTS, EXPERT_FF, d))


def _moe_combine(wts_ref, x_ref, mod_ref, lng_ref, lnb_ref, ya_ref, yb_ref, token0, n):
    d = D_MODEL
    rows = slice(token0, token0 + n)
    cols = []
    for k in range(2):
        wt = jnp.broadcast_to(wts_ref[k:k + 1, rows], (LANES, n)).T
        cols.append(jnp.concatenate([wt] * (d // LANES), axis=1))
    y = cols[0] * _load_rows(ya_ref, n, token0) + cols[1] * _load_rows(yb_ref, n, token0)
    gt_f = mod_ref[:, 5 * d:6 * d]
    return _layer_norm(DEEPNORM_ALPHA * x_ref[rows, :] + (1.0 + gt_f) * y, lng_ref[...], lnb_ref[...])


def _combine_specs(tm, n_steps, steps_per_batch, layer):
    d = D_MODEL
    return [
        pl.BlockSpec((2, tm), lambda i: (0, i)),
        pl.BlockSpec((tm, d), lambda i: (i, 0)),
        pl.BlockSpec((None, 1, N_MOD * d), lambda i: (layer * MOD_ROWS + i // steps_per_batch, 0, 0)),
        pl.BlockSpec((1, d), lambda i: (0, 0)),
        pl.BlockSpec((1, d), lambda i: (0, 0)),
        pl.BlockSpec((tm * ROW_CHUNKS, LANES), lambda i: (i, 0)),
        pl.BlockSpec((tm * ROW_CHUNKS, LANES), lambda i: (n_steps + i, 0)),
    ]


def _gather_rows_of_tokens(ys, pos2):
    return _sc_gather_rows(ys, pos2.reshape(-1))


def _combine_body(wts_ref, x_ref, mod_ref, lng_ref, lnb_ref, ya_ref, yb_ref, o_ref):
    for r in range(x_ref.shape[0] // SUB_ROWS):
        o_ref[r * SUB_ROWS:(r + 1) * SUB_ROWS, :] = _moe_combine(wts_ref, x_ref, mod_ref, lng_ref, lnb_ref, ya_ref,
                                                                 yb_ref, r * SUB_ROWS, SUB_ROWS)


def _combine_ln(moe, mod_rows, layer, ln_g, ln_b, seq):
    wts, x1, ys, pos2 = moe
    yg = _gather_rows_of_tokens(ys, pos2)
    t, d = x1.shape
    tm = 1024
    n_steps = t // tm
    return pl.pallas_call(
        _combine_body,
        grid=(n_steps,),
        in_specs=_combine_specs(tm, n_steps, seq // tm, layer),
        out_specs=pl.BlockSpec((tm, d), lambda i: (i, 0)),
        out_shape=jax.ShapeDtypeStruct((t, d), F32),
        compiler_params=_cparams(1),
        name="moe_combine_ln",
    )(wts, x1, mod_rows, ln_g.reshape(1, d), ln_b.reshape(1, d), yg, yg)


def _gmlp_body(wts_ref, x1_ref, modp_ref, lng_ref, lnb_ref, ya_ref, yb_ref,
               mod_ref, w_ref, b_ref, g_ref, beta_ref, ws_ref, bs_ref, x2_ref, o_ref, ws_bf, w_bf):
    @pl.when(pl.program_id(0) == 0)
    def _():
        tri = lax.broadcasted_iota(I32, (CHUNK, CHUNK), 0) >= lax.broadcasted_iota(I32, (CHUNK, CHUNK), 1)
        for g in range(N_SGU_GROUPS):
            ws_bf[g] = jnp.where(tri, ws_ref[g], 0.0).astype(BF16)
        w_bf[...] = w_ref[...].astype(BF16)

    d = D_MODEL
    sh = mod_ref[:, 0:d]
    sc = mod_ref[:, d:2 * d]
    sub_rows = x1_ref.shape[0]
    for r in range(x1_ref.shape[0] // sub_rows):
        r0 = r * sub_rows
        x2 = _moe_combine(wts_ref, x1_ref, modp_ref, lng_ref, lnb_ref, ya_ref, yb_ref, r0, sub_rows)
        x2_ref[r0:r0 + sub_rows, :] = x2
        h = x2 * (1.0 + sc) + sh
        z = jnp.dot(h.astype(BF16), w_bf[...], preferred_element_type=F32) + b_ref[...]
        z = 0.5 * z * (1.0 + lax.erf(z * (2.0 ** -0.5)))
        u = z[:, :GMLP_WIDTH]
        v = _layer_norm(z[:, GMLP_WIDTH:], g_ref[...], beta_ref[...]).astype(BF16)
        for ci in range(sub_rows // CHUNK):
            rows = slice(ci * CHUNK, (ci + 1) * CHUNK)
            out_rows = slice(r0 + ci * CHUNK, r0 + (ci + 1) * CHUNK)
            for g in range(N_SGU_GROUPS):
                lanes = slice(g * SGU_GROUP_DIM, (g + 1) * SGU_GROUP_DIM)
                mixed = jnp.dot(ws_bf[g], v[rows, lanes], preferred_element_type=F32) + bs_ref[:, g:g + 1]
                o_ref[out_rows, lanes] = (u[rows, lanes] * mixed).astype(BF16)


def _gmlp_gate(moe, ln_g, ln_b, mod_rows, layer, w_in, b_in, sgu_g, sgu_b, w_s, b_s, seq):
    wts, x1, ys, pos2 = moe
    yg = _gather_rows_of_tokens(ys, pos2)
    t, d = x1.shape
    tm = 512
    steps_per_batch = seq // tm
    n_steps = t // tm
    gw = GMLP_WIDTH
    return pl.pallas_call(
        _gmlp_body,
        grid=(n_steps,),
        in_specs=_combine_specs(tm, n_steps, steps_per_batch, layer - 1) + [
            pl.BlockSpec((None, 1, N_MOD * d), lambda i: (layer * MOD_ROWS + i // steps_per_batch, 0, 0)),
            pl.BlockSpec((d, 2 * gw), lambda i: (0, 0)),
            pl.BlockSpec((1, 2 * gw), lambda i: (0, 0)),
            pl.BlockSpec((1, gw), lambda i: (0, 0)),
            pl.BlockSpec((1, gw), lambda i: (0, 0)),
            pl.BlockSpec((N_SGU_GROUPS, CHUNK, CHUNK), lambda i: (0, 0, 0)),
            pl.BlockSpec((CHUNK, N_SGU_GROUPS), lambda i: (0, 0)),
        ],
        out_specs=[pl.BlockSpec((tm, d), lambda i: (i, 0)), pl.BlockSpec((tm, gw), lambda i: (i, 0))],
        out_shape=[jax.ShapeDtypeStruct((t, d), F32), jax.ShapeDtypeStruct((t, gw), BF16)],
        scratch_shapes=[pltpu.VMEM((N_SGU_GROUPS, CHUNK, CHUNK), BF16), pltpu.VMEM((d, 2 * gw), BF16)],
        compiler_params=_cparams(1),
        name="combine_gmlp_gate",
    )(wts, x1, mod_rows, ln_g.reshape(1, d), ln_b.reshape(1, d), yg, yg,
      mod_rows, w_in, b_in.reshape(1, 2 * gw), sgu_g.reshape(1, gw), sgu_b.reshape(1, gw), w_s, b_s.T)


def _router_params(w_group, b_group, w_expert, b_expert):
    def lanes(group_part, expert_part):
        rows = group_part.shape[0]
        gap = jnp.zeros((rows, EXPERT_COL0 - N_EXPERT_GROUPS), F32)
        tail = jnp.zeros((rows, ROUTER_LANES - EXPERT_COL0 - N_EXPERTS), F32)
        return jnp.concatenate([group_part, gap, expert_part, tail], axis=1)

    return lanes(w_group, w_expert), lanes(b_group[None, :], b_expert[None, :])


def _moe_experts(x1, h2, ids, wts, layer, w_gate_up, w_down):
    t = x1.shape[0]
    n_rows = 2 * t + N_EXPERTS * EXPERT_TILE
    n_tiles = n_rows // EXPERT_TILE
    pos, tile_expert, n_used, tile_valid = _expert_sort(ids, n_tiles)
    pos2 = pos.reshape(2, t)
    xs = _sc_scatter_rows(h2, pos2, n_rows)
    ys = _expert_mlp(tile_expert, n_used, tile_valid, xs, w_gate_up, w_down, layer)
    return wts, x1, ys, pos2


def kernel(x, c, positions, ada_w, ada_b, post_ln_g, post_ln_b, attn_w_qkv, attn_b_qkv, attn_sinks, attn_w_o, attn_b_o, gmlp_w_in, gmlp_b_in, gmlp_sgu_ln_g, gmlp_sgu_ln_b, gmlp_w_s, gmlp_b_s, gmlp_w_out, gmlp_b_out, moe_w_group_router, moe_b_group_router, moe_w_expert_router, moe_b_expert_router, moe_w_gate_up, moe_w_down):
    batch, seq, d = x.shape
    t = batch * seq
    x2 = x.reshape(t, d)
    c_pad = jnp.pad(c, ((0, MOD_ROWS - batch), (0, 0)))
    mod_rows = _adaln_mod(c_pad, ada_w, ada_b).reshape(DEPTH * MOD_ROWS, 1, N_MOD * d)

    moe = None
    for layer in range(DEPTH):
        j = layer // 2
        if layer % 2 == 0:
            if moe is not None:
                x2 = _combine_ln(moe, mod_rows, layer - 1, post_ln_g[layer - 1, 1], post_ln_b[layer - 1, 1], seq)
            qkv = _qkv_rope(x2, mod_rows, layer, positions, attn_w_qkv[j], attn_b_qkv[j], seq)
            mix = _attention(qkv, attn_sinks[j], batch, seq)
            w_out, b_out = attn_w_o[j], attn_b_o[j]
        else:
            x2, mix = _gmlp_gate(moe, post_ln_g[layer - 1, 1], post_ln_b[layer - 1, 1], mod_rows, layer,
                                 gmlp_w_in[j], gmlp_b_in[j], gmlp_sgu_ln_g[j], gmlp_sgu_ln_b[j],
                                 gmlp_w_s[j], gmlp_b_s[j], seq)
            w_out, b_out = gmlp_w_out[j], gmlp_b_out[j]
        w_router, b_router = _router_params(moe_w_group_router[layer], moe_b_group_router[layer],
                                            moe_w_expert_router[layer], moe_b_expert_router[layer])
        x1, h2, ids, wts = _proj_ln_router(mix, x2, mod_rows, layer, w_out, b_out, post_ln_g[layer, 0],
                                           post_ln_b[layer, 0], w_router, b_router, seq)
        moe = _moe_experts(x1, h2, ids, wts, layer, moe_w_gate_up, moe_w_down)
    x2 = _combine_ln(moe, mod_rows, DEPTH - 1, post_ln_g[DEPTH - 1, 1], post_ln_b[DEPTH - 1, 1], seq)
    return x2.reshape(batch, seq, d)
```

```python
import functools

import jax
import jax.numpy as jnp
from jax import lax
from jax.experimental import pallas as pl
from jax.experimental.pallas import tpu as pltpu
from jax.experimental.pallas import tpu_sc as plsc

F32 = jnp.float32
BF16 = jnp.bfloat16
I32 = jnp.int32

D_MODEL = 1024
DEPTH = 2
HEAD_DIM = 64
N_Q_HEADS = 16
N_KV_HEADS = 4
GQA_GROUP = N_Q_HEADS // N_KV_HEADS
WINDOW = 128
ROPE_THETA = 10000.0
Q_WIDTH = N_Q_HEADS * HEAD_DIM
KV_WIDTH = N_KV_HEADS * HEAD_DIM
QKV_WIDTH = Q_WIDTH + 2 * KV_WIDTH
CHUNK = 128
GMLP_WIDTH = 2 * D_MODEL
N_SGU_GROUPS = 8
SGU_GROUP_DIM = GMLP_WIDTH // N_SGU_GROUPS
N_EXPERT_GROUPS = 4
EXPERTS_PER_GROUP = 8
N_EXPERTS = N_EXPERT_GROUPS * EXPERTS_PER_GROUP
EXPERT_FF = D_MODEL // 4
N_MOD = 6
DEEPNORM_ALPHA = (2.0 * DEPTH) ** 0.25
LN_EPS = 1e-5

LANES = 128
MOD_ROWS = 8
ROUTER_LANES = 128
EXPERT_COL0 = 8
SORT_CHUNK = 256
EXPERT_TILE = 512
SUB_ROWS = 256
VMEM_LIMIT = 56 * 1024 * 1024
SC_CORES = 2
SC_WORKERS = 32
SC_CHUNK = 64
SC_SLOTS = 3


def _cparams(n_axes, vmem=VMEM_LIMIT):
    return pltpu.CompilerParams(dimension_semantics=("arbitrary",) * n_axes, vmem_limit_bytes=vmem)


U32 = jnp.uint32
ROW_CHUNKS = D_MODEL // 2 // LANES


def _store_rows(ref, val, token0=0):
    n = val.shape[0]
    half = D_MODEL // 2
    words = pltpu.pack_elementwise([val[:, :half], val[:, half:]], packed_dtype=BF16)
    for c in range(ROW_CHUNKS):
        ref[pl.ds(token0 * ROW_CHUNKS + c, n, stride=ROW_CHUNKS), :] = words[:, c * LANES:(c + 1) * LANES]


def _load_rows(ref, n, token0=0):
    words = jnp.concatenate([ref[pl.ds(token0 * ROW_CHUNKS + c, n, stride=ROW_CHUNKS), :]
                             for c in range(ROW_CHUNKS)], axis=1)
    lo = lax.bitcast_convert_type(words << 16, F32)
    hi = lax.bitcast_convert_type(words & jnp.uint32(0xFFFF0000), F32)
    return jnp.concatenate([lo, hi], axis=1)


def _layer_norm(r, g, b):
    mu = jnp.mean(r, axis=-1, keepdims=True)
    d = r - mu
    var = jnp.mean(d * d, axis=-1, keepdims=True)
    return d * lax.rsqrt(var + LN_EPS) * g + b


def _mod_body(c_ref, w_ref, b_ref, o_ref):
    c = c_ref[...]
    ca = c * jax.nn.sigmoid(c)
    o_ref[...] = jnp.dot(ca.astype(BF16), w_ref[...].astype(BF16), preferred_element_type=F32) + b_ref[...]


def _adaln_mod(c_pad, ada_w, ada_b):
    tn = 1536
    n_out = N_MOD * D_MODEL
    return pl.pallas_call(
        _mod_body,
        grid=(DEPTH, n_out // tn),
        in_specs=[
            pl.BlockSpec((MOD_ROWS, D_MODEL), lambda l, j: (0, 0)),
            pl.BlockSpec((None, D_MODEL, tn), lambda l, j: (l, 0, j)),
            pl.BlockSpec((None, 1, tn), lambda l, j: (l, 0, j)),
        ],
        out_specs=pl.BlockSpec((None, MOD_ROWS, tn), lambda l, j: (l, 0, j)),
        out_shape=jax.ShapeDtypeStruct((DEPTH, MOD_ROWS, n_out), F32),
        compiler_params=_cparams(2),
        name="adaln_mod",
    )(c_pad, ada_w, ada_b.reshape(DEPTH, 1, n_out))


def _qkv_body(x_ref, mod_ref, pos_ref, w_ref, b_ref, invf_ref, o_ref, wbf_ref):
    @pl.when(pl.program_id(0) == 0)
    def _():
        wbf_ref[...] = w_ref[...].astype(BF16)

    tm = x_ref.shape[0]
    sh = mod_ref[:, 0:D_MODEL]
    sc = mod_ref[:, D_MODEL:2 * D_MODEL]
    h = x_ref[...] * (1.0 + sc) + sh
    qkv = jnp.dot(h.astype(BF16), wbf_ref[...], preferred_element_type=F32) + b_ref[...]

    ang = invf_ref[...] * pos_ref[...].astype(F32)
    c = jnp.cos(ang)
    s = jnp.sin(ang)
    ct = jnp.concatenate([c, c, c, c], axis=0).T
    st = jnp.concatenate([-s, s, -s, s], axis=0).T
    lane = lax.broadcasted_iota(I32, (tm, LANES), 1)
    first_half = (lane & (HEAD_DIM // 2)) == 0
    n_rope = (Q_WIDTH + KV_WIDTH) // LANES
    for j in range(n_rope):
        blk = qkv[:, j * LANES:(j + 1) * LANES]
        rot = jnp.where(first_half, pltpu.roll(blk, LANES - HEAD_DIM // 2, 1), pltpu.roll(blk, HEAD_DIM // 2, 1))
        r = blk * ct + rot * st
        if j < Q_WIDTH // LANES:
            r = r * (HEAD_DIM ** -0.5)
        o_ref[:, j * LANES:(j + 1) * LANES] = r.astype(BF16)
    o_ref[:, Q_WIDTH + KV_WIDTH:] = qkv[:, Q_WIDTH + KV_WIDTH:].astype(BF16)


def _qkv_rope(x2, mod_rows, layer, positions, w_qkv, b_qkv, seq):
    t = x2.shape[0]
    tm = 1024
    steps_per_batch = seq // tm
    inv_freq = ROPE_THETA ** (-jnp.arange(0, HEAD_DIM, 2, dtype=F32) / HEAD_DIM)
    return pl.pallas_call(
        _qkv_body,
        grid=(t // tm,),
        in_specs=[
            pl.BlockSpec((tm, D_MODEL), lambda i: (i, 0)),
            pl.BlockSpec((None, 1, N_MOD * D_MODEL), lambda i: (layer * MOD_ROWS + i // steps_per_batch, 0, 0)),
            pl.BlockSpec((None, 1, tm), lambda i: (i, 0, 0)),
            pl.BlockSpec((D_MODEL, QKV_WIDTH), lambda i: (0, 0)),
            pl.BlockSpec((1, QKV_WIDTH), lambda i: (0, 0)),
            pl.BlockSpec((HEAD_DIM // 2, 1), lambda i: (0, 0)),
        ],
        out_specs=pl.BlockSpec((tm, QKV_WIDTH), lambda i: (i, 0)),
        out_shape=jax.ShapeDtypeStruct((t, QKV_WIDTH), BF16),
        scratch_shapes=[pltpu.VMEM((D_MODEL, QKV_WIDTH), BF16)],
        compiler_params=_cparams(1),
        name="qkv_rope",
    )(x2, mod_rows, positions.reshape(t // tm, 1, tm), w_qkv, b_qkv.reshape(1, QKV_WIDTH),
      inv_freq.reshape(HEAD_DIM // 2, 1))


BF16_ROWS = 16


def _attn_prepare(kv, kab_ref, vx_ref, slot):
    kv = kv.astype(F32)
    low = lax.broadcasted_iota(I32, (WINDOW, LANES), 1) < HEAD_DIM
    ones = jnp.ones((WINDOW, LANES), F32)
    for g in range(N_KV_HEADS):
        for part, ref in ((0, None), (KV_WIDTH, vx_ref)):
            tile = kv[:, part + (g // 2) * LANES:part + (g // 2 + 1) * LANES]
            other = pltpu.roll(tile, HEAD_DIM, 1)
            in_low, in_high = (tile, other) if g % 2 == 0 else (other, tile)
            if ref is None:
                kab_ref[slot, 2 * g] = jnp.where(low, in_low, 0.0).astype(BF16)
                kab_ref[slot, 2 * g + 1] = jnp.where(low, 0.0, in_high).astype(BF16)
            else:
                both = jnp.where(low, in_low, in_high)
                vx_ref[slot, g] = jnp.concatenate([both, ones], axis=1).astype(BF16)


def _attn_block(sink_ref, q, kab_ref, vx_ref, s_ref, p_ref, prev, cur, first_block):
    for g in range(N_KV_HEADS):
        q_pair = jnp.concatenate([q[:, (2 * g) * LANES:(2 * g + 1) * LANES],
                                  q[:, (2 * g + 1) * LANES:(2 * g + 2) * LANES]], axis=0)
        for a in range(2):
            kband = jnp.concatenate([kab_ref[prev, 2 * g + a], kab_ref[cur, 2 * g + a]], axis=0)
            s = lax.dot_general(q_pair, kband, (((1,), (1,)), ((), ())), preferred_element_type=F32)
            s_ref[GQA_GROUP * g + a] = s[:WINDOW]
            s_ref[GQA_GROUP * g + 2 + a] = s[WINDOW:]
    qi = lax.broadcasted_iota(I32, (WINDOW, 2 * WINDOW), 0) + WINDOW
    kj = lax.broadcasted_iota(I32, (WINDOW, 2 * WINDOW), 1)
    mask = (kj <= qi) & (kj > qi - WINDOW) & ((kj >= WINDOW) | jnp.logical_not(first_block))
    key0 = lax.broadcasted_iota(I32, (1, 2 * WINDOW), 1) == 0
    for h in range(N_Q_HEADS):
        s = jnp.where(mask, s_ref[h], jnp.where(key0, sink_ref[h], -jnp.inf))
        m = jnp.max(s, axis=-1, keepdims=True)
        p_ref[h] = jnp.exp(s - m).astype(BF16)
    low = lax.broadcasted_iota(I32, (WINDOW, LANES), 1) < HEAD_DIM
    sink_row = ((lax.broadcasted_iota(I32, (BF16_ROWS, 2 * LANES), 0) == 0)
                & (lax.broadcasted_iota(I32, (BF16_ROWS, 2 * LANES), 1) < LANES))
    out_tiles = []
    for g in range(N_KV_HEADS):
        v_prev = vx_ref[prev, g]
        v_head = jnp.where(sink_row, 0.0, v_prev[:BF16_ROWS].astype(F32)).astype(BF16)
        vband = jnp.concatenate([v_head, v_prev[BF16_ROWS:], vx_ref[cur, g]], axis=0)
        p4 = p_ref[GQA_GROUP * g:GQA_GROUP * (g + 1)].reshape(GQA_GROUP * WINDOW, 2 * WINDOW)
        o4 = jnp.dot(p4, vband, preferred_element_type=F32)
        heads = []
        for j in range(GQA_GROUP):
            blk = o4[j * WINDOW:(j + 1) * WINDOW]
            heads.append(blk[:, :LANES] / blk[:, LANES:])
        out_tiles.append(jnp.where(low, heads[0], heads[1]))
        out_tiles.append(jnp.where(low, heads[2], heads[3]))
    return jnp.concatenate(out_tiles, axis=1).astype(BF16)


def _attn_body(sink_ref, q_ref, kv_ref, o_ref, kab_ref, vx_ref, s_ref, p_ref):
    n = pl.program_id(1)

    @pl.when(n == 0)
    def _():
        kab_ref[1] = jnp.zeros(kab_ref.shape[1:], kab_ref.dtype)
        half = (N_KV_HEADS, WINDOW, LANES)
        vx_ref[1] = jnp.concatenate([jnp.zeros(half, BF16), jnp.ones(half, BF16)], axis=-1)

    scratch = (kab_ref, vx_ref, s_ref, p_ref)
    for blk in range(q_ref.shape[0] // WINDOW):
        rows = slice(blk * WINDOW, (blk + 1) * WINDOW)
        cur = blk % 2
        _attn_prepare(kv_ref[rows, :], kab_ref, vx_ref, cur)
        o_ref[rows, :] = _attn_block(sink_ref, q_ref[rows, :], *scratch, 1 - cur, cur,
                                     (n == 0) if blk == 0 else False)


def _attention(qkv, sinks, batch, seq):
    t = qkv.shape[0]
    tq = 4 * WINDOW
    steps = seq // tq
    kv_col = Q_WIDTH // (2 * KV_WIDTH)
    return pl.pallas_call(
        _attn_body,
        grid=(batch, steps),
        in_specs=[
            pl.BlockSpec(memory_space=pltpu.SMEM),
            pl.BlockSpec((tq, Q_WIDTH), lambda b, n: (b * steps + n, 0)),
            pl.BlockSpec((tq, 2 * KV_WIDTH), lambda b, n: (b * steps + n, kv_col)),
        ],
        out_specs=pl.BlockSpec((tq, Q_WIDTH), lambda b, n: (b * steps + n, 0)),
        out_shape=jax.ShapeDtypeStruct((t, Q_WIDTH), BF16),
        scratch_shapes=[
            pltpu.VMEM((2, 2 * N_KV_HEADS, WINDOW, LANES), BF16),
            pltpu.VMEM((2, N_KV_HEADS, WINDOW, 2 * LANES), BF16),
            pltpu.VMEM((N_Q_HEADS, WINDOW, 2 * WINDOW), F32),
            pltpu.VMEM((N_Q_HEADS, WINDOW, 2 * WINDOW), BF16),
        ],
        compiler_params=_cparams(2),
        name="swa_attention",
    )(sinks, qkv, qkv)


def _route(lt):
    tm = lt.shape[1]
    row = lax.broadcasted_iota(I32, (EXPERTS_PER_GROUP, tm), 0)
    neg = -jnp.inf
    gl = jnp.where(row < N_EXPERT_GROUPS, lt[0:EXPERTS_PER_GROUP], neg)
    gm = jnp.max(gl, axis=0, keepdims=True)
    g_p = 1.0 / jnp.sum(jnp.exp(gl - gm), axis=0, keepdims=True)
    g_idx = jnp.min(jnp.where(gl == gm, row, EXPERTS_PER_GROUP), axis=0, keepdims=True)
    sel = lt[EXPERT_COL0 + (N_EXPERT_GROUPS - 1) * EXPERTS_PER_GROUP:EXPERT_COL0 + N_EXPERTS]
    for g in range(N_EXPERT_GROUPS - 2, -1, -1):
        lo = EXPERT_COL0 + g * EXPERTS_PER_GROUP
        sel = jnp.where(g_idx == g, lt[lo:lo + EXPERTS_PER_GROUP], sel)
    v1 = jnp.max(sel, axis=0, keepdims=True)
    i1 = jnp.min(jnp.where(sel == v1, row, EXPERTS_PER_GROUP), axis=0, keepdims=True)
    sel2 = jnp.where(row == i1, neg, sel)
    v2 = jnp.max(sel2, axis=0, keepdims=True)
    i2 = jnp.min(jnp.where(sel2 == v2, row, EXPERTS_PER_GROUP), axis=0, keepdims=True)
    e2 = jnp.exp(v2 - v1)
    w1 = g_p / (1.0 + e2)
    w2 = g_p * e2 / (1.0 + e2)
    base = g_idx * EXPERTS_PER_GROUP
    return base + i1, base + i2, w1, w2


def _proj_body(o_ref, x_ref, mod_ref, w_ref, b_ref, lng_ref, lnb_ref, wr_ref, br_ref,
               x1_ref, h2_ref, ids_ref, wts_ref, wbf_ref):
    @pl.when(pl.program_id(0) == 0)
    def _():
        wbf_ref[...] = w_ref[...].astype(BF16)

    d = D_MODEL
    gt_m = mod_ref[:, 2 * d:3 * d]
    sh_f = mod_ref[:, 3 * d:4 * d]
    sc_f = mod_ref[:, 4 * d:5 * d]
    wr = wr_ref[...].astype(BF16)
    for r in range(x_ref.shape[0] // SUB_ROWS):
        rows = slice(r * SUB_ROWS, (r + 1) * SUB_ROWS)
        y = jnp.dot(o_ref[rows, :], wbf_ref[...], preferred_element_type=F32) + b_ref[...]
        x1 = _layer_norm(DEEPNORM_ALPHA * x_ref[rows, :] + (1.0 + gt_m) * y, lng_ref[...], lnb_ref[...])
        x1_ref[rows, :] = x1
        h2 = x1 * (1.0 + sc_f) + sh_f
        _store_rows(h2_ref, h2, r * SUB_ROWS)
        logits = jnp.dot(h2.astype(BF16), wr, preferred_element_type=F32) + br_ref[...]
        ea, eb, wa, wb = _route(logits.T)
        ids_ref[0:1, rows] = ea
        ids_ref[1:2, rows] = eb
        wts_ref[0:1, rows] = wa
        wts_ref[1:2, rows] = wb


def _proj_ln_router(o, x2, mod_rows, layer, w, b, ln_g, ln_b, w_router, b_router, seq):
    t, k = o.shape
    tm = 1024
    steps_per_batch = seq // tm
    d = D_MODEL
    return pl.pallas_call(
        _proj_body,
        grid=(t // tm,),
        in_specs=[
            pl.BlockSpec((tm, k), lambda i: (i, 0)),
            pl.BlockSpec((tm, d), lambda i: (i, 0)),
            pl.BlockSpec((None, 1, N_MOD * d), lambda i: (layer * MOD_ROWS + i // steps_per_batch, 0, 0)),
            pl.BlockSpec((k, d), lambda i: (0, 0), pipeline_mode=pl.Buffered(1)),
            pl.BlockSpec((1, d), lambda i: (0, 0)),
            pl.BlockSpec((1, d), lambda i: (0, 0)),
            pl.BlockSpec((1, d), lambda i: (0, 0)),
            pl.BlockSpec((d, ROUTER_LANES), lambda i: (0, 0)),
            pl.BlockSpec((1, ROUTER_LANES), lambda i: (0, 0)),
        ],
        out_specs=[
            pl.BlockSpec((tm, d), lambda i: (i, 0)),
            pl.BlockSpec((tm * ROW_CHUNKS, LANES), lambda i: (i, 0)),
            pl.BlockSpec((2, tm), lambda i: (0, i)),
            pl.BlockSpec((2, tm), lambda i: (0, i)),
        ],
        out_shape=[
            jax.ShapeDtypeStruct((t, d), F32),
            jax.ShapeDtypeStruct((t * ROW_CHUNKS, LANES), U32),
            jax.ShapeDtypeStruct((2, t), I32),
            jax.ShapeDtypeStruct((2, t), F32),
        ],
        scratch_shapes=[pltpu.VMEM((k, d), BF16)],
        compiler_params=_cparams(1),
        name="proj_ln_router",
    )(o, x2, mod_rows, w, b.reshape(1, d), ln_g.reshape(1, d), ln_b.reshape(1, d), w_router, b_router)


def _sort_body(ids_ref, pos_ref, te_ref, nused_ref, tv_ref, rank_ref):
    n_rows = ids_ref.shape[0]
    c = SORT_CHUNK
    erow = lax.broadcasted_iota(I32, (N_EXPERTS, c), 0)
    tri = (lax.broadcasted_iota(I32, (c, c), 0) <= lax.broadcasted_iota(I32, (c, c), 1)).astype(BF16)

    def rank_step(r, carry):
        onehot = erow == ids_ref[pl.ds(r, 1), :]
        pref = jnp.dot(onehot.astype(BF16), tri, preferred_element_type=F32)
        rank = jnp.sum(jnp.where(onehot, pref + carry, 0.0), axis=0, keepdims=True) - 1.0
        rank_ref[pl.ds(r, 1), :] = rank
        return carry + pref[:, c - 1:c]

    counts = lax.fori_loop(0, n_rows, rank_step, jnp.zeros((N_EXPERTS, 1), F32), unroll=8)
    n_tile = jnp.floor((counts + (EXPERT_TILE - 1)) * (1.0 / EXPERT_TILE))
    low = (lax.broadcasted_iota(I32, (N_EXPERTS, N_EXPERTS), 1)
           <= lax.broadcasted_iota(I32, (N_EXPERTS, N_EXPERTS), 0)).astype(BF16)
    cum = jnp.dot(low, jnp.broadcast_to(n_tile, (N_EXPERTS, LANES)).astype(BF16),
                  preferred_element_type=F32)[:, 0:1]
    row_off = (cum - n_tile) * EXPERT_TILE

    def pos_step(r, _):
        onehot = erow == ids_ref[pl.ds(r, 1), :]
        off = jnp.sum(jnp.where(onehot, row_off, 0.0), axis=0, keepdims=True)
        pos_ref[pl.ds(r, 1), :] = (off + rank_ref[pl.ds(r, 1), :]).astype(I32)
        return 0

    lax.fori_loop(0, n_rows, pos_step, 0, unroll=8)
    total = jnp.max(cum, axis=0, keepdims=True)
    n_lanes = te_ref.shape[1]
    tile = jnp.minimum(lax.broadcasted_iota(I32, (N_EXPERTS, n_lanes), 1).astype(F32), total - 1.0)
    te_ref[...] = jnp.sum(jnp.where(cum <= tile, 1.0, 0.0), axis=0, keepdims=True).astype(I32)
    nused_ref[...] = jnp.broadcast_to(total, nused_ref.shape).astype(I32)
    tile_f = lax.broadcasted_iota(I32, (N_EXPERTS, n_lanes), 1).astype(F32)
    first = cum - n_tile
    rows_left = jnp.clip(counts - (tile_f - first) * EXPERT_TILE, 0.0, float(EXPERT_TILE))
    owns = (first <= tile_f) & (tile_f < cum)
    tv_ref[...] = jnp.sum(jnp.where(owns, rows_left, 0.0), axis=0, keepdims=True).astype(I32)


def _expert_sort(ids, n_tiles):
    n_assign = ids.shape[0] * ids.shape[1]
    n_rows = n_assign // SORT_CHUNK
    te_lanes = -(-n_tiles // LANES) * LANES
    pos, te, nused, tv = pl.pallas_call(
        _sort_body,
        grid=(1,),
        in_specs=[pl.BlockSpec((n_rows, SORT_CHUNK), lambda i: (0, 0))],
        out_specs=[
            pl.BlockSpec((n_rows, SORT_CHUNK), lambda i: (0, 0)),
            pl.BlockSpec((1, te_lanes), lambda i: (0, 0)),
            pl.BlockSpec((1, LANES), lambda i: (0, 0)),
            pl.BlockSpec((1, te_lanes), lambda i: (0, 0)),
        ],
        out_shape=[
            jax.ShapeDtypeStruct((n_rows, SORT_CHUNK), I32),
            jax.ShapeDtypeStruct((1, te_lanes), I32),
            jax.ShapeDtypeStruct((1, LANES), I32),
            jax.ShapeDtypeStruct((1, te_lanes), I32),
        ],
        scratch_shapes=[pltpu.VMEM((n_rows, SORT_CHUNK), F32)],
        compiler_params=_cparams(1),
        name="expert_sort",
    )(ids.reshape(n_rows, SORT_CHUNK))
    return pos, te[0, :n_tiles], nused[0, :1], tv[0, :n_tiles]


def _sc_mesh():
    return plsc.VectorSubcoreMesh(core_axis_name="c", subcore_axis_name="s", num_cores=SC_CORES,
                                  num_subcores=SC_WORKERS // SC_CORES)


def _sc_ring(n_chunks, read, write):
    reads, writes = {}, {}
    for j in range(min(SC_SLOTS - 1, n_chunks)):
        reads[j] = read(j)
    for j in range(n_chunks):
        for cp in reads.pop(j):
            cp.wait()
        nxt = j + SC_SLOTS - 1
        if nxt < n_chunks:
            for cp in writes.pop(nxt - SC_SLOTS, []):
                cp.wait()
            reads[nxt] = read(nxt)
        writes[j] = write(j)
    for cps in writes.values():
        for cp in cps:
            cp.wait()


def _sc_scatter_rows(src, pos2, n_rows):
    t = pos2.shape[1]
    src3 = src.reshape(t, ROW_CHUNKS, LANES)
    per_worker = t // SC_WORKERS
    n_chunks = per_worker // SC_CHUNK
    idx = pos2.reshape(2, SC_WORKERS, n_chunks, SC_CHUNK).transpose(1, 0, 2, 3)
    idx = idx.reshape(SC_WORKERS, 2 * n_chunks, SC_CHUNK)

    @functools.partial(
        pl.kernel, mesh=_sc_mesh(),
        out_type=jax.ShapeDtypeStruct((n_rows, ROW_CHUNKS, LANES), src.dtype),
        scratch_types=[
            pltpu.VMEM((2 * n_chunks, SC_CHUNK), I32),
            pltpu.VMEM((SC_SLOTS, SC_CHUNK, ROW_CHUNKS, LANES), src.dtype),
            pltpu.SemaphoreType.DMA((SC_SLOTS,)),
            pltpu.SemaphoreType.DMA((SC_SLOTS,)),
        ],
        name="sc_scatter_rows",
    )
    def scatter(src_hbm, idx_hbm, out_hbm, idx_v, rows_v, rsem, wsem):
        wid = lax.axis_index("s") * SC_CORES + lax.axis_index("c")
        pltpu.sync_copy(idx_hbm.at[wid], idx_v)

        def read(j):
            b = j % SC_SLOTS
            return [pltpu.async_copy(src_hbm.at[pl.ds(wid * per_worker + j * SC_CHUNK, SC_CHUNK)], rows_v.at[b],
                                     rsem.at[b])]

        def write(j):
            b = j % SC_SLOTS
            return [pltpu.async_copy(rows_v.at[b], out_hbm.at[idx_v.at[k * n_chunks + j]], wsem.at[b])
                    for k in range(2)]

        _sc_ring(n_chunks, read, write)

    return scatter(src3, idx).reshape(n_rows * ROW_CHUNKS, LANES)


def _sc_gather_rows(table, idx):
    n = idx.shape[0]
    table3 = table.reshape(-1, ROW_CHUNKS, LANES)
    per_worker = n // SC_WORKERS
    n_chunks = per_worker // SC_CHUNK

    @functools.partial(
        pl.kernel, mesh=_sc_mesh(),
        out_type=jax.ShapeDtypeStruct((n, ROW_CHUNKS, LANES), table.dtype),
        scratch_types=[
            pltpu.VMEM((n_chunks, SC_CHUNK), I32),
            pltpu.VMEM((SC_SLOTS, SC_CHUNK, ROW_CHUNKS, LANES), table.dtype),
            pltpu.SemaphoreType.DMA((SC_SLOTS,)),
            pltpu.SemaphoreType.DMA((SC_SLOTS,)),
        ],
        name="sc_gather_rows",
    )
    def gather(table_hbm, idx_hbm, out_hbm, idx_v, rows_v, rsem, wsem):
        wid = lax.axis_index("s") * SC_CORES + lax.axis_index("c")
        pltpu.sync_copy(idx_hbm.at[wid], idx_v)

        def read(j):
            b = j % SC_SLOTS
            return [pltpu.async_copy(table_hbm.at[idx_v.at[j]], rows_v.at[b], rsem.at[b])]

        def write(j):
            b = j % SC_SLOTS
            return [pltpu.async_copy(rows_v.at[b], out_hbm.at[pl.ds(wid * per_worker + j * SC_CHUNK, SC_CHUNK)],
                                     wsem.at[b])]

        _sc_ring(n_chunks, read, write)

    out = gather(table3, idx.reshape(SC_WORKERS, n_chunks, SC_CHUNK))
    return out.reshape(n * ROW_CHUNKS, LANES)


XS_SLOTS = 3


def _expert_body(te_ref, nused_ref, tv_ref, xs_hbm, wgu_ref, wd_ref, ys_ref, wgu_bf, wd_bf, xbuf, xsem):
    i = pl.program_id(0)
    n_used = nused_ref[0]
    used = i < n_used
    prev = te_ref[jnp.maximum(i - 1, 0)]
    tile_rows = EXPERT_TILE * ROW_CHUNKS

    def fetch(tile):
        slot = lax.rem(tile, XS_SLOTS)
        r0 = pl.multiple_of(tile * tile_rows, tile_rows)
        return pltpu.make_async_copy(xs_hbm.at[pl.ds(r0, tile_rows)], xbuf.at[slot], xsem.at[slot])

    @pl.when((i == 0) & (n_used > 0))
    def _():
        fetch(0).start()

    @pl.when((i == 0) & (n_used > 1))
    def _():
        fetch(1).start()

    @pl.when(i + 2 < n_used)
    def _():
        fetch(i + 2).start()

    @pl.when(used & ((i == 0) | (te_ref[i] != prev)))
    def _():
        wgu_bf[...] = wgu_ref[...].astype(BF16)
        wd_bf[...] = wd_ref[...].astype(BF16)

    @pl.when(used)
    def _():
        fetch(i).wait()
        live = lax.broadcasted_iota(I32, (EXPERT_TILE, 1), 0) < tv_ref[i]
        xs = jnp.where(live, _load_rows(xbuf.at[lax.rem(i, XS_SLOTS)], EXPERT_TILE), 0.0).astype(BF16)
        gu = jnp.dot(xs, wgu_bf[...], preferred_element_type=F32)
        gate = gu[:, :EXPERT_FF]
        up = gu[:, EXPERT_FF:]
        act = gate * jax.nn.sigmoid(gate) * up
        _store_rows(ys_ref, jnp.dot(act.astype(BF16), wd_bf[...], preferred_element_type=F32))

    @pl.when(jnp.logical_not(used))
    def _():
        ys_ref[...] = jnp.zeros(ys_ref.shape, ys_ref.dtype)


def _expert_mlp(tile_expert, n_used, tile_valid, xs, w_gate_up, w_down, layer):
    d = D_MODEL
    n_tiles = xs.shape[0] // (EXPERT_TILE * ROW_CHUNKS)
    f2 = 2 * EXPERT_FF
    tile_rows = EXPERT_TILE * ROW_CHUNKS
    e0 = layer * N_EXPERTS
    grid_spec = pltpu.PrefetchScalarGridSpec(
        num_scalar_prefetch=3,
        grid=(n_tiles,),
        in_specs=[
            pl.BlockSpec(memory_space=pl.ANY),
            pl.BlockSpec((None, d, f2), lambda i, te, nu, tv: (e0 + te[i], 0, 0)),
            pl.BlockSpec((None, EXPERT_FF, d), lambda i, te, nu, tv: (e0 + te[i], 0, 0)),
        ],
        out_specs=pl.BlockSpec((tile_rows, LANES), lambda i, te, nu, tv: (i, 0)),
        scratch_shapes=[pltpu.VMEM((d, f2), BF16), pltpu.VMEM((EXPERT_FF, d), BF16),
                        pltpu.VMEM((XS_SLOTS, tile_rows, LANES), U32), pltpu.SemaphoreType.DMA((XS_SLOTS,))],
    )
    return pl.pallas_call(
        _expert_body,
        grid_spec=grid_spec,
        out_shape=jax.ShapeDtypeStruct(xs.shape, U32),
        compiler_params=_cparams(1),
        name="expert_mlp",
    )(tile_expert, n_used, tile_valid, xs, w_gate_up.reshape(DEPTH * N_EXPERTS, d, f2),
      w_down.reshape(DEPTH * N_EXPERTS, EXPERT_FF, d))


def _moe_combine(wts_ref, x_ref, mod_ref, lng_ref, lnb_ref, ya_ref, yb_ref, token0, n):
    d = D_MODEL
    rows = slice(token0, token0 + n)
    cols = []
    for k in range(2):
        wt = jnp.broadcast_to(wts_ref[k:k + 1, rows], (LANES, n)).T
        cols.append(jnp.concatenate([wt] * (d // LANES), axis=1))
    y = cols[0] * _load_rows(ya_ref, n, token0) + cols[1] * _load_rows(yb_ref, n, token0)
    gt_f = mod_ref[:, 5 * d:6 * d]
    return _layer_norm(DEEPNORM_ALPHA * x_ref[rows, :] + (1.0 + gt_f) * y, lng_ref[...], lnb_ref[...])


def _combine_specs(tm, n_steps, steps_per_batch, layer):
    d = D_MODEL
    return [
        pl.BlockSpec((2, tm), lambda i: (0, i)),
        pl.BlockSpec((tm, d), lambda i: (i, 0)),
        pl.BlockSpec((None, 1, N_MOD * d), lambda i: (layer * MOD_ROWS + i // steps_per_batch, 0, 0)),
        pl.BlockSpec((1, d), lambda i: (0, 0)),
        pl.BlockSpec((1, d), lambda i: (0, 0)),
        pl.BlockSpec((tm * ROW_CHUNKS, LANES), lambda i: (i, 0)),
        pl.BlockSpec((tm * ROW_CHUNKS, LANES), lambda i: (n_steps + i, 0)),
    ]


def _gather_rows_of_tokens(ys, pos2):
    return _sc_gather_rows(ys, pos2.reshape(-1))


def _combine_body(wts_ref, x_ref, mod_ref, lng_ref, lnb_ref, ya_ref, yb_ref, o_ref):
    for r in range(x_ref.shape[0] // SUB_ROWS):
        o_ref[r * SUB_ROWS:(r + 1) * SUB_ROWS, :] = _moe_combine(wts_ref, x_ref, mod_ref, lng_ref, lnb_ref, ya_ref,
                                                                 yb_ref, r * SUB_ROWS, SUB_ROWS)


def _combine_ln(moe, mod_rows, layer, ln_g, ln_b, seq):
    wts, x1, ys, pos2 = moe
    yg = _gather_rows_of_tokens(ys, pos2)
    t, d = x1.shape
    tm = 1024
    n_steps = t // tm
    return pl.pallas_call(
        _combine_body,
        grid=(n_steps,),
        in_specs=_combine_specs(tm, n_steps, seq // tm, layer),
        out_specs=pl.BlockSpec((tm, d), lambda i: (i, 0)),
        out_shape=jax.ShapeDtypeStruct((t, d), F32),
        compiler_params=_cparams(1),
        name="moe_combine_ln",
    )(wts, x1, mod_rows, ln_g.reshape(1, d), ln_b.reshape(1, d), yg, yg)


def _gmlp_body(wts_ref, x1_ref, modp_ref, lng_ref, lnb_ref, ya_ref, yb_ref,
               mod_ref, w_ref, b_ref, g_ref, beta_ref, ws_ref, bs_ref, x2_ref, o_ref, ws_bf, w_bf):
    @pl.when(pl.program_id(0) == 0)
    def _():
        tri = lax.broadcasted_iota(I32, (CHUNK, CHUNK), 0) >= lax.broadcasted_iota(I32, (CHUNK, CHUNK), 1)
        for g in range(N_SGU_GROUPS):
            ws_bf[g] = jnp.where(tri, ws_ref[g], 0.0).astype(BF16)
        w_bf[...] = w_ref[...].astype(BF16)

    d = D_MODEL
    sh = mod_ref[:, 0:d]
    sc = mod_ref[:, d:2 * d]
    sub_rows = x1_ref.shape[0]
    for r in range(x1_ref.shape[0] // sub_rows):
        r0 = r * sub_rows
        x2 = _moe_combine(wts_ref, x1_ref, modp_ref, lng_ref, lnb_ref, ya_ref, yb_ref, r0, sub_rows)
        x2_ref[r0:r0 + sub_rows, :] = x2
        h = x2 * (1.0 + sc) + sh
        z = jnp.dot(h.astype(BF16), w_bf[...], preferred_element_type=F32) + b_ref[...]
        z = 0.5 * z * (1.0 + lax.erf(z * (2.0 ** -0.5)))
        u = z[:, :GMLP_WIDTH]
        v = _layer_norm(z[:, GMLP_WIDTH:], g_ref[...], beta_ref[...]).astype(BF16)
        for ci in range(sub_rows // CHUNK):
            rows = slice(ci * CHUNK, (ci + 1) * CHUNK)
            out_rows = slice(r0 + ci * CHUNK, r0 + (ci + 1) * CHUNK)
            for g in range(N_SGU_GROUPS):
                lanes = slice(g * SGU_GROUP_DIM, (g + 1) * SGU_GROUP_DIM)
                mixed = jnp.dot(ws_bf[g], v[rows, lanes], preferred_element_type=F32) + bs_ref[:, g:g + 1]
                o_ref[out_rows, lanes] = (u[rows, lanes] * mixed).astype(BF16)


def _gmlp_gate(moe, ln_g, ln_b, mod_rows, layer, w_in, b_in, sgu_g, sgu_b, w_s, b_s, seq):
    wts, x1, ys, pos2 = moe
    yg = _gather_rows_of_tokens(ys, pos2)
    t, d = x1.shape
    tm = 512
    steps_per_batch = seq // tm
    n_steps = t // tm
    gw = GMLP_WIDTH
    return pl.pallas_call(
        _gmlp_body,
        grid=(n_steps,),
        in_specs=_combine_specs(tm, n_steps, steps_per_batch, layer - 1) + [
            pl.BlockSpec((None, 1, N_MOD * d), lambda i: (layer * MOD_ROWS + i // steps_per_batch, 0, 0)),
            pl.BlockSpec((d, 2 * gw), lambda i: (0, 0)),
            pl.BlockSpec((1, 2 * gw), lambda i: (0, 0)),
            pl.BlockSpec((1, gw), lambda i: (0, 0)),
            pl.BlockSpec((1, gw), lambda i: (0, 0)),
            pl.BlockSpec((N_SGU_GROUPS, CHUNK, CHUNK), lambda i: (0, 0, 0)),
            pl.BlockSpec((CHUNK, N_SGU_GROUPS), lambda i: (0, 0)),
        ],
        out_specs=[pl.BlockSpec((tm, d), lambda i: (i, 0)), pl.BlockSpec((tm, gw), lambda i: (i, 0))],
        out_shape=[jax.ShapeDtypeStruct((t, d), F32), jax.ShapeDtypeStruct((t, gw), BF16)],
        scratch_shapes=[pltpu.VMEM((N_SGU_GROUPS, CHUNK, CHUNK), BF16), pltpu.VMEM((d, 2 * gw), BF16)],
        compiler_params=_cparams(1),
        name="combine_gmlp_gate",
    )(wts, x1, mod_rows, ln_g.reshape(1, d), ln_b.reshape(1, d), yg, yg,
      mod_rows, w_in, b_in.reshape(1, 2 * gw), sgu_g.reshape(1, gw), sgu_b.reshape(1, gw), w_s, b_s.T)


def _router_params(w_group, b_group, w_expert, b_expert):
    def lanes(group_part, expert_part):
        rows = group_part.shape[0]
        gap = jnp.zeros((rows, EXPERT_COL0 - N_EXPERT_GROUPS), F32)
        tail = jnp.zeros((rows, ROUTER_LANES - EXPERT_COL0 - N_EXPERTS), F32)
        return jnp.concatenate([group_part, gap, expert_part, tail], axis=1)

    return lanes(w_group, w_expert), lanes(b_group[None, :], b_expert[None, :])


def _moe_experts(x1, h2, ids, wts, layer, w_gate_up, w_down):
    t = x1.shape[0]
    n_rows = 2 * t + N_EXPERTS * EXPERT_TILE
    n_tiles = n_rows // EXPERT_TILE
    pos, tile_expert, n_used, tile_valid = _expert_sort(ids, n_tiles)
    pos2 = pos.reshape(2, t)
    xs = _sc_scatter_rows(h2, pos2, n_rows)
    ys = _expert_mlp(tile_expert, n_used, tile_valid, xs, w_gate_up, w_down, layer)
    return wts, x1, ys, pos2


def kernel(x, c, positions, ada_w, ada_b, post_ln_g, post_ln_b, attn_w_qkv, attn_b_qkv, attn_sinks, attn_w_o, attn_b_o, gmlp_w_in, gmlp_b_in, gmlp_sgu_ln_g, gmlp_sgu_ln_b, gmlp_w_s, gmlp_b_s, gmlp_w_out, gmlp_b_out, moe_w_group_router, moe_b_group_router, moe_w_expert_router, moe_b_expert_router, moe_w_gate_up, moe_w_down):
    batch, seq, d = x.shape
    t = batch * seq
    assert d == D_MODEL and batch <= MOD_ROWS and ada_w.shape[0] == DEPTH == 2
    assert seq % 1024 == 0, "token tiles of 1024 rows must not straddle sequences"
    assert t % (SC_WORKERS * SC_CHUNK) == 0, "every SparseCore subcore moves whole chunks"
    x2 = x.reshape(t, d)
    c_pad = jnp.pad(c, ((0, MOD_ROWS - batch), (0, 0)))
    mod_rows = _adaln_mod(c_pad, ada_w, ada_b).reshape(DEPTH * MOD_ROWS, 1, N_MOD * d)

    moe = None
    for layer in range(DEPTH):
        j = layer // 2
        if layer % 2 == 0:
            if moe is not None:
                x2 = _combine_ln(moe, mod_rows, layer - 1, post_ln_g[layer - 1, 1], post_ln_b[layer - 1, 1], seq)
            qkv = _qkv_rope(x2, mod_rows, layer, positions, attn_w_qkv[j], attn_b_qkv[j], seq)
            mix = _attention(qkv, attn_sinks[j], batch, seq)
            w_out, b_out = attn_w_o[j], attn_b_o[j]
        else:
            x2, mix = _gmlp_gate(moe, post_ln_g[layer - 1, 1], post_ln_b[layer - 1, 1], mod_rows, layer,
                                 gmlp_w_in[j], gmlp_b_in[j], gmlp_sgu_ln_g[j], gmlp_sgu_ln_b[j],
                                 gmlp_w_s[j], gmlp_b_s[j], seq)
            w_out, b_out = gmlp_w_out[j], gmlp_b_out[j]
        w_router, b_router = _router_params(moe_w_group_router[layer], moe_b_group_router[layer],
                                            moe_w_expert_router[layer], moe_b_expert_router[layer])
        x1, h2, ids, wts = _proj_ln_router(mix, x2, mod_rows, layer, w_out, b_out, post_ln_g[layer, 0],
                                           post_ln_b[layer, 0], w_router, b_router, seq)
        moe = _moe_experts(x1, h2, ids, wts, layer, moe_w_gate_up, moe_w_down)
    x2 = _combine_ln(moe, mod_rows, DEPTH - 1, post_ln_g[DEPTH - 1, 1], post_ln_b[DEPTH - 1, 1], seq)
    return x2.reshape(batch, seq, d)
```

```python
import functools

import jax
import jax.numpy as jnp
from jax import lax
from jax.experimental import pallas as pl
from jax.experimental.pallas import tpu as pltpu
from jax.experimental.pallas import tpu_sc as plsc

F32 = jnp.float32
BF16 = jnp.bfloat16
I32 = jnp.int32

D_MODEL = 1024
DEPTH = 2
HEAD_DIM = 64
N_Q_HEADS = 16
N_KV_HEADS = 4
GQA_GROUP = N_Q_HEADS // N_KV_HEADS
WINDOW = 128
ROPE_THETA = 10000.0
Q_WIDTH = N_Q_HEADS * HEAD_DIM
KV_WIDTH = N_KV_HEADS * HEAD_DIM
QKV_WIDTH = Q_WIDTH + 2 * KV_WIDTH
CHUNK = 128
GMLP_WIDTH = 2 * D_MODEL
N_SGU_GROUPS = 8
SGU_GROUP_DIM = GMLP_WIDTH // N_SGU_GROUPS
N_EXPERT_GROUPS = 4
EXPERTS_PER_GROUP = 8
N_EXPERTS = N_EXPERT_GROUPS * EXPERTS_PER_GROUP
EXPERT_FF = D_MODEL // 4
N_MOD = 6
DEEPNORM_ALPHA = (2.0 * DEPTH) ** 0.25
LN_EPS = 1e-5

LANES = 128
MOD_ROWS = 8
ROUTER_LANES = 128
EXPERT_COL0 = 8
SORT_CHUNK = 256
EXPERT_TILE = 512
SUB_ROWS = 256
VMEM_LIMIT = 56 * 1024 * 1024
SC_CORES = 2
SC_WORKERS = 32
SC_CHUNK = 64
SC_SLOTS = 3


def _cparams(n_axes, vmem=VMEM_LIMIT):
    return pltpu.CompilerParams(dimension_semantics=("arbitrary",) * n_axes, vmem_limit_bytes=vmem)


U32 = jnp.uint32
ROW_CHUNKS = D_MODEL // 2 // LANES


def _store_rows(ref, val, token0=0):
    n = val.shape[0]
    half = D_MODEL // 2
    words = pltpu.pack_elementwise([val[:, :half], val[:, half:]], packed_dtype=BF16)
    for c in range(ROW_CHUNKS):
        ref[pl.ds(token0 * ROW_CHUNKS + c, n, stride=ROW_CHUNKS), :] = words[:, c * LANES:(c + 1) * LANES]


def _load_rows(ref, n, token0=0):
    words = jnp.concatenate([ref[pl.ds(token0 * ROW_CHUNKS + c, n, stride=ROW_CHUNKS), :]
                             for c in range(ROW_CHUNKS)], axis=1)
    lo = lax.bitcast_convert_type(words << 16, F32)
    hi = lax.bitcast_convert_type(words & jnp.uint32(0xFFFF0000), F32)
    return jnp.concatenate([lo, hi], axis=1)


def _layer_norm(r, g, b):
    mu = jnp.mean(r, axis=-1, keepdims=True)
    d = r - mu
    var = jnp.mean(d * d, axis=-1, keepdims=True)
    return d * lax.rsqrt(var + LN_EPS) * g + b


def _mod_body(c_ref, w_ref, b_ref, o_ref):
    c = c_ref[...]
    ca = c * jax.nn.sigmoid(c)
    o_ref[...] = jnp.dot(ca.astype(BF16), w_ref[...].astype(BF16), preferred_element_type=F32) + b_ref[...]


def _adaln_mod(c_pad, ada_w, ada_b, layer):
    tn = 1536
    n_out = N_MOD * D_MODEL
    mod = pl.pallas_call(
        _mod_body,
        grid=(n_out // tn,),
        in_specs=[
            pl.BlockSpec((MOD_ROWS, D_MODEL), lambda j: (0, 0)),
            pl.BlockSpec((None, D_MODEL, tn), lambda j: (layer, 0, j)),
            pl.BlockSpec((None, 1, tn), lambda j: (layer, 0, j)),
        ],
        out_specs=pl.BlockSpec((MOD_ROWS, tn), lambda j: (0, j)),
        out_shape=jax.ShapeDtypeStruct((MOD_ROWS, n_out), F32),
        compiler_params=_cparams(1),
        name="adaln_mod",
    )(c_pad, ada_w, ada_b.reshape(DEPTH, 1, n_out))
    return mod.reshape(MOD_ROWS, 1, n_out)


def _qkv_body(x_ref, mod_ref, pos_ref, w_ref, b_ref, invf_ref, o_ref, wbf_ref):
    @pl.when(pl.program_id(0) == 0)
    def _():
        wbf_ref[...] = w_ref[...].astype(BF16)

    tm = x_ref.shape[0]
    sh = mod_ref[:, 0:D_MODEL]
    sc = mod_ref[:, D_MODEL:2 * D_MODEL]
    h = x_ref[...] * (1.0 + sc) + sh
    qkv = jnp.dot(h.astype(BF16), wbf_ref[...], preferred_element_type=F32) + b_ref[...]

    ang = invf_ref[...] * pos_ref[...].astype(F32)
    c = jnp.cos(ang)
    s = jnp.sin(ang)
    ct = jnp.concatenate([c, c, c, c], axis=0).T
    st = jnp.concatenate([-s, s, -s, s], axis=0).T
    lane = lax.broadcasted_iota(I32, (tm, LANES), 1)
    first_half = (lane & (HEAD_DIM // 2)) == 0
    n_rope = (Q_WIDTH + KV_WIDTH) // LANES
    for j in range(n_rope):
        blk = qkv[:, j * LANES:(j + 1) * LANES]
        rot = jnp.where(first_half, pltpu.roll(blk, LANES - HEAD_DIM // 2, 1), pltpu.roll(blk, HEAD_DIM // 2, 1))
        r = blk * ct + rot * st
        if j < Q_WIDTH // LANES:
            r = r * (HEAD_DIM ** -0.5)
        o_ref[:, j * LANES:(j + 1) * LANES] = r.astype(BF16)
    o_ref[:, Q_WIDTH + KV_WIDTH:] = qkv[:, Q_WIDTH + KV_WIDTH:].astype(BF16)


def _qkv_rope(x2, mod_rows, positions, w_qkv, b_qkv, seq):
    t = x2.shape[0]
    tm = 1024
    steps_per_batch = seq // tm
    inv_freq = ROPE_THETA ** (-jnp.arange(0, HEAD_DIM, 2, dtype=F32) / HEAD_DIM)
    return pl.pallas_call(
        _qkv_body,
        grid=(t // tm,),
        in_specs=[
            pl.BlockSpec((tm, D_MODEL), lambda i: (i, 0)),
            pl.BlockSpec((None, 1, N_MOD * D_MODEL), lambda i: (i // steps_per_batch, 0, 0)),
            pl.BlockSpec((None, 1, tm), lambda i: (i, 0, 0)),
            pl.BlockSpec((D_MODEL, QKV_WIDTH), lambda i: (0, 0)),
            pl.BlockSpec((1, QKV_WIDTH), lambda i: (0, 0)),
            pl.BlockSpec((HEAD_DIM // 2, 1), lambda i: (0, 0)),
        ],
        out_specs=pl.BlockSpec((tm, QKV_WIDTH), lambda i: (i, 0)),
        out_shape=jax.ShapeDtypeStruct((t, QKV_WIDTH), BF16),
        scratch_shapes=[pltpu.VMEM((D_MODEL, QKV_WIDTH), BF16)],
        compiler_params=_cparams(1),
        name="qkv_rope",
    )(x2, mod_rows, positions.reshape(t // tm, 1, tm), w_qkv, b_qkv.reshape(1, QKV_WIDTH),
      inv_freq.reshape(HEAD_DIM // 2, 1))


BF16_ROWS = 16


def _attn_prepare(kv, kab_ref, vx_ref, slot):
    kv = kv.astype(F32)
    low = lax.broadcasted_iota(I32, (WINDOW, LANES), 1) < HEAD_DIM
    ones = jnp.ones((WINDOW, LANES), F32)
    for g in range(N_KV_HEADS):
        for part, ref in ((0, None), (KV_WIDTH, vx_ref)):
            tile = kv[:, part + (g // 2) * LANES:part + (g // 2 + 1) * LANES]
            other = pltpu.roll(tile, HEAD_DIM, 1)
            in_low, in_high = (tile, other) if g % 2 == 0 else (other, tile)
            if ref is None:
                kab_ref[slot, 2 * g] = jnp.where(low, in_low, 0.0).astype(BF16)
                kab_ref[slot, 2 * g + 1] = jnp.where(low, 0.0, in_high).astype(BF16)
            else:
                both = jnp.where(low, in_low, in_high)
                vx_ref[slot, g] = jnp.concatenate([both, ones], axis=1).astype(BF16)


def _attn_block(sink_ref, q, kab_ref, vx_ref, s_ref, p_ref, prev, cur, first_block):
    for g in range(N_KV_HEADS):
        q_pair = jnp.concatenate([q[:, (2 * g) * LANES:(2 * g + 1) * LANES],
                                  q[:, (2 * g + 1) * LANES:(2 * g + 2) * LANES]], axis=0)
        for a in range(2):
            kband = jnp.concatenate([kab_ref[prev, 2 * g + a], kab_ref[cur, 2 * g + a]], axis=0)
            s = lax.dot_general(q_pair, kband, (((1,), (1,)), ((), ())), preferred_element_type=F32)
            s_ref[GQA_GROUP * g + a] = s[:WINDOW]
            s_ref[GQA_GROUP * g + 2 + a] = s[WINDOW:]
    qi = lax.broadcasted_iota(I32, (WINDOW, 2 * WINDOW), 0) + WINDOW
    kj = lax.broadcasted_iota(I32, (WINDOW, 2 * WINDOW), 1)
    mask = (kj <= qi) & (kj > qi - WINDOW) & ((kj >= WINDOW) | jnp.logical_not(first_block))
    key0 = lax.broadcasted_iota(I32, (1, 2 * WINDOW), 1) == 0
    for h in range(N_Q_HEADS):
        s = jnp.where(mask, s_ref[h], jnp.where(key0, sink_ref[h], -jnp.inf))
        m = jnp.max(s, axis=-1, keepdims=True)
        p_ref[h] = jnp.exp(s - m).astype(BF16)
    low = lax.broadcasted_iota(I32, (WINDOW, LANES), 1) < HEAD_DIM
    sink_row = ((lax.broadcasted_iota(I32, (BF16_ROWS, 2 * LANES), 0) == 0)
                & (lax.broadcasted_iota(I32, (BF16_ROWS, 2 * LANES), 1) < LANES))
    out_tiles = []
    for g in range(N_KV_HEADS):
        v_prev = vx_ref[prev, g]
        v_head = jnp.where(sink_row, 0.0, v_prev[:BF16_ROWS].astype(F32)).astype(BF16)
        vband = jnp.concatenate([v_head, v_prev[BF16_ROWS:], vx_ref[cur, g]], axis=0)
        p4 = p_ref[GQA_GROUP * g:GQA_GROUP * (g + 1)].reshape(GQA_GROUP * WINDOW, 2 * WINDOW)
        o4 = jnp.dot(p4, vband, preferred_element_type=F32)
        heads = []
        for j in range(GQA_GROUP):
            blk = o4[j * WINDOW:(j + 1) * WINDOW]
            heads.append(blk[:, :LANES] / blk[:, LANES:])
        out_tiles.append(jnp.where(low, heads[0], heads[1]))
        out_tiles.append(jnp.where(low, heads[2], heads[3]))
    return jnp.concatenate(out_tiles, axis=1).astype(BF16)


def _attn_body(sink_ref, q_ref, kv_ref, o_ref, kab_ref, vx_ref, s_ref, p_ref):
    n = pl.program_id(1)

    @pl.when(n == 0)
    def _():
        kab_ref[1] = jnp.zeros(kab_ref.shape[1:], kab_ref.dtype)
        half = (N_KV_HEADS, WINDOW, LANES)
        vx_ref[1] = jnp.concatenate([jnp.zeros(half, BF16), jnp.ones(half, BF16)], axis=-1)

    scratch = (kab_ref, vx_ref, s_ref, p_ref)
    for blk in range(q_ref.shape[0] // WINDOW):
        rows = slice(blk * WINDOW, (blk + 1) * WINDOW)
        cur = blk % 2
        _attn_prepare(kv_ref[rows, :], kab_ref, vx_ref, cur)
        o_ref[rows, :] = _attn_block(sink_ref, q_ref[rows, :], *scratch, 1 - cur, cur,
                                     (n == 0) if blk == 0 else False)


def _attention(qkv, sinks, batch, seq):
    t = qkv.shape[0]
    tq = 8 * WINDOW
    steps = seq // tq
    kv_col = Q_WIDTH // (2 * KV_WIDTH)
    return pl.pallas_call(
        _attn_body,
        grid=(batch, steps),
        in_specs=[
            pl.BlockSpec(memory_space=pltpu.SMEM),
            pl.BlockSpec((tq, Q_WIDTH), lambda b, n: (b * steps + n, 0)),
            pl.BlockSpec((tq, 2 * KV_WIDTH), lambda b, n: (b * steps + n, kv_col)),
        ],
        out_specs=pl.BlockSpec((tq, Q_WIDTH), lambda b, n: (b * steps + n, 0)),
        out_shape=jax.ShapeDtypeStruct((t, Q_WIDTH), BF16),
        scratch_shapes=[
            pltpu.VMEM((2, 2 * N_KV_HEADS, WINDOW, LANES), BF16),
            pltpu.VMEM((2, N_KV_HEADS, WINDOW, 2 * LANES), BF16),
            pltpu.VMEM((N_Q_HEADS, WINDOW, 2 * WINDOW), F32),
            pltpu.VMEM((N_Q_HEADS, WINDOW, 2 * WINDOW), BF16),
        ],
        compiler_params=_cparams(2),
        name="swa_attention",
    )(sinks, qkv, qkv)


def _route(lt):
    tm = lt.shape[1]
    row = lax.broadcasted_iota(I32, (EXPERTS_PER_GROUP, tm), 0)
    neg = -jnp.inf
    gl = jnp.where(row < N_EXPERT_GROUPS, lt[0:EXPERTS_PER_GROUP], neg)
    gm = jnp.max(gl, axis=0, keepdims=True)
    g_p = 1.0 / jnp.sum(jnp.exp(gl - gm), axis=0, keepdims=True)
    g_idx = jnp.min(jnp.where(gl == gm, row, EXPERTS_PER_GROUP), axis=0, keepdims=True)
    sel = lt[EXPERT_COL0 + (N_EXPERT_GROUPS - 1) * EXPERTS_PER_GROUP:EXPERT_COL0 + N_EXPERTS]
    for g in range(N_EXPERT_GROUPS - 2, -1, -1):
        lo = EXPERT_COL0 + g * EXPERTS_PER_GROUP
        sel = jnp.where(g_idx == g, lt[lo:lo + EXPERTS_PER_GROUP], sel)
    v1 = jnp.max(sel, axis=0, keepdims=True)
    i1 = jnp.min(jnp.where(sel == v1, row, EXPERTS_PER_GROUP), axis=0, keepdims=True)
    sel2 = jnp.where(row == i1, neg, sel)
    v2 = jnp.max(sel2, axis=0, keepdims=True)
    i2 = jnp.min(jnp.where(sel2 == v2, row, EXPERTS_PER_GROUP), axis=0, keepdims=True)
    e2 = jnp.exp(v2 - v1)
    w1 = g_p / (1.0 + e2)
    w2 = g_p * e2 / (1.0 + e2)
    base = g_idx * EXPERTS_PER_GROUP
    return base + i1, base + i2, w1, w2


def _proj_body(o_ref, x_ref, mod_ref, w_ref, b_ref, lng_ref, lnb_ref, wr_ref, br_ref,
               x1_ref, h2_ref, ids_ref, wts_ref, wbf_ref):
    @pl.when(pl.program_id(0) == 0)
    def _():
        wbf_ref[...] = w_ref[...].astype(BF16)

    d = D_MODEL
    gt_m = mod_ref[:, 2 * d:3 * d]
    sh_f = mod_ref[:, 3 * d:4 * d]
    sc_f = mod_ref[:, 4 * d:5 * d]
    wr = wr_ref[...].astype(BF16)
    for r in range(x_ref.shape[0] // SUB_ROWS):
        rows = slice(r * SUB_ROWS, (r + 1) * SUB_ROWS)
        y = jnp.dot(o_ref[rows, :], wbf_ref[...], preferred_element_type=F32) + b_ref[...]
        x1 = _layer_norm(DEEPNORM_ALPHA * x_ref[rows, :] + (1.0 + gt_m) * y, lng_ref[...], lnb_ref[...])
        x1_ref[rows, :] = x1
        h2 = x1 * (1.0 + sc_f) + sh_f
        _store_rows(h2_ref, h2, r * SUB_ROWS)
        logits = jnp.dot(h2.astype(BF16), wr, preferred_element_type=F32) + br_ref[...]
        ea, eb, wa, wb = _route(logits.T)
        ids_ref[0:1, rows] = ea
        ids_ref[1:2, rows] = eb
        wts_ref[0:1, rows] = wa
        wts_ref[1:2, rows] = wb


def _proj_ln_router(o, x2, mod_rows, w, b, ln_g, ln_b, w_router, b_router, seq):
    t, k = o.shape
    tm = 1024
    steps_per_batch = seq // tm
    d = D_MODEL
    return pl.pallas_call(
        _proj_body,
        grid=(t // tm,),
        in_specs=[
            pl.BlockSpec((tm, k), lambda i: (i, 0)),
            pl.BlockSpec((tm, d), lambda i: (i, 0)),
            pl.BlockSpec((None, 1, N_MOD * d), lambda i: (i // steps_per_batch, 0, 0)),
            pl.BlockSpec((k, d), lambda i: (0, 0), pipeline_mode=pl.Buffered(1)),
            pl.BlockSpec((1, d), lambda i: (0, 0)),
            pl.BlockSpec((1, d), lambda i: (0, 0)),
            pl.BlockSpec((1, d), lambda i: (0, 0)),
            pl.BlockSpec((d, ROUTER_LANES), lambda i: (0, 0)),
            pl.BlockSpec((1, ROUTER_LANES), lambda i: (0, 0)),
        ],
        out_specs=[
            pl.BlockSpec((tm, d), lambda i: (i, 0)),
            pl.BlockSpec((tm * ROW_CHUNKS, LANES), lambda i: (i, 0)),
            pl.BlockSpec((2, tm), lambda i: (0, i)),
            pl.BlockSpec((2, tm), lambda i: (0, i)),
        ],
        out_shape=[
            jax.ShapeDtypeStruct((t, d), F32),
            jax.ShapeDtypeStruct((t * ROW_CHUNKS, LANES), U32),
            jax.ShapeDtypeStruct((2, t), I32),
            jax.ShapeDtypeStruct((2, t), F32),
        ],
        scratch_shapes=[pltpu.VMEM((k, d), BF16)],
        compiler_params=_cparams(1),
        name="proj_ln_router",
    )(o, x2, mod_rows, w, b.reshape(1, d), ln_g.reshape(1, d), ln_b.reshape(1, d), w_router, b_router)


def _sort_body(ids_ref, pos_ref, te_ref, nused_ref, tv_ref, rank_ref):
    n_rows = ids_ref.shape[0]
    c = SORT_CHUNK
    erow = lax.broadcasted_iota(I32, (N_EXPERTS, c), 0)
    tri = (lax.broadcasted_iota(I32, (c, c), 0) <= lax.broadcasted_iota(I32, (c, c), 1)).astype(BF16)

    def rank_step(r, carry):
        onehot = erow == ids_ref[pl.ds(r, 1), :]
        pref = jnp.dot(onehot.astype(BF16), tri, preferred_element_type=F32)
        rank = jnp.sum(jnp.where(onehot, pref + carry, 0.0), axis=0, keepdims=True) - 1.0
        rank_ref[pl.ds(r, 1), :] = rank
        return carry + pref[:, c - 1:c]

    counts = lax.fori_loop(0, n_rows, rank_step, jnp.zeros((N_EXPERTS, 1), F32), unroll=8)
    n_tile = jnp.floor((counts + (EXPERT_TILE - 1)) * (1.0 / EXPERT_TILE))
    low = (lax.broadcasted_iota(I32, (N_EXPERTS, N_EXPERTS), 1)
           <= lax.broadcasted_iota(I32, (N_EXPERTS, N_EXPERTS), 0)).astype(BF16)
    cum = jnp.dot(low, jnp.broadcast_to(n_tile, (N_EXPERTS, LANES)).astype(BF16),
                  preferred_element_type=F32)[:, 0:1]
    row_off = (cum - n_tile) * EXPERT_TILE

    def pos_step(r, _):
        onehot = erow == ids_ref[pl.ds(r, 1), :]
        off = jnp.sum(jnp.where(onehot, row_off, 0.0), axis=0, keepdims=True)
        pos_ref[pl.ds(r, 1), :] = (off + rank_ref[pl.ds(r, 1), :]).astype(I32)
        return 0

    lax.fori_loop(0, n_rows, pos_step, 0, unroll=8)
    total = jnp.max(cum, axis=0, keepdims=True)
    n_lanes = te_ref.shape[1]
    tile = jnp.minimum(lax.broadcasted_iota(I32, (N_EXPERTS, n_lanes), 1).astype(F32), total - 1.0)
    te_ref[...] = jnp.sum(jnp.where(cum <= tile, 1.0, 0.0), axis=0, keepdims=True).astype(I32)
    nused_ref[...] = jnp.broadcast_to(total, nused_ref.shape).astype(I32)
    tile_f = lax.broadcasted_iota(I32, (N_EXPERTS, n_lanes), 1).astype(F32)
    first = cum - n_tile
    rows_left = jnp.clip(counts - (tile_f - first) * EXPERT_TILE, 0.0, float(EXPERT_TILE))
    owns = (first <= tile_f) & (tile_f < cum)
    tv_ref[...] = jnp.sum(jnp.where(owns, rows_left, 0.0), axis=0, keepdims=True).astype(I32)


def _expert_sort(ids, n_tiles):
    n_assign = ids.shape[0] * ids.shape[1]
    n_rows = n_assign // SORT_CHUNK
    te_lanes = -(-n_tiles // LANES) * LANES
    pos, te, nused, tv = pl.pallas_call(
        _sort_body,
        grid=(1,),
        in_specs=[pl.BlockSpec((n_rows, SORT_CHUNK), lambda i: (0, 0))],
        out_specs=[
            pl.BlockSpec((n_rows, SORT_CHUNK), lambda i: (0, 0)),
            pl.BlockSpec((1, te_lanes), lambda i: (0, 0)),
            pl.BlockSpec((1, LANES), lambda i: (0, 0)),
            pl.BlockSpec((1, te_lanes), lambda i: (0, 0)),
        ],
        out_shape=[
            jax.ShapeDtypeStruct((n_rows, SORT_CHUNK), I32),
            jax.ShapeDtypeStruct((1, te_lanes), I32),
            jax.ShapeDtypeStruct((1, LANES), I32),
            jax.ShapeDtypeStruct((1, te_lanes), I32),
        ],
        scratch_shapes=[pltpu.VMEM((n_rows, SORT_CHUNK), F32)],
        compiler_params=_cparams(1),
        name="expert_sort",
    )(ids.reshape(n_rows, SORT_CHUNK))
    return pos, te[0, :n_tiles], nused[0, :1], tv[0, :n_tiles]


def _sc_mesh():
    return plsc.VectorSubcoreMesh(core_axis_name="c", subcore_axis_name="s", num_cores=SC_CORES,
                                  num_subcores=SC_WORKERS // SC_CORES)


def _sc_ring(n_chunks, read, write):
    reads, writes = {}, {}
    for j in range(min(SC_SLOTS - 1, n_chunks)):
        reads[j] = read(j)
    for j in range(n_chunks):
        for cp in reads.pop(j):
            cp.wait()
        nxt = j + SC_SLOTS - 1
        if nxt < n_chunks:
            for cp in writes.pop(nxt - SC_SLOTS, []):
                cp.wait()
            reads[nxt] = read(nxt)
        writes[j] = write(j)
    for cps in writes.values():
        for cp in cps:
            cp.wait()


def _sc_scatter_rows(src, pos2, n_rows):
    t = pos2.shape[1]
    src3 = src.reshape(t, ROW_CHUNKS, LANES)
    per_worker = t // SC_WORKERS
    n_chunks = per_worker // SC_CHUNK
    idx = pos2.reshape(2, SC_WORKERS, n_chunks, SC_CHUNK).transpose(1, 0, 2, 3)
    idx = idx.reshape(SC_WORKERS, 2 * n_chunks, SC_CHUNK)

    @functools.partial(
        pl.kernel, mesh=_sc_mesh(),
        out_type=jax.ShapeDtypeStruct((n_rows, ROW_CHUNKS, LANES), src.dtype),
        scratch_types=[
            pltpu.VMEM((2 * n_chunks, SC_CHUNK), I32),
            pltpu.VMEM((SC_SLOTS, SC_CHUNK, ROW_CHUNKS, LANES), src.dtype),
            pltpu.SemaphoreType.DMA((SC_SLOTS,)),
            pltpu.SemaphoreType.DMA((SC_SLOTS,)),
        ],
        name="sc_scatter_rows",
    )
    def scatter(src_hbm, idx_hbm, out_hbm, idx_v, rows_v, rsem, wsem):
        wid = lax.axis_index("s") * SC_CORES + lax.axis_index("c")
        pltpu.sync_copy(idx_hbm.at[wid], idx_v)

        def read(j):
            b = j % SC_SLOTS
            return [pltpu.async_copy(src_hbm.at[pl.ds(wid * per_worker + j * SC_CHUNK, SC_CHUNK)], rows_v.at[b],
                                     rsem.at[b])]

        def write(j):
            b = j % SC_SLOTS
            return [pltpu.async_copy(rows_v.at[b], out_hbm.at[idx_v.at[k * n_chunks + j]], wsem.at[b])
                    for k in range(2)]

        _sc_ring(n_chunks, read, write)

    return scatter(src3, idx).reshape(n_rows * ROW_CHUNKS, LANES)


def _sc_gather_rows(table, idx):
    n = idx.shape[0]
    table3 = table.reshape(-1, ROW_CHUNKS, LANES)
    per_worker = n // SC_WORKERS
    n_chunks = per_worker // SC_CHUNK

    @functools.partial(
        pl.kernel, mesh=_sc_mesh(),
        out_type=jax.ShapeDtypeStruct((n, ROW_CHUNKS, LANES), table.dtype),
        scratch_types=[
            pltpu.VMEM((n_chunks, SC_CHUNK), I32),
            pltpu.VMEM((SC_SLOTS, SC_CHUNK, ROW_CHUNKS, LANES), table.dtype),
            pltpu.SemaphoreType.DMA((SC_SLOTS,)),
            pltpu.SemaphoreType.DMA((SC_SLOTS,)),
        ],
        name="sc_gather_rows",
    )
    def gather(table_hbm, idx_hbm, out_hbm, idx_v, rows_v, rsem, wsem):
        wid = lax.axis_index("s") * SC_CORES + lax.axis_index("c")
        pltpu.sync_copy(idx_hbm.at[wid], idx_v)

        def read(j):
            b = j % SC_SLOTS
            return [pltpu.async_copy(table_hbm.at[idx_v.at[j]], rows_v.at[b], rsem.at[b])]

        def write(j):
            b = j % SC_SLOTS
            return [pltpu.async_copy(rows_v.at[b], out_hbm.at[pl.ds(wid * per_worker + j * SC_CHUNK, SC_CHUNK)],
                                     wsem.at[b])]

        _sc_ring(n_chunks, read, write)

    out = gather(table3, idx.reshape(SC_WORKERS, n_chunks, SC_CHUNK))
    return out.reshape(n * ROW_CHUNKS, LANES)


XS_SLOTS = 3


def _expert_body(te_ref, nused_ref, tv_ref, xs_hbm, wgu_ref, wd_ref, ys_ref, wgu_bf, wd_bf, xbuf, xsem):
    i = pl.program_id(0)
    n_used = nused_ref[0]
    used = i < n_used
    prev = te_ref[jnp.maximum(i - 1, 0)]
    tile_rows = EXPERT_TILE * ROW_CHUNKS

    def fetch(tile):
        slot = lax.rem(tile, XS_SLOTS)
        r0 = pl.multiple_of(tile * tile_rows, tile_rows)
        return pltpu.make_async_copy(xs_hbm.at[pl.ds(r0, tile_rows)], xbuf.at[slot], xsem.at[slot])

    @pl.when((i == 0) & (n_used > 0))
    def _():
        fetch(0).start()

    @pl.when((i == 0) & (n_used > 1))
    def _():
        fetch(1).start()

    @pl.when(i + 2 < n_used)
    def _():
        fetch(i + 2).start()

    @pl.when(used & ((i == 0) | (te_ref[i] != prev)))
    def _():
        wgu_bf[...] = wgu_ref[...].astype(BF16)
        wd_bf[...] = wd_ref[...].astype(BF16)

    @pl.when(used)
    def _():
        fetch(i).wait()
        live = lax.broadcasted_iota(I32, (EXPERT_TILE, 1), 0) < tv_ref[i]
        xs = jnp.where(live, _load_rows(xbuf.at[lax.rem(i, XS_SLOTS)], EXPERT_TILE), 0.0).astype(BF16)
        gu = jnp.dot(xs, wgu_bf[...], preferred_element_type=F32)
        gate = gu[:, :EXPERT_FF]
        up = gu[:, EXPERT_FF:]
        act = gate * jax.nn.sigmoid(gate) * up
        _store_rows(ys_ref, jnp.dot(act.astype(BF16), wd_bf[...], preferred_element_type=F32))

    @pl.when(jnp.logical_not(used))
    def _():
        ys_ref[...] = jnp.zeros(ys_ref.shape, ys_ref.dtype)


def _expert_mlp(tile_expert, n_used, tile_valid, xs, w_gate_up, w_down, layer):
    d = D_MODEL
    n_tiles = xs.shape[0] // (EXPERT_TILE * ROW_CHUNKS)
    f2 = 2 * EXPERT_FF
    tile_rows = EXPERT_TILE * ROW_CHUNKS
    e0 = layer * N_EXPERTS
    grid_spec = pltpu.PrefetchScalarGridSpec(
        num_scalar_prefetch=3,
        grid=(n_tiles,),
        in_specs=[
            pl.BlockSpec(memory_space=pl.ANY),
            pl.BlockSpec((None, d, f2), lambda i, te, nu, tv: (e0 + te[i], 0, 0)),
            pl.BlockSpec((None, EXPERT_FF, d), lambda i, te, nu, tv: (e0 + te[i], 0, 0)),
        ],
        out_specs=pl.BlockSpec((tile_rows, LANES), lambda i, te, nu, tv: (i, 0)),
        scratch_shapes=[pltpu.VMEM((d, f2), BF16), pltpu.VMEM((EXPERT_FF, d), BF16),
                        pltpu.VMEM((XS_SLOTS, tile_rows, LANES), U32), pltpu.SemaphoreType.DMA((XS_SLOTS,))],
    )
    return pl.pallas_call(
        _expert_body,
        grid_spec=grid_spec,
        out_shape=jax.ShapeDtypeStruct(xs.shape, U32),
        compiler_params=_cparams(1),
        name="expert_mlp",
    )(tile_expert, n_used, tile_valid, xs, w_gate_up.reshape(DEPTH * N_EXPERTS, d, f2),
      w_down.reshape(DEPTH * N_EXPERTS, EXPERT_FF, d))


def _moe_combine(wts_ref, x_ref, mod_ref, lng_ref, lnb_ref, ya_ref, yb_ref, token0, n):
    d = D_MODEL
    rows = slice(token0, token0 + n)
    cols = []
    for k in range(2):
        wt = jnp.broadcast_to(wts_ref[k:k + 1, rows], (LANES, n)).T
        cols.append(jnp.concatenate([wt] * (d // LANES), axis=1))
    y = cols[0] * _load_rows(ya_ref, n, token0) + cols[1] * _load_rows(yb_ref, n, token0)
    gt_f = mod_ref[:, 5 * d:6 * d]
    return _layer_norm(DEEPNORM_ALPHA * x_ref[rows, :] + (1.0 + gt_f) * y, lng_ref[...], lnb_ref[...])


def _combine_specs(tm, n_steps, steps_per_batch):
    d = D_MODEL
    return [
        pl.BlockSpec((2, tm), lambda i: (0, i)),
        pl.BlockSpec((tm, d), lambda i: (i, 0)),
        pl.BlockSpec((None, 1, N_MOD * d), lambda i: (i // steps_per_batch, 0, 0)),
        pl.BlockSpec((1, d), lambda i: (0, 0)),
        pl.BlockSpec((1, d), lambda i: (0, 0)),
        pl.BlockSpec((tm * ROW_CHUNKS, LANES), lambda i: (i, 0)),
        pl.BlockSpec((tm * ROW_CHUNKS, LANES), lambda i: (n_steps + i, 0)),
    ]


def _gather_rows_of_tokens(ys, pos2):
    return _sc_gather_rows(ys, pos2.reshape(-1))


def _combine_body(wts_ref, x_ref, mod_ref, lng_ref, lnb_ref, ya_ref, yb_ref, o_ref):
    for r in range(x_ref.shape[0] // SUB_ROWS):
        o_ref[r * SUB_ROWS:(r + 1) * SUB_ROWS, :] = _moe_combine(wts_ref, x_ref, mod_ref, lng_ref, lnb_ref, ya_ref,
                                                                 yb_ref, r * SUB_ROWS, SUB_ROWS)


def _combine_ln(moe, mod_rows, ln_g, ln_b, seq):
    wts, x1, ys, pos2 = moe
    yg = _gather_rows_of_tokens(ys, pos2)
    t, d = x1.shape
    tm = 1024
    n_steps = t // tm
    return pl.pallas_call(
        _combine_body,
        grid=(n_steps,),
        in_specs=_combine_specs(tm, n_steps, seq // tm),
        out_specs=pl.BlockSpec((tm, d), lambda i: (i, 0)),
        out_shape=jax.ShapeDtypeStruct((t, d), F32),
        compiler_params=_cparams(1),
        name="moe_combine_ln",
    )(wts, x1, mod_rows, ln_g.reshape(1, d), ln_b.reshape(1, d), yg, yg)


def _gmlp_body(wts_ref, x1_ref, modp_ref, lng_ref, lnb_ref, ya_ref, yb_ref,
               mod_ref, w_ref, b_ref, g_ref, beta_ref, ws_ref, bs_ref, x2_ref, o_ref, ws_bf, w_bf):
    @pl.when(pl.program_id(0) == 0)
    def _():
        tri = lax.broadcasted_iota(I32, (CHUNK, CHUNK), 0) >= lax.broadcasted_iota(I32, (CHUNK, CHUNK), 1)
        for g in range(N_SGU_GROUPS):
            ws_bf[g] = jnp.where(tri, ws_ref[g], 0.0).astype(BF16)
        w_bf[...] = w_ref[...].astype(BF16)

    d = D_MODEL
    sh = mod_ref[:, 0:d]
    sc = mod_ref[:, d:2 * d]
    sub_rows = x1_ref.shape[0]
    for r in range(x1_ref.shape[0] // sub_rows):
        r0 = r * sub_rows
        x2 = _moe_combine(wts_ref, x1_ref, modp_ref, lng_ref, lnb_ref, ya_ref, yb_ref, r0, sub_rows)
        x2_ref[r0:r0 + sub_rows, :] = x2
        h = x2 * (1.0 + sc) + sh
        z = jnp.dot(h.astype(BF16), w_bf[...], preferred_element_type=F32) + b_ref[...]
        z = 0.5 * z * (1.0 + lax.erf(z * (2.0 ** -0.5)))
        u = z[:, :GMLP_WIDTH]
        v = _layer_norm(z[:, GMLP_WIDTH:], g_ref[...], beta_ref[...]).astype(BF16)
        for ci in range(sub_rows // CHUNK):
            rows = slice(ci * CHUNK, (ci + 1) * CHUNK)
            out_rows = slice(r0 + ci * CHUNK, r0 + (ci + 1) * CHUNK)
            for g in range(N_SGU_GROUPS):
                lanes = slice(g * SGU_GROUP_DIM, (g + 1) * SGU_GROUP_DIM)
                mixed = jnp.dot(ws_bf[g], v[rows, lanes], preferred_element_type=F32) + bs_ref[:, g:g + 1]
                o_ref[out_rows, lanes] = (u[rows, lanes] * mixed).astype(BF16)


def _gmlp_gate(moe, ln_g, ln_b, mod_prev, mod_rows, w_in, b_in, sgu_g, sgu_b, w_s, b_s, seq):
    wts, x1, ys, pos2 = moe
    yg = _gather_rows_of_tokens(ys, pos2)
    t, d = x1.shape
    tm = 512
    steps_per_batch = seq // tm
    n_steps = t // tm
    gw = GMLP_WIDTH
    return pl.pallas_call(
        _gmlp_body,
        grid=(n_steps,),
        in_specs=_combine_specs(tm, n_steps, steps_per_batch) + [
            pl.BlockSpec((None, 1, N_MOD * d), lambda i: (i // steps_per_batch, 0, 0)),
            pl.BlockSpec((d, 2 * gw), lambda i: (0, 0)),
            pl.BlockSpec((1, 2 * gw), lambda i: (0, 0)),
            pl.BlockSpec((1, gw), lambda i: (0, 0)),
            pl.BlockSpec((1, gw), lambda i: (0, 0)),
            pl.BlockSpec((N_SGU_GROUPS, CHUNK, CHUNK), lambda i: (0, 0, 0)),
            pl.BlockSpec((CHUNK, N_SGU_GROUPS), lambda i: (0, 0)),
        ],
        out_specs=[pl.BlockSpec((tm, d), lambda i: (i, 0)), pl.BlockSpec((tm, gw), lambda i: (i, 0))],
        out_shape=[jax.ShapeDtypeStruct((t, d), F32), jax.ShapeDtypeStruct((t, gw), BF16)],
        scratch_shapes=[pltpu.VMEM((N_SGU_GROUPS, CHUNK, CHUNK), BF16), pltpu.VMEM((d, 2 * gw), BF16)],
        compiler_params=_cparams(1),
        name="combine_gmlp_gate",
    )(wts, x1, mod_prev, ln_g.reshape(1, d), ln_b.reshape(1, d), yg, yg,
      mod_rows, w_in, b_in.reshape(1, 2 * gw), sgu_g.reshape(1, gw), sgu_b.reshape(1, gw), w_s, b_s.T)


def _router_params(w_group, b_group, w_expert, b_expert):
    def lanes(group_part, expert_part):
        rows = group_part.shape[0]
        gap = jnp.zeros((rows, EXPERT_COL0 - N_EXPERT_GROUPS), F32)
        tail = jnp.zeros((rows, ROUTER_LANES - EXPERT_COL0 - N_EXPERTS), F32)
        return jnp.concatenate([group_part, gap, expert_part, tail], axis=1)

    return lanes(w_group, w_expert), lanes(b_group[None, :], b_expert[None, :])


def _moe_experts(x1, h2, ids, wts, layer, w_gate_up, w_down):
    t = x1.shape[0]
    n_rows = 2 * t + N_EXPERTS * EXPERT_TILE
    n_tiles = n_rows // EXPERT_TILE
    pos, tile_expert, n_used, tile_valid = _expert_sort(ids, n_tiles)
    pos2 = pos.reshape(2, t)
    xs = _sc_scatter_rows(h2, pos2, n_rows)
    ys = _expert_mlp(tile_expert, n_used, tile_valid, xs, w_gate_up, w_down, layer)
    return wts, x1, ys, pos2


def kernel(x, c, positions, ada_w, ada_b, post_ln_g, post_ln_b, attn_w_qkv, attn_b_qkv, attn_sinks, attn_w_o, attn_b_o, gmlp_w_in, gmlp_b_in, gmlp_sgu_ln_g, gmlp_sgu_ln_b, gmlp_w_s, gmlp_b_s, gmlp_w_out, gmlp_b_out, moe_w_group_router, moe_b_group_router, moe_w_expert_router, moe_b_expert_router, moe_w_gate_up, moe_w_down):
    batch, seq, d = x.shape
    t = batch * seq
    assert d == D_MODEL and batch <= MOD_ROWS and ada_w.shape[0] == DEPTH == 2
    assert seq % 1024 == 0, "token tiles of 1024 rows must not straddle sequences"
    assert t % (SC_WORKERS * SC_CHUNK) == 0, "every SparseCore subcore moves whole chunks"
    x2 = x.reshape(t, d)
    c_pad = jnp.pad(c, ((0, MOD_ROWS - batch), (0, 0)))
    mods = [_adaln_mod(c_pad, ada_w, ada_b, layer) for layer in range(DEPTH)]

    moe = None
    for layer in range(DEPTH):
        j = layer // 2
        if layer % 2 == 0:
            if moe is not None:
                x2 = _combine_ln(moe, mods[layer - 1], post_ln_g[layer - 1, 1], post_ln_b[layer - 1, 1], seq)
            qkv = _qkv_rope(x2, mods[layer], positions, attn_w_qkv[j], attn_b_qkv[j], seq)
            mix = _attention(qkv, attn_sinks[j], batch, seq)
            w_out, b_out = attn_w_o[j], attn_b_o[j]
        else:
            x2, mix = _gmlp_gate(moe, post_ln_g[layer - 1, 1], post_ln_b[layer - 1, 1], mods[layer - 1], mods[layer],
                                 gmlp_w_in[j], gmlp_b_in[j], gmlp_sgu_ln_g[j], gmlp_sgu_ln_b[j],
                                 gmlp_w_s[j], gmlp_b_s[j], seq)
            w_out, b_out = gmlp_w_out[j], gmlp_b_out[j]
        w_router, b_router = _router_params(moe_w_group_router[layer], moe_b_group_router[layer],
                                            moe_w_expert_router[layer], moe_b_expert_router[layer])
        x1, h2, ids, wts = _proj_ln_router(mix, x2, mods[layer], w_out, b_out, post_ln_g[layer, 0],
                                           post_ln_b[layer, 0], w_router, b_router, seq)
        moe = _moe_experts(x1, h2, ids, wts, layer, moe_w_gate_up, moe_w_down)
    x2 = _combine_ln(moe, mods[DEPTH - 1], post_ln_g[DEPTH - 1, 1], post_ln_b[DEPTH - 1, 1], seq)
    return x2.reshape(batch, seq, d)
```

```python
import functools

import jax
import jax.numpy as jnp
from jax import lax
from jax.experimental import pallas as pl
from jax.experimental.pallas import tpu as pltpu
from jax.experimental.pallas import tpu_sc as plsc

F32 = jnp.float32
BF16 = jnp.bfloat16
I32 = jnp.int32

D_MODEL = 1024
DEPTH = 2
HEAD_DIM = 64
N_Q_HEADS = 16
N_KV_HEADS = 4
GQA_GROUP = N_Q_HEADS // N_KV_HEADS
WINDOW = 128
ROPE_THETA = 10000.0
Q_WIDTH = N_Q_HEADS * HEAD_DIM
KV_WIDTH = N_KV_HEADS * HEAD_DIM
QKV_WIDTH = Q_WIDTH + 2 * KV_WIDTH
CHUNK = 128
GMLP_WIDTH = 2 * D_MODEL
N_SGU_GROUPS = 8
SGU_GROUP_DIM = GMLP_WIDTH // N_SGU_GROUPS
N_EXPERT_GROUPS = 4
EXPERTS_PER_GROUP = 8
N_EXPERTS = N_EXPERT_GROUPS * EXPERTS_PER_GROUP
EXPERT_FF = D_MODEL // 4
N_MOD = 6
DEEPNORM_ALPHA = (2.0 * DEPTH) ** 0.25
LN_EPS = 1e-5

LANES = 128
MOD_ROWS = 8
ROUTER_LANES = 128
EXPERT_COL0 = 8
SORT_CHUNK = 256
EXPERT_TILE = 512
SUB_ROWS = 256
VMEM_LIMIT = 56 * 1024 * 1024
SC_CORES = 2
SC_WORKERS = 32
SC_CHUNK = 64
SC_SLOTS = 3


def _cparams(n_axes, vmem=VMEM_LIMIT):
    return pltpu.CompilerParams(dimension_semantics=("arbitrary",) * n_axes, vmem_limit_bytes=vmem)


U32 = jnp.uint32
ROW_CHUNKS = D_MODEL // 2 // LANES


def _store_rows(ref, val, token0=0):
    n = val.shape[0]
    half = D_MODEL // 2
    words = pltpu.pack_elementwise([val[:, :half], val[:, half:]], packed_dtype=BF16)
    for c in range(ROW_CHUNKS):
        ref[pl.ds(token0 * ROW_CHUNKS + c, n, stride=ROW_CHUNKS), :] = words[:, c * LANES:(c + 1) * LANES]


def _load_rows(ref, n, token0=0):
    words = jnp.concatenate([ref[pl.ds(token0 * ROW_CHUNKS + c, n, stride=ROW_CHUNKS), :]
                             for c in range(ROW_CHUNKS)], axis=1)
    lo = lax.bitcast_convert_type(words << 16, F32)
    hi = lax.bitcast_convert_type(words & jnp.uint32(0xFFFF0000), F32)
    return jnp.concatenate([lo, hi], axis=1)


def _layer_norm(r, g, b):
    mu = jnp.mean(r, axis=-1, keepdims=True)
    d = r - mu
    var = jnp.mean(d * d, axis=-1, keepdims=True)
    return d * lax.rsqrt(var + LN_EPS) * g + b


def _mod_body(c_ref, w_ref, b_ref, o_ref):
    c = c_ref[...]
    ca = c * jax.nn.sigmoid(c)
    o_ref[...] = jnp.dot(ca.astype(BF16), w_ref[...].astype(BF16), preferred_element_type=F32) + b_ref[...]


def _adaln_mod(c_pad, ada_w, ada_b, layer):
    tn = 1536
    n_out = N_MOD * D_MODEL
    mod = pl.pallas_call(
        _mod_body,
        grid=(n_out // tn,),
        in_specs=[
            pl.BlockSpec((MOD_ROWS, D_MODEL), lambda j: (0, 0)),
            pl.BlockSpec((None, D_MODEL, tn), lambda j: (layer, 0, j)),
            pl.BlockSpec((None, 1, tn), lambda j: (layer, 0, j)),
        ],
        out_specs=pl.BlockSpec((MOD_ROWS, tn), lambda j: (0, j)),
        out_shape=jax.ShapeDtypeStruct((MOD_ROWS, n_out), F32),
        compiler_params=_cparams(1),
        name="adaln_mod",
    )(c_pad, ada_w, ada_b.reshape(DEPTH, 1, n_out))
    return mod.reshape(MOD_ROWS, 1, n_out)


def _qkv_body(x_ref, mod_ref, pos_ref, w_ref, b_ref, invf_ref, o_ref, wbf_ref):
    @pl.when(pl.program_id(0) == 0)
    def _():
        wbf_ref[...] = w_ref[...].astype(BF16)

    tm = x_ref.shape[0]
    sh = mod_ref[:, 0:D_MODEL]
    sc = mod_ref[:, D_MODEL:2 * D_MODEL]
    h = x_ref[...] * (1.0 + sc) + sh
    qkv = jnp.dot(h.astype(BF16), wbf_ref[...], preferred_element_type=F32) + b_ref[...]

    ang = invf_ref[...] * pos_ref[...].astype(F32)
    c = jnp.cos(ang)
    s = jnp.sin(ang)
    ct = jnp.concatenate([c, c, c, c], axis=0).T
    st = jnp.concatenate([-s, s, -s, s], axis=0).T
    lane = lax.broadcasted_iota(I32, (tm, LANES), 1)
    first_half = (lane & (HEAD_DIM // 2)) == 0
    n_rope = (Q_WIDTH + KV_WIDTH) // LANES
    for j in range(n_rope):
        blk = qkv[:, j * LANES:(j + 1) * LANES]
        rot = jnp.where(first_half, pltpu.roll(blk, LANES - HEAD_DIM // 2, 1), pltpu.roll(blk, HEAD_DIM // 2, 1))
        r = blk * ct + rot * st
        if j < Q_WIDTH // LANES:
            r = r * (HEAD_DIM ** -0.5)
        o_ref[:, j * LANES:(j + 1) * LANES] = r.astype(BF16)
    o_ref[:, Q_WIDTH + KV_WIDTH:] = qkv[:, Q_WIDTH + KV_WIDTH:].astype(BF16)


def _qkv_rope(x2, mod_rows, positions, w_qkv, b_qkv, seq):
    t = x2.shape[0]
    tm = 1024
    steps_per_batch = seq // tm
    inv_freq = ROPE_THETA ** (-jnp.arange(0, HEAD_DIM, 2, dtype=F32) / HEAD_DIM)
    return pl.pallas_call(
        _qkv_body,
        grid=(t // tm,),
        in_specs=[
            pl.BlockSpec((tm, D_MODEL), lambda i: (i, 0)),
            pl.BlockSpec((None, 1, N_MOD * D_MODEL), lambda i: (i // steps_per_batch, 0, 0)),
            pl.BlockSpec((None, 1, tm), lambda i: (i, 0, 0)),
            pl.BlockSpec((D_MODEL, QKV_WIDTH), lambda i: (0, 0)),
            pl.BlockSpec((1, QKV_WIDTH), lambda i: (0, 0)),
            pl.BlockSpec((HEAD_DIM // 2, 1), lambda i: (0, 0)),
        ],
        out_specs=pl.BlockSpec((tm, QKV_WIDTH), lambda i: (i, 0)),
        out_shape=jax.ShapeDtypeStruct((t, QKV_WIDTH), BF16),
        scratch_shapes=[pltpu.VMEM((D_MODEL, QKV_WIDTH), BF16)],
        compiler_params=_cparams(1),
        name="qkv_rope",
    )(x2, mod_rows, positions.reshape(t // tm, 1, tm), w_qkv, b_qkv.reshape(1, QKV_WIDTH),
      inv_freq.reshape(HEAD_DIM // 2, 1))


BF16_ROWS = 16


def _attn_prepare(kv, kab_ref, vx_ref, slot):
    kv = kv.astype(F32)
    low = lax.broadcasted_iota(I32, (WINDOW, LANES), 1) < HEAD_DIM
    ones = jnp.ones((WINDOW, LANES), F32)
    for g in range(N_KV_HEADS):
        for part, ref in ((0, None), (KV_WIDTH, vx_ref)):
            tile = kv[:, part + (g // 2) * LANES:part + (g // 2 + 1) * LANES]
            other = pltpu.roll(tile, HEAD_DIM, 1)
            in_low, in_high = (tile, other) if g % 2 == 0 else (other, tile)
            if ref is None:
                kab_ref[slot, 2 * g] = jnp.where(low, in_low, 0.0).astype(BF16)
                kab_ref[slot, 2 * g + 1] = jnp.where(low, 0.0, in_high).astype(BF16)
            else:
                both = jnp.where(low, in_low, in_high)
                vx_ref[slot, g] = jnp.concatenate([both, ones], axis=1).astype(BF16)


def _attn_block(sink_ref, q, kab_ref, vx_ref, s_ref, p_ref, prev, cur, first_block):
    for g in range(N_KV_HEADS):
        q_pair = jnp.concatenate([q[:, (2 * g) * LANES:(2 * g + 1) * LANES],
                                  q[:, (2 * g + 1) * LANES:(2 * g + 2) * LANES]], axis=0)
        for a in range(2):
            kband = jnp.concatenate([kab_ref[prev, 2 * g + a], kab_ref[cur, 2 * g + a]], axis=0)
            s = lax.dot_general(q_pair, kband, (((1,), (1,)), ((), ())), preferred_element_type=F32)
            s_ref[GQA_GROUP * g + a] = s[:WINDOW]
            s_ref[GQA_GROUP * g + 2 + a] = s[WINDOW:]
    qi = lax.broadcasted_iota(I32, (WINDOW, 2 * WINDOW), 0) + WINDOW
    kj = lax.broadcasted_iota(I32, (WINDOW, 2 * WINDOW), 1)
    mask = (kj <= qi) & (kj > qi - WINDOW) & ((kj >= WINDOW) | jnp.logical_not(first_block))
    key0 = lax.broadcasted_iota(I32, (1, 2 * WINDOW), 1) == 0
    for h in range(N_Q_HEADS):
        s = jnp.where(mask, s_ref[h], jnp.where(key0, sink_ref[h], -jnp.inf))
        m = jnp.max(s, axis=-1, keepdims=True)
        p_ref[h] = jnp.exp(s - m).astype(BF16)
    low = lax.broadcasted_iota(I32, (WINDOW, LANES), 1) < HEAD_DIM
    sink_row = ((lax.broadcasted_iota(I32, (BF16_ROWS, 2 * LANES), 0) == 0)
                & (lax.broadcasted_iota(I32, (BF16_ROWS, 2 * LANES), 1) < LANES))
    out_tiles = []
    for g in range(N_KV_HEADS):
        v_prev = vx_ref[prev, g]
        v_head = jnp.where(sink_row, 0.0, v_prev[:BF16_ROWS].astype(F32)).astype(BF16)
        vband = jnp.concatenate([v_head, v_prev[BF16_ROWS:], vx_ref[cur, g]], axis=0)
        p4 = p_ref[GQA_GROUP * g:GQA_GROUP * (g + 1)].reshape(GQA_GROUP * WINDOW, 2 * WINDOW)
        o4 = jnp.dot(p4, vband, preferred_element_type=F32)
        heads = []
        for j in range(GQA_GROUP):
            blk = o4[j * WINDOW:(j + 1) * WINDOW]
            heads.append(blk[:, :LANES] / blk[:, LANES:])
        out_tiles.append(jnp.where(low, heads[0], heads[1]))
        out_tiles.append(jnp.where(low, heads[2], heads[3]))
    return jnp.concatenate(out_tiles, axis=1).astype(BF16)


def _attn_body(sink_ref, q_ref, kv_ref, o_ref, kab_ref, vx_ref, s_ref, p_ref):
    n = pl.program_id(1)

    @pl.when(n == 0)
    def _():
        kab_ref[1] = jnp.zeros(kab_ref.shape[1:], kab_ref.dtype)
        half = (N_KV_HEADS, WINDOW, LANES)
        vx_ref[1] = jnp.concatenate([jnp.zeros(half, BF16), jnp.ones(half, BF16)], axis=-1)

    scratch = (kab_ref, vx_ref, s_ref, p_ref)
    for blk in range(q_ref.shape[0] // WINDOW):
        rows = slice(blk * WINDOW, (blk + 1) * WINDOW)
        cur = blk % 2
        _attn_prepare(kv_ref[rows, :], kab_ref, vx_ref, cur)
        o_ref[rows, :] = _attn_block(sink_ref, q_ref[rows, :], *scratch, 1 - cur, cur,
                                     (n == 0) if blk == 0 else False)


def _attention(qkv, sinks, batch, seq):
    t = qkv.shape[0]
    tq = 8 * WINDOW
    steps = seq // tq
    kv_col = Q_WIDTH // (2 * KV_WIDTH)
    return pl.pallas_call(
        _attn_body,
        grid=(batch, steps),
        in_specs=[
            pl.BlockSpec(memory_space=pltpu.SMEM),
            pl.BlockSpec((tq, Q_WIDTH), lambda b, n: (b * steps + n, 0)),
            pl.BlockSpec((tq, 2 * KV_WIDTH), lambda b, n: (b * steps + n, kv_col)),
        ],
        out_specs=pl.BlockSpec((tq, Q_WIDTH), lambda b, n: (b * steps + n, 0)),
        out_shape=jax.ShapeDtypeStruct((t, Q_WIDTH), BF16),
        scratch_shapes=[
            pltpu.VMEM((2, 2 * N_KV_HEADS, WINDOW, LANES), BF16),
            pltpu.VMEM((2, N_KV_HEADS, WINDOW, 2 * LANES), BF16),
            pltpu.VMEM((N_Q_HEADS, WINDOW, 2 * WINDOW), F32),
            pltpu.VMEM((N_Q_HEADS, WINDOW, 2 * WINDOW), BF16),
        ],
        compiler_params=_cparams(2),
        name="swa_attention",
    )(sinks, qkv, qkv)


def _route(lt):
    tm = lt.shape[1]
    row = lax.broadcasted_iota(I32, (EXPERTS_PER_GROUP, tm), 0)
    neg = -jnp.inf
    gl = jnp.where(row < N_EXPERT_GROUPS, lt[0:EXPERTS_PER_GROUP], neg)
    gm = jnp.max(gl, axis=0, keepdims=True)
    g_p = 1.0 / jnp.sum(jnp.exp(gl - gm), axis=0, keepdims=True)
    g_idx = jnp.min(jnp.where(gl == gm, row, EXPERTS_PER_GROUP), axis=0, keepdims=True)
    sel = lt[EXPERT_COL0 + (N_EXPERT_GROUPS - 1) * EXPERTS_PER_GROUP:EXPERT_COL0 + N_EXPERTS]
    for g in range(N_EXPERT_GROUPS - 2, -1, -1):
        lo = EXPERT_COL0 + g * EXPERTS_PER_GROUP
        sel = jnp.where(g_idx == g, lt[lo:lo + EXPERTS_PER_GROUP], sel)
    v1 = jnp.max(sel, axis=0, keepdims=True)
    i1 = jnp.min(jnp.where(sel == v1, row, EXPERTS_PER_GROUP), axis=0, keepdims=True)
    sel2 = jnp.where(row == i1, neg, sel)
    v2 = jnp.max(sel2, axis=0, keepdims=True)
    i2 = jnp.min(jnp.where(sel2 == v2, row, EXPERTS_PER_GROUP), axis=0, keepdims=True)
    e2 = jnp.exp(v2 - v1)
    w1 = g_p / (1.0 + e2)
    w2 = g_p * e2 / (1.0 + e2)
    base = g_idx * EXPERTS_PER_GROUP
    return base + i1, base + i2, w1, w2


def _proj_body(o_ref, x_ref, mod_ref, w_ref, b_ref, lng_ref, lnb_ref, wr_ref, br_ref,
               x1_ref, h2_ref, ids_ref, wts_ref, wbf_ref):
    @pl.when(pl.program_id(0) == 0)
    def _():
        wbf_ref[...] = w_ref[...].astype(BF16)

    d = D_MODEL
    gt_m = mod_ref[:, 2 * d:3 * d]
    sh_f = mod_ref[:, 3 * d:4 * d]
    sc_f = mod_ref[:, 4 * d:5 * d]
    wr = wr_ref[...].astype(BF16)
    for r in range(x_ref.shape[0] // SUB_ROWS):
        rows = slice(r * SUB_ROWS, (r + 1) * SUB_ROWS)
        y = jnp.dot(o_ref[rows, :], wbf_ref[...], preferred_element_type=F32) + b_ref[...]
        x1 = _layer_norm(DEEPNORM_ALPHA * x_ref[rows, :] + (1.0 + gt_m) * y, lng_ref[...], lnb_ref[...])
        x1_ref[rows, :] = x1
        h2 = x1 * (1.0 + sc_f) + sh_f
        _store_rows(h2_ref, h2, r * SUB_ROWS)
        logits = jnp.dot(h2.astype(BF16), wr, preferred_element_type=F32) + br_ref[...]
        ea, eb, wa, wb = _route(logits.T)
        ids_ref[0:1, rows] = ea
        ids_ref[1:2, rows] = eb
        wts_ref[0:1, rows] = wa
        wts_ref[1:2, rows] = wb


def _proj_ln_router(o, x2, mod_rows, w, b, ln_g, ln_b, w_router, b_router, seq):
    t, k = o.shape
    tm = 1024
    steps_per_batch = seq // tm
    d = D_MODEL
    return pl.pallas_call(
        _proj_body,
        grid=(t // tm,),
        in_specs=[
            pl.BlockSpec((tm, k), lambda i: (i, 0)),
            pl.BlockSpec((tm, d), lambda i: (i, 0)),
            pl.BlockSpec((None, 1, N_MOD * d), lambda i: (i // steps_per_batch, 0, 0)),
            pl.BlockSpec((k, d), lambda i: (0, 0), pipeline_mode=pl.Buffered(1)),
            pl.BlockSpec((1, d), lambda i: (0, 0)),
            pl.BlockSpec((1, d), lambda i: (0, 0)),
            pl.BlockSpec((1, d), lambda i: (0, 0)),
            pl.BlockSpec((d, ROUTER_LANES), lambda i: (0, 0)),
            pl.BlockSpec((1, ROUTER_LANES), lambda i: (0, 0)),
        ],
        out_specs=[
            pl.BlockSpec((tm, d), lambda i: (i, 0)),
            pl.BlockSpec((tm * ROW_CHUNKS, LANES), lambda i: (i, 0)),
            pl.BlockSpec((2, tm), lambda i: (0, i)),
            pl.BlockSpec((2, tm), lambda i: (0, i)),
        ],
        out_shape=[
            jax.ShapeDtypeStruct((t, d), F32),
            jax.ShapeDtypeStruct((t * ROW_CHUNKS, LANES), U32),
            jax.ShapeDtypeStruct((2, t), I32),
            jax.ShapeDtypeStruct((2, t), F32),
        ],
        scratch_shapes=[pltpu.VMEM((k, d), BF16)],
        compiler_params=_cparams(1),
        name="proj_ln_router",
    )(o, x2, mod_rows, w, b.reshape(1, d), ln_g.reshape(1, d), ln_b.reshape(1, d), w_router, b_router)


def _sort_body(ids_ref, pos_ref, te_ref, nused_ref, tv_ref, wplan_ref, rank_ref):
    n_rows = ids_ref.shape[0]
    c = SORT_CHUNK
    erow = lax.broadcasted_iota(I32, (N_EXPERTS, c), 0)
    tri = (lax.broadcasted_iota(I32, (c, c), 0) <= lax.broadcasted_iota(I32, (c, c), 1)).astype(BF16)

    def rank_step(r, carry):
        onehot = erow == ids_ref[pl.ds(r, 1), :]
        pref = jnp.dot(onehot.astype(BF16), tri, preferred_element_type=F32)
        rank = jnp.sum(jnp.where(onehot, pref + carry, 0.0), axis=0, keepdims=True) - 1.0
        rank_ref[pl.ds(r, 1), :] = rank
        return carry + pref[:, c - 1:c]

    counts = lax.fori_loop(0, n_rows, rank_step, jnp.zeros((N_EXPERTS, 1), F32), unroll=8)
    n_tile = jnp.floor((counts + (EXPERT_TILE - 1)) * (1.0 / EXPERT_TILE))
    low = (lax.broadcasted_iota(I32, (N_EXPERTS, N_EXPERTS), 1)
           <= lax.broadcasted_iota(I32, (N_EXPERTS, N_EXPERTS), 0)).astype(BF16)
    cum = jnp.dot(low, jnp.broadcast_to(n_tile, (N_EXPERTS, LANES)).astype(BF16),
                  preferred_element_type=F32)[:, 0:1]
    row_off = (cum - n_tile) * EXPERT_TILE

    def pos_step(r, _):
        onehot = erow == ids_ref[pl.ds(r, 1), :]
        off = jnp.sum(jnp.where(onehot, row_off, 0.0), axis=0, keepdims=True)
        pos_ref[pl.ds(r, 1), :] = (off + rank_ref[pl.ds(r, 1), :]).astype(I32)
        return 0

    lax.fori_loop(0, n_rows, pos_step, 0, unroll=8)
    total = jnp.max(cum, axis=0, keepdims=True)
    n_lanes = te_ref.shape[1]
    tile = jnp.minimum(lax.broadcasted_iota(I32, (N_EXPERTS, n_lanes), 1).astype(F32), total - 1.0)
    te_ref[...] = jnp.sum(jnp.where(cum <= tile, 1.0, 0.0), axis=0, keepdims=True).astype(I32)
    nused_ref[...] = jnp.broadcast_to(total, nused_ref.shape).astype(I32)
    tile_f = lax.broadcasted_iota(I32, (N_EXPERTS, n_lanes), 1).astype(F32)
    first = cum - n_tile
    rows_left = jnp.clip(counts - (tile_f - first) * EXPERT_TILE, 0.0, float(EXPERT_TILE))
    owns = (first <= tile_f) & (tile_f < cum)
    tv_ref[...] = jnp.sum(jnp.where(owns, rows_left, 0.0), axis=0, keepdims=True).astype(I32)
    def lane_sum(x):
        return jnp.sum(x, axis=0, keepdims=True)

    is_first = lane_sum(jnp.where(owns & (tile_f == first), 1.0, 0.0))
    group_end = lane_sum(jnp.where(owns, cum, 0.0))
    next_expert = jnp.where(group_end < total, lane_sum(jnp.where(cum <= group_end, 1.0, 0.0)), -1.0)
    ordinal = lane_sum(jnp.where((cum <= tile_f[0:1]) & (n_tile > 0.0), 1.0, 0.0))
    slot = ordinal - 2.0 * jnp.floor(ordinal * 0.5)
    wplan_ref[0:1, :] = is_first.astype(I32)
    wplan_ref[1:2, :] = next_expert.astype(I32)
    wplan_ref[2:3, :] = slot.astype(I32)
    wplan_ref[3:8, :] = jnp.zeros((5, n_lanes), I32)


def _expert_sort(ids, n_tiles):
    n_assign = ids.shape[0] * ids.shape[1]
    n_rows = n_assign // SORT_CHUNK
    te_lanes = -(-n_tiles // LANES) * LANES
    pos, te, nused, tv, wplan = pl.pallas_call(
        _sort_body,
        grid=(1,),
        in_specs=[pl.BlockSpec((n_rows, SORT_CHUNK), lambda i: (0, 0))],
        out_specs=[
            pl.BlockSpec((n_rows, SORT_CHUNK), lambda i: (0, 0)),
            pl.BlockSpec((1, te_lanes), lambda i: (0, 0)),
            pl.BlockSpec((1, LANES), lambda i: (0, 0)),
            pl.BlockSpec((1, te_lanes), lambda i: (0, 0)),
            pl.BlockSpec((8, te_lanes), lambda i: (0, 0)),
        ],
        out_shape=[
            jax.ShapeDtypeStruct((n_rows, SORT_CHUNK), I32),
            jax.ShapeDtypeStruct((1, te_lanes), I32),
            jax.ShapeDtypeStruct((1, LANES), I32),
            jax.ShapeDtypeStruct((1, te_lanes), I32),
            jax.ShapeDtypeStruct((8, te_lanes), I32),
        ],
        scratch_shapes=[pltpu.VMEM((n_rows, SORT_CHUNK), F32)],
        compiler_params=_cparams(1),
        name="expert_sort",
    )(ids.reshape(n_rows, SORT_CHUNK))
    tile_plan = (te[0, :n_tiles], nused[0, :1], tv[0, :n_tiles],
                 wplan[0, :n_tiles], wplan[1, :n_tiles], wplan[2, :n_tiles])
    return pos, tile_plan


def _sc_mesh():
    return plsc.VectorSubcoreMesh(core_axis_name="c", subcore_axis_name="s", num_cores=SC_CORES,
                                  num_subcores=SC_WORKERS // SC_CORES)


def _sc_ring(n_chunks, read, write):
    reads, writes = {}, {}
    for j in range(min(SC_SLOTS - 1, n_chunks)):
        reads[j] = read(j)
    for j in range(n_chunks):
        for cp in reads.pop(j):
            cp.wait()
        nxt = j + SC_SLOTS - 1
        if nxt < n_chunks:
            for cp in writes.pop(nxt - SC_SLOTS, []):
                cp.wait()
            reads[nxt] = read(nxt)
        writes[j] = write(j)
    for cps in writes.values():
        for cp in cps:
            cp.wait()


def _sc_scatter_rows(src, pos2, n_rows):
    t = pos2.shape[1]
    src3 = src.reshape(t, ROW_CHUNKS, LANES)
    per_worker = t // SC_WORKERS
    n_chunks = per_worker // SC_CHUNK
    idx = pos2.reshape(2, SC_WORKERS, n_chunks, SC_CHUNK).transpose(1, 0, 2, 3)
    idx = idx.reshape(SC_WORKERS, 2 * n_chunks, SC_CHUNK)

    @functools.partial(
        pl.kernel, mesh=_sc_mesh(),
        out_type=jax.ShapeDtypeStruct((n_rows, ROW_CHUNKS, LANES), src.dtype),
        scratch_types=[
            pltpu.VMEM((2 * n_chunks, SC_CHUNK), I32),
            pltpu.VMEM((SC_SLOTS, SC_CHUNK, ROW_CHUNKS, LANES), src.dtype),
            pltpu.SemaphoreType.DMA((SC_SLOTS,)),
            pltpu.SemaphoreType.DMA((SC_SLOTS,)),
        ],
        name="sc_scatter_rows",
    )
    def scatter(src_hbm, idx_hbm, out_hbm, idx_v, rows_v, rsem, wsem):
        wid = lax.axis_index("s") * SC_CORES + lax.axis_index("c")
        pltpu.sync_copy(idx_hbm.at[wid], idx_v)

        def read(j):
            b = j % SC_SLOTS
            return [pltpu.async_copy(src_hbm.at[pl.ds(wid * per_worker + j * SC_CHUNK, SC_CHUNK)], rows_v.at[b],
                                     rsem.at[b])]

        def write(j):
            b = j % SC_SLOTS
            return [pltpu.async_copy(rows_v.at[b], out_hbm.at[idx_v.at[k * n_chunks + j]], wsem.at[b])
                    for k in range(2)]

        _sc_ring(n_chunks, read, write)

    return scatter(src3, idx).reshape(n_rows * ROW_CHUNKS, LANES)


def _sc_gather_rows(table, idx):
    n = idx.shape[0]
    table3 = table.reshape(-1, ROW_CHUNKS, LANES)
    per_worker = n // SC_WORKERS
    n_chunks = per_worker // SC_CHUNK

    @functools.partial(
        pl.kernel, mesh=_sc_mesh(),
        out_type=jax.ShapeDtypeStruct((n, ROW_CHUNKS, LANES), table.dtype),
        scratch_types=[
            pltpu.VMEM((n_chunks, SC_CHUNK), I32),
            pltpu.VMEM((SC_SLOTS, SC_CHUNK, ROW_CHUNKS, LANES), table.dtype),
            pltpu.SemaphoreType.DMA((SC_SLOTS,)),
            pltpu.SemaphoreType.DMA((SC_SLOTS,)),
        ],
        name="sc_gather_rows",
    )
    def gather(table_hbm, idx_hbm, out_hbm, idx_v, rows_v, rsem, wsem):
        wid = lax.axis_index("s") * SC_CORES + lax.axis_index("c")
        pltpu.sync_copy(idx_hbm.at[wid], idx_v)

        def read(j):
            b = j % SC_SLOTS
            return [pltpu.async_copy(table_hbm.at[idx_v.at[j]], rows_v.at[b], rsem.at[b])]

        def write(j):
            b = j % SC_SLOTS
            return [pltpu.async_copy(rows_v.at[b], out_hbm.at[pl.ds(wid * per_worker + j * SC_CHUNK, SC_CHUNK)],
                                     wsem.at[b])]

        _sc_ring(n_chunks, read, write)

    out = gather(table3, idx.reshape(SC_WORKERS, n_chunks, SC_CHUNK))
    return out.reshape(n * ROW_CHUNKS, LANES)


XS_SLOTS = 3


def _expert_body(te_ref, nused_ref, tv_ref, first_ref, next_ref, slot_ref, xs_hbm, wgu_hbm, wd_hbm, ys_ref,
                 wgu_bf, wd_bf, xbuf, xsem, wgu_buf, wd_buf, wsem, *, e0):
    i = pl.program_id(0)
    n_used = nused_ref[0]
    used = i < n_used
    tile_rows = EXPERT_TILE * ROW_CHUNKS

    def weight_fetch(expert, slot):
        return (pltpu.make_async_copy(wgu_hbm.at[e0 + expert], wgu_buf.at[slot], wsem.at[0, slot]),
                pltpu.make_async_copy(wd_hbm.at[e0 + expert], wd_buf.at[slot], wsem.at[1, slot]))

    @pl.when((i == 0) & used)
    def _():
        for cp in weight_fetch(te_ref[0], slot_ref[0]):
            cp.start()

    @pl.when(used & (first_ref[i] == 1))
    def _():
        slot = slot_ref[i]
        for cp in weight_fetch(te_ref[i], slot):
            cp.wait()
        wgu_bf[...] = wgu_buf[slot].astype(BF16)
        wd_bf[...] = wd_buf[slot].astype(BF16)

        @pl.when(next_ref[i] >= 0)
        def _():
            for cp in weight_fetch(next_ref[i], 1 - slot):
                cp.start()


    def fetch(tile):
        slot = lax.rem(tile, XS_SLOTS)
        r0 = pl.multiple_of(tile * tile_rows, tile_rows)
        return pltpu.make_async_copy(xs_hbm.at[pl.ds(r0, tile_rows)], xbuf.at[slot], xsem.at[slot])

    @pl.when((i == 0) & (n_used > 0))
    def _():
        fetch(0).start()

    @pl.when((i == 0) & (n_used > 1))
    def _():
        fetch(1).start()

    @pl.when(i + 2 < n_used)
    def _():
        fetch(i + 2).start()

    @pl.when(used)
    def _():
        fetch(i).wait()
        live = lax.broadcasted_iota(I32, (EXPERT_TILE, 1), 0) < tv_ref[i]
        xs = jnp.where(live, _load_rows(xbuf.at[lax.rem(i, XS_SLOTS)], EXPERT_TILE), 0.0).astype(BF16)
        gu = jnp.dot(xs, wgu_bf[...], preferred_element_type=F32)
        gate = gu[:, :EXPERT_FF]
        up = gu[:, EXPERT_FF:]
        act = gate * jax.nn.sigmoid(gate) * up
        _store_rows(ys_ref, jnp.dot(act.astype(BF16), wd_bf[...], preferred_element_type=F32))

    @pl.when(jnp.logical_not(used))
    def _():
        ys_ref[...] = jnp.zeros(ys_ref.shape, ys_ref.dtype)


def _expert_mlp(tile_plan, xs, w_gate_up, w_down, layer):
    d = D_MODEL
    n_tiles = xs.shape[0] // (EXPERT_TILE * ROW_CHUNKS)
    f2 = 2 * EXPERT_FF
    tile_rows = EXPERT_TILE * ROW_CHUNKS
    grid_spec = pltpu.PrefetchScalarGridSpec(
        num_scalar_prefetch=len(tile_plan),
        grid=(n_tiles,),
        in_specs=[
            pl.BlockSpec(memory_space=pl.ANY),
            pl.BlockSpec(memory_space=pl.ANY),
            pl.BlockSpec(memory_space=pl.ANY),
        ],
        out_specs=pl.BlockSpec((tile_rows, LANES), lambda i, *plan: (i, 0)),
        scratch_shapes=[pltpu.VMEM((d, f2), BF16), pltpu.VMEM((EXPERT_FF, d), BF16),
                        pltpu.VMEM((XS_SLOTS, tile_rows, LANES), U32), pltpu.SemaphoreType.DMA((XS_SLOTS,)),
                        pltpu.VMEM((2, d, f2), F32), pltpu.VMEM((2, EXPERT_FF, d), F32),
                        pltpu.SemaphoreType.DMA((2, 2))],
    )
    return pl.pallas_call(
        functools.partial(_expert_body, e0=layer * N_EXPERTS),
        grid_spec=grid_spec,
        out_shape=jax.ShapeDtypeStruct(xs.shape, U32),
        compiler_params=_cparams(1),
        name="expert_mlp",
    )(*tile_plan, xs, w_gate_up.reshape(DEPTH * N_EXPERTS, d, f2),
      w_down.reshape(DEPTH * N_EXPERTS, EXPERT_FF, d))


def _moe_combine(wts_ref, x_ref, mod_ref, lng_ref, lnb_ref, ya_ref, yb_ref, token0, n):
    d = D_MODEL
    rows = slice(token0, token0 + n)
    cols = []
    for k in range(2):
        wt = jnp.broadcast_to(wts_ref[k:k + 1, rows], (LANES, n)).T
        cols.append(jnp.concatenate([wt] * (d // LANES), axis=1))
    y = cols[0] * _load_rows(ya_ref, n, token0) + cols[1] * _load_rows(yb_ref, n, token0)
    gt_f = mod_ref[:, 5 * d:6 * d]
    return _layer_norm(DEEPNORM_ALPHA * x_ref[rows, :] + (1.0 + gt_f) * y, lng_ref[...], lnb_ref[...])


def _combine_specs(tm, n_steps, steps_per_batch):
    d = D_MODEL
    return [
        pl.BlockSpec((2, tm), lambda i: (0, i)),
        pl.BlockSpec((tm, d), lambda i: (i, 0)),
        pl.BlockSpec((None, 1, N_MOD * d), lambda i: (i // steps_per_batch, 0, 0)),
        pl.BlockSpec((1, d), lambda i: (0, 0)),
        pl.BlockSpec((1, d), lambda i: (0, 0)),
        pl.BlockSpec((tm * ROW_CHUNKS, LANES), lambda i: (i, 0)),
        pl.BlockSpec((tm * ROW_CHUNKS, LANES), lambda i: (n_steps + i, 0)),
    ]


def _gather_rows_of_tokens(ys, pos2):
    return _sc_gather_rows(ys, pos2.reshape(-1))


def _combine_body(wts_ref, x_ref, mod_ref, lng_ref, lnb_ref, ya_ref, yb_ref, o_ref):
    for r in range(x_ref.shape[0] // SUB_ROWS):
        o_ref[r * SUB_ROWS:(r + 1) * SUB_ROWS, :] = _moe_combine(wts_ref, x_ref, mod_ref, lng_ref, lnb_ref, ya_ref,
                                                                 yb_ref, r * SUB_ROWS, SUB_ROWS)


def _combine_ln(moe, mod_rows, ln_g, ln_b, seq):
    wts, x1, ys, pos2 = moe
    yg = _gather_rows_of_tokens(ys, pos2)
    t, d = x1.shape
    tm = 1024
    n_steps = t // tm
    return pl.pallas_call(
        _combine_body,
        grid=(n_steps,),
        in_specs=_combine_specs(tm, n_steps, seq // tm),
        out_specs=pl.BlockSpec((tm, d), lambda i: (i, 0)),
        out_shape=jax.ShapeDtypeStruct((t, d), F32),
        compiler_params=_cparams(1),
        name="moe_combine_ln",
    )(wts, x1, mod_rows, ln_g.reshape(1, d), ln_b.reshape(1, d), yg, yg)


def _gmlp_body(wts_ref, x1_ref, modp_ref, lng_ref, lnb_ref, ya_ref, yb_ref,
               mod_ref, w_ref, b_ref, g_ref, beta_ref, ws_ref, bs_ref, x2_ref, o_ref, ws_bf, w_bf):
    @pl.when(pl.program_id(0) == 0)
    def _():
        tri = lax.broadcasted_iota(I32, (CHUNK, CHUNK), 0) >= lax.broadcasted_iota(I32, (CHUNK, CHUNK), 1)
        for g in range(N_SGU_GROUPS):
            ws_bf[g] = jnp.where(tri, ws_ref[g], 0.0).astype(BF16)
        w_bf[...] = w_ref[...].astype(BF16)

    d = D_MODEL
    sh = mod_ref[:, 0:d]
    sc = mod_ref[:, d:2 * d]
    sub_rows = x1_ref.shape[0]
    for r in range(x1_ref.shape[0] // sub_rows):
        r0 = r * sub_rows
        x2 = _moe_combine(wts_ref, x1_ref, modp_ref, lng_ref, lnb_ref, ya_ref, yb_ref, r0, sub_rows)
        x2_ref[r0:r0 + sub_rows, :] = x2
        h = x2 * (1.0 + sc) + sh
        z = jnp.dot(h.astype(BF16), w_bf[...], preferred_element_type=F32) + b_ref[...]
        z = 0.5 * z * (1.0 + lax.erf(z * (2.0 ** -0.5)))
        u = z[:, :GMLP_WIDTH]
        v = _layer_norm(z[:, GMLP_WIDTH:], g_ref[...], beta_ref[...]).astype(BF16)
        for ci in range(sub_rows // CHUNK):
            rows = slice(ci * CHUNK, (ci + 1) * CHUNK)
            out_rows = slice(r0 + ci * CHUNK, r0 + (ci + 1) * CHUNK)
            for g in range(N_SGU_GROUPS):
                lanes = slice(g * SGU_GROUP_DIM, (g + 1) * SGU_GROUP_DIM)
                mixed = jnp.dot(ws_bf[g], v[rows, lanes], preferred_element_type=F32) + bs_ref[:, g:g + 1]
                o_ref[out_rows, lanes] = (u[rows, lanes] * mixed).astype(BF16)


def _gmlp_gate(moe, ln_g, ln_b, mod_prev, mod_rows, w_in, b_in, sgu_g, sgu_b, w_s, b_s, seq):
    wts, x1, ys, pos2 = moe
    yg = _gather_rows_of_tokens(ys, pos2)
    t, d = x1.shape
    tm = 512
    steps_per_batch = seq // tm
    n_steps = t // tm
    gw = GMLP_WIDTH
    return pl.pallas_call(
        _gmlp_body,
        grid=(n_steps,),
        in_specs=_combine_specs(tm, n_steps, steps_per_batch) + [
            pl.BlockSpec((None, 1, N_MOD * d), lambda i: (i // steps_per_batch, 0, 0)),
            pl.BlockSpec((d, 2 * gw), lambda i: (0, 0)),
            pl.BlockSpec((1, 2 * gw), lambda i: (0, 0)),
            pl.BlockSpec((1, gw), lambda i: (0, 0)),
            pl.BlockSpec((1, gw), lambda i: (0, 0)),
            pl.BlockSpec((N_SGU_GROUPS, CHUNK, CHUNK), lambda i: (0, 0, 0)),
            pl.BlockSpec((CHUNK, N_SGU_GROUPS), lambda i: (0, 0)),
        ],
        out_specs=[pl.BlockSpec((tm, d), lambda i: (i, 0)), pl.BlockSpec((tm, gw), lambda i: (i, 0))],
        out_shape=[jax.ShapeDtypeStruct((t, d), F32), jax.ShapeDtypeStruct((t, gw), BF16)],
        scratch_shapes=[pltpu.VMEM((N_SGU_GROUPS, CHUNK, CHUNK), BF16), pltpu.VMEM((d, 2 * gw), BF16)],
        compiler_params=_cparams(1),
        name="combine_gmlp_gate",
    )(wts, x1, mod_prev, ln_g.reshape(1, d), ln_b.reshape(1, d), yg, yg,
      mod_rows, w_in, b_in.reshape(1, 2 * gw), sgu_g.reshape(1, gw), sgu_b.reshape(1, gw), w_s, b_s.T)


def _router_params(w_group, b_group, w_expert, b_expert):
    def lanes(group_part, expert_part):
        rows = group_part.shape[0]
        gap = jnp.zeros((rows, EXPERT_COL0 - N_EXPERT_GROUPS), F32)
        tail = jnp.zeros((rows, ROUTER_LANES - EXPERT_COL0 - N_EXPERTS), F32)
        return jnp.concatenate([group_part, gap, expert_part, tail], axis=1)

    return lanes(w_group, w_expert), lanes(b_group[None, :], b_expert[None, :])


def _moe_experts(x1, h2, ids, wts, layer, w_gate_up, w_down):
    t = x1.shape[0]
    n_rows = 2 * t + N_EXPERTS * EXPERT_TILE
    n_tiles = n_rows // EXPERT_TILE
    pos, tile_plan = _expert_sort(ids, n_tiles)
    pos2 = pos.reshape(2, t)
    xs = _sc_scatter_rows(h2, pos2, n_rows)
    ys = _expert_mlp(tile_plan, xs, w_gate_up, w_down, layer)
    return wts, x1, ys, pos2


def kernel(x, c, positions, ada_w, ada_b, post_ln_g, post_ln_b, attn_w_qkv, attn_b_qkv, attn_sinks, attn_w_o, attn_b_o, gmlp_w_in, gmlp_b_in, gmlp_sgu_ln_g, gmlp_sgu_ln_b, gmlp_w_s, gmlp_b_s, gmlp_w_out, gmlp_b_out, moe_w_group_router, moe_b_group_router, moe_w_expert_router, moe_b_expert_router, moe_w_gate_up, moe_w_down):
    batch, seq, d = x.shape
    t = batch * seq
    assert d == D_MODEL and batch <= MOD_ROWS and ada_w.shape[0] == DEPTH == 2
    assert seq % 1024 == 0, "token tiles of 1024 rows must not straddle sequences"
    assert t % (SC_WORKERS * SC_CHUNK) == 0, "every SparseCore subcore moves whole chunks"
    x2 = x.reshape(t, d)
    c_pad = jnp.pad(c, ((0, MOD_ROWS - batch), (0, 0)))
    mods = [_adaln_mod(c_pad, ada_w, ada_b, layer) for layer in range(DEPTH)]

    moe = None
    for layer in range(DEPTH):
        j = layer // 2
        if layer % 2 == 0:
            if moe is not None:
                x2 = _combine_ln(moe, mods[layer - 1], post_ln_g[layer - 1, 1], post_ln_b[layer - 1, 1], seq)
            qkv = _qkv_rope(x2, mods[layer], positions, attn_w_qkv[j], attn_b_qkv[j], seq)
            mix = _attention(qkv, attn_sinks[j], batch, seq)
            w_out, b_out = attn_w_o[j], attn_b_o[j]
        else:
            x2, mix = _gmlp_gate(moe, post_ln_g[layer - 1, 1], post_ln_b[layer - 1, 1], mods[layer - 1], mods[layer],
                                 gmlp_w_in[j], gmlp_b_in[j], gmlp_sgu_ln_g[j], gmlp_sgu_ln_b[j],
                                 gmlp_w_s[j], gmlp_b_s[j], seq)
            w_out, b_out = gmlp_w_out[j], gmlp_b_out[j]
        w_router, b_router = _router_params(moe_w_group_router[layer], moe_b_group_router[layer],
                                            moe_w_expert_router[layer], moe_b_expert_router[layer])
        x1, h2, ids, wts = _proj_ln_router(mix, x2, mods[layer], w_out, b_out, post_ln_g[layer, 0],
                                           post_ln_b[layer, 0], w_router, b_router, seq)
        moe = _moe_experts(x1, h2, ids, wts, layer, moe_w_gate_up, moe_w_down)
    x2 = _combine_ln(moe, mods[DEPTH - 1], post_ln_g[DEPTH - 1, 1], post_ln_b[DEPTH - 1, 1], seq)
    return x2.reshape(batch, seq, d)
```

```python
import functools

import jax
import jax.numpy as jnp
from jax import lax
from jax.experimental import pallas as pl
from jax.experimental.pallas import tpu as pltpu
from jax.experimental.pallas import tpu_sc as plsc

F32 = jnp.float32
BF16 = jnp.bfloat16
I32 = jnp.int32

D_MODEL = 1024
DEPTH = 2
HEAD_DIM = 64
N_Q_HEADS = 16
N_KV_HEADS = 4
GQA_GROUP = N_Q_HEADS // N_KV_HEADS
WINDOW = 128
ROPE_THETA = 10000.0
Q_WIDTH = N_Q_HEADS * HEAD_DIM
KV_WIDTH = N_KV_HEADS * HEAD_DIM
QKV_WIDTH = Q_WIDTH + 2 * KV_WIDTH
CHUNK = 128
GMLP_WIDTH = 2 * D_MODEL
N_SGU_GROUPS = 8
SGU_GROUP_DIM = GMLP_WIDTH // N_SGU_GROUPS
N_EXPERT_GROUPS = 4
EXPERTS_PER_GROUP = 8
N_EXPERTS = N_EXPERT_GROUPS * EXPERTS_PER_GROUP
EXPERT_FF = D_MODEL // 4
N_MOD = 6
DEEPNORM_ALPHA = (2.0 * DEPTH) ** 0.25
LN_EPS = 1e-5

LANES = 128
MOD_ROWS = 8
ROUTER_LANES = 128
EXPERT_COL0 = 8
SORT_CHUNK = 256
EXPERT_TILE = 512
SUB_ROWS = 256
VMEM_LIMIT = 56 * 1024 * 1024
SC_CORES = 2
SC_WORKERS = 32
SC_CHUNK = 64
SC_SLOTS = 3


def _cparams(n_axes, vmem=VMEM_LIMIT):
    return pltpu.CompilerParams(dimension_semantics=("arbitrary",) * n_axes, vmem_limit_bytes=vmem)


U32 = jnp.uint32
ROW_CHUNKS = D_MODEL // 2 // LANES


def _store_rows(ref, val, token0=0):
    n = val.shape[0]
    half = D_MODEL // 2
    words = pltpu.pack_elementwise([val[:, :half], val[:, half:]], packed_dtype=BF16)
    for c in range(ROW_CHUNKS):
        ref[pl.ds(token0 * ROW_CHUNKS + c, n, stride=ROW_CHUNKS), :] = words[:, c * LANES:(c + 1) * LANES]


def _load_rows(ref, n, token0=0):
    words = jnp.concatenate([ref[pl.ds(token0 * ROW_CHUNKS + c, n, stride=ROW_CHUNKS), :]
                             for c in range(ROW_CHUNKS)], axis=1)
    lo = lax.bitcast_convert_type(words << 16, F32)
    hi = lax.bitcast_convert_type(words & jnp.uint32(0xFFFF0000), F32)
    return jnp.concatenate([lo, hi], axis=1)


def _layer_norm(r, g, b):
    mu = jnp.mean(r, axis=-1, keepdims=True)
    d = r - mu
    var = jnp.mean(d * d, axis=-1, keepdims=True)
    return d * lax.rsqrt(var + LN_EPS) * g + b


def _mod_body(c_ref, w_ref, b_ref, o_ref):
    c = c_ref[...]
    ca = c * jax.nn.sigmoid(c)
    o_ref[...] = jnp.dot(ca.astype(BF16), w_ref[...].astype(BF16), preferred_element_type=F32) + b_ref[...]


def _adaln_mod(c_pad, ada_w, ada_b, layer):
    tn = 1536
    n_out = N_MOD * D_MODEL
    mod = pl.pallas_call(
        _mod_body,
        grid=(n_out // tn,),
        in_specs=[
            pl.BlockSpec((MOD_ROWS, D_MODEL), lambda j: (0, 0)),
            pl.BlockSpec((None, D_MODEL, tn), lambda j: (layer, 0, j)),
            pl.BlockSpec((None, 1, tn), lambda j: (layer, 0, j)),
        ],
        out_specs=pl.BlockSpec((MOD_ROWS, tn), lambda j: (0, j)),
        out_shape=jax.ShapeDtypeStruct((MOD_ROWS, n_out), F32),
        compiler_params=_cparams(1),
        name="adaln_mod",
    )(c_pad, ada_w, ada_b.reshape(DEPTH, 1, n_out))
    return mod.reshape(MOD_ROWS, 1, n_out)


def _qkv_body(x_ref, mod_ref, pos_ref, w_ref, b_ref, invf_ref, o_ref, wbf_ref):
    @pl.when(pl.program_id(0) == 0)
    def _():
        wbf_ref[...] = w_ref[...].astype(BF16)

    tm = x_ref.shape[0]
    sh = mod_ref[:, 0:D_MODEL]
    sc = mod_ref[:, D_MODEL:2 * D_MODEL]
    h = x_ref[...] * (1.0 + sc) + sh
    qkv = jnp.dot(h.astype(BF16), wbf_ref[...], preferred_element_type=F32) + b_ref[...]

    ang = invf_ref[...] * pos_ref[...].astype(F32)
    c = jnp.cos(ang)
    s = jnp.sin(ang)
    ct = jnp.concatenate([c, c, c, c], axis=0).T
    st = jnp.concatenate([-s, s, -s, s], axis=0).T
    lane = lax.broadcasted_iota(I32, (tm, LANES), 1)
    first_half = (lane & (HEAD_DIM // 2)) == 0
    n_rope = (Q_WIDTH + KV_WIDTH) // LANES
    for j in range(n_rope):
        blk = qkv[:, j * LANES:(j + 1) * LANES]
        rot = jnp.where(first_half, pltpu.roll(blk, LANES - HEAD_DIM // 2, 1), pltpu.roll(blk, HEAD_DIM // 2, 1))
        r = blk * ct + rot * st
        if j < Q_WIDTH // LANES:
            r = r * (HEAD_DIM ** -0.5)
        o_ref[:, j * LANES:(j + 1) * LANES] = r.astype(BF16)
    o_ref[:, Q_WIDTH + KV_WIDTH:] = qkv[:, Q_WIDTH + KV_WIDTH:].astype(BF16)


def _qkv_rope(x2, mod_rows, positions, w_qkv, b_qkv, seq):
    t = x2.shape[0]
    tm = 1024
    steps_per_batch = seq // tm
    inv_freq = ROPE_THETA ** (-jnp.arange(0, HEAD_DIM, 2, dtype=F32) / HEAD_DIM)
    return pl.pallas_call(
        _qkv_body,
        grid=(t // tm,),
        in_specs=[
            pl.BlockSpec((tm, D_MODEL), lambda i: (i, 0)),
            pl.BlockSpec((None, 1, N_MOD * D_MODEL), lambda i: (i // steps_per_batch, 0, 0)),
            pl.BlockSpec((None, 1, tm), lambda i: (i, 0, 0)),
            pl.BlockSpec((D_MODEL, QKV_WIDTH), lambda i: (0, 0)),
            pl.BlockSpec((1, QKV_WIDTH), lambda i: (0, 0)),
            pl.BlockSpec((HEAD_DIM // 2, 1), lambda i: (0, 0)),
        ],
        out_specs=pl.BlockSpec((tm, QKV_WIDTH), lambda i: (i, 0)),
        out_shape=jax.ShapeDtypeStruct((t, QKV_WIDTH), BF16),
        scratch_shapes=[pltpu.VMEM((D_MODEL, QKV_WIDTH), BF16)],
        compiler_params=_cparams(1),
        name="qkv_rope",
    )(x2, mod_rows, positions.reshape(t // tm, 1, tm), w_qkv, b_qkv.reshape(1, QKV_WIDTH),
      inv_freq.reshape(HEAD_DIM // 2, 1))


BF16_ROWS = 16


def _attn_prepare(kv, kab_ref, vx_ref, slot):
    kv = kv.astype(F32)
    low = lax.broadcasted_iota(I32, (WINDOW, LANES), 1) < HEAD_DIM
    ones = jnp.ones((WINDOW, LANES), F32)
    for g in range(N_KV_HEADS):
        for part, ref in ((0, None), (KV_WIDTH, vx_ref)):
            tile = kv[:, part + (g // 2) * LANES:part + (g // 2 + 1) * LANES]
            other = pltpu.roll(tile, HEAD_DIM, 1)
            in_low, in_high = (tile, other) if g % 2 == 0 else (other, tile)
            if ref is None:
                kab_ref[slot, 2 * g] = jnp.where(low, in_low, 0.0).astype(BF16)
                kab_ref[slot, 2 * g + 1] = jnp.where(low, 0.0, in_high).astype(BF16)
            else:
                both = jnp.where(low, in_low, in_high)
                vx_ref[slot, g] = jnp.concatenate([both, ones], axis=1).astype(BF16)


def _attn_block(sink_ref, q, kab_ref, vx_ref, s_ref, p_ref, prev, cur, first_block):
    for g in range(N_KV_HEADS):
        q_pair = jnp.concatenate([q[:, (2 * g) * LANES:(2 * g + 1) * LANES],
                                  q[:, (2 * g + 1) * LANES:(2 * g + 2) * LANES]], axis=0)
        for a in range(2):
            kband = jnp.concatenate([kab_ref[prev, 2 * g + a], kab_ref[cur, 2 * g + a]], axis=0)
            s = lax.dot_general(q_pair, kband, (((1,), (1,)), ((), ())), preferred_element_type=F32)
            s_ref[GQA_GROUP * g + a] = s[:WINDOW]
            s_ref[GQA_GROUP * g + 2 + a] = s[WINDOW:]
    qi = lax.broadcasted_iota(I32, (WINDOW, 2 * WINDOW), 0) + WINDOW
    kj = lax.broadcasted_iota(I32, (WINDOW, 2 * WINDOW), 1)
    mask = (kj <= qi) & (kj > qi - WINDOW) & ((kj >= WINDOW) | jnp.logical_not(first_block))
    key0 = lax.broadcasted_iota(I32, (1, 2 * WINDOW), 1) == 0
    for h in range(N_Q_HEADS):
        s = jnp.where(mask, s_ref[h], jnp.where(key0, sink_ref[h], -jnp.inf))
        m = jnp.max(s, axis=-1, keepdims=True)
        p_ref[h] = jnp.exp(s - m).astype(BF16)
    low = lax.broadcasted_iota(I32, (WINDOW, LANES), 1) < HEAD_DIM
    sink_row = ((lax.broadcasted_iota(I32, (BF16_ROWS, 2 * LANES), 0) == 0)
                & (lax.broadcasted_iota(I32, (BF16_ROWS, 2 * LANES), 1) < LANES))
    out_tiles = []
    for g in range(N_KV_HEADS):
        v_prev = vx_ref[prev, g]
        v_head = jnp.where(sink_row, 0.0, v_prev[:BF16_ROWS].astype(F32)).astype(BF16)
        vband = jnp.concatenate([v_head, v_prev[BF16_ROWS:], vx_ref[cur, g]], axis=0)
        p4 = p_ref[GQA_GROUP * g:GQA_GROUP * (g + 1)].reshape(GQA_GROUP * WINDOW, 2 * WINDOW)
        o4 = jnp.dot(p4, vband, preferred_element_type=F32)
        heads = []
        for j in range(GQA_GROUP):
            blk = o4[j * WINDOW:(j + 1) * WINDOW]
            heads.append(blk[:, :LANES] / blk[:, LANES:])
        out_tiles.append(jnp.where(low, heads[0], heads[1]))
        out_tiles.append(jnp.where(low, heads[2], heads[3]))
    return jnp.concatenate(out_tiles, axis=1).astype(BF16)


def _attn_body(sink_ref, q_ref, kv_ref, o_ref, kab_ref, vx_ref, s_ref, p_ref):
    n = pl.program_id(1)

    @pl.when(n == 0)
    def _():
        kab_ref[1] = jnp.zeros(kab_ref.shape[1:], kab_ref.dtype)
        half = (N_KV_HEADS, WINDOW, LANES)
        vx_ref[1] = jnp.concatenate([jnp.zeros(half, BF16), jnp.ones(half, BF16)], axis=-1)

    scratch = (kab_ref, vx_ref, s_ref, p_ref)
    for blk in range(q_ref.shape[0] // WINDOW):
        rows = slice(blk * WINDOW, (blk + 1) * WINDOW)
        cur = blk % 2
        _attn_prepare(kv_ref[rows, :], kab_ref, vx_ref, cur)
        o_ref[rows, :] = _attn_block(sink_ref, q_ref[rows, :], *scratch, 1 - cur, cur,
                                     (n == 0) if blk == 0 else False)


def _attention(qkv, sinks, batch, seq):
    t = qkv.shape[0]
    tq = 8 * WINDOW
    steps = seq // tq
    kv_col = Q_WIDTH // (2 * KV_WIDTH)
    return pl.pallas_call(
        _attn_body,
        grid=(batch, steps),
        in_specs=[
            pl.BlockSpec(memory_space=pltpu.SMEM),
            pl.BlockSpec((tq, Q_WIDTH), lambda b, n: (b * steps + n, 0)),
            pl.BlockSpec((tq, 2 * KV_WIDTH), lambda b, n: (b * steps + n, kv_col)),
        ],
        out_specs=pl.BlockSpec((tq, Q_WIDTH), lambda b, n: (b * steps + n, 0)),
        out_shape=jax.ShapeDtypeStruct((t, Q_WIDTH), BF16),
        scratch_shapes=[
            pltpu.VMEM((2, 2 * N_KV_HEADS, WINDOW, LANES), BF16),
            pltpu.VMEM((2, N_KV_HEADS, WINDOW, 2 * LANES), BF16),
            pltpu.VMEM((N_Q_HEADS, WINDOW, 2 * WINDOW), F32),
            pltpu.VMEM((N_Q_HEADS, WINDOW, 2 * WINDOW), BF16),
        ],
        compiler_params=_cparams(2),
        name="swa_attention",
    )(sinks, qkv, qkv)


def _route(lt):
    tm = lt.shape[1]
    row = lax.broadcasted_iota(I32, (EXPERTS_PER_GROUP, tm), 0)
    neg = -jnp.inf
    gl = jnp.where(row < N_EXPERT_GROUPS, lt[0:EXPERTS_PER_GROUP], neg)
    gm = jnp.max(gl, axis=0, keepdims=True)
    g_p = 1.0 / jnp.sum(jnp.exp(gl - gm), axis=0, keepdims=True)
    g_idx = jnp.min(jnp.where(gl == gm, row, EXPERTS_PER_GROUP), axis=0, keepdims=True)
    sel = lt[EXPERT_COL0 + (N_EXPERT_GROUPS - 1) * EXPERTS_PER_GROUP:EXPERT_COL0 + N_EXPERTS]
    for g in range(N_EXPERT_GROUPS - 2, -1, -1):
        lo = EXPERT_COL0 + g * EXPERTS_PER_GROUP
        sel = jnp.where(g_idx == g, lt[lo:lo + EXPERTS_PER_GROUP], sel)
    v1 = jnp.max(sel, axis=0, keepdims=True)
    i1 = jnp.min(jnp.where(sel == v1, row, EXPERTS_PER_GROUP), axis=0, keepdims=True)
    sel2 = jnp.where(row == i1, neg, sel)
    v2 = jnp.max(sel2, axis=0, keepdims=True)
    i2 = jnp.min(jnp.where(sel2 == v2, row, EXPERTS_PER_GROUP), axis=0, keepdims=True)
    e2 = jnp.exp(v2 - v1)
    w1 = g_p / (1.0 + e2)
    w2 = g_p * e2 / (1.0 + e2)
    base = g_idx * EXPERTS_PER_GROUP
    return base + i1, base + i2, w1, w2


def _proj_body(o_ref, x_ref, mod_ref, w_ref, b_ref, lng_ref, lnb_ref, wr_ref, br_ref,
               x1_ref, h2_ref, ids_ref, wts_ref, wbf_ref):
    @pl.when(pl.program_id(0) == 0)
    def _():
        wbf_ref[...] = w_ref[...].astype(BF16)

    d = D_MODEL
    gt_m = mod_ref[:, 2 * d:3 * d]
    sh_f = mod_ref[:, 3 * d:4 * d]
    sc_f = mod_ref[:, 4 * d:5 * d]
    wr = wr_ref[...].astype(BF16)
    for r in range(x_ref.shape[0] // SUB_ROWS):
        rows = slice(r * SUB_ROWS, (r + 1) * SUB_ROWS)
        y = jnp.dot(o_ref[rows, :], wbf_ref[...], preferred_element_type=F32) + b_ref[...]
        x1 = _layer_norm(DEEPNORM_ALPHA * x_ref[rows, :] + (1.0 + gt_m) * y, lng_ref[...], lnb_ref[...])
        x1_ref[rows, :] = x1
        h2 = x1 * (1.0 + sc_f) + sh_f
        _store_rows(h2_ref, h2, r * SUB_ROWS)
        logits = jnp.dot(h2.astype(BF16), wr, preferred_element_type=F32) + br_ref[...]
        ea, eb, wa, wb = _route(logits.T)
        ids_ref[0:1, rows] = ea
        ids_ref[1:2, rows] = eb
        wts_ref[0:1, rows] = wa
        wts_ref[1:2, rows] = wb


def _proj_ln_router(o, x2, mod_rows, w, b, ln_g, ln_b, w_router, b_router, seq):
    t, k = o.shape
    tm = 1024
    steps_per_batch = seq // tm
    d = D_MODEL
    return pl.pallas_call(
        _proj_body,
        grid=(t // tm,),
        in_specs=[
            pl.BlockSpec((tm, k), lambda i: (i, 0)),
            pl.BlockSpec((tm, d), lambda i: (i, 0)),
            pl.BlockSpec((None, 1, N_MOD * d), lambda i: (i // steps_per_batch, 0, 0)),
            pl.BlockSpec((k, d), lambda i: (0, 0), pipeline_mode=pl.Buffered(1)),
            pl.BlockSpec((1, d), lambda i: (0, 0)),
            pl.BlockSpec((1, d), lambda i: (0, 0)),
            pl.BlockSpec((1, d), lambda i: (0, 0)),
            pl.BlockSpec((d, ROUTER_LANES), lambda i: (0, 0)),
            pl.BlockSpec((1, ROUTER_LANES), lambda i: (0, 0)),
        ],
        out_specs=[
            pl.BlockSpec((tm, d), lambda i: (i, 0)),
            pl.BlockSpec((tm * ROW_CHUNKS, LANES), lambda i: (i, 0)),
            pl.BlockSpec((2, tm), lambda i: (0, i)),
            pl.BlockSpec((2, tm), lambda i: (0, i)),
        ],
        out_shape=[
            jax.ShapeDtypeStruct((t, d), F32),
            jax.ShapeDtypeStruct((t * ROW_CHUNKS, LANES), U32),
            jax.ShapeDtypeStruct((2, t), I32),
            jax.ShapeDtypeStruct((2, t), F32),
        ],
        scratch_shapes=[pltpu.VMEM((k, d), BF16)],
        compiler_params=_cparams(1),
        name="proj_ln_router",
    )(o, x2, mod_rows, w, b.reshape(1, d), ln_g.reshape(1, d), ln_b.reshape(1, d), w_router, b_router)


def _sort_body(ids_ref, pos_ref, te_ref, nused_ref, tv_ref, wplan_ref, rank_ref):
    n_rows = ids_ref.shape[0]
    c = SORT_CHUNK
    erow = lax.broadcasted_iota(I32, (N_EXPERTS, c), 0)
    tri = (lax.broadcasted_iota(I32, (c, c), 0) <= lax.broadcasted_iota(I32, (c, c), 1)).astype(BF16)

    def rank_step(r, carry):
        onehot = erow == ids_ref[pl.ds(r, 1), :]
        pref = jnp.dot(onehot.astype(BF16), tri, preferred_element_type=F32)
        rank = jnp.sum(jnp.where(onehot, pref + carry, 0.0), axis=0, keepdims=True) - 1.0
        rank_ref[pl.ds(r, 1), :] = rank
        return carry + pref[:, c - 1:c]

    counts = lax.fori_loop(0, n_rows, rank_step, jnp.zeros((N_EXPERTS, 1), F32), unroll=8)
    n_tile = jnp.floor((counts + (EXPERT_TILE - 1)) * (1.0 / EXPERT_TILE))
    low = (lax.broadcasted_iota(I32, (N_EXPERTS, N_EXPERTS), 1)
           <= lax.broadcasted_iota(I32, (N_EXPERTS, N_EXPERTS), 0)).astype(BF16)
    cum = jnp.dot(low, jnp.broadcast_to(n_tile, (N_EXPERTS, LANES)).astype(BF16),
                  preferred_element_type=F32)[:, 0:1]
    row_off = (cum - n_tile) * EXPERT_TILE

    def pos_step(r, _):
        onehot = erow == ids_ref[pl.ds(r, 1), :]
        off = jnp.sum(jnp.where(onehot, row_off, 0.0), axis=0, keepdims=True)
        pos_ref[pl.ds(r, 1), :] = (off + rank_ref[pl.ds(r, 1), :]).astype(I32)
        return 0

    lax.fori_loop(0, n_rows, pos_step, 0, unroll=8)
    total = jnp.max(cum, axis=0, keepdims=True)
    n_lanes = te_ref.shape[1]
    tile = jnp.minimum(lax.broadcasted_iota(I32, (N_EXPERTS, n_lanes), 1).astype(F32), total - 1.0)
    te_ref[...] = jnp.sum(jnp.where(cum <= tile, 1.0, 0.0), axis=0, keepdims=True).astype(I32)
    nused_ref[...] = jnp.broadcast_to(total, nused_ref.shape).astype(I32)
    tile_f = lax.broadcasted_iota(I32, (N_EXPERTS, n_lanes), 1).astype(F32)
    first = cum - n_tile
    rows_left = jnp.clip(counts - (tile_f - first) * EXPERT_TILE, 0.0, float(EXPERT_TILE))
    owns = (first <= tile_f) & (tile_f < cum)
    tv_ref[...] = jnp.sum(jnp.where(owns, rows_left, 0.0), axis=0, keepdims=True).astype(I32)
    def lane_sum(x):
        return jnp.sum(x, axis=0, keepdims=True)

    is_first = lane_sum(jnp.where(owns & (tile_f == first), 1.0, 0.0))
    group_end = lane_sum(jnp.where(owns, cum, 0.0))
    next_expert = jnp.where(group_end < total, lane_sum(jnp.where(cum <= group_end, 1.0, 0.0)), -1.0)
    ordinal = lane_sum(jnp.where((cum <= tile_f[0:1]) & (n_tile > 0.0), 1.0, 0.0))
    slot = ordinal - 2.0 * jnp.floor(ordinal * 0.5)
    wplan_ref[0:1, :] = is_first.astype(I32)
    wplan_ref[1:2, :] = next_expert.astype(I32)
    wplan_ref[2:3, :] = slot.astype(I32)
    wplan_ref[3:8, :] = jnp.zeros((5, n_lanes), I32)


def _expert_sort(ids, n_tiles):
    n_assign = ids.shape[0] * ids.shape[1]
    n_rows = n_assign // SORT_CHUNK
    te_lanes = -(-n_tiles // LANES) * LANES
    pos, te, nused, tv, wplan = pl.pallas_call(
        _sort_body,
        grid=(1,),
        in_specs=[pl.BlockSpec((n_rows, SORT_CHUNK), lambda i: (0, 0))],
        out_specs=[
            pl.BlockSpec((n_rows, SORT_CHUNK), lambda i: (0, 0)),
            pl.BlockSpec((1, te_lanes), lambda i: (0, 0)),
            pl.BlockSpec((1, LANES), lambda i: (0, 0)),
            pl.BlockSpec((1, te_lanes), lambda i: (0, 0)),
            pl.BlockSpec((8, te_lanes), lambda i: (0, 0)),
        ],
        out_shape=[
            jax.ShapeDtypeStruct((n_rows, SORT_CHUNK), I32),
            jax.ShapeDtypeStruct((1, te_lanes), I32),
            jax.ShapeDtypeStruct((1, LANES), I32),
            jax.ShapeDtypeStruct((1, te_lanes), I32),
            jax.ShapeDtypeStruct((8, te_lanes), I32),
        ],
        scratch_shapes=[pltpu.VMEM((n_rows, SORT_CHUNK), F32)],
        compiler_params=_cparams(1),
        name="expert_sort",
    )(ids.reshape(n_rows, SORT_CHUNK))
    tile_plan = (te[0, :n_tiles], nused[0, :1], tv[0, :n_tiles],
                 wplan[0, :n_tiles], wplan[1, :n_tiles], wplan[2, :n_tiles])
    return pos, tile_plan


def _sc_mesh():
    return plsc.VectorSubcoreMesh(core_axis_name="c", subcore_axis_name="s", num_cores=SC_CORES,
                                  num_subcores=SC_WORKERS // SC_CORES)


def _sc_ring(n_chunks, read, write):
    reads, writes = {}, {}
    for j in range(min(SC_SLOTS - 1, n_chunks)):
        reads[j] = read(j)
    for j in range(n_chunks):
        for cp in reads.pop(j):
            cp.wait()
        nxt = j + SC_SLOTS - 1
        if nxt < n_chunks:
            for cp in writes.pop(nxt - SC_SLOTS, []):
                cp.wait()
            reads[nxt] = read(nxt)
        writes[j] = write(j)
    for cps in writes.values():
        for cp in cps:
            cp.wait()


def _sc_scatter_rows(src, pos2, n_rows):
    t = pos2.shape[1]
    src3 = src.reshape(t, ROW_CHUNKS, LANES)
    per_worker = t // SC_WORKERS
    n_chunks = per_worker // SC_CHUNK
    idx = pos2.reshape(2, SC_WORKERS, n_chunks, SC_CHUNK).transpose(1, 0, 2, 3)
    idx = idx.reshape(SC_WORKERS, 2 * n_chunks, SC_CHUNK)

    @functools.partial(
        pl.kernel, mesh=_sc_mesh(),
        out_type=jax.ShapeDtypeStruct((n_rows, ROW_CHUNKS, LANES), src.dtype),
        scratch_types=[
            pltpu.VMEM((2 * n_chunks, SC_CHUNK), I32),
            pltpu.VMEM((SC_SLOTS, SC_CHUNK, ROW_CHUNKS, LANES), src.dtype),
            pltpu.SemaphoreType.DMA((SC_SLOTS,)),
            pltpu.SemaphoreType.DMA((SC_SLOTS,)),
        ],
        name="sc_scatter_rows",
    )
    def scatter(src_hbm, idx_hbm, out_hbm, idx_v, rows_v, rsem, wsem):
        wid = lax.axis_index("s") * SC_CORES + lax.axis_index("c")
        pltpu.sync_copy(idx_hbm.at[wid], idx_v)

        def read(j):
            b = j % SC_SLOTS
            return [pltpu.async_copy(src_hbm.at[pl.ds(wid * per_worker + j * SC_CHUNK, SC_CHUNK)], rows_v.at[b],
                                     rsem.at[b])]

        def write(j):
            b = j % SC_SLOTS
            return [pltpu.async_copy(rows_v.at[b], out_hbm.at[idx_v.at[k * n_chunks + j]], wsem.at[b])
                    for k in range(2)]

        _sc_ring(n_chunks, read, write)

    return scatter(src3, idx).reshape(n_rows * ROW_CHUNKS, LANES)


def _sc_gather_rows(table, idx):
    n = idx.shape[0]
    table3 = table.reshape(-1, ROW_CHUNKS, LANES)
    per_worker = n // SC_WORKERS
    n_chunks = per_worker // SC_CHUNK

    @functools.partial(
        pl.kernel, mesh=_sc_mesh(),
        out_type=jax.ShapeDtypeStruct((n, ROW_CHUNKS, LANES), table.dtype),
        scratch_types=[
            pltpu.VMEM((n_chunks, SC_CHUNK), I32),
            pltpu.VMEM((SC_SLOTS, SC_CHUNK, ROW_CHUNKS, LANES), table.dtype),
            pltpu.SemaphoreType.DMA((SC_SLOTS,)),
            pltpu.SemaphoreType.DMA((SC_SLOTS,)),
        ],
        name="sc_gather_rows",
    )
    def gather(table_hbm, idx_hbm, out_hbm, idx_v, rows_v, rsem, wsem):
        wid = lax.axis_index("s") * SC_CORES + lax.axis_index("c")
        pltpu.sync_copy(idx_hbm.at[wid], idx_v)

        def read(j):
            b = j % SC_SLOTS
            return [pltpu.async_copy(table_hbm.at[idx_v.at[j]], rows_v.at[b], rsem.at[b])]

        def write(j):
            b = j % SC_SLOTS
            return [pltpu.async_copy(rows_v.at[b], out_hbm.at[pl.ds(wid * per_worker + j * SC_CHUNK, SC_CHUNK)],
                                     wsem.at[b])]

        _sc_ring(n_chunks, read, write)

    out = gather(table3, idx.reshape(SC_WORKERS, n_chunks, SC_CHUNK))
    return out.reshape(n * ROW_CHUNKS, LANES)


XS_SLOTS = 4


def _expert_body(te_ref, nused_ref, tv_ref, first_ref, next_ref, slot_ref, xs_hbm, wgu_hbm, wd_hbm, ys_ref,
                 wgu_bf, wd_bf, xbuf, xsem, wgu_buf, wd_buf, wsem, *, e0):
    i = pl.program_id(0)
    n_used = nused_ref[0]
    used = i < n_used
    tile_rows = EXPERT_TILE * ROW_CHUNKS

    def weight_fetch(expert, slot):
        return (pltpu.make_async_copy(wgu_hbm.at[e0 + expert], wgu_buf.at[slot], wsem.at[0, slot]),
                pltpu.make_async_copy(wd_hbm.at[e0 + expert], wd_buf.at[slot], wsem.at[1, slot]))

    @pl.when((i == 0) & used)
    def _():
        for cp in weight_fetch(te_ref[0], slot_ref[0]):
            cp.start()

    @pl.when(used & (first_ref[i] == 1))
    def _():
        slot = slot_ref[i]
        for cp in weight_fetch(te_ref[i], slot):
            cp.wait()
        wgu_bf[...] = wgu_buf[slot].astype(BF16)
        wd_bf[...] = wd_buf[slot].astype(BF16)

        @pl.when(next_ref[i] >= 0)
        def _():
            for cp in weight_fetch(next_ref[i], 1 - slot):
                cp.start()


    def fetch(tile):
        slot = lax.rem(tile, XS_SLOTS)
        r0 = pl.multiple_of(tile * tile_rows, tile_rows)
        return pltpu.make_async_copy(xs_hbm.at[pl.ds(r0, tile_rows)], xbuf.at[slot], xsem.at[slot])

    ahead = XS_SLOTS - 1
    for tile in range(ahead):
        @pl.when((i == 0) & (tile < n_used))
        def _(tile=tile):
            fetch(tile).start()

    @pl.when(i + ahead < n_used)
    def _():
        fetch(i + ahead).start()

    @pl.when(used)
    def _():
        fetch(i).wait()
        live = lax.broadcasted_iota(I32, (EXPERT_TILE, 1), 0) < tv_ref[i]
        xs = jnp.where(live, _load_rows(xbuf.at[lax.rem(i, XS_SLOTS)], EXPERT_TILE), 0.0).astype(BF16)
        gu = jnp.dot(xs, wgu_bf[...], preferred_element_type=F32)
        gate = gu[:, :EXPERT_FF]
        up = gu[:, EXPERT_FF:]
        act = gate * jax.nn.sigmoid(gate) * up
        _store_rows(ys_ref, jnp.dot(act.astype(BF16), wd_bf[...], preferred_element_type=F32))

    @pl.when(jnp.logical_not(used))
    def _():
        ys_ref[...] = jnp.zeros(ys_ref.shape, ys_ref.dtype)


def _expert_mlp(tile_plan, xs, w_gate_up, w_down, layer):
    d = D_MODEL
    n_tiles = xs.shape[0] // (EXPERT_TILE * ROW_CHUNKS)
    f2 = 2 * EXPERT_FF
    tile_rows = EXPERT_TILE * ROW_CHUNKS
    grid_spec = pltpu.PrefetchScalarGridSpec(
        num_scalar_prefetch=len(tile_plan),
        grid=(n_tiles,),
        in_specs=[
            pl.BlockSpec(memory_space=pl.ANY),
            pl.BlockSpec(memory_space=pl.ANY),
            pl.BlockSpec(memory_space=pl.ANY),
        ],
        out_specs=pl.BlockSpec((tile_rows, LANES), lambda i, *plan: (i, 0)),
        scratch_shapes=[pltpu.VMEM((d, f2), BF16), pltpu.VMEM((EXPERT_FF, d), BF16),
                        pltpu.VMEM((XS_SLOTS, tile_rows, LANES), U32), pltpu.SemaphoreType.DMA((XS_SLOTS,)),
                        pltpu.VMEM((2, d, f2), F32), pltpu.VMEM((2, EXPERT_FF, d), F32),
                        pltpu.SemaphoreType.DMA((2, 2))],
    )
    return pl.pallas_call(
        functools.partial(_expert_body, e0=layer * N_EXPERTS),
        grid_spec=grid_spec,
        out_shape=jax.ShapeDtypeStruct(xs.shape, U32),
        compiler_params=_cparams(1),
        name="expert_mlp",
    )(*tile_plan, xs, w_gate_up.reshape(DEPTH * N_EXPERTS, d, f2),
      w_down.reshape(DEPTH * N_EXPERTS, EXPERT_FF, d))


def _moe_combine(wts_ref, x_ref, mod_ref, lng_ref, lnb_ref, ya_ref, yb_ref, token0, n):
    d = D_MODEL
    rows = slice(token0, token0 + n)
    cols = []
    for k in range(2):
        wt = jnp.broadcast_to(wts_ref[k:k + 1, rows], (LANES, n)).T
        cols.append(jnp.concatenate([wt] * (d // LANES), axis=1))
    y = cols[0] * _load_rows(ya_ref, n, token0) + cols[1] * _load_rows(yb_ref, n, token0)
    gt_f = mod_ref[:, 5 * d:6 * d]
    return _layer_norm(DEEPNORM_ALPHA * x_ref[rows, :] + (1.0 + gt_f) * y, lng_ref[...], lnb_ref[...])


def _combine_specs(tm, n_steps, steps_per_batch):
    d = D_MODEL
    return [
        pl.BlockSpec((2, tm), lambda i: (0, i)),
        pl.BlockSpec((tm, d), lambda i: (i, 0)),
        pl.BlockSpec((None, 1, N_MOD * d), lambda i: (i // steps_per_batch, 0, 0)),
        pl.BlockSpec((1, d), lambda i: (0, 0)),
        pl.BlockSpec((1, d), lambda i: (0, 0)),
        pl.BlockSpec((tm * ROW_CHUNKS, LANES), lambda i: (i, 0)),
        pl.BlockSpec((tm * ROW_CHUNKS, LANES), lambda i: (n_steps + i, 0)),
    ]


def _gather_rows_of_tokens(ys, pos2):
    return _sc_gather_rows(ys, pos2.reshape(-1))


def _combine_body(wts_ref, x_ref, mod_ref, lng_ref, lnb_ref, ya_ref, yb_ref, o_ref):
    for r in range(x_ref.shape[0] // SUB_ROWS):
        o_ref[r * SUB_ROWS:(r + 1) * SUB_ROWS, :] = _moe_combine(wts_ref, x_ref, mod_ref, lng_ref, lnb_ref, ya_ref,
                                                                 yb_ref, r * SUB_ROWS, SUB_ROWS)


def _combine_ln(moe, mod_rows, ln_g, ln_b, seq):
    wts, x1, ys, pos2 = moe
    yg = _gather_rows_of_tokens(ys, pos2)
    t, d = x1.shape
    tm = 1024
    n_steps = t // tm
    return pl.pallas_call(
        _combine_body,
        grid=(n_steps,),
        in_specs=_combine_specs(tm, n_steps, seq // tm),
        out_specs=pl.BlockSpec((tm, d), lambda i: (i, 0)),
        out_shape=jax.ShapeDtypeStruct((t, d), F32),
        compiler_params=_cparams(1),
        name="moe_combine_ln",
    )(wts, x1, mod_rows, ln_g.reshape(1, d), ln_b.reshape(1, d), yg, yg)


def _gmlp_body(wts_ref, x1_ref, modp_ref, lng_ref, lnb_ref, ya_ref, yb_ref,
               mod_ref, w_ref, b_ref, g_ref, beta_ref, ws_ref, bs_ref, x2_ref, o_ref, ws_bf, w_bf):
    @pl.when(pl.program_id(0) == 0)
    def _():
        tri = lax.broadcasted_iota(I32, (CHUNK, CHUNK), 0) >= lax.broadcasted_iota(I32, (CHUNK, CHUNK), 1)
        for g in range(N_SGU_GROUPS):
            ws_bf[g] = jnp.where(tri, ws_ref[g], 0.0).astype(BF16)
        w_bf[...] = w_ref[...].astype(BF16)

    d = D_MODEL
    sh = mod_ref[:, 0:d]
    sc = mod_ref[:, d:2 * d]
    sub_rows = x1_ref.shape[0]
    for r in range(x1_ref.shape[0] // sub_rows):
        r0 = r * sub_rows
        x2 = _moe_combine(wts_ref, x1_ref, modp_ref, lng_ref, lnb_ref, ya_ref, yb_ref, r0, sub_rows)
        x2_ref[r0:r0 + sub_rows, :] = x2
        h = x2 * (1.0 + sc) + sh
        z = jnp.dot(h.astype(BF16), w_bf[...], preferred_element_type=F32) + b_ref[...]
        z = 0.5 * z * (1.0 + lax.erf(z * (2.0 ** -0.5)))
        u = z[:, :GMLP_WIDTH]
        v = _layer_norm(z[:, GMLP_WIDTH:], g_ref[...], beta_ref[...]).astype(BF16)
        for ci in range(sub_rows // CHUNK):
            rows = slice(ci * CHUNK, (ci + 1) * CHUNK)
            out_rows = slice(r0 + ci * CHUNK, r0 + (ci + 1) * CHUNK)
            for g in range(N_SGU_GROUPS):
                lanes = slice(g * SGU_GROUP_DIM, (g + 1) * SGU_GROUP_DIM)
                mixed = jnp.dot(ws_bf[g], v[rows, lanes], preferred_element_type=F32) + bs_ref[:, g:g + 1]
                o_ref[out_rows, lanes] = (u[rows, lanes] * mixed).astype(BF16)


def _gmlp_gate(moe, ln_g, ln_b, mod_prev, mod_rows, w_in, b_in, sgu_g, sgu_b, w_s, b_s, seq):
    wts, x1, ys, pos2 = moe
    yg = _gather_rows_of_tokens(ys, pos2)
    t, d = x1.shape
    tm = 512
    steps_per_batch = seq // tm
    n_steps = t // tm
    gw = GMLP_WIDTH
    return pl.pallas_call(
        _gmlp_body,
        grid=(n_steps,),
        in_specs=_combine_specs(tm, n_steps, steps_per_batch) + [
            pl.BlockSpec((None, 1, N_MOD * d), lambda i: (i // steps_per_batch, 0, 0)),
            pl.BlockSpec((d, 2 * gw), lambda i: (0, 0)),
            pl.BlockSpec((1, 2 * gw), lambda i: (0, 0)),
            pl.BlockSpec((1, gw), lambda i: (0, 0)),
            pl.BlockSpec((1, gw), lambda i: (0, 0)),
            pl.BlockSpec((N_SGU_GROUPS, CHUNK, CHUNK), lambda i: (0, 0, 0)),
            pl.BlockSpec((CHUNK, N_SGU_GROUPS), lambda i: (0, 0)),
        ],
        out_specs=[pl.BlockSpec((tm, d), lambda i: (i, 0)), pl.BlockSpec((tm, gw), lambda i: (i, 0))],
        out_shape=[jax.ShapeDtypeStruct((t, d), F32), jax.ShapeDtypeStruct((t, gw), BF16)],
        scratch_shapes=[pltpu.VMEM((N_SGU_GROUPS, CHUNK, CHUNK), BF16), pltpu.VMEM((d, 2 * gw), BF16)],
        compiler_params=_cparams(1),
        name="combine_gmlp_gate",
    )(wts, x1, mod_prev, ln_g.reshape(1, d), ln_b.reshape(1, d), yg, yg,
      mod_rows, w_in, b_in.reshape(1, 2 * gw), sgu_g.reshape(1, gw), sgu_b.reshape(1, gw), w_s, b_s.T)


def _router_params(w_group, b_group, w_expert, b_expert):
    def lanes(group_part, expert_part):
        rows = group_part.shape[0]
        gap = jnp.zeros((rows, EXPERT_COL0 - N_EXPERT_GROUPS), F32)
        tail = jnp.zeros((rows, ROUTER_LANES - EXPERT_COL0 - N_EXPERTS), F32)
        return jnp.concatenate([group_part, gap, expert_part, tail], axis=1)

    return lanes(w_group, w_expert), lanes(b_group[None, :], b_expert[None, :])


def _moe_experts(x1, h2, ids, wts, layer, w_gate_up, w_down):
    t = x1.shape[0]
    n_rows = 2 * t + N_EXPERTS * EXPERT_TILE
    n_tiles = n_rows // EXPERT_TILE
    pos, tile_plan = _expert_sort(ids, n_tiles)
    pos2 = pos.reshape(2, t)
    xs = _sc_scatter_rows(h2, pos2, n_rows)
    ys = _expert_mlp(tile_plan, xs, w_gate_up, w_down, layer)
    return wts, x1, ys, pos2


def kernel(x, c, positions, ada_w, ada_b, post_ln_g, post_ln_b, attn_w_qkv, attn_b_qkv, attn_sinks, attn_w_o, attn_b_o, gmlp_w_in, gmlp_b_in, gmlp_sgu_ln_g, gmlp_sgu_ln_b, gmlp_w_s, gmlp_b_s, gmlp_w_out, gmlp_b_out, moe_w_group_router, moe_b_group_router, moe_w_expert_router, moe_b_expert_router, moe_w_gate_up, moe_w_down):
    batch, seq, d = x.shape
    t = batch * seq
    assert d == D_MODEL and batch <= MOD_ROWS and ada_w.shape[0] == DEPTH == 2
    assert seq % 1024 == 0, "token tiles of 1024 rows must not straddle sequences"
    assert t % (SC_WORKERS * SC_CHUNK) == 0, "every SparseCore subcore moves whole chunks"
    x2 = x.reshape(t, d)
    c_pad = jnp.pad(c, ((0, MOD_ROWS - batch), (0, 0)))
    mods = [_adaln_mod(c_pad, ada_w, ada_b, layer) for layer in range(DEPTH)]

    moe = None
    for layer in range(DEPTH):
        j = layer // 2
        if layer % 2 == 0:
            if moe is not None:
                x2 = _combine_ln(moe, mods[layer - 1], post_ln_g[layer - 1, 1], post_ln_b[layer - 1, 1], seq)
            qkv = _qkv_rope(x2, mods[layer], positions, attn_w_qkv[j], attn_b_qkv[j], seq)
            mix = _attention(qkv, attn_sinks[j], batch, seq)
            w_out, b_out = attn_w_o[j], attn_b_o[j]
        else:
            x2, mix = _gmlp_gate(moe, post_ln_g[layer - 1, 1], post_ln_b[layer - 1, 1], mods[layer - 1], mods[layer],
                                 gmlp_w_in[j], gmlp_b_in[j], gmlp_sgu_ln_g[j], gmlp_sgu_ln_b[j],
                                 gmlp_w_s[j], gmlp_b_s[j], seq)
            w_out, b_out = gmlp_w_out[j], gmlp_b_out[j]
        w_router, b_router = _router_params(moe_w_group_router[layer], moe_b_group_router[layer],
                                            moe_w_expert_router[layer], moe_b_expert_router[layer])
        x1, h2, ids, wts = _proj_ln_router(mix, x2, mods[layer], w_out, b_out, post_ln_g[layer, 0],
                                           post_ln_b[layer, 0], w_router, b_router, seq)
        moe = _moe_experts(x1, h2, ids, wts, layer, moe_w_gate_up, moe_w_down)
    x2 = _combine_ln(moe, mods[DEPTH - 1], post_ln_g[DEPTH - 1, 1], post_ln_b[DEPTH - 1, 1], seq)
    return x2.reshape(batch, seq, d)
```

```python
import functools

import jax
import jax.numpy as jnp
from jax import lax
from jax.experimental import pallas as pl
from jax.experimental.pallas import tpu as pltpu
from jax.experimental.pallas import tpu_sc as plsc

F32 = jnp.float32
BF16 = jnp.bfloat16
I32 = jnp.int32

D_MODEL = 1024
DEPTH = 2
HEAD_DIM = 64
N_Q_HEADS = 16
N_KV_HEADS = 4
GQA_GROUP = N_Q_HEADS // N_KV_HEADS
WINDOW = 128
ROPE_THETA = 10000.0
Q_WIDTH = N_Q_HEADS * HEAD_DIM
KV_WIDTH = N_KV_HEADS * HEAD_DIM
QKV_WIDTH = Q_WIDTH + 2 * KV_WIDTH
CHUNK = 128
GMLP_WIDTH = 2 * D_MODEL
N_SGU_GROUPS = 8
SGU_GROUP_DIM = GMLP_WIDTH // N_SGU_GROUPS
N_EXPERT_GROUPS = 4
EXPERTS_PER_GROUP = 8
N_EXPERTS = N_EXPERT_GROUPS * EXPERTS_PER_GROUP
EXPERT_FF = D_MODEL // 4
N_MOD = 6
DEEPNORM_ALPHA = (2.0 * DEPTH) ** 0.25
LN_EPS = 1e-5

LANES = 128
MOD_ROWS = 8
ROUTER_LANES = 128
EXPERT_COL0 = 8
SORT_CHUNK = 256
EXPERT_TILE = 512
SUB_ROWS = 256
VMEM_LIMIT = 56 * 1024 * 1024
SC_CORES = 2
SC_WORKERS = 32
SC_CHUNK = 64
SC_SLOTS = 3


def _cparams(n_axes, vmem=VMEM_LIMIT):
    return pltpu.CompilerParams(dimension_semantics=("arbitrary",) * n_axes, vmem_limit_bytes=vmem)


U32 = jnp.uint32
ROW_CHUNKS = D_MODEL // 2 // LANES


def _store_rows(ref, val, token0=0):
    n = val.shape[0]
    half = D_MODEL // 2
    words = pltpu.pack_elementwise([val[:, :half], val[:, half:]], packed_dtype=BF16)
    for c in range(ROW_CHUNKS):
        ref[pl.ds(token0 * ROW_CHUNKS + c, n, stride=ROW_CHUNKS), :] = words[:, c * LANES:(c + 1) * LANES]


def _load_rows(ref, n, token0=0):
    words = jnp.concatenate([ref[pl.ds(token0 * ROW_CHUNKS + c, n, stride=ROW_CHUNKS), :]
                             for c in range(ROW_CHUNKS)], axis=1)
    lo = lax.bitcast_convert_type(words << 16, F32)
    hi = lax.bitcast_convert_type(words & jnp.uint32(0xFFFF0000), F32)
    return jnp.concatenate([lo, hi], axis=1)


def _layer_norm(r, g, b, eps=LN_EPS):
    mu = jnp.mean(r, axis=-1, keepdims=True)
    d = r - mu
    var = jnp.mean(d * d, axis=-1, keepdims=True)
    return d * lax.rsqrt(var + eps) * g + b


def _mod_body(c_ref, w_ref, b_ref, o_ref):
    c = c_ref[...]
    ca = c * jax.nn.sigmoid(c)
    o_ref[...] = jnp.dot(ca.astype(BF16), w_ref[...].astype(BF16), preferred_element_type=F32) + b_ref[...]


def _adaln_mod(c_pad, ada_w, ada_b, layer):
    tn = 1536
    n_out = N_MOD * D_MODEL
    mod = pl.pallas_call(
        _mod_body,
        grid=(n_out // tn,),
        in_specs=[
            pl.BlockSpec((MOD_ROWS, D_MODEL), lambda j: (0, 0)),
            pl.BlockSpec((None, D_MODEL, tn), lambda j: (layer, 0, j)),
            pl.BlockSpec((None, 1, tn), lambda j: (layer, 0, j)),
        ],
        out_specs=pl.BlockSpec((MOD_ROWS, tn), lambda j: (0, j)),
        out_shape=jax.ShapeDtypeStruct((MOD_ROWS, n_out), F32),
        compiler_params=_cparams(1),
        name="adaln_mod",
    )(c_pad, ada_w, ada_b.reshape(DEPTH, 1, n_out))
    return mod.reshape(MOD_ROWS, 1, n_out)


def _qkv_body(x_ref, mod_ref, pos_ref, w_ref, b_ref, invf_ref, o_ref, wbf_ref):
    @pl.when(pl.program_id(0) == 0)
    def _():
        wbf_ref[...] = w_ref[...].astype(BF16)

    tm = x_ref.shape[0]
    sh = mod_ref[:, 0:D_MODEL]
    sc = mod_ref[:, D_MODEL:2 * D_MODEL]
    h = x_ref[...] * (1.0 + sc) + sh
    qkv = jnp.dot(h.astype(BF16), wbf_ref[...], preferred_element_type=F32) + b_ref[...]

    ang = invf_ref[...] * pos_ref[...].astype(F32)
    c = jnp.cos(ang)
    s = jnp.sin(ang)
    ct = jnp.concatenate([c, c, c, c], axis=0).T
    st = jnp.concatenate([-s, s, -s, s], axis=0).T
    lane = lax.broadcasted_iota(I32, (tm, LANES), 1)
    first_half = (lane & (HEAD_DIM // 2)) == 0
    n_rope = (Q_WIDTH + KV_WIDTH) // LANES
    for j in range(n_rope):
        blk = qkv[:, j * LANES:(j + 1) * LANES]
        rot = jnp.where(first_half, pltpu.roll(blk, LANES - HEAD_DIM // 2, 1), pltpu.roll(blk, HEAD_DIM // 2, 1))
        r = blk * ct + rot * st
        if j < Q_WIDTH // LANES:
            r = r * (HEAD_DIM ** -0.5)
        o_ref[:, j * LANES:(j + 1) * LANES] = r.astype(BF16)
    o_ref[:, Q_WIDTH + KV_WIDTH:] = qkv[:, Q_WIDTH + KV_WIDTH:].astype(BF16)


def _qkv_rope(x2, mod_rows, positions, w_qkv, b_qkv, seq):
    t = x2.shape[0]
    tm = 1024
    steps_per_batch = seq // tm
    inv_freq = ROPE_THETA ** (-jnp.arange(0, HEAD_DIM, 2, dtype=F32) / HEAD_DIM)
    return pl.pallas_call(
        _qkv_body,
        grid=(t // tm,),
        in_specs=[
            pl.BlockSpec((tm, D_MODEL), lambda i: (i, 0)),
            pl.BlockSpec((None, 1, N_MOD * D_MODEL), lambda i: (i // steps_per_batch, 0, 0)),
            pl.BlockSpec((None, 1, tm), lambda i: (i, 0, 0)),
            pl.BlockSpec((D_MODEL, QKV_WIDTH), lambda i: (0, 0)),
            pl.BlockSpec((1, QKV_WIDTH), lambda i: (0, 0)),
            pl.BlockSpec((HEAD_DIM // 2, 1), lambda i: (0, 0)),
        ],
        out_specs=pl.BlockSpec((tm, QKV_WIDTH), lambda i: (i, 0)),
        out_shape=jax.ShapeDtypeStruct((t, QKV_WIDTH), BF16),
        scratch_shapes=[pltpu.VMEM((D_MODEL, QKV_WIDTH), BF16)],
        compiler_params=_cparams(1),
        name="qkv_rope",
    )(x2, mod_rows, positions.reshape(t // tm, 1, tm), w_qkv, b_qkv.reshape(1, QKV_WIDTH),
      inv_freq.reshape(HEAD_DIM // 2, 1))


BF16_ROWS = 16


def _attn_prepare(kv, kab_ref, vx_ref, slot):
    kv = kv.astype(F32)
    low = lax.broadcasted_iota(I32, (WINDOW, LANES), 1) < HEAD_DIM
    ones = jnp.ones((WINDOW, LANES), F32)
    for g in range(N_KV_HEADS):
        for part, ref in ((0, None), (KV_WIDTH, vx_ref)):
            tile = kv[:, part + (g // 2) * LANES:part + (g // 2 + 1) * LANES]
            other = pltpu.roll(tile, HEAD_DIM, 1)
            in_low, in_high = (tile, other) if g % 2 == 0 else (other, tile)
            if ref is None:
                kab_ref[slot, 2 * g] = jnp.where(low, in_low, 0.0).astype(BF16)
                kab_ref[slot, 2 * g + 1] = jnp.where(low, 0.0, in_high).astype(BF16)
            else:
                both = jnp.where(low, in_low, in_high)
                vx_ref[slot, g] = jnp.concatenate([both, ones], axis=1).astype(BF16)


def _attn_block(sink_ref, q, kab_ref, vx_ref, s_ref, p_ref, prev, cur, first_block):
    for g in range(N_KV_HEADS):
        q_pair = jnp.concatenate([q[:, (2 * g) * LANES:(2 * g + 1) * LANES],
                                  q[:, (2 * g + 1) * LANES:(2 * g + 2) * LANES]], axis=0)
        for a in range(2):
            kband = jnp.concatenate([kab_ref[prev, 2 * g + a], kab_ref[cur, 2 * g + a]], axis=0)
            s = lax.dot_general(q_pair, kband, (((1,), (1,)), ((), ())), preferred_element_type=F32)
            s_ref[GQA_GROUP * g + a] = s[:WINDOW]
            s_ref[GQA_GROUP * g + 2 + a] = s[WINDOW:]
    qi = lax.broadcasted_iota(I32, (WINDOW, 2 * WINDOW), 0) + WINDOW
    kj = lax.broadcasted_iota(I32, (WINDOW, 2 * WINDOW), 1)
    mask = (kj <= qi) & (kj > qi - WINDOW) & ((kj >= WINDOW) | jnp.logical_not(first_block))
    key0 = lax.broadcasted_iota(I32, (1, 2 * WINDOW), 1) == 0
    for h in range(N_Q_HEADS):
        s = jnp.where(mask, s_ref[h], jnp.where(key0, sink_ref[h], -jnp.inf))
        m = jnp.max(s, axis=-1, keepdims=True)
        p_ref[h] = jnp.exp(s - m).astype(BF16)
    low = lax.broadcasted_iota(I32, (WINDOW, LANES), 1) < HEAD_DIM
    sink_row = ((lax.broadcasted_iota(I32, (BF16_ROWS, 2 * LANES), 0) == 0)
                & (lax.broadcasted_iota(I32, (BF16_ROWS, 2 * LANES), 1) < LANES))
    out_tiles = []
    for g in range(N_KV_HEADS):
        v_prev = vx_ref[prev, g]
        v_head = jnp.where(sink_row, 0.0, v_prev[:BF16_ROWS].astype(F32)).astype(BF16)
        vband = jnp.concatenate([v_head, v_prev[BF16_ROWS:], vx_ref[cur, g]], axis=0)
        p4 = p_ref[GQA_GROUP * g:GQA_GROUP * (g + 1)].reshape(GQA_GROUP * WINDOW, 2 * WINDOW)
        o4 = jnp.dot(p4, vband, preferred_element_type=F32)
        heads = []
        for j in range(GQA_GROUP):
            blk = o4[j * WINDOW:(j + 1) * WINDOW]
            heads.append(blk[:, :LANES] / blk[:, LANES:])
        out_tiles.append(jnp.where(low, heads[0], heads[1]))
        out_tiles.append(jnp.where(low, heads[2], heads[3]))
    return jnp.concatenate(out_tiles, axis=1).astype(BF16)


def _attn_body(sink_ref, q_ref, kv_ref, o_ref, kab_ref, vx_ref, s_ref, p_ref):
    n = pl.program_id(1)

    @pl.when(n == 0)
    def _():
        kab_ref[1] = jnp.zeros(kab_ref.shape[1:], kab_ref.dtype)
        half = (N_KV_HEADS, WINDOW, LANES)
        vx_ref[1] = jnp.concatenate([jnp.zeros(half, BF16), jnp.ones(half, BF16)], axis=-1)

    scratch = (kab_ref, vx_ref, s_ref, p_ref)
    for blk in range(q_ref.shape[0] // WINDOW):
        rows = slice(blk * WINDOW, (blk + 1) * WINDOW)
        cur = blk % 2
        _attn_prepare(kv_ref[rows, :], kab_ref, vx_ref, cur)
        o_ref[rows, :] = _attn_block(sink_ref, q_ref[rows, :], *scratch, 1 - cur, cur,
                                     (n == 0) if blk == 0 else False)


def _attention(qkv, sinks, batch, seq):
    t = qkv.shape[0]
    tq = 8 * WINDOW
    steps = seq // tq
    kv_col = Q_WIDTH // (2 * KV_WIDTH)
    return pl.pallas_call(
        _attn_body,
        grid=(batch, steps),
        in_specs=[
            pl.BlockSpec(memory_space=pltpu.SMEM),
            pl.BlockSpec((tq, Q_WIDTH), lambda b, n: (b * steps + n, 0)),
            pl.BlockSpec((tq, 2 * KV_WIDTH), lambda b, n: (b * steps + n, kv_col)),
        ],
        out_specs=pl.BlockSpec((tq, Q_WIDTH), lambda b, n: (b * steps + n, 0)),
        out_shape=jax.ShapeDtypeStruct((t, Q_WIDTH), BF16),
        scratch_shapes=[
            pltpu.VMEM((2, 2 * N_KV_HEADS, WINDOW, LANES), BF16),
            pltpu.VMEM((2, N_KV_HEADS, WINDOW, 2 * LANES), BF16),
            pltpu.VMEM((N_Q_HEADS, WINDOW, 2 * WINDOW), F32),
            pltpu.VMEM((N_Q_HEADS, WINDOW, 2 * WINDOW), BF16),
        ],
        compiler_params=_cparams(2),
        name="swa_attention",
    )(sinks, qkv, qkv)


def _route(lt):
    tm = lt.shape[1]
    row = lax.broadcasted_iota(I32, (EXPERTS_PER_GROUP, tm), 0)
    neg = -jnp.inf
    gl = jnp.where(row < N_EXPERT_GROUPS, lt[0:EXPERTS_PER_GROUP], neg)
    gm = jnp.max(gl, axis=0, keepdims=True)
    g_p = 1.0 / jnp.sum(jnp.exp(gl - gm), axis=0, keepdims=True)
    g_idx = jnp.min(jnp.where(gl == gm, row, EXPERTS_PER_GROUP), axis=0, keepdims=True)
    sel = lt[EXPERT_COL0 + (N_EXPERT_GROUPS - 1) * EXPERTS_PER_GROUP:EXPERT_COL0 + N_EXPERTS]
    for g in range(N_EXPERT_GROUPS - 2, -1, -1):
        lo = EXPERT_COL0 + g * EXPERTS_PER_GROUP
        sel = jnp.where(g_idx == g, lt[lo:lo + EXPERTS_PER_GROUP], sel)
    v1 = jnp.max(sel, axis=0, keepdims=True)
    i1 = jnp.min(jnp.where(sel == v1, row, EXPERTS_PER_GROUP), axis=0, keepdims=True)
    sel2 = jnp.where(row == i1, neg, sel)
    v2 = jnp.max(sel2, axis=0, keepdims=True)
    i2 = jnp.min(jnp.where(sel2 == v2, row, EXPERTS_PER_GROUP), axis=0, keepdims=True)
    e2 = jnp.exp(v2 - v1)
    w1 = g_p / (1.0 + e2)
    w2 = g_p * e2 / (1.0 + e2)
    base = g_idx * EXPERTS_PER_GROUP
    return base + i1, base + i2, w1, w2


def _proj_body(o_ref, x_ref, mod_ref, w_ref, b_ref, lng_ref, lnb_ref, wr_ref, br_ref,
               x1_ref, h2_ref, ids_ref, wts_ref, wbf_ref):
    @pl.when(pl.program_id(0) == 0)
    def _():
        wbf_ref[...] = w_ref[...].astype(BF16)

    d = D_MODEL
    gt_m = mod_ref[:, 2 * d:3 * d]
    sh_f = mod_ref[:, 3 * d:4 * d]
    sc_f = mod_ref[:, 4 * d:5 * d]
    wr = wr_ref[...].astype(BF16)
    for r in range(x_ref.shape[0] // SUB_ROWS):
        rows = slice(r * SUB_ROWS, (r + 1) * SUB_ROWS)
        y = jnp.dot(o_ref[rows, :], wbf_ref[...], preferred_element_type=F32) + b_ref[...]
        x1 = _layer_norm(DEEPNORM_ALPHA * x_ref[rows, :] + (1.0 + gt_m) * y, lng_ref[...], lnb_ref[...])
        x1_ref[rows, :] = x1
        h2 = x1 * (1.0 + sc_f) + sh_f
        _store_rows(h2_ref, h2, r * SUB_ROWS)
        logits = jnp.dot(h2.astype(BF16), wr, preferred_element_type=F32) + br_ref[...]
        ea, eb, wa, wb = _route(logits.T)
        ids_ref[0:1, rows] = ea
        ids_ref[1:2, rows] = eb
        wts_ref[0:1, rows] = wa
        wts_ref[1:2, rows] = wb


def _proj_ln_router(o, x2, mod_rows, w, b, ln_g, ln_b, w_router, b_router, seq):
    t, k = o.shape
    tm = 1024
    steps_per_batch = seq // tm
    d = D_MODEL
    return pl.pallas_call(
        _proj_body,
        grid=(t // tm,),
        in_specs=[
            pl.BlockSpec((tm, k), lambda i: (i, 0)),
            pl.BlockSpec((tm, d), lambda i: (i, 0)),
            pl.BlockSpec((None, 1, N_MOD * d), lambda i: (i // steps_per_batch, 0, 0)),
            pl.BlockSpec((k, d), lambda i: (0, 0), pipeline_mode=pl.Buffered(1)),
            pl.BlockSpec((1, d), lambda i: (0, 0)),
            pl.BlockSpec((1, d), lambda i: (0, 0)),
            pl.BlockSpec((1, d), lambda i: (0, 0)),
            pl.BlockSpec((d, ROUTER_LANES), lambda i: (0, 0)),
            pl.BlockSpec((1, ROUTER_LANES), lambda i: (0, 0)),
        ],
        out_specs=[
            pl.BlockSpec((tm, d), lambda i: (i, 0)),
            pl.BlockSpec((tm * ROW_CHUNKS, LANES), lambda i: (i, 0)),
            pl.BlockSpec((2, tm), lambda i: (0, i)),
            pl.BlockSpec((2, tm), lambda i: (0, i)),
        ],
        out_shape=[
            jax.ShapeDtypeStruct((t, d), F32),
            jax.ShapeDtypeStruct((t * ROW_CHUNKS, LANES), U32),
            jax.ShapeDtypeStruct((2, t), I32),
            jax.ShapeDtypeStruct((2, t), F32),
        ],
        scratch_shapes=[pltpu.VMEM((k, d), BF16)],
        compiler_params=_cparams(1),
        name="proj_ln_router",
    )(o, x2, mod_rows, w, b.reshape(1, d), ln_g.reshape(1, d), ln_b.reshape(1, d), w_router, b_router)


def _sort_body(ids_ref, pos_ref, te_ref, nused_ref, tv_ref, wplan_ref, rank_ref):
    n_rows = ids_ref.shape[0]
    c = SORT_CHUNK
    erow = lax.broadcasted_iota(I32, (N_EXPERTS, c), 0)
    tri = (lax.broadcasted_iota(I32, (c, c), 0) <= lax.broadcasted_iota(I32, (c, c), 1)).astype(BF16)

    def rank_step(r, carry):
        onehot = erow == ids_ref[pl.ds(r, 1), :]
        pref = jnp.dot(onehot.astype(BF16), tri, preferred_element_type=F32)
        rank = jnp.sum(jnp.where(onehot, pref + carry, 0.0), axis=0, keepdims=True) - 1.0
        rank_ref[pl.ds(r, 1), :] = rank
        return carry + pref[:, c - 1:c]

    counts = lax.fori_loop(0, n_rows, rank_step, jnp.zeros((N_EXPERTS, 1), F32), unroll=8)
    n_tile = jnp.floor((counts + (EXPERT_TILE - 1)) * (1.0 / EXPERT_TILE))
    low = (lax.broadcasted_iota(I32, (N_EXPERTS, N_EXPERTS), 1)
           <= lax.broadcasted_iota(I32, (N_EXPERTS, N_EXPERTS), 0)).astype(BF16)
    cum = jnp.dot(low, jnp.broadcast_to(n_tile, (N_EXPERTS, LANES)).astype(BF16),
                  preferred_element_type=F32)[:, 0:1]
    row_off = (cum - n_tile) * EXPERT_TILE

    def pos_step(r, _):
        onehot = erow == ids_ref[pl.ds(r, 1), :]
        off = jnp.sum(jnp.where(onehot, row_off, 0.0), axis=0, keepdims=True)
        pos_ref[pl.ds(r, 1), :] = (off + rank_ref[pl.ds(r, 1), :]).astype(I32)
        return 0

    lax.fori_loop(0, n_rows, pos_step, 0, unroll=8)
    total = jnp.max(cum, axis=0, keepdims=True)
    n_lanes = te_ref.shape[1]
    tile = jnp.minimum(lax.broadcasted_iota(I32, (N_EXPERTS, n_lanes), 1).astype(F32), total - 1.0)
    te_ref[...] = jnp.sum(jnp.where(cum <= tile, 1.0, 0.0), axis=0, keepdims=True).astype(I32)
    nused_ref[...] = jnp.broadcast_to(total, nused_ref.shape).astype(I32)
    tile_f = lax.broadcasted_iota(I32, (N_EXPERTS, n_lanes), 1).astype(F32)
    first = cum - n_tile
    rows_left = jnp.clip(counts - (tile_f - first) * EXPERT_TILE, 0.0, float(EXPERT_TILE))
    owns = (first <= tile_f) & (tile_f < cum)
    tv_ref[...] = jnp.sum(jnp.where(owns, rows_left, 0.0), axis=0, keepdims=True).astype(I32)
    def lane_sum(x):
        return jnp.sum(x, axis=0, keepdims=True)

    is_first = lane_sum(jnp.where(owns & (tile_f == first), 1.0, 0.0))
    group_end = lane_sum(jnp.where(owns, cum, 0.0))
    next_expert = jnp.where(group_end < total, lane_sum(jnp.where(cum <= group_end, 1.0, 0.0)), -1.0)
    ordinal = lane_sum(jnp.where((cum <= tile_f[0:1]) & (n_tile > 0.0), 1.0, 0.0))
    slot = ordinal - 2.0 * jnp.floor(ordinal * 0.5)
    wplan_ref[0:1, :] = is_first.astype(I32)
    wplan_ref[1:2, :] = next_expert.astype(I32)
    wplan_ref[2:3, :] = slot.astype(I32)
    wplan_ref[3:8, :] = jnp.zeros((5, n_lanes), I32)


def _expert_sort(ids, n_tiles):
    n_assign = ids.shape[0] * ids.shape[1]
    n_rows = n_assign // SORT_CHUNK
    te_lanes = -(-n_tiles // LANES) * LANES
    pos, te, nused, tv, wplan = pl.pallas_call(
        _sort_body,
        grid=(1,),
        in_specs=[pl.BlockSpec((n_rows, SORT_CHUNK), lambda i: (0, 0))],
        out_specs=[
            pl.BlockSpec((n_rows, SORT_CHUNK), lambda i: (0, 0)),
            pl.BlockSpec((1, te_lanes), lambda i: (0, 0)),
            pl.BlockSpec((1, LANES), lambda i: (0, 0)),
            pl.BlockSpec((1, te_lanes), lambda i: (0, 0)),
            pl.BlockSpec((8, te_lanes), lambda i: (0, 0)),
        ],
        out_shape=[
            jax.ShapeDtypeStruct((n_rows, SORT_CHUNK), I32),
            jax.ShapeDtypeStruct((1, te_lanes), I32),
            jax.ShapeDtypeStruct((1, LANES), I32),
            jax.ShapeDtypeStruct((1, te_lanes), I32),
            jax.ShapeDtypeStruct((8, te_lanes), I32),
        ],
        scratch_shapes=[pltpu.VMEM((n_rows, SORT_CHUNK), F32)],
        compiler_params=_cparams(1),
        name="expert_sort",
    )(ids.reshape(n_rows, SORT_CHUNK))
    tile_plan = (te[0, :n_tiles], nused[0, :1], tv[0, :n_tiles],
                 wplan[0, :n_tiles], wplan[1, :n_tiles], wplan[2, :n_tiles])
    return pos, tile_plan


def _sc_mesh():
    return plsc.VectorSubcoreMesh(core_axis_name="c", subcore_axis_name="s", num_cores=SC_CORES,
                                  num_subcores=SC_WORKERS // SC_CORES)


def _sc_ring(n_chunks, read, write):
    reads, writes = {}, {}
    for j in range(min(SC_SLOTS - 1, n_chunks)):
        reads[j] = read(j)
    for j in range(n_chunks):
        for cp in reads.pop(j):
            cp.wait()
        nxt = j + SC_SLOTS - 1
        if nxt < n_chunks:
            for cp in writes.pop(nxt - SC_SLOTS, []):
                cp.wait()
            reads[nxt] = read(nxt)
        writes[j] = write(j)
    for cps in writes.values():
        for cp in cps:
            cp.wait()


def _sc_scatter_rows(src, pos2, n_rows):
    t = pos2.shape[1]
    src3 = src.reshape(t, ROW_CHUNKS, LANES)
    per_worker = t // SC_WORKERS
    n_chunks = per_worker // SC_CHUNK
    idx = pos2.reshape(2, SC_WORKERS, n_chunks, SC_CHUNK).transpose(1, 0, 2, 3)
    idx = idx.reshape(SC_WORKERS, 2 * n_chunks, SC_CHUNK)

    @functools.partial(
        pl.kernel, mesh=_sc_mesh(),
        out_type=jax.ShapeDtypeStruct((n_rows, ROW_CHUNKS, LANES), src.dtype),
        scratch_types=[
            pltpu.VMEM((2 * n_chunks, SC_CHUNK), I32),
            pltpu.VMEM((SC_SLOTS, SC_CHUNK, ROW_CHUNKS, LANES), src.dtype),
            pltpu.SemaphoreType.DMA((SC_SLOTS,)),
            pltpu.SemaphoreType.DMA((SC_SLOTS,)),
        ],
        name="sc_scatter_rows",
    )
    def scatter(src_hbm, idx_hbm, out_hbm, idx_v, rows_v, rsem, wsem):
        wid = lax.axis_index("s") * SC_CORES + lax.axis_index("c")
        pltpu.sync_copy(idx_hbm.at[wid], idx_v)

        def read(j):
            b = j % SC_SLOTS
            return [pltpu.async_copy(src_hbm.at[pl.ds(wid * per_worker + j * SC_CHUNK, SC_CHUNK)], rows_v.at[b],
                                     rsem.at[b])]

        def write(j):
            b = j % SC_SLOTS
            return [pltpu.async_copy(rows_v.at[b], out_hbm.at[idx_v.at[k * n_chunks + j]], wsem.at[b])
                    for k in range(2)]

        _sc_ring(n_chunks, read, write)

    return scatter(src3, idx).reshape(n_rows * ROW_CHUNKS, LANES)


def _sc_gather_rows(table, idx):
    n = idx.shape[0]
    table3 = table.reshape(-1, ROW_CHUNKS, LANES)
    per_worker = n // SC_WORKERS
    n_chunks = per_worker // SC_CHUNK

    @functools.partial(
        pl.kernel, mesh=_sc_mesh(),
        out_type=jax.ShapeDtypeStruct((n, ROW_CHUNKS, LANES), table.dtype),
        scratch_types=[
            pltpu.VMEM((n_chunks, SC_CHUNK), I32),
            pltpu.VMEM((SC_SLOTS, SC_CHUNK, ROW_CHUNKS, LANES), table.dtype),
            pltpu.SemaphoreType.DMA((SC_SLOTS,)),
            pltpu.SemaphoreType.DMA((SC_SLOTS,)),
        ],
        name="sc_gather_rows",
    )
    def gather(table_hbm, idx_hbm, out_hbm, idx_v, rows_v, rsem, wsem):
        wid = lax.axis_index("s") * SC_CORES + lax.axis_index("c")
        pltpu.sync_copy(idx_hbm.at[wid], idx_v)

        def read(j):
            b = j % SC_SLOTS
            return [pltpu.async_copy(table_hbm.at[idx_v.at[j]], rows_v.at[b], rsem.at[b])]

        def write(j):
            b = j % SC_SLOTS
            return [pltpu.async_copy(rows_v.at[b], out_hbm.at[pl.ds(wid * per_worker + j * SC_CHUNK, SC_CHUNK)],
                                     wsem.at[b])]

        _sc_ring(n_chunks, read, write)

    out = gather(table3, idx.reshape(SC_WORKERS, n_chunks, SC_CHUNK))
    return out.reshape(n * ROW_CHUNKS, LANES)


XS_SLOTS = 4


def _expert_body(te_ref, nused_ref, tv_ref, first_ref, next_ref, slot_ref, xs_hbm, wgu_hbm, wd_hbm, ys_ref,
                 wgu_bf, wd_bf, xbuf, xsem, wgu_buf, wd_buf, wsem, *, e0):
    i = pl.program_id(0)
    n_used = nused_ref[0]
    used = i < n_used
    tile_rows = EXPERT_TILE * ROW_CHUNKS

    def weight_fetch(expert, slot):
        return (pltpu.make_async_copy(wgu_hbm.at[e0 + expert], wgu_buf.at[slot], wsem.at[0, slot]),
                pltpu.make_async_copy(wd_hbm.at[e0 + expert], wd_buf.at[slot], wsem.at[1, slot]))

    @pl.when((i == 0) & used)
    def _():
        for cp in weight_fetch(te_ref[0], slot_ref[0]):
            cp.start()

    @pl.when(used & (first_ref[i] == 1))
    def _():
        slot = slot_ref[i]
        for cp in weight_fetch(te_ref[i], slot):
            cp.wait()
        wgu_bf[...] = wgu_buf[slot].astype(BF16)
        wd_bf[...] = wd_buf[slot].astype(BF16)

        @pl.when(next_ref[i] >= 0)
        def _():
            for cp in weight_fetch(next_ref[i], 1 - slot):
                cp.start()


    def fetch(tile):
        slot = lax.rem(tile, XS_SLOTS)
        r0 = pl.multiple_of(tile * tile_rows, tile_rows)
        return pltpu.make_async_copy(xs_hbm.at[pl.ds(r0, tile_rows)], xbuf.at[slot], xsem.at[slot])

    ahead = XS_SLOTS - 1
    for tile in range(ahead):
        @pl.when((i == 0) & (tile < n_used))
        def _(tile=tile):
            fetch(tile).start()

    @pl.when(i + ahead < n_used)
    def _():
        fetch(i + ahead).start()

    @pl.when(used)
    def _():
        fetch(i).wait()
        live = lax.broadcasted_iota(I32, (EXPERT_TILE, 1), 0) < tv_ref[i]
        xs = jnp.where(live, _load_rows(xbuf.at[lax.rem(i, XS_SLOTS)], EXPERT_TILE), 0.0).astype(BF16)
        gu = jnp.dot(xs, wgu_bf[...], preferred_element_type=F32)
        gate = gu[:, :EXPERT_FF]
        up = gu[:, EXPERT_FF:]
        act = gate * jax.nn.sigmoid(gate) * up
        _store_rows(ys_ref, jnp.dot(act.astype(BF16), wd_bf[...], preferred_element_type=F32))

    @pl.when(jnp.logical_not(used))
    def _():
        ys_ref[...] = jnp.zeros(ys_ref.shape, ys_ref.dtype)


def _expert_mlp(tile_plan, xs, w_gate_up, w_down, layer):
    d = D_MODEL
    n_tiles = xs.shape[0] // (EXPERT_TILE * ROW_CHUNKS)
    f2 = 2 * EXPERT_FF
    tile_rows = EXPERT_TILE * ROW_CHUNKS
    grid_spec = pltpu.PrefetchScalarGridSpec(
        num_scalar_prefetch=len(tile_plan),
        grid=(n_tiles,),
        in_specs=[
            pl.BlockSpec(memory_space=pl.ANY),
            pl.BlockSpec(memory_space=pl.ANY),
            pl.BlockSpec(memory_space=pl.ANY),
        ],
        out_specs=pl.BlockSpec((tile_rows, LANES), lambda i, *plan: (i, 0)),
        scratch_shapes=[pltpu.VMEM((d, f2), BF16), pltpu.VMEM((EXPERT_FF, d), BF16),
                        pltpu.VMEM((XS_SLOTS, tile_rows, LANES), U32), pltpu.SemaphoreType.DMA((XS_SLOTS,)),
                        pltpu.VMEM((2, d, f2), F32), pltpu.VMEM((2, EXPERT_FF, d), F32),
                        pltpu.SemaphoreType.DMA((2, 2))],
    )
    return pl.pallas_call(
        functools.partial(_expert_body, e0=layer * N_EXPERTS),
        grid_spec=grid_spec,
        out_shape=jax.ShapeDtypeStruct(xs.shape, U32),
        compiler_params=_cparams(1),
        name="expert_mlp",
    )(*tile_plan, xs, w_gate_up.reshape(DEPTH * N_EXPERTS, d, f2),
      w_down.reshape(DEPTH * N_EXPERTS, EXPERT_FF, d))


def _moe_combine(wts_ref, x_ref, mod_ref, lng_ref, lnb_ref, ya_ref, yb_ref, token0, n):
    d = D_MODEL
    rows = slice(token0, token0 + n)
    cols = []
    for k in range(2):
        wt = jnp.broadcast_to(wts_ref[k:k + 1, rows], (LANES, n)).T
        cols.append(jnp.concatenate([wt] * (d // LANES), axis=1))
    y = cols[0] * _load_rows(ya_ref, n, token0) + cols[1] * _load_rows(yb_ref, n, token0)
    gt_f = mod_ref[:, 5 * d:6 * d]
    return _layer_norm(DEEPNORM_ALPHA * x_ref[rows, :] + (1.0 + gt_f) * y, lng_ref[...], lnb_ref[...])


def _combine_specs(tm, n_steps, steps_per_batch):
    d = D_MODEL
    return [
        pl.BlockSpec((2, tm), lambda i: (0, i)),
        pl.BlockSpec((tm, d), lambda i: (i, 0)),
        pl.BlockSpec((None, 1, N_MOD * d), lambda i: (i // steps_per_batch, 0, 0)),
        pl.BlockSpec((1, d), lambda i: (0, 0)),
        pl.BlockSpec((1, d), lambda i: (0, 0)),
        pl.BlockSpec((tm * ROW_CHUNKS, LANES), lambda i: (i, 0)),
        pl.BlockSpec((tm * ROW_CHUNKS, LANES), lambda i: (n_steps + i, 0)),
    ]


def _gather_rows_of_tokens(ys, pos2):
    return _sc_gather_rows(ys, pos2.reshape(-1))


def _combine_body(wts_ref, x_ref, mod_ref, lng_ref, lnb_ref, ya_ref, yb_ref, o_ref):
    for r in range(x_ref.shape[0] // SUB_ROWS):
        o_ref[r * SUB_ROWS:(r + 1) * SUB_ROWS, :] = _moe_combine(wts_ref, x_ref, mod_ref, lng_ref, lnb_ref, ya_ref,
                                                                 yb_ref, r * SUB_ROWS, SUB_ROWS)


def _combine_ln(moe, mod_rows, ln_g, ln_b, seq):
    wts, x1, ys, pos2 = moe
    yg = _gather_rows_of_tokens(ys, pos2)
    t, d = x1.shape
    tm = 1024
    n_steps = t // tm
    return pl.pallas_call(
        _combine_body,
        grid=(n_steps,),
        in_specs=_combine_specs(tm, n_steps, seq // tm),
        out_specs=pl.BlockSpec((tm, d), lambda i: (i, 0)),
        out_shape=jax.ShapeDtypeStruct((t, d), F32),
        compiler_params=_cparams(1),
        name="moe_combine_ln",
    )(wts, x1, mod_rows, ln_g.reshape(1, d), ln_b.reshape(1, d), yg, yg)


def _gmlp_body(wts_ref, x1_ref, modp_ref, lng_ref, lnb_ref, ya_ref, yb_ref,
               mod_ref, w_ref, b_ref, g_ref, beta_ref, ws_ref, bs_ref, x2_ref, o_ref, ws_bf, w_bf):
    @pl.when(pl.program_id(0) == 0)
    def _():
        tri = lax.broadcasted_iota(I32, (CHUNK, CHUNK), 0) >= lax.broadcasted_iota(I32, (CHUNK, CHUNK), 1)
        for g in range(N_SGU_GROUPS):
            ws_bf[g] = jnp.where(tri, 0.5 * ws_ref[g], 0.0).astype(BF16)
        w_bf[...] = w_ref[...].astype(BF16)

    d = D_MODEL
    sh = mod_ref[:, 0:d]
    sc = mod_ref[:, d:2 * d]
    sub_rows = x1_ref.shape[0]
    for r in range(x1_ref.shape[0] // sub_rows):
        r0 = r * sub_rows
        x2 = _moe_combine(wts_ref, x1_ref, modp_ref, lng_ref, lnb_ref, ya_ref, yb_ref, r0, sub_rows)
        x2_ref[r0:r0 + sub_rows, :] = x2
        h = x2 * (1.0 + sc) + sh
        z = jnp.dot(h.astype(BF16), w_bf[...], preferred_element_type=F32) + b_ref[...]
        z2 = z * (1.0 + lax.erf(z * (2.0 ** -0.5)))
        u = z2[:, :GMLP_WIDTH]
        v = _layer_norm(z2[:, GMLP_WIDTH:], g_ref[...], beta_ref[...], eps=4.0 * LN_EPS).astype(BF16)
        for ci in range(sub_rows // CHUNK):
            rows = slice(ci * CHUNK, (ci + 1) * CHUNK)
            out_rows = slice(r0 + ci * CHUNK, r0 + (ci + 1) * CHUNK)
            for g in range(N_SGU_GROUPS):
                lanes = slice(g * SGU_GROUP_DIM, (g + 1) * SGU_GROUP_DIM)
                mixed = jnp.dot(ws_bf[g], v[rows, lanes], preferred_element_type=F32) + 0.5 * bs_ref[:, g:g + 1]
                o_ref[out_rows, lanes] = (u[rows, lanes] * mixed).astype(BF16)


def _gmlp_gate(moe, ln_g, ln_b, mod_prev, mod_rows, w_in, b_in, sgu_g, sgu_b, w_s, b_s, seq):
    wts, x1, ys, pos2 = moe
    yg = _gather_rows_of_tokens(ys, pos2)
    t, d = x1.shape
    tm = 512
    steps_per_batch = seq // tm
    n_steps = t // tm
    gw = GMLP_WIDTH
    return pl.pallas_call(
        _gmlp_body,
        grid=(n_steps,),
        in_specs=_combine_specs(tm, n_steps, steps_per_batch) + [
            pl.BlockSpec((None, 1, N_MOD * d), lambda i: (i // steps_per_batch, 0, 0)),
            pl.BlockSpec((d, 2 * gw), lambda i: (0, 0)),
            pl.BlockSpec((1, 2 * gw), lambda i: (0, 0)),
            pl.BlockSpec((1, gw), lambda i: (0, 0)),
            pl.BlockSpec((1, gw), lambda i: (0, 0)),
            pl.BlockSpec((N_SGU_GROUPS, CHUNK, CHUNK), lambda i: (0, 0, 0)),
            pl.BlockSpec((CHUNK, N_SGU_GROUPS), lambda i: (0, 0)),
        ],
        out_specs=[pl.BlockSpec((tm, d), lambda i: (i, 0)), pl.BlockSpec((tm, gw), lambda i: (i, 0))],
        out_shape=[jax.ShapeDtypeStruct((t, d), F32), jax.ShapeDtypeStruct((t, gw), BF16)],
        scratch_shapes=[pltpu.VMEM((N_SGU_GROUPS, CHUNK, CHUNK), BF16), pltpu.VMEM((d, 2 * gw), BF16)],
        compiler_params=_cparams(1),
        name="combine_gmlp_gate",
    )(wts, x1, mod_prev, ln_g.reshape(1, d), ln_b.reshape(1, d), yg, yg,
      mod_rows, w_in, b_in.reshape(1, 2 * gw), sgu_g.reshape(1, gw), sgu_b.reshape(1, gw), w_s, b_s.T)


def _router_params(w_group, b_group, w_expert, b_expert):
    def lanes(group_part, expert_part):
        rows = group_part.shape[0]
        gap = jnp.zeros((rows, EXPERT_COL0 - N_EXPERT_GROUPS), F32)
        tail = jnp.zeros((rows, ROUTER_LANES - EXPERT_COL0 - N_EXPERTS), F32)
        return jnp.concatenate([group_part, gap, expert_part, tail], axis=1)

    return lanes(w_group, w_expert), lanes(b_group[None, :], b_expert[None, :])


def _moe_experts(x1, h2, ids, wts, layer, w_gate_up, w_down):
    t = x1.shape[0]
    n_rows = 2 * t + N_EXPERTS * EXPERT_TILE
    n_tiles = n_rows // EXPERT_TILE
    pos, tile_plan = _expert_sort(ids, n_tiles)
    pos2 = pos.reshape(2, t)
    xs = _sc_scatter_rows(h2, pos2, n_rows)
    ys = _expert_mlp(tile_plan, xs, w_gate_up, w_down, layer)
    return wts, x1, ys, pos2


def kernel(x, c, positions, ada_w, ada_b, post_ln_g, post_ln_b, attn_w_qkv, attn_b_qkv, attn_sinks, attn_w_o, attn_b_o, gmlp_w_in, gmlp_b_in, gmlp_sgu_ln_g, gmlp_sgu_ln_b, gmlp_w_s, gmlp_b_s, gmlp_w_out, gmlp_b_out, moe_w_group_router, moe_b_group_router, moe_w_expert_router, moe_b_expert_router, moe_w_gate_up, moe_w_down):
    batch, seq, d = x.shape
    t = batch * seq
    assert d == D_MODEL and batch <= MOD_ROWS and ada_w.shape[0] == DEPTH == 2
    assert seq % 1024 == 0, "token tiles of 1024 rows must not straddle sequences"
    assert t % (SC_WORKERS * SC_CHUNK) == 0, "every SparseCore subcore moves whole chunks"
    x2 = x.reshape(t, d)
    c_pad = jnp.pad(c, ((0, MOD_ROWS - batch), (0, 0)))
    mods = [_adaln_mod(c_pad, ada_w, ada_b, layer) for layer in range(DEPTH)]

    moe = None
    for layer in range(DEPTH):
        j = layer // 2
        if layer % 2 == 0:
            if moe is not None:
                x2 = _combine_ln(moe, mods[layer - 1], post_ln_g[layer - 1, 1], post_ln_b[layer - 1, 1], seq)
            qkv = _qkv_rope(x2, mods[layer], positions, attn_w_qkv[j], attn_b_qkv[j], seq)
            mix = _attention(qkv, attn_sinks[j], batch, seq)
            w_out, b_out = attn_w_o[j], attn_b_o[j]
        else:
            x2, mix = _gmlp_gate(moe, post_ln_g[layer - 1, 1], post_ln_b[layer - 1, 1], mods[layer - 1], mods[layer],
                                 gmlp_w_in[j], gmlp_b_in[j], gmlp_sgu_ln_g[j], gmlp_sgu_ln_b[j],
                                 gmlp_w_s[j], gmlp_b_s[j], seq)
            w_out, b_out = gmlp_w_out[j], gmlp_b_out[j]
        w_router, b_router = _router_params(moe_w_group_router[layer], moe_b_group_router[layer],
                                            moe_w_expert_router[layer], moe_b_expert_router[layer])
        x1, h2, ids, wts = _proj_ln_router(mix, x2, mods[layer], w_out, b_out, post_ln_g[layer, 0],
                                           post_ln_b[layer, 0], w_router, b_router, seq)
        moe = _moe_experts(x1, h2, ids, wts, layer, moe_w_gate_up, moe_w_down)
    x2 = _combine_ln(moe, mods[DEPTH - 1], post_ln_g[DEPTH - 1, 1], post_ln_b[DEPTH - 1, 1], seq)
    return x2.reshape(batch, seq, d)
```

```python
import functools

import jax
import jax.numpy as jnp
from jax import lax
from jax.experimental import pallas as pl
from jax.experimental.pallas import tpu as pltpu
from jax.experimental.pallas import tpu_sc as plsc

F32 = jnp.float32
BF16 = jnp.bfloat16
I32 = jnp.int32

D_MODEL = 1024
DEPTH = 2
HEAD_DIM = 64
N_Q_HEADS = 16
N_KV_HEADS = 4
GQA_GROUP = N_Q_HEADS // N_KV_HEADS
WINDOW = 128
ROPE_THETA = 10000.0
Q_WIDTH = N_Q_HEADS * HEAD_DIM
KV_WIDTH = N_KV_HEADS * HEAD_DIM
QKV_WIDTH = Q_WIDTH + 2 * KV_WIDTH
CHUNK = 128
GMLP_WIDTH = 2 * D_MODEL
N_SGU_GROUPS = 8
SGU_GROUP_DIM = GMLP_WIDTH // N_SGU_GROUPS
N_EXPERT_GROUPS = 4
EXPERTS_PER_GROUP = 8
N_EXPERTS = N_EXPERT_GROUPS * EXPERTS_PER_GROUP
EXPERT_FF = D_MODEL // 4
N_MOD = 6
DEEPNORM_ALPHA = (2.0 * DEPTH) ** 0.25
LN_EPS = 1e-5

LANES = 128
MOD_ROWS = 8
ROUTER_LANES = 128
EXPERT_COL0 = 8
SORT_CHUNK = 256
EXPERT_TILE = 512
SUB_ROWS = 256
VMEM_LIMIT = 56 * 1024 * 1024
SC_CORES = 2
SC_WORKERS = 32
SC_CHUNK = 64
SC_SLOTS = 3


def _cparams(n_axes, vmem=VMEM_LIMIT):
    return pltpu.CompilerParams(dimension_semantics=("arbitrary",) * n_axes, vmem_limit_bytes=vmem)


U32 = jnp.uint32
ROW_CHUNKS = D_MODEL // 2 // LANES


def _store_rows(ref, val, token0=0):
    n = val.shape[0]
    half = D_MODEL // 2
    words = pltpu.pack_elementwise([val[:, :half], val[:, half:]], packed_dtype=BF16)
    for c in range(ROW_CHUNKS):
        ref[pl.ds(token0 * ROW_CHUNKS + c, n, stride=ROW_CHUNKS), :] = words[:, c * LANES:(c + 1) * LANES]


def _load_rows(ref, n, token0=0):
    words = jnp.concatenate([ref[pl.ds(token0 * ROW_CHUNKS + c, n, stride=ROW_CHUNKS), :]
                             for c in range(ROW_CHUNKS)], axis=1)
    lo = lax.bitcast_convert_type(words << 16, F32)
    hi = lax.bitcast_convert_type(words & jnp.uint32(0xFFFF0000), F32)
    return jnp.concatenate([lo, hi], axis=1)


def _layer_norm(r, g, b, eps=LN_EPS):
    mu = jnp.mean(r, axis=-1, keepdims=True)
    d = r - mu
    var = jnp.mean(d * d, axis=-1, keepdims=True)
    return d * lax.rsqrt(var + eps) * g + b


def _deepnorm_ln(x, branch, gate_row, g, b):
    scale = (1.0 + gate_row) * (1.0 / DEEPNORM_ALPHA)
    return _layer_norm(x + scale * branch, g, b, eps=LN_EPS / (DEEPNORM_ALPHA * DEEPNORM_ALPHA))


def _mod_body(c_ref, w_ref, b_ref, o_ref):
    c = c_ref[...]
    ca = c * jax.nn.sigmoid(c)
    o_ref[...] = jnp.dot(ca.astype(BF16), w_ref[...].astype(BF16), preferred_element_type=F32) + b_ref[...]


def _adaln_mod(c_pad, ada_w, ada_b, layer):
    tn = 1536
    n_out = N_MOD * D_MODEL
    mod = pl.pallas_call(
        _mod_body,
        grid=(n_out // tn,),
        in_specs=[
            pl.BlockSpec((MOD_ROWS, D_MODEL), lambda j: (0, 0)),
            pl.BlockSpec((None, D_MODEL, tn), lambda j: (layer, 0, j)),
            pl.BlockSpec((None, 1, tn), lambda j: (layer, 0, j)),
        ],
        out_specs=pl.BlockSpec((MOD_ROWS, tn), lambda j: (0, j)),
        out_shape=jax.ShapeDtypeStruct((MOD_ROWS, n_out), F32),
        compiler_params=_cparams(1),
        name="adaln_mod",
    )(c_pad, ada_w, ada_b.reshape(DEPTH, 1, n_out))
    return mod.reshape(MOD_ROWS, 1, n_out)


def _qkv_body(x_ref, mod_ref, pos_ref, w_ref, b_ref, invf_ref, o_ref, wbf_ref):
    @pl.when(pl.program_id(0) == 0)
    def _():
        wbf_ref[...] = w_ref[...].astype(BF16)

    tm = x_ref.shape[0]
    sh = mod_ref[:, 0:D_MODEL]
    sc = mod_ref[:, D_MODEL:2 * D_MODEL]
    h = x_ref[...] * (1.0 + sc) + sh
    qkv = jnp.dot(h.astype(BF16), wbf_ref[...], preferred_element_type=F32) + b_ref[...]

    ang = invf_ref[...] * pos_ref[...].astype(F32)
    c = jnp.cos(ang)
    s = jnp.sin(ang)
    ct = jnp.concatenate([c, c, c, c], axis=0).T
    st = jnp.concatenate([-s, s, -s, s], axis=0).T
    lane = lax.broadcasted_iota(I32, (tm, LANES), 1)
    first_half = (lane & (HEAD_DIM // 2)) == 0
    n_rope = (Q_WIDTH + KV_WIDTH) // LANES
    for j in range(n_rope):
        blk = qkv[:, j * LANES:(j + 1) * LANES]
        rot = jnp.where(first_half, pltpu.roll(blk, LANES - HEAD_DIM // 2, 1), pltpu.roll(blk, HEAD_DIM // 2, 1))
        r = blk * ct + rot * st
        if j < Q_WIDTH // LANES:
            r = r * (HEAD_DIM ** -0.5)
        o_ref[:, j * LANES:(j + 1) * LANES] = r.astype(BF16)
    o_ref[:, Q_WIDTH + KV_WIDTH:] = qkv[:, Q_WIDTH + KV_WIDTH:].astype(BF16)


def _qkv_rope(x2, mod_rows, positions, w_qkv, b_qkv, seq):
    t = x2.shape[0]
    tm = 1024
    steps_per_batch = seq // tm
    inv_freq = ROPE_THETA ** (-jnp.arange(0, HEAD_DIM, 2, dtype=F32) / HEAD_DIM)
    return pl.pallas_call(
        _qkv_body,
        grid=(t // tm,),
        in_specs=[
            pl.BlockSpec((tm, D_MODEL), lambda i: (i, 0)),
            pl.BlockSpec((None, 1, N_MOD * D_MODEL), lambda i: (i // steps_per_batch, 0, 0)),
            pl.BlockSpec((None, 1, tm), lambda i: (i, 0, 0)),
            pl.BlockSpec((D_MODEL, QKV_WIDTH), lambda i: (0, 0)),
            pl.BlockSpec((1, QKV_WIDTH), lambda i: (0, 0)),
            pl.BlockSpec((HEAD_DIM // 2, 1), lambda i: (0, 0)),
        ],
        out_specs=pl.BlockSpec((tm, QKV_WIDTH), lambda i: (i, 0)),
        out_shape=jax.ShapeDtypeStruct((t, QKV_WIDTH), BF16),
        scratch_shapes=[pltpu.VMEM((D_MODEL, QKV_WIDTH), BF16)],
        compiler_params=_cparams(1),
        name="qkv_rope",
    )(x2, mod_rows, positions.reshape(t // tm, 1, tm), w_qkv, b_qkv.reshape(1, QKV_WIDTH),
      inv_freq.reshape(HEAD_DIM // 2, 1))


BF16_ROWS = 16


def _attn_prepare(kv, kab_ref, vx_ref, slot):
    kv = kv.astype(F32)
    low = lax.broadcasted_iota(I32, (WINDOW, LANES), 1) < HEAD_DIM
    ones = jnp.ones((WINDOW, LANES), F32)
    for g in range(N_KV_HEADS):
        for part, ref in ((0, None), (KV_WIDTH, vx_ref)):
            tile = kv[:, part + (g // 2) * LANES:part + (g // 2 + 1) * LANES]
            other = pltpu.roll(tile, HEAD_DIM, 1)
            in_low, in_high = (tile, other) if g % 2 == 0 else (other, tile)
            if ref is None:
                kab_ref[slot, 2 * g] = jnp.where(low, in_low, 0.0).astype(BF16)
                kab_ref[slot, 2 * g + 1] = jnp.where(low, 0.0, in_high).astype(BF16)
            else:
                both = jnp.where(low, in_low, in_high)
                vx_ref[slot, g] = jnp.concatenate([both, ones], axis=1).astype(BF16)


def _attn_block(sink_ref, q, kab_ref, vx_ref, s_ref, p_ref, prev, cur, first_block):
    for g in range(N_KV_HEADS):
        q_pair = jnp.concatenate([q[:, (2 * g) * LANES:(2 * g + 1) * LANES],
                                  q[:, (2 * g + 1) * LANES:(2 * g + 2) * LANES]], axis=0)
        for a in range(2):
            kband = jnp.concatenate([kab_ref[prev, 2 * g + a], kab_ref[cur, 2 * g + a]], axis=0)
            s = lax.dot_general(q_pair, kband, (((1,), (1,)), ((), ())), preferred_element_type=F32)
            s_ref[GQA_GROUP * g + a] = s[:WINDOW]
            s_ref[GQA_GROUP * g + 2 + a] = s[WINDOW:]
    qi = lax.broadcasted_iota(I32, (WINDOW, 2 * WINDOW), 0) + WINDOW
    kj = lax.broadcasted_iota(I32, (WINDOW, 2 * WINDOW), 1)
    mask = (kj <= qi) & (kj > qi - WINDOW) & ((kj >= WINDOW) | jnp.logical_not(first_block))
    key0 = lax.broadcasted_iota(I32, (1, 2 * WINDOW), 1) == 0
    for h in range(N_Q_HEADS):
        s = jnp.where(mask, s_ref[h], jnp.where(key0, sink_ref[h], -jnp.inf))
        m = jnp.max(s, axis=-1, keepdims=True)
        p_ref[h] = jnp.exp(s - m).astype(BF16)
    low = lax.broadcasted_iota(I32, (WINDOW, LANES), 1) < HEAD_DIM
    sink_row = ((lax.broadcasted_iota(I32, (BF16_ROWS, 2 * LANES), 0) == 0)
                & (lax.broadcasted_iota(I32, (BF16_ROWS, 2 * LANES), 1) < LANES))
    out_tiles = []
    for g in range(N_KV_HEADS):
        v_prev = vx_ref[prev, g]
        v_head = jnp.where(sink_row, 0.0, v_prev[:BF16_ROWS].astype(F32)).astype(BF16)
        vband = jnp.concatenate([v_head, v_prev[BF16_ROWS:], vx_ref[cur, g]], axis=0)
        p4 = p_ref[GQA_GROUP * g:GQA_GROUP * (g + 1)].reshape(GQA_GROUP * WINDOW, 2 * WINDOW)
        o4 = jnp.dot(p4, vband, preferred_element_type=F32)
        heads = []
        for j in range(GQA_GROUP):
            blk = o4[j * WINDOW:(j + 1) * WINDOW]
            heads.append(blk[:, :LANES] / blk[:, LANES:])
        out_tiles.append(jnp.where(low, heads[0], heads[1]))
        out_tiles.append(jnp.where(low, heads[2], heads[3]))
    return jnp.concatenate(out_tiles, axis=1).astype(BF16)


def _attn_body(sink_ref, q_ref, kv_ref, o_ref, kab_ref, vx_ref, s_ref, p_ref):
    n = pl.program_id(1)

    @pl.when(n == 0)
    def _():
        kab_ref[1] = jnp.zeros(kab_ref.shape[1:], kab_ref.dtype)
        half = (N_KV_HEADS, WINDOW, LANES)
        vx_ref[1] = jnp.concatenate([jnp.zeros(half, BF16), jnp.ones(half, BF16)], axis=-1)

    scratch = (kab_ref, vx_ref, s_ref, p_ref)
    for blk in range(q_ref.shape[0] // WINDOW):
        rows = slice(blk * WINDOW, (blk + 1) * WINDOW)
        cur = blk % 2
        _attn_prepare(kv_ref[rows, :], kab_ref, vx_ref, cur)
        o_ref[rows, :] = _attn_block(sink_ref, q_ref[rows, :], *scratch, 1 - cur, cur,
                                     (n == 0) if blk == 0 else False)


def _attention(qkv, sinks, batch, seq):
    t = qkv.shape[0]
    tq = 8 * WINDOW
    steps = seq // tq
    kv_col = Q_WIDTH // (2 * KV_WIDTH)
    return pl.pallas_call(
        _attn_body,
        grid=(batch, steps),
        in_specs=[
            pl.BlockSpec(memory_space=pltpu.SMEM),
            pl.BlockSpec((tq, Q_WIDTH), lambda b, n: (b * steps + n, 0)),
            pl.BlockSpec((tq, 2 * KV_WIDTH), lambda b, n: (b * steps + n, kv_col)),
        ],
        out_specs=pl.BlockSpec((tq, Q_WIDTH), lambda b, n: (b * steps + n, 0)),
        out_shape=jax.ShapeDtypeStruct((t, Q_WIDTH), BF16),
        scratch_shapes=[
            pltpu.VMEM((2, 2 * N_KV_HEADS, WINDOW, LANES), BF16),
            pltpu.VMEM((2, N_KV_HEADS, WINDOW, 2 * LANES), BF16),
            pltpu.VMEM((N_Q_HEADS, WINDOW, 2 * WINDOW), F32),
            pltpu.VMEM((N_Q_HEADS, WINDOW, 2 * WINDOW), BF16),
        ],
        compiler_params=_cparams(2),
        name="swa_attention",
    )(sinks, qkv, qkv)


def _route(lt):
    tm = lt.shape[1]
    row = lax.broadcasted_iota(I32, (EXPERTS_PER_GROUP, tm), 0)
    neg = -jnp.inf
    gl = jnp.where(row < N_EXPERT_GROUPS, lt[0:EXPERTS_PER_GROUP], neg)
    gm = jnp.max(gl, axis=0, keepdims=True)
    g_p = 1.0 / jnp.sum(jnp.exp(gl - gm), axis=0, keepdims=True)
    g_idx = jnp.min(jnp.where(gl == gm, row, EXPERTS_PER_GROUP), axis=0, keepdims=True)
    sel = lt[EXPERT_COL0 + (N_EXPERT_GROUPS - 1) * EXPERTS_PER_GROUP:EXPERT_COL0 + N_EXPERTS]
    for g in range(N_EXPERT_GROUPS - 2, -1, -1):
        lo = EXPERT_COL0 + g * EXPERTS_PER_GROUP
        sel = jnp.where(g_idx == g, lt[lo:lo + EXPERTS_PER_GROUP], sel)
    v1 = jnp.max(sel, axis=0, keepdims=True)
    i1 = jnp.min(jnp.where(sel == v1, row, EXPERTS_PER_GROUP), axis=0, keepdims=True)
    sel2 = jnp.where(row == i1, neg, sel)
    v2 = jnp.max(sel2, axis=0, keepdims=True)
    i2 = jnp.min(jnp.where(sel2 == v2, row, EXPERTS_PER_GROUP), axis=0, keepdims=True)
    e2 = jnp.exp(v2 - v1)
    w1 = g_p / (1.0 + e2)
    w2 = g_p * e2 / (1.0 + e2)
    base = g_idx * EXPERTS_PER_GROUP
    return base + i1, base + i2, w1, w2


def _proj_body(o_ref, x_ref, mod_ref, w_ref, b_ref, lng_ref, lnb_ref, wr_ref, br_ref,
               x1_ref, h2_ref, ids_ref, wts_ref, wbf_ref):
    @pl.when(pl.program_id(0) == 0)
    def _():
        wbf_ref[...] = w_ref[...].astype(BF16)

    d = D_MODEL
    gt_m = mod_ref[:, 2 * d:3 * d]
    sh_f = mod_ref[:, 3 * d:4 * d]
    sc_f = mod_ref[:, 4 * d:5 * d]
    wr = wr_ref[...].astype(BF16)
    for r in range(x_ref.shape[0] // SUB_ROWS):
        rows = slice(r * SUB_ROWS, (r + 1) * SUB_ROWS)
        y = jnp.dot(o_ref[rows, :], wbf_ref[...], preferred_element_type=F32) + b_ref[...]
        x1 = _deepnorm_ln(x_ref[rows, :], y, gt_m, lng_ref[...], lnb_ref[...])
        x1_ref[rows, :] = x1
        h2 = x1 * (1.0 + sc_f) + sh_f
        _store_rows(h2_ref, h2, r * SUB_ROWS)
        logits = jnp.dot(h2.astype(BF16), wr, preferred_element_type=F32) + br_ref[...]
        ea, eb, wa, wb = _route(logits.T)
        ids_ref[0:1, rows] = ea
        ids_ref[1:2, rows] = eb
        wts_ref[0:1, rows] = wa
        wts_ref[1:2, rows] = wb


def _proj_ln_router(o, x2, mod_rows, w, b, ln_g, ln_b, w_router, b_router, seq):
    t, k = o.shape
    tm = 1024
    steps_per_batch = seq // tm
    d = D_MODEL
    return pl.pallas_call(
        _proj_body,
        grid=(t // tm,),
        in_specs=[
            pl.BlockSpec((tm, k), lambda i: (i, 0)),
            pl.BlockSpec((tm, d), lambda i: (i, 0)),
            pl.BlockSpec((None, 1, N_MOD * d), lambda i: (i // steps_per_batch, 0, 0)),
            pl.BlockSpec((k, d), lambda i: (0, 0), pipeline_mode=pl.Buffered(1)),
            pl.BlockSpec((1, d), lambda i: (0, 0)),
            pl.BlockSpec((1, d), lambda i: (0, 0)),
            pl.BlockSpec((1, d), lambda i: (0, 0)),
            pl.BlockSpec((d, ROUTER_LANES), lambda i: (0, 0)),
            pl.BlockSpec((1, ROUTER_LANES), lambda i: (0, 0)),
        ],
        out_specs=[
            pl.BlockSpec((tm, d), lambda i: (i, 0)),
            pl.BlockSpec((tm * ROW_CHUNKS, LANES), lambda i: (i, 0)),
            pl.BlockSpec((2, tm), lambda i: (0, i)),
            pl.BlockSpec((2, tm), lambda i: (0, i)),
        ],
        out_shape=[
            jax.ShapeDtypeStruct((t, d), F32),
            jax.ShapeDtypeStruct((t * ROW_CHUNKS, LANES), U32),
            jax.ShapeDtypeStruct((2, t), I32),
            jax.ShapeDtypeStruct((2, t), F32),
        ],
        scratch_shapes=[pltpu.VMEM((k, d), BF16)],
        compiler_params=_cparams(1),
        name="proj_ln_router",
    )(o, x2, mod_rows, w, b.reshape(1, d), ln_g.reshape(1, d), ln_b.reshape(1, d), w_router, b_router)


def _sort_body(ids_ref, pos_ref, te_ref, nused_ref, tv_ref, wplan_ref, rank_ref):
    n_rows = ids_ref.shape[0]
    c = SORT_CHUNK
    erow = lax.broadcasted_iota(I32, (N_EXPERTS, c), 0)
    tri = (lax.broadcasted_iota(I32, (c, c), 0) <= lax.broadcasted_iota(I32, (c, c), 1)).astype(BF16)

    def rank_step(r, carry):
        onehot = erow == ids_ref[pl.ds(r, 1), :]
        pref = jnp.dot(onehot.astype(BF16), tri, preferred_element_type=F32)
        rank = jnp.sum(jnp.where(onehot, pref + carry, 0.0), axis=0, keepdims=True) - 1.0
        rank_ref[pl.ds(r, 1), :] = rank
        return carry + pref[:, c - 1:c]

    counts = lax.fori_loop(0, n_rows, rank_step, jnp.zeros((N_EXPERTS, 1), F32), unroll=8)
    n_tile = jnp.floor((counts + (EXPERT_TILE - 1)) * (1.0 / EXPERT_TILE))
    low = (lax.broadcasted_iota(I32, (N_EXPERTS, N_EXPERTS), 1)
           <= lax.broadcasted_iota(I32, (N_EXPERTS, N_EXPERTS), 0)).astype(BF16)
    cum = jnp.dot(low, jnp.broadcast_to(n_tile, (N_EXPERTS, LANES)).astype(BF16),
                  preferred_element_type=F32)[:, 0:1]
    row_off = (cum - n_tile) * EXPERT_TILE

    def pos_step(r, _):
        onehot = erow == ids_ref[pl.ds(r, 1), :]
        off = jnp.sum(jnp.where(onehot, row_off, 0.0), axis=0, keepdims=True)
        pos_ref[pl.ds(r, 1), :] = (off + rank_ref[pl.ds(r, 1), :]).astype(I32)
        return 0

    lax.fori_loop(0, n_rows, pos_step, 0, unroll=8)
    total = jnp.max(cum, axis=0, keepdims=True)
    n_lanes = te_ref.shape[1]
    tile = jnp.minimum(lax.broadcasted_iota(I32, (N_EXPERTS, n_lanes), 1).astype(F32), total - 1.0)
    te_ref[...] = jnp.sum(jnp.where(cum <= tile, 1.0, 0.0), axis=0, keepdims=True).astype(I32)
    nused_ref[...] = jnp.broadcast_to(total, nused_ref.shape).astype(I32)
    tile_f = lax.broadcasted_iota(I32, (N_EXPERTS, n_lanes), 1).astype(F32)
    first = cum - n_tile
    rows_left = jnp.clip(counts - (tile_f - first) * EXPERT_TILE, 0.0, float(EXPERT_TILE))
    owns = (first <= tile_f) & (tile_f < cum)
    tv_ref[...] = jnp.sum(jnp.where(owns, rows_left, 0.0), axis=0, keepdims=True).astype(I32)
    def lane_sum(x):
        return jnp.sum(x, axis=0, keepdims=True)

    is_first = lane_sum(jnp.where(owns & (tile_f == first), 1.0, 0.0))
    group_end = lane_sum(jnp.where(owns, cum, 0.0))
    next_expert = jnp.where(group_end < total, lane_sum(jnp.where(cum <= group_end, 1.0, 0.0)), -1.0)
    ordinal = lane_sum(jnp.where((cum <= tile_f[0:1]) & (n_tile > 0.0), 1.0, 0.0))
    slot = ordinal - 2.0 * jnp.floor(ordinal * 0.5)
    wplan_ref[0:1, :] = is_first.astype(I32)
    wplan_ref[1:2, :] = next_expert.astype(I32)
    wplan_ref[2:3, :] = slot.astype(I32)
    wplan_ref[3:8, :] = jnp.zeros((5, n_lanes), I32)


def _expert_sort(ids, n_tiles):
    n_assign = ids.shape[0] * ids.shape[1]
    n_rows = n_assign // SORT_CHUNK
    te_lanes = -(-n_tiles // LANES) * LANES
    pos, te, nused, tv, wplan = pl.pallas_call(
        _sort_body,
        grid=(1,),
        in_specs=[pl.BlockSpec((n_rows, SORT_CHUNK), lambda i: (0, 0))],
        out_specs=[
            pl.BlockSpec((n_rows, SORT_CHUNK), lambda i: (0, 0)),
            pl.BlockSpec((1, te_lanes), lambda i: (0, 0)),
            pl.BlockSpec((1, LANES), lambda i: (0, 0)),
            pl.BlockSpec((1, te_lanes), lambda i: (0, 0)),
            pl.BlockSpec((8, te_lanes), lambda i: (0, 0)),
        ],
        out_shape=[
            jax.ShapeDtypeStruct((n_rows, SORT_CHUNK), I32),
            jax.ShapeDtypeStruct((1, te_lanes), I32),
            jax.ShapeDtypeStruct((1, LANES), I32),
            jax.ShapeDtypeStruct((1, te_lanes), I32),
            jax.ShapeDtypeStruct((8, te_lanes), I32),
        ],
        scratch_shapes=[pltpu.VMEM((n_rows, SORT_CHUNK), F32)],
        compiler_params=_cparams(1),
        name="expert_sort",
    )(ids.reshape(n_rows, SORT_CHUNK))
    tile_plan = (te[0, :n_tiles], nused[0, :1], tv[0, :n_tiles],
                 wplan[0, :n_tiles], wplan[1, :n_tiles], wplan[2, :n_tiles])
    return pos, tile_plan


def _sc_mesh():
    return plsc.VectorSubcoreMesh(core_axis_name="c", subcore_axis_name="s", num_cores=SC_CORES,
                                  num_subcores=SC_WORKERS // SC_CORES)


def _sc_ring(n_chunks, read, write):
    reads, writes = {}, {}
    for j in range(min(SC_SLOTS - 1, n_chunks)):
        reads[j] = read(j)
    for j in range(n_chunks):
        for cp in reads.pop(j):
            cp.wait()
        nxt = j + SC_SLOTS - 1
        if nxt < n_chunks:
            for cp in writes.pop(nxt - SC_SLOTS, []):
                cp.wait()
            reads[nxt] = read(nxt)
        writes[j] = write(j)
    for cps in writes.values():
        for cp in cps:
            cp.wait()


def _sc_scatter_rows(src, pos2, n_rows):
    t = pos2.shape[1]
    src3 = src.reshape(t, ROW_CHUNKS, LANES)
    per_worker = t // SC_WORKERS
    n_chunks = per_worker // SC_CHUNK
    idx = pos2.reshape(2, SC_WORKERS, n_chunks, SC_CHUNK).transpose(1, 0, 2, 3)
    idx = idx.reshape(SC_WORKERS, 2 * n_chunks, SC_CHUNK)

    @functools.partial(
        pl.kernel, mesh=_sc_mesh(),
        out_type=jax.ShapeDtypeStruct((n_rows, ROW_CHUNKS, LANES), src.dtype),
        scratch_types=[
            pltpu.VMEM((2 * n_chunks, SC_CHUNK), I32),
            pltpu.VMEM((SC_SLOTS, SC_CHUNK, ROW_CHUNKS, LANES), src.dtype),
            pltpu.SemaphoreType.DMA((SC_SLOTS,)),
            pltpu.SemaphoreType.DMA((SC_SLOTS,)),
        ],
        name="sc_scatter_rows",
    )
    def scatter(src_hbm, idx_hbm, out_hbm, idx_v, rows_v, rsem, wsem):
        wid = lax.axis_index("s") * SC_CORES + lax.axis_index("c")
        pltpu.sync_copy(idx_hbm.at[wid], idx_v)

        def read(j):
            b = j % SC_SLOTS
            return [pltpu.async_copy(src_hbm.at[pl.ds(wid * per_worker + j * SC_CHUNK, SC_CHUNK)], rows_v.at[b],
                                     rsem.at[b])]

        def write(j):
            b = j % SC_SLOTS
            return [pltpu.async_copy(rows_v.at[b], out_hbm.at[idx_v.at[k * n_chunks + j]], wsem.at[b])
                    for k in range(2)]

        _sc_ring(n_chunks, read, write)

    return scatter(src3, idx).reshape(n_rows * ROW_CHUNKS, LANES)


def _sc_gather_rows(table, idx):
    n = idx.shape[0]
    table3 = table.reshape(-1, ROW_CHUNKS, LANES)
    per_worker = n // SC_WORKERS
    n_chunks = per_worker // SC_CHUNK

    @functools.partial(
        pl.kernel, mesh=_sc_mesh(),
        out_type=jax.ShapeDtypeStruct((n, ROW_CHUNKS, LANES), table.dtype),
        scratch_types=[
            pltpu.VMEM((n_chunks, SC_CHUNK), I32),
            pltpu.VMEM((SC_SLOTS, SC_CHUNK, ROW_CHUNKS, LANES), table.dtype),
            pltpu.SemaphoreType.DMA((SC_SLOTS,)),
            pltpu.SemaphoreType.DMA((SC_SLOTS,)),
        ],
        name="sc_gather_rows",
    )
    def gather(table_hbm, idx_hbm, out_hbm, idx_v, rows_v, rsem, wsem):
        wid = lax.axis_index("s") * SC_CORES + lax.axis_index("c")
        pltpu.sync_copy(idx_hbm.at[wid], idx_v)

        def read(j):
            b = j % SC_SLOTS
            return [pltpu.async_copy(table_hbm.at[idx_v.at[j]], rows_v.at[b], rsem.at[b])]

        def write(j):
            b = j % SC_SLOTS
            return [pltpu.async_copy(rows_v.at[b], out_hbm.at[pl.ds(wid * per_worker + j * SC_CHUNK, SC_CHUNK)],
                                     wsem.at[b])]

        _sc_ring(n_chunks, read, write)

    out = gather(table3, idx.reshape(SC_WORKERS, n_chunks, SC_CHUNK))
    return out.reshape(n * ROW_CHUNKS, LANES)


XS_SLOTS = 4


def _expert_body(te_ref, nused_ref, tv_ref, first_ref, next_ref, slot_ref, xs_hbm, wgu_hbm, wd_hbm, ys_ref,
                 wgu_bf, wd_bf, xbuf, xsem, wgu_buf, wd_buf, wsem, *, e0):
    i = pl.program_id(0)
    n_used = nused_ref[0]
    used = i < n_used
    tile_rows = EXPERT_TILE * ROW_CHUNKS

    def weight_fetch(expert, slot):
        return (pltpu.make_async_copy(wgu_hbm.at[e0 + expert], wgu_buf.at[slot], wsem.at[0, slot]),
                pltpu.make_async_copy(wd_hbm.at[e0 + expert], wd_buf.at[slot], wsem.at[1, slot]))

    @pl.when((i == 0) & used)
    def _():
        for cp in weight_fetch(te_ref[0], slot_ref[0]):
            cp.start()

    @pl.when(used & (first_ref[i] == 1))
    def _():
        slot = slot_ref[i]
        for cp in weight_fetch(te_ref[i], slot):
            cp.wait()
        wgu_bf[...] = wgu_buf[slot].astype(BF16)
        wd_bf[...] = wd_buf[slot].astype(BF16)

        @pl.when(next_ref[i] >= 0)
        def _():
            for cp in weight_fetch(next_ref[i], 1 - slot):
                cp.start()


    def fetch(tile):
        slot = lax.rem(tile, XS_SLOTS)
        r0 = pl.multiple_of(tile * tile_rows, tile_rows)
        return pltpu.make_async_copy(xs_hbm.at[pl.ds(r0, tile_rows)], xbuf.at[slot], xsem.at[slot])

    ahead = XS_SLOTS - 1
    for tile in range(ahead):
        @pl.when((i == 0) & (tile < n_used))
        def _(tile=tile):
            fetch(tile).start()

    @pl.when(i + ahead < n_used)
    def _():
        fetch(i + ahead).start()

    @pl.when(used)
    def _():
        fetch(i).wait()
        live = lax.broadcasted_iota(I32, (EXPERT_TILE, 1), 0) < tv_ref[i]
        xs = jnp.where(live, _load_rows(xbuf.at[lax.rem(i, XS_SLOTS)], EXPERT_TILE), 0.0).astype(BF16)
        gu = jnp.dot(xs, wgu_bf[...], preferred_element_type=F32)
        gate = gu[:, :EXPERT_FF]
        up = gu[:, EXPERT_FF:]
        act = gate * jax.nn.sigmoid(gate) * up
        _store_rows(ys_ref, jnp.dot(act.astype(BF16), wd_bf[...], preferred_element_type=F32))

    @pl.when(jnp.logical_not(used))
    def _():
        ys_ref[...] = jnp.zeros(ys_ref.shape, ys_ref.dtype)


def _expert_mlp(tile_plan, xs, w_gate_up, w_down, layer):
    d = D_MODEL
    n_tiles = xs.shape[0] // (EXPERT_TILE * ROW_CHUNKS)
    f2 = 2 * EXPERT_FF
    tile_rows = EXPERT_TILE * ROW_CHUNKS
    grid_spec = pltpu.PrefetchScalarGridSpec(
        num_scalar_prefetch=len(tile_plan),
        grid=(n_tiles,),
        in_specs=[
            pl.BlockSpec(memory_space=pl.ANY),
            pl.BlockSpec(memory_space=pl.ANY),
            pl.BlockSpec(memory_space=pl.ANY),
        ],
        out_specs=pl.BlockSpec((tile_rows, LANES), lambda i, *plan: (i, 0)),
        scratch_shapes=[pltpu.VMEM((d, f2), BF16), pltpu.VMEM((EXPERT_FF, d), BF16),
                        pltpu.VMEM((XS_SLOTS, tile_rows, LANES), U32), pltpu.SemaphoreType.DMA((XS_SLOTS,)),
                        pltpu.VMEM((2, d, f2), F32), pltpu.VMEM((2, EXPERT_FF, d), F32),
                        pltpu.SemaphoreType.DMA((2, 2))],
    )
    return pl.pallas_call(
        functools.partial(_expert_body, e0=layer * N_EXPERTS),
        grid_spec=grid_spec,
        out_shape=jax.ShapeDtypeStruct(xs.shape, U32),
        compiler_params=_cparams(1),
        name="expert_mlp",
    )(*tile_plan, xs, w_gate_up.reshape(DEPTH * N_EXPERTS, d, f2),
      w_down.reshape(DEPTH * N_EXPERTS, EXPERT_FF, d))


def _moe_combine(wts_ref, x_ref, mod_ref, lng_ref, lnb_ref, ya_ref, yb_ref, token0, n):
    d = D_MODEL
    rows = slice(token0, token0 + n)
    cols = []
    for k in range(2):
        wt = jnp.broadcast_to(wts_ref[k:k + 1, rows], (LANES, n)).T
        cols.append(jnp.concatenate([wt] * (d // LANES), axis=1))
    y = cols[0] * _load_rows(ya_ref, n, token0) + cols[1] * _load_rows(yb_ref, n, token0)
    gt_f = mod_ref[:, 5 * d:6 * d]
    return _deepnorm_ln(x_ref[rows, :], y, gt_f, lng_ref[...], lnb_ref[...])


def _combine_specs(tm, n_steps, steps_per_batch):
    d = D_MODEL
    return [
        pl.BlockSpec((2, tm), lambda i: (0, i)),
        pl.BlockSpec((tm, d), lambda i: (i, 0)),
        pl.BlockSpec((None, 1, N_MOD * d), lambda i: (i // steps_per_batch, 0, 0)),
        pl.BlockSpec((1, d), lambda i: (0, 0)),
        pl.BlockSpec((1, d), lambda i: (0, 0)),
        pl.BlockSpec((tm * ROW_CHUNKS, LANES), lambda i: (i, 0)),
        pl.BlockSpec((tm * ROW_CHUNKS, LANES), lambda i: (n_steps + i, 0)),
    ]


def _gather_rows_of_tokens(ys, pos2):
    return _sc_gather_rows(ys, pos2.reshape(-1))


def _combine_body(wts_ref, x_ref, mod_ref, lng_ref, lnb_ref, ya_ref, yb_ref, o_ref):
    for r in range(x_ref.shape[0] // SUB_ROWS):
        o_ref[r * SUB_ROWS:(r + 1) * SUB_ROWS, :] = _moe_combine(wts_ref, x_ref, mod_ref, lng_ref, lnb_ref, ya_ref,
                                                                 yb_ref, r * SUB_ROWS, SUB_ROWS)


def _combine_ln(moe, mod_rows, ln_g, ln_b, seq):
    wts, x1, ys, pos2 = moe
    yg = _gather_rows_of_tokens(ys, pos2)
    t, d = x1.shape
    tm = 1024
    n_steps = t // tm
    return pl.pallas_call(
        _combine_body,
        grid=(n_steps,),
        in_specs=_combine_specs(tm, n_steps, seq // tm),
        out_specs=pl.BlockSpec((tm, d), lambda i: (i, 0)),
        out_shape=jax.ShapeDtypeStruct((t, d), F32),
        compiler_params=_cparams(1),
        name="moe_combine_ln",
    )(wts, x1, mod_rows, ln_g.reshape(1, d), ln_b.reshape(1, d), yg, yg)


def _gmlp_body(wts_ref, x1_ref, modp_ref, lng_ref, lnb_ref, ya_ref, yb_ref,
               mod_ref, w_ref, b_ref, g_ref, beta_ref, ws_ref, bs_ref, x2_ref, o_ref, ws_bf, w_bf):
    @pl.when(pl.program_id(0) == 0)
    def _():
        tri = lax.broadcasted_iota(I32, (CHUNK, CHUNK), 0) >= lax.broadcasted_iota(I32, (CHUNK, CHUNK), 1)
        for g in range(N_SGU_GROUPS):
            ws_bf[g] = jnp.where(tri, 0.5 * ws_ref[g], 0.0).astype(BF16)
        w_bf[...] = w_ref[...].astype(BF16)

    d = D_MODEL
    sh = mod_ref[:, 0:d]
    sc = mod_ref[:, d:2 * d]
    sub_rows = x1_ref.shape[0]
    for r in range(x1_ref.shape[0] // sub_rows):
        r0 = r * sub_rows
        x2 = _moe_combine(wts_ref, x1_ref, modp_ref, lng_ref, lnb_ref, ya_ref, yb_ref, r0, sub_rows)
        x2_ref[r0:r0 + sub_rows, :] = x2
        h = x2 * (1.0 + sc) + sh
        z = jnp.dot(h.astype(BF16), w_bf[...], preferred_element_type=F32) + b_ref[...]
        z2 = z * (1.0 + lax.erf(z * (2.0 ** -0.5)))
        u = z2[:, :GMLP_WIDTH]
        v = _layer_norm(z2[:, GMLP_WIDTH:], g_ref[...], beta_ref[...], eps=4.0 * LN_EPS).astype(BF16)
        for ci in range(sub_rows // CHUNK):
            rows = slice(ci * CHUNK, (ci + 1) * CHUNK)
            out_rows = slice(r0 + ci * CHUNK, r0 + (ci + 1) * CHUNK)
            for g in range(N_SGU_GROUPS):
                lanes = slice(g * SGU_GROUP_DIM, (g + 1) * SGU_GROUP_DIM)
                mixed = jnp.dot(ws_bf[g], v[rows, lanes], preferred_element_type=F32) + 0.5 * bs_ref[:, g:g + 1]
                o_ref[out_rows, lanes] = (u[rows, lanes] * mixed).astype(BF16)


def _gmlp_gate(moe, ln_g, ln_b, mod_prev, mod_rows, w_in, b_in, sgu_g, sgu_b, w_s, b_s, seq):
    wts, x1, ys, pos2 = moe
    yg = _gather_rows_of_tokens(ys, pos2)
    t, d = x1.shape
    tm = 512
    steps_per_batch = seq // tm
    n_steps = t // tm
    gw = GMLP_WIDTH
    return pl.pallas_call(
        _gmlp_body,
        grid=(n_steps,),
        in_specs=_combine_specs(tm, n_steps, steps_per_batch) + [
            pl.BlockSpec((None, 1, N_MOD * d), lambda i: (i // steps_per_batch, 0, 0)),
            pl.BlockSpec((d, 2 * gw), lambda i: (0, 0)),
            pl.BlockSpec((1, 2 * gw), lambda i: (0, 0)),
            pl.BlockSpec((1, gw), lambda i: (0, 0)),
            pl.BlockSpec((1, gw), lambda i: (0, 0)),
            pl.BlockSpec((N_SGU_GROUPS, CHUNK, CHUNK), lambda i: (0, 0, 0)),
            pl.BlockSpec((CHUNK, N_SGU_GROUPS), lambda i: (0, 0)),
        ],
        out_specs=[pl.BlockSpec((tm, d), lambda i: (i, 0)), pl.BlockSpec((tm, gw), lambda i: (i, 0))],
        out_shape=[jax.ShapeDtypeStruct((t, d), F32), jax.ShapeDtypeStruct((t, gw), BF16)],
        scratch_shapes=[pltpu.VMEM((N_SGU_GROUPS, CHUNK, CHUNK), BF16), pltpu.VMEM((d, 2 * gw), BF16)],
        compiler_params=_cparams(1),
        name="combine_gmlp_gate",
    )(wts, x1, mod_prev, ln_g.reshape(1, d), ln_b.reshape(1, d), yg, yg,
      mod_rows, w_in, b_in.reshape(1, 2 * gw), sgu_g.reshape(1, gw), sgu_b.reshape(1, gw), w_s, b_s.T)


def _router_params(w_group, b_group, w_expert, b_expert):
    def lanes(group_part, expert_part):
        rows = group_part.shape[0]
        gap = jnp.zeros((rows, EXPERT_COL0 - N_EXPERT_GROUPS), F32)
        tail = jnp.zeros((rows, ROUTER_LANES - EXPERT_COL0 - N_EXPERTS), F32)
        return jnp.concatenate([group_part, gap, expert_part, tail], axis=1)

    return lanes(w_group, w_expert), lanes(b_group[None, :], b_expert[None, :])


def _moe_experts(x1, h2, ids, wts, layer, w_gate_up, w_down):
    t = x1.shape[0]
    n_rows = 2 * t + N_EXPERTS * EXPERT_TILE
    n_tiles = n_rows // EXPERT_TILE
    pos, tile_plan = _expert_sort(ids, n_tiles)
    pos2 = pos.reshape(2, t)
    xs = _sc_scatter_rows(h2, pos2, n_rows)
    ys = _expert_mlp(tile_plan, xs, w_gate_up, w_down, layer)
    return wts, x1, ys, pos2


def kernel(x, c, positions, ada_w, ada_b, post_ln_g, post_ln_b, attn_w_qkv, attn_b_qkv, attn_sinks, attn_w_o, attn_b_o, gmlp_w_in, gmlp_b_in, gmlp_sgu_ln_g, gmlp_sgu_ln_b, gmlp_w_s, gmlp_b_s, gmlp_w_out, gmlp_b_out, moe_w_group_router, moe_b_group_router, moe_w_expert_router, moe_b_expert_router, moe_w_gate_up, moe_w_down):
    batch, seq, d = x.shape
    t = batch * seq
    assert d == D_MODEL and batch <= MOD_ROWS and ada_w.shape[0] == DEPTH == 2
    assert seq % 1024 == 0, "token tiles of 1024 rows must not straddle sequences"
    assert t % (SC_WORKERS * SC_CHUNK) == 0, "every SparseCore subcore moves whole chunks"
    x2 = x.reshape(t, d)
    c_pad = jnp.pad(c, ((0, MOD_ROWS - batch), (0, 0)))
    mods = [_adaln_mod(c_pad, ada_w, ada_b, layer) for layer in range(DEPTH)]

    moe = None
    for layer in range(DEPTH):
        j = layer // 2
        if layer % 2 == 0:
            if moe is not None:
                x2 = _combine_ln(moe, mods[layer - 1], post_ln_g[layer - 1, 1], post_ln_b[layer - 1, 1], seq)
            qkv = _qkv_rope(x2, mods[layer], positions, attn_w_qkv[j], attn_b_qkv[j], seq)
            mix = _attention(qkv, attn_sinks[j], batch, seq)
            w_out, b_out = attn_w_o[j], attn_b_o[j]
        else:
            x2, mix = _gmlp_gate(moe, post_ln_g[layer - 1, 1], post_ln_b[layer - 1, 1], mods[layer - 1], mods[layer],
                                 gmlp_w_in[j], gmlp_b_in[j], gmlp_sgu_ln_g[j], gmlp_sgu_ln_b[j],
                                 gmlp_w_s[j], gmlp_b_s[j], seq)
            w_out, b_out = gmlp_w_out[j], gmlp_b_out[j]
        w_router, b_router = _router_params(moe_w_group_router[layer], moe_b_group_router[layer],
                                            moe_w_expert_router[layer], moe_b_expert_router[layer])
        x1, h2, ids, wts = _proj_ln_router(mix, x2, mods[layer], w_out, b_out, post_ln_g[layer, 0],
                                           post_ln_b[layer, 0], w_router, b_router, seq)
        moe = _moe_experts(x1, h2, ids, wts, layer, moe_w_gate_up, moe_w_down)
    x2 = _combine_ln(moe, mods[DEPTH - 1], post_ln_g[DEPTH - 1, 1], post_ln_b[DEPTH - 1, 1], seq)
    return x2.reshape(batch, seq, d)
```

```python
import functools

import jax
import jax.numpy as jnp
from jax import lax
from jax.experimental import pallas as pl
from jax.experimental.pallas import tpu as pltpu
from jax.experimental.pallas import tpu_sc as plsc

F32 = jnp.float32
BF16 = jnp.bfloat16
I32 = jnp.int32

D_MODEL = 1024
DEPTH = 2
HEAD_DIM = 64
N_Q_HEADS = 16
N_KV_HEADS = 4
GQA_GROUP = N_Q_HEADS // N_KV_HEADS
WINDOW = 128
ROPE_THETA = 10000.0
Q_WIDTH = N_Q_HEADS * HEAD_DIM
KV_WIDTH = N_KV_HEADS * HEAD_DIM
QKV_WIDTH = Q_WIDTH + 2 * KV_WIDTH
CHUNK = 128
GMLP_WIDTH = 2 * D_MODEL
N_SGU_GROUPS = 8
SGU_GROUP_DIM = GMLP_WIDTH // N_SGU_GROUPS
N_EXPERT_GROUPS = 4
EXPERTS_PER_GROUP = 8
N_EXPERTS = N_EXPERT_GROUPS * EXPERTS_PER_GROUP
EXPERT_FF = D_MODEL // 4
N_MOD = 6
DEEPNORM_ALPHA = (2.0 * DEPTH) ** 0.25
LN_EPS = 1e-5

LANES = 128
MOD_ROWS = 8
ROUTER_LANES = 128
EXPERT_COL0 = 8
SORT_CHUNK = 256
EXPERT_TILE = 512
SUB_ROWS = 256
VMEM_LIMIT = 56 * 1024 * 1024
SC_CORES = 2
SC_WORKERS = 32
SC_CHUNK = 64
SC_SLOTS = 3


def _cparams(n_axes, vmem=VMEM_LIMIT):
    return pltpu.CompilerParams(dimension_semantics=("arbitrary",) * n_axes, vmem_limit_bytes=vmem)


U32 = jnp.uint32
ROW_CHUNKS = D_MODEL // 2 // LANES


def _store_rows(ref, val, token0=0):
    n = val.shape[0]
    half = D_MODEL // 2
    words = pltpu.pack_elementwise([val[:, :half], val[:, half:]], packed_dtype=BF16)
    for c in range(ROW_CHUNKS):
        ref[pl.ds(token0 * ROW_CHUNKS + c, n, stride=ROW_CHUNKS), :] = words[:, c * LANES:(c + 1) * LANES]


def _load_rows(ref, n, token0=0):
    words = jnp.concatenate([ref[pl.ds(token0 * ROW_CHUNKS + c, n, stride=ROW_CHUNKS), :]
                             for c in range(ROW_CHUNKS)], axis=1)
    lo = lax.bitcast_convert_type(words << 16, F32)
    hi = lax.bitcast_convert_type(words & jnp.uint32(0xFFFF0000), F32)
    return jnp.concatenate([lo, hi], axis=1)


def _layer_norm(r, g, b, eps=LN_EPS):
    mu = jnp.mean(r, axis=-1, keepdims=True)
    d = r - mu
    var = jnp.mean(d * d, axis=-1, keepdims=True)
    return d * lax.rsqrt(var + eps) * g + b


def _deepnorm_ln(x, branch, gate_row, g, b):
    scale = (1.0 + gate_row) * (1.0 / DEEPNORM_ALPHA)
    return _layer_norm(x + scale * branch, g, b, eps=LN_EPS / (DEEPNORM_ALPHA * DEEPNORM_ALPHA))


def _mod_body(c_ref, w_ref, b_ref, o_ref):
    c = c_ref[...]
    ca = c * jax.nn.sigmoid(c)
    o_ref[...] = jnp.dot(ca.astype(BF16), w_ref[...].astype(BF16), preferred_element_type=F32) + b_ref[...]


def _adaln_mod(c_pad, ada_w, ada_b, layer):
    tn = 1536
    n_out = N_MOD * D_MODEL
    mod = pl.pallas_call(
        _mod_body,
        grid=(n_out // tn,),
        in_specs=[
            pl.BlockSpec((MOD_ROWS, D_MODEL), lambda j: (0, 0)),
            pl.BlockSpec((None, D_MODEL, tn), lambda j: (layer, 0, j)),
            pl.BlockSpec((None, 1, tn), lambda j: (layer, 0, j)),
        ],
        out_specs=pl.BlockSpec((MOD_ROWS, tn), lambda j: (0, j)),
        out_shape=jax.ShapeDtypeStruct((MOD_ROWS, n_out), F32),
        compiler_params=_cparams(1),
        name="adaln_mod",
    )(c_pad, ada_w, ada_b.reshape(DEPTH, 1, n_out))
    return mod.reshape(MOD_ROWS, 1, n_out)


def _qkv_body(x_ref, mod_ref, pos_ref, w_ref, b_ref, invf_ref, o_ref, wbf_ref):
    @pl.when(pl.program_id(0) == 0)
    def _():
        wbf_ref[...] = w_ref[...].astype(BF16)

    tm = x_ref.shape[0]
    sh = mod_ref[:, 0:D_MODEL]
    sc = mod_ref[:, D_MODEL:2 * D_MODEL]
    h = x_ref[...] * (1.0 + sc) + sh
    qkv = jnp.dot(h.astype(BF16), wbf_ref[...], preferred_element_type=F32) + b_ref[...]

    ang = invf_ref[...] * pos_ref[...].astype(F32)
    c = jnp.cos(ang)
    s = jnp.sin(ang)
    ct = jnp.concatenate([c, c, c, c], axis=0).T
    st = jnp.concatenate([-s, s, -s, s], axis=0).T
    lane = lax.broadcasted_iota(I32, (tm, LANES), 1)
    first_half = (lane & (HEAD_DIM // 2)) == 0
    n_rope = (Q_WIDTH + KV_WIDTH) // LANES
    for j in range(n_rope):
        blk = qkv[:, j * LANES:(j + 1) * LANES]
        rot = jnp.where(first_half, pltpu.roll(blk, LANES - HEAD_DIM // 2, 1), pltpu.roll(blk, HEAD_DIM // 2, 1))
        r = blk * ct + rot * st
        if j < Q_WIDTH // LANES:
            r = r * (HEAD_DIM ** -0.5)
        o_ref[:, j * LANES:(j + 1) * LANES] = r.astype(BF16)
    o_ref[:, Q_WIDTH + KV_WIDTH:] = qkv[:, Q_WIDTH + KV_WIDTH:].astype(BF16)


def _qkv_rope(x2, mod_rows, positions, w_qkv, b_qkv, seq):
    t = x2.shape[0]
    tm = 1024
    steps_per_batch = seq // tm
    inv_freq = ROPE_THETA ** (-jnp.arange(0, HEAD_DIM, 2, dtype=F32) / HEAD_DIM)
    return pl.pallas_call(
        _qkv_body,
        grid=(t // tm,),
        in_specs=[
            pl.BlockSpec((tm, D_MODEL), lambda i: (i, 0)),
            pl.BlockSpec((None, 1, N_MOD * D_MODEL), lambda i: (i // steps_per_batch, 0, 0)),
            pl.BlockSpec((None, 1, tm), lambda i: (i, 0, 0)),
            pl.BlockSpec((D_MODEL, QKV_WIDTH), lambda i: (0, 0)),
            pl.BlockSpec((1, QKV_WIDTH), lambda i: (0, 0)),
            pl.BlockSpec((HEAD_DIM // 2, 1), lambda i: (0, 0)),
        ],
        out_specs=pl.BlockSpec((tm, QKV_WIDTH), lambda i: (i, 0)),
        out_shape=jax.ShapeDtypeStruct((t, QKV_WIDTH), BF16),
        scratch_shapes=[pltpu.VMEM((D_MODEL, QKV_WIDTH), BF16)],
        compiler_params=_cparams(1),
        name="qkv_rope",
    )(x2, mod_rows, positions.reshape(t // tm, 1, tm), w_qkv, b_qkv.reshape(1, QKV_WIDTH),
      inv_freq.reshape(HEAD_DIM // 2, 1))


BF16_ROWS = 16
BF16_ONE_PAIR = 0x3F803F80


def _attn_prepare(kv, kab_ref, vx_ref, slot):
    words = pltpu.bitcast(kv, U32)
    low = lax.broadcasted_iota(I32, (WINDOW // 2, LANES), 1) < HEAD_DIM
    zeros = jnp.zeros((WINDOW // 2, LANES), U32)
    ones = jnp.full((WINDOW // 2, LANES), BF16_ONE_PAIR, U32)
    for g in range(N_KV_HEADS):
        for part, ref in ((0, None), (KV_WIDTH, vx_ref)):
            tile = words[:, part + (g // 2) * LANES:part + (g // 2 + 1) * LANES]
            other = pltpu.roll(tile, HEAD_DIM, 1)
            in_low, in_high = (tile, other) if g % 2 == 0 else (other, tile)
            if ref is None:
                kab_ref[slot, 2 * g] = pltpu.bitcast(jnp.where(low, in_low, zeros), BF16)
                kab_ref[slot, 2 * g + 1] = pltpu.bitcast(jnp.where(low, zeros, in_high), BF16)
            else:
                both = jnp.where(low, in_low, in_high)
                vx_ref[slot, g] = pltpu.bitcast(jnp.concatenate([both, ones], axis=1), BF16)


def _attn_block(sink_ref, q, kab_ref, vx_ref, s_ref, p_ref, prev, cur, first_block):
    for g in range(N_KV_HEADS):
        q_pair = jnp.concatenate([q[:, (2 * g) * LANES:(2 * g + 1) * LANES],
                                  q[:, (2 * g + 1) * LANES:(2 * g + 2) * LANES]], axis=0)
        for a in range(2):
            kband = jnp.concatenate([kab_ref[prev, 2 * g + a], kab_ref[cur, 2 * g + a]], axis=0)
            s = lax.dot_general(q_pair, kband, (((1,), (1,)), ((), ())), preferred_element_type=F32)
            s_ref[GQA_GROUP * g + a] = s[:WINDOW]
            s_ref[GQA_GROUP * g + 2 + a] = s[WINDOW:]
    qi = lax.broadcasted_iota(I32, (WINDOW, 2 * WINDOW), 0) + WINDOW
    kj = lax.broadcasted_iota(I32, (WINDOW, 2 * WINDOW), 1)
    mask = (kj <= qi) & (kj > qi - WINDOW) & ((kj >= WINDOW) | jnp.logical_not(first_block))
    key0 = lax.broadcasted_iota(I32, (1, 2 * WINDOW), 1) == 0
    for h in range(N_Q_HEADS):
        s = jnp.where(mask, s_ref[h], jnp.where(key0, sink_ref[h], -jnp.inf))
        m = jnp.max(s, axis=-1, keepdims=True)
        p_ref[h] = jnp.exp(s - m).astype(BF16)
    low = lax.broadcasted_iota(I32, (WINDOW, LANES), 1) < HEAD_DIM
    sink_row = ((lax.broadcasted_iota(I32, (BF16_ROWS, 2 * LANES), 0) == 0)
                & (lax.broadcasted_iota(I32, (BF16_ROWS, 2 * LANES), 1) < LANES))
    out_tiles = []
    for g in range(N_KV_HEADS):
        v_prev = vx_ref[prev, g]
        v_head = jnp.where(sink_row, 0.0, v_prev[:BF16_ROWS].astype(F32)).astype(BF16)
        vband = jnp.concatenate([v_head, v_prev[BF16_ROWS:], vx_ref[cur, g]], axis=0)
        p4 = p_ref[GQA_GROUP * g:GQA_GROUP * (g + 1)].reshape(GQA_GROUP * WINDOW, 2 * WINDOW)
        o4 = jnp.dot(p4, vband, preferred_element_type=F32)
        heads = []
        for j in range(GQA_GROUP):
            blk = o4[j * WINDOW:(j + 1) * WINDOW]
            heads.append(blk[:, :LANES] / blk[:, LANES:])
        out_tiles.append(jnp.where(low, heads[0], heads[1]))
        out_tiles.append(jnp.where(low, heads[2], heads[3]))
    return jnp.concatenate(out_tiles, axis=1).astype(BF16)


def _attn_body(sink_ref, q_ref, kv_ref, o_ref, kab_ref, vx_ref, s_ref, p_ref):
    n = pl.program_id(1)

    @pl.when(n == 0)
    def _():
        kab_ref[1] = jnp.zeros(kab_ref.shape[1:], kab_ref.dtype)
        half = (N_KV_HEADS, WINDOW, LANES)
        vx_ref[1] = jnp.concatenate([jnp.zeros(half, BF16), jnp.ones(half, BF16)], axis=-1)

    scratch = (kab_ref, vx_ref, s_ref, p_ref)
    for blk in range(q_ref.shape[0] // WINDOW):
        rows = slice(blk * WINDOW, (blk + 1) * WINDOW)
        cur = blk % 2
        _attn_prepare(kv_ref[rows, :], kab_ref, vx_ref, cur)
        o_ref[rows, :] = _attn_block(sink_ref, q_ref[rows, :], *scratch, 1 - cur, cur,
                                     (n == 0) if blk == 0 else False)


def _attention(qkv, sinks, batch, seq):
    t = qkv.shape[0]
    tq = 8 * WINDOW
    steps = seq // tq
    kv_col = Q_WIDTH // (2 * KV_WIDTH)
    return pl.pallas_call(
        _attn_body,
        grid=(batch, steps),
        in_specs=[
            pl.BlockSpec(memory_space=pltpu.SMEM),
            pl.BlockSpec((tq, Q_WIDTH), lambda b, n: (b * steps + n, 0)),
            pl.BlockSpec((tq, 2 * KV_WIDTH), lambda b, n: (b * steps + n, kv_col)),
        ],
        out_specs=pl.BlockSpec((tq, Q_WIDTH), lambda b, n: (b * steps + n, 0)),
        out_shape=jax.ShapeDtypeStruct((t, Q_WIDTH), BF16),
        scratch_shapes=[
            pltpu.VMEM((2, 2 * N_KV_HEADS, WINDOW, LANES), BF16),
            pltpu.VMEM((2, N_KV_HEADS, WINDOW, 2 * LANES), BF16),
            pltpu.VMEM((N_Q_HEADS, WINDOW, 2 * WINDOW), F32),
            pltpu.VMEM((N_Q_HEADS, WINDOW, 2 * WINDOW), BF16),
        ],
        compiler_params=_cparams(2),
        name="swa_attention",
    )(sinks, qkv, qkv)


def _route(lt):
    tm = lt.shape[1]
    row = lax.broadcasted_iota(I32, (EXPERTS_PER_GROUP, tm), 0)
    neg = -jnp.inf
    gl = jnp.where(row < N_EXPERT_GROUPS, lt[0:EXPERTS_PER_GROUP], neg)
    gm = jnp.max(gl, axis=0, keepdims=True)
    g_p = 1.0 / jnp.sum(jnp.exp(gl - gm), axis=0, keepdims=True)
    g_idx = jnp.min(jnp.where(gl == gm, row, EXPERTS_PER_GROUP), axis=0, keepdims=True)
    sel = lt[EXPERT_COL0 + (N_EXPERT_GROUPS - 1) * EXPERTS_PER_GROUP:EXPERT_COL0 + N_EXPERTS]
    for g in range(N_EXPERT_GROUPS - 2, -1, -1):
        lo = EXPERT_COL0 + g * EXPERTS_PER_GROUP
        sel = jnp.where(g_idx == g, lt[lo:lo + EXPERTS_PER_GROUP], sel)
    v1 = jnp.max(sel, axis=0, keepdims=True)
    i1 = jnp.min(jnp.where(sel == v1, row, EXPERTS_PER_GROUP), axis=0, keepdims=True)
    sel2 = jnp.where(row == i1, neg, sel)
    v2 = jnp.max(sel2, axis=0, keepdims=True)
    i2 = jnp.min(jnp.where(sel2 == v2, row, EXPERTS_PER_GROUP), axis=0, keepdims=True)
    e2 = jnp.exp(v2 - v1)
    w1 = g_p / (1.0 + e2)
    w2 = g_p * e2 / (1.0 + e2)
    base = g_idx * EXPERTS_PER_GROUP
    return base + i1, base + i2, w1, w2


def _proj_body(o_ref, x_ref, mod_ref, w_ref, b_ref, lng_ref, lnb_ref, wr_ref, br_ref,
               x1_ref, h2_ref, ids_ref, wts_ref, wbf_ref):
    @pl.when(pl.program_id(0) == 0)
    def _():
        wbf_ref[...] = w_ref[...].astype(BF16)

    d = D_MODEL
    gt_m = mod_ref[:, 2 * d:3 * d]
    sh_f = mod_ref[:, 3 * d:4 * d]
    sc_f = mod_ref[:, 4 * d:5 * d]
    wr = wr_ref[...].astype(BF16)
    for r in range(x_ref.shape[0] // SUB_ROWS):
        rows = slice(r * SUB_ROWS, (r + 1) * SUB_ROWS)
        y = jnp.dot(o_ref[rows, :], wbf_ref[...], preferred_element_type=F32) + b_ref[...]
        x1 = _deepnorm_ln(x_ref[rows, :], y, gt_m, lng_ref[...], lnb_ref[...])
        x1_ref[rows, :] = x1
        h2 = x1 * (1.0 + sc_f) + sh_f
        _store_rows(h2_ref, h2, r * SUB_ROWS)
        logits = jnp.dot(h2.astype(BF16), wr, preferred_element_type=F32) + br_ref[...]
        ea, eb, wa, wb = _route(logits.T)
        ids_ref[0:1, rows] = ea
        ids_ref[1:2, rows] = eb
        wts_ref[0:1, rows] = wa
        wts_ref[1:2, rows] = wb


def _proj_ln_router(o, x2, mod_rows, w, b, ln_g, ln_b, w_router, b_router, seq):
    t, k = o.shape
    tm = 1024
    steps_per_batch = seq // tm
    d = D_MODEL
    return pl.pallas_call(
        _proj_body,
        grid=(t // tm,),
        in_specs=[
            pl.BlockSpec((tm, k), lambda i: (i, 0)),
            pl.BlockSpec((tm, d), lambda i: (i, 0)),
            pl.BlockSpec((None, 1, N_MOD * d), lambda i: (i // steps_per_batch, 0, 0)),
            pl.BlockSpec((k, d), lambda i: (0, 0), pipeline_mode=pl.Buffered(1)),
            pl.BlockSpec((1, d), lambda i: (0, 0)),
            pl.BlockSpec((1, d), lambda i: (0, 0)),
            pl.BlockSpec((1, d), lambda i: (0, 0)),
            pl.BlockSpec((d, ROUTER_LANES), lambda i: (0, 0)),
            pl.BlockSpec((1, ROUTER_LANES), lambda i: (0, 0)),
        ],
        out_specs=[
            pl.BlockSpec((tm, d), lambda i: (i, 0)),
            pl.BlockSpec((tm * ROW_CHUNKS, LANES), lambda i: (i, 0)),
            pl.BlockSpec((2, tm), lambda i: (0, i)),
            pl.BlockSpec((2, tm), lambda i: (0, i)),
        ],
        out_shape=[
            jax.ShapeDtypeStruct((t, d), F32),
            jax.ShapeDtypeStruct((t * ROW_CHUNKS, LANES), U32),
            jax.ShapeDtypeStruct((2, t), I32),
            jax.ShapeDtypeStruct((2, t), F32),
        ],
        scratch_shapes=[pltpu.VMEM((k, d), BF16)],
        compiler_params=_cparams(1),
        name="proj_ln_router",
    )(o, x2, mod_rows, w, b.reshape(1, d), ln_g.reshape(1, d), ln_b.reshape(1, d), w_router, b_router)


def _sort_body(ids_ref, pos_ref, te_ref, nused_ref, tv_ref, wplan_ref, rank_ref):
    n_rows = ids_ref.shape[0]
    c = SORT_CHUNK
    erow = lax.broadcasted_iota(I32, (N_EXPERTS, c), 0)
    tri = (lax.broadcasted_iota(I32, (c, c), 0) <= lax.broadcasted_iota(I32, (c, c), 1)).astype(BF16)

    def rank_step(r, carry):
        onehot = erow == ids_ref[pl.ds(r, 1), :]
        pref = jnp.dot(onehot.astype(BF16), tri, preferred_element_type=F32)
        rank = jnp.sum(jnp.where(onehot, pref + carry, 0.0), axis=0, keepdims=True) - 1.0
        rank_ref[pl.ds(r, 1), :] = rank
        return carry + pref[:, c - 1:c]

    counts = lax.fori_loop(0, n_rows, rank_step, jnp.zeros((N_EXPERTS, 1), F32), unroll=8)
    n_tile = jnp.floor((counts + (EXPERT_TILE - 1)) * (1.0 / EXPERT_TILE))
    low = (lax.broadcasted_iota(I32, (N_EXPERTS, N_EXPERTS), 1)
           <= lax.broadcasted_iota(I32, (N_EXPERTS, N_EXPERTS), 0)).astype(BF16)
    cum = jnp.dot(low, jnp.broadcast_to(n_tile, (N_EXPERTS, LANES)).astype(BF16),
                  preferred_element_type=F32)[:, 0:1]
    row_off = (cum - n_tile) * EXPERT_TILE

    def pos_step(r, _):
        onehot = erow == ids_ref[pl.ds(r, 1), :]
        off = jnp.sum(jnp.where(onehot, row_off, 0.0), axis=0, keepdims=True)
        pos_ref[pl.ds(r, 1), :] = (off + rank_ref[pl.ds(r, 1), :]).astype(I32)
        return 0

    lax.fori_loop(0, n_rows, pos_step, 0, unroll=8)
    total = jnp.max(cum, axis=0, keepdims=True)
    n_lanes = te_ref.shape[1]
    tile = jnp.minimum(lax.broadcasted_iota(I32, (N_EXPERTS, n_lanes), 1).astype(F32), total - 1.0)
    te_ref[...] = jnp.sum(jnp.where(cum <= tile, 1.0, 0.0), axis=0, keepdims=True).astype(I32)
    nused_ref[...] = jnp.broadcast_to(total, nused_ref.shape).astype(I32)
    tile_f = lax.broadcasted_iota(I32, (N_EXPERTS, n_lanes), 1).astype(F32)
    first = cum - n_tile
    rows_left = jnp.clip(counts - (tile_f - first) * EXPERT_TILE, 0.0, float(EXPERT_TILE))
    owns = (first <= tile_f) & (tile_f < cum)
    tv_ref[...] = jnp.sum(jnp.where(owns, rows_left, 0.0), axis=0, keepdims=True).astype(I32)
    def lane_sum(x):
        return jnp.sum(x, axis=0, keepdims=True)

    is_first = lane_sum(jnp.where(owns & (tile_f == first), 1.0, 0.0))
    group_end = lane_sum(jnp.where(owns, cum, 0.0))
    next_expert = jnp.where(group_end < total, lane_sum(jnp.where(cum <= group_end, 1.0, 0.0)), -1.0)
    ordinal = lane_sum(jnp.where((cum <= tile_f[0:1]) & (n_tile > 0.0), 1.0, 0.0))
    slot = ordinal - 2.0 * jnp.floor(ordinal * 0.5)
    wplan_ref[0:1, :] = is_first.astype(I32)
    wplan_ref[1:2, :] = next_expert.astype(I32)
    wplan_ref[2:3, :] = slot.astype(I32)
    wplan_ref[3:8, :] = jnp.zeros((5, n_lanes), I32)


def _expert_sort(ids, n_tiles):
    n_assign = ids.shape[0] * ids.shape[1]
    n_rows = n_assign // SORT_CHUNK
    te_lanes = -(-n_tiles // LANES) * LANES
    pos, te, nused, tv, wplan = pl.pallas_call(
        _sort_body,
        grid=(1,),
        in_specs=[pl.BlockSpec((n_rows, SORT_CHUNK), lambda i: (0, 0))],
        out_specs=[
            pl.BlockSpec((n_rows, SORT_CHUNK), lambda i: (0, 0)),
            pl.BlockSpec((1, te_lanes), lambda i: (0, 0)),
            pl.BlockSpec((1, LANES), lambda i: (0, 0)),
            pl.BlockSpec((1, te_lanes), lambda i: (0, 0)),
            pl.BlockSpec((8, te_lanes), lambda i: (0, 0)),
        ],
        out_shape=[
            jax.ShapeDtypeStruct((n_rows, SORT_CHUNK), I32),
            jax.ShapeDtypeStruct((1, te_lanes), I32),
            jax.ShapeDtypeStruct((1, LANES), I32),
            jax.ShapeDtypeStruct((1, te_lanes), I32),
            jax.ShapeDtypeStruct((8, te_lanes), I32),
        ],
        scratch_shapes=[pltpu.VMEM((n_rows, SORT_CHUNK), F32)],
        compiler_params=_cparams(1),
        name="expert_sort",
    )(ids.reshape(n_rows, SORT_CHUNK))
    tile_plan = (te[0, :n_tiles], nused[0, :1], tv[0, :n_tiles],
                 wplan[0, :n_tiles], wplan[1, :n_tiles], wplan[2, :n_tiles])
    return pos, tile_plan


def _sc_mesh():
    return plsc.VectorSubcoreMesh(core_axis_name="c", subcore_axis_name="s", num_cores=SC_CORES,
                                  num_subcores=SC_WORKERS // SC_CORES)


def _sc_ring(n_chunks, read, write):
    reads, writes = {}, {}
    for j in range(min(SC_SLOTS - 1, n_chunks)):
        reads[j] = read(j)
    for j in range(n_chunks):
        for cp in reads.pop(j):
            cp.wait()
        nxt = j + SC_SLOTS - 1
        if nxt < n_chunks:
            for cp in writes.pop(nxt - SC_SLOTS, []):
                cp.wait()
            reads[nxt] = read(nxt)
        writes[j] = write(j)
    for cps in writes.values():
        for cp in cps:
            cp.wait()


def _sc_scatter_rows(src, pos2, n_rows):
    t = pos2.shape[1]
    src3 = src.reshape(t, ROW_CHUNKS, LANES)
    per_worker = t // SC_WORKERS
    n_chunks = per_worker // SC_CHUNK
    idx = pos2.reshape(2, SC_WORKERS, n_chunks, SC_CHUNK).transpose(1, 0, 2, 3)
    idx = idx.reshape(SC_WORKERS, 2 * n_chunks, SC_CHUNK)

    @functools.partial(
        pl.kernel, mesh=_sc_mesh(),
        out_type=jax.ShapeDtypeStruct((n_rows, ROW_CHUNKS, LANES), src.dtype),
        scratch_types=[
            pltpu.VMEM((2 * n_chunks, SC_CHUNK), I32),
            pltpu.VMEM((SC_SLOTS, SC_CHUNK, ROW_CHUNKS, LANES), src.dtype),
            pltpu.SemaphoreType.DMA((SC_SLOTS,)),
            pltpu.SemaphoreType.DMA((SC_SLOTS,)),
        ],
        name="sc_scatter_rows",
    )
    def scatter(src_hbm, idx_hbm, out_hbm, idx_v, rows_v, rsem, wsem):
        wid = lax.axis_index("s") * SC_CORES + lax.axis_index("c")
        pltpu.sync_copy(idx_hbm.at[wid], idx_v)

        def read(j):
            b = j % SC_SLOTS
            return [pltpu.async_copy(src_hbm.at[pl.ds(wid * per_worker + j * SC_CHUNK, SC_CHUNK)], rows_v.at[b],
                                     rsem.at[b])]

        def write(j):
            b = j % SC_SLOTS
            return [pltpu.async_copy(rows_v.at[b], out_hbm.at[idx_v.at[k * n_chunks + j]], wsem.at[b])
                    for k in range(2)]

        _sc_ring(n_chunks, read, write)

    return scatter(src3, idx).reshape(n_rows * ROW_CHUNKS, LANES)


def _sc_gather_rows(table, idx):
    n = idx.shape[0]
    table3 = table.reshape(-1, ROW_CHUNKS, LANES)
    per_worker = n // SC_WORKERS
    n_chunks = per_worker // SC_CHUNK

    @functools.partial(
        pl.kernel, mesh=_sc_mesh(),
        out_type=jax.ShapeDtypeStruct((n, ROW_CHUNKS, LANES), table.dtype),
        scratch_types=[
            pltpu.VMEM((n_chunks, SC_CHUNK), I32),
            pltpu.VMEM((SC_SLOTS, SC_CHUNK, ROW_CHUNKS, LANES), table.dtype),
            pltpu.SemaphoreType.DMA((SC_SLOTS,)),
            pltpu.SemaphoreType.DMA((SC_SLOTS,)),
        ],
        name="sc_gather_rows",
    )
    def gather(table_hbm, idx_hbm, out_hbm, idx_v, rows_v, rsem, wsem):
        wid = lax.axis_index("s") * SC_CORES + lax.axis_index("c")
        pltpu.sync_copy(idx_hbm.at[wid], idx_v)

        def read(j):
            b = j % SC_SLOTS
            return [pltpu.async_copy(table_hbm.at[idx_v.at[j]], rows_v.at[b], rsem.at[b])]

        def write(j):
            b = j % SC_SLOTS
            return [pltpu.async_copy(rows_v.at[b], out_hbm.at[pl.ds(wid * per_worker + j * SC_CHUNK, SC_CHUNK)],
                                     wsem.at[b])]

        _sc_ring(n_chunks, read, write)

    out = gather(table3, idx.reshape(SC_WORKERS, n_chunks, SC_CHUNK))
    return out.reshape(n * ROW_CHUNKS, LANES)


XS_SLOTS = 4


def _expert_body(te_ref, nused_ref, tv_ref, first_ref, next_ref, slot_ref, xs_hbm, wgu_hbm, wd_hbm, ys_ref,
                 wgu_bf, wd_bf, xbuf, xsem, wgu_buf, wd_buf, wsem, *, e0):
    i = pl.program_id(0)
    n_used = nused_ref[0]
    used = i < n_used
    tile_rows = EXPERT_TILE * ROW_CHUNKS

    def weight_fetch(expert, slot):
        return (pltpu.make_async_copy(wgu_hbm.at[e0 + expert], wgu_buf.at[slot], wsem.at[0, slot]),
                pltpu.make_async_copy(wd_hbm.at[e0 + expert], wd_buf.at[slot], wsem.at[1, slot]))

    @pl.when((i == 0) & used)
    def _():
        for cp in weight_fetch(te_ref[0], slot_ref[0]):
            cp.start()

    @pl.when(used & (first_ref[i] == 1))
    def _():
        slot = slot_ref[i]
        for cp in weight_fetch(te_ref[i], slot):
            cp.wait()
        wgu_bf[...] = wgu_buf[slot].astype(BF16)
        wd_bf[...] = wd_buf[slot].astype(BF16)

        @pl.when(next_ref[i] >= 0)
        def _():
            for cp in weight_fetch(next_ref[i], 1 - slot):
                cp.start()


    def fetch(tile):
        slot = lax.rem(tile, XS_SLOTS)
        r0 = pl.multiple_of(tile * tile_rows, tile_rows)
        return pltpu.make_async_copy(xs_hbm.at[pl.ds(r0, tile_rows)], xbuf.at[slot], xsem.at[slot])

    ahead = XS_SLOTS - 1
    for tile in range(ahead):
        @pl.when((i == 0) & (tile < n_used))
        def _(tile=tile):
            fetch(tile).start()

    @pl.when(i + ahead < n_used)
    def _():
        fetch(i + ahead).start()

    @pl.when(used)
    def _():
        fetch(i).wait()
        live = lax.broadcasted_iota(I32, (EXPERT_TILE, 1), 0) < tv_ref[i]
        xs = jnp.where(live, _load_rows(xbuf.at[lax.rem(i, XS_SLOTS)], EXPERT_TILE), 0.0).astype(BF16)
        gu = jnp.dot(xs, wgu_bf[...], preferred_element_type=F32)
        gate = gu[:, :EXPERT_FF]
        up = gu[:, EXPERT_FF:]
        act = gate * jax.nn.sigmoid(gate) * up
        _store_rows(ys_ref, jnp.dot(act.astype(BF16), wd_bf[...], preferred_element_type=F32))

    @pl.when(jnp.logical_not(used))
    def _():
        ys_ref[...] = jnp.zeros(ys_ref.shape, ys_ref.dtype)


def _expert_mlp(tile_plan, xs, w_gate_up, w_down, layer):
    d = D_MODEL
    n_tiles = xs.shape[0] // (EXPERT_TILE * ROW_CHUNKS)
    f2 = 2 * EXPERT_FF
    tile_rows = EXPERT_TILE * ROW_CHUNKS
    grid_spec = pltpu.PrefetchScalarGridSpec(
        num_scalar_prefetch=len(tile_plan),
        grid=(n_tiles,),
        in_specs=[
            pl.BlockSpec(memory_space=pl.ANY),
            pl.BlockSpec(memory_space=pl.ANY),
            pl.BlockSpec(memory_space=pl.ANY),
        ],
        out_specs=pl.BlockSpec((tile_rows, LANES), lambda i, *plan: (i, 0)),
        scratch_shapes=[pltpu.VMEM((d, f2), BF16), pltpu.VMEM((EXPERT_FF, d), BF16),
                        pltpu.VMEM((XS_SLOTS, tile_rows, LANES), U32), pltpu.SemaphoreType.DMA((XS_SLOTS,)),
                        pltpu.VMEM((2, d, f2), F32), pltpu.VMEM((2, EXPERT_FF, d), F32),
                        pltpu.SemaphoreType.DMA((2, 2))],
    )
    return pl.pallas_call(
        functools.partial(_expert_body, e0=layer * N_EXPERTS),
        grid_spec=grid_spec,
        out_shape=jax.ShapeDtypeStruct(xs.shape, U32),
        compiler_params=_cparams(1),
        name="expert_mlp",
    )(*tile_plan, xs, w_gate_up.reshape(DEPTH * N_EXPERTS, d, f2),
      w_down.reshape(DEPTH * N_EXPERTS, EXPERT_FF, d))


def _moe_combine(wts_ref, x_ref, mod_ref, lng_ref, lnb_ref, ya_ref, yb_ref, token0, n):
    d = D_MODEL
    rows = slice(token0, token0 + n)
    cols = []
    for k in range(2):
        wt = jnp.broadcast_to(wts_ref[k:k + 1, rows], (LANES, n)).T
        cols.append(jnp.concatenate([wt] * (d // LANES), axis=1))
    y = cols[0] * _load_rows(ya_ref, n, token0) + cols[1] * _load_rows(yb_ref, n, token0)
    gt_f = mod_ref[:, 5 * d:6 * d]
    return _deepnorm_ln(x_ref[rows, :], y, gt_f, lng_ref[...], lnb_ref[...])


def _combine_specs(tm, n_steps, steps_per_batch):
    d = D_MODEL
    return [
        pl.BlockSpec((2, tm), lambda i: (0, i)),
        pl.BlockSpec((tm, d), lambda i: (i, 0)),
        pl.BlockSpec((None, 1, N_MOD * d), lambda i: (i // steps_per_batch, 0, 0)),
        pl.BlockSpec((1, d), lambda i: (0, 0)),
        pl.BlockSpec((1, d), lambda i: (0, 0)),
        pl.BlockSpec((tm * ROW_CHUNKS, LANES), lambda i: (i, 0)),
        pl.BlockSpec((tm * ROW_CHUNKS, LANES), lambda i: (n_steps + i, 0)),
    ]


def _gather_rows_of_tokens(ys, pos2):
    return _sc_gather_rows(ys, pos2.reshape(-1))


def _combine_body(wts_ref, x_ref, mod_ref, lng_ref, lnb_ref, ya_ref, yb_ref, o_ref):
    for r in range(x_ref.shape[0] // SUB_ROWS):
        o_ref[r * SUB_ROWS:(r + 1) * SUB_ROWS, :] = _moe_combine(wts_ref, x_ref, mod_ref, lng_ref, lnb_ref, ya_ref,
                                                                 yb_ref, r * SUB_ROWS, SUB_ROWS)


def _combine_ln(moe, mod_rows, ln_g, ln_b, seq):
    wts, x1, ys, pos2 = moe
    yg = _gather_rows_of_tokens(ys, pos2)
    t, d = x1.shape
    tm = 1024
    n_steps = t // tm
    return pl.pallas_call(
        _combine_body,
        grid=(n_steps,),
        in_specs=_combine_specs(tm, n_steps, seq // tm),
        out_specs=pl.BlockSpec((tm, d), lambda i: (i, 0)),
        out_shape=jax.ShapeDtypeStruct((t, d), F32),
        compiler_params=_cparams(1),
        name="moe_combine_ln",
    )(wts, x1, mod_rows, ln_g.reshape(1, d), ln_b.reshape(1, d), yg, yg)


def _gmlp_body(wts_ref, x1_ref, modp_ref, lng_ref, lnb_ref, ya_ref, yb_ref,
               mod_ref, w_ref, b_ref, g_ref, beta_ref, ws_ref, bs_ref, x2_ref, o_ref, ws_bf, w_bf):
    @pl.when(pl.program_id(0) == 0)
    def _():
        tri = lax.broadcasted_iota(I32, (CHUNK, CHUNK), 0) >= lax.broadcasted_iota(I32, (CHUNK, CHUNK), 1)
        for g in range(N_SGU_GROUPS):
            ws_bf[g] = jnp.where(tri, 0.5 * ws_ref[g], 0.0).astype(BF16)
        w_bf[...] = w_ref[...].astype(BF16)

    d = D_MODEL
    sh = mod_ref[:, 0:d]
    sc = mod_ref[:, d:2 * d]
    sub_rows = x1_ref.shape[0]
    for r in range(x1_ref.shape[0] // sub_rows):
        r0 = r * sub_rows
        x2 = _moe_combine(wts_ref, x1_ref, modp_ref, lng_ref, lnb_ref, ya_ref, yb_ref, r0, sub_rows)
        x2_ref[r0:r0 + sub_rows, :] = x2
        h = x2 * (1.0 + sc) + sh
        z = jnp.dot(h.astype(BF16), w_bf[...], preferred_element_type=F32) + b_ref[...]
        z2 = z * (1.0 + lax.erf(z * (2.0 ** -0.5)))
        u = z2[:, :GMLP_WIDTH]
        v = _layer_norm(z2[:, GMLP_WIDTH:], g_ref[...], beta_ref[...], eps=4.0 * LN_EPS).astype(BF16)
        for ci in range(sub_rows // CHUNK):
            rows = slice(ci * CHUNK, (ci + 1) * CHUNK)
            out_rows = slice(r0 + ci * CHUNK, r0 + (ci + 1) * CHUNK)
            for g in range(N_SGU_GROUPS):
                lanes = slice(g * SGU_GROUP_DIM, (g + 1) * SGU_GROUP_DIM)
                mixed = jnp.dot(ws_bf[g], v[rows, lanes], preferred_element_type=F32) + 0.5 * bs_ref[:, g:g + 1]
                o_ref[out_rows, lanes] = (u[rows, lanes] * mixed).astype(BF16)


def _gmlp_gate(moe, ln_g, ln_b, mod_prev, mod_rows, w_in, b_in, sgu_g, sgu_b, w_s, b_s, seq):
    wts, x1, ys, pos2 = moe
    yg = _gather_rows_of_tokens(ys, pos2)
    t, d = x1.shape
    tm = 512
    steps_per_batch = seq // tm
    n_steps = t // tm
    gw = GMLP_WIDTH
    return pl.pallas_call(
        _gmlp_body,
        grid=(n_steps,),
        in_specs=_combine_specs(tm, n_steps, steps_per_batch) + [
            pl.BlockSpec((None, 1, N_MOD * d), lambda i: (i // steps_per_batch, 0, 0)),
            pl.BlockSpec((d, 2 * gw), lambda i: (0, 0)),
            pl.BlockSpec((1, 2 * gw), lambda i: (0, 0)),
            pl.BlockSpec((1, gw), lambda i: (0, 0)),
            pl.BlockSpec((1, gw), lambda i: (0, 0)),
            pl.BlockSpec((N_SGU_GROUPS, CHUNK, CHUNK), lambda i: (0, 0, 0)),
            pl.BlockSpec((CHUNK, N_SGU_GROUPS), lambda i: (0, 0)),
        ],
        out_specs=[pl.BlockSpec((tm, d), lambda i: (i, 0)), pl.BlockSpec((tm, gw), lambda i: (i, 0))],
        out_shape=[jax.ShapeDtypeStruct((t, d), F32), jax.ShapeDtypeStruct((t, gw), BF16)],
        scratch_shapes=[pltpu.VMEM((N_SGU_GROUPS, CHUNK, CHUNK), BF16), pltpu.VMEM((d, 2 * gw), BF16)],
        compiler_params=_cparams(1),
        name="combine_gmlp_gate",
    )(wts, x1, mod_prev, ln_g.reshape(1, d), ln_b.reshape(1, d), yg, yg,
      mod_rows, w_in, b_in.reshape(1, 2 * gw), sgu_g.reshape(1, gw), sgu_b.reshape(1, gw), w_s, b_s.T)


def _router_params(w_group, b_group, w_expert, b_expert):
    def lanes(group_part, expert_part):
        rows = group_part.shape[0]
        gap = jnp.zeros((rows, EXPERT_COL0 - N_EXPERT_GROUPS), F32)
        tail = jnp.zeros((rows, ROUTER_LANES - EXPERT_COL0 - N_EXPERTS), F32)
        return jnp.concatenate([group_part, gap, expert_part, tail], axis=1)

    return lanes(w_group, w_expert), lanes(b_group[None, :], b_expert[None, :])


def _moe_experts(x1, h2, ids, wts, layer, w_gate_up, w_down):
    t = x1.shape[0]
    n_rows = 2 * t + N_EXPERTS * EXPERT_TILE
    n_tiles = n_rows // EXPERT_TILE
    pos, tile_plan = _expert_sort(ids, n_tiles)
    pos2 = pos.reshape(2, t)
    xs = _sc_scatter_rows(h2, pos2, n_rows)
    ys = _expert_mlp(tile_plan, xs, w_gate_up, w_down, layer)
    return wts, x1, ys, pos2


def kernel(x, c, positions, ada_w, ada_b, post_ln_g, post_ln_b, attn_w_qkv, attn_b_qkv, attn_sinks, attn_w_o, attn_b_o, gmlp_w_in, gmlp_b_in, gmlp_sgu_ln_g, gmlp_sgu_ln_b, gmlp_w_s, gmlp_b_s, gmlp_w_out, gmlp_b_out, moe_w_group_router, moe_b_group_router, moe_w_expert_router, moe_b_expert_router, moe_w_gate_up, moe_w_down):
    batch, seq, d = x.shape
    t = batch * seq
    assert d == D_MODEL and batch <= MOD_ROWS and ada_w.shape[0] == DEPTH == 2
    assert seq % 1024 == 0, "token tiles of 1024 rows must not straddle sequences"
    assert t % (SC_WORKERS * SC_CHUNK) == 0, "every SparseCore subcore moves whole chunks"
    x2 = x.reshape(t, d)
    c_pad = jnp.pad(c, ((0, MOD_ROWS - batch), (0, 0)))
    mods = [_adaln_mod(c_pad, ada_w, ada_b, layer) for layer in range(DEPTH)]

    moe = None
    for layer in range(DEPTH):
        j = layer // 2
        if layer % 2 == 0:
            if moe is not None:
                x2 = _combine_ln(moe, mods[layer - 1], post_ln_g[layer - 1, 1], post_ln_b[layer - 1, 1], seq)
            qkv = _qkv_rope(x2, mods[layer], positions, attn_w_qkv[j], attn_b_qkv[j], seq)
            mix = _attention(qkv, attn_sinks[j], batch, seq)
            w_out, b_out = attn_w_o[j], attn_b_o[j]
        else:
            x2, mix = _gmlp_gate(moe, post_ln_g[layer - 1, 1], post_ln_b[layer - 1, 1], mods[layer - 1], mods[layer],
                                 gmlp_w_in[j], gmlp_b_in[j], gmlp_sgu_ln_g[j], gmlp_sgu_ln_b[j],
                                 gmlp_w_s[j], gmlp_b_s[j], seq)
            w_out, b_out = gmlp_w_out[j], gmlp_b_out[j]
        w_router, b_router = _router_params(moe_w_group_router[layer], moe_b_group_router[layer],
                                            moe_w_expert_router[layer], moe_b_expert_router[layer])
        x1, h2, ids, wts = _proj_ln_router(mix, x2, mods[layer], w_out, b_out, post_ln_g[layer, 0],
                                           post_ln_b[layer, 0], w_router, b_router, seq)
        moe = _moe_experts(x1, h2, ids, wts, layer, moe_w_gate_up, moe_w_down)
    x2 = _combine_ln(moe, mods[DEPTH - 1], post_ln_g[DEPTH - 1, 1], post_ln_b[DEPTH - 1, 1], seq)
    return x2.reshape(batch, seq, d)
```

```python
import functools

import jax
import jax.numpy as jnp
from jax import lax
from jax.experimental import pallas as pl
from jax.experimental.pallas import tpu as pltpu
from jax.experimental.pallas import tpu_sc as plsc

F32 = jnp.float32
BF16 = jnp.bfloat16
I32 = jnp.int32

D_MODEL = 1024
DEPTH = 2
HEAD_DIM = 64
N_Q_HEADS = 16
N_KV_HEADS = 4
GQA_GROUP = N_Q_HEADS // N_KV_HEADS
WINDOW = 128
ROPE_THETA = 10000.0
Q_WIDTH = N_Q_HEADS * HEAD_DIM
KV_WIDTH = N_KV_HEADS * HEAD_DIM
QKV_WIDTH = Q_WIDTH + 2 * KV_WIDTH
CHUNK = 128
GMLP_WIDTH = 2 * D_MODEL
N_SGU_GROUPS = 8
SGU_GROUP_DIM = GMLP_WIDTH // N_SGU_GROUPS
N_EXPERT_GROUPS = 4
EXPERTS_PER_GROUP = 8
N_EXPERTS = N_EXPERT_GROUPS * EXPERTS_PER_GROUP
EXPERT_FF = D_MODEL // 4
N_MOD = 6
DEEPNORM_ALPHA = (2.0 * DEPTH) ** 0.25
LN_EPS = 1e-5

LANES = 128
MOD_ROWS = 8
ROUTER_LANES = 128
EXPERT_COL0 = 8
SORT_CHUNK = 256
EXPERT_TILE = 512
SUB_ROWS = 256
VMEM_LIMIT = 56 * 1024 * 1024
SC_CORES = 2
SC_WORKERS = 32
SC_CHUNK = 64
SC_SLOTS = 3


def _cparams(n_axes, vmem=VMEM_LIMIT):
    return pltpu.CompilerParams(dimension_semantics=("arbitrary",) * n_axes, vmem_limit_bytes=vmem)


U32 = jnp.uint32
ROW_CHUNKS = D_MODEL // 2 // LANES


def _store_rows(ref, val, token0=0):
    n = val.shape[0]
    half = D_MODEL // 2
    words = pltpu.pack_elementwise([val[:, :half], val[:, half:]], packed_dtype=BF16)
    for c in range(ROW_CHUNKS):
        ref[pl.ds(token0 * ROW_CHUNKS + c, n, stride=ROW_CHUNKS), :] = words[:, c * LANES:(c + 1) * LANES]


def _load_rows(ref, n, token0=0):
    words = jnp.concatenate([ref[pl.ds(token0 * ROW_CHUNKS + c, n, stride=ROW_CHUNKS), :]
                             for c in range(ROW_CHUNKS)], axis=1)
    lo = lax.bitcast_convert_type(words << 16, F32)
    hi = lax.bitcast_convert_type(words & jnp.uint32(0xFFFF0000), F32)
    return jnp.concatenate([lo, hi], axis=1)


def _layer_norm(r, g, b, eps=LN_EPS):
    mu = jnp.mean(r, axis=-1, keepdims=True)
    d = r - mu
    var = jnp.mean(d * d, axis=-1, keepdims=True)
    return d * lax.rsqrt(var + eps) * g + b


def _deepnorm_ln(x, branch, gate_row, g, b):
    scale = (1.0 + gate_row) * (1.0 / DEEPNORM_ALPHA)
    return _layer_norm(x + scale * branch, g, b, eps=LN_EPS / (DEEPNORM_ALPHA * DEEPNORM_ALPHA))


def _mod_body(c_ref, w_ref, b_ref, o_ref):
    c = c_ref[...]
    ca = c * jax.nn.sigmoid(c)
    mod = jnp.dot(ca.astype(BF16), w_ref[...].astype(BF16), preferred_element_type=F32) + b_ref[...]
    for r in range(o_ref.shape[0]):
        o_ref[r] = mod[r:r + 1]


def _adaln_mod(c_pad, ada_w, ada_b, layer, batch):
    tn = 1536
    n_out = N_MOD * D_MODEL
    return pl.pallas_call(
        _mod_body,
        grid=(n_out // tn,),
        in_specs=[
            pl.BlockSpec((MOD_ROWS, D_MODEL), lambda j: (0, 0)),
            pl.BlockSpec((None, D_MODEL, tn), lambda j: (layer, 0, j)),
            pl.BlockSpec((None, 1, tn), lambda j: (layer, 0, j)),
        ],
        out_specs=pl.BlockSpec((batch, 1, tn), lambda j: (0, 0, j)),
        out_shape=jax.ShapeDtypeStruct((batch, 1, n_out), F32),
        compiler_params=_cparams(1),
        name="adaln_mod",
    )(c_pad, ada_w, ada_b.reshape(DEPTH, 1, n_out))


def _qkv_body(x_ref, mod_ref, pos_ref, w_ref, b_ref, invf_ref, o_ref, wbf_ref):
    @pl.when(pl.program_id(0) == 0)
    def _():
        wbf_ref[...] = w_ref[...].astype(BF16)

    tm = x_ref.shape[0]
    sh = mod_ref[:, 0:D_MODEL]
    sc = mod_ref[:, D_MODEL:2 * D_MODEL]
    h = x_ref[...] * (1.0 + sc) + sh
    qkv = jnp.dot(h.astype(BF16), wbf_ref[...], preferred_element_type=F32) + b_ref[...]

    ang = invf_ref[...] * pos_ref[...].astype(F32)
    c = jnp.cos(ang)
    s = jnp.sin(ang)
    ct = jnp.concatenate([c, c, c, c], axis=0).T
    st = jnp.concatenate([-s, s, -s, s], axis=0).T
    lane = lax.broadcasted_iota(I32, (tm, LANES), 1)
    first_half = (lane & (HEAD_DIM // 2)) == 0
    n_rope = (Q_WIDTH + KV_WIDTH) // LANES
    for j in range(n_rope):
        blk = qkv[:, j * LANES:(j + 1) * LANES]
        rot = jnp.where(first_half, pltpu.roll(blk, LANES - HEAD_DIM // 2, 1), pltpu.roll(blk, HEAD_DIM // 2, 1))
        r = blk * ct + rot * st
        if j < Q_WIDTH // LANES:
            r = r * (HEAD_DIM ** -0.5)
        o_ref[:, j * LANES:(j + 1) * LANES] = r.astype(BF16)
    o_ref[:, Q_WIDTH + KV_WIDTH:] = qkv[:, Q_WIDTH + KV_WIDTH:].astype(BF16)


def _qkv_rope(x2, mod_rows, positions, w_qkv, b_qkv, seq):
    t = x2.shape[0]
    tm = 1024
    steps_per_batch = seq // tm
    inv_freq = ROPE_THETA ** (-jnp.arange(0, HEAD_DIM, 2, dtype=F32) / HEAD_DIM)
    return pl.pallas_call(
        _qkv_body,
        grid=(t // tm,),
        in_specs=[
            pl.BlockSpec((tm, D_MODEL), lambda i: (i, 0)),
            pl.BlockSpec((None, 1, N_MOD * D_MODEL), lambda i: (i // steps_per_batch, 0, 0)),
            pl.BlockSpec((None, 1, tm), lambda i: (i, 0, 0)),
            pl.BlockSpec((D_MODEL, QKV_WIDTH), lambda i: (0, 0)),
            pl.BlockSpec((1, QKV_WIDTH), lambda i: (0, 0)),
            pl.BlockSpec((HEAD_DIM // 2, 1), lambda i: (0, 0)),
        ],
        out_specs=pl.BlockSpec((tm, QKV_WIDTH), lambda i: (i, 0)),
        out_shape=jax.ShapeDtypeStruct((t, QKV_WIDTH), BF16),
        scratch_shapes=[pltpu.VMEM((D_MODEL, QKV_WIDTH), BF16)],
        compiler_params=_cparams(1),
        name="qkv_rope",
    )(x2, mod_rows, positions.reshape(t // tm, 1, tm), w_qkv, b_qkv.reshape(1, QKV_WIDTH),
      inv_freq.reshape(HEAD_DIM // 2, 1))


BF16_ROWS = 16
BF16_ONE_PAIR = 0x3F803F80


def _attn_prepare(kv, kab_ref, vx_ref, slot):
    words = pltpu.bitcast(kv, U32)
    low = lax.broadcasted_iota(I32, (WINDOW // 2, LANES), 1) < HEAD_DIM
    zeros = jnp.zeros((WINDOW // 2, LANES), U32)
    ones = jnp.full((WINDOW // 2, LANES), BF16_ONE_PAIR, U32)
    for g in range(N_KV_HEADS):
        for part, ref in ((0, None), (KV_WIDTH, vx_ref)):
            tile = words[:, part + (g // 2) * LANES:part + (g // 2 + 1) * LANES]
            other = pltpu.roll(tile, HEAD_DIM, 1)
            in_low, in_high = (tile, other) if g % 2 == 0 else (other, tile)
            if ref is None:
                kab_ref[slot, 2 * g] = pltpu.bitcast(jnp.where(low, in_low, zeros), BF16)
                kab_ref[slot, 2 * g + 1] = pltpu.bitcast(jnp.where(low, zeros, in_high), BF16)
            else:
                both = jnp.where(low, in_low, in_high)
                vx_ref[slot, g] = pltpu.bitcast(jnp.concatenate([both, ones], axis=1), BF16)


def _attn_block(sink_ref, q, kab_ref, vx_ref, s_ref, p_ref, prev, cur, first_block):
    for g in range(N_KV_HEADS):
        q_pair = jnp.concatenate([q[:, (2 * g) * LANES:(2 * g + 1) * LANES],
                                  q[:, (2 * g + 1) * LANES:(2 * g + 2) * LANES]], axis=0)
        for a in range(2):
            kband = jnp.concatenate([kab_ref[prev, 2 * g + a], kab_ref[cur, 2 * g + a]], axis=0)
            s = lax.dot_general(q_pair, kband, (((1,), (1,)), ((), ())), preferred_element_type=F32)
            s_ref[GQA_GROUP * g + a] = s[:WINDOW]
            s_ref[GQA_GROUP * g + 2 + a] = s[WINDOW:]
    qi = lax.broadcasted_iota(I32, (WINDOW, 2 * WINDOW), 0) + WINDOW
    kj = lax.broadcasted_iota(I32, (WINDOW, 2 * WINDOW), 1)
    mask = (kj <= qi) & (kj > qi - WINDOW) & ((kj >= WINDOW) | jnp.logical_not(first_block))
    key0 = lax.broadcasted_iota(I32, (1, 2 * WINDOW), 1) == 0
    for h in range(N_Q_HEADS):
        s = jnp.where(mask, s_ref[h], jnp.where(key0, sink_ref[h], -jnp.inf))
        m = jnp.max(s, axis=-1, keepdims=True)
        p_ref[h] = jnp.exp(s - m).astype(BF16)
    low = lax.broadcasted_iota(I32, (WINDOW, LANES), 1) < HEAD_DIM
    sink_row = ((lax.broadcasted_iota(I32, (BF16_ROWS, 2 * LANES), 0) == 0)
                & (lax.broadcasted_iota(I32, (BF16_ROWS, 2 * LANES), 1) < LANES))
    out_tiles = []
    for g in range(N_KV_HEADS):
        v_prev = vx_ref[prev, g]
        v_head = jnp.where(sink_row, 0.0, v_prev[:BF16_ROWS].astype(F32)).astype(BF16)
        vband = jnp.concatenate([v_head, v_prev[BF16_ROWS:], vx_ref[cur, g]], axis=0)
        p4 = p_ref[GQA_GROUP * g:GQA_GROUP * (g + 1)].reshape(GQA_GROUP * WINDOW, 2 * WINDOW)
        o4 = jnp.dot(p4, vband, preferred_element_type=F32)
        heads = []
        for j in range(GQA_GROUP):
            blk = o4[j * WINDOW:(j + 1) * WINDOW]
            heads.append(blk[:, :LANES] / blk[:, LANES:])
        out_tiles.append(jnp.where(low, heads[0], heads[1]))
        out_tiles.append(jnp.where(low, heads[2], heads[3]))
    return jnp.concatenate(out_tiles, axis=1).astype(BF16)


def _attn_body(sink_ref, q_ref, kv_ref, o_ref, kab_ref, vx_ref, s_ref, p_ref):
    n = pl.program_id(1)

    @pl.when(n == 0)
    def _():
        kab_ref[1] = jnp.zeros(kab_ref.shape[1:], kab_ref.dtype)
        half = (N_KV_HEADS, WINDOW, LANES)
        vx_ref[1] = jnp.concatenate([jnp.zeros(half, BF16), jnp.ones(half, BF16)], axis=-1)

    scratch = (kab_ref, vx_ref, s_ref, p_ref)
    for blk in range(q_ref.shape[0] // WINDOW):
        rows = slice(blk * WINDOW, (blk + 1) * WINDOW)
        cur = blk % 2
        _attn_prepare(kv_ref[rows, :], kab_ref, vx_ref, cur)
        o_ref[rows, :] = _attn_block(sink_ref, q_ref[rows, :], *scratch, 1 - cur, cur,
                                     (n == 0) if blk == 0 else False)


def _attention(qkv, sinks, batch, seq):
    t = qkv.shape[0]
    tq = 8 * WINDOW
    steps = seq // tq
    kv_col = Q_WIDTH // (2 * KV_WIDTH)
    return pl.pallas_call(
        _attn_body,
        grid=(batch, steps),
        in_specs=[
            pl.BlockSpec(memory_space=pltpu.SMEM),
            pl.BlockSpec((tq, Q_WIDTH), lambda b, n: (b * steps + n, 0)),
            pl.BlockSpec((tq, 2 * KV_WIDTH), lambda b, n: (b * steps + n, kv_col)),
        ],
        out_specs=pl.BlockSpec((tq, Q_WIDTH), lambda b, n: (b * steps + n, 0)),
        out_shape=jax.ShapeDtypeStruct((t, Q_WIDTH), BF16),
        scratch_shapes=[
            pltpu.VMEM((2, 2 * N_KV_HEADS, WINDOW, LANES), BF16),
            pltpu.VMEM((2, N_KV_HEADS, WINDOW, 2 * LANES), BF16),
            pltpu.VMEM((N_Q_HEADS, WINDOW, 2 * WINDOW), F32),
            pltpu.VMEM((N_Q_HEADS, WINDOW, 2 * WINDOW), BF16),
        ],
        compiler_params=_cparams(2),
        name="swa_attention",
    )(sinks, qkv, qkv)


def _route(lt):
    tm = lt.shape[1]
    row = lax.broadcasted_iota(I32, (EXPERTS_PER_GROUP, tm), 0)
    neg = -jnp.inf
    gl = jnp.where(row < N_EXPERT_GROUPS, lt[0:EXPERTS_PER_GROUP], neg)
    gm = jnp.max(gl, axis=0, keepdims=True)
    g_p = 1.0 / jnp.sum(jnp.exp(gl - gm), axis=0, keepdims=True)
    g_idx = jnp.min(jnp.where(gl == gm, row, EXPERTS_PER_GROUP), axis=0, keepdims=True)
    sel = lt[EXPERT_COL0 + (N_EXPERT_GROUPS - 1) * EXPERTS_PER_GROUP:EXPERT_COL0 + N_EXPERTS]
    for g in range(N_EXPERT_GROUPS - 2, -1, -1):
        lo = EXPERT_COL0 + g * EXPERTS_PER_GROUP
        sel = jnp.where(g_idx == g, lt[lo:lo + EXPERTS_PER_GROUP], sel)
    v1 = jnp.max(sel, axis=0, keepdims=True)
    i1 = jnp.min(jnp.where(sel == v1, row, EXPERTS_PER_GROUP), axis=0, keepdims=True)
    sel2 = jnp.where(row == i1, neg, sel)
    v2 = jnp.max(sel2, axis=0, keepdims=True)
    i2 = jnp.min(jnp.where(sel2 == v2, row, EXPERTS_PER_GROUP), axis=0, keepdims=True)
    e2 = jnp.exp(v2 - v1)
    w1 = g_p / (1.0 + e2)
    w2 = g_p * e2 / (1.0 + e2)
    base = g_idx * EXPERTS_PER_GROUP
    return base + i1, base + i2, w1, w2


def _proj_body(o_ref, x_ref, mod_ref, w_ref, b_ref, lng_ref, lnb_ref, wr_ref, br_ref,
               x1_ref, h2_ref, ids_ref, wts_ref, wbf_ref):
    @pl.when(pl.program_id(0) == 0)
    def _():
        wbf_ref[...] = w_ref[...].astype(BF16)

    d = D_MODEL
    gt_m = mod_ref[:, 2 * d:3 * d]
    sh_f = mod_ref[:, 3 * d:4 * d]
    sc_f = mod_ref[:, 4 * d:5 * d]
    wr = wr_ref[...].astype(BF16)
    for r in range(x_ref.shape[0] // SUB_ROWS):
        rows = slice(r * SUB_ROWS, (r + 1) * SUB_ROWS)
        y = jnp.dot(o_ref[rows, :], wbf_ref[...], preferred_element_type=F32) + b_ref[...]
        x1 = _deepnorm_ln(x_ref[rows, :], y, gt_m, lng_ref[...], lnb_ref[...])
        x1_ref[rows, :] = x1
        h2 = x1 * (1.0 + sc_f) + sh_f
        _store_rows(h2_ref, h2, r * SUB_ROWS)
        logits = jnp.dot(h2.astype(BF16), wr, preferred_element_type=F32) + br_ref[...]
        ea, eb, wa, wb = _route(logits.T)
        ids_ref[0:1, rows] = ea
        ids_ref[1:2, rows] = eb
        wts_ref[0:1, rows] = wa
        wts_ref[1:2, rows] = wb


def _proj_ln_router(o, x2, mod_rows, w, b, ln_g, ln_b, w_router, b_router, seq):
    t, k = o.shape
    tm = 1024
    steps_per_batch = seq // tm
    d = D_MODEL
    return pl.pallas_call(
        _proj_body,
        grid=(t // tm,),
        in_specs=[
            pl.BlockSpec((tm, k), lambda i: (i, 0)),
            pl.BlockSpec((tm, d), lambda i: (i, 0)),
            pl.BlockSpec((None, 1, N_MOD * d), lambda i: (i // steps_per_batch, 0, 0)),
            pl.BlockSpec((k, d), lambda i: (0, 0), pipeline_mode=pl.Buffered(1)),
            pl.BlockSpec((1, d), lambda i: (0, 0)),
            pl.BlockSpec((1, d), lambda i: (0, 0)),
            pl.BlockSpec((1, d), lambda i: (0, 0)),
            pl.BlockSpec((d, ROUTER_LANES), lambda i: (0, 0)),
            pl.BlockSpec((1, ROUTER_LANES), lambda i: (0, 0)),
        ],
        out_specs=[
            pl.BlockSpec((tm, d), lambda i: (i, 0)),
            pl.BlockSpec((tm * ROW_CHUNKS, LANES), lambda i: (i, 0)),
            pl.BlockSpec((2, tm), lambda i: (0, i)),
            pl.BlockSpec((2, tm), lambda i: (0, i)),
        ],
        out_shape=[
            jax.ShapeDtypeStruct((t, d), F32),
            jax.ShapeDtypeStruct((t * ROW_CHUNKS, LANES), U32),
            jax.ShapeDtypeStruct((2, t), I32),
            jax.ShapeDtypeStruct((2, t), F32),
        ],
        scratch_shapes=[pltpu.VMEM((k, d), BF16)],
        compiler_params=_cparams(1),
        name="proj_ln_router",
    )(o, x2, mod_rows, w, b.reshape(1, d), ln_g.reshape(1, d), ln_b.reshape(1, d), w_router, b_router)


PLAN_EXPERT, PLAN_USED, PLAN_VALID, PLAN_FIRST, PLAN_NEXT, PLAN_SLOT = range(6)
PLAN_ROWS = 8


def _sort_body(ids_ref, pos_ref, plan_ref, rank_ref):
    n_rows = ids_ref.shape[0]
    c = SORT_CHUNK
    erow = lax.broadcasted_iota(I32, (N_EXPERTS, c), 0)
    tri = (lax.broadcasted_iota(I32, (c, c), 0) <= lax.broadcasted_iota(I32, (c, c), 1)).astype(BF16)

    def rank_step(r, carry):
        onehot = erow == ids_ref[pl.ds(r, 1), :]
        pref = jnp.dot(onehot.astype(BF16), tri, preferred_element_type=F32)
        rank = jnp.sum(jnp.where(onehot, pref + carry, 0.0), axis=0, keepdims=True) - 1.0
        rank_ref[pl.ds(r, 1), :] = rank
        return carry + pref[:, c - 1:c]

    counts = lax.fori_loop(0, n_rows, rank_step, jnp.zeros((N_EXPERTS, 1), F32), unroll=8)
    n_tile = jnp.floor((counts + (EXPERT_TILE - 1)) * (1.0 / EXPERT_TILE))
    low = (lax.broadcasted_iota(I32, (N_EXPERTS, N_EXPERTS), 1)
           <= lax.broadcasted_iota(I32, (N_EXPERTS, N_EXPERTS), 0)).astype(BF16)
    cum = jnp.dot(low, jnp.broadcast_to(n_tile, (N_EXPERTS, LANES)).astype(BF16),
                  preferred_element_type=F32)[:, 0:1]
    row_off = (cum - n_tile) * EXPERT_TILE

    def pos_step(r, _):
        onehot = erow == ids_ref[pl.ds(r, 1), :]
        off = jnp.sum(jnp.where(onehot, row_off, 0.0), axis=0, keepdims=True)
        pos_ref[pl.ds(r, 1), :] = (off + rank_ref[pl.ds(r, 1), :]).astype(I32)
        return 0

    lax.fori_loop(0, n_rows, pos_step, 0, unroll=8)
    total = jnp.max(cum, axis=0, keepdims=True)
    n_lanes = plan_ref.shape[1]
    tile = jnp.minimum(lax.broadcasted_iota(I32, (N_EXPERTS, n_lanes), 1).astype(F32), total - 1.0)
    plan_ref[PLAN_EXPERT:PLAN_EXPERT + 1, :] = jnp.sum(jnp.where(cum <= tile, 1.0, 0.0), axis=0,
                                                       keepdims=True).astype(I32)
    plan_ref[PLAN_USED:PLAN_USED + 1, :] = jnp.broadcast_to(total, (1, n_lanes)).astype(I32)
    tile_f = lax.broadcasted_iota(I32, (N_EXPERTS, n_lanes), 1).astype(F32)
    first = cum - n_tile
    rows_left = jnp.clip(counts - (tile_f - first) * EXPERT_TILE, 0.0, float(EXPERT_TILE))
    owns = (first <= tile_f) & (tile_f < cum)
    plan_ref[PLAN_VALID:PLAN_VALID + 1, :] = jnp.sum(jnp.where(owns, rows_left, 0.0), axis=0,
                                                     keepdims=True).astype(I32)
    def lane_sum(x):
        return jnp.sum(x, axis=0, keepdims=True)

    is_first = lane_sum(jnp.where(owns & (tile_f == first), 1.0, 0.0))
    group_end = lane_sum(jnp.where(owns, cum, 0.0))
    next_expert = jnp.where(group_end < total, lane_sum(jnp.where(cum <= group_end, 1.0, 0.0)), -1.0)
    ordinal = lane_sum(jnp.where((cum <= tile_f[0:1]) & (n_tile > 0.0), 1.0, 0.0))
    slot = ordinal - 2.0 * jnp.floor(ordinal * 0.5)
    plan_ref[PLAN_FIRST:PLAN_FIRST + 1, :] = is_first.astype(I32)
    plan_ref[PLAN_NEXT:PLAN_NEXT + 1, :] = next_expert.astype(I32)
    plan_ref[PLAN_SLOT:PLAN_SLOT + 1, :] = slot.astype(I32)
    plan_ref[PLAN_SLOT + 1:PLAN_ROWS, :] = jnp.zeros((PLAN_ROWS - PLAN_SLOT - 1, n_lanes), I32)


def _expert_sort(ids, n_tiles):
    n_assign = ids.shape[0] * ids.shape[1]
    n_rows = n_assign // SORT_CHUNK
    te_lanes = -(-n_tiles // LANES) * LANES
    return pl.pallas_call(
        _sort_body,
        grid=(1,),
        in_specs=[pl.BlockSpec((n_rows, SORT_CHUNK), lambda i: (0, 0))],
        out_specs=[
            pl.BlockSpec((n_rows, SORT_CHUNK), lambda i: (0, 0)),
            pl.BlockSpec((PLAN_ROWS, te_lanes), lambda i: (0, 0)),
        ],
        out_shape=[
            jax.ShapeDtypeStruct((n_rows, SORT_CHUNK), I32),
            jax.ShapeDtypeStruct((PLAN_ROWS, te_lanes), I32),
        ],
        scratch_shapes=[pltpu.VMEM((n_rows, SORT_CHUNK), F32)],
        compiler_params=_cparams(1),
        name="expert_sort",
    )(ids.reshape(n_rows, SORT_CHUNK))


def _sc_mesh():
    return plsc.VectorSubcoreMesh(core_axis_name="c", subcore_axis_name="s", num_cores=SC_CORES,
                                  num_subcores=SC_WORKERS // SC_CORES)


def _sc_ring(n_chunks, read, write):
    reads, writes = {}, {}
    for j in range(min(SC_SLOTS - 1, n_chunks)):
        reads[j] = read(j)
    for j in range(n_chunks):
        for cp in reads.pop(j):
            cp.wait()
        nxt = j + SC_SLOTS - 1
        if nxt < n_chunks:
            for cp in writes.pop(nxt - SC_SLOTS, []):
                cp.wait()
            reads[nxt] = read(nxt)
        writes[j] = write(j)
    for cps in writes.values():
        for cp in cps:
            cp.wait()


def _sc_scatter_rows(src, pos2, n_rows):
    t = pos2.shape[1]
    src3 = src.reshape(t, ROW_CHUNKS, LANES)
    per_worker = t // SC_WORKERS
    n_chunks = per_worker // SC_CHUNK
    idx = pos2.reshape(2, SC_WORKERS, n_chunks, SC_CHUNK).transpose(1, 0, 2, 3)
    idx = idx.reshape(SC_WORKERS, 2 * n_chunks, SC_CHUNK)

    @functools.partial(
        pl.kernel, mesh=_sc_mesh(),
        out_type=jax.ShapeDtypeStruct((n_rows, ROW_CHUNKS, LANES), src.dtype),
        scratch_types=[
            pltpu.VMEM((2 * n_chunks, SC_CHUNK), I32),
            pltpu.VMEM((SC_SLOTS, SC_CHUNK, ROW_CHUNKS, LANES), src.dtype),
            pltpu.SemaphoreType.DMA((SC_SLOTS,)),
            pltpu.SemaphoreType.DMA((SC_SLOTS,)),
        ],
        name="sc_scatter_rows",
    )
    def scatter(src_hbm, idx_hbm, out_hbm, idx_v, rows_v, rsem, wsem):
        wid = lax.axis_index("s") * SC_CORES + lax.axis_index("c")
        pltpu.sync_copy(idx_hbm.at[wid], idx_v)

        def read(j):
            b = j % SC_SLOTS
            return [pltpu.async_copy(src_hbm.at[pl.ds(wid * per_worker + j * SC_CHUNK, SC_CHUNK)], rows_v.at[b],
                                     rsem.at[b])]

        def write(j):
            b = j % SC_SLOTS
            return [pltpu.async_copy(rows_v.at[b], out_hbm.at[idx_v.at[k * n_chunks + j]], wsem.at[b])
                    for k in range(2)]

        _sc_ring(n_chunks, read, write)

    return scatter(src3, idx).reshape(n_rows * ROW_CHUNKS, LANES)


def _sc_gather_rows(table, idx):
    n = idx.shape[0]
    table3 = table.reshape(-1, ROW_CHUNKS, LANES)
    per_worker = n // SC_WORKERS
    n_chunks = per_worker // SC_CHUNK

    @functools.partial(
        pl.kernel, mesh=_sc_mesh(),
        out_type=jax.ShapeDtypeStruct((n, ROW_CHUNKS, LANES), table.dtype),
        scratch_types=[
            pltpu.VMEM((n_chunks, SC_CHUNK), I32),
            pltpu.VMEM((SC_SLOTS, SC_CHUNK, ROW_CHUNKS, LANES), table.dtype),
            pltpu.SemaphoreType.DMA((SC_SLOTS,)),
            pltpu.SemaphoreType.DMA((SC_SLOTS,)),
        ],
        name="sc_gather_rows",
    )
    def gather(table_hbm, idx_hbm, out_hbm, idx_v, rows_v, rsem, wsem):
        wid = lax.axis_index("s") * SC_CORES + lax.axis_index("c")
        pltpu.sync_copy(idx_hbm.at[wid], idx_v)

        def read(j):
            b = j % SC_SLOTS
            return [pltpu.async_copy(table_hbm.at[idx_v.at[j]], rows_v.at[b], rsem.at[b])]

        def write(j):
            b = j % SC_SLOTS
            return [pltpu.async_copy(rows_v.at[b], out_hbm.at[pl.ds(wid * per_worker + j * SC_CHUNK, SC_CHUNK)],
                                     wsem.at[b])]

        _sc_ring(n_chunks, read, write)

    out = gather(table3, idx.reshape(SC_WORKERS, n_chunks, SC_CHUNK))
    return out.reshape(n * ROW_CHUNKS, LANES)


XS_SLOTS = 4


def _expert_body(plan_ref, xs_hbm, wgu_hbm, wd_hbm, ys_ref,
                 wgu_bf, wd_bf, xbuf, xsem, wgu_buf, wd_buf, wsem, *, e0):
    i = pl.program_id(0)
    n_used = plan_ref[PLAN_USED, 0]
    used = i < n_used
    tile_rows = EXPERT_TILE * ROW_CHUNKS

    def weight_fetch(expert, slot):
        return (pltpu.make_async_copy(wgu_hbm.at[e0 + expert], wgu_buf.at[slot], wsem.at[0, slot]),
                pltpu.make_async_copy(wd_hbm.at[e0 + expert], wd_buf.at[slot], wsem.at[1, slot]))

    @pl.when((i == 0) & used)
    def _():
        for cp in weight_fetch(plan_ref[PLAN_EXPERT, 0], plan_ref[PLAN_SLOT, 0]):
            cp.start()

    @pl.when(used & (plan_ref[PLAN_FIRST, i] == 1))
    def _():
        slot = plan_ref[PLAN_SLOT, i]
        for cp in weight_fetch(plan_ref[PLAN_EXPERT, i], slot):
            cp.wait()
        wgu_bf[...] = wgu_buf[slot].astype(BF16)
        wd_bf[...] = wd_buf[slot].astype(BF16)

        @pl.when(plan_ref[PLAN_NEXT, i] >= 0)
        def _():
            for cp in weight_fetch(plan_ref[PLAN_NEXT, i], 1 - slot):
                cp.start()


    def fetch(tile):
        slot = lax.rem(tile, XS_SLOTS)
        r0 = pl.multiple_of(tile * tile_rows, tile_rows)
        return pltpu.make_async_copy(xs_hbm.at[pl.ds(r0, tile_rows)], xbuf.at[slot], xsem.at[slot])

    ahead = XS_SLOTS - 1
    for tile in range(ahead):
        @pl.when((i == 0) & (tile < n_used))
        def _(tile=tile):
            fetch(tile).start()

    @pl.when(i + ahead < n_used)
    def _():
        fetch(i + ahead).start()

    @pl.when(used)
    def _():
        fetch(i).wait()
        live = lax.broadcasted_iota(I32, (EXPERT_TILE, 1), 0) < plan_ref[PLAN_VALID, i]
        xs = jnp.where(live, _load_rows(xbuf.at[lax.rem(i, XS_SLOTS)], EXPERT_TILE), 0.0).astype(BF16)
        gu = jnp.dot(xs, wgu_bf[...], preferred_element_type=F32)
        gate = gu[:, :EXPERT_FF]
        up = gu[:, EXPERT_FF:]
        act = gate * jax.nn.sigmoid(gate) * up
        _store_rows(ys_ref, jnp.dot(act.astype(BF16), wd_bf[...], preferred_element_type=F32))

    @pl.when(jnp.logical_not(used))
    def _():
        ys_ref[...] = jnp.zeros(ys_ref.shape, ys_ref.dtype)


def _expert_mlp(tile_plan, xs, w_gate_up, w_down, layer):
    d = D_MODEL
    n_tiles = xs.shape[0] // (EXPERT_TILE * ROW_CHUNKS)
    f2 = 2 * EXPERT_FF
    tile_rows = EXPERT_TILE * ROW_CHUNKS
    grid_spec = pltpu.PrefetchScalarGridSpec(
        num_scalar_prefetch=1,
        grid=(n_tiles,),
        in_specs=[
            pl.BlockSpec(memory_space=pl.ANY),
            pl.BlockSpec(memory_space=pl.ANY),
            pl.BlockSpec(memory_space=pl.ANY),
        ],
        out_specs=pl.BlockSpec((tile_rows, LANES), lambda i, plan: (i, 0)),
        scratch_shapes=[pltpu.VMEM((d, f2), BF16), pltpu.VMEM((EXPERT_FF, d), BF16),
                        pltpu.VMEM((XS_SLOTS, tile_rows, LANES), U32), pltpu.SemaphoreType.DMA((XS_SLOTS,)),
                        pltpu.VMEM((2, d, f2), F32), pltpu.VMEM((2, EXPERT_FF, d), F32),
                        pltpu.SemaphoreType.DMA((2, 2))],
    )
    return pl.pallas_call(
        functools.partial(_expert_body, e0=layer * N_EXPERTS),
        grid_spec=grid_spec,
        out_shape=jax.ShapeDtypeStruct(xs.shape, U32),
        compiler_params=_cparams(1),
        name="expert_mlp",
    )(tile_plan, xs, w_gate_up.reshape(DEPTH * N_EXPERTS, d, f2),
      w_down.reshape(DEPTH * N_EXPERTS, EXPERT_FF, d))


def _moe_combine(wts_ref, x_ref, mod_ref, lng_ref, lnb_ref, ya_ref, yb_ref, token0, n):
    d = D_MODEL
    rows = slice(token0, token0 + n)
    cols = []
    for k in range(2):
        wt = jnp.broadcast_to(wts_ref[k:k + 1, rows], (LANES, n)).T
        cols.append(jnp.concatenate([wt] * (d // LANES), axis=1))
    y = cols[0] * _load_rows(ya_ref, n, token0) + cols[1] * _load_rows(yb_ref, n, token0)
    gt_f = mod_ref[:, 5 * d:6 * d]
    return _deepnorm_ln(x_ref[rows, :], y, gt_f, lng_ref[...], lnb_ref[...])


def _combine_specs(tm, n_steps, steps_per_batch):
    d = D_MODEL
    return [
        pl.BlockSpec((2, tm), lambda i: (0, i)),
        pl.BlockSpec((tm, d), lambda i: (i, 0)),
        pl.BlockSpec((None, 1, N_MOD * d), lambda i: (i // steps_per_batch, 0, 0)),
        pl.BlockSpec((1, d), lambda i: (0, 0)),
        pl.BlockSpec((1, d), lambda i: (0, 0)),
        pl.BlockSpec((tm * ROW_CHUNKS, LANES), lambda i: (i, 0)),
        pl.BlockSpec((tm * ROW_CHUNKS, LANES), lambda i: (n_steps + i, 0)),
    ]


def _gather_rows_of_tokens(ys, pos2):
    return _sc_gather_rows(ys, pos2.reshape(-1))


def _combine_body(wts_ref, x_ref, mod_ref, lng_ref, lnb_ref, ya_ref, yb_ref, o_ref):
    for r in range(x_ref.shape[0] // SUB_ROWS):
        o_ref[r * SUB_ROWS:(r + 1) * SUB_ROWS, :] = _moe_combine(wts_ref, x_ref, mod_ref, lng_ref, lnb_ref, ya_ref,
                                                                 yb_ref, r * SUB_ROWS, SUB_ROWS)


def _combine_ln(moe, mod_rows, ln_g, ln_b, seq):
    wts, x1, ys, pos2 = moe
    yg = _gather_rows_of_tokens(ys, pos2)
    t, d = x1.shape
    tm = 1024
    n_steps = t // tm
    return pl.pallas_call(
        _combine_body,
        grid=(n_steps,),
        in_specs=_combine_specs(tm, n_steps, seq // tm),
        out_specs=pl.BlockSpec((tm, d), lambda i: (i, 0)),
        out_shape=jax.ShapeDtypeStruct((t, d), F32),
        compiler_params=_cparams(1),
        name="moe_combine_ln",
    )(wts, x1, mod_rows, ln_g.reshape(1, d), ln_b.reshape(1, d), yg, yg)


def _gmlp_body(wts_ref, x1_ref, modp_ref, lng_ref, lnb_ref, ya_ref, yb_ref,
               mod_ref, w_ref, b_ref, g_ref, beta_ref, ws_ref, bs_ref, x2_ref, o_ref, ws_bf, w_bf):
    @pl.when(pl.program_id(0) == 0)
    def _():
        tri = lax.broadcasted_iota(I32, (CHUNK, CHUNK), 0) >= lax.broadcasted_iota(I32, (CHUNK, CHUNK), 1)
        for g in range(N_SGU_GROUPS):
            ws_bf[g] = jnp.where(tri, 0.5 * ws_ref[g], 0.0).astype(BF16)
        w_bf[...] = w_ref[...].astype(BF16)

    d = D_MODEL
    sh = mod_ref[:, 0:d]
    sc = mod_ref[:, d:2 * d]
    sub_rows = x1_ref.shape[0]
    for r in range(x1_ref.shape[0] // sub_rows):
        r0 = r * sub_rows
        x2 = _moe_combine(wts_ref, x1_ref, modp_ref, lng_ref, lnb_ref, ya_ref, yb_ref, r0, sub_rows)
        x2_ref[r0:r0 + sub_rows, :] = x2
        h = x2 * (1.0 + sc) + sh
        z = jnp.dot(h.astype(BF16), w_bf[...], preferred_element_type=F32) + b_ref[...]
        z2 = z * (1.0 + lax.erf(z * (2.0 ** -0.5)))
        u = z2[:, :GMLP_WIDTH]
        v = _layer_norm(z2[:, GMLP_WIDTH:], g_ref[...], beta_ref[...], eps=4.0 * LN_EPS).astype(BF16)
        for ci in range(sub_rows // CHUNK):
            rows = slice(ci * CHUNK, (ci + 1) * CHUNK)
            out_rows = slice(r0 + ci * CHUNK, r0 + (ci + 1) * CHUNK)
            for g in range(N_SGU_GROUPS):
                lanes = slice(g * SGU_GROUP_DIM, (g + 1) * SGU_GROUP_DIM)
                mixed = jnp.dot(ws_bf[g], v[rows, lanes], preferred_element_type=F32) + 0.5 * bs_ref[:, g:g + 1]
                o_ref[out_rows, lanes] = (u[rows, lanes] * mixed).astype(BF16)


def _gmlp_gate(moe, ln_g, ln_b, mod_prev, mod_rows, w_in, b_in, sgu_g, sgu_b, w_s, b_s, seq):
    wts, x1, ys, pos2 = moe
    yg = _gather_rows_of_tokens(ys, pos2)
    t, d = x1.shape
    tm = 512
    steps_per_batch = seq // tm
    n_steps = t // tm
    gw = GMLP_WIDTH
    return pl.pallas_call(
        _gmlp_body,
        grid=(n_steps,),
        in_specs=_combine_specs(tm, n_steps, steps_per_batch) + [
            pl.BlockSpec((None, 1, N_MOD * d), lambda i: (i // steps_per_batch, 0, 0)),
            pl.BlockSpec((d, 2 * gw), lambda i: (0, 0)),
            pl.BlockSpec((1, 2 * gw), lambda i: (0, 0)),
            pl.BlockSpec((1, gw), lambda i: (0, 0)),
            pl.BlockSpec((1, gw), lambda i: (0, 0)),
            pl.BlockSpec((N_SGU_GROUPS, CHUNK, CHUNK), lambda i: (0, 0, 0)),
            pl.BlockSpec((CHUNK, N_SGU_GROUPS), lambda i: (0, 0)),
        ],
        out_specs=[pl.BlockSpec((tm, d), lambda i: (i, 0)), pl.BlockSpec((tm, gw), lambda i: (i, 0))],
        out_shape=[jax.ShapeDtypeStruct((t, d), F32), jax.ShapeDtypeStruct((t, gw), BF16)],
        scratch_shapes=[pltpu.VMEM((N_SGU_GROUPS, CHUNK, CHUNK), BF16), pltpu.VMEM((d, 2 * gw), BF16)],
        compiler_params=_cparams(1),
        name="combine_gmlp_gate",
    )(wts, x1, mod_prev, ln_g.reshape(1, d), ln_b.reshape(1, d), yg, yg,
      mod_rows, w_in, b_in.reshape(1, 2 * gw), sgu_g.reshape(1, gw), sgu_b.reshape(1, gw), w_s, b_s.T)


def _router_params(w_group, b_group, w_expert, b_expert):
    def lanes(group_part, expert_part):
        rows = group_part.shape[0]
        gap = jnp.zeros((rows, EXPERT_COL0 - N_EXPERT_GROUPS), F32)
        tail = jnp.zeros((rows, ROUTER_LANES - EXPERT_COL0 - N_EXPERTS), F32)
        return jnp.concatenate([group_part, gap, expert_part, tail], axis=1)

    return lanes(w_group, w_expert), lanes(b_group[None, :], b_expert[None, :])


def _moe_experts(x1, h2, ids, wts, layer, w_gate_up, w_down):
    t = x1.shape[0]
    n_rows = 2 * t + N_EXPERTS * EXPERT_TILE
    n_tiles = n_rows // EXPERT_TILE
    pos, tile_plan = _expert_sort(ids, n_tiles)
    pos2 = pos.reshape(2, t)
    xs = _sc_scatter_rows(h2, pos2, n_rows)
    ys = _expert_mlp(tile_plan, xs, w_gate_up, w_down, layer)
    return wts, x1, ys, pos2


def kernel(x, c, positions, ada_w, ada_b, post_ln_g, post_ln_b, attn_w_qkv, attn_b_qkv, attn_sinks, attn_w_o, attn_b_o, gmlp_w_in, gmlp_b_in, gmlp_sgu_ln_g, gmlp_sgu_ln_b, gmlp_w_s, gmlp_b_s, gmlp_w_out, gmlp_b_out, moe_w_group_router, moe_b_group_router, moe_w_expert_router, moe_b_expert_router, moe_w_gate_up, moe_w_down):
    batch, seq, d = x.shape
    t = batch * seq
    assert d == D_MODEL and batch <= MOD_ROWS and ada_w.shape[0] == DEPTH == 2
    assert seq % 1024 == 0, "token tiles of 1024 rows must not straddle sequences"
    assert t % (SC_WORKERS * SC_CHUNK) == 0, "every SparseCore subcore moves whole chunks"
    x2 = x.reshape(t, d)
    c_pad = jnp.pad(c, ((0, MOD_ROWS - batch), (0, 0)))
    mods = [_adaln_mod(c_pad, ada_w, ada_b, layer, batch) for layer in range(DEPTH)]

    moe = None
    for layer in range(DEPTH):
        j = layer // 2
        if layer % 2 == 0:
            if moe is not None:
                x2 = _combine_ln(moe, mods[layer - 1], post_ln_g[layer - 1, 1], post_ln_b[layer - 1, 1], seq)
            qkv = _qkv_rope(x2, mods[layer], positions, attn_w_qkv[j], attn_b_qkv[j], seq)
            mix = _attention(qkv, attn_sinks[j], batch, seq)
            w_out, b_out = attn_w_o[j], attn_b_o[j]
        else:
            x2, mix = _gmlp_gate(moe, post_ln_g[layer - 1, 1], post_ln_b[layer - 1, 1], mods[layer - 1], mods[layer],
                                 gmlp_w_in[j], gmlp_b_in[j], gmlp_sgu_ln_g[j], gmlp_sgu_ln_b[j],
                                 gmlp_w_s[j], gmlp_b_s[j], seq)
            w_out, b_out = gmlp_w_out[j], gmlp_b_out[j]
        w_router, b_router = _router_params(moe_w_group_router[layer], moe_b_group_router[layer],
                                            moe_w_expert_router[layer], moe_b_expert_router[layer])
        x1, h2, ids, wts = _proj_ln_router(mix, x2, mods[layer], w_out, b_out, post_ln_g[layer, 0],
                                           post_ln_b[layer, 0], w_router, b_router, seq)
        moe = _moe_experts(x1, h2, ids, wts, layer, moe_w_gate_up, moe_w_down)
    x2 = _combine_ln(moe, mods[DEPTH - 1], post_ln_g[DEPTH - 1, 1], post_ln_b[DEPTH - 1, 1], seq)
    return x2.reshape(batch, seq, d)
```

```python
import functools

import jax
import jax.numpy as jnp
from jax import lax
from jax.experimental import pallas as pl
from jax.experimental.pallas import tpu as pltpu
from jax.experimental.pallas import tpu_sc as plsc

F32 = jnp.float32
BF16 = jnp.bfloat16
I32 = jnp.int32

D_MODEL = 1024
DEPTH = 2
HEAD_DIM = 64
N_Q_HEADS = 16
N_KV_HEADS = 4
GQA_GROUP = N_Q_HEADS // N_KV_HEADS
WINDOW = 128
ROPE_THETA = 10000.0
Q_WIDTH = N_Q_HEADS * HEAD_DIM
KV_WIDTH = N_KV_HEADS * HEAD_DIM
QKV_WIDTH = Q_WIDTH + 2 * KV_WIDTH
CHUNK = 128
GMLP_WIDTH = 2 * D_MODEL
N_SGU_GROUPS = 8
SGU_GROUP_DIM = GMLP_WIDTH // N_SGU_GROUPS
N_EXPERT_GROUPS = 4
EXPERTS_PER_GROUP = 8
N_EXPERTS = N_EXPERT_GROUPS * EXPERTS_PER_GROUP
EXPERT_FF = D_MODEL // 4
N_MOD = 6
DEEPNORM_ALPHA = (2.0 * DEPTH) ** 0.25
LN_EPS = 1e-5

LANES = 128
MOD_ROWS = 8
ROUTER_LANES = 128
EXPERT_COL0 = 8
SORT_CHUNK = 256
EXPERT_TILE = 512
SUB_ROWS = 256
VMEM_LIMIT = 56 * 1024 * 1024
SC_CORES = 2
SC_WORKERS = 32
SC_CHUNK = 64
SC_SLOTS = 3


def _cparams(n_axes, vmem=VMEM_LIMIT):
    return pltpu.CompilerParams(dimension_semantics=("arbitrary",) * n_axes, vmem_limit_bytes=vmem)


U32 = jnp.uint32
ROW_CHUNKS = D_MODEL // 2 // LANES


def _store_rows(ref, val, token0=0):
    n = val.shape[0]
    half = D_MODEL // 2
    words = pltpu.pack_elementwise([val[:, :half], val[:, half:]], packed_dtype=BF16)
    for c in range(ROW_CHUNKS):
        ref[pl.ds(token0 * ROW_CHUNKS + c, n, stride=ROW_CHUNKS), :] = words[:, c * LANES:(c + 1) * LANES]


def _load_rows(ref, n, token0=0):
    words = jnp.concatenate([ref[pl.ds(token0 * ROW_CHUNKS + c, n, stride=ROW_CHUNKS), :]
                             for c in range(ROW_CHUNKS)], axis=1)
    lo = lax.bitcast_convert_type(words << 16, F32)
    hi = lax.bitcast_convert_type(words & jnp.uint32(0xFFFF0000), F32)
    return jnp.concatenate([lo, hi], axis=1)


def _layer_norm(r, g, b, eps=LN_EPS):
    mu = jnp.mean(r, axis=-1, keepdims=True)
    d = r - mu
    var = jnp.mean(d * d, axis=-1, keepdims=True)
    return d * lax.rsqrt(var + eps) * g + b


def _deepnorm_ln(x, branch, gate_row, g, b):
    scale = (1.0 + gate_row) * (1.0 / DEEPNORM_ALPHA)
    return _layer_norm(x + scale * branch, g, b, eps=LN_EPS / (DEEPNORM_ALPHA * DEEPNORM_ALPHA))


def _mod_body(c_ref, w_ref, b_ref, o_ref):
    c = c_ref[...]
    ca = c * jax.nn.sigmoid(c)
    mod = jnp.dot(ca.astype(BF16), w_ref[...].astype(BF16), preferred_element_type=F32) + b_ref[...]
    for r in range(o_ref.shape[0]):
        o_ref[r] = mod[r:r + 1]


def _adaln_mod(c_pad, ada_w, ada_b, layer, batch):
    tn = 1536
    n_out = N_MOD * D_MODEL
    return pl.pallas_call(
        _mod_body,
        grid=(n_out // tn,),
        in_specs=[
            pl.BlockSpec((MOD_ROWS, D_MODEL), lambda j: (0, 0)),
            pl.BlockSpec((None, D_MODEL, tn), lambda j: (layer, 0, j)),
            pl.BlockSpec((None, 1, tn), lambda j: (layer, 0, j)),
        ],
        out_specs=pl.BlockSpec((batch, 1, tn), lambda j: (0, 0, j)),
        out_shape=jax.ShapeDtypeStruct((batch, 1, n_out), F32),
        compiler_params=_cparams(1),
        name="adaln_mod",
    )(c_pad, ada_w, ada_b.reshape(DEPTH, 1, n_out))


def _qkv_body(x_ref, mod_ref, pos_ref, w_ref, b_ref, invf_ref, o_ref, wbf_ref):
    @pl.when(pl.program_id(0) == 0)
    def _():
        wbf_ref[...] = w_ref[...].astype(BF16)

    tm = x_ref.shape[0]
    sh = mod_ref[:, 0:D_MODEL]
    sc = mod_ref[:, D_MODEL:2 * D_MODEL]
    h = x_ref[...] * (1.0 + sc) + sh
    qkv = jnp.dot(h.astype(BF16), wbf_ref[...], preferred_element_type=F32) + b_ref[...]

    ang = invf_ref[...] * pos_ref[...].astype(F32)
    c = jnp.cos(ang)
    s = jnp.sin(ang)
    ct = jnp.concatenate([c, c, c, c], axis=0).T
    st = jnp.concatenate([-s, s, -s, s], axis=0).T
    lane = lax.broadcasted_iota(I32, (tm, LANES), 1)
    first_half = (lane & (HEAD_DIM // 2)) == 0
    n_rope = (Q_WIDTH + KV_WIDTH) // LANES
    for j in range(n_rope):
        blk = qkv[:, j * LANES:(j + 1) * LANES]
        rot = jnp.where(first_half, pltpu.roll(blk, LANES - HEAD_DIM // 2, 1), pltpu.roll(blk, HEAD_DIM // 2, 1))
        r = blk * ct + rot * st
        if j < Q_WIDTH // LANES:
            r = r * (HEAD_DIM ** -0.5)
        o_ref[:, j * LANES:(j + 1) * LANES] = r.astype(BF16)
    o_ref[:, Q_WIDTH + KV_WIDTH:] = qkv[:, Q_WIDTH + KV_WIDTH:].astype(BF16)


def _qkv_rope(x2, mod_rows, positions, w_qkv, b_qkv, seq):
    t = x2.shape[0]
    tm = 1024
    steps_per_batch = seq // tm
    inv_freq = ROPE_THETA ** (-jnp.arange(0, HEAD_DIM, 2, dtype=F32) / HEAD_DIM)
    return pl.pallas_call(
        _qkv_body,
        grid=(t // tm,),
        in_specs=[
            pl.BlockSpec((tm, D_MODEL), lambda i: (i, 0)),
            pl.BlockSpec((None, 1, N_MOD * D_MODEL), lambda i: (i // steps_per_batch, 0, 0)),
            pl.BlockSpec((None, 1, tm), lambda i: (i, 0, 0)),
            pl.BlockSpec((D_MODEL, QKV_WIDTH), lambda i: (0, 0)),
            pl.BlockSpec((1, QKV_WIDTH), lambda i: (0, 0)),
            pl.BlockSpec((HEAD_DIM // 2, 1), lambda i: (0, 0)),
        ],
        out_specs=pl.BlockSpec((tm, QKV_WIDTH), lambda i: (i, 0)),
        out_shape=jax.ShapeDtypeStruct((t, QKV_WIDTH), BF16),
        scratch_shapes=[pltpu.VMEM((D_MODEL, QKV_WIDTH), BF16)],
        compiler_params=_cparams(1),
        name="qkv_rope",
    )(x2, mod_rows, positions.reshape(t // tm, 1, tm), w_qkv, b_qkv.reshape(1, QKV_WIDTH),
      inv_freq.reshape(HEAD_DIM // 2, 1))


BF16_ROWS = 16
BF16_ONE_PAIR = 0x3F803F80


def _attn_prepare(kv, kab_ref, vx_ref, slot):
    words = pltpu.bitcast(kv, U32)
    low = lax.broadcasted_iota(I32, (WINDOW // 2, LANES), 1) < HEAD_DIM
    zeros = jnp.zeros((WINDOW // 2, LANES), U32)
    ones = jnp.full((WINDOW // 2, LANES), BF16_ONE_PAIR, U32)
    for g in range(N_KV_HEADS):
        for part, ref in ((0, None), (KV_WIDTH, vx_ref)):
            tile = words[:, part + (g // 2) * LANES:part + (g // 2 + 1) * LANES]
            other = pltpu.roll(tile, HEAD_DIM, 1)
            in_low, in_high = (tile, other) if g % 2 == 0 else (other, tile)
            if ref is None:
                kab_ref[slot, 2 * g] = pltpu.bitcast(jnp.where(low, in_low, zeros), BF16)
                kab_ref[slot, 2 * g + 1] = pltpu.bitcast(jnp.where(low, zeros, in_high), BF16)
            else:
                both = jnp.where(low, in_low, in_high)
                vx_ref[slot, g] = pltpu.bitcast(jnp.concatenate([both, ones], axis=1), BF16)


def _attn_block(sink_ref, q, kab_ref, vx_ref, s_ref, p_ref, prev, cur, first_block):
    for g in range(N_KV_HEADS):
        q_pair = jnp.concatenate([q[:, (2 * g) * LANES:(2 * g + 1) * LANES],
                                  q[:, (2 * g + 1) * LANES:(2 * g + 2) * LANES]], axis=0)
        for a in range(2):
            kband = jnp.concatenate([kab_ref[prev, 2 * g + a], kab_ref[cur, 2 * g + a]], axis=0)
            s = lax.dot_general(q_pair, kband, (((1,), (1,)), ((), ())), preferred_element_type=F32)
            s_ref[GQA_GROUP * g + a] = s[:WINDOW]
            s_ref[GQA_GROUP * g + 2 + a] = s[WINDOW:]
    qi = lax.broadcasted_iota(I32, (WINDOW, 2 * WINDOW), 0) + WINDOW
    kj = lax.broadcasted_iota(I32, (WINDOW, 2 * WINDOW), 1)
    mask = (kj <= qi) & (kj > qi - WINDOW) & ((kj >= WINDOW) | jnp.logical_not(first_block))
    key0 = lax.broadcasted_iota(I32, (1, 2 * WINDOW), 1) == 0
    for h in range(N_Q_HEADS):
        s = jnp.where(mask, s_ref[h], jnp.where(key0, sink_ref[h], -jnp.inf))
        m = jnp.max(s, axis=-1, keepdims=True)
        p_ref[h] = jnp.exp(s - m).astype(BF16)
    low = lax.broadcasted_iota(I32, (WINDOW, LANES), 1) < HEAD_DIM
    sink_row = ((lax.broadcasted_iota(I32, (BF16_ROWS, 2 * LANES), 0) == 0)
                & (lax.broadcasted_iota(I32, (BF16_ROWS, 2 * LANES), 1) < LANES))
    out_tiles = []
    for g in range(N_KV_HEADS):
        v_prev = vx_ref[prev, g]
        v_head = jnp.where(sink_row, 0.0, v_prev[:BF16_ROWS].astype(F32)).astype(BF16)
        vband = jnp.concatenate([v_head, v_prev[BF16_ROWS:], vx_ref[cur, g]], axis=0)
        p4 = p_ref[GQA_GROUP * g:GQA_GROUP * (g + 1)].reshape(GQA_GROUP * WINDOW, 2 * WINDOW)
        o4 = jnp.dot(p4, vband, preferred_element_type=F32)
        heads = []
        for j in range(GQA_GROUP):
            blk = o4[j * WINDOW:(j + 1) * WINDOW]
            heads.append(blk[:, :LANES] / blk[:, LANES:])
        out_tiles.append(jnp.where(low, heads[0], heads[1]))
        out_tiles.append(jnp.where(low, heads[2], heads[3]))
    return jnp.concatenate(out_tiles, axis=1).astype(BF16)


def _attn_body(sink_ref, q_ref, kv_ref, o_ref, kab_ref, vx_ref, s_ref, p_ref):
    n = pl.program_id(1)

    @pl.when(n == 0)
    def _():
        kab_ref[1] = jnp.zeros(kab_ref.shape[1:], kab_ref.dtype)
        half = (N_KV_HEADS, WINDOW, LANES)
        vx_ref[1] = jnp.concatenate([jnp.zeros(half, BF16), jnp.ones(half, BF16)], axis=-1)

    scratch = (kab_ref, vx_ref, s_ref, p_ref)
    for blk in range(q_ref.shape[0] // WINDOW):
        rows = slice(blk * WINDOW, (blk + 1) * WINDOW)
        cur = blk % 2
        _attn_prepare(kv_ref[rows, :], kab_ref, vx_ref, cur)
        o_ref[rows, :] = _attn_block(sink_ref, q_ref[rows, :], *scratch, 1 - cur, cur,
                                     (n == 0) if blk == 0 else False)


def _attention(qkv, sinks, batch, seq):
    t = qkv.shape[0]
    tq = 8 * WINDOW
    steps = seq // tq
    kv_col = Q_WIDTH // (2 * KV_WIDTH)
    return pl.pallas_call(
        _attn_body,
        grid=(batch, steps),
        in_specs=[
            pl.BlockSpec(memory_space=pltpu.SMEM),
            pl.BlockSpec((tq, Q_WIDTH), lambda b, n: (b * steps + n, 0)),
            pl.BlockSpec((tq, 2 * KV_WIDTH), lambda b, n: (b * steps + n, kv_col)),
        ],
        out_specs=pl.BlockSpec((tq, Q_WIDTH), lambda b, n: (b * steps + n, 0)),
        out_shape=jax.ShapeDtypeStruct((t, Q_WIDTH), BF16),
        scratch_shapes=[
            pltpu.VMEM((2, 2 * N_KV_HEADS, WINDOW, LANES), BF16),
            pltpu.VMEM((2, N_KV_HEADS, WINDOW, 2 * LANES), BF16),
            pltpu.VMEM((N_Q_HEADS, WINDOW, 2 * WINDOW), F32),
            pltpu.VMEM((N_Q_HEADS, WINDOW, 2 * WINDOW), BF16),
        ],
        compiler_params=_cparams(2),
        name="swa_attention",
    )(sinks, qkv, qkv)


def _route(lt):
    tm = lt.shape[1]
    row = lax.broadcasted_iota(I32, (EXPERTS_PER_GROUP, tm), 0)
    neg = -jnp.inf
    gl = jnp.where(row < N_EXPERT_GROUPS, lt[0:EXPERTS_PER_GROUP], neg)
    gm = jnp.max(gl, axis=0, keepdims=True)
    g_p = 1.0 / jnp.sum(jnp.exp(gl - gm), axis=0, keepdims=True)
    g_idx = jnp.min(jnp.where(gl == gm, row, EXPERTS_PER_GROUP), axis=0, keepdims=True)
    sel = lt[EXPERT_COL0 + (N_EXPERT_GROUPS - 1) * EXPERTS_PER_GROUP:EXPERT_COL0 + N_EXPERTS]
    for g in range(N_EXPERT_GROUPS - 2, -1, -1):
        lo = EXPERT_COL0 + g * EXPERTS_PER_GROUP
        sel = jnp.where(g_idx == g, lt[lo:lo + EXPERTS_PER_GROUP], sel)
    v1 = jnp.max(sel, axis=0, keepdims=True)
    i1 = jnp.min(jnp.where(sel == v1, row, EXPERTS_PER_GROUP), axis=0, keepdims=True)
    sel2 = jnp.where(row == i1, neg, sel)
    v2 = jnp.max(sel2, axis=0, keepdims=True)
    i2 = jnp.min(jnp.where(sel2 == v2, row, EXPERTS_PER_GROUP), axis=0, keepdims=True)
    e2 = jnp.exp(v2 - v1)
    w1 = g_p / (1.0 + e2)
    w2 = g_p * e2 / (1.0 + e2)
    base = g_idx * EXPERTS_PER_GROUP
    return base + i1, base + i2, w1, w2


def _proj_body(o_ref, x_ref, mod_ref, w_ref, b_ref, lng_ref, lnb_ref, wr_ref, br_ref,
               x1_ref, h2_ref, ids_ref, wts_ref, wbf_ref):
    @pl.when(pl.program_id(0) == 0)
    def _():
        wbf_ref[...] = w_ref[...].astype(BF16)

    d = D_MODEL
    gt_m = mod_ref[:, 2 * d:3 * d]
    sh_f = mod_ref[:, 3 * d:4 * d]
    sc_f = mod_ref[:, 4 * d:5 * d]
    wr = wr_ref[...].astype(BF16)
    for r in range(x_ref.shape[0] // SUB_ROWS):
        rows = slice(r * SUB_ROWS, (r + 1) * SUB_ROWS)
        y = jnp.dot(o_ref[rows, :], wbf_ref[...], preferred_element_type=F32) + b_ref[...]
        x1 = _deepnorm_ln(x_ref[rows, :], y, gt_m, lng_ref[...], lnb_ref[...])
        x1_ref[rows, :] = x1
        h2 = x1 * (1.0 + sc_f) + sh_f
        _store_rows(h2_ref, h2, r * SUB_ROWS)
        logits = jnp.dot(h2.astype(BF16), wr, preferred_element_type=F32) + br_ref[...]
        ea, eb, wa, wb = _route(logits.T)
        ids_ref[0:1, rows] = ea
        ids_ref[1:2, rows] = eb
        wts_ref[0:1, rows] = wa
        wts_ref[1:2, rows] = wb


def _proj_ln_router(o, x2, mod_rows, w, b, ln_g, ln_b, w_router, b_router, seq):
    t, k = o.shape
    tm = 1024
    steps_per_batch = seq // tm
    d = D_MODEL
    return pl.pallas_call(
        _proj_body,
        grid=(t // tm,),
        in_specs=[
            pl.BlockSpec((tm, k), lambda i: (i, 0)),
            pl.BlockSpec((tm, d), lambda i: (i, 0)),
            pl.BlockSpec((None, 1, N_MOD * d), lambda i: (i // steps_per_batch, 0, 0)),
            pl.BlockSpec((k, d), lambda i: (0, 0), pipeline_mode=pl.Buffered(1)),
            pl.BlockSpec((1, d), lambda i: (0, 0)),
            pl.BlockSpec((1, d), lambda i: (0, 0)),
            pl.BlockSpec((1, d), lambda i: (0, 0)),
            pl.BlockSpec((d, ROUTER_LANES), lambda i: (0, 0)),
            pl.BlockSpec((1, ROUTER_LANES), lambda i: (0, 0)),
        ],
        out_specs=[
            pl.BlockSpec((tm, d), lambda i: (i, 0)),
            pl.BlockSpec((tm * ROW_CHUNKS, LANES), lambda i: (i, 0)),
            pl.BlockSpec((2, tm), lambda i: (0, i)),
            pl.BlockSpec((2, tm), lambda i: (0, i)),
        ],
        out_shape=[
            jax.ShapeDtypeStruct((t, d), F32),
            jax.ShapeDtypeStruct((t * ROW_CHUNKS, LANES), U32),
            jax.ShapeDtypeStruct((2, t), I32),
            jax.ShapeDtypeStruct((2, t), F32),
        ],
        scratch_shapes=[pltpu.VMEM((k, d), BF16)],
        compiler_params=_cparams(1),
        name="proj_ln_router",
    )(o, x2, mod_rows, w, b.reshape(1, d), ln_g.reshape(1, d), ln_b.reshape(1, d), w_router, b_router)


PLAN_EXPERT, PLAN_USED, PLAN_VALID, PLAN_FIRST, PLAN_NEXT, PLAN_SLOT = range(6)
PLAN_ROWS = 8


def _sort_body(ids_ref, pos_ref, plan_ref, rank_ref):
    n_rows = ids_ref.shape[0]
    c = SORT_CHUNK
    erow = lax.broadcasted_iota(I32, (N_EXPERTS, c), 0)
    tri = (lax.broadcasted_iota(I32, (c, c), 0) <= lax.broadcasted_iota(I32, (c, c), 1)).astype(BF16)

    def rank_step(r, carry):
        onehot = erow == ids_ref[pl.ds(r, 1), :]
        pref = jnp.dot(onehot.astype(BF16), tri, preferred_element_type=F32)
        rank = jnp.sum(jnp.where(onehot, pref + carry, 0.0), axis=0, keepdims=True) - 1.0
        rank_ref[pl.ds(r, 1), :] = rank
        return carry + pref[:, c - 1:c]

    counts = lax.fori_loop(0, n_rows, rank_step, jnp.zeros((N_EXPERTS, 1), F32), unroll=8)
    n_tile = jnp.floor((counts + (EXPERT_TILE - 1)) * (1.0 / EXPERT_TILE))
    low = (lax.broadcasted_iota(I32, (N_EXPERTS, N_EXPERTS), 1)
           <= lax.broadcasted_iota(I32, (N_EXPERTS, N_EXPERTS), 0)).astype(BF16)
    cum = jnp.dot(low, jnp.broadcast_to(n_tile, (N_EXPERTS, LANES)).astype(BF16),
                  preferred_element_type=F32)[:, 0:1]
    row_off = (cum - n_tile) * EXPERT_TILE

    def pos_step(r, _):
        onehot = erow == ids_ref[pl.ds(r, 1), :]
        off = jnp.sum(jnp.where(onehot, row_off, 0.0), axis=0, keepdims=True)
        pos_ref[pl.ds(r, 1), :] = (off + rank_ref[pl.ds(r, 1), :]).astype(I32)
        return 0

    lax.fori_loop(0, n_rows, pos_step, 0, unroll=8)
    total = jnp.max(cum, axis=0, keepdims=True)
    n_lanes = plan_ref.shape[1]
    tile = jnp.minimum(lax.broadcasted_iota(I32, (N_EXPERTS, n_lanes), 1).astype(F32), total - 1.0)
    plan_ref[PLAN_EXPERT:PLAN_EXPERT + 1, :] = jnp.sum(jnp.where(cum <= tile, 1.0, 0.0), axis=0,
                                                       keepdims=True).astype(I32)
    plan_ref[PLAN_USED:PLAN_USED + 1, :] = jnp.broadcast_to(total, (1, n_lanes)).astype(I32)
    tile_f = lax.broadcasted_iota(I32, (N_EXPERTS, n_lanes), 1).astype(F32)
    first = cum - n_tile
    rows_left = jnp.clip(counts - (tile_f - first) * EXPERT_TILE, 0.0, float(EXPERT_TILE))
    owns = (first <= tile_f) & (tile_f < cum)
    plan_ref[PLAN_VALID:PLAN_VALID + 1, :] = jnp.sum(jnp.where(owns, rows_left, 0.0), axis=0,
                                                     keepdims=True).astype(I32)
    def lane_sum(x):
        return jnp.sum(x, axis=0, keepdims=True)

    is_first = lane_sum(jnp.where(owns & (tile_f == first), 1.0, 0.0))
    group_end = lane_sum(jnp.where(owns, cum, 0.0))
    next_expert = jnp.where(group_end < total, lane_sum(jnp.where(cum <= group_end, 1.0, 0.0)), -1.0)
    ordinal = lane_sum(jnp.where((cum <= tile_f[0:1]) & (n_tile > 0.0), 1.0, 0.0))
    slot = ordinal - 2.0 * jnp.floor(ordinal * 0.5)
    plan_ref[PLAN_FIRST:PLAN_FIRST + 1, :] = is_first.astype(I32)
    plan_ref[PLAN_NEXT:PLAN_NEXT + 1, :] = next_expert.astype(I32)
    plan_ref[PLAN_SLOT:PLAN_SLOT + 1, :] = slot.astype(I32)
    plan_ref[PLAN_SLOT + 1:PLAN_ROWS, :] = jnp.zeros((PLAN_ROWS - PLAN_SLOT - 1, n_lanes), I32)


def _expert_sort(ids, n_tiles):
    n_assign = ids.shape[0] * ids.shape[1]
    n_rows = n_assign // SORT_CHUNK
    te_lanes = -(-n_tiles // LANES) * LANES
    return pl.pallas_call(
        _sort_body,
        grid=(1,),
        in_specs=[pl.BlockSpec((n_rows, SORT_CHUNK), lambda i: (0, 0))],
        out_specs=[
            pl.BlockSpec((n_rows, SORT_CHUNK), lambda i: (0, 0)),
            pl.BlockSpec((PLAN_ROWS, te_lanes), lambda i: (0, 0)),
        ],
        out_shape=[
            jax.ShapeDtypeStruct((n_rows, SORT_CHUNK), I32),
            jax.ShapeDtypeStruct((PLAN_ROWS, te_lanes), I32),
        ],
        scratch_shapes=[pltpu.VMEM((n_rows, SORT_CHUNK), F32)],
        compiler_params=_cparams(1),
        name="expert_sort",
    )(ids.reshape(n_rows, SORT_CHUNK))


def _sc_mesh():
    return plsc.VectorSubcoreMesh(core_axis_name="c", subcore_axis_name="s", num_cores=SC_CORES,
                                  num_subcores=SC_WORKERS // SC_CORES)


def _sc_ring(n_chunks, read, write):
    reads, writes = {}, {}
    for j in range(min(SC_SLOTS - 1, n_chunks)):
        reads[j] = read(j)
    for j in range(n_chunks):
        for cp in reads.pop(j):
            cp.wait()
        nxt = j + SC_SLOTS - 1
        if nxt < n_chunks:
            for cp in writes.pop(nxt - SC_SLOTS, []):
                cp.wait()
            reads[nxt] = read(nxt)
        writes[j] = write(j)
    for cps in writes.values():
        for cp in cps:
            cp.wait()


def _sc_scatter_rows(src, pos2, n_rows):
    t = pos2.shape[1]
    src3 = src.reshape(t, ROW_CHUNKS, LANES)
    per_worker = t // SC_WORKERS
    n_chunks = per_worker // SC_CHUNK
    idx = pos2.reshape(2, SC_WORKERS, n_chunks, SC_CHUNK).transpose(1, 0, 2, 3)
    idx = idx.reshape(SC_WORKERS, 2 * n_chunks, SC_CHUNK)

    @functools.partial(
        pl.kernel, mesh=_sc_mesh(),
        out_type=jax.ShapeDtypeStruct((n_rows, ROW_CHUNKS, LANES), src.dtype),
        scratch_types=[
            pltpu.VMEM((2 * n_chunks, SC_CHUNK), I32),
            pltpu.VMEM((SC_SLOTS, SC_CHUNK, ROW_CHUNKS, LANES), src.dtype),
            pltpu.SemaphoreType.DMA((SC_SLOTS,)),
            pltpu.SemaphoreType.DMA((SC_SLOTS,)),
        ],
        name="sc_scatter_rows",
    )
    def scatter(src_hbm, idx_hbm, out_hbm, idx_v, rows_v, rsem, wsem):
        wid = lax.axis_index("s") * SC_CORES + lax.axis_index("c")
        pltpu.sync_copy(idx_hbm.at[wid], idx_v)

        def read(j):
            b = j % SC_SLOTS
            return [pltpu.async_copy(src_hbm.at[pl.ds(wid * per_worker + j * SC_CHUNK, SC_CHUNK)], rows_v.at[b],
                                     rsem.at[b])]

        def write(j):
            b = j % SC_SLOTS
            return [pltpu.async_copy(rows_v.at[b], out_hbm.at[idx_v.at[k * n_chunks + j]], wsem.at[b])
                    for k in range(2)]

        _sc_ring(n_chunks, read, write)

    return scatter(src3, idx).reshape(n_rows * ROW_CHUNKS, LANES)


def _sc_gather_rows(table, idx):
    n = idx.shape[0]
    table3 = table.reshape(-1, ROW_CHUNKS, LANES)
    per_worker = n // SC_WORKERS
    n_chunks = per_worker // SC_CHUNK

    @functools.partial(
        pl.kernel, mesh=_sc_mesh(),
        out_type=jax.ShapeDtypeStruct((n, ROW_CHUNKS, LANES), table.dtype),
        scratch_types=[
            pltpu.VMEM((n_chunks, SC_CHUNK), I32),
            pltpu.VMEM((SC_SLOTS, SC_CHUNK, ROW_CHUNKS, LANES), table.dtype),
            pltpu.SemaphoreType.DMA((SC_SLOTS,)),
            pltpu.SemaphoreType.DMA((SC_SLOTS,)),
        ],
        name="sc_gather_rows",
    )
    def gather(table_hbm, idx_hbm, out_hbm, idx_v, rows_v, rsem, wsem):
        wid = lax.axis_index("s") * SC_CORES + lax.axis_index("c")
        pltpu.sync_copy(idx_hbm.at[wid], idx_v)

        def read(j):
            b = j % SC_SLOTS
            return [pltpu.async_copy(table_hbm.at[idx_v.at[j]], rows_v.at[b], rsem.at[b])]

        def write(j):
            b = j % SC_SLOTS
            return [pltpu.async_copy(rows_v.at[b], out_hbm.at[pl.ds(wid * per_worker + j * SC_CHUNK, SC_CHUNK)],
                                     wsem.at[b])]

        _sc_ring(n_chunks, read, write)

    out = gather(table3, idx.reshape(SC_WORKERS, n_chunks, SC_CHUNK))
    return out.reshape(n * ROW_CHUNKS, LANES)


XS_SLOTS = 4


def _expert_body(plan_ref, xs_hbm, wgu_hbm, wd_hbm, ys_ref,
                 wgu_bf, wd_bf, xbuf, xsem, wgu_buf, wd_buf, wsem, *, e0):
    i = pl.program_id(0)
    n_used = plan_ref[PLAN_USED, 0]
    used = i < n_used
    tile_rows = EXPERT_TILE * ROW_CHUNKS

    def weight_fetch(expert, slot):
        return (pltpu.make_async_copy(wgu_hbm.at[e0 + expert], wgu_buf.at[slot], wsem.at[0, slot]),
                pltpu.make_async_copy(wd_hbm.at[e0 + expert], wd_buf.at[slot], wsem.at[1, slot]))

    @pl.when((i == 0) & used)
    def _():
        for cp in weight_fetch(plan_ref[PLAN_EXPERT, 0], plan_ref[PLAN_SLOT, 0]):
            cp.start()

    @pl.when(used & (plan_ref[PLAN_FIRST, i] == 1))
    def _():
        slot = plan_ref[PLAN_SLOT, i]
        for cp in weight_fetch(plan_ref[PLAN_EXPERT, i], slot):
            cp.wait()
        wgu_bf[...] = wgu_buf[slot].astype(BF16)
        wd_bf[...] = wd_buf[slot].astype(BF16)

        @pl.when(plan_ref[PLAN_NEXT, i] >= 0)
        def _():
            for cp in weight_fetch(plan_ref[PLAN_NEXT, i], 1 - slot):
                cp.start()


    def fetch(tile):
        slot = lax.rem(tile, XS_SLOTS)
        r0 = pl.multiple_of(tile * tile_rows, tile_rows)
        return pltpu.make_async_copy(xs_hbm.at[pl.ds(r0, tile_rows)], xbuf.at[slot], xsem.at[slot])

    ahead = XS_SLOTS - 1
    for tile in range(ahead):
        @pl.when((i == 0) & (tile < n_used))
        def _(tile=tile):
            fetch(tile).start()

    @pl.when(i + ahead < n_used)
    def _():
        fetch(i + ahead).start()

    @pl.when(used)
    def _():
        fetch(i).wait()
        live = lax.broadcasted_iota(I32, (EXPERT_TILE, 1), 0) < plan_ref[PLAN_VALID, i]
        xs = jnp.where(live, _load_rows(xbuf.at[lax.rem(i, XS_SLOTS)], EXPERT_TILE), 0.0).astype(BF16)
        gu = jnp.dot(xs, wgu_bf[...], preferred_element_type=F32)
        gate = gu[:, :EXPERT_FF]
        up = gu[:, EXPERT_FF:]
        act = gate * jax.nn.sigmoid(gate) * up
        _store_rows(ys_ref, jnp.dot(act.astype(BF16), wd_bf[...], preferred_element_type=F32))

    @pl.when(jnp.logical_not(used))
    def _():
        ys_ref[...] = jnp.zeros(ys_ref.shape, ys_ref.dtype)


def _expert_mlp(tile_plan, xs, w_gate_up, w_down, layer):
    d = D_MODEL
    n_tiles = xs.shape[0] // (EXPERT_TILE * ROW_CHUNKS)
    f2 = 2 * EXPERT_FF
    tile_rows = EXPERT_TILE * ROW_CHUNKS
    grid_spec = pltpu.PrefetchScalarGridSpec(
        num_scalar_prefetch=1,
        grid=(n_tiles,),
        in_specs=[
            pl.BlockSpec(memory_space=pl.ANY),
            pl.BlockSpec(memory_space=pl.ANY),
            pl.BlockSpec(memory_space=pl.ANY),
        ],
        out_specs=pl.BlockSpec((tile_rows, LANES), lambda i, plan: (i, 0)),
        scratch_shapes=[pltpu.VMEM((d, f2), BF16), pltpu.VMEM((EXPERT_FF, d), BF16),
                        pltpu.VMEM((XS_SLOTS, tile_rows, LANES), U32), pltpu.SemaphoreType.DMA((XS_SLOTS,)),
                        pltpu.VMEM((2, d, f2), F32), pltpu.VMEM((2, EXPERT_FF, d), F32),
                        pltpu.SemaphoreType.DMA((2, 2))],
    )
    return pl.pallas_call(
        functools.partial(_expert_body, e0=layer * N_EXPERTS),
        grid_spec=grid_spec,
        out_shape=jax.ShapeDtypeStruct(xs.shape, U32),
        compiler_params=_cparams(1),
        name="expert_mlp",
    )(tile_plan, xs, w_gate_up.reshape(DEPTH * N_EXPERTS, d, f2),
      w_down.reshape(DEPTH * N_EXPERTS, EXPERT_FF, d))


def _moe_combine(wts_ref, x_ref, mod_ref, lng_ref, lnb_ref, ya_ref, yb_ref, token0, n):
    d = D_MODEL
    rows = slice(token0, token0 + n)
    cols = []
    for k in range(2):
        wt = jnp.broadcast_to(wts_ref[k:k + 1, rows], (LANES, n)).T
        cols.append(jnp.concatenate([wt] * (d // LANES), axis=1))
    y = cols[0] * _load_rows(ya_ref, n, token0) + cols[1] * _load_rows(yb_ref, n, token0)
    gt_f = mod_ref[:, 5 * d:6 * d]
    return _deepnorm_ln(x_ref[rows, :], y, gt_f, lng_ref[...], lnb_ref[...])


def _combine_specs(tm, n_steps, steps_per_batch, step0=0):
    d = D_MODEL
    return [
        pl.BlockSpec((2, tm), lambda i: (0, step0 + i)),
        pl.BlockSpec((tm, d), lambda i: (step0 + i, 0)),
        pl.BlockSpec((None, 1, N_MOD * d), lambda i: ((step0 + i) // steps_per_batch, 0, 0)),
        pl.BlockSpec((1, d), lambda i: (0, 0)),
        pl.BlockSpec((1, d), lambda i: (0, 0)),
        pl.BlockSpec((tm * ROW_CHUNKS, LANES), lambda i: (i, 0)),
        pl.BlockSpec((tm * ROW_CHUNKS, LANES), lambda i: (n_steps + i, 0)),
    ]


def _gather_rows_of_tokens(ys, pos2):
    return _sc_gather_rows(ys, pos2.reshape(-1))


def _combine_body(wts_ref, x_ref, mod_ref, lng_ref, lnb_ref, ya_ref, yb_ref, o_ref):
    for r in range(x_ref.shape[0] // SUB_ROWS):
        o_ref[r * SUB_ROWS:(r + 1) * SUB_ROWS, :] = _moe_combine(wts_ref, x_ref, mod_ref, lng_ref, lnb_ref, ya_ref,
                                                                 yb_ref, r * SUB_ROWS, SUB_ROWS)


def _combine_ln(moe, mod_rows, ln_g, ln_b, seq):
    wts, x1, ys, pos2 = moe
    yg = _gather_rows_of_tokens(ys, pos2)
    t, d = x1.shape
    tm = 1024
    n_steps = t // tm
    return pl.pallas_call(
        _combine_body,
        grid=(n_steps,),
        in_specs=_combine_specs(tm, n_steps, seq // tm),
        out_specs=pl.BlockSpec((tm, d), lambda i: (i, 0)),
        out_shape=jax.ShapeDtypeStruct((t, d), F32),
        compiler_params=_cparams(1),
        name="moe_combine_ln",
    )(wts, x1, mod_rows, ln_g.reshape(1, d), ln_b.reshape(1, d), yg, yg)


def _gmlp_body(wts_ref, x1_ref, modp_ref, lng_ref, lnb_ref, ya_ref, yb_ref,
               mod_ref, w_ref, b_ref, g_ref, beta_ref, ws_ref, bs_ref, *rest):
    x2_ref, o_ref, ws_bf, w_bf = rest[-4:]
    @pl.when(pl.program_id(0) == 0)
    def _():
        tri = lax.broadcasted_iota(I32, (CHUNK, CHUNK), 0) >= lax.broadcasted_iota(I32, (CHUNK, CHUNK), 1)
        for g in range(N_SGU_GROUPS):
            ws_bf[g] = jnp.where(tri, 0.5 * ws_ref[g], 0.0).astype(BF16)
        w_bf[...] = w_ref[...].astype(BF16)

    d = D_MODEL
    sh = mod_ref[:, 0:d]
    sc = mod_ref[:, d:2 * d]
    sub_rows = x1_ref.shape[0]
    for r in range(x1_ref.shape[0] // sub_rows):
        r0 = r * sub_rows
        x2 = _moe_combine(wts_ref, x1_ref, modp_ref, lng_ref, lnb_ref, ya_ref, yb_ref, r0, sub_rows)
        x2_ref[r0:r0 + sub_rows, :] = x2
        h = x2 * (1.0 + sc) + sh
        z = jnp.dot(h.astype(BF16), w_bf[...], preferred_element_type=F32) + b_ref[...]
        z2 = z * (1.0 + lax.erf(z * (2.0 ** -0.5)))
        u = z2[:, :GMLP_WIDTH]
        v = _layer_norm(z2[:, GMLP_WIDTH:], g_ref[...], beta_ref[...], eps=4.0 * LN_EPS).astype(BF16)
        for ci in range(sub_rows // CHUNK):
            rows = slice(ci * CHUNK, (ci + 1) * CHUNK)
            out_rows = slice(r0 + ci * CHUNK, r0 + (ci + 1) * CHUNK)
            for g in range(N_SGU_GROUPS):
                lanes = slice(g * SGU_GROUP_DIM, (g + 1) * SGU_GROUP_DIM)
                mixed = jnp.dot(ws_bf[g], v[rows, lanes], preferred_element_type=F32) + 0.5 * bs_ref[:, g:g + 1]
                o_ref[out_rows, lanes] = (u[rows, lanes] * mixed).astype(BF16)


def _gmlp_gate(moe, ln_g, ln_b, mod_prev, mod_rows, w_in, b_in, sgu_g, sgu_b, w_s, b_s, seq):
    wts, x1, ys, pos2 = moe
    t, d = x1.shape
    tm = 512
    steps_per_batch = seq // tm
    gw = GMLP_WIDTH
    bounds = (0, t // 4, t)
    assert (t // 4) % tm == 0 and (t // 2) % (SC_WORKERS * SC_CHUNK) == 0
    outs = ()
    for lo, hi in zip(bounds[:-1], bounds[1:]):
        yg = _sc_gather_rows(ys, pos2[:, lo:hi].reshape(-1))
        step0 = lo // tm
        n_steps = (hi - lo) // tm
        n_in = 14
        outs = pl.pallas_call(
            _gmlp_body,
            grid=(n_steps,),
            in_specs=_combine_specs(tm, n_steps, steps_per_batch, step0) + [
                pl.BlockSpec((None, 1, N_MOD * d), lambda i, step0=step0: ((step0 + i) // steps_per_batch, 0, 0)),
                pl.BlockSpec((d, 2 * gw), lambda i: (0, 0)),
                pl.BlockSpec((1, 2 * gw), lambda i: (0, 0)),
                pl.BlockSpec((1, gw), lambda i: (0, 0)),
                pl.BlockSpec((1, gw), lambda i: (0, 0)),
                pl.BlockSpec((N_SGU_GROUPS, CHUNK, CHUNK), lambda i: (0, 0, 0)),
                pl.BlockSpec((CHUNK, N_SGU_GROUPS), lambda i: (0, 0)),
            ] + [pl.BlockSpec(memory_space=pl.ANY)] * len(outs),
            out_specs=[pl.BlockSpec((tm, d), lambda i, step0=step0: (step0 + i, 0)),
                       pl.BlockSpec((tm, gw), lambda i, step0=step0: (step0 + i, 0))],
            out_shape=[jax.ShapeDtypeStruct((t, d), F32), jax.ShapeDtypeStruct((t, gw), BF16)],
            input_output_aliases={n_in + k: k for k in range(len(outs))},
            scratch_shapes=[pltpu.VMEM((N_SGU_GROUPS, CHUNK, CHUNK), BF16), pltpu.VMEM((d, 2 * gw), BF16)],
            compiler_params=_cparams(1),
            name="combine_gmlp_gate",
        )(wts, x1, mod_prev, ln_g.reshape(1, d), ln_b.reshape(1, d), yg, yg,
          mod_rows, w_in, b_in.reshape(1, 2 * gw), sgu_g.reshape(1, gw), sgu_b.reshape(1, gw), w_s, b_s.T, *outs)
    return outs


def _router_params(w_group, b_group, w_expert, b_expert):
    def lanes(group_part, expert_part):
        rows = group_part.shape[0]
        gap = jnp.zeros((rows, EXPERT_COL0 - N_EXPERT_GROUPS), F32)
        tail = jnp.zeros((rows, ROUTER_LANES - EXPERT_COL0 - N_EXPERTS), F32)
        return jnp.concatenate([group_part, gap, expert_part, tail], axis=1)

    return lanes(w_group, w_expert), lanes(b_group[None, :], b_expert[None, :])


def _moe_experts(x1, h2, ids, wts, layer, w_gate_up, w_down):
    t = x1.shape[0]
    n_rows = 2 * t + N_EXPERTS * EXPERT_TILE
    n_tiles = n_rows // EXPERT_TILE
    pos, tile_plan = _expert_sort(ids, n_tiles)
    pos2 = pos.reshape(2, t)
    xs = _sc_scatter_rows(h2, pos2, n_rows)
    ys = _expert_mlp(tile_plan, xs, w_gate_up, w_down, layer)
    return wts, x1, ys, pos2


def kernel(x, c, positions, ada_w, ada_b, post_ln_g, post_ln_b, attn_w_qkv, attn_b_qkv, attn_sinks, attn_w_o, attn_b_o, gmlp_w_in, gmlp_b_in, gmlp_sgu_ln_g, gmlp_sgu_ln_b, gmlp_w_s, gmlp_b_s, gmlp_w_out, gmlp_b_out, moe_w_group_router, moe_b_group_router, moe_w_expert_router, moe_b_expert_router, moe_w_gate_up, moe_w_down):
    batch, seq, d = x.shape
    t = batch * seq
    assert d == D_MODEL and batch <= MOD_ROWS and ada_w.shape[0] == DEPTH == 2
    assert seq % 1024 == 0, "token tiles of 1024 rows must not straddle sequences"
    assert t % (SC_WORKERS * SC_CHUNK) == 0, "every SparseCore subcore moves whole chunks"
    x2 = x.reshape(t, d)
    c_pad = jnp.pad(c, ((0, MOD_ROWS - batch), (0, 0)))
    mods = [_adaln_mod(c_pad, ada_w, ada_b, layer, batch) for layer in range(DEPTH)]

    moe = None
    for layer in range(DEPTH):
        j = layer // 2
        if layer % 2 == 0:
            if moe is not None:
                x2 = _combine_ln(moe, mods[layer - 1], post_ln_g[layer - 1, 1], post_ln_b[layer - 1, 1], seq)
            qkv = _qkv_rope(x2, mods[layer], positions, attn_w_qkv[j], attn_b_qkv[j], seq)
            mix = _attention(qkv, attn_sinks[j], batch, seq)
            w_out, b_out = attn_w_o[j], attn_b_o[j]
        else:
            x2, mix = _gmlp_gate(moe, post_ln_g[layer - 1, 1], post_ln_b[layer - 1, 1], mods[layer - 1], mods[layer],
                                 gmlp_w_in[j], gmlp_b_in[j], gmlp_sgu_ln_g[j], gmlp_sgu_ln_b[j],
                                 gmlp_w_s[j], gmlp_b_s[j], seq)
            w_out, b_out = gmlp_w_out[j], gmlp_b_out[j]
        w_router, b_router = _router_params(moe_w_group_router[layer], moe_b_group_router[layer],
                                            moe_w_expert_router[layer], moe_b_expert_router[layer])
        x1, h2, ids, wts = _proj_ln_router(mix, x2, mods[layer], w_out, b_out, post_ln_g[layer, 0],
                                           post_ln_b[layer, 0], w_router, b_router, seq)
        moe = _moe_experts(x1, h2, ids, wts, layer, moe_w_gate_up, moe_w_down)
    x2 = _combine_ln(moe, mods[DEPTH - 1], post_ln_g[DEPTH - 1, 1], post_ln_b[DEPTH - 1, 1], seq)
    return x2.reshape(batch, seq, d)
```

```python
import functools

import jax
import jax.numpy as jnp
from jax import lax
from jax.experimental import pallas as pl
from jax.experimental.pallas import tpu as pltpu
from jax.experimental.pallas import tpu_sc as plsc

F32 = jnp.float32
BF16 = jnp.bfloat16
I32 = jnp.int32

D_MODEL = 1024
DEPTH = 2
HEAD_DIM = 64
N_Q_HEADS = 16
N_KV_HEADS = 4
GQA_GROUP = N_Q_HEADS // N_KV_HEADS
WINDOW = 128
ROPE_THETA = 10000.0
Q_WIDTH = N_Q_HEADS * HEAD_DIM
KV_WIDTH = N_KV_HEADS * HEAD_DIM
QKV_WIDTH = Q_WIDTH + 2 * KV_WIDTH
CHUNK = 128
GMLP_WIDTH = 2 * D_MODEL
N_SGU_GROUPS = 8
SGU_GROUP_DIM = GMLP_WIDTH // N_SGU_GROUPS
N_EXPERT_GROUPS = 4
EXPERTS_PER_GROUP = 8
N_EXPERTS = N_EXPERT_GROUPS * EXPERTS_PER_GROUP
EXPERT_FF = D_MODEL // 4
N_MOD = 6
DEEPNORM_ALPHA = (2.0 * DEPTH) ** 0.25
LN_EPS = 1e-5

LANES = 128
MOD_ROWS = 8
ROUTER_LANES = 128
EXPERT_COL0 = 8
SORT_CHUNK = 256
EXPERT_TILE = 512
SUB_ROWS = 256
VMEM_LIMIT = 56 * 1024 * 1024
SC_CORES = 2
SC_WORKERS = 32
SC_CHUNK = 64
SC_SLOTS = 3


def _cparams(n_axes, vmem=VMEM_LIMIT):
    return pltpu.CompilerParams(dimension_semantics=("arbitrary",) * n_axes, vmem_limit_bytes=vmem)


U32 = jnp.uint32
ROW_CHUNKS = D_MODEL // 2 // LANES


def _store_rows(ref, val, token0=0):
    n = val.shape[0]
    half = D_MODEL // 2
    words = pltpu.pack_elementwise([val[:, :half], val[:, half:]], packed_dtype=BF16)
    for c in range(ROW_CHUNKS):
        ref[pl.ds(token0 * ROW_CHUNKS + c, n, stride=ROW_CHUNKS), :] = words[:, c * LANES:(c + 1) * LANES]


def _load_rows(ref, n, token0=0):
    words = jnp.concatenate([ref[pl.ds(token0 * ROW_CHUNKS + c, n, stride=ROW_CHUNKS), :]
                             for c in range(ROW_CHUNKS)], axis=1)
    lo = lax.bitcast_convert_type(words << 16, F32)
    hi = lax.bitcast_convert_type(words & jnp.uint32(0xFFFF0000), F32)
    return jnp.concatenate([lo, hi], axis=1)


def _layer_norm(r, g, b, eps=LN_EPS):
    mu = jnp.mean(r, axis=-1, keepdims=True)
    d = r - mu
    var = jnp.mean(d * d, axis=-1, keepdims=True)
    return d * lax.rsqrt(var + eps) * g + b


def _deepnorm_ln(x, branch, gate_row, g, b):
    scale = (1.0 + gate_row) * (1.0 / DEEPNORM_ALPHA)
    return _layer_norm(x + scale * branch, g, b, eps=LN_EPS / (DEEPNORM_ALPHA * DEEPNORM_ALPHA))


def _mod_body(c_ref, w_ref, b_ref, o_ref):
    c = c_ref[...]
    ca = c * jax.nn.sigmoid(c)
    mod = jnp.dot(ca.astype(BF16), w_ref[...].astype(BF16), preferred_element_type=F32) + b_ref[...]
    for r in range(o_ref.shape[0]):
        o_ref[r] = mod[r:r + 1]


def _adaln_mod(c_pad, ada_w, ada_b, layer, batch):
    tn = 1536
    n_out = N_MOD * D_MODEL
    return pl.pallas_call(
        _mod_body,
        grid=(n_out // tn,),
        in_specs=[
            pl.BlockSpec((MOD_ROWS, D_MODEL), lambda j: (0, 0)),
            pl.BlockSpec((None, D_MODEL, tn), lambda j: (layer, 0, j)),
            pl.BlockSpec((None, 1, tn), lambda j: (layer, 0, j)),
        ],
        out_specs=pl.BlockSpec((batch, 1, tn), lambda j: (0, 0, j)),
        out_shape=jax.ShapeDtypeStruct((batch, 1, n_out), F32),
        compiler_params=_cparams(1),
        name="adaln_mod",
    )(c_pad, ada_w, ada_b.reshape(DEPTH, 1, n_out))


def _qkv_body(x_ref, mod_ref, pos_ref, w_ref, b_ref, invf_ref, o_ref, wbf_ref):
    @pl.when(pl.program_id(0) == 0)
    def _():
        wbf_ref[...] = w_ref[...].astype(BF16)

    tm = x_ref.shape[0]
    sh = mod_ref[:, 0:D_MODEL]
    sc = mod_ref[:, D_MODEL:2 * D_MODEL]
    h = x_ref[...] * (1.0 + sc) + sh
    qkv = jnp.dot(h.astype(BF16), wbf_ref[...], preferred_element_type=F32) + b_ref[...]

    ang = invf_ref[...] * pos_ref[...].astype(F32)
    c = jnp.cos(ang)
    s = jnp.sin(ang)
    ct = jnp.concatenate([c, c, c, c], axis=0).T
    st = jnp.concatenate([-s, s, -s, s], axis=0).T
    lane = lax.broadcasted_iota(I32, (tm, LANES), 1)
    first_half = (lane & (HEAD_DIM // 2)) == 0
    n_rope = (Q_WIDTH + KV_WIDTH) // LANES
    for j in range(n_rope):
        blk = qkv[:, j * LANES:(j + 1) * LANES]
        rot = jnp.where(first_half, pltpu.roll(blk, LANES - HEAD_DIM // 2, 1), pltpu.roll(blk, HEAD_DIM // 2, 1))
        r = blk * ct + rot * st
        if j < Q_WIDTH // LANES:
            r = r * (HEAD_DIM ** -0.5)
        o_ref[:, j * LANES:(j + 1) * LANES] = r.astype(BF16)
    o_ref[:, Q_WIDTH + KV_WIDTH:] = qkv[:, Q_WIDTH + KV_WIDTH:].astype(BF16)


def _qkv_rope(x2, mod_rows, positions, w_qkv, b_qkv, seq):
    t = x2.shape[0]
    tm = 1024
    steps_per_batch = seq // tm
    inv_freq = ROPE_THETA ** (-jnp.arange(0, HEAD_DIM, 2, dtype=F32) / HEAD_DIM)
    return pl.pallas_call(
        _qkv_body,
        grid=(t // tm,),
        in_specs=[
            pl.BlockSpec((tm, D_MODEL), lambda i: (i, 0)),
            pl.BlockSpec((None, 1, N_MOD * D_MODEL), lambda i: (i // steps_per_batch, 0, 0)),
            pl.BlockSpec((None, 1, tm), lambda i: (i, 0, 0)),
            pl.BlockSpec((D_MODEL, QKV_WIDTH), lambda i: (0, 0)),
            pl.BlockSpec((1, QKV_WIDTH), lambda i: (0, 0)),
            pl.BlockSpec((HEAD_DIM // 2, 1), lambda i: (0, 0)),
        ],
        out_specs=pl.BlockSpec((tm, QKV_WIDTH), lambda i: (i, 0)),
        out_shape=jax.ShapeDtypeStruct((t, QKV_WIDTH), BF16),
        scratch_shapes=[pltpu.VMEM((D_MODEL, QKV_WIDTH), BF16)],
        compiler_params=_cparams(1),
        name="qkv_rope",
    )(x2, mod_rows, positions.reshape(t // tm, 1, tm), w_qkv, b_qkv.reshape(1, QKV_WIDTH),
      inv_freq.reshape(HEAD_DIM // 2, 1))


BF16_ROWS = 16
BF16_ONE_PAIR = 0x3F803F80


def _attn_prepare(kv, kab_ref, vx_ref, slot):
    words = pltpu.bitcast(kv, U32)
    low = lax.broadcasted_iota(I32, (WINDOW // 2, LANES), 1) < HEAD_DIM
    zeros = jnp.zeros((WINDOW // 2, LANES), U32)
    ones = jnp.full((WINDOW // 2, LANES), BF16_ONE_PAIR, U32)
    for g in range(N_KV_HEADS):
        for part, ref in ((0, None), (KV_WIDTH, vx_ref)):
            tile = words[:, part + (g // 2) * LANES:part + (g // 2 + 1) * LANES]
            other = pltpu.roll(tile, HEAD_DIM, 1)
            in_low, in_high = (tile, other) if g % 2 == 0 else (other, tile)
            if ref is None:
                kab_ref[slot, 2 * g] = pltpu.bitcast(jnp.where(low, in_low, zeros), BF16)
                kab_ref[slot, 2 * g + 1] = pltpu.bitcast(jnp.where(low, zeros, in_high), BF16)
            else:
                both = jnp.where(low, in_low, in_high)
                vx_ref[slot, g] = pltpu.bitcast(jnp.concatenate([both, ones], axis=1), BF16)


def _attn_block(sink_ref, q, kab_ref, vx_ref, s_ref, p_ref, prev, cur, first_block):
    for g in range(N_KV_HEADS):
        q_pair = jnp.concatenate([q[:, (2 * g) * LANES:(2 * g + 1) * LANES],
                                  q[:, (2 * g + 1) * LANES:(2 * g + 2) * LANES]], axis=0)
        for a in range(2):
            kband = jnp.concatenate([kab_ref[prev, 2 * g + a], kab_ref[cur, 2 * g + a]], axis=0)
            s = lax.dot_general(q_pair, kband, (((1,), (1,)), ((), ())), preferred_element_type=F32)
            s_ref[GQA_GROUP * g + a] = s[:WINDOW]
            s_ref[GQA_GROUP * g + 2 + a] = s[WINDOW:]
    qi = lax.broadcasted_iota(I32, (WINDOW, 2 * WINDOW), 0) + WINDOW
    kj = lax.broadcasted_iota(I32, (WINDOW, 2 * WINDOW), 1)
    mask = (kj <= qi) & (kj > qi - WINDOW) & ((kj >= WINDOW) | jnp.logical_not(first_block))
    key0 = lax.broadcasted_iota(I32, (1, 2 * WINDOW), 1) == 0
    for h in range(N_Q_HEADS):
        s = jnp.where(mask, s_ref[h], jnp.where(key0, sink_ref[h], -jnp.inf))
        m = jnp.max(s, axis=-1, keepdims=True)
        p_ref[h] = jnp.exp(s - m).astype(BF16)
    low = lax.broadcasted_iota(I32, (WINDOW, LANES), 1) < HEAD_DIM
    sink_row = ((lax.broadcasted_iota(I32, (BF16_ROWS, 2 * LANES), 0) == 0)
                & (lax.broadcasted_iota(I32, (BF16_ROWS, 2 * LANES), 1) < LANES))
    out_tiles = []
    for g in range(N_KV_HEADS):
        v_prev = vx_ref[prev, g]
        v_head = jnp.where(sink_row, 0.0, v_prev[:BF16_ROWS].astype(F32)).astype(BF16)
        vband = jnp.concatenate([v_head, v_prev[BF16_ROWS:], vx_ref[cur, g]], axis=0)
        p4 = p_ref[GQA_GROUP * g:GQA_GROUP * (g + 1)].reshape(GQA_GROUP * WINDOW, 2 * WINDOW)
        o4 = jnp.dot(p4, vband, preferred_element_type=F32)
        heads = []
        for j in range(GQA_GROUP):
            blk = o4[j * WINDOW:(j + 1) * WINDOW]
            heads.append(blk[:, :LANES] / blk[:, LANES:])
        out_tiles.append(jnp.where(low, heads[0], heads[1]))
        out_tiles.append(jnp.where(low, heads[2], heads[3]))
    return jnp.concatenate(out_tiles, axis=1).astype(BF16)


def _attn_body(sink_ref, q_ref, kv_ref, o_ref, kab_ref, vx_ref, s_ref, p_ref):
    n = pl.program_id(1)

    @pl.when(n == 0)
    def _():
        kab_ref[1] = jnp.zeros(kab_ref.shape[1:], kab_ref.dtype)
        half = (N_KV_HEADS, WINDOW, LANES)
        vx_ref[1] = jnp.concatenate([jnp.zeros(half, BF16), jnp.ones(half, BF16)], axis=-1)

    scratch = (kab_ref, vx_ref, s_ref, p_ref)
    for blk in range(q_ref.shape[0] // WINDOW):
        rows = slice(blk * WINDOW, (blk + 1) * WINDOW)
        cur = blk % 2
        _attn_prepare(kv_ref[rows, :], kab_ref, vx_ref, cur)
        o_ref[rows, :] = _attn_block(sink_ref, q_ref[rows, :], *scratch, 1 - cur, cur,
                                     (n == 0) if blk == 0 else False)


def _attention(qkv, sinks, batch, seq):
    t = qkv.shape[0]
    tq = 8 * WINDOW
    steps = seq // tq
    kv_col = Q_WIDTH // (2 * KV_WIDTH)
    return pl.pallas_call(
        _attn_body,
        grid=(batch, steps),
        in_specs=[
            pl.BlockSpec(memory_space=pltpu.SMEM),
            pl.BlockSpec((tq, Q_WIDTH), lambda b, n: (b * steps + n, 0)),
            pl.BlockSpec((tq, 2 * KV_WIDTH), lambda b, n: (b * steps + n, kv_col)),
        ],
        out_specs=pl.BlockSpec((tq, Q_WIDTH), lambda b, n: (b * steps + n, 0)),
        out_shape=jax.ShapeDtypeStruct((t, Q_WIDTH), BF16),
        scratch_shapes=[
            pltpu.VMEM((2, 2 * N_KV_HEADS, WINDOW, LANES), BF16),
            pltpu.VMEM((2, N_KV_HEADS, WINDOW, 2 * LANES), BF16),
            pltpu.VMEM((N_Q_HEADS, WINDOW, 2 * WINDOW), F32),
            pltpu.VMEM((N_Q_HEADS, WINDOW, 2 * WINDOW), BF16),
        ],
        compiler_params=_cparams(2),
        name="swa_attention",
    )(sinks, qkv, qkv)


def _route(lt):
    tm = lt.shape[1]
    row = lax.broadcasted_iota(I32, (EXPERTS_PER_GROUP, tm), 0)
    neg = -jnp.inf
    gl = jnp.where(row < N_EXPERT_GROUPS, lt[0:EXPERTS_PER_GROUP], neg)
    gm = jnp.max(gl, axis=0, keepdims=True)
    g_p = 1.0 / jnp.sum(jnp.exp(gl - gm), axis=0, keepdims=True)
    g_idx = jnp.min(jnp.where(gl == gm, row, EXPERTS_PER_GROUP), axis=0, keepdims=True)
    sel = lt[EXPERT_COL0 + (N_EXPERT_GROUPS - 1) * EXPERTS_PER_GROUP:EXPERT_COL0 + N_EXPERTS]
    for g in range(N_EXPERT_GROUPS - 2, -1, -1):
        lo = EXPERT_COL0 + g * EXPERTS_PER_GROUP
        sel = jnp.where(g_idx == g, lt[lo:lo + EXPERTS_PER_GROUP], sel)
    v1 = jnp.max(sel, axis=0, keepdims=True)
    i1 = jnp.min(jnp.where(sel == v1, row, EXPERTS_PER_GROUP), axis=0, keepdims=True)
    sel2 = jnp.where(row == i1, neg, sel)
    v2 = jnp.max(sel2, axis=0, keepdims=True)
    i2 = jnp.min(jnp.where(sel2 == v2, row, EXPERTS_PER_GROUP), axis=0, keepdims=True)
    e2 = jnp.exp(v2 - v1)
    w1 = g_p / (1.0 + e2)
    w2 = g_p * e2 / (1.0 + e2)
    base = g_idx * EXPERTS_PER_GROUP
    return base + i1, base + i2, w1, w2


def _proj_body(o_ref, x_ref, mod_ref, w_ref, b_ref, lng_ref, lnb_ref, wr_ref, br_ref,
               x1_ref, h2_ref, ids_ref, wts_ref, wbf_ref):
    @pl.when(pl.program_id(0) == 0)
    def _():
        wbf_ref[...] = w_ref[...].astype(BF16)

    d = D_MODEL
    gt_m = mod_ref[:, 2 * d:3 * d]
    sh_f = mod_ref[:, 3 * d:4 * d]
    sc_f = mod_ref[:, 4 * d:5 * d]
    wr = wr_ref[...].astype(BF16)
    y_all = jnp.dot(o_ref[...], wbf_ref[...], preferred_element_type=F32)
    for r in range(x_ref.shape[0] // SUB_ROWS):
        rows = slice(r * SUB_ROWS, (r + 1) * SUB_ROWS)
        y = y_all[rows, :] + b_ref[...]
        x1 = _deepnorm_ln(x_ref[rows, :], y, gt_m, lng_ref[...], lnb_ref[...])
        x1_ref[rows, :] = x1
        h2 = x1 * (1.0 + sc_f) + sh_f
        _store_rows(h2_ref, h2, r * SUB_ROWS)
        logits = jnp.dot(h2.astype(BF16), wr, preferred_element_type=F32) + br_ref[...]
        ea, eb, wa, wb = _route(logits.T)
        ids_ref[0:1, rows] = ea
        ids_ref[1:2, rows] = eb
        wts_ref[0:1, rows] = wa
        wts_ref[1:2, rows] = wb


def _proj_ln_router(o, x2, mod_rows, w, b, ln_g, ln_b, w_router, b_router, seq):
    t, k = o.shape
    tm = 1024
    steps_per_batch = seq // tm
    d = D_MODEL
    return pl.pallas_call(
        _proj_body,
        grid=(t // tm,),
        in_specs=[
            pl.BlockSpec((tm, k), lambda i: (i, 0)),
            pl.BlockSpec((tm, d), lambda i: (i, 0)),
            pl.BlockSpec((None, 1, N_MOD * d), lambda i: (i // steps_per_batch, 0, 0)),
            pl.BlockSpec((k, d), lambda i: (0, 0), pipeline_mode=pl.Buffered(1)),
            pl.BlockSpec((1, d), lambda i: (0, 0)),
            pl.BlockSpec((1, d), lambda i: (0, 0)),
            pl.BlockSpec((1, d), lambda i: (0, 0)),
            pl.BlockSpec((d, ROUTER_LANES), lambda i: (0, 0)),
            pl.BlockSpec((1, ROUTER_LANES), lambda i: (0, 0)),
        ],
        out_specs=[
            pl.BlockSpec((tm, d), lambda i: (i, 0)),
            pl.BlockSpec((tm * ROW_CHUNKS, LANES), lambda i: (i, 0)),
            pl.BlockSpec((2, tm), lambda i: (0, i)),
            pl.BlockSpec((2, tm), lambda i: (0, i)),
        ],
        out_shape=[
            jax.ShapeDtypeStruct((t, d), F32),
            jax.ShapeDtypeStruct((t * ROW_CHUNKS, LANES), U32),
            jax.ShapeDtypeStruct((2, t), I32),
            jax.ShapeDtypeStruct((2, t), F32),
        ],
        scratch_shapes=[pltpu.VMEM((k, d), BF16)],
        compiler_params=_cparams(1),
        name="proj_ln_router",
    )(o, x2, mod_rows, w, b.reshape(1, d), ln_g.reshape(1, d), ln_b.reshape(1, d), w_router, b_router)


PLAN_EXPERT, PLAN_USED, PLAN_VALID, PLAN_FIRST, PLAN_NEXT, PLAN_SLOT = range(6)
PLAN_ROWS = 8


def _sort_body(ids_ref, pos_ref, plan_ref, rank_ref):
    n_rows = ids_ref.shape[0]
    c = SORT_CHUNK
    erow = lax.broadcasted_iota(I32, (N_EXPERTS, c), 0)
    tri = (lax.broadcasted_iota(I32, (c, c), 0) <= lax.broadcasted_iota(I32, (c, c), 1)).astype(BF16)

    def rank_step(r, carry):
        onehot = erow == ids_ref[pl.ds(r, 1), :]
        pref = jnp.dot(onehot.astype(BF16), tri, preferred_element_type=F32)
        rank = jnp.sum(jnp.where(onehot, pref + carry, 0.0), axis=0, keepdims=True) - 1.0
        rank_ref[pl.ds(r, 1), :] = rank
        return carry + pref[:, c - 1:c]

    counts = lax.fori_loop(0, n_rows, rank_step, jnp.zeros((N_EXPERTS, 1), F32), unroll=8)
    n_tile = jnp.floor((counts + (EXPERT_TILE - 1)) * (1.0 / EXPERT_TILE))
    low = (lax.broadcasted_iota(I32, (N_EXPERTS, N_EXPERTS), 1)
           <= lax.broadcasted_iota(I32, (N_EXPERTS, N_EXPERTS), 0)).astype(BF16)
    cum = jnp.dot(low, jnp.broadcast_to(n_tile, (N_EXPERTS, LANES)).astype(BF16),
                  preferred_element_type=F32)[:, 0:1]
    row_off = (cum - n_tile) * EXPERT_TILE

    def pos_step(r, _):
        onehot = erow == ids_ref[pl.ds(r, 1), :]
        off = jnp.sum(jnp.where(onehot, row_off, 0.0), axis=0, keepdims=True)
        pos_ref[pl.ds(r, 1), :] = (off + rank_ref[pl.ds(r, 1), :]).astype(I32)
        return 0

    lax.fori_loop(0, n_rows, pos_step, 0, unroll=8)
    total = jnp.max(cum, axis=0, keepdims=True)
    n_lanes = plan_ref.shape[1]
    tile = jnp.minimum(lax.broadcasted_iota(I32, (N_EXPERTS, n_lanes), 1).astype(F32), total - 1.0)
    plan_ref[PLAN_EXPERT:PLAN_EXPERT + 1, :] = jnp.sum(jnp.where(cum <= tile, 1.0, 0.0), axis=0,
                                                       keepdims=True).astype(I32)
    plan_ref[PLAN_USED:PLAN_USED + 1, :] = jnp.broadcast_to(total, (1, n_lanes)).astype(I32)
    tile_f = lax.broadcasted_iota(I32, (N_EXPERTS, n_lanes), 1).astype(F32)
    first = cum - n_tile
    rows_left = jnp.clip(counts - (tile_f - first) * EXPERT_TILE, 0.0, float(EXPERT_TILE))
    owns = (first <= tile_f) & (tile_f < cum)
    plan_ref[PLAN_VALID:PLAN_VALID + 1, :] = jnp.sum(jnp.where(owns, rows_left, 0.0), axis=0,
                                                     keepdims=True).astype(I32)
    def lane_sum(x):
        return jnp.sum(x, axis=0, keepdims=True)

    is_first = lane_sum(jnp.where(owns & (tile_f == first), 1.0, 0.0))
    group_end = lane_sum(jnp.where(owns, cum, 0.0))
    next_expert = jnp.where(group_end < total, lane_sum(jnp.where(cum <= group_end, 1.0, 0.0)), -1.0)
    ordinal = lane_sum(jnp.where((cum <= tile_f[0:1]) & (n_tile > 0.0), 1.0, 0.0))
    slot = ordinal - 2.0 * jnp.floor(ordinal * 0.5)
    plan_ref[PLAN_FIRST:PLAN_FIRST + 1, :] = is_first.astype(I32)
    plan_ref[PLAN_NEXT:PLAN_NEXT + 1, :] = next_expert.astype(I32)
    plan_ref[PLAN_SLOT:PLAN_SLOT + 1, :] = slot.astype(I32)
    plan_ref[PLAN_SLOT + 1:PLAN_ROWS, :] = jnp.zeros((PLAN_ROWS - PLAN_SLOT - 1, n_lanes), I32)


def _expert_sort(ids, n_tiles):
    n_assign = ids.shape[0] * ids.shape[1]
    n_rows = n_assign // SORT_CHUNK
    te_lanes = -(-n_tiles // LANES) * LANES
    return pl.pallas_call(
        _sort_body,
        grid=(1,),
        in_specs=[pl.BlockSpec((n_rows, SORT_CHUNK), lambda i: (0, 0))],
        out_specs=[
            pl.BlockSpec((n_rows, SORT_CHUNK), lambda i: (0, 0)),
            pl.BlockSpec((PLAN_ROWS, te_lanes), lambda i: (0, 0)),
        ],
        out_shape=[
            jax.ShapeDtypeStruct((n_rows, SORT_CHUNK), I32),
            jax.ShapeDtypeStruct((PLAN_ROWS, te_lanes), I32),
        ],
        scratch_shapes=[pltpu.VMEM((n_rows, SORT_CHUNK), F32)],
        compiler_params=_cparams(1),
        name="expert_sort",
    )(ids.reshape(n_rows, SORT_CHUNK))


def _sc_mesh():
    return plsc.VectorSubcoreMesh(core_axis_name="c", subcore_axis_name="s", num_cores=SC_CORES,
                                  num_subcores=SC_WORKERS // SC_CORES)


def _sc_ring(n_chunks, read, write):
    reads, writes = {}, {}
    for j in range(min(SC_SLOTS - 1, n_chunks)):
        reads[j] = read(j)
    for j in range(n_chunks):
        for cp in reads.pop(j):
            cp.wait()
        nxt = j + SC_SLOTS - 1
        if nxt < n_chunks:
            for cp in writes.pop(nxt - SC_SLOTS, []):
                cp.wait()
            reads[nxt] = read(nxt)
        writes[j] = write(j)
    for cps in writes.values():
        for cp in cps:
            cp.wait()


def _sc_scatter_rows(src, pos2, n_rows):
    t = pos2.shape[1]
    src3 = src.reshape(t, ROW_CHUNKS, LANES)
    per_worker = t // SC_WORKERS
    n_chunks = per_worker // SC_CHUNK
    idx = pos2.reshape(2, SC_WORKERS, n_chunks, SC_CHUNK).transpose(1, 0, 2, 3)
    idx = idx.reshape(SC_WORKERS, 2 * n_chunks, SC_CHUNK)

    @functools.partial(
        pl.kernel, mesh=_sc_mesh(),
        out_type=jax.ShapeDtypeStruct((n_rows, ROW_CHUNKS, LANES), src.dtype),
        scratch_types=[
            pltpu.VMEM((2 * n_chunks, SC_CHUNK), I32),
            pltpu.VMEM((SC_SLOTS, SC_CHUNK, ROW_CHUNKS, LANES), src.dtype),
            pltpu.SemaphoreType.DMA((SC_SLOTS,)),
            pltpu.SemaphoreType.DMA((SC_SLOTS,)),
        ],
        name="sc_scatter_rows",
    )
    def scatter(src_hbm, idx_hbm, out_hbm, idx_v, rows_v, rsem, wsem):
        wid = lax.axis_index("s") * SC_CORES + lax.axis_index("c")
        pltpu.sync_copy(idx_hbm.at[wid], idx_v)

        def read(j):
            b = j % SC_SLOTS
            return [pltpu.async_copy(src_hbm.at[pl.ds(wid * per_worker + j * SC_CHUNK, SC_CHUNK)], rows_v.at[b],
                                     rsem.at[b])]

        def write(j):
            b = j % SC_SLOTS
            return [pltpu.async_copy(rows_v.at[b], out_hbm.at[idx_v.at[k * n_chunks + j]], wsem.at[b])
                    for k in range(2)]

        _sc_ring(n_chunks, read, write)

    return scatter(src3, idx).reshape(n_rows * ROW_CHUNKS, LANES)


def _sc_gather_rows(table, idx):
    n = idx.shape[0]
    table3 = table.reshape(-1, ROW_CHUNKS, LANES)
    per_worker = n // SC_WORKERS
    n_chunks = per_worker // SC_CHUNK

    @functools.partial(
        pl.kernel, mesh=_sc_mesh(),
        out_type=jax.ShapeDtypeStruct((n, ROW_CHUNKS, LANES), table.dtype),
        scratch_types=[
            pltpu.VMEM((n_chunks, SC_CHUNK), I32),
            pltpu.VMEM((SC_SLOTS, SC_CHUNK, ROW_CHUNKS, LANES), table.dtype),
            pltpu.SemaphoreType.DMA((SC_SLOTS,)),
            pltpu.SemaphoreType.DMA((SC_SLOTS,)),
        ],
        name="sc_gather_rows",
    )
    def gather(table_hbm, idx_hbm, out_hbm, idx_v, rows_v, rsem, wsem):
        wid = lax.axis_index("s") * SC_CORES + lax.axis_index("c")
        pltpu.sync_copy(idx_hbm.at[wid], idx_v)

        def read(j):
            b = j % SC_SLOTS
            return [pltpu.async_copy(table_hbm.at[idx_v.at[j]], rows_v.at[b], rsem.at[b])]

        def write(j):
            b = j % SC_SLOTS
            return [pltpu.async_copy(rows_v.at[b], out_hbm.at[pl.ds(wid * per_worker + j * SC_CHUNK, SC_CHUNK)],
                                     wsem.at[b])]

        _sc_ring(n_chunks, read, write)

    out = gather(table3, idx.reshape(SC_WORKERS, n_chunks, SC_CHUNK))
    return out.reshape(n * ROW_CHUNKS, LANES)


XS_SLOTS = 4


def _expert_body(plan_ref, xs_hbm, wgu_hbm, wd_hbm, ys_ref,
                 wgu_bf, wd_bf, xbuf, xsem, wgu_buf, wd_buf, wsem, *, e0):
    i = pl.program_id(0)
    n_used = plan_ref[PLAN_USED, 0]
    used = i < n_used
    tile_rows = EXPERT_TILE * ROW_CHUNKS

    def weight_fetch(expert, slot):
        return (pltpu.make_async_copy(wgu_hbm.at[e0 + expert], wgu_buf.at[slot], wsem.at[0, slot]),
                pltpu.make_async_copy(wd_hbm.at[e0 + expert], wd_buf.at[slot], wsem.at[1, slot]))

    @pl.when((i == 0) & used)
    def _():
        for cp in weight_fetch(plan_ref[PLAN_EXPERT, 0], plan_ref[PLAN_SLOT, 0]):
            cp.start()

    @pl.when(used & (plan_ref[PLAN_FIRST, i] == 1))
    def _():
        slot = plan_ref[PLAN_SLOT, i]
        for cp in weight_fetch(plan_ref[PLAN_EXPERT, i], slot):
            cp.wait()
        wgu_bf[...] = wgu_buf[slot].astype(BF16)
        wd_bf[...] = wd_buf[slot].astype(BF16)

        @pl.when(plan_ref[PLAN_NEXT, i] >= 0)
        def _():
            for cp in weight_fetch(plan_ref[PLAN_NEXT, i], 1 - slot):
                cp.start()


    def fetch(tile):
        slot = lax.rem(tile, XS_SLOTS)
        r0 = pl.multiple_of(tile * tile_rows, tile_rows)
        return pltpu.make_async_copy(xs_hbm.at[pl.ds(r0, tile_rows)], xbuf.at[slot], xsem.at[slot])

    ahead = XS_SLOTS - 1
    for tile in range(ahead):
        @pl.when((i == 0) & (tile < n_used))
        def _(tile=tile):
            fetch(tile).start()

    @pl.when(i + ahead < n_used)
    def _():
        fetch(i + ahead).start()

    @pl.when(used)
    def _():
        fetch(i).wait()
        live = lax.broadcasted_iota(I32, (EXPERT_TILE, 1), 0) < plan_ref[PLAN_VALID, i]
        xs = jnp.where(live, _load_rows(xbuf.at[lax.rem(i, XS_SLOTS)], EXPERT_TILE), 0.0).astype(BF16)
        gu = jnp.dot(xs, wgu_bf[...], preferred_element_type=F32)
        gate = gu[:, :EXPERT_FF]
        up = gu[:, EXPERT_FF:]
        act = gate * jax.nn.sigmoid(gate) * up
        _store_rows(ys_ref, jnp.dot(act.astype(BF16), wd_bf[...], preferred_element_type=F32))

    @pl.when(jnp.logical_not(used))
    def _():
        ys_ref[...] = jnp.zeros(ys_ref.shape, ys_ref.dtype)


def _expert_mlp(tile_plan, xs, w_gate_up, w_down, layer):
    d = D_MODEL
    n_tiles = xs.shape[0] // (EXPERT_TILE * ROW_CHUNKS)
    f2 = 2 * EXPERT_FF
    tile_rows = EXPERT_TILE * ROW_CHUNKS
    grid_spec = pltpu.PrefetchScalarGridSpec(
        num_scalar_prefetch=1,
        grid=(n_tiles,),
        in_specs=[
            pl.BlockSpec(memory_space=pl.ANY),
            pl.BlockSpec(memory_space=pl.ANY),
            pl.BlockSpec(memory_space=pl.ANY),
        ],
        out_specs=pl.BlockSpec((tile_rows, LANES), lambda i, plan: (i, 0)),
        scratch_shapes=[pltpu.VMEM((d, f2), BF16), pltpu.VMEM((EXPERT_FF, d), BF16),
                        pltpu.VMEM((XS_SLOTS, tile_rows, LANES), U32), pltpu.SemaphoreType.DMA((XS_SLOTS,)),
                        pltpu.VMEM((2, d, f2), F32), pltpu.VMEM((2, EXPERT_FF, d), F32),
                        pltpu.SemaphoreType.DMA((2, 2))],
    )
    return pl.pallas_call(
        functools.partial(_expert_body, e0=layer * N_EXPERTS),
        grid_spec=grid_spec,
        out_shape=jax.ShapeDtypeStruct(xs.shape, U32),
        compiler_params=_cparams(1),
        name="expert_mlp",
    )(tile_plan, xs, w_gate_up.reshape(DEPTH * N_EXPERTS, d, f2),
      w_down.reshape(DEPTH * N_EXPERTS, EXPERT_FF, d))


def _moe_combine(wts_ref, x_ref, mod_ref, lng_ref, lnb_ref, ya_ref, yb_ref, token0, n):
    d = D_MODEL
    rows = slice(token0, token0 + n)
    cols = []
    for k in range(2):
        wt = jnp.broadcast_to(wts_ref[k:k + 1, rows], (LANES, n)).T
        cols.append(jnp.concatenate([wt] * (d // LANES), axis=1))
    y = cols[0] * _load_rows(ya_ref, n, token0) + cols[1] * _load_rows(yb_ref, n, token0)
    gt_f = mod_ref[:, 5 * d:6 * d]
    return _deepnorm_ln(x_ref[rows, :], y, gt_f, lng_ref[...], lnb_ref[...])


def _combine_specs(tm, n_steps, steps_per_batch):
    d = D_MODEL
    return [
        pl.BlockSpec((2, tm), lambda i: (0, i)),
        pl.BlockSpec((tm, d), lambda i: (i, 0)),
        pl.BlockSpec((None, 1, N_MOD * d), lambda i: (i // steps_per_batch, 0, 0)),
        pl.BlockSpec((1, d), lambda i: (0, 0)),
        pl.BlockSpec((1, d), lambda i: (0, 0)),
        pl.BlockSpec((tm * ROW_CHUNKS, LANES), lambda i: (i, 0)),
        pl.BlockSpec((tm * ROW_CHUNKS, LANES), lambda i: (n_steps + i, 0)),
    ]


def _gather_rows_of_tokens(ys, pos2):
    return _sc_gather_rows(ys, pos2.reshape(-1))


def _combine_body(wts_ref, x_ref, mod_ref, lng_ref, lnb_ref, ya_ref, yb_ref, o_ref):
    for r in range(x_ref.shape[0] // SUB_ROWS):
        o_ref[r * SUB_ROWS:(r + 1) * SUB_ROWS, :] = _moe_combine(wts_ref, x_ref, mod_ref, lng_ref, lnb_ref, ya_ref,
                                                                 yb_ref, r * SUB_ROWS, SUB_ROWS)


def _combine_ln(moe, mod_rows, ln_g, ln_b, seq):
    wts, x1, ys, pos2 = moe
    yg = _gather_rows_of_tokens(ys, pos2)
    t, d = x1.shape
    tm = 1024
    n_steps = t // tm
    return pl.pallas_call(
        _combine_body,
        grid=(n_steps,),
        in_specs=_combine_specs(tm, n_steps, seq // tm),
        out_specs=pl.BlockSpec((tm, d), lambda i: (i, 0)),
        out_shape=jax.ShapeDtypeStruct((t, d), F32),
        compiler_params=_cparams(1),
        name="moe_combine_ln",
    )(wts, x1, mod_rows, ln_g.reshape(1, d), ln_b.reshape(1, d), yg, yg)


def _gmlp_body(wts_ref, x1_ref, modp_ref, lng_ref, lnb_ref, ya_ref, yb_ref,
               mod_ref, w_ref, b_ref, g_ref, beta_ref, ws_ref, bs_ref, x2_ref, o_ref, ws_bf, w_bf):
    @pl.when(pl.program_id(0) == 0)
    def _():
        tri = lax.broadcasted_iota(I32, (CHUNK, CHUNK), 0) >= lax.broadcasted_iota(I32, (CHUNK, CHUNK), 1)
        for g in range(N_SGU_GROUPS):
            ws_bf[g] = jnp.where(tri, 0.5 * ws_ref[g], 0.0).astype(BF16)
        w_bf[...] = w_ref[...].astype(BF16)

    d = D_MODEL
    sh = mod_ref[:, 0:d]
    sc = mod_ref[:, d:2 * d]
    sub_rows = x1_ref.shape[0]
    for r in range(x1_ref.shape[0] // sub_rows):
        r0 = r * sub_rows
        x2 = _moe_combine(wts_ref, x1_ref, modp_ref, lng_ref, lnb_ref, ya_ref, yb_ref, r0, sub_rows)
        x2_ref[r0:r0 + sub_rows, :] = x2
        h = x2 * (1.0 + sc) + sh
        z = jnp.dot(h.astype(BF16), w_bf[...], preferred_element_type=F32) + b_ref[...]
        z2 = z * (1.0 + lax.erf(z * (2.0 ** -0.5)))
        u = z2[:, :GMLP_WIDTH]
        v = _layer_norm(z2[:, GMLP_WIDTH:], g_ref[...], beta_ref[...], eps=4.0 * LN_EPS).astype(BF16)
        for ci in range(sub_rows // CHUNK):
            rows = slice(ci * CHUNK, (ci + 1) * CHUNK)
            out_rows = slice(r0 + ci * CHUNK, r0 + (ci + 1) * CHUNK)
            for g in range(N_SGU_GROUPS):
                lanes = slice(g * SGU_GROUP_DIM, (g + 1) * SGU_GROUP_DIM)
                mixed = jnp.dot(ws_bf[g], v[rows, lanes], preferred_element_type=F32) + 0.5 * bs_ref[:, g:g + 1]
                o_ref[out_rows, lanes] = (u[rows, lanes] * mixed).astype(BF16)


def _gmlp_gate(moe, ln_g, ln_b, mod_prev, mod_rows, w_in, b_in, sgu_g, sgu_b, w_s, b_s, seq):
    wts, x1, ys, pos2 = moe
    yg = _gather_rows_of_tokens(ys, pos2)
    t, d = x1.shape
    tm = 512
    steps_per_batch = seq // tm
    n_steps = t // tm
    gw = GMLP_WIDTH
    return pl.pallas_call(
        _gmlp_body,
        grid=(n_steps,),
        in_specs=_combine_specs(tm, n_steps, steps_per_batch) + [
            pl.BlockSpec((None, 1, N_MOD * d), lambda i: (i // steps_per_batch, 0, 0)),
            pl.BlockSpec((d, 2 * gw), lambda i: (0, 0)),
            pl.BlockSpec((1, 2 * gw), lambda i: (0, 0)),
            pl.BlockSpec((1, gw), lambda i: (0, 0)),
            pl.BlockSpec((1, gw), lambda i: (0, 0)),
            pl.BlockSpec((N_SGU_GROUPS, CHUNK, CHUNK), lambda i: (0, 0, 0)),
            pl.BlockSpec((CHUNK, N_SGU_GROUPS), lambda i: (0, 0)),
        ],
        out_specs=[pl.BlockSpec((tm, d), lambda i: (i, 0)), pl.BlockSpec((tm, gw), lambda i: (i, 0))],
        out_shape=[jax.ShapeDtypeStruct((t, d), F32), jax.ShapeDtypeStruct((t, gw), BF16)],
        scratch_shapes=[pltpu.VMEM((N_SGU_GROUPS, CHUNK, CHUNK), BF16), pltpu.VMEM((d, 2 * gw), BF16)],
        compiler_params=_cparams(1),
        name="combine_gmlp_gate",
    )(wts, x1, mod_prev, ln_g.reshape(1, d), ln_b.reshape(1, d), yg, yg,
      mod_rows, w_in, b_in.reshape(1, 2 * gw), sgu_g.reshape(1, gw), sgu_b.reshape(1, gw), w_s, b_s.T)


def _router_params(w_group, b_group, w_expert, b_expert):
    def lanes(group_part, expert_part):
        rows = group_part.shape[0]
        gap = jnp.zeros((rows, EXPERT_COL0 - N_EXPERT_GROUPS), F32)
        tail = jnp.zeros((rows, ROUTER_LANES - EXPERT_COL0 - N_EXPERTS), F32)
        return jnp.concatenate([group_part, gap, expert_part, tail], axis=1)

    return lanes(w_group, w_expert), lanes(b_group[None, :], b_expert[None, :])


def _moe_experts(x1, h2, ids, wts, layer, w_gate_up, w_down):
    t = x1.shape[0]
    n_rows = 2 * t + N_EXPERTS * EXPERT_TILE
    n_tiles = n_rows // EXPERT_TILE
    pos, tile_plan = _expert_sort(ids, n_tiles)
    pos2 = pos.reshape(2, t)
    xs = _sc_scatter_rows(h2, pos2, n_rows)
    ys = _expert_mlp(tile_plan, xs, w_gate_up, w_down, layer)
    return wts, x1, ys, pos2


def kernel(x, c, positions, ada_w, ada_b, post_ln_g, post_ln_b, attn_w_qkv, attn_b_qkv, attn_sinks, attn_w_o, attn_b_o, gmlp_w_in, gmlp_b_in, gmlp_sgu_ln_g, gmlp_sgu_ln_b, gmlp_w_s, gmlp_b_s, gmlp_w_out, gmlp_b_out, moe_w_group_router, moe_b_group_router, moe_w_expert_router, moe_b_expert_router, moe_w_gate_up, moe_w_down):
    batch, seq, d = x.shape
    t = batch * seq
    assert d == D_MODEL and batch <= MOD_ROWS and ada_w.shape[0] == DEPTH == 2
    assert seq % 1024 == 0, "token tiles of 1024 rows must not straddle sequences"
    assert t % (SC_WORKERS * SC_CHUNK) == 0, "every SparseCore subcore moves whole chunks"
    x2 = x.reshape(t, d)
    c_pad = jnp.pad(c, ((0, MOD_ROWS - batch), (0, 0)))
    mods = [_adaln_mod(c_pad, ada_w, ada_b, layer, batch) for layer in range(DEPTH)]

    moe = None
    for layer in range(DEPTH):
        j = layer // 2
        if layer % 2 == 0:
            if moe is not None:
                x2 = _combine_ln(moe, mods[layer - 1], post_ln_g[layer - 1, 1], post_ln_b[layer - 1, 1], seq)
            qkv = _qkv_rope(x2, mods[layer], positions, attn_w_qkv[j], attn_b_qkv[j], seq)
            mix = _attention(qkv, attn_sinks[j], batch, seq)
            w_out, b_out = attn_w_o[j], attn_b_o[j]
        else:
            x2, mix = _gmlp_gate(moe, post_ln_g[layer - 1, 1], post_ln_b[layer - 1, 1], mods[layer - 1], mods[layer],
                                 gmlp_w_in[j], gmlp_b_in[j], gmlp_sgu_ln_g[j], gmlp_sgu_ln_b[j],
                                 gmlp_w_s[j], gmlp_b_s[j], seq)
            w_out, b_out = gmlp_w_out[j], gmlp_b_out[j]
        w_router, b_router = _router_params(moe_w_group_router[layer], moe_b_group_router[layer],
                                            moe_w_expert_router[layer], moe_b_expert_router[layer])
        x1, h2, ids, wts = _proj_ln_router(mix, x2, mods[layer], w_out, b_out, post_ln_g[layer, 0],
                                           post_ln_b[layer, 0], w_router, b_router, seq)
        moe = _moe_experts(x1, h2, ids, wts, layer, moe_w_gate_up, moe_w_down)
    x2 = _combine_ln(moe, mods[DEPTH - 1], post_ln_g[DEPTH - 1, 1], post_ln_b[DEPTH - 1, 1], seq)
    return x2.reshape(batch, seq, d)
```

```python
import functools

import jax
import jax.numpy as jnp
from jax import lax
from jax.experimental import pallas as pl
from jax.experimental.pallas import tpu as pltpu
from jax.experimental.pallas import tpu_sc as plsc

F32 = jnp.float32
BF16 = jnp.bfloat16
I32 = jnp.int32

D_MODEL = 1024
DEPTH = 2
HEAD_DIM = 64
N_Q_HEADS = 16
N_KV_HEADS = 4
GQA_GROUP = N_Q_HEADS // N_KV_HEADS
WINDOW = 128
ROPE_THETA = 10000.0
Q_WIDTH = N_Q_HEADS * HEAD_DIM
KV_WIDTH = N_KV_HEADS * HEAD_DIM
QKV_WIDTH = Q_WIDTH + 2 * KV_WIDTH
CHUNK = 128
GMLP_WIDTH = 2 * D_MODEL
N_SGU_GROUPS = 8
SGU_GROUP_DIM = GMLP_WIDTH // N_SGU_GROUPS
N_EXPERT_GROUPS = 4
EXPERTS_PER_GROUP = 8
N_EXPERTS = N_EXPERT_GROUPS * EXPERTS_PER_GROUP
EXPERT_FF = D_MODEL // 4
N_MOD = 6
DEEPNORM_ALPHA = (2.0 * DEPTH) ** 0.25
LN_EPS = 1e-5

LANES = 128
MOD_ROWS = 8
ROUTER_LANES = 128
EXPERT_COL0 = 8
SORT_CHUNK = 256
EXPERT_TILE = 512
SUB_ROWS = 512
VMEM_LIMIT = 56 * 1024 * 1024
SC_CORES = 2
SC_WORKERS = 32
SC_CHUNK = 64
SC_SLOTS = 3


def _cparams(n_axes, vmem=VMEM_LIMIT):
    return pltpu.CompilerParams(dimension_semantics=("arbitrary",) * n_axes, vmem_limit_bytes=vmem)


U32 = jnp.uint32
ROW_CHUNKS = D_MODEL // 2 // LANES


def _store_rows(ref, val, token0=0):
    n = val.shape[0]
    half = D_MODEL // 2
    words = pltpu.pack_elementwise([val[:, :half], val[:, half:]], packed_dtype=BF16)
    for c in range(ROW_CHUNKS):
        ref[pl.ds(token0 * ROW_CHUNKS + c, n, stride=ROW_CHUNKS), :] = words[:, c * LANES:(c + 1) * LANES]


def _load_rows(ref, n, token0=0):
    words = jnp.concatenate([ref[pl.ds(token0 * ROW_CHUNKS + c, n, stride=ROW_CHUNKS), :]
                             for c in range(ROW_CHUNKS)], axis=1)
    lo = lax.bitcast_convert_type(words << 16, F32)
    hi = lax.bitcast_convert_type(words & jnp.uint32(0xFFFF0000), F32)
    return jnp.concatenate([lo, hi], axis=1)


def _layer_norm(r, g, b, eps=LN_EPS):
    mu = jnp.mean(r, axis=-1, keepdims=True)
    d = r - mu
    var = jnp.mean(d * d, axis=-1, keepdims=True)
    return d * lax.rsqrt(var + eps) * g + b


def _deepnorm_ln(x, branch, gate_row, g, b):
    scale = (1.0 + gate_row) * (1.0 / DEEPNORM_ALPHA)
    return _layer_norm(x + scale * branch, g, b, eps=LN_EPS / (DEEPNORM_ALPHA * DEEPNORM_ALPHA))


def _mod_body(c_ref, w_ref, b_ref, o_ref):
    c = c_ref[...]
    ca = c * jax.nn.sigmoid(c)
    mod = jnp.dot(ca.astype(BF16), w_ref[...].astype(BF16), preferred_element_type=F32) + b_ref[...]
    for r in range(o_ref.shape[0]):
        o_ref[r] = mod[r:r + 1]


def _adaln_mod(c_pad, ada_w, ada_b, layer, batch):
    tn = 1536
    n_out = N_MOD * D_MODEL
    return pl.pallas_call(
        _mod_body,
        grid=(n_out // tn,),
        in_specs=[
            pl.BlockSpec((MOD_ROWS, D_MODEL), lambda j: (0, 0)),
            pl.BlockSpec((None, D_MODEL, tn), lambda j: (layer, 0, j)),
            pl.BlockSpec((None, 1, tn), lambda j: (layer, 0, j)),
        ],
        out_specs=pl.BlockSpec((batch, 1, tn), lambda j: (0, 0, j)),
        out_shape=jax.ShapeDtypeStruct((batch, 1, n_out), F32),
        compiler_params=_cparams(1),
        name="adaln_mod",
    )(c_pad, ada_w, ada_b.reshape(DEPTH, 1, n_out))


def _qkv_body(x_ref, mod_ref, pos_ref, w_ref, b_ref, invf_ref, o_ref, wbf_ref):
    @pl.when(pl.program_id(0) == 0)
    def _():
        wbf_ref[...] = w_ref[...].astype(BF16)

    tm = x_ref.shape[0]
    sh = mod_ref[:, 0:D_MODEL]
    sc = mod_ref[:, D_MODEL:2 * D_MODEL]
    h = x_ref[...] * (1.0 + sc) + sh
    qkv = jnp.dot(h.astype(BF16), wbf_ref[...], preferred_element_type=F32) + b_ref[...]

    ang = invf_ref[...] * pos_ref[...].astype(F32)
    c = jnp.cos(ang)
    s = jnp.sin(ang)
    ct = jnp.concatenate([c, c, c, c], axis=0).T
    st = jnp.concatenate([-s, s, -s, s], axis=0).T
    lane = lax.broadcasted_iota(I32, (tm, LANES), 1)
    first_half = (lane & (HEAD_DIM // 2)) == 0
    n_rope = (Q_WIDTH + KV_WIDTH) // LANES
    for j in range(n_rope):
        blk = qkv[:, j * LANES:(j + 1) * LANES]
        rot = jnp.where(first_half, pltpu.roll(blk, LANES - HEAD_DIM // 2, 1), pltpu.roll(blk, HEAD_DIM // 2, 1))
        r = blk * ct + rot * st
        if j < Q_WIDTH // LANES:
            r = r * (HEAD_DIM ** -0.5)
        o_ref[:, j * LANES:(j + 1) * LANES] = r.astype(BF16)
    o_ref[:, Q_WIDTH + KV_WIDTH:] = qkv[:, Q_WIDTH + KV_WIDTH:].astype(BF16)


def _qkv_rope(x2, mod_rows, positions, w_qkv, b_qkv, seq):
    t = x2.shape[0]
    tm = 1024
    steps_per_batch = seq // tm
    inv_freq = ROPE_THETA ** (-jnp.arange(0, HEAD_DIM, 2, dtype=F32) / HEAD_DIM)
    return pl.pallas_call(
        _qkv_body,
        grid=(t // tm,),
        in_specs=[
            pl.BlockSpec((tm, D_MODEL), lambda i: (i, 0)),
            pl.BlockSpec((None, 1, N_MOD * D_MODEL), lambda i: (i // steps_per_batch, 0, 0)),
            pl.BlockSpec((None, 1, tm), lambda i: (i, 0, 0)),
            pl.BlockSpec((D_MODEL, QKV_WIDTH), lambda i: (0, 0)),
            pl.BlockSpec((1, QKV_WIDTH), lambda i: (0, 0)),
            pl.BlockSpec((HEAD_DIM // 2, 1), lambda i: (0, 0)),
        ],
        out_specs=pl.BlockSpec((tm, QKV_WIDTH), lambda i: (i, 0)),
        out_shape=jax.ShapeDtypeStruct((t, QKV_WIDTH), BF16),
        scratch_shapes=[pltpu.VMEM((D_MODEL, QKV_WIDTH), BF16)],
        compiler_params=_cparams(1),
        name="qkv_rope",
    )(x2, mod_rows, positions.reshape(t // tm, 1, tm), w_qkv, b_qkv.reshape(1, QKV_WIDTH),
      inv_freq.reshape(HEAD_DIM // 2, 1))


BF16_ROWS = 16
BF16_ONE_PAIR = 0x3F803F80


def _attn_prepare(kv, kab_ref, vx_ref, slot):
    words = pltpu.bitcast(kv, U32)
    low = lax.broadcasted_iota(I32, (WINDOW // 2, LANES), 1) < HEAD_DIM
    zeros = jnp.zeros((WINDOW // 2, LANES), U32)
    ones = jnp.full((WINDOW // 2, LANES), BF16_ONE_PAIR, U32)
    for g in range(N_KV_HEADS):
        for part, ref in ((0, None), (KV_WIDTH, vx_ref)):
            tile = words[:, part + (g // 2) * LANES:part + (g // 2 + 1) * LANES]
            other = pltpu.roll(tile, HEAD_DIM, 1)
            in_low, in_high = (tile, other) if g % 2 == 0 else (other, tile)
            if ref is None:
                kab_ref[slot, 2 * g] = pltpu.bitcast(jnp.where(low, in_low, zeros), BF16)
                kab_ref[slot, 2 * g + 1] = pltpu.bitcast(jnp.where(low, zeros, in_high), BF16)
            else:
                both = jnp.where(low, in_low, in_high)
                vx_ref[slot, g] = pltpu.bitcast(jnp.concatenate([both, ones], axis=1), BF16)


def _attn_block(sink_ref, q, kab_ref, vx_ref, s_ref, p_ref, prev, cur, first_block):
    for g in range(N_KV_HEADS):
        q_pair = jnp.concatenate([q[:, (2 * g) * LANES:(2 * g + 1) * LANES],
                                  q[:, (2 * g + 1) * LANES:(2 * g + 2) * LANES]], axis=0)
        for a in range(2):
            kband = jnp.concatenate([kab_ref[prev, 2 * g + a], kab_ref[cur, 2 * g + a]], axis=0)
            s = lax.dot_general(q_pair, kband, (((1,), (1,)), ((), ())), preferred_element_type=F32)
            s_ref[GQA_GROUP * g + a] = s[:WINDOW]
            s_ref[GQA_GROUP * g + 2 + a] = s[WINDOW:]
    qi = lax.broadcasted_iota(I32, (WINDOW, 2 * WINDOW), 0) + WINDOW
    kj = lax.broadcasted_iota(I32, (WINDOW, 2 * WINDOW), 1)
    mask = (kj <= qi) & (kj > qi - WINDOW) & ((kj >= WINDOW) | jnp.logical_not(first_block))
    key0 = lax.broadcasted_iota(I32, (1, 2 * WINDOW), 1) == 0
    for h in range(N_Q_HEADS):
        s = jnp.where(mask, s_ref[h], jnp.where(key0, sink_ref[h], -jnp.inf))
        m = jnp.max(s, axis=-1, keepdims=True)
        p_ref[h] = jnp.exp(s - m).astype(BF16)
    low = lax.broadcasted_iota(I32, (WINDOW, LANES), 1) < HEAD_DIM
    sink_row = ((lax.broadcasted_iota(I32, (BF16_ROWS, 2 * LANES), 0) == 0)
                & (lax.broadcasted_iota(I32, (BF16_ROWS, 2 * LANES), 1) < LANES))
    out_tiles = []
    for g in range(N_KV_HEADS):
        v_prev = vx_ref[prev, g]
        v_head = jnp.where(sink_row, 0.0, v_prev[:BF16_ROWS].astype(F32)).astype(BF16)
        vband = jnp.concatenate([v_head, v_prev[BF16_ROWS:], vx_ref[cur, g]], axis=0)
        p4 = p_ref[GQA_GROUP * g:GQA_GROUP * (g + 1)].reshape(GQA_GROUP * WINDOW, 2 * WINDOW)
        o4 = jnp.dot(p4, vband, preferred_element_type=F32)
        heads = []
        for j in range(GQA_GROUP):
            blk = o4[j * WINDOW:(j + 1) * WINDOW]
            heads.append(blk[:, :LANES] / blk[:, LANES:])
        out_tiles.append(jnp.where(low, heads[0], heads[1]))
        out_tiles.append(jnp.where(low, heads[2], heads[3]))
    return jnp.concatenate(out_tiles, axis=1).astype(BF16)


def _attn_body(sink_ref, q_ref, kv_ref, o_ref, kab_ref, vx_ref, s_ref, p_ref):
    n = pl.program_id(1)

    @pl.when(n == 0)
    def _():
        kab_ref[1] = jnp.zeros(kab_ref.shape[1:], kab_ref.dtype)
        half = (N_KV_HEADS, WINDOW, LANES)
        vx_ref[1] = jnp.concatenate([jnp.zeros(half, BF16), jnp.ones(half, BF16)], axis=-1)

    scratch = (kab_ref, vx_ref, s_ref, p_ref)
    for blk in range(q_ref.shape[0] // WINDOW):
        rows = slice(blk * WINDOW, (blk + 1) * WINDOW)
        cur = blk % 2
        _attn_prepare(kv_ref[rows, :], kab_ref, vx_ref, cur)
        o_ref[rows, :] = _attn_block(sink_ref, q_ref[rows, :], *scratch, 1 - cur, cur,
                                     (n == 0) if blk == 0 else False)


def _attention(qkv, sinks, batch, seq):
    t = qkv.shape[0]
    tq = 8 * WINDOW
    steps = seq // tq
    kv_col = Q_WIDTH // (2 * KV_WIDTH)
    return pl.pallas_call(
        _attn_body,
        grid=(batch, steps),
        in_specs=[
            pl.BlockSpec(memory_space=pltpu.SMEM),
            pl.BlockSpec((tq, Q_WIDTH), lambda b, n: (b * steps + n, 0)),
            pl.BlockSpec((tq, 2 * KV_WIDTH), lambda b, n: (b * steps + n, kv_col)),
        ],
        out_specs=pl.BlockSpec((tq, Q_WIDTH), lambda b, n: (b * steps + n, 0)),
        out_shape=jax.ShapeDtypeStruct((t, Q_WIDTH), BF16),
        scratch_shapes=[
            pltpu.VMEM((2, 2 * N_KV_HEADS, WINDOW, LANES), BF16),
            pltpu.VMEM((2, N_KV_HEADS, WINDOW, 2 * LANES), BF16),
            pltpu.VMEM((N_Q_HEADS, WINDOW, 2 * WINDOW), F32),
            pltpu.VMEM((N_Q_HEADS, WINDOW, 2 * WINDOW), BF16),
        ],
        compiler_params=_cparams(2),
        name="swa_attention",
    )(sinks, qkv, qkv)


def _route(lt):
    tm = lt.shape[1]
    row = lax.broadcasted_iota(I32, (EXPERTS_PER_GROUP, tm), 0)
    neg = -jnp.inf
    gl = jnp.where(row < N_EXPERT_GROUPS, lt[0:EXPERTS_PER_GROUP], neg)
    gm = jnp.max(gl, axis=0, keepdims=True)
    g_p = 1.0 / jnp.sum(jnp.exp(gl - gm), axis=0, keepdims=True)
    g_idx = jnp.min(jnp.where(gl == gm, row, EXPERTS_PER_GROUP), axis=0, keepdims=True)
    sel = lt[EXPERT_COL0 + (N_EXPERT_GROUPS - 1) * EXPERTS_PER_GROUP:EXPERT_COL0 + N_EXPERTS]
    for g in range(N_EXPERT_GROUPS - 2, -1, -1):
        lo = EXPERT_COL0 + g * EXPERTS_PER_GROUP
        sel = jnp.where(g_idx == g, lt[lo:lo + EXPERTS_PER_GROUP], sel)
    v1 = jnp.max(sel, axis=0, keepdims=True)
    i1 = jnp.min(jnp.where(sel == v1, row, EXPERTS_PER_GROUP), axis=0, keepdims=True)
    sel2 = jnp.where(row == i1, neg, sel)
    v2 = jnp.max(sel2, axis=0, keepdims=True)
    i2 = jnp.min(jnp.where(sel2 == v2, row, EXPERTS_PER_GROUP), axis=0, keepdims=True)
    e2 = jnp.exp(v2 - v1)
    w1 = g_p / (1.0 + e2)
    w2 = g_p * e2 / (1.0 + e2)
    base = g_idx * EXPERTS_PER_GROUP
    return base + i1, base + i2, w1, w2


def _proj_body(o_ref, x_ref, mod_ref, w_ref, b_ref, lng_ref, lnb_ref, wr_ref, br_ref,
               x1_ref, h2_ref, ids_ref, wts_ref, wbf_ref):
    @pl.when(pl.program_id(0) == 0)
    def _():
        wbf_ref[...] = w_ref[...].astype(BF16)

    d = D_MODEL
    gt_m = mod_ref[:, 2 * d:3 * d]
    sh_f = mod_ref[:, 3 * d:4 * d]
    sc_f = mod_ref[:, 4 * d:5 * d]
    wr = wr_ref[...].astype(BF16)
    y_all = jnp.dot(o_ref[...], wbf_ref[...], preferred_element_type=F32)
    for r in range(x_ref.shape[0] // SUB_ROWS):
        rows = slice(r * SUB_ROWS, (r + 1) * SUB_ROWS)
        y = y_all[rows, :] + b_ref[...]
        x1 = _deepnorm_ln(x_ref[rows, :], y, gt_m, lng_ref[...], lnb_ref[...])
        x1_ref[rows, :] = x1
        h2 = x1 * (1.0 + sc_f) + sh_f
        _store_rows(h2_ref, h2, r * SUB_ROWS)
        logits = jnp.dot(h2.astype(BF16), wr, preferred_element_type=F32) + br_ref[...]
        ea, eb, wa, wb = _route(logits.T)
        ids_ref[0:1, rows] = ea
        ids_ref[1:2, rows] = eb
        wts_ref[0:1, rows] = wa
        wts_ref[1:2, rows] = wb


def _proj_ln_router(o, x2, mod_rows, w, b, ln_g, ln_b, w_router, b_router, seq):
    t, k = o.shape
    tm = 1024
    steps_per_batch = seq // tm
    d = D_MODEL
    return pl.pallas_call(
        _proj_body,
        grid=(t // tm,),
        in_specs=[
            pl.BlockSpec((tm, k), lambda i: (i, 0)),
            pl.BlockSpec((tm, d), lambda i: (i, 0)),
            pl.BlockSpec((None, 1, N_MOD * d), lambda i: (i // steps_per_batch, 0, 0)),
            pl.BlockSpec((k, d), lambda i: (0, 0), pipeline_mode=pl.Buffered(1)),
            pl.BlockSpec((1, d), lambda i: (0, 0)),
            pl.BlockSpec((1, d), lambda i: (0, 0)),
            pl.BlockSpec((1, d), lambda i: (0, 0)),
            pl.BlockSpec((d, ROUTER_LANES), lambda i: (0, 0)),
            pl.BlockSpec((1, ROUTER_LANES), lambda i: (0, 0)),
        ],
        out_specs=[
            pl.BlockSpec((tm, d), lambda i: (i, 0)),
            pl.BlockSpec((tm * ROW_CHUNKS, LANES), lambda i: (i, 0)),
            pl.BlockSpec((2, tm), lambda i: (0, i)),
            pl.BlockSpec((2, tm), lambda i: (0, i)),
        ],
        out_shape=[
            jax.ShapeDtypeStruct((t, d), F32),
            jax.ShapeDtypeStruct((t * ROW_CHUNKS, LANES), U32),
            jax.ShapeDtypeStruct((2, t), I32),
            jax.ShapeDtypeStruct((2, t), F32),
        ],
        scratch_shapes=[pltpu.VMEM((k, d), BF16)],
        compiler_params=_cparams(1),
        name="proj_ln_router",
    )(o, x2, mod_rows, w, b.reshape(1, d), ln_g.reshape(1, d), ln_b.reshape(1, d), w_router, b_router)


PLAN_EXPERT, PLAN_USED, PLAN_VALID, PLAN_FIRST, PLAN_NEXT, PLAN_SLOT = range(6)
PLAN_ROWS = 8


def _sort_body(ids_ref, pos_ref, plan_ref, rank_ref):
    n_rows = ids_ref.shape[0]
    c = SORT_CHUNK
    erow = lax.broadcasted_iota(I32, (N_EXPERTS, c), 0)
    tri = (lax.broadcasted_iota(I32, (c, c), 0) <= lax.broadcasted_iota(I32, (c, c), 1)).astype(BF16)

    def rank_step(r, carry):
        onehot = erow == ids_ref[pl.ds(r, 1), :]
        pref = jnp.dot(onehot.astype(BF16), tri, preferred_element_type=F32)
        rank = jnp.sum(jnp.where(onehot, pref + carry, 0.0), axis=0, keepdims=True) - 1.0
        rank_ref[pl.ds(r, 1), :] = rank
        return carry + pref[:, c - 1:c]

    counts = lax.fori_loop(0, n_rows, rank_step, jnp.zeros((N_EXPERTS, 1), F32), unroll=8)
    n_tile = jnp.floor((counts + (EXPERT_TILE - 1)) * (1.0 / EXPERT_TILE))
    low = (lax.broadcasted_iota(I32, (N_EXPERTS, N_EXPERTS), 1)
           <= lax.broadcasted_iota(I32, (N_EXPERTS, N_EXPERTS), 0)).astype(BF16)
    cum = jnp.dot(low, jnp.broadcast_to(n_tile, (N_EXPERTS, LANES)).astype(BF16),
                  preferred_element_type=F32)[:, 0:1]
    row_off = (cum - n_tile) * EXPERT_TILE

    def pos_step(r, _):
        onehot = erow == ids_ref[pl.ds(r, 1), :]
        off = jnp.sum(jnp.where(onehot, row_off, 0.0), axis=0, keepdims=True)
        pos_ref[pl.ds(r, 1), :] = (off + rank_ref[pl.ds(r, 1), :]).astype(I32)
        return 0

    lax.fori_loop(0, n_rows, pos_step, 0, unroll=8)
    total = jnp.max(cum, axis=0, keepdims=True)
    n_lanes = plan_ref.shape[1]
    tile = jnp.minimum(lax.broadcasted_iota(I32, (N_EXPERTS, n_lanes), 1).astype(F32), total - 1.0)
    plan_ref[PLAN_EXPERT:PLAN_EXPERT + 1, :] = jnp.sum(jnp.where(cum <= tile, 1.0, 0.0), axis=0,
                                                       keepdims=True).astype(I32)
    plan_ref[PLAN_USED:PLAN_USED + 1, :] = jnp.broadcast_to(total, (1, n_lanes)).astype(I32)
    tile_f = lax.broadcasted_iota(I32, (N_EXPERTS, n_lanes), 1).astype(F32)
    first = cum - n_tile
    rows_left = jnp.clip(counts - (tile_f - first) * EXPERT_TILE, 0.0, float(EXPERT_TILE))
    owns = (first <= tile_f) & (tile_f < cum)
    plan_ref[PLAN_VALID:PLAN_VALID + 1, :] = jnp.sum(jnp.where(owns, rows_left, 0.0), axis=0,
                                                     keepdims=True).astype(I32)
    def lane_sum(x):
        return jnp.sum(x, axis=0, keepdims=True)

    is_first = lane_sum(jnp.where(owns & (tile_f == first), 1.0, 0.0))
    group_end = lane_sum(jnp.where(owns, cum, 0.0))
    next_expert = jnp.where(group_end < total, lane_sum(jnp.where(cum <= group_end, 1.0, 0.0)), -1.0)
    ordinal = lane_sum(jnp.where((cum <= tile_f[0:1]) & (n_tile > 0.0), 1.0, 0.0))
    slot = ordinal - 2.0 * jnp.floor(ordinal * 0.5)
    plan_ref[PLAN_FIRST:PLAN_FIRST + 1, :] = is_first.astype(I32)
    plan_ref[PLAN_NEXT:PLAN_NEXT + 1, :] = next_expert.astype(I32)
    plan_ref[PLAN_SLOT:PLAN_SLOT + 1, :] = slot.astype(I32)
    plan_ref[PLAN_SLOT + 1:PLAN_ROWS, :] = jnp.zeros((PLAN_ROWS - PLAN_SLOT - 1, n_lanes), I32)


def _expert_sort(ids, n_tiles):
    n_assign = ids.shape[0] * ids.shape[1]
    n_rows = n_assign // SORT_CHUNK
    te_lanes = -(-n_tiles // LANES) * LANES
    return pl.pallas_call(
        _sort_body,
        grid=(1,),
        in_specs=[pl.BlockSpec((n_rows, SORT_CHUNK), lambda i: (0, 0))],
        out_specs=[
            pl.BlockSpec((n_rows, SORT_CHUNK), lambda i: (0, 0)),
            pl.BlockSpec((PLAN_ROWS, te_lanes), lambda i: (0, 0)),
        ],
        out_shape=[
            jax.ShapeDtypeStruct((n_rows, SORT_CHUNK), I32),
            jax.ShapeDtypeStruct((PLAN_ROWS, te_lanes), I32),
        ],
        scratch_shapes=[pltpu.VMEM((n_rows, SORT_CHUNK), F32)],
        compiler_params=_cparams(1),
        name="expert_sort",
    )(ids.reshape(n_rows, SORT_CHUNK))


def _sc_mesh():
    return plsc.VectorSubcoreMesh(core_axis_name="c", subcore_axis_name="s", num_cores=SC_CORES,
                                  num_subcores=SC_WORKERS // SC_CORES)


def _sc_ring(n_chunks, read, write):
    reads, writes = {}, {}
    for j in range(min(SC_SLOTS - 1, n_chunks)):
        reads[j] = read(j)
    for j in range(n_chunks):
        for cp in reads.pop(j):
            cp.wait()
        nxt = j + SC_SLOTS - 1
        if nxt < n_chunks:
            for cp in writes.pop(nxt - SC_SLOTS, []):
                cp.wait()
            reads[nxt] = read(nxt)
        writes[j] = write(j)
    for cps in writes.values():
        for cp in cps:
            cp.wait()


def _sc_scatter_rows(src, pos2, n_rows):
    t = pos2.shape[1]
    src3 = src.reshape(t, ROW_CHUNKS, LANES)
    per_worker = t // SC_WORKERS
    n_chunks = per_worker // SC_CHUNK
    idx = pos2.reshape(2, SC_WORKERS, n_chunks, SC_CHUNK).transpose(1, 0, 2, 3)
    idx = idx.reshape(SC_WORKERS, 2 * n_chunks, SC_CHUNK)

    @functools.partial(
        pl.kernel, mesh=_sc_mesh(),
        out_type=jax.ShapeDtypeStruct((n_rows, ROW_CHUNKS, LANES), src.dtype),
        scratch_types=[
            pltpu.VMEM((2 * n_chunks, SC_CHUNK), I32),
            pltpu.VMEM((SC_SLOTS, SC_CHUNK, ROW_CHUNKS, LANES), src.dtype),
            pltpu.SemaphoreType.DMA((SC_SLOTS,)),
            pltpu.SemaphoreType.DMA((SC_SLOTS,)),
        ],
        name="sc_scatter_rows",
    )
    def scatter(src_hbm, idx_hbm, out_hbm, idx_v, rows_v, rsem, wsem):
        wid = lax.axis_index("s") * SC_CORES + lax.axis_index("c")
        pltpu.sync_copy(idx_hbm.at[wid], idx_v)

        def read(j):
            b = j % SC_SLOTS
            return [pltpu.async_copy(src_hbm.at[pl.ds(wid * per_worker + j * SC_CHUNK, SC_CHUNK)], rows_v.at[b],
                                     rsem.at[b])]

        def write(j):
            b = j % SC_SLOTS
            return [pltpu.async_copy(rows_v.at[b], out_hbm.at[idx_v.at[k * n_chunks + j]], wsem.at[b])
                    for k in range(2)]

        _sc_ring(n_chunks, read, write)

    return scatter(src3, idx).reshape(n_rows * ROW_CHUNKS, LANES)


def _sc_gather_rows(table, idx):
    n = idx.shape[0]
    table3 = table.reshape(-1, ROW_CHUNKS, LANES)
    per_worker = n // SC_WORKERS
    n_chunks = per_worker // SC_CHUNK

    @functools.partial(
        pl.kernel, mesh=_sc_mesh(),
        out_type=jax.ShapeDtypeStruct((n, ROW_CHUNKS, LANES), table.dtype),
        scratch_types=[
            pltpu.VMEM((n_chunks, SC_CHUNK), I32),
            pltpu.VMEM((SC_SLOTS, SC_CHUNK, ROW_CHUNKS, LANES), table.dtype),
            pltpu.SemaphoreType.DMA((SC_SLOTS,)),
            pltpu.SemaphoreType.DMA((SC_SLOTS,)),
        ],
        name="sc_gather_rows",
    )
    def gather(table_hbm, idx_hbm, out_hbm, idx_v, rows_v, rsem, wsem):
        wid = lax.axis_index("s") * SC_CORES + lax.axis_index("c")
        pltpu.sync_copy(idx_hbm.at[wid], idx_v)

        def read(j):
            b = j % SC_SLOTS
            return [pltpu.async_copy(table_hbm.at[idx_v.at[j]], rows_v.at[b], rsem.at[b])]

        def write(j):
            b = j % SC_SLOTS
            return [pltpu.async_copy(rows_v.at[b], out_hbm.at[pl.ds(wid * per_worker + j * SC_CHUNK, SC_CHUNK)],
                                     wsem.at[b])]

        _sc_ring(n_chunks, read, write)

    out = gather(table3, idx.reshape(SC_WORKERS, n_chunks, SC_CHUNK))
    return out.reshape(n * ROW_CHUNKS, LANES)


XS_SLOTS = 4


def _expert_body(plan_ref, xs_hbm, wgu_hbm, wd_hbm, ys_ref,
                 wgu_bf, wd_bf, xbuf, xsem, wgu_buf, wd_buf, wsem, *, e0):
    i = pl.program_id(0)
    n_used = plan_ref[PLAN_USED, 0]
    used = i < n_used
    tile_rows = EXPERT_TILE * ROW_CHUNKS

    def weight_fetch(expert, slot):
        return (pltpu.make_async_copy(wgu_hbm.at[e0 + expert], wgu_buf.at[slot], wsem.at[0, slot]),
                pltpu.make_async_copy(wd_hbm.at[e0 + expert], wd_buf.at[slot], wsem.at[1, slot]))

    @pl.when((i == 0) & used)
    def _():
        for cp in weight_fetch(plan_ref[PLAN_EXPERT, 0], plan_ref[PLAN_SLOT, 0]):
            cp.start()

    @pl.when(used & (plan_ref[PLAN_FIRST, i] == 1))
    def _():
        slot = plan_ref[PLAN_SLOT, i]
        for cp in weight_fetch(plan_ref[PLAN_EXPERT, i], slot):
            cp.wait()
        wgu_bf[...] = wgu_buf[slot].astype(BF16)
        wd_bf[...] = wd_buf[slot].astype(BF16)

        @pl.when(plan_ref[PLAN_NEXT, i] >= 0)
        def _():
            for cp in weight_fetch(plan_ref[PLAN_NEXT, i], 1 - slot):
                cp.start()


    def fetch(tile):
        slot = lax.rem(tile, XS_SLOTS)
        r0 = pl.multiple_of(tile * tile_rows, tile_rows)
        return pltpu.make_async_copy(xs_hbm.at[pl.ds(r0, tile_rows)], xbuf.at[slot], xsem.at[slot])

    ahead = XS_SLOTS - 1
    for tile in range(ahead):
        @pl.when((i == 0) & (tile < n_used))
        def _(tile=tile):
            fetch(tile).start()

    @pl.when(i + ahead < n_used)
    def _():
        fetch(i + ahead).start()

    @pl.when(used)
    def _():
        fetch(i).wait()
        live = lax.broadcasted_iota(I32, (EXPERT_TILE, 1), 0) < plan_ref[PLAN_VALID, i]
        xs = jnp.where(live, _load_rows(xbuf.at[lax.rem(i, XS_SLOTS)], EXPERT_TILE), 0.0).astype(BF16)
        gu = jnp.dot(xs, wgu_bf[...], preferred_element_type=F32)
        gate = gu[:, :EXPERT_FF]
        up = gu[:, EXPERT_FF:]
        act = gate * jax.nn.sigmoid(gate) * up
        _store_rows(ys_ref, jnp.dot(act.astype(BF16), wd_bf[...], preferred_element_type=F32))

    @pl.when(jnp.logical_not(used))
    def _():
        ys_ref[...] = jnp.zeros(ys_ref.shape, ys_ref.dtype)


def _expert_mlp(tile_plan, xs, w_gate_up, w_down, layer):
    d = D_MODEL
    n_tiles = xs.shape[0] // (EXPERT_TILE * ROW_CHUNKS)
    f2 = 2 * EXPERT_FF
    tile_rows = EXPERT_TILE * ROW_CHUNKS
    grid_spec = pltpu.PrefetchScalarGridSpec(
        num_scalar_prefetch=1,
        grid=(n_tiles,),
        in_specs=[
            pl.BlockSpec(memory_space=pl.ANY),
            pl.BlockSpec(memory_space=pl.ANY),
            pl.BlockSpec(memory_space=pl.ANY),
        ],
        out_specs=pl.BlockSpec((tile_rows, LANES), lambda i, plan: (i, 0)),
        scratch_shapes=[pltpu.VMEM((d, f2), BF16), pltpu.VMEM((EXPERT_FF, d), BF16),
                        pltpu.VMEM((XS_SLOTS, tile_rows, LANES), U32), pltpu.SemaphoreType.DMA((XS_SLOTS,)),
                        pltpu.VMEM((2, d, f2), F32), pltpu.VMEM((2, EXPERT_FF, d), F32),
                        pltpu.SemaphoreType.DMA((2, 2))],
    )
    return pl.pallas_call(
        functools.partial(_expert_body, e0=layer * N_EXPERTS),
        grid_spec=grid_spec,
        out_shape=jax.ShapeDtypeStruct(xs.shape, U32),
        compiler_params=_cparams(1),
        name="expert_mlp",
    )(tile_plan, xs, w_gate_up.reshape(DEPTH * N_EXPERTS, d, f2),
      w_down.reshape(DEPTH * N_EXPERTS, EXPERT_FF, d))


def _moe_combine(wts_ref, x_ref, mod_ref, lng_ref, lnb_ref, ya_ref, yb_ref, token0, n):
    d = D_MODEL
    rows = slice(token0, token0 + n)
    cols = []
    for k in range(2):
        wt = jnp.broadcast_to(wts_ref[k:k + 1, rows], (LANES, n)).T
        cols.append(jnp.concatenate([wt] * (d // LANES), axis=1))
    y = cols[0] * _load_rows(ya_ref, n, token0) + cols[1] * _load_rows(yb_ref, n, token0)
    gt_f = mod_ref[:, 5 * d:6 * d]
    return _deepnorm_ln(x_ref[rows, :], y, gt_f, lng_ref[...], lnb_ref[...])


def _combine_specs(tm, n_steps, steps_per_batch):
    d = D_MODEL
    return [
        pl.BlockSpec((2, tm), lambda i: (0, i)),
        pl.BlockSpec((tm, d), lambda i: (i, 0)),
        pl.BlockSpec((None, 1, N_MOD * d), lambda i: (i // steps_per_batch, 0, 0)),
        pl.BlockSpec((1, d), lambda i: (0, 0)),
        pl.BlockSpec((1, d), lambda i: (0, 0)),
        pl.BlockSpec((tm * ROW_CHUNKS, LANES), lambda i: (i, 0)),
        pl.BlockSpec((tm * ROW_CHUNKS, LANES), lambda i: (n_steps + i, 0)),
    ]


def _gather_rows_of_tokens(ys, pos2):
    return _sc_gather_rows(ys, pos2.reshape(-1))


def _combine_body(wts_ref, x_ref, mod_ref, lng_ref, lnb_ref, ya_ref, yb_ref, o_ref):
    for r in range(x_ref.shape[0] // SUB_ROWS):
        o_ref[r * SUB_ROWS:(r + 1) * SUB_ROWS, :] = _moe_combine(wts_ref, x_ref, mod_ref, lng_ref, lnb_ref, ya_ref,
                                                                 yb_ref, r * SUB_ROWS, SUB_ROWS)


def _combine_ln(moe, mod_rows, ln_g, ln_b, seq):
    wts, x1, ys, pos2 = moe
    yg = _gather_rows_of_tokens(ys, pos2)
    t, d = x1.shape
    tm = 1024
    n_steps = t // tm
    return pl.pallas_call(
        _combine_body,
        grid=(n_steps,),
        in_specs=_combine_specs(tm, n_steps, seq // tm),
        out_specs=pl.BlockSpec((tm, d), lambda i: (i, 0)),
        out_shape=jax.ShapeDtypeStruct((t, d), F32),
        compiler_params=_cparams(1),
        name="moe_combine_ln",
    )(wts, x1, mod_rows, ln_g.reshape(1, d), ln_b.reshape(1, d), yg, yg)


def _gmlp_body(wts_ref, x1_ref, modp_ref, lng_ref, lnb_ref, ya_ref, yb_ref,
               mod_ref, w_ref, b_ref, g_ref, beta_ref, ws_ref, bs_ref, x2_ref, o_ref, ws_bf, w_bf):
    @pl.when(pl.program_id(0) == 0)
    def _():
        tri = lax.broadcasted_iota(I32, (CHUNK, CHUNK), 0) >= lax.broadcasted_iota(I32, (CHUNK, CHUNK), 1)
        for g in range(N_SGU_GROUPS):
            ws_bf[g] = jnp.where(tri, 0.5 * ws_ref[g], 0.0).astype(BF16)
        w_bf[...] = w_ref[...].astype(BF16)

    d = D_MODEL
    sh = mod_ref[:, 0:d]
    sc = mod_ref[:, d:2 * d]
    sub_rows = x1_ref.shape[0]
    for r in range(x1_ref.shape[0] // sub_rows):
        r0 = r * sub_rows
        x2 = _moe_combine(wts_ref, x1_ref, modp_ref, lng_ref, lnb_ref, ya_ref, yb_ref, r0, sub_rows)
        x2_ref[r0:r0 + sub_rows, :] = x2
        h = x2 * (1.0 + sc) + sh
        z = jnp.dot(h.astype(BF16), w_bf[...], preferred_element_type=F32) + b_ref[...]
        z2 = z * (1.0 + lax.erf(z * (2.0 ** -0.5)))
        u = z2[:, :GMLP_WIDTH]
        v = _layer_norm(z2[:, GMLP_WIDTH:], g_ref[...], beta_ref[...], eps=4.0 * LN_EPS).astype(BF16)
        for ci in range(sub_rows // CHUNK):
            rows = slice(ci * CHUNK, (ci + 1) * CHUNK)
            out_rows = slice(r0 + ci * CHUNK, r0 + (ci + 1) * CHUNK)
            for g in range(N_SGU_GROUPS):
                lanes = slice(g * SGU_GROUP_DIM, (g + 1) * SGU_GROUP_DIM)
                mixed = jnp.dot(ws_bf[g], v[rows, lanes], preferred_element_type=F32) + 0.5 * bs_ref[:, g:g + 1]
                o_ref[out_rows, lanes] = (u[rows, lanes] * mixed).astype(BF16)


def _gmlp_gate(moe, ln_g, ln_b, mod_prev, mod_rows, w_in, b_in, sgu_g, sgu_b, w_s, b_s, seq):
    wts, x1, ys, pos2 = moe
    yg = _gather_rows_of_tokens(ys, pos2)
    t, d = x1.shape
    tm = 512
    steps_per_batch = seq // tm
    n_steps = t // tm
    gw = GMLP_WIDTH
    return pl.pallas_call(
        _gmlp_body,
        grid=(n_steps,),
        in_specs=_combine_specs(tm, n_steps, steps_per_batch) + [
            pl.BlockSpec((None, 1, N_MOD * d), lambda i: (i // steps_per_batch, 0, 0)),
            pl.BlockSpec((d, 2 * gw), lambda i: (0, 0)),
            pl.BlockSpec((1, 2 * gw), lambda i: (0, 0)),
            pl.BlockSpec((1, gw), lambda i: (0, 0)),
            pl.BlockSpec((1, gw), lambda i: (0, 0)),
            pl.BlockSpec((N_SGU_GROUPS, CHUNK, CHUNK), lambda i: (0, 0, 0)),
            pl.BlockSpec((CHUNK, N_SGU_GROUPS), lambda i: (0, 0)),
        ],
        out_specs=[pl.BlockSpec((tm, d), lambda i: (i, 0)), pl.BlockSpec((tm, gw), lambda i: (i, 0))],
        out_shape=[jax.ShapeDtypeStruct((t, d), F32), jax.ShapeDtypeStruct((t, gw), BF16)],
        scratch_shapes=[pltpu.VMEM((N_SGU_GROUPS, CHUNK, CHUNK), BF16), pltpu.VMEM((d, 2 * gw), BF16)],
        compiler_params=_cparams(1),
        name="combine_gmlp_gate",
    )(wts, x1, mod_prev, ln_g.reshape(1, d), ln_b.reshape(1, d), yg, yg,
      mod_rows, w_in, b_in.reshape(1, 2 * gw), sgu_g.reshape(1, gw), sgu_b.reshape(1, gw), w_s, b_s.T)


def _router_params(w_group, b_group, w_expert, b_expert):
    def lanes(group_part, expert_part):
        rows = group_part.shape[0]
        gap = jnp.zeros((rows, EXPERT_COL0 - N_EXPERT_GROUPS), F32)
        tail = jnp.zeros((rows, ROUTER_LANES - EXPERT_COL0 - N_EXPERTS), F32)
        return jnp.concatenate([group_part, gap, expert_part, tail], axis=1)

    return lanes(w_group, w_expert), lanes(b_group[None, :], b_expert[None, :])


def _moe_experts(x1, h2, ids, wts, layer, w_gate_up, w_down):
    t = x1.shape[0]
    n_rows = 2 * t + N_EXPERTS * EXPERT_TILE
    n_tiles = n_rows // EXPERT_TILE
    pos, tile_plan = _expert_sort(ids, n_tiles)
    pos2 = pos.reshape(2, t)
    xs = _sc_scatter_rows(h2, pos2, n_rows)
    ys = _expert_mlp(tile_plan, xs, w_gate_up, w_down, layer)
    return wts, x1, ys, pos2


def kernel(x, c, positions, ada_w, ada_b, post_ln_g, post_ln_b, attn_w_qkv, attn_b_qkv, attn_sinks, attn_w_o, attn_b_o, gmlp_w_in, gmlp_b_in, gmlp_sgu_ln_g, gmlp_sgu_ln_b, gmlp_w_s, gmlp_b_s, gmlp_w_out, gmlp_b_out, moe_w_group_router, moe_b_group_router, moe_w_expert_router, moe_b_expert_router, moe_w_gate_up, moe_w_down):
    batch, seq, d = x.shape
    t = batch * seq
    assert d == D_MODEL and batch <= MOD_ROWS and ada_w.shape[0] == DEPTH == 2
    assert seq % 1024 == 0, "token tiles of 1024 rows must not straddle sequences"
    assert t % (SC_WORKERS * SC_CHUNK) == 0, "every SparseCore subcore moves whole chunks"
    x2 = x.reshape(t, d)
    c_pad = jnp.pad(c, ((0, MOD_ROWS - batch), (0, 0)))
    mods = [_adaln_mod(c_pad, ada_w, ada_b, layer, batch) for layer in range(DEPTH)]

    moe = None
    for layer in range(DEPTH):
        j = layer // 2
        if layer % 2 == 0:
            if moe is not None:
                x2 = _combine_ln(moe, mods[layer - 1], post_ln_g[layer - 1, 1], post_ln_b[layer - 1, 1], seq)
            qkv = _qkv_rope(x2, mods[layer], positions, attn_w_qkv[j], attn_b_qkv[j], seq)
            mix = _attention(qkv, attn_sinks[j], batch, seq)
            w_out, b_out = attn_w_o[j], attn_b_o[j]
        else:
            x2, mix = _gmlp_gate(moe, post_ln_g[layer - 1, 1], post_ln_b[layer - 1, 1], mods[layer - 1], mods[layer],
                                 gmlp_w_in[j], gmlp_b_in[j], gmlp_sgu_ln_g[j], gmlp_sgu_ln_b[j],
                                 gmlp_w_s[j], gmlp_b_s[j], seq)
            w_out, b_out = gmlp_w_out[j], gmlp_b_out[j]
        w_router, b_router = _router_params(moe_w_group_router[layer], moe_b_group_router[layer],
                                            moe_w_expert_router[layer], moe_b_expert_router[layer])
        x1, h2, ids, wts = _proj_ln_router(mix, x2, mods[layer], w_out, b_out, post_ln_g[layer, 0],
                                           post_ln_b[layer, 0], w_router, b_router, seq)
        moe = _moe_experts(x1, h2, ids, wts, layer, moe_w_gate_up, moe_w_down)
    x2 = _combine_ln(moe, mods[DEPTH - 1], post_ln_g[DEPTH - 1, 1], post_ln_b[DEPTH - 1, 1], seq)
    return x2.reshape(batch, seq, d)
```
